```python
import jax
import jax.numpy as jnp
from jax import lax
import numpy as np

D_MODEL = 1024
BATCH = 32
SEQ = 2048
DEPTH = 2

GRID_W = 64
CTX_LEN = 256
N_MIXERS = 2
N_ATTN_LAYERS = (DEPTH + N_MIXERS - 1) // N_MIXERS
N_RET_LAYERS = DEPTH // N_MIXERS

HEAD_DIM = 64
N_HEADS = D_MODEL // HEAD_DIM
N_KV_HEADS = N_HEADS // 4
GQA_GROUP = N_HEADS // N_KV_HEADS
WINDOW = 128
ATTN_BLOCK = 128
ATTN_PROJ = (N_HEADS + 2 * N_KV_HEADS) * HEAD_DIM

RET_HEADS = D_MODEL // 256
RET_QK_DIM = D_MODEL // RET_HEADS
RET_V_DIM = 2 * D_MODEL // RET_HEADS
RET_VWIDTH = 2 * D_MODEL
RET_CHUNK = 128
RET_PROJ = 2 * D_MODEL + 2 * RET_VWIDTH

D_FF = -(-8 * D_MODEL // (3 * 256)) * 256

ROPE_BASE = 10000.0
EPS = 1e-6
NEG_INF = -1e30

kernel_name = 'hybrid_swa_sink_retention_dit'


def rms_norm(x, g):
    xf = x.astype(jnp.float32)
    y = xf * lax.rsqrt(jnp.mean(xf * xf, axis=-1, keepdims=True) + EPS)
    return (y * g.astype(jnp.float32)).astype(x.dtype)


def modulate(h, shift, scale):
    return h * (1 + scale) + shift


def grid_positions(n):
    rows = n // GRID_W
    row = jnp.broadcast_to(jnp.arange(rows, dtype=jnp.int32)[:, None], (rows, GRID_W)).reshape(n)
    col = jnp.broadcast_to(jnp.arange(GRID_W, dtype=jnp.int32)[None, :], (rows, GRID_W)).reshape(n)
    return row, col


def rope_tables(n, head_dim):
    row, col = grid_positions(n)
    axis_dim = head_dim // 2
    inv = ROPE_BASE ** (-jnp.arange(0, axis_dim, 2, dtype=jnp.float32) / axis_dim)
    ang_r = row.astype(jnp.float32)[:, None] * inv
    ang_c = col.astype(jnp.float32)[:, None] * inv
    return jnp.cos(ang_r), jnp.sin(ang_r), jnp.cos(ang_c), jnp.sin(ang_c)


def rotate_axis(x, cos, sin):
    x1, x2 = jnp.split(x, 2, axis=-1)
    cos = cos[:, None, :]
    sin = sin[:, None, :]
    return jnp.concatenate([x1 * cos - x2 * sin, x1 * sin + x2 * cos], axis=-1)


def rope_2d(x, tables):
    cos_r, sin_r, cos_c, sin_c = tables
    xr, xc = jnp.split(x.astype(jnp.float32), 2, axis=-1)
    out = jnp.concatenate([rotate_axis(xr, cos_r, sin_r), rotate_axis(xc, cos_c, sin_c)], axis=-1)
    return out.astype(x.dtype)


def swiglu(h, w_in, w_out):
    gate, up = jnp.split(h @ w_in, 2, axis=-1)
    return (jax.nn.silu(gate) * up) @ w_out


def windowed_gqa_sink(h_x, h_c, w_qkv, q_gain, k_gain, sink, w_o, need_ctx_out):
    B, S, _ = h_x.shape
    L = h_c.shape[1]
    nb = S // ATTN_BLOCK
    band = ATTN_BLOCK + 2 * WINDOW
    scale = HEAD_DIM ** -0.5
    qd = N_HEADS * HEAD_DIM
    kvd = N_KV_HEADS * HEAD_DIM
    tables = rope_tables(S, HEAD_DIM)
    sink_g = sink.astype(jnp.float32).reshape(N_KV_HEADS, GQA_GROUP)[None, :, :, None, None]

    q_x, k_x, v_x = jnp.split(h_x @ w_qkv, [qd, qd + kvd], axis=-1)
    q_x = rope_2d(rms_norm(q_x.reshape(B, S, N_HEADS, HEAD_DIM), q_gain), tables)
    q_x = q_x.reshape(B, S, N_KV_HEADS, GQA_GROUP, HEAD_DIM)
    k_x = rope_2d(rms_norm(k_x.reshape(B, S, N_KV_HEADS, HEAD_DIM), k_gain), tables)
    v_x = v_x.reshape(B, S, N_KV_HEADS, HEAD_DIM)
    k_c, v_c = jnp.split(h_c @ w_qkv[:, qd:], 2, axis=-1)
    k_c = rms_norm(k_c.reshape(B, L, N_KV_HEADS, HEAD_DIM), k_gain)
    v_c = v_c.reshape(B, L, N_KV_HEADS, HEAD_DIM)

    pad = ((0, 0), (WINDOW, WINDOW), (0, 0), (0, 0))
    k_pad = jnp.pad(k_x, pad)
    v_pad = jnp.pad(v_x, pad)
    r_idx = jnp.arange(ATTN_BLOCK, dtype=jnp.int32)[:, None]
    n_idx = jnp.arange(band, dtype=jnp.int32)[None, :]
    in_window = (n_idx >= r_idx) & (n_idx - r_idx <= 2 * WINDOW)

    def block(b):
        start = b * ATTN_BLOCK
        qb = lax.dynamic_slice_in_dim(q_x, start, ATTN_BLOCK, axis=1)
        kb = lax.dynamic_slice_in_dim(k_pad, start, band, axis=1)
        vb = lax.dynamic_slice_in_dim(v_pad, start, band, axis=1)
        key_pos = start - WINDOW + n_idx
        valid = in_window & (key_pos >= 0) & (key_pos < S)
        s_ctx = jnp.einsum('bqkgd,bnkd->bkgqn', qb, k_c, preferred_element_type=jnp.float32) * scale
        s_loc = jnp.einsum('bqkgd,bnkd->bkgqn', qb, kb, preferred_element_type=jnp.float32) * scale
        s_loc = jnp.where(valid, s_loc, NEG_INF)
        sink_col = jnp.broadcast_to(sink_g, s_ctx.shape[:-1] + (1,))
        p = jax.nn.softmax(jnp.concatenate([s_ctx, s_loc, sink_col], axis=-1), axis=-1).astype(vb.dtype)
        return (jnp.einsum('bkgqn,bnkd->bqkgd', p[..., :L], v_c)
                + jnp.einsum('bkgqn,bnkd->bqkgd', p[..., L:L + band], vb))

    o_x = jnp.moveaxis(lax.map(block, jnp.arange(nb, dtype=jnp.int32)), 0, 1).reshape(B, S, qd)
    out_x = o_x @ w_o

    out_c = None
    if need_ctx_out:
        q_c = rms_norm((h_c @ w_qkv[:, :qd]).reshape(B, L, N_HEADS, HEAD_DIM), q_gain)
        q_c = q_c.reshape(B, L, N_KV_HEADS, GQA_GROUP, HEAD_DIM)
        s_c = jnp.einsum('bqkgd,bnkd->bkgqn', q_c, k_c, preferred_element_type=jnp.float32) * scale
        sink_col = jnp.broadcast_to(sink_g, s_c.shape[:-1] + (1,))
        p_c = jax.nn.softmax(jnp.concatenate([s_c, sink_col], axis=-1), axis=-1).astype(v_c.dtype)
        o_c = jnp.einsum('bkgqn,bnkd->bqkgd', p_c[..., :L], v_c)
        out_c = o_c.reshape(B, L, qd) @ w_o
    return out_c, out_x


def retention_chunked(q, k, v, log_gamma, state0):
    B, T, H, _ = q.shape
    nc = T // RET_CHUNK
    pos = jnp.arange(RET_CHUNK, dtype=jnp.float32)
    diff = pos[:, None] - pos[None, :]
    intra = jnp.where(diff >= 0, jnp.exp(log_gamma[:, None, None] * jnp.maximum(diff, 0.0)), 0.0)
    q_decay = jnp.exp(log_gamma[None, :] * (pos + 1.0)[:, None])[None, :, :, None]
    k_decay = jnp.exp(log_gamma[None, :] * (RET_CHUNK - 1.0 - pos)[:, None])[None, :, :, None]
    chunk_decay = jnp.exp(log_gamma * RET_CHUNK)[None, :, None, None]

    def to_chunks(a):
        return jnp.moveaxis(a.reshape(B, nc, RET_CHUNK, H, a.shape[-1]), 1, 0)

    def step(state, inp):
        qc, kc, vc = inp
        scores = jnp.einsum('bnhd,bmhd->bhnm', qc, kc) * intra
        inner = jnp.einsum('bhnm,bmhe->bnhe', scores, vc)
        cross = jnp.einsum('bnhd,bhde->bnhe', qc, state) * q_decay
        new_state = state * chunk_decay + jnp.einsum('bmhd,bmhe->bhde', kc * k_decay, vc)
        return new_state, inner + cross

    state, out = lax.scan(step, state0, (to_chunks(q), to_chunks(k), to_chunks(v)))
    return jnp.moveaxis(out, 0, 1).reshape(B, T, H, v.shape[-1]), state


def retention_final_state(k, v, log_gamma):
    T = k.shape[1]
    pos = jnp.arange(T, dtype=jnp.float32)
    decay = jnp.exp(log_gamma[None, :] * (T - 1.0 - pos)[:, None])[None, :, :, None]
    return jnp.einsum('bthd,bthe->bhde', k * decay, v)


def bidir_retention(h_x, h_c, w_qkvg, decay_logit, gn_gain, w_o, need_ctx_out):
    B, S, _ = h_x.shape
    L = h_c.shape[1]
    qk = RET_HEADS * RET_QK_DIM
    f32 = jnp.float32
    tables = rope_tables(S, RET_QK_DIM)
    log_g = jax.nn.log_sigmoid(decay_logit.astype(f32))
    k_scale = RET_QK_DIM ** -0.5

    q_x, k_x, v_x, g_x = jnp.split(h_x @ w_qkvg, [qk, 2 * qk, 2 * qk + RET_VWIDTH], axis=-1)
    q_x = rope_2d(q_x.reshape(B, S, RET_HEADS, RET_QK_DIM), tables).astype(f32)
    k_x = rope_2d(k_x.reshape(B, S, RET_HEADS, RET_QK_DIM), tables).astype(f32) * k_scale
    v_x = v_x.reshape(B, S, RET_HEADS, RET_V_DIM).astype(f32)
    k_c, v_c = jnp.split(h_c @ w_qkvg[:, qk:2 * qk + RET_VWIDTH], [qk], axis=-1)
    k_c = k_c.reshape(B, L, RET_HEADS, RET_QK_DIM).astype(f32) * k_scale
    v_c = v_c.reshape(B, L, RET_HEADS, RET_V_DIM).astype(f32)

    def flip(a):
        return jnp.flip(a, axis=1)

    def gated_out(o, g):
        mu = jnp.mean(o, axis=-1, keepdims=True)
        var = jnp.mean(jnp.square(o - mu), axis=-1, keepdims=True)
        y = ((o - mu) * lax.rsqrt(var + EPS)).reshape(o.shape[0], o.shape[1], RET_VWIDTH) * gn_gain.astype(f32)
        return (jax.nn.silu(g) * y.astype(g.dtype)) @ w_o

    out_c = None
    if need_ctx_out:
        q_c = (h_c @ w_qkvg[:, :qk]).reshape(B, L, RET_HEADS, RET_QK_DIM).astype(f32)
        g_c = h_c @ w_qkvg[:, 2 * qk + RET_VWIDTH:]
        zero_state = jnp.zeros((B, RET_HEADS, RET_QK_DIM, RET_V_DIM), f32)
        oc_f, state_f = retention_chunked(q_c, k_c, v_c, log_g[0], zero_state)
        oc_b, state_b = retention_chunked(flip(q_c), flip(k_c), flip(v_c), log_g[1], zero_state)
        out_c = gated_out(oc_f + flip(oc_b), g_c)
    else:
        state_f = retention_final_state(k_c, v_c, log_g[0])
        state_b = retention_final_state(flip(k_c), flip(v_c), log_g[1])

    ox_f, _ = retention_chunked(q_x, k_x, v_x, log_g[0], state_f)
    ox_b, _ = retention_chunked(flip(q_x), flip(k_x), flip(v_x), log_g[1], state_b)
    out_x = gated_out(ox_f + flip(ox_b), g_x)
    return out_c, out_x


def _fwd_setup_inputs(seed: int = 0) -> dict:
    key = jax.random.key(seed)
    ks = jax.random.split(key, 19)
    f32 = jnp.float32

    def nrm(k, shape, scale):
        return jax.random.normal(k, shape, f32) * scale

    gamma = 1.0 - 2.0 ** (-5.0 - np.arange(RET_HEADS))
    decay_init = jnp.asarray(np.log(gamma / (1.0 - gamma)), dtype=f32)
    return {
        'x': nrm(ks[0], (BATCH, SEQ, D_MODEL), 1.0),
        'c': nrm(ks[1], (BATCH, D_MODEL), 1.0),
        'ctx': nrm(ks[2], (BATCH, CTX_LEN, D_MODEL), 1.0),
        'c_ctx': nrm(ks[3], (D_MODEL,), 1.0),
        'ada_w': nrm(ks[4], (DEPTH, D_MODEL, 6 * D_MODEL), 0.5 * D_MODEL ** -0.5),
        'ada_b': nrm(ks[5], (DEPTH, 6 * D_MODEL), 0.02),
        'norm1_g': 1.0 + nrm(ks[6], (DEPTH, D_MODEL), 0.02),
        'norm2_g': 1.0 + nrm(ks[7], (DEPTH, D_MODEL), 0.02),
        'ffn_w_in': nrm(ks[8], (DEPTH, D_MODEL, 2 * D_FF), D_MODEL ** -0.5),
        'ffn_w_out': nrm(ks[9], (DEPTH, D_FF, D_MODEL), D_FF ** -0.5),
        'attn_w_qkv': nrm(ks[10], (N_ATTN_LAYERS, D_MODEL, ATTN_PROJ), D_MODEL ** -0.5),
        'attn_q_norm': 1.0 + nrm(ks[11], (N_ATTN_LAYERS, HEAD_DIM), 0.02),
        'attn_k_norm': 1.0 + nrm(ks[12], (N_ATTN_LAYERS, HEAD_DIM), 0.02),
        'attn_sink': nrm(ks[13], (N_ATTN_LAYERS, N_HEADS), 0.5),
        'attn_w_o': nrm(ks[14], (N_ATTN_LAYERS, N_HEADS * HEAD_DIM, D_MODEL), (N_HEADS * HEAD_DIM) ** -0.5),
        'ret_w_qkvg': nrm(ks[15], (N_RET_LAYERS, D_MODEL, RET_PROJ), D_MODEL ** -0.5),
        'ret_decay_logit': decay_init[None, None, :] + nrm(ks[16], (N_RET_LAYERS, 2, RET_HEADS), 0.01),
        'ret_gn_g': 1.0 + nrm(ks[17], (N_RET_LAYERS, RET_VWIDTH), 0.02),
        'ret_w_o': nrm(ks[18], (N_RET_LAYERS, RET_VWIDTH, D_MODEL), RET_VWIDTH ** -0.5),
    }


def _fwd_reference(x, c, ctx, c_ctx, ada_w, ada_b, norm1_g, norm2_g, ffn_w_in, ffn_w_out,
              attn_w_qkv, attn_q_norm, attn_k_norm, attn_sink, attn_w_o,
              ret_w_qkvg, ret_decay_logit, ret_gn_g, ret_w_o):
    c_act = jax.nn.silu(c)[:, None, :]
    cc_act = jax.nn.silu(c_ctx)[None, None, :]
    y_ctx = ctx
    for i in range(DEPTH):
        need_ctx_out = i < DEPTH - 1
        mx = jnp.split(c_act @ ada_w[i] + ada_b[i], 6, axis=-1)
        mc = jnp.split(cc_act @ ada_w[i] + ada_b[i], 6, axis=-1)
        h_x = modulate(rms_norm(x, norm1_g[i]), mx[0], mx[1])
        h_c = modulate(rms_norm(y_ctx, norm1_g[i]), mc[0], mc[1])
        j = i // N_MIXERS
        if i % N_MIXERS == 0:
            out_c, out_x = windowed_gqa_sink(h_x, h_c, attn_w_qkv[j], attn_q_norm[j], attn_k_norm[j],
                                             attn_sink[j], attn_w_o[j], need_ctx_out)
        else:
            out_c, out_x = bidir_retention(h_x, h_c, ret_w_qkvg[j], ret_decay_logit[j], ret_gn_g[j],
                                           ret_w_o[j], need_ctx_out)
        x = x + mx[2] * out_x
        x = x + mx[5] * swiglu(modulate(rms_norm(x, norm2_g[i]), mx[3], mx[4]), ffn_w_in[i], ffn_w_out[i])
        if need_ctx_out:
            y_ctx = y_ctx + mc[2] * out_c
            y_ctx = y_ctx + mc[5] * swiglu(modulate(rms_norm(y_ctx, norm2_g[i]), mc[3], mc[4]),
                                           ffn_w_in[i], ffn_w_out[i])
    return x


import jax as _jax
import jax.numpy as _jnp

TWIN_FORMAT = 'train_step'
FWD_PARAMS = ['x', 'c', 'ctx', 'c_ctx', 'ada_w', 'ada_b', 'norm1_g', 'norm2_g', 'ffn_w_in', 'ffn_w_out', 'attn_w_qkv', 'attn_q_norm', 'attn_k_norm', 'attn_sink', 'attn_w_o', 'ret_w_qkvg', 'ret_decay_logit', 'ret_gn_g', 'ret_w_o']
TWIN_WEIGHTS = ['c_ctx', 'ada_w', 'ada_b', 'norm1_g', 'norm2_g', 'ffn_w_in', 'ffn_w_out', 'attn_w_qkv', 'attn_q_norm', 'attn_k_norm', 'attn_sink', 'attn_w_o', 'ret_w_qkvg', 'ret_decay_logit', 'ret_gn_g', 'ret_w_o']
TWIN_DIFF_INPUT = 'x'
TWIN_INPUTS = ['x', 'c', 'ctx', 'c_ctx', 'ada_w', 'ada_b', 'norm1_g', 'norm2_g', 'ffn_w_in', 'ffn_w_out', 'attn_w_qkv', 'attn_q_norm', 'attn_k_norm', 'attn_sink', 'attn_w_o', 'ret_w_qkvg', 'ret_decay_logit', 'ret_gn_g', 'ret_w_o', 'loss_target', 'm_c_ctx', 'm_ada_w', 'm_ada_b', 'm_norm1_g', 'm_norm2_g', 'm_ffn_w_in', 'm_ffn_w_out', 'm_attn_w_qkv', 'm_attn_q_norm', 'm_attn_k_norm', 'm_attn_sink', 'm_attn_w_o', 'm_ret_w_qkvg', 'm_ret_decay_logit', 'm_ret_gn_g', 'm_ret_w_o', 'v_c_ctx', 'v_ada_w', 'v_ada_b', 'v_norm1_g', 'v_norm2_g', 'v_ffn_w_in', 'v_ffn_w_out', 'v_attn_w_qkv', 'v_attn_q_norm', 'v_attn_k_norm', 'v_attn_sink', 'v_attn_w_o', 'v_ret_w_qkvg', 'v_ret_decay_logit', 'v_ret_gn_g', 'v_ret_w_o']
TWIN_OUTPUTS = ['loss', 'grad_x', 'grad_c_ctx', 'grad_ada_w', 'grad_ada_b', 'grad_norm1_g', 'grad_norm2_g', 'grad_ffn_w_in', 'grad_ffn_w_out', 'grad_attn_w_qkv', 'grad_attn_q_norm', 'grad_attn_k_norm', 'grad_attn_sink', 'grad_attn_w_o', 'grad_ret_w_qkvg', 'grad_ret_decay_logit', 'grad_ret_gn_g', 'grad_ret_w_o', 'delta_c_ctx', 'delta_ada_w', 'delta_ada_b', 'delta_norm1_g', 'delta_norm2_g', 'delta_ffn_w_in', 'delta_ffn_w_out', 'delta_attn_w_qkv', 'delta_attn_q_norm', 'delta_attn_k_norm', 'delta_attn_sink', 'delta_attn_w_o', 'delta_ret_w_qkvg', 'delta_ret_decay_logit', 'delta_ret_gn_g', 'delta_ret_w_o', 'new_m_c_ctx', 'new_m_ada_w', 'new_m_ada_b', 'new_m_norm1_g', 'new_m_norm2_g', 'new_m_ffn_w_in', 'new_m_ffn_w_out', 'new_m_attn_w_qkv', 'new_m_attn_q_norm', 'new_m_attn_k_norm', 'new_m_attn_sink', 'new_m_attn_w_o', 'new_m_ret_w_qkvg', 'new_m_ret_decay_logit', 'new_m_ret_gn_g', 'new_m_ret_w_o', 'new_v_c_ctx', 'new_v_ada_w', 'new_v_ada_b', 'new_v_norm1_g', 'new_v_norm2_g', 'new_v_ffn_w_in', 'new_v_ffn_w_out', 'new_v_attn_w_qkv', 'new_v_attn_q_norm', 'new_v_attn_k_norm', 'new_v_attn_sink', 'new_v_attn_w_o', 'new_v_ret_w_qkvg', 'new_v_ret_decay_logit', 'new_v_ret_gn_g', 'new_v_ret_w_o']
TWIN_LEAF_KINDS = {'loss': 'loss', 'grad_x': 'grad_x', 'grad_c_ctx': 'grad_w', 'grad_ada_w': 'grad_w', 'grad_ada_b': 'grad_w', 'grad_norm1_g': 'grad_w', 'grad_norm2_g': 'grad_w', 'grad_ffn_w_in': 'grad_w', 'grad_ffn_w_out': 'grad_w', 'grad_attn_w_qkv': 'grad_w', 'grad_attn_q_norm': 'grad_w', 'grad_attn_k_norm': 'grad_w', 'grad_attn_sink': 'grad_w', 'grad_attn_w_o': 'grad_w', 'grad_ret_w_qkvg': 'grad_w', 'grad_ret_decay_logit': 'grad_w', 'grad_ret_gn_g': 'grad_w', 'grad_ret_w_o': 'grad_w', 'delta_c_ctx': 'delta_w', 'delta_ada_w': 'delta_w', 'delta_ada_b': 'delta_w', 'delta_norm1_g': 'delta_w', 'delta_norm2_g': 'delta_w', 'delta_ffn_w_in': 'delta_w', 'delta_ffn_w_out': 'delta_w', 'delta_attn_w_qkv': 'delta_w', 'delta_attn_q_norm': 'delta_w', 'delta_attn_k_norm': 'delta_w', 'delta_attn_sink': 'delta_w', 'delta_attn_w_o': 'delta_w', 'delta_ret_w_qkvg': 'delta_w', 'delta_ret_decay_logit': 'delta_w', 'delta_ret_gn_g': 'delta_w', 'delta_ret_w_o': 'delta_w', 'new_m_c_ctx': 'new_m', 'new_m_ada_w': 'new_m', 'new_m_ada_b': 'new_m', 'new_m_norm1_g': 'new_m', 'new_m_norm2_g': 'new_m', 'new_m_ffn_w_in': 'new_m', 'new_m_ffn_w_out': 'new_m', 'new_m_attn_w_qkv': 'new_m', 'new_m_attn_q_norm': 'new_m', 'new_m_attn_k_norm': 'new_m', 'new_m_attn_sink': 'new_m', 'new_m_attn_w_o': 'new_m', 'new_m_ret_w_qkvg': 'new_m', 'new_m_ret_decay_logit': 'new_m', 'new_m_ret_gn_g': 'new_m', 'new_m_ret_w_o': 'new_m', 'new_v_c_ctx': 'new_v', 'new_v_ada_w': 'new_v', 'new_v_ada_b': 'new_v', 'new_v_norm1_g': 'new_v', 'new_v_norm2_g': 'new_v', 'new_v_ffn_w_in': 'new_v', 'new_v_ffn_w_out': 'new_v', 'new_v_attn_w_qkv': 'new_v', 'new_v_attn_q_norm': 'new_v', 'new_v_attn_k_norm': 'new_v', 'new_v_attn_sink': 'new_v', 'new_v_attn_w_o': 'new_v', 'new_v_ret_w_qkvg': 'new_v', 'new_v_ret_decay_logit': 'new_v', 'new_v_ret_gn_g': 'new_v', 'new_v_ret_w_o': 'new_v'}


def _forward(args):
    return _fwd_reference(*[args[k] for k in FWD_PARAMS])


def _output_shape():
    out = _jax.eval_shape(lambda: _forward(_fwd_setup_inputs(0)))
    return out.shape, out.dtype

N_MICROBATCH = 1
ADAM_LR = 0.001
ADAM_B1 = 0.9
ADAM_B2 = 0.999
ADAM_EPS = 1e-08
ADAM_WD = 0.01
ADAM_STEP = 10
PER_EXAMPLE_BATCH_AXIS = {'x': 0, 'c': 0, 'ctx': 0, 'loss_target': 0}
SHARED_INPUTS = []
_WEIGHT_DTYPES = {'c_ctx': _jnp.float32, 'ada_w': _jnp.float32, 'ada_b': _jnp.float32, 'norm1_g': _jnp.float32, 'norm2_g': _jnp.float32, 'ffn_w_in': _jnp.float32, 'ffn_w_out': _jnp.float32, 'attn_w_qkv': _jnp.float32, 'attn_q_norm': _jnp.float32, 'attn_k_norm': _jnp.float32, 'attn_sink': _jnp.float32, 'attn_w_o': _jnp.float32, 'ret_w_qkvg': _jnp.float32, 'ret_decay_logit': _jnp.float32, 'ret_gn_g': _jnp.float32, 'ret_w_o': _jnp.float32}
MOMENT_SCALE = {'c_ctx': 3.966430e-01, 'ada_w': 1.744475e+00, 'ada_b': 3.840612e+00, 'norm1_g': 1.891511e+00, 'norm2_g': 6.836904e+00, 'ffn_w_in': 9.447452e-02, 'ffn_w_out': 1.193380e-01, 'attn_w_qkv': 4.053822e-01, 'attn_q_norm': 5.501094e-01, 'attn_k_norm': 5.659409e-01, 'attn_sink': 3.554206e-02, 'attn_w_o': 2.961446e-01, 'ret_w_qkvg': 1.086084e-01, 'ret_decay_logit': 4.460498e-01, 'ret_gn_g': 1.386766e+00, 'ret_w_o': 1.029975e-01}


def _to_microbatches(a, axis):
    t = _jnp.moveaxis(a, axis, 0)
    t = t.reshape((N_MICROBATCH, t.shape[0] // N_MICROBATCH) + t.shape[1:])
    return _jnp.moveaxis(t, 1, axis + 1)


def setup_inputs(seed: int = 0) -> dict:
    inp = _fwd_setup_inputs(seed)
    key = _jax.random.fold_in(_jax.random.key(seed), 7919)
    shape, _ = _output_shape()
    out = dict(inp)
    out["loss_target"] = _jax.random.normal(_jax.random.fold_in(key, 0), shape, _jnp.float32)
    for i, name in enumerate(TWIN_WEIGHTS):
        w = inp[name].astype(_jnp.float32)
        if MOMENT_SCALE is None:
            s = _jnp.sqrt(_jnp.mean(_jnp.square(w)) + 1e-30)
        else:
            s = MOMENT_SCALE[name]
        km, kv = _jax.random.split(_jax.random.fold_in(key, i + 1))
        out[name] = w
        out["m_" + name] = s * _jax.random.normal(km, w.shape, _jnp.float32)
        out["v_" + name] = (s * s) * _jax.random.uniform(kv, w.shape, _jnp.float32, 0.5, 1.5)
    if N_MICROBATCH > 1:
        for name, axis in PER_EXAMPLE_BATCH_AXIS.items():
            out[name] = _to_microbatches(out[name], axis)
    return {'x': out['x'], 'c': out['c'], 'ctx': out['ctx'], 'c_ctx': out['c_ctx'], 'ada_w': out['ada_w'], 'ada_b': out['ada_b'], 'norm1_g': out['norm1_g'], 'norm2_g': out['norm2_g'], 'ffn_w_in': out['ffn_w_in'], 'ffn_w_out': out['ffn_w_out'], 'attn_w_qkv': out['attn_w_qkv'], 'attn_q_norm': out['attn_q_norm'], 'attn_k_norm': out['attn_k_norm'], 'attn_sink': out['attn_sink'], 'attn_w_o': out['attn_w_o'], 'ret_w_qkvg': out['ret_w_qkvg'], 'ret_decay_logit': out['ret_decay_logit'], 'ret_gn_g': out['ret_gn_g'], 'ret_w_o': out['ret_w_o'], 'loss_target': out['loss_target'], 'm_c_ctx': out['m_c_ctx'], 'm_ada_w': out['m_ada_w'], 'm_ada_b': out['m_ada_b'], 'm_norm1_g': out['m_norm1_g'], 'm_norm2_g': out['m_norm2_g'], 'm_ffn_w_in': out['m_ffn_w_in'], 'm_ffn_w_out': out['m_ffn_w_out'], 'm_attn_w_qkv': out['m_attn_w_qkv'], 'm_attn_q_norm': out['m_attn_q_norm'], 'm_attn_k_norm': out['m_attn_k_norm'], 'm_attn_sink': out['m_attn_sink'], 'm_attn_w_o': out['m_attn_w_o'], 'm_ret_w_qkvg': out['m_ret_w_qkvg'], 'm_ret_decay_logit': out['m_ret_decay_logit'], 'm_ret_gn_g': out['m_ret_gn_g'], 'm_ret_w_o': out['m_ret_w_o'], 'v_c_ctx': out['v_c_ctx'], 'v_ada_w': out['v_ada_w'], 'v_ada_b': out['v_ada_b'], 'v_norm1_g': out['v_norm1_g'], 'v_norm2_g': out['v_norm2_g'], 'v_ffn_w_in': out['v_ffn_w_in'], 'v_ffn_w_out': out['v_ffn_w_out'], 'v_attn_w_qkv': out['v_attn_w_qkv'], 'v_attn_q_norm': out['v_attn_q_norm'], 'v_attn_k_norm': out['v_attn_k_norm'], 'v_attn_sink': out['v_attn_sink'], 'v_attn_w_o': out['v_attn_w_o'], 'v_ret_w_qkvg': out['v_ret_w_qkvg'], 'v_ret_decay_logit': out['v_ret_decay_logit'], 'v_ret_gn_g': out['v_ret_gn_g'], 'v_ret_w_o': out['v_ret_w_o']}


def _loss(weights, diff, rest, loss_target):
    with _jax.named_scope("forward"):
        args = {**rest, TWIN_DIFF_INPUT: diff, **{k: w.astype(_WEIGHT_DTYPES[k]) for k, w in weights.items()}}
        y = _forward(args)
    with _jax.named_scope("loss_head"):
        err = _jnp.square(y.astype(_jnp.float32) - loss_target)
        return 0.5 * _jnp.sum(_jnp.mean(err, axis=-1)) if err.ndim else 0.5 * err


def _adamw(w, g, m, v):
    m = ADAM_B1 * m + (1.0 - ADAM_B1) * g
    v = ADAM_B2 * v + (1.0 - ADAM_B2) * _jnp.square(g)
    m_hat = m / (1.0 - ADAM_B1 ** ADAM_STEP)
    v_hat = v / (1.0 - ADAM_B2 ** ADAM_STEP)
    delta = -ADAM_LR * (m_hat / (_jnp.sqrt(v_hat) + ADAM_EPS) + ADAM_WD * w)
    return delta, m, v


def reference(x, c, ctx, c_ctx, ada_w, ada_b, norm1_g, norm2_g, ffn_w_in, ffn_w_out, attn_w_qkv, attn_q_norm, attn_k_norm, attn_sink, attn_w_o, ret_w_qkvg, ret_decay_logit, ret_gn_g, ret_w_o, loss_target, m_c_ctx, m_ada_w, m_ada_b, m_norm1_g, m_norm2_g, m_ffn_w_in, m_ffn_w_out, m_attn_w_qkv, m_attn_q_norm, m_attn_k_norm, m_attn_sink, m_attn_w_o, m_ret_w_qkvg, m_ret_decay_logit, m_ret_gn_g, m_ret_w_o, v_c_ctx, v_ada_w, v_ada_b, v_norm1_g, v_norm2_g, v_ffn_w_in, v_ffn_w_out, v_attn_w_qkv, v_attn_q_norm, v_attn_k_norm, v_attn_sink, v_attn_w_o, v_ret_w_qkvg, v_ret_decay_logit, v_ret_gn_g, v_ret_w_o):
    given = dict(x=x, c=c, ctx=ctx, c_ctx=c_ctx, ada_w=ada_w, ada_b=ada_b, norm1_g=norm1_g, norm2_g=norm2_g, ffn_w_in=ffn_w_in, ffn_w_out=ffn_w_out, attn_w_qkv=attn_w_qkv, attn_q_norm=attn_q_norm, attn_k_norm=attn_k_norm, attn_sink=attn_sink, attn_w_o=attn_w_o, ret_w_qkvg=ret_w_qkvg, ret_decay_logit=ret_decay_logit, ret_gn_g=ret_gn_g, ret_w_o=ret_w_o, loss_target=loss_target, m_c_ctx=m_c_ctx, m_ada_w=m_ada_w, m_ada_b=m_ada_b, m_norm1_g=m_norm1_g, m_norm2_g=m_norm2_g, m_ffn_w_in=m_ffn_w_in, m_ffn_w_out=m_ffn_w_out, m_attn_w_qkv=m_attn_w_qkv, m_attn_q_norm=m_attn_q_norm, m_attn_k_norm=m_attn_k_norm, m_attn_sink=m_attn_sink, m_attn_w_o=m_attn_w_o, m_ret_w_qkvg=m_ret_w_qkvg, m_ret_decay_logit=m_ret_decay_logit, m_ret_gn_g=m_ret_gn_g, m_ret_w_o=m_ret_w_o, v_c_ctx=v_c_ctx, v_ada_w=v_ada_w, v_ada_b=v_ada_b, v_norm1_g=v_norm1_g, v_norm2_g=v_norm2_g, v_ffn_w_in=v_ffn_w_in, v_ffn_w_out=v_ffn_w_out, v_attn_w_qkv=v_attn_w_qkv, v_attn_q_norm=v_attn_q_norm, v_attn_k_norm=v_attn_k_norm, v_attn_sink=v_attn_sink, v_attn_w_o=v_attn_w_o, v_ret_w_qkvg=v_ret_w_qkvg, v_ret_decay_logit=v_ret_decay_logit, v_ret_gn_g=v_ret_gn_g, v_ret_w_o=v_ret_w_o)
    weights = {n: given[n] for n in TWIN_WEIGHTS}
    shared = {n: given[n] for n in SHARED_INPUTS}
    per_example = {n: given[n] for n in ['x', 'c', 'ctx']}
    grad_fn = _jax.value_and_grad(_loss, argnums=(0, 1))

    def one_microbatch(ex, loss_target):
        ex = dict(ex)
        diff = ex.pop(TWIN_DIFF_INPUT)
        return grad_fn(weights, diff, {**shared, **ex}, loss_target)

    if N_MICROBATCH == 1:
        loss, (grad_w, grad_x) = one_microbatch(per_example, given["loss_target"])
    else:
        def body(carry, xs):
            loss_sum, grad_sum = carry
            l_k, (gw_k, gx_k) = one_microbatch(xs[0], xs[1])
            with _jax.named_scope("update"):
                return (loss_sum + l_k, _jax.tree.map(_jnp.add, grad_sum, gw_k)), gx_k

        init = (_jnp.zeros((), _jnp.float32), _jax.tree.map(_jnp.zeros_like, weights))
        (loss, grad_w), grad_x = _jax.lax.scan(body, init, (per_example, given["loss_target"]))
    with _jax.named_scope("update"):
        delta_w, new_m, new_v = {}, {}, {}
        for n in TWIN_WEIGHTS:
            delta_w[n], new_m[n], new_v[n] = _adamw(weights[n], grad_w[n], given["m_" + n], given["v_" + n])
    return (loss, grad_x, *[grad_w[n] for n in TWIN_WEIGHTS], *[delta_w[n] for n in TWIN_WEIGHTS],
            *[new_m[n] for n in TWIN_WEIGHTS], *[new_v[n] for n in TWIN_WEIGHTS])
```

```python
import functools
import math

import jax
import jax.numpy as jnp
from jax import lax
from jax.experimental import pallas as pl
from jax.experimental.pallas import tpu as pltpu

F32 = jnp.float32
MXU_DTYPE = jnp.bfloat16

D_MODEL = 1024
HEAD_DIM = 64
N_HEADS = 16
N_KV_HEADS = 4
GQA_GROUP = 4
WINDOW = 128
ATTN_BLOCK = 128
RET_HEADS = 4
RET_QK_DIM = 256
RET_V_DIM = 512
RET_VWIDTH = 2048
RET_CHUNK = 128
D_FF = 2816
GRID_W = 64
ROPE_BASE = 10000.0
EPS = 1e-6
NEG_INF = -1e30

ADAM_LR = 0.001
ADAM_B1 = 0.9
ADAM_B2 = 0.999
ADAM_EPS = 1e-08
ADAM_WD = 0.01
ADAM_STEP = 10

N_DEV = 8
LANES = 128
ROW_TILE = 512
VMEM_LIMIT = 48 * 1024 * 1024

MESH = pl.DeviceIdType.MESH
_ANY = pl.BlockSpec(memory_space=pl.ANY)
_SMEM = pl.BlockSpec(memory_space=pltpu.SMEM)


def _params(**kw):
    return pltpu.CompilerParams(vmem_limit_bytes=VMEM_LIMIT, **kw)


def _mx(v):
    return v.astype(MXU_DTYPE)


def _dot(a, b, dims):
    return lax.dot_general(_mx(a), _mx(b), (dims, ((), ())), preferred_element_type=F32)


_NN = ((1,), (0,))
_NT = ((1,), (1,))
_TN = ((0,), (0,))


def _tile(n, cands):
    for c in cands:
        if n % c == 0:
            return c
    return n


def _k_tile(k):
    if k <= 2048:
        return k
    for t in range(2048, 0, -LANES):
        if k % t == 0:
            return t
    return k


def _mm(a, b, mode, out_dtype, name, *, bias=None, res=None, gate=None, gidx=None):
    if mode == "nn":
        (M, K), (_, N) = a.shape, b.shape
    elif mode == "nt":
        (M, K), (N, _) = a.shape, b.shape
    else:
        (K, M), (_, N) = a.shape, b.shape
    if res is not None:
        tm = ROW_TILE
    else:
        tm = M if M <= 1024 else _tile(M, (512, 256, 128))
    tn = N if N <= 512 else _tile(N, (512, 256, 128))
    tk = _k_tile(K)
    nk = K // tk
    dims = {"nn": _NN, "nt": _NT, "tn": _TN}[mode]
    a_spec = (pl.BlockSpec((tk, tm), lambda i, j, k: (k, i)) if mode == "tn"
              else pl.BlockSpec((tm, tk), lambda i, j, k: (i, k)))
    b_spec = (pl.BlockSpec((tn, tk), lambda i, j, k: (j, k)) if mode == "nt"
              else pl.BlockSpec((tk, tn), lambda i, j, k: (k, j)))
    o_spec = pl.BlockSpec((tm, tn), lambda i, j, k: (i, j))
    in_specs, operands = [a_spec, b_spec], [a, b]
    if bias is not None:
        in_specs.append(pl.BlockSpec((1, tn), lambda i, j, k: (0, j)))
        operands.append(bias)
    if res is not None:
        in_specs += [o_spec, pl.BlockSpec((1, 1, tn), lambda i, j, k: (gidx(i), 0, j))]
        operands += [res, gate]
        out_shape = (jax.ShapeDtypeStruct((M, N), F32), jax.ShapeDtypeStruct((M, N), F32))
        out_specs = (o_spec, o_spec)
    else:
        out_shape = jax.ShapeDtypeStruct((M, N), out_dtype)
        out_specs = o_spec

    def body(*refs):
        a_ref, b_ref = refs[0], refs[1]
        extra = refs[2:len(operands)]
        outs = refs[len(operands):]
        prod = _dot(a_ref[...], b_ref[...], dims)

        def finish(acc):
            if bias is not None:
                outs[0][...] = (acc + extra[0][...]).astype(out_dtype)
            elif res is not None:
                outs[0][...] = acc
                outs[1][...] = extra[0][...] + extra[1][0] * acc
            else:
                outs[0][...] = acc.astype(out_dtype)

        if nk == 1:
            finish(prod)
        else:
            acc_ref = outs[-1]
            outs = outs[:-1]
            k = pl.program_id(2)

            @pl.when(k == 0)
            def _():
                acc_ref[...] = prod

            @pl.when(k > 0)
            def _():
                acc_ref[...] += prod

            @pl.when(k == nk - 1)
            def _():
                finish(acc_ref[...])

    return pl.pallas_call(
        body, name=name, grid=(M // tm, N // tn, nk), in_specs=in_specs, out_specs=out_specs, out_shape=out_shape,
        scratch_shapes=[pltpu.VMEM((tm, tn), F32)] if nk > 1 else [],
        compiler_params=_params())(*operands)


def _group_index(n_x_tiles, tiles_per_example, n_examples):
    def gidx(i):
        return jnp.where(i < n_x_tiles, i // tiles_per_example, n_examples)
    return gidx


def _norm_mod_fwd(x, g, shift, scale, gidx, name):
    T, Dm = x.shape

    def body(x_ref, g_ref, sh_ref, sc_ref, h_ref):
        xv = x_ref[...]
        r = lax.rsqrt(jnp.mean(xv * xv, axis=-1, keepdims=True) + EPS)
        y = xv * r * g_ref[...]
        h_ref[...] = (y * (1.0 + sc_ref[0]) + sh_ref[0]).astype(h_ref.dtype)

    row = pl.BlockSpec((ROW_TILE, Dm), lambda i: (i, 0))
    mod = pl.BlockSpec((1, 1, Dm), lambda i: (gidx(i), 0, 0))
    return pl.pallas_call(
        body, name=name, grid=(T // ROW_TILE,),
        in_specs=[row, pl.BlockSpec((1, Dm), lambda i: (0, 0)), mod, mod],
        out_specs=row, out_shape=jax.ShapeDtypeStruct((T, Dm), MXU_DTYPE),
        compiler_params=_params())(x, g, shift, scale)


def _first_of_group(i, gidx):
    return jnp.logical_or(i == 0, gidx(i) != gidx(jnp.maximum(i - 1, 0)))


def _norm_mod_bwd(dh, x, g, scale, dres, gidx, n_groups, name):
    T, Dm = x.shape

    def body(dh_ref, x_ref, g_ref, sc_ref, dres_ref, dx_ref, dsh_ref, dsc_ref, dg_ref):
        i = pl.program_id(0)
        xv, dhv = x_ref[...], dh_ref[...]
        r = lax.rsqrt(jnp.mean(xv * xv, axis=-1, keepdims=True) + EPS)
        xn = xv * r
        y = xn * g_ref[...]

        @pl.when(_first_of_group(i, gidx))
        def _():
            dsh_ref[...] = jnp.zeros_like(dsh_ref)
            dsc_ref[...] = jnp.zeros_like(dsc_ref)

        @pl.when(i == 0)
        def _():
            dg_ref[...] = jnp.zeros_like(dg_ref)

        dsh_ref[0] += jnp.sum(dhv, axis=0, keepdims=True)
        dsc_ref[0] += jnp.sum(dhv * y, axis=0, keepdims=True)
        dy = dhv * (1.0 + sc_ref[0])
        dg_ref[...] += jnp.sum(dy * xn, axis=0, keepdims=True)
        dxn = dy * g_ref[...]
        dx = r * (dxn - xn * jnp.mean(dxn * xn, axis=-1, keepdims=True))
        dx_ref[...] = dres_ref[...] + dx

    row = pl.BlockSpec((ROW_TILE, Dm), lambda i: (i, 0))
    mod = pl.BlockSpec((1, 1, Dm), lambda i: (gidx(i), 0, 0))
    vec = pl.BlockSpec((1, Dm), lambda i: (0, 0))
    return pl.pallas_call(
        body, name=name, grid=(T // ROW_TILE,),
        in_specs=[row, row, vec, mod, row],
        out_specs=(row, mod, mod, vec),
        out_shape=(jax.ShapeDtypeStruct((T, Dm), F32), jax.ShapeDtypeStruct((n_groups, 1, Dm), F32),
                   jax.ShapeDtypeStruct((n_groups, 1, Dm), F32), jax.ShapeDtypeStruct((1, Dm), F32)),
        compiler_params=_params())(dh, x, g, scale, dres)


def _gate_bwd(dy, f, gate, gidx, n_groups, name):
    T, Dm = dy.shape

    def body(dy_ref, f_ref, gate_ref, dz_ref, dgate_ref):
        i = pl.program_id(0)
        dyv = dy_ref[...]

        @pl.when(_first_of_group(i, gidx))
        def _():
            dgate_ref[...] = jnp.zeros_like(dgate_ref)

        dgate_ref[0] += jnp.sum(dyv * f_ref[...], axis=0, keepdims=True)
        dz_ref[...] = (dyv * gate_ref[0]).astype(dz_ref.dtype)

    row = pl.BlockSpec((ROW_TILE, Dm), lambda i: (i, 0))
    mod = pl.BlockSpec((1, 1, Dm), lambda i: (gidx(i), 0, 0))
    return pl.pallas_call(
        body, name=name, grid=(T // ROW_TILE,), in_specs=[row, row, mod], out_specs=(row, mod),
        out_shape=(jax.ShapeDtypeStruct((T, Dm), MXU_DTYPE), jax.ShapeDtypeStruct((n_groups, 1, Dm), F32)),
        compiler_params=_params())(dy, f, gate)


SWIGLU_ROWS = 256


def _swiglu_fwd(u, name):
    T = u.shape[0]

    def body(u_ref, a_ref):
        gate, up = u_ref[:, :D_FF], u_ref[:, D_FF:]
        a_ref[...] = (gate * jax.nn.sigmoid(gate) * up).astype(a_ref.dtype)

    return pl.pallas_call(
        body, name=name, grid=(T // SWIGLU_ROWS,),
        in_specs=[pl.BlockSpec((SWIGLU_ROWS, 2 * D_FF), lambda i: (i, 0))],
        out_specs=pl.BlockSpec((SWIGLU_ROWS, D_FF), lambda i: (i, 0)),
        out_shape=jax.ShapeDtypeStruct((T, D_FF), MXU_DTYPE), compiler_params=_params())(u)


def _swiglu_bwd(da, u, name):
    T = u.shape[0]

    def body(da_ref, u_ref, du_ref):
        gate, up, dav = u_ref[:, :D_FF], u_ref[:, D_FF:], da_ref[...]
        sg = jax.nn.sigmoid(gate)
        du_ref[:, :D_FF] = (dav * up * (sg * (1.0 + gate * (1.0 - sg)))).astype(du_ref.dtype)
        du_ref[:, D_FF:] = (dav * gate * sg).astype(du_ref.dtype)

    return pl.pallas_call(
        body, name=name, grid=(T // SWIGLU_ROWS,),
        in_specs=[pl.BlockSpec((SWIGLU_ROWS, D_FF), lambda i: (i, 0)),
                  pl.BlockSpec((SWIGLU_ROWS, 2 * D_FF), lambda i: (i, 0))],
        out_specs=pl.BlockSpec((SWIGLU_ROWS, 2 * D_FF), lambda i: (i, 0)),
        out_shape=jax.ShapeDtypeStruct((T, 2 * D_FF), MXU_DTYPE), compiler_params=_params())(da, u)


def _loss_fwd_bwd(y, target, name):
    T, Dm = y.shape

    def body(y_ref, t_ref, loss_ref, dy_ref):
        err = y_ref[...] - t_ref[...]

        @pl.when(pl.program_id(0) == 0)
        def _():
            loss_ref[...] = jnp.zeros_like(loss_ref)

        loss_ref[...] += 0.5 * jnp.sum(jnp.mean(err * err, axis=-1, keepdims=True))
        dy_ref[...] = err * (1.0 / Dm)

    row = pl.BlockSpec((ROW_TILE, Dm), lambda i: (i, 0))
    return pl.pallas_call(
        body, name=name, grid=(T // ROW_TILE,), in_specs=[row, row],
        out_specs=(pl.BlockSpec((8, LANES), lambda i: (0, 0)), row),
        out_shape=(jax.ShapeDtypeStruct((8, LANES), F32), jax.ShapeDtypeStruct((T, Dm), F32)),
        compiler_params=_params())(y, target)


def _rope_tables(seq, head_dim):
    axis_dim = head_dim // 2
    half = axis_dim // 2
    pos = jnp.arange(seq, dtype=jnp.int32)
    row = (pos // GRID_W).astype(F32)[:, None]
    col = (pos % GRID_W).astype(F32)[:, None]
    inv = ROPE_BASE ** (-jnp.arange(0, axis_dim, 2, dtype=F32) / axis_dim)
    lane = jnp.arange(head_dim, dtype=jnp.int32)
    within = lane % axis_dim
    ang = jnp.where((lane // axis_dim == 0)[None, :], row, col) * inv[within % half][None, :]
    cos = jnp.cos(ang)
    sin = jnp.where((within < half)[None, :], -jnp.sin(ang), jnp.sin(ang))
    cos = jnp.concatenate([cos, jnp.ones((ROW_TILE, head_dim), F32)], axis=0)
    sin = jnp.concatenate([sin, jnp.zeros((ROW_TILE, head_dim), F32)], axis=0)
    return cos, sin


def _pair_swap(v, half):
    if 2 * half == LANES:
        return pltpu.roll(v, half, axis=1)
    lane = lax.broadcasted_iota(jnp.int32, v.shape, 1)
    return jnp.where((lane % (2 * half)) < half, pltpu.roll(v, LANES - half, axis=1), pltpu.roll(v, half, axis=1))


def _head_sum(v, ones_ref):
    hi = v.astype(MXU_DTYPE)
    lo = (v - hi.astype(F32)).astype(MXU_DTYPE)
    return (jnp.dot(hi, ones_ref[...], preferred_element_type=F32)
            + jnp.dot(lo, ones_ref[...], preferred_element_type=F32))


def _head_ones():
    lane = jnp.arange(LANES)
    return (lane[:, None] // HEAD_DIM == lane[None, :] // HEAD_DIM).astype(MXU_DTYPE)


ATTN_QK_BLOCKS = (N_HEADS + N_KV_HEADS) * HEAD_DIM // LANES
ATTN_ALL_BLOCKS = (N_HEADS + 2 * N_KV_HEADS) * HEAD_DIM // LANES
ATTN_Q_BLOCKS = N_HEADS * HEAD_DIM // LANES


def _attn_prep_fwd(qkv, gains, cos, sin, tidx, name):
    T = qkv.shape[0]

    def body(x_ref, g_ref, cos_ref, sin_ref, ones_ref, o_ref):
        cb = pl.program_id(1)

        @pl.when(cb < ATTN_QK_BLOCKS)
        def _():
            xv = x_ref[...]
            r = lax.rsqrt(_head_sum(xv * xv, ones_ref) * (1.0 / HEAD_DIM) + EPS)
            y = xv * r * g_ref[0]
            o_ref[...] = (y * cos_ref[...] + _pair_swap(y, HEAD_DIM // 4) * sin_ref[...]).astype(o_ref.dtype)

        @pl.when(cb >= ATTN_QK_BLOCKS)
        def _():
            o_ref[...] = x_ref[...].astype(o_ref.dtype)

    blk = pl.BlockSpec((ROW_TILE, LANES), lambda i, cb: (i, cb))
    tab = pl.BlockSpec((ROW_TILE, LANES), lambda i, cb: (tidx(i), 0))
    return pl.pallas_call(
        body, name=name, grid=(T // ROW_TILE, ATTN_ALL_BLOCKS),
        in_specs=[blk, pl.BlockSpec((1, 1, LANES), lambda i, cb: (jnp.where(cb < ATTN_Q_BLOCKS, 0, 1), 0, 0)),
                  tab, tab, pl.BlockSpec((LANES, LANES), lambda i, cb: (0, 0))],
        out_specs=blk, out_shape=jax.ShapeDtypeStruct(qkv.shape, MXU_DTYPE),
        compiler_params=_params())(qkv, gains, cos, sin, _head_ones())


def _attn_prep_bwd(dqk, dv, qkv, gains, cos, sin, tidx, name):
    T = qkv.shape[0]
    nt = T // ROW_TILE

    def body(dqk_ref, dv_ref, x_ref, g_ref, cos_ref, sin_ref, ones_ref, o_ref, dg_ref):
        cb, i = pl.program_id(0), pl.program_id(1)

        @pl.when(i == 0)
        def _():
            dg_ref[...] = jnp.zeros_like(dg_ref)

        @pl.when(cb < ATTN_QK_BLOCKS)
        def _():
            xv, d = x_ref[...], dqk_ref[...]
            r = lax.rsqrt(_head_sum(xv * xv, ones_ref) * (1.0 / HEAD_DIM) + EPS)
            xn = xv * r
            dy = d * cos_ref[...] + _pair_swap(d * sin_ref[...], HEAD_DIM // 4)
            dg_ref[0] += jnp.sum(dy * xn, axis=0, keepdims=True)
            dxn = dy * g_ref[0]
            dx = r * (dxn - xn * (_head_sum(dxn * xn, ones_ref) * (1.0 / HEAD_DIM)))
            o_ref[...] = dx.astype(o_ref.dtype)

        @pl.when(cb >= ATTN_QK_BLOCKS)
        def _():
            o_ref[...] = dv_ref[...].astype(o_ref.dtype)

    blk = pl.BlockSpec((ROW_TILE, LANES), lambda cb, i: (i, cb))
    tab = pl.BlockSpec((ROW_TILE, LANES), lambda cb, i: (tidx(i), 0))
    return pl.pallas_call(
        body, name=name, grid=(ATTN_ALL_BLOCKS, nt),
        in_specs=[pl.BlockSpec((ROW_TILE, LANES), lambda cb, i: (i, jnp.minimum(cb, ATTN_QK_BLOCKS - 1))),
                  pl.BlockSpec((ROW_TILE, LANES), lambda cb, i: (i, jnp.maximum(cb - ATTN_QK_BLOCKS, 0))),
                  blk, pl.BlockSpec((1, 1, LANES), lambda cb, i: (jnp.where(cb < ATTN_Q_BLOCKS, 0, 1), 0, 0)),
                  tab, tab, pl.BlockSpec((LANES, LANES), lambda cb, i: (0, 0))],
        out_specs=(blk, pl.BlockSpec((1, 1, LANES), lambda cb, i: (cb, 0, 0))),
        out_shape=(jax.ShapeDtypeStruct(qkv.shape, MXU_DTYPE), jax.ShapeDtypeStruct((ATTN_ALL_BLOCKS, 1, LANES), F32)),
        compiler_params=_params())(dqk, dv, qkv, gains, cos, sin, _head_ones())


RET_QK_BLOCKS = 2 * RET_HEADS * RET_QK_DIM // LANES


def _ret_rope(x, cos, sin, tidx, backward, out_dtype, name):
    T = x.shape[0]
    k_scale = RET_QK_DIM ** -0.5

    def body(x_ref, cos_ref, sin_ref, o_ref):
        xv = x_ref[...]
        scale = jnp.where(pl.program_id(1) < RET_QK_BLOCKS // 2, 1.0, k_scale)
        if backward:
            out = xv * cos_ref[...] + pltpu.roll(xv * sin_ref[...], LANES // 2, axis=1)
        else:
            out = xv * cos_ref[...] + pltpu.roll(xv, LANES // 2, axis=1) * sin_ref[...]
        o_ref[...] = (out * scale).astype(o_ref.dtype)

    blk = pl.BlockSpec((ROW_TILE, LANES), lambda i, cb: (i, cb))
    tab = pl.BlockSpec((ROW_TILE, LANES), lambda i, cb: (tidx(i), cb % 2))
    return pl.pallas_call(
        body, name=name, grid=(T // ROW_TILE, RET_QK_BLOCKS), in_specs=[blk, tab, tab], out_specs=blk,
        out_shape=jax.ShapeDtypeStruct((T, RET_QK_BLOCKS * LANES), out_dtype), compiler_params=_params())(x, cos, sin)


def _attn_probs(qv, kcv, klv, sink_ref, kv_head, j, qb, seq):
    scale = HEAD_DIM ** -0.5
    rows = GQA_GROUP * qb
    s_c = _dot(qv, kcv, _NT) * scale
    g = lax.broadcasted_iota(jnp.int32, (rows, 1), 0) // qb
    sink = jnp.zeros((rows, 1), F32)
    for gi in range(GQA_GROUP):
        sink = jnp.where(g == gi, sink_ref[kv_head, gi], sink)
    m = jnp.maximum(jnp.max(s_c, axis=-1, keepdims=True), sink)
    s_l = None
    if klv is not None:
        band = klv.shape[0]
        s_l = _dot(qv, klv, _NT) * scale
        i = lax.broadcasted_iota(jnp.int32, (rows, band), 0) % qb
        n = lax.broadcasted_iota(jnp.int32, (rows, band), 1)
        key_pos = j * ATTN_BLOCK - WINDOW + n
        valid = (n >= i) & (n - i <= 2 * WINDOW) & (key_pos >= 0) & (key_pos < seq)
        s_l = jnp.where(valid, s_l, NEG_INF)
        m = jnp.maximum(m, jnp.max(s_l, axis=-1, keepdims=True))
    e_c = jnp.exp(s_c - m)
    e_s = jnp.exp(sink - m)
    den = jnp.sum(e_c, axis=-1, keepdims=True) + e_s
    e_l = None
    if klv is not None:
        e_l = jnp.exp(s_l - m)
        den = den + jnp.sum(e_l, axis=-1, keepdims=True)
    inv = 1.0 / den
    return e_c * inv, (None if e_l is None else e_l * inv), e_s * inv


def _attn_specs(B, seq, ctx_len, qb, has_local):
    nb = seq // qb
    q_spec = pl.BlockSpec((1, GQA_GROUP, qb, HEAD_DIM), lambda b, k, j: (b, k, j, 0))
    c_spec = pl.BlockSpec((1, 1, ctx_len, HEAD_DIM), lambda b, k, j: (b, k, 0, 0))
    local = []
    if has_local:
        local = [pl.BlockSpec((1, 1, qb, HEAD_DIM), lambda b, k, j: (b, k, jnp.maximum(j - 1, 0), 0)),
                 pl.BlockSpec((1, 1, qb, HEAD_DIM), lambda b, k, j: (b, k, j, 0)),
                 pl.BlockSpec((1, 1, qb, HEAD_DIM), lambda b, k, j: (b, k, jnp.minimum(j + 1, nb - 1), 0))]
    return nb, q_spec, c_spec, local


def _attn_fwd(q, k, v, kc, vc, sink, name):
    B, _, seq, _ = q.shape
    ctx_len = kc.shape[2]
    has_local = k is not None
    qb = ATTN_BLOCK if has_local else seq
    nb, q_spec, c_spec, local = _attn_specs(B, seq, ctx_len, qb, has_local)

    def body(*refs):
        q_ref = refs[0]
        kc_ref, vc_ref, sink_ref, o_ref = refs[-4:]
        kv_head, j = pl.program_id(1), pl.program_id(2)
        qv = q_ref[0].reshape(GQA_GROUP * qb, HEAD_DIM)
        klv = vlv = None
        if has_local:
            klv = jnp.concatenate([r[0, 0] for r in refs[1:4]], axis=0)
            vlv = jnp.concatenate([r[0, 0] for r in refs[4:7]], axis=0)
        p_c, p_l, _ = _attn_probs(qv, kc_ref[0, 0], klv, sink_ref, kv_head, j, qb, seq)
        o = _dot(p_c, vc_ref[0, 0], _NN)
        if has_local:
            o = o + _dot(p_l, vlv, _NN)
        o_ref[0] = o.reshape(GQA_GROUP, qb, HEAD_DIM)

    operands = [q] + ([k, k, k, v, v, v] if has_local else []) + [kc, vc, sink]
    return pl.pallas_call(
        body, name=name, grid=(B, N_KV_HEADS, nb),
        in_specs=[q_spec] + local + local + [c_spec, c_spec, _SMEM],
        out_specs=q_spec, out_shape=jax.ShapeDtypeStruct(q.shape, F32), compiler_params=_params())(*operands)


def _attn_bwd(q, k, v, kc, vc, sink, do, name):
    B, _, seq, _ = q.shape
    ctx_len = kc.shape[2]
    has_local = k is not None
    qb = ATTN_BLOCK if has_local else seq
    nb, q_spec, c_spec, local = _attn_specs(B, seq, ctx_len, qb, has_local)
    scale = HEAD_DIM ** -0.5

    def body(*refs):
        n_in = 1 + (6 if has_local else 0) + 4
        q_ref = refs[0]
        kc_ref, vc_ref, sink_ref, do_ref = refs[n_in - 4:n_in]
        outs = refs[n_in:]
        dq_ref = outs[0]
        dkc_ref, dvc_ref, dsink_ref = outs[-3:]
        b, kv_head, j = pl.program_id(0), pl.program_id(1), pl.program_id(2)
        rows = GQA_GROUP * qb
        qv = q_ref[0].reshape(rows, HEAD_DIM)
        dov = do_ref[0].reshape(rows, HEAD_DIM)
        kcv, vcv = kc_ref[0, 0], vc_ref[0, 0]
        klv = vlv = None
        if has_local:
            klv = jnp.concatenate([r[0, 0] for r in refs[1:4]], axis=0)
            vlv = jnp.concatenate([r[0, 0] for r in refs[4:7]], axis=0)
        p_c, p_l, p_s = _attn_probs(qv, kcv, klv, sink_ref, kv_head, j, qb, seq)
        dp_c = _dot(dov, vcv, _NT)
        delta = jnp.sum(p_c * dp_c, axis=-1, keepdims=True)
        if has_local:
            dp_l = _dot(dov, vlv, _NT)
            delta = delta + jnp.sum(p_l * dp_l, axis=-1, keepdims=True)
        ds_c = p_c * (dp_c - delta) * scale
        dq = _dot(ds_c, kcv, _NN)

        @pl.when(j == 0)
        def _():
            dkc_ref[...] = jnp.zeros_like(dkc_ref)
            dvc_ref[...] = jnp.zeros_like(dvc_ref)
            if has_local:
                outs[1][...] = jnp.zeros_like(outs[1])
                outs[2][...] = jnp.zeros_like(outs[2])

        @pl.when((b == 0) & (kv_head == 0) & (j == 0))
        def _():
            dsink_ref[...] = jnp.zeros_like(dsink_ref)

        dkc_ref[0, 0] += _dot(ds_c, qv, _TN)
        dvc_ref[0, 0] += _dot(p_c, dov, _TN)
        if has_local:
            ds_l = p_l * (dp_l - delta) * scale
            dq = dq + _dot(ds_l, klv, _NN)
            dkl = _dot(ds_l, qv, _TN)
            dvl = _dot(p_l, dov, _TN)
            dk_ref, dv_ref = outs[1], outs[2]
            for t in range(3):
                def add(t=t):
                    start = pl.multiple_of((j - 1 + t) * qb, qb)
                    dk_ref[0, 0, pl.ds(start, qb), :] += dkl[t * qb:(t + 1) * qb]
                    dv_ref[0, 0, pl.ds(start, qb), :] += dvl[t * qb:(t + 1) * qb]
                if t == 0:
                    pl.when(j > 0)(add)
                elif t == 2:
                    pl.when(j < nb - 1)(add)
                else:
                    add()
        dq_ref[0] = dq.reshape(GQA_GROUP, qb, HEAD_DIM)
        dsk = -(p_s * delta)
        sub = lax.broadcasted_iota(jnp.int32, (8, LANES), 0)
        tile = jnp.zeros((8, LANES), F32)
        for gi in range(GQA_GROUP):
            tile = jnp.where(sub == gi, jnp.sum(dsk[gi * qb:(gi + 1) * qb]), tile)
        dsink_ref[pl.ds(pl.multiple_of(kv_head * 8, 8), 8), :] += tile

    full = pl.BlockSpec((1, 1, seq, HEAD_DIM), lambda b, k, j: (b, k, 0, 0))
    operands = [q] + ([k, k, k, v, v, v] if has_local else []) + [kc, vc, sink, do]
    out_specs = [q_spec] + ([full, full] if has_local else []) + [c_spec, c_spec,
                                                                  pl.BlockSpec((32, LANES), lambda b, k, j: (0, 0))]
    out_shape = ([jax.ShapeDtypeStruct(q.shape, F32)]
                 + ([jax.ShapeDtypeStruct(k.shape, F32)] * 2 if has_local else [])
                 + [jax.ShapeDtypeStruct(kc.shape, F32)] * 2 + [jax.ShapeDtypeStruct((32, LANES), F32)])
    return pl.pallas_call(
        body, name=name, grid=(B, N_KV_HEADS, nb),
        in_specs=[q_spec] + local + local + [c_spec, c_spec, _SMEM, q_spec],
        out_specs=tuple(out_specs), out_shape=tuple(out_shape), compiler_params=_params())(*operands)


def _ret_decays(lg, rev):
    n = lax.broadcasted_iota(jnp.int32, (RET_CHUNK, RET_CHUNK), 0).astype(F32)
    m = lax.broadcasted_iota(jnp.int32, (RET_CHUNK, RET_CHUNK), 1).astype(F32)
    pos = lax.broadcasted_iota(jnp.int32, (RET_CHUNK, 1), 0).astype(F32)
    diff = (m - n) if rev else (n - m)
    a_exp = jnp.maximum(diff, 0.0)
    intra = jnp.where(diff >= 0, jnp.exp(lg * a_exp), 0.0)
    q_exp = (RET_CHUNK - pos) if rev else (pos + 1.0)
    k_exp = pos if rev else (RET_CHUNK - 1.0 - pos)
    chunk = jnp.exp(jnp.full((1, 1), RET_CHUNK, F32) * lg)
    return intra, a_exp, jnp.exp(lg * q_exp), q_exp, jnp.exp(lg * k_exp), k_exp, chunk


def _ctx_decay(lg, ctx_len, rev):
    t = lax.broadcasted_iota(jnp.int32, (ctx_len, 1), 0).astype(F32)
    expo = t if rev else (ctx_len - 1.0 - t)
    return jnp.exp(lg * expo), expo


def _ret_specs(B, seq, ctx_len, order):
    nc = seq // RET_CHUNK
    x_blocks = B * seq // ctx_len

    def rows(b, c):
        return b * nc + order(c, nc)

    q_spec = pl.BlockSpec((RET_CHUNK, RET_QK_DIM), lambda b, h, c: (rows(b, c), h))
    k_spec = pl.BlockSpec((RET_CHUNK, RET_QK_DIM), lambda b, h, c: (rows(b, c), RET_HEADS + h))
    v_spec = pl.BlockSpec((RET_CHUNK, RET_V_DIM), lambda b, h, c: (rows(b, c), RET_HEADS + h))
    kc_spec = pl.BlockSpec((ctx_len, RET_QK_DIM), lambda b, h, c: (x_blocks + b, RET_HEADS + h))
    vc_spec = pl.BlockSpec((ctx_len, RET_V_DIM), lambda b, h, c: (x_blocks + b, RET_HEADS + h))
    st_spec = pl.BlockSpec((1, 1, 1, RET_QK_DIM, RET_V_DIM), lambda b, h, c: (b, h, order(c, nc), 0, 0))
    o_spec = pl.BlockSpec((RET_CHUNK, RET_V_DIM), lambda b, h, c: (rows(b, c), h))
    return nc, q_spec, k_spec, v_spec, kc_spec, vc_spec, st_spec, o_spec


def _ret_fwd(qk, qkvg, log_g, B, seq, ctx_len, rev, name):
    direction = 1 if rev else 0
    order = (lambda c, nc: nc - 1 - c) if rev else (lambda c, nc: c)
    nc, q_spec, k_spec, v_spec, kc_spec, vc_spec, st_spec, o_spec = _ret_specs(B, seq, ctx_len, order)

    def body(lg_ref, q_ref, k_ref, v_ref, kc_ref, vc_ref, o_ref, st_ref, state):
        h, c = pl.program_id(1), pl.program_id(2)
        lg = lg_ref[direction, h]
        intra, _, q_dec, _, k_dec, _, chunk_dec = _ret_decays(lg, rev)

        @pl.when(c == 0)
        def _():
            dec, _ = _ctx_decay(lg, ctx_len, rev)
            state[...] = _dot(kc_ref[...] * dec, vc_ref[...], _TN)

        qv, kv, vv = q_ref[...], k_ref[...], v_ref[...]
        s_in = state[...]
        st_ref[0, 0, 0] = s_in
        w = _dot(qv, kv, _NT) * intra
        o_ref[...] = _dot(w, vv, _NN) + _dot(qv, s_in, _NN) * q_dec
        state[...] = s_in * chunk_dec + _dot(kv * k_dec, vv, _TN)

    return pl.pallas_call(
        body, name=name, grid=(B, RET_HEADS, nc),
        in_specs=[_SMEM, q_spec, k_spec, v_spec, kc_spec, vc_spec],
        out_specs=(o_spec, st_spec),
        out_shape=(jax.ShapeDtypeStruct((B * seq, RET_VWIDTH), F32),
                   jax.ShapeDtypeStruct((B, RET_HEADS, nc, RET_QK_DIM, RET_V_DIM), F32)),
        scratch_shapes=[pltpu.VMEM((RET_QK_DIM, RET_V_DIM), F32)],
        compiler_params=_params())(log_g, qk, qk, qkvg, qk, qkvg)


def _ret_bwd(qk, qkvg, log_g, states, do, B, seq, ctx_len, rev, name):
    direction = 1 if rev else 0
    order = (lambda c, nc: c) if rev else (lambda c, nc: nc - 1 - c)
    nc, q_spec, k_spec, v_spec, kc_spec, vc_spec, st_spec, o_spec = _ret_specs(B, seq, ctx_len, order)

    def body(lg_ref, q_ref, k_ref, v_ref, kc_ref, vc_ref, st_ref, do_ref,
             dq_ref, dk_ref, dv_ref, dkc_ref, dvc_ref, dlg_ref, dstate):
        h, c = pl.program_id(1), pl.program_id(2)
        lg = lg_ref[direction, h]
        intra, a_exp, q_dec, q_exp, k_dec, k_exp, chunk_dec = _ret_decays(lg, rev)

        @pl.when(c == 0)
        def _():
            dstate[...] = jnp.zeros_like(dstate)
            dlg_ref[...] = jnp.zeros_like(dlg_ref)

        qv, kv, vv, dov = q_ref[...], k_ref[...], v_ref[...], do_ref[...]
        s_in, ds_out = st_ref[0, 0, 0], dstate[...]
        p = _dot(qv, kv, _NT)
        w = p * intra
        dw = _dot(dov, vv, _NT)
        dp = dw * intra
        do_dec = dov * q_dec
        kd = kv * k_dec
        v_ds = _dot(vv, ds_out, _NT)
        dq_ref[...] = _dot(dp, kv, _NN) + _dot(do_dec, s_in, _NT)
        dk_ref[...] = _dot(dp, qv, _TN) + v_ds * k_dec
        dv_ref[...] = _dot(w, dov, _TN) + _dot(kd, ds_out, _NN)
        q_s = _dot(qv, s_in, _NN)
        dlg = (jnp.sum(dw * w * a_exp)
               + jnp.sum(q_exp * q_dec * jnp.sum(dov * q_s, axis=-1, keepdims=True))
               + jnp.sum(k_exp * k_dec * jnp.sum(kv * v_ds, axis=-1, keepdims=True))
               + RET_CHUNK * jnp.sum(chunk_dec * (ds_out * s_in)))
        ds_in = ds_out * chunk_dec + _dot(qv, do_dec, _TN)
        dstate[...] = ds_in
        dlg_ref[...] += dlg

        @pl.when(c == nc - 1)
        def _():
            dec, expo = _ctx_decay(lg, ctx_len, rev)
            kcv, vcv = kc_ref[...], vc_ref[...]
            vc_ds = _dot(vcv, ds_in, _NT)
            dkc_ref[...] = vc_ds * dec
            dvc_ref[...] = _dot(kcv * dec, ds_in, _NN)
            dlg_ref[...] += jnp.sum(expo * dec * jnp.sum(kcv * vc_ds, axis=-1, keepdims=True))

    dq_spec = pl.BlockSpec((RET_CHUNK, RET_QK_DIM), q_spec.index_map)
    return pl.pallas_call(
        body, name=name, grid=(B, RET_HEADS, nc),
        in_specs=[_SMEM, q_spec, k_spec, v_spec, kc_spec, vc_spec, st_spec, o_spec],
        out_specs=(dq_spec, dq_spec, o_spec,
                   pl.BlockSpec((ctx_len, RET_QK_DIM), lambda b, h, c: (b, h)),
                   pl.BlockSpec((ctx_len, RET_V_DIM), lambda b, h, c: (b, h)),
                   pl.BlockSpec((1, 1, 8, LANES), lambda b, h, c: (b, h, 0, 0))),
        out_shape=(jax.ShapeDtypeStruct((B * seq, RET_HEADS * RET_QK_DIM), F32),
                   jax.ShapeDtypeStruct((B * seq, RET_HEADS * RET_QK_DIM), F32),
                   jax.ShapeDtypeStruct((B * seq, RET_VWIDTH), F32),
                   jax.ShapeDtypeStruct((B * ctx_len, RET_HEADS * RET_QK_DIM), F32),
                   jax.ShapeDtypeStruct((B * ctx_len, RET_VWIDTH), F32),
                   jax.ShapeDtypeStruct((B, RET_HEADS, 8, LANES), F32)),
        scratch_shapes=[pltpu.VMEM((RET_QK_DIM, RET_V_DIM), F32)],
        compiler_params=_params())(log_g, qk, qk, qkvg, qk, qkvg, states, do)


def _gated_out_fwd(o_f, o_b, qkvg, gn_gain, name):
    T = o_f.shape[0]
    g_off = (2 * RET_HEADS * RET_QK_DIM + RET_VWIDTH) // RET_V_DIM

    def body(of_ref, ob_ref, g_ref, gain_ref, z_ref):
        o = of_ref[...] + ob_ref[...]
        mu = jnp.mean(o, axis=-1, keepdims=True)
        var = jnp.mean(jnp.square(o - mu), axis=-1, keepdims=True)
        y = (o - mu) * lax.rsqrt(var + EPS) * gain_ref[...]
        gv = g_ref[...]
        z_ref[...] = (gv * jax.nn.sigmoid(gv) * y).astype(z_ref.dtype)

    blk = pl.BlockSpec((ROW_TILE, RET_V_DIM), lambda i, h: (i, h))
    return pl.pallas_call(
        body, name=name, grid=(T // ROW_TILE, RET_HEADS),
        in_specs=[blk, blk, pl.BlockSpec((ROW_TILE, RET_V_DIM), lambda i, h: (i, g_off + h)),
                  pl.BlockSpec((1, RET_V_DIM), lambda i, h: (0, h))],
        out_specs=blk, out_shape=jax.ShapeDtypeStruct((T, RET_VWIDTH), MXU_DTYPE),
        compiler_params=_params())(o_f, o_b, qkvg, gn_gain)


def _gated_out_bwd(dz, o_f, o_b, qkvg, gn_gain, name):
    T = o_f.shape[0]
    g_off = (2 * RET_HEADS * RET_QK_DIM + RET_VWIDTH) // RET_V_DIM

    def body(dz_ref, of_ref, ob_ref, g_ref, gain_ref, do_ref, dg_ref, dgain_ref):
        o = of_ref[...] + ob_ref[...]
        mu = jnp.mean(o, axis=-1, keepdims=True)
        var = jnp.mean(jnp.square(o - mu), axis=-1, keepdims=True)
        rstd = lax.rsqrt(var + EPS)
        yhat = (o - mu) * rstd
        gv, dzv = g_ref[...], dz_ref[...]
        sg = jax.nn.sigmoid(gv)
        dg_ref[...] = (dzv * (yhat * gain_ref[...]) * (sg * (1.0 + gv * (1.0 - sg)))).astype(dg_ref.dtype)
        dy = dzv * (gv * sg)

        @pl.when(pl.program_id(1) == 0)
        def _():
            dgain_ref[...] = jnp.zeros_like(dgain_ref)

        dgain_ref[...] += jnp.sum(dy * yhat, axis=0, keepdims=True)
        dyh = dy * gain_ref[...]
        do_ref[...] = rstd * (dyh - jnp.mean(dyh, axis=-1, keepdims=True)
                              - yhat * jnp.mean(dyh * yhat, axis=-1, keepdims=True))

    blk = pl.BlockSpec((ROW_TILE, RET_V_DIM), lambda h, i: (i, h))
    vec = pl.BlockSpec((1, RET_V_DIM), lambda h, i: (0, h))
    return pl.pallas_call(
        body, name=name, grid=(RET_HEADS, T // ROW_TILE),
        in_specs=[blk, blk, blk, pl.BlockSpec((ROW_TILE, RET_V_DIM), lambda h, i: (i, g_off + h)), vec],
        out_specs=(blk, blk, vec),
        out_shape=(jax.ShapeDtypeStruct((T, RET_VWIDTH), F32), jax.ShapeDtypeStruct((T, RET_VWIDTH), MXU_DTYPE),
                   jax.ShapeDtypeStruct((1, RET_VWIDTH), F32)),
        compiler_params=_params())(dz, o_f, o_b, qkvg, gn_gain)


def _adamw(w, m, v, parts, name):
    R, C = w.shape
    tr = _tile(R, (256, 128, 64, 32, 16, 8))
    n_parts = [p.shape[0] for p in parts]

    def body(*refs):
        w_ref, m_ref, v_ref = refs[:3]
        part_refs = refs[3:3 + len(parts)]
        g_ref, d_ref, nm_ref, nv_ref = refs[3 + len(parts):]
        g = None
        for ref, n in zip(part_refs, n_parts):
            for r in range(n):
                term = ref[r].astype(F32)
                g = term if g is None else g + term
        mn = ADAM_B1 * m_ref[...] + (1.0 - ADAM_B1) * g
        vn = ADAM_B2 * v_ref[...] + (1.0 - ADAM_B2) * jnp.square(g)
        m_hat = mn / (1.0 - ADAM_B1 ** ADAM_STEP)
        v_hat = vn / (1.0 - ADAM_B2 ** ADAM_STEP)
        g_ref[...] = g
        d_ref[...] = -ADAM_LR * (m_hat / (jnp.sqrt(v_hat) + ADAM_EPS) + ADAM_WD * w_ref[...])
        nm_ref[...] = mn
        nv_ref[...] = vn

    blk = pl.BlockSpec((tr, C), lambda i: (i, 0))
    part_specs = [pl.BlockSpec((n, tr, C), lambda i: (0, i, 0)) for n in n_parts]
    shp = jax.ShapeDtypeStruct((R, C), F32)
    return pl.pallas_call(
        body, name=name, grid=(R // tr,), in_specs=[blk, blk, blk] + part_specs,
        out_specs=(blk, blk, blk, blk), out_shape=(shp, shp, shp, shp),
        compiler_params=_params())(w, m, v, *parts)


def _sum_rows(parts, name):
    n, R, C = parts.shape
    tr = _tile(R, (256, 128, 64, 32, 16, 8))

    def body(p_ref, o_ref):
        acc = p_ref[0]
        for r in range(1, n):
            acc = acc + p_ref[r]
        o_ref[...] = acc

    return pl.pallas_call(
        body, name=name, grid=(R // tr,), in_specs=[pl.BlockSpec((n, tr, C), lambda i: (0, i, 0))],
        out_specs=pl.BlockSpec((tr, C), lambda i: (i, 0)), out_shape=jax.ShapeDtypeStruct((R, C), F32),
        compiler_params=_params())(parts)


def _pair_add(a, b, out_dtype, name):
    n, R, C = a.shape
    tr = _tile(R, (256, 128, 64, 32, 16, 8))

    def body(a_ref, b_ref, o_ref):
        o_ref[...] = (a_ref[...] + b_ref[...]).astype(out_dtype)

    blk = pl.BlockSpec((1, tr, C), lambda q, i: (q, i, 0))
    return pl.pallas_call(
        body, name=name, grid=(n, R // tr), in_specs=[blk, blk], out_specs=blk,
        out_shape=jax.ShapeDtypeStruct((n, R, C), out_dtype), compiler_params=_params())(a, b)


def _my_coords():
    return lax.axis_index("x"), lax.axis_index("y"), lax.axis_index("c")


def _flip(coord, bit):
    return 1 - coord if bit else coord


def _all_gather(x2d, name):
    R, C = x2d.shape

    def body(x_ref, out_ref, send_sems, recv_sems, local_sem):
        x, y, c = _my_coords()
        me, sibling = (x, y, c), (x, y, 1 - c)
        chips = [(1 - x, y), (x, 1 - y), (1 - x, 1 - y)]

        def rows(px, py, pc):
            return out_ref.at[4 * px + 2 * py + pc]

        def copy(k, block, to, src=None):
            return pltpu.make_async_remote_copy(
                src_ref=rows(*block) if src is None else src, dst_ref=rows(*block),
                send_sem=send_sems.at[k], recv_sem=recv_sems.at[k], device_id=to, device_id_type=MESH)

        mine = pltpu.make_async_copy(x_ref, rows(*me), local_sem)
        mine.start()
        first = [copy(0, me, sibling, src=x_ref)]
        first += [copy(1 + j, me, (*chip, c), src=x_ref) for j, chip in enumerate(chips)]
        for cp in first:
            cp.start()
        passed = [copy(4 + j, (*chip, c), sibling) for j, chip in enumerate(chips)]
        for j, chip in enumerate(chips):
            copy(1 + j, (*chip, c), me).wait_recv()
            passed[j].start()
        copy(0, sibling, me).wait_recv()
        for j, chip in enumerate(chips):
            copy(4 + j, (*chip, 1 - c), me).wait_recv()
        for cp in first + passed:
            cp.wait_send()
        mine.wait()

    return pl.pallas_call(
        body, name=name, out_shape=jax.ShapeDtypeStruct((N_DEV, R, C), x2d.dtype),
        in_specs=[_ANY], out_specs=_ANY,
        scratch_shapes=[pltpu.SemaphoreType.DMA((7,)), pltpu.SemaphoreType.DMA((7,)), pltpu.SemaphoreType.DMA],
    )(x2d)


def _exchange(send, masks, row_for_peer, name):
    _, R, C = send.shape

    def body(send_ref, recv_ref, send_sems, recv_sems):
        x, y, c = _my_coords()
        copies = []
        for k, (bx, by, bc) in enumerate(masks):
            peer = (_flip(x, bx), _flip(y, by), _flip(c, bc))
            copies.append(pltpu.make_async_remote_copy(
                src_ref=send_ref.at[row_for_peer(*peer)], dst_ref=recv_ref.at[k],
                send_sem=send_sems.at[k], recv_sem=recv_sems.at[k], device_id=peer, device_id_type=MESH))
        for cp in copies:
            cp.start()
        for cp in copies:
            cp.wait()

    return pl.pallas_call(
        body, name=name, out_shape=jax.ShapeDtypeStruct((len(masks), R, C), send.dtype),
        in_specs=[_ANY], out_specs=_ANY,
        scratch_shapes=[pltpu.SemaphoreType.DMA((len(masks),)), pltpu.SemaphoreType.DMA((len(masks),))],
    )(send)


PACK_COLS = 1024
PACK_ROW_ALIGN = 256

BIG_WEIGHTS = {
    "ffn_w_in": (2, (2, D_MODEL, 2 * D_FF)),
    "ffn_w_out": (1, (2, D_FF, D_MODEL)),
    "attn_w_qkv": (2, (1, D_MODEL, (N_HEADS + 2 * N_KV_HEADS) * HEAD_DIM)),
    "attn_w_o": (1, (1, N_HEADS * HEAD_DIM, D_MODEL)),
    "ret_w_qkvg": (2, (1, D_MODEL, 2 * D_MODEL + 2 * RET_VWIDTH)),
    "ret_gn_g": (1, (1, RET_VWIDTH)),
    "ret_w_o": (1, (1, RET_VWIDTH, D_MODEL)),
}


def _shard_shape(name):
    axis, full = BIG_WEIGHTS[name]
    return tuple(d // N_DEV if a == axis else d for a, d in enumerate(full))


def _pack_rows():
    n = sum(math.prod(_shard_shape(k)) for k in BIG_WEIGHTS)
    rows = -(-n // PACK_COLS)
    return -(-rows // PACK_ROW_ALIGN) * PACK_ROW_ALIGN


def _pack_shards(shards, dtype):
    flat = jnp.concatenate([shards[k].reshape(-1).astype(dtype) for k in BIG_WEIGHTS])
    rows = _pack_rows()
    return jnp.pad(flat, (0, rows * PACK_COLS - flat.shape[0])).reshape(rows, PACK_COLS)


def _unpack_shards(packed):
    lead = packed.shape[:-2]
    flat = packed.reshape(lead + (-1,))
    out, off = {}, 0
    for k in BIG_WEIGHTS:
        shp = _shard_shape(k)
        n = math.prod(shp)
        out[k] = flat[..., off:off + n].reshape(lead + shp)
        off += n
    return out


def _join_shards(name, stacked):
    axis, full = BIG_WEIGHTS[name]
    return jnp.moveaxis(stacked, 0, axis).reshape(full)


def _split_shards(name, full_arr):
    axis, full = BIG_WEIGHTS[name]
    shp = full[:axis] + (N_DEV, full[axis] // N_DEV) + full[axis + 1:]
    return jnp.moveaxis(full_arr.reshape(shp), axis, 0)


def _to_heads(a, B, n, heads):
    return a.reshape(B, n, heads, HEAD_DIM).transpose(0, 2, 1, 3)


def _from_heads(a):
    B, heads, n, _ = a.shape
    return a.transpose(0, 2, 1, 3).reshape(B * n, heads * HEAD_DIM)


def _mods(mod_x, mod_c, layer):
    both = jnp.concatenate([mod_x[:, layer], mod_c[layer][None]], axis=0)
    return [both[:, None, k * D_MODEL:(k + 1) * D_MODEL] for k in range(6)]


def _local_step(x, ctx, target, mod_x, mod_c, w, small):
    B, S, _ = x.shape
    L = ctx.shape[1]
    NX, NC = B * S, B * L
    T = NX + NC
    tiles_per_ex = S // ROW_TILE
    nxt = NX // ROW_TILE
    gidx = _group_index(nxt, tiles_per_ex, B)
    tidx = lambda i: jnp.where(i < nxt, i % tiles_per_ex, tiles_per_ex)
    G = B + 1
    x0 = jnp.concatenate([x.reshape(NX, D_MODEL), ctx.reshape(NC, D_MODEL)], axis=0)
    acos, asin = [jnp.tile(t, (1, LANES // HEAD_DIM)) for t in _rope_tables(S, HEAD_DIM)]
    rcos, rsin = _rope_tables(S, RET_QK_DIM)
    sink = small["attn_sink"].reshape(N_KV_HEADS, GQA_GROUP)
    gains = jnp.stack([jnp.tile(small["attn_q_norm"].reshape(1, HEAD_DIM), (1, LANES // HEAD_DIM)),
                       jnp.tile(small["attn_k_norm"].reshape(1, HEAD_DIM), (1, LANES // HEAD_DIM))])
    log_g = jax.nn.log_sigmoid(small["ret_decay_logit"].reshape(2, RET_HEADS))
    n1, n2 = small["norm1_g"], small["norm2_g"]
    qd, kvd = N_HEADS * HEAD_DIM, N_KV_HEADS * HEAD_DIM

    m0 = _mods(mod_x, mod_c, 0)
    h1 = _norm_mod_fwd(x0, n1[0:1], m0[0], m0[1], gidx, "l0_norm1")
    qkv = _mm(h1, w["attn_w_qkv"][0], "nn", F32, "l0_qkv")
    qkv_r = _attn_prep_fwd(qkv, gains, acos, asin, tidx, "l0_qk_prep")
    q_x, k_x, v_x = (_to_heads(qkv_r[:NX, :qd], B, S, N_HEADS), _to_heads(qkv_r[:NX, qd:qd + kvd], B, S, N_KV_HEADS),
                     _to_heads(qkv_r[:NX, qd + kvd:], B, S, N_KV_HEADS))
    q_c, k_c, v_c = (_to_heads(qkv_r[NX:, :qd], B, L, N_HEADS), _to_heads(qkv_r[NX:, qd:qd + kvd], B, L, N_KV_HEADS),
                     _to_heads(qkv_r[NX:, qd + kvd:], B, L, N_KV_HEADS))
    o_x = _attn_fwd(q_x, k_x, v_x, k_c, v_c, sink, "l0_attn_x")
    o_c = _attn_fwd(q_c, None, None, k_c, v_c, sink, "l0_attn_c")
    o0 = jnp.concatenate([_from_heads(o_x), _from_heads(o_c)], axis=0).astype(MXU_DTYPE)
    mo0, x1 = _mm(o0, w["attn_w_o"][0], "nn", F32, "l0_attn_out", res=x0, gate=m0[2], gidx=gidx)
    h2 = _norm_mod_fwd(x1, n2[0:1], m0[3], m0[4], gidx, "l0_norm2")
    u0 = _mm(h2, w["ffn_w_in"][0], "nn", F32, "l0_ffn_in")
    a0 = _swiglu_fwd(u0, "l0_swiglu")
    f0, x2 = _mm(a0, w["ffn_w_out"][0], "nn", F32, "l0_ffn_out", res=x1, gate=m0[5], gidx=gidx)

    m1 = _mods(mod_x, mod_c, 1)
    g1 = _norm_mod_fwd(x2, n1[1:2], m1[0], m1[1], gidx, "l1_norm1")
    qkvg = _mm(g1, w["ret_w_qkvg"][0], "nn", F32, "l1_qkvg")
    qk = _ret_rope(qkvg, rcos, rsin, tidx, False, F32, "l1_rope")
    of, st_f = _ret_fwd(qk, qkvg, log_g, B, S, L, False, "l1_ret_fwd")
    ob, st_b = _ret_fwd(qk, qkvg, log_g, B, S, L, True, "l1_ret_rev")
    gn = w["ret_gn_g"].reshape(1, RET_VWIDTH)
    z1 = _gated_out_fwd(of, ob, qkvg, gn, "l1_gated_out")
    xx2 = x2[:NX]
    gx = lambda i: i // tiles_per_ex
    m1x = [t[:B] for t in m1]
    mo1, y1 = _mm(z1, w["ret_w_o"][0], "nn", F32, "l1_ret_out", res=xx2, gate=m1x[2], gidx=gx)
    k2 = _norm_mod_fwd(y1, n2[1:2], m1x[3], m1x[4], gx, "l1_norm2")
    u1 = _mm(k2, w["ffn_w_in"][1], "nn", F32, "l1_ffn_in")
    a1 = _swiglu_fwd(u1, "l1_swiglu")
    f1, y2 = _mm(a1, w["ffn_w_out"][1], "nn", F32, "l1_ffn_out", res=y1, gate=m1x[5], gidx=gx)

    loss_tile, dy2 = _loss_fwd_bwd(y2, target.reshape(NX, D_MODEL), "loss")

    zg = jnp.zeros((1, 1, D_MODEL), F32)
    dz, dgate5_1 = _gate_bwd(dy2, f1, m1x[5], gx, B, "l1_ffn_gate_bwd")
    gw_ffn_out1 = _mm(a1, dz, "tn", F32, "l1_ffn_out_dw")
    da = _mm(dz, w["ffn_w_out"][1], "nt", F32, "l1_ffn_out_dx")
    du = _swiglu_bwd(da, u1, "l1_swiglu_bwd")
    gw_ffn_in1 = _mm(k2, du, "tn", F32, "l1_ffn_in_dw")
    dk2 = _mm(du, w["ffn_w_in"][1], "nt", F32, "l1_ffn_in_dx")
    dy1, dsh3_1, dsc4_1, dn2_1 = _norm_mod_bwd(dk2, y1, n2[1:2], m1x[4], dy2, gx, B, "l1_norm2_bwd")
    dzo, dgate2_1 = _gate_bwd(dy1, mo1, m1x[2], gx, B, "l1_ret_gate_bwd")
    gw_ret_o = _mm(z1, dzo, "tn", F32, "l1_ret_out_dw")
    dz1 = _mm(dzo, w["ret_w_o"][0], "nt", F32, "l1_ret_out_dx")
    do_r, dg_r, dgn = _gated_out_bwd(dz1, of, ob, qkvg, gn, "l1_gated_out_bwd")
    dq_f, dk_f, dv_f, dkc_f, dvc_f, dlg_f = _ret_bwd(qk, qkvg, log_g, st_f, do_r, B, S, L, False, "l1_ret_fwd_bwd")
    dq_b, dk_b, dv_b, dkc_b, dvc_b, dlg_b = _ret_bwd(qk, qkvg, log_g, st_b, do_r, B, S, L, True, "l1_ret_rev_bwd")
    dqk_rot = jnp.concatenate([
        jnp.concatenate([dq_f + dq_b, dk_f + dk_b], axis=1),
        jnp.concatenate([jnp.zeros((NC, RET_HEADS * RET_QK_DIM), F32), dkc_f + dkc_b], axis=1)], axis=0)
    dqk_pre = _ret_rope(dqk_rot, rcos, rsin, tidx, True, MXU_DTYPE, "l1_rope_bwd")
    dvg = jnp.concatenate([
        jnp.concatenate([(dv_f + dv_b).astype(MXU_DTYPE), dg_r], axis=1),
        jnp.concatenate([(dvc_f + dvc_b).astype(MXU_DTYPE), jnp.zeros((NC, RET_VWIDTH), MXU_DTYPE)], axis=1)], axis=0)
    dqkvg = jnp.concatenate([dqk_pre, dvg], axis=1)
    gw_ret_qkvg = _mm(g1, dqkvg, "tn", F32, "l1_qkvg_dw")
    dg1 = _mm(dqkvg, w["ret_w_qkvg"][0], "nt", F32, "l1_qkvg_dx")
    dres1 = jnp.concatenate([dy1, jnp.zeros((NC, D_MODEL), F32)], axis=0)
    dx2, dsh0_1, dsc1_1, dn1_1 = _norm_mod_bwd(dg1, x2, n1[1:2], m1[1], dres1, gidx, G, "l1_norm1_bwd")
    dlg = jnp.stack([jnp.sum(dlg_f[:, :, 0, 0], axis=0), jnp.sum(dlg_b[:, :, 0, 0], axis=0)])
    d_decay = (dlg * jax.nn.sigmoid(-small["ret_decay_logit"].reshape(2, RET_HEADS))).reshape(1, 2, RET_HEADS)

    dz, dgate5_0 = _gate_bwd(dx2, f0, m0[5], gidx, G, "l0_ffn_gate_bwd")
    gw_ffn_out0 = _mm(a0, dz, "tn", F32, "l0_ffn_out_dw")
    da = _mm(dz, w["ffn_w_out"][0], "nt", F32, "l0_ffn_out_dx")
    du = _swiglu_bwd(da, u0, "l0_swiglu_bwd")
    gw_ffn_in0 = _mm(h2, du, "tn", F32, "l0_ffn_in_dw")
    dh2 = _mm(du, w["ffn_w_in"][0], "nt", F32, "l0_ffn_in_dx")
    dx1, dsh3_0, dsc4_0, dn2_0 = _norm_mod_bwd(dh2, x1, n2[0:1], m0[4], dx2, gidx, G, "l0_norm2_bwd")
    dzo, dgate2_0 = _gate_bwd(dx1, mo0, m0[2], gidx, G, "l0_attn_gate_bwd")
    gw_attn_o = _mm(o0, dzo, "tn", F32, "l0_attn_out_dw")
    do0 = _mm(dzo, w["attn_w_o"][0], "nt", MXU_DTYPE, "l0_attn_out_dx")
    do_x, do_c = _to_heads(do0[:NX], B, S, N_HEADS), _to_heads(do0[NX:], B, L, N_HEADS)
    dq_x, dk_x, dv_x, dkc1, dvc1, dsink_x = _attn_bwd(q_x, k_x, v_x, k_c, v_c, sink, do_x, "l0_attn_x_bwd")
    dq_c, dkc2, dvc2, dsink_c = _attn_bwd(q_c, None, None, k_c, v_c, sink, do_c, "l0_attn_c_bwd")
    dqk = jnp.concatenate([
        jnp.concatenate([_from_heads(dq_x), _from_heads(dk_x)], axis=1),
        jnp.concatenate([_from_heads(dq_c), _from_heads(dkc1 + dkc2)], axis=1)], axis=0)
    dvv = jnp.concatenate([_from_heads(dv_x), _from_heads(dvc1 + dvc2)], axis=0)
    dqkv, dgains = _attn_prep_bwd(dqk, dvv, qkv, gains, acos, asin, tidx, "l0_qk_prep_bwd")
    gw_attn_qkv = _mm(h1, dqkv, "tn", F32, "l0_qkv_dw")
    dh1 = _mm(dqkv, w["attn_w_qkv"][0], "nt", F32, "l0_qkv_dx")
    dx0, dsh0_0, dsc1_0, dn1_0 = _norm_mod_bwd(dh1, x0, n1[0:1], m0[1], dx1, gidx, G, "l0_norm1_bwd")

    dgains = dgains[:, 0, :HEAD_DIM] + dgains[:, 0, HEAD_DIM:]
    dsink = (dsink_x + dsink_c).reshape(N_KV_HEADS, 8, LANES)[:, :GQA_GROUP, 0].reshape(1, N_HEADS)
    grads_big = {
        "ffn_w_in": jnp.stack([gw_ffn_in0, gw_ffn_in1]),
        "ffn_w_out": jnp.stack([gw_ffn_out0, gw_ffn_out1]),
        "attn_w_qkv": gw_attn_qkv[None],
        "attn_w_o": gw_attn_o[None],
        "ret_w_qkvg": gw_ret_qkvg[None],
        "ret_gn_g": dgn,
        "ret_w_o": gw_ret_o[None],
    }
    grads_small = {
        "norm1_g": jnp.concatenate([dn1_0, dn1_1], axis=0),
        "norm2_g": jnp.concatenate([dn2_0, dn2_1], axis=0),
        "attn_q_norm": jnp.sum(dgains[:ATTN_Q_BLOCKS], axis=0)[None],
        "attn_k_norm": jnp.sum(dgains[ATTN_Q_BLOCKS:ATTN_QK_BLOCKS], axis=0)[None],
        "attn_sink": dsink,
        "ret_decay_logit": d_decay,
    }

    def pad_g(t):
        return jnp.concatenate([t, zg], axis=0)

    d0 = jnp.concatenate([dsh0_0, dsc1_0, dgate2_0, dsh3_0, dsc4_0, dgate5_0], axis=2)[:, 0]
    d1 = jnp.concatenate([dsh0_1, dsc1_1, pad_g(dgate2_1), pad_g(dsh3_1), pad_g(dsc4_1), pad_g(dgate5_1)],
                         axis=2)[:, 0]
    dmod_x = jnp.stack([d0[:B], d1[:B]], axis=1)
    dmod_c = jnp.stack([d0[B], d1[B]], axis=0)
    return loss_tile, dx0[:NX].reshape(B, S, D_MODEL), grads_big, grads_small, dmod_x, dmod_c


SMALL_NAMES = ("c_ctx", "ada_b", "norm1_g", "norm2_g", "attn_q_norm", "attn_k_norm", "attn_sink", "ret_decay_logit")
ADA_ROWS = 64


def _pack_small(d, rows):
    flat = jnp.concatenate([d[k].reshape(-1) for k in SMALL_NAMES])
    n = rows * LANES
    return jnp.pad(flat, (0, n - flat.shape[0])).reshape(rows, LANES)


def _unpack_small(packed, shapes):
    flat = packed.reshape(-1)
    out, off = {}, 0
    for k in SMALL_NAMES:
        n = math.prod(shapes[k])
        out[k] = flat[off:off + n].reshape(shapes[k])
        off += n
    return out


def kernel(x, c, ctx, c_ctx, ada_w, ada_b, norm1_g, norm2_g, ffn_w_in, ffn_w_out, attn_w_qkv, attn_q_norm, attn_k_norm, attn_sink, attn_w_o, ret_w_qkvg, ret_decay_logit, ret_gn_g, ret_w_o, loss_target, m_c_ctx, m_ada_w, m_ada_b, m_norm1_g, m_norm2_g, m_ffn_w_in, m_ffn_w_out, m_attn_w_qkv, m_attn_q_norm, m_attn_k_norm, m_attn_sink, m_attn_w_o, m_ret_w_qkvg, m_ret_decay_logit, m_ret_gn_g, m_ret_w_o, v_c_ctx, v_ada_w, v_ada_b, v_norm1_g, v_norm2_g, v_ffn_w_in, v_ffn_w_out, v_attn_w_qkv, v_attn_q_norm, v_attn_k_norm, v_attn_sink, v_attn_w_o, v_ret_w_qkvg, v_ret_decay_logit, v_ret_gn_g, v_ret_w_o):
    weights = dict(c_ctx=c_ctx, ada_w=ada_w, ada_b=ada_b, norm1_g=norm1_g, norm2_g=norm2_g, ffn_w_in=ffn_w_in,
                   ffn_w_out=ffn_w_out, attn_w_qkv=attn_w_qkv, attn_q_norm=attn_q_norm, attn_k_norm=attn_k_norm,
                   attn_sink=attn_sink, attn_w_o=attn_w_o, ret_w_qkvg=ret_w_qkvg, ret_decay_logit=ret_decay_logit,
                   ret_gn_g=ret_gn_g, ret_w_o=ret_w_o)
    mom1 = dict(c_ctx=m_c_ctx, ada_w=m_ada_w, ada_b=m_ada_b, norm1_g=m_norm1_g, norm2_g=m_norm2_g, ffn_w_in=m_ffn_w_in,
                ffn_w_out=m_ffn_w_out, attn_w_qkv=m_attn_w_qkv, attn_q_norm=m_attn_q_norm, attn_k_norm=m_attn_k_norm,
                attn_sink=m_attn_sink, attn_w_o=m_attn_w_o, ret_w_qkvg=m_ret_w_qkvg, ret_decay_logit=m_ret_decay_logit,
                ret_gn_g=m_ret_gn_g, ret_w_o=m_ret_w_o)
    mom2 = dict(c_ctx=v_c_ctx, ada_w=v_ada_w, ada_b=v_ada_b, norm1_g=v_norm1_g, norm2_g=v_norm2_g, ffn_w_in=v_ffn_w_in,
                ffn_w_out=v_ffn_w_out, attn_w_qkv=v_attn_w_qkv, attn_q_norm=v_attn_q_norm, attn_k_norm=v_attn_k_norm,
                attn_sink=v_attn_sink, attn_w_o=v_attn_w_o, ret_w_qkvg=v_ret_w_qkvg, ret_decay_logit=v_ret_decay_logit,
                ret_gn_g=v_ret_gn_g, ret_w_o=v_ret_w_o)
    B = x.shape[0]
    mx_, my_, mc_ = _my_coords()
    me = 4 * mx_ + 2 * my_ + mc_
    my_chip = 2 * mx_ + my_
    ada_cols = ada_w.shape[2]

    gathered = _unpack_shards(_all_gather(_pack_shards(weights, MXU_DTYPE), "gather_weights"))
    w_full = {k: _join_shards(k, gathered[k]) for k in BIG_WEIGHTS}
    w_full["ret_gn_g"] = _all_gather(ret_gn_g, "gather_gn_gain").reshape(1, RET_VWIDTH)

    c_all = _all_gather(jax.nn.silu(c), "gather_c").reshape(N_DEV * B, D_MODEL)
    cc_act = jax.nn.silu(c_ctx)[None]
    ada_in = jnp.concatenate([c_all, cc_act, jnp.zeros((ADA_ROWS - N_DEV * B - 1, D_MODEL), F32)], axis=0)
    ada_in = ada_in.astype(MXU_DTYPE)
    ada_w2 = jnp.concatenate([ada_w[0], ada_w[1]], axis=1)
    bias = lax.dynamic_slice_in_dim(ada_b.reshape(2, N_DEV, ada_cols), me, 1, axis=1).reshape(1, 2 * ada_cols)
    mod_cols = _mm(ada_in, ada_w2, "nn", F32, "ada_fwd", bias=bias)
    mod_all = _all_gather(mod_cols, "gather_mod")
    mod_all = mod_all.reshape(N_DEV, ADA_ROWS, 2, ada_cols).transpose(1, 2, 0, 3).reshape(ADA_ROWS, 2, N_DEV * ada_cols)
    mod_x = lax.dynamic_slice_in_dim(mod_all, me * B, B, axis=0)
    mod_c = mod_all[N_DEV * B]

    small = {k: weights[k] for k in SMALL_NAMES}
    loss_tile, grad_x, g_big, g_small, dmod_x, dmod_c = _local_step(x, ctx, loss_target, mod_x, mod_c, w_full, small)
    loss = lax.psum(loss_tile[0, 0], ("x", "y", "c"))

    n_mod = 2 * 6 * D_MODEL
    dm_rows = jnp.concatenate([dmod_x.reshape(B, n_mod), dmod_c.reshape(1, n_mod),
                               jnp.zeros((8 - B - 1, n_mod), F32)], axis=0)
    dm_all = _all_gather(dm_rows, "gather_dmod")
    dmc_tot = _sum_rows(dm_all[:, B:B + 1].reshape(N_DEV, 1, n_mod)[:, :, :].reshape(N_DEV, n_mod // LANES, LANES),
                        "sum_dmod_c").reshape(1, n_mod)
    dmod_rows = jnp.concatenate([dm_all[:, :B].reshape(N_DEV * B, n_mod), dmc_tot,
                                 jnp.zeros((ADA_ROWS - N_DEV * B - 1, n_mod), F32)], axis=0)
    dmod_mine = lax.dynamic_slice_in_dim(dmod_rows.reshape(ADA_ROWS, 2, N_DEV, ada_cols), me, 1, axis=2)
    dmod_mine = dmod_mine.reshape(ADA_ROWS, 2 * ada_cols).astype(MXU_DTYPE)
    g_ada2 = _mm(ada_in, dmod_mine, "tn", F32, "ada_dw")
    g_ada_w = jnp.stack([g_ada2[:, :ada_cols], g_ada2[:, ada_cols:]])
    dmc_mine = jnp.concatenate([dmod_mine[N_DEV * B:N_DEV * B + 1], jnp.zeros((7, 2 * ada_cols), MXU_DTYPE)], axis=0)
    dcc_part = _mm(dmc_mine, ada_w2, "nt", F32, "ada_dc")[0:1]
    g_ada_b = _sum_rows(dmod_rows[:, None, :].reshape(ADA_ROWS, n_mod // LANES, LANES), "sum_dmod_b").reshape(2, 6 * D_MODEL)
    sg = jax.nn.sigmoid(c_ctx)
    g_small["c_ctx"] = dcc_part.reshape(D_MODEL) * (sg * (1.0 + c_ctx * (1.0 - sg)))
    g_small["ada_b"] = g_ada_b * (1.0 / N_DEV)

    shapes = {k: weights[k].shape for k in SMALL_NAMES}
    n_small = sum(math.prod(s) for s in shapes.values())
    srows = -(-(-(-n_small // LANES)) // 8) * 8
    gs_all = _all_gather(_pack_small(g_small, srows), "gather_small_grads")
    sm = _adamw(_pack_small({k: weights[k] for k in SMALL_NAMES}, srows), _pack_small({k: mom1[k] for k in SMALL_NAMES}, srows),
                _pack_small({k: mom2[k] for k in SMALL_NAMES}, srows), [gs_all], "adamw_small")
    sm = [_unpack_small(t, shapes) for t in sm]

    ada_shape = ada_w.shape
    r2 = lambda t: t.reshape(ada_shape[0] * ada_shape[1], ada_shape[2])
    ada = [t.reshape(ada_shape) for t in _adamw(r2(ada_w), r2(m_ada_w), r2(v_ada_w), [r2(g_ada_w)[None]], "adamw_ada")]

    split = {k: _split_shards(k, g_big[k]) for k in BIG_WEIGHTS}
    packed = jnp.stack([_pack_shards({k: split[k][j] for k in BIG_WEIGHTS}, F32) for j in range(N_DEV)])
    packed = packed.reshape(4, 2, packed.shape[1], PACK_COLS)
    keep = lax.dynamic_index_in_dim(packed, mc_, axis=1, keepdims=False)
    give = lax.dynamic_index_in_dim(packed, 1 - mc_, axis=1, keepdims=False)
    from_sibling = _exchange(give.reshape(1, 4 * give.shape[1], PACK_COLS), [(0, 0, 1)], lambda px, py, pc: 0,
                             "rs_sibling").reshape(keep.shape)
    pair = _pair_add(keep, from_sibling, F32, "rs_pair_sum")
    from_chips = _exchange(pair, [(1, 0, 0), (0, 1, 0), (1, 1, 0)], lambda px, py, pc: 2 * px + py, "rs_chips")
    own = lax.dynamic_index_in_dim(pair, my_chip, axis=0, keepdims=True)
    big = _adamw(_pack_shards(weights, F32), _pack_shards(mom1, F32), _pack_shards(mom2, F32), [own, from_chips], "adamw_big")
    big = [_unpack_shards(t) for t in big]

    def pick(i, name):
        if name in BIG_WEIGHTS:
            return big[i][name]
        if name == "ada_w":
            return ada[i]
        return sm[i][name]

    order = ("c_ctx", "ada_w", "ada_b", "norm1_g", "norm2_g", "ffn_w_in", "ffn_w_out", "attn_w_qkv", "attn_q_norm",
             "attn_k_norm", "attn_sink", "attn_w_o", "ret_w_qkvg", "ret_decay_logit", "ret_gn_g", "ret_w_o")
    outs = [loss, grad_x]
    for i in range(4):
        outs += [pick(i, n) for n in order]
    return tuple(outs)
```

```python
import functools
import math

import jax
import jax.numpy as jnp
from jax import lax
from jax.experimental import pallas as pl
from jax.experimental.pallas import tpu as pltpu

F32 = jnp.float32
MXU_DTYPE = jnp.bfloat16

D_MODEL = 1024
HEAD_DIM = 64
N_HEADS = 16
N_KV_HEADS = 4
GQA_GROUP = 4
WINDOW = 128
ATTN_BLOCK = 128
RET_HEADS = 4
RET_QK_DIM = 256
RET_V_DIM = 512
RET_VWIDTH = 2048
RET_CHUNK = 128
D_FF = 2816
GRID_W = 64
ROPE_BASE = 10000.0
EPS = 1e-6
NEG_INF = -1e30

ADAM_LR = 0.001
ADAM_B1 = 0.9
ADAM_B2 = 0.999
ADAM_EPS = 1e-08
ADAM_WD = 0.01
ADAM_STEP = 10

N_DEV = 8
LANES = 128
ROW_TILE = 512
VMEM_LIMIT = 48 * 1024 * 1024

MESH = pl.DeviceIdType.MESH
_ANY = pl.BlockSpec(memory_space=pl.ANY)
_SMEM = pl.BlockSpec(memory_space=pltpu.SMEM)


def _params(**kw):
    return pltpu.CompilerParams(vmem_limit_bytes=VMEM_LIMIT, **kw)


def _mx(v):
    return v.astype(MXU_DTYPE)


def _dot(a, b, dims):
    return lax.dot_general(_mx(a), _mx(b), (dims, ((), ())), preferred_element_type=F32)


_NN = ((1,), (0,))
_NT = ((1,), (1,))
_TN = ((0,), (0,))


def _tile(n, cands):
    for c in cands:
        if n % c == 0:
            return c
    return n


def _big_tile(n, cap):
    if n <= cap:
        return n
    for t in range(cap - cap % LANES, 0, -LANES):
        if n % t == 0:
            return t
    return n


MM_ROWS = 1024
MM_COLS = 1408
MM_DEPTH = 2048


def _k_tile(k):
    return _big_tile(k, MM_DEPTH)


def _mm(a, b, mode, out_dtype, name, *, bias=None, res=None, gate=None, gidx_for=None, gate_rows=None):
    if mode == "nn":
        (M, K), (_, N) = a.shape, b.shape
    elif mode == "nt":
        (M, K), (N, _) = a.shape, b.shape
    else:
        (K, M), (_, N) = a.shape, b.shape
    if res is not None:
        tm, tn = gate_rows, _big_tile(N, 512)
        gidx = gidx_for(tm)
    else:
        tm = _big_tile(M, MM_COLS if mode == "tn" else MM_ROWS)
        tn = _big_tile(N, MM_COLS)
    tk = _k_tile(K)
    nk = K // tk
    dims = {"nn": _NN, "nt": _NT, "tn": _TN}[mode]
    a_spec = (pl.BlockSpec((tk, tm), lambda i, j, k: (k, i)) if mode == "tn"
              else pl.BlockSpec((tm, tk), lambda i, j, k: (i, k)))
    b_spec = (pl.BlockSpec((tn, tk), lambda i, j, k: (j, k)) if mode == "nt"
              else pl.BlockSpec((tk, tn), lambda i, j, k: (k, j)))
    o_spec = pl.BlockSpec((tm, tn), lambda i, j, k: (i, j))
    in_specs, operands = [a_spec, b_spec], [a, b]
    if bias is not None:
        in_specs.append(pl.BlockSpec((1, tn), lambda i, j, k: (0, j)))
        operands.append(bias)
    if res is not None:
        in_specs += [o_spec, pl.BlockSpec((1, 1, tn), lambda i, j, k: (gidx(i), 0, j))]
        operands += [res, gate]
        out_shape = (jax.ShapeDtypeStruct((M, N), F32), jax.ShapeDtypeStruct((M, N), F32))
        out_specs = (o_spec, o_spec)
    else:
        out_shape = jax.ShapeDtypeStruct((M, N), out_dtype)
        out_specs = o_spec

    def body(*refs):
        a_ref, b_ref = refs[0], refs[1]
        extra = refs[2:len(operands)]
        outs = refs[len(operands):]
        prod = _dot(a_ref[...], b_ref[...], dims)

        def finish(acc):
            if bias is not None:
                outs[0][...] = (acc + extra[0][...]).astype(out_dtype)
            elif res is not None:
                outs[0][...] = acc
                outs[1][...] = extra[0][...] + extra[1][0] * acc
            else:
                outs[0][...] = acc.astype(out_dtype)

        if nk == 1:
            finish(prod)
        else:
            acc_ref = outs[-1]
            outs = outs[:-1]
            k = pl.program_id(2)

            @pl.when(k == 0)
            def _():
                acc_ref[...] = prod

            @pl.when(k > 0)
            def _():
                acc_ref[...] += prod

            @pl.when(k == nk - 1)
            def _():
                finish(acc_ref[...])

    return pl.pallas_call(
        body, name=name, grid=(M // tm, N // tn, nk), in_specs=in_specs, out_specs=out_specs, out_shape=out_shape,
        scratch_shapes=[pltpu.VMEM((tm, tn), F32)] if nk > 1 else [],
        compiler_params=_params())(*operands)


def _group_index(n_x_tiles, tiles_per_example, n_examples):
    def gidx(i):
        return jnp.where(i < n_x_tiles, i // tiles_per_example, n_examples)
    return gidx


def _norm_mod_fwd(x, g, shift, scale, gidx, name):
    T, Dm = x.shape

    def body(x_ref, g_ref, sh_ref, sc_ref, h_ref):
        xv = x_ref[...]
        r = lax.rsqrt(jnp.mean(xv * xv, axis=-1, keepdims=True) + EPS)
        y = xv * r * g_ref[...]
        h_ref[...] = (y * (1.0 + sc_ref[0]) + sh_ref[0]).astype(h_ref.dtype)

    row = pl.BlockSpec((ROW_TILE, Dm), lambda i: (i, 0))
    mod = pl.BlockSpec((1, 1, Dm), lambda i: (gidx(i), 0, 0))
    return pl.pallas_call(
        body, name=name, grid=(T // ROW_TILE,),
        in_specs=[row, pl.BlockSpec((1, Dm), lambda i: (0, 0)), mod, mod],
        out_specs=row, out_shape=jax.ShapeDtypeStruct((T, Dm), MXU_DTYPE),
        compiler_params=_params())(x, g, shift, scale)


def _first_of_group(i, gidx):
    return jnp.logical_or(i == 0, gidx(i) != gidx(jnp.maximum(i - 1, 0)))


def _norm_mod_bwd(dh, x, g, scale, dres, gidx, n_groups, name):
    T, Dm = x.shape

    def body(dh_ref, x_ref, g_ref, sc_ref, dres_ref, dx_ref, dsh_ref, dsc_ref, dg_ref):
        i = pl.program_id(0)
        xv, dhv = x_ref[...], dh_ref[...]
        r = lax.rsqrt(jnp.mean(xv * xv, axis=-1, keepdims=True) + EPS)
        xn = xv * r
        y = xn * g_ref[...]

        @pl.when(_first_of_group(i, gidx))
        def _():
            dsh_ref[...] = jnp.zeros_like(dsh_ref)
            dsc_ref[...] = jnp.zeros_like(dsc_ref)

        @pl.when(i == 0)
        def _():
            dg_ref[...] = jnp.zeros_like(dg_ref)

        dsh_ref[0] += jnp.sum(dhv, axis=0, keepdims=True)
        dsc_ref[0] += jnp.sum(dhv * y, axis=0, keepdims=True)
        dy = dhv * (1.0 + sc_ref[0])
        dg_ref[...] += jnp.sum(dy * xn, axis=0, keepdims=True)
        dxn = dy * g_ref[...]
        dx = r * (dxn - xn * jnp.mean(dxn * xn, axis=-1, keepdims=True))
        dx_ref[...] = dres_ref[...] + dx

    row = pl.BlockSpec((ROW_TILE, Dm), lambda i: (i, 0))
    mod = pl.BlockSpec((1, 1, Dm), lambda i: (gidx(i), 0, 0))
    vec = pl.BlockSpec((1, Dm), lambda i: (0, 0))
    return pl.pallas_call(
        body, name=name, grid=(T // ROW_TILE,),
        in_specs=[row, row, vec, mod, row],
        out_specs=(row, mod, mod, vec),
        out_shape=(jax.ShapeDtypeStruct((T, Dm), F32), jax.ShapeDtypeStruct((n_groups, 1, Dm), F32),
                   jax.ShapeDtypeStruct((n_groups, 1, Dm), F32), jax.ShapeDtypeStruct((1, Dm), F32)),
        compiler_params=_params())(dh, x, g, scale, dres)


def _gate_bwd(dy, f, gate, gidx, n_groups, name):
    T, Dm = dy.shape

    def body(dy_ref, f_ref, gate_ref, dz_ref, dgate_ref):
        i = pl.program_id(0)
        dyv = dy_ref[...]

        @pl.when(_first_of_group(i, gidx))
        def _():
            dgate_ref[...] = jnp.zeros_like(dgate_ref)

        dgate_ref[0] += jnp.sum(dyv * f_ref[...], axis=0, keepdims=True)
        dz_ref[...] = (dyv * gate_ref[0]).astype(dz_ref.dtype)

    row = pl.BlockSpec((ROW_TILE, Dm), lambda i: (i, 0))
    mod = pl.BlockSpec((1, 1, Dm), lambda i: (gidx(i), 0, 0))
    return pl.pallas_call(
        body, name=name, grid=(T // ROW_TILE,), in_specs=[row, row, mod], out_specs=(row, mod),
        out_shape=(jax.ShapeDtypeStruct((T, Dm), MXU_DTYPE), jax.ShapeDtypeStruct((n_groups, 1, Dm), F32)),
        compiler_params=_params())(dy, f, gate)


SWIGLU_ROWS = 256


def _swiglu_fwd(u, name):
    T = u.shape[0]

    def body(u_ref, a_ref):
        gate, up = u_ref[:, :D_FF], u_ref[:, D_FF:]
        a_ref[...] = (gate * jax.nn.sigmoid(gate) * up).astype(a_ref.dtype)

    return pl.pallas_call(
        body, name=name, grid=(T // SWIGLU_ROWS,),
        in_specs=[pl.BlockSpec((SWIGLU_ROWS, 2 * D_FF), lambda i: (i, 0))],
        out_specs=pl.BlockSpec((SWIGLU_ROWS, D_FF), lambda i: (i, 0)),
        out_shape=jax.ShapeDtypeStruct((T, D_FF), MXU_DTYPE), compiler_params=_params())(u)


def _swiglu_bwd(da, u, name):
    T = u.shape[0]

    def body(da_ref, u_ref, du_ref):
        gate, up, dav = u_ref[:, :D_FF], u_ref[:, D_FF:], da_ref[...]
        sg = jax.nn.sigmoid(gate)
        du_ref[:, :D_FF] = (dav * up * (sg * (1.0 + gate * (1.0 - sg)))).astype(du_ref.dtype)
        du_ref[:, D_FF:] = (dav * gate * sg).astype(du_ref.dtype)

    return pl.pallas_call(
        body, name=name, grid=(T // SWIGLU_ROWS,),
        in_specs=[pl.BlockSpec((SWIGLU_ROWS, D_FF), lambda i: (i, 0)),
                  pl.BlockSpec((SWIGLU_ROWS, 2 * D_FF), lambda i: (i, 0))],
        out_specs=pl.BlockSpec((SWIGLU_ROWS, 2 * D_FF), lambda i: (i, 0)),
        out_shape=jax.ShapeDtypeStruct((T, 2 * D_FF), MXU_DTYPE), compiler_params=_params())(da, u)


def _loss_fwd_bwd(y, target, name):
    T, Dm = y.shape

    def body(y_ref, t_ref, loss_ref, dy_ref):
        err = y_ref[...] - t_ref[...]

        @pl.when(pl.program_id(0) == 0)
        def _():
            loss_ref[...] = jnp.zeros_like(loss_ref)

        loss_ref[...] += 0.5 * jnp.sum(jnp.mean(err * err, axis=-1, keepdims=True))
        dy_ref[...] = err * (1.0 / Dm)

    row = pl.BlockSpec((ROW_TILE, Dm), lambda i: (i, 0))
    return pl.pallas_call(
        body, name=name, grid=(T // ROW_TILE,), in_specs=[row, row],
        out_specs=(pl.BlockSpec((8, LANES), lambda i: (0, 0)), row),
        out_shape=(jax.ShapeDtypeStruct((8, LANES), F32), jax.ShapeDtypeStruct((T, Dm), F32)),
        compiler_params=_params())(y, target)


def _rope_tables(seq, head_dim):
    axis_dim = head_dim // 2
    half = axis_dim // 2
    pos = jnp.arange(seq, dtype=jnp.int32)
    row = (pos // GRID_W).astype(F32)[:, None]
    col = (pos % GRID_W).astype(F32)[:, None]
    inv = ROPE_BASE ** (-jnp.arange(0, axis_dim, 2, dtype=F32) / axis_dim)
    lane = jnp.arange(head_dim, dtype=jnp.int32)
    within = lane % axis_dim
    ang = jnp.where((lane // axis_dim == 0)[None, :], row, col) * inv[within % half][None, :]
    cos = jnp.cos(ang)
    sin = jnp.where((within < half)[None, :], -jnp.sin(ang), jnp.sin(ang))
    cos = jnp.concatenate([cos, jnp.ones((ROW_TILE, head_dim), F32)], axis=0)
    sin = jnp.concatenate([sin, jnp.zeros((ROW_TILE, head_dim), F32)], axis=0)
    return cos, sin


def _pair_swap(v, half):
    if 2 * half == LANES:
        return pltpu.roll(v, half, axis=1)
    lane = lax.broadcasted_iota(jnp.int32, v.shape, 1)
    return jnp.where((lane % (2 * half)) < half, pltpu.roll(v, LANES - half, axis=1), pltpu.roll(v, half, axis=1))


def _head_sum(v, ones_ref):
    hi = v.astype(MXU_DTYPE)
    lo = (v - hi.astype(F32)).astype(MXU_DTYPE)
    return (jnp.dot(hi, ones_ref[...], preferred_element_type=F32)
            + jnp.dot(lo, ones_ref[...], preferred_element_type=F32))


def _head_ones():
    lane = jnp.arange(LANES)
    return (lane[:, None] // HEAD_DIM == lane[None, :] // HEAD_DIM).astype(MXU_DTYPE)


ATTN_QK_BLOCKS = (N_HEADS + N_KV_HEADS) * HEAD_DIM // LANES
ATTN_ALL_BLOCKS = (N_HEADS + 2 * N_KV_HEADS) * HEAD_DIM // LANES
ATTN_Q_BLOCKS = N_HEADS * HEAD_DIM // LANES


def _attn_prep_fwd(qkv, gains, cos, sin, tidx, name):
    T = qkv.shape[0]

    def body(x_ref, g_ref, cos_ref, sin_ref, ones_ref, o_ref):
        cb = pl.program_id(1)

        @pl.when(cb < ATTN_QK_BLOCKS)
        def _():
            xv = x_ref[...]
            r = lax.rsqrt(_head_sum(xv * xv, ones_ref) * (1.0 / HEAD_DIM) + EPS)
            y = xv * r * g_ref[0]
            o_ref[...] = (y * cos_ref[...] + _pair_swap(y, HEAD_DIM // 4) * sin_ref[...]).astype(o_ref.dtype)

        @pl.when(cb >= ATTN_QK_BLOCKS)
        def _():
            o_ref[...] = x_ref[...].astype(o_ref.dtype)

    blk = pl.BlockSpec((ROW_TILE, LANES), lambda i, cb: (i, cb))
    tab = pl.BlockSpec((ROW_TILE, LANES), lambda i, cb: (tidx(i), 0))
    return pl.pallas_call(
        body, name=name, grid=(T // ROW_TILE, ATTN_ALL_BLOCKS),
        in_specs=[blk, pl.BlockSpec((1, 1, LANES), lambda i, cb: (jnp.where(cb < ATTN_Q_BLOCKS, 0, 1), 0, 0)),
                  tab, tab, pl.BlockSpec((LANES, LANES), lambda i, cb: (0, 0))],
        out_specs=blk, out_shape=jax.ShapeDtypeStruct(qkv.shape, MXU_DTYPE),
        compiler_params=_params())(qkv, gains, cos, sin, _head_ones())


def _attn_prep_bwd(dqk, dv, qkv, gains, cos, sin, tidx, name):
    T = qkv.shape[0]
    nt = T // ROW_TILE

    def body(dqk_ref, dv_ref, x_ref, g_ref, cos_ref, sin_ref, ones_ref, o_ref, dg_ref):
        cb, i = pl.program_id(0), pl.program_id(1)

        @pl.when(i == 0)
        def _():
            dg_ref[...] = jnp.zeros_like(dg_ref)

        @pl.when(cb < ATTN_QK_BLOCKS)
        def _():
            xv, d = x_ref[...], dqk_ref[...]
            r = lax.rsqrt(_head_sum(xv * xv, ones_ref) * (1.0 / HEAD_DIM) + EPS)
            xn = xv * r
            dy = d * cos_ref[...] + _pair_swap(d * sin_ref[...], HEAD_DIM // 4)
            dg_ref[0] += jnp.sum(dy * xn, axis=0, keepdims=True)
            dxn = dy * g_ref[0]
            dx = r * (dxn - xn * (_head_sum(dxn * xn, ones_ref) * (1.0 / HEAD_DIM)))
            o_ref[...] = dx.astype(o_ref.dtype)

        @pl.when(cb >= ATTN_QK_BLOCKS)
        def _():
            o_ref[...] = dv_ref[...].astype(o_ref.dtype)

    blk = pl.BlockSpec((ROW_TILE, LANES), lambda cb, i: (i, cb))
    tab = pl.BlockSpec((ROW_TILE, LANES), lambda cb, i: (tidx(i), 0))
    return pl.pallas_call(
        body, name=name, grid=(ATTN_ALL_BLOCKS, nt),
        in_specs=[pl.BlockSpec((ROW_TILE, LANES), lambda cb, i: (i, jnp.minimum(cb, ATTN_QK_BLOCKS - 1))),
                  pl.BlockSpec((ROW_TILE, LANES), lambda cb, i: (i, jnp.maximum(cb - ATTN_QK_BLOCKS, 0))),
                  blk, pl.BlockSpec((1, 1, LANES), lambda cb, i: (jnp.where(cb < ATTN_Q_BLOCKS, 0, 1), 0, 0)),
                  tab, tab, pl.BlockSpec((LANES, LANES), lambda cb, i: (0, 0))],
        out_specs=(blk, pl.BlockSpec((1, 1, LANES), lambda cb, i: (cb, 0, 0))),
        out_shape=(jax.ShapeDtypeStruct(qkv.shape, MXU_DTYPE), jax.ShapeDtypeStruct((ATTN_ALL_BLOCKS, 1, LANES), F32)),
        compiler_params=_params())(dqk, dv, qkv, gains, cos, sin, _head_ones())


RET_QK_BLOCKS = 2 * RET_HEADS * RET_QK_DIM // LANES


def _ret_rope(x, cos, sin, tidx, backward, out_dtype, name):
    T = x.shape[0]
    k_scale = RET_QK_DIM ** -0.5

    def body(x_ref, cos_ref, sin_ref, o_ref):
        xv = x_ref[...]
        scale = jnp.where(pl.program_id(1) < RET_QK_BLOCKS // 2, 1.0, k_scale)
        if backward:
            out = xv * cos_ref[...] + pltpu.roll(xv * sin_ref[...], LANES // 2, axis=1)
        else:
            out = xv * cos_ref[...] + pltpu.roll(xv, LANES // 2, axis=1) * sin_ref[...]
        o_ref[...] = (out * scale).astype(o_ref.dtype)

    blk = pl.BlockSpec((ROW_TILE, LANES), lambda i, cb: (i, cb))
    tab = pl.BlockSpec((ROW_TILE, LANES), lambda i, cb: (tidx(i), cb % 2))
    return pl.pallas_call(
        body, name=name, grid=(T // ROW_TILE, RET_QK_BLOCKS), in_specs=[blk, tab, tab], out_specs=blk,
        out_shape=jax.ShapeDtypeStruct((T, RET_QK_BLOCKS * LANES), out_dtype), compiler_params=_params())(x, cos, sin)


def _attn_probs(qv, kcv, klv, sink_ref, kv_head, j, qb, seq):
    scale = HEAD_DIM ** -0.5
    rows = GQA_GROUP * qb
    s_c = _dot(qv, kcv, _NT) * scale
    g = lax.broadcasted_iota(jnp.int32, (rows, 1), 0) // qb
    sink = jnp.zeros((rows, 1), F32)
    for gi in range(GQA_GROUP):
        sink = jnp.where(g == gi, sink_ref[kv_head, gi], sink)
    m = jnp.maximum(jnp.max(s_c, axis=-1, keepdims=True), sink)
    s_l = None
    if klv is not None:
        band = klv.shape[0]
        s_l = _dot(qv, klv, _NT) * scale
        i = lax.broadcasted_iota(jnp.int32, (rows, band), 0) % qb
        n = lax.broadcasted_iota(jnp.int32, (rows, band), 1)
        key_pos = j * ATTN_BLOCK - WINDOW + n
        valid = (n >= i) & (n - i <= 2 * WINDOW) & (key_pos >= 0) & (key_pos < seq)
        s_l = jnp.where(valid, s_l, NEG_INF)
        m = jnp.maximum(m, jnp.max(s_l, axis=-1, keepdims=True))
    e_c = jnp.exp(s_c - m)
    e_s = jnp.exp(sink - m)
    den = jnp.sum(e_c, axis=-1, keepdims=True) + e_s
    e_l = None
    if klv is not None:
        e_l = jnp.exp(s_l - m)
        den = den + jnp.sum(e_l, axis=-1, keepdims=True)
    inv = 1.0 / den
    return e_c * inv, (None if e_l is None else e_l * inv), e_s * inv


def _attn_specs(B, seq, ctx_len, qb, has_local):
    nb = seq // qb
    q_spec = pl.BlockSpec((1, GQA_GROUP, qb, HEAD_DIM), lambda b, k, j: (b, k, j, 0))
    c_spec = pl.BlockSpec((1, 1, ctx_len, HEAD_DIM), lambda b, k, j: (b, k, 0, 0))
    local = []
    if has_local:
        local = [pl.BlockSpec((1, 1, qb, HEAD_DIM), lambda b, k, j: (b, k, jnp.maximum(j - 1, 0), 0)),
                 pl.BlockSpec((1, 1, qb, HEAD_DIM), lambda b, k, j: (b, k, j, 0)),
                 pl.BlockSpec((1, 1, qb, HEAD_DIM), lambda b, k, j: (b, k, jnp.minimum(j + 1, nb - 1), 0))]
    return nb, q_spec, c_spec, local


def _attn_fwd(q, k, v, kc, vc, sink, name):
    B, _, seq, _ = q.shape
    ctx_len = kc.shape[2]
    has_local = k is not None
    qb = ATTN_BLOCK if has_local else seq
    nb, q_spec, c_spec, local = _attn_specs(B, seq, ctx_len, qb, has_local)

    def body(*refs):
        q_ref = refs[0]
        kc_ref, vc_ref, sink_ref, o_ref = refs[-4:]
        kv_head, j = pl.program_id(1), pl.program_id(2)
        qv = q_ref[0].reshape(GQA_GROUP * qb, HEAD_DIM)
        klv = vlv = None
        if has_local:
            klv = jnp.concatenate([r[0, 0] for r in refs[1:4]], axis=0)
            vlv = jnp.concatenate([r[0, 0] for r in refs[4:7]], axis=0)
        p_c, p_l, _ = _attn_probs(qv, kc_ref[0, 0], klv, sink_ref, kv_head, j, qb, seq)
        o = _dot(p_c, vc_ref[0, 0], _NN)
        if has_local:
            o = o + _dot(p_l, vlv, _NN)
        o_ref[0] = o.reshape(GQA_GROUP, qb, HEAD_DIM)

    operands = [q] + ([k, k, k, v, v, v] if has_local else []) + [kc, vc, sink]
    return pl.pallas_call(
        body, name=name, grid=(B, N_KV_HEADS, nb),
        in_specs=[q_spec] + local + local + [c_spec, c_spec, _SMEM],
        out_specs=q_spec, out_shape=jax.ShapeDtypeStruct(q.shape, F32), compiler_params=_params())(*operands)


def _attn_bwd(q, k, v, kc, vc, sink, do, name):
    B, _, seq, _ = q.shape
    ctx_len = kc.shape[2]
    has_local = k is not None
    qb = ATTN_BLOCK if has_local else seq
    nb, q_spec, c_spec, local = _attn_specs(B, seq, ctx_len, qb, has_local)
    scale = HEAD_DIM ** -0.5

    def body(*refs):
        n_in = 1 + (6 if has_local else 0) + 4
        q_ref = refs[0]
        kc_ref, vc_ref, sink_ref, do_ref = refs[n_in - 4:n_in]
        outs = refs[n_in:]
        dq_ref = outs[0]
        dkc_ref, dvc_ref, dsink_ref = outs[-3:]
        b, kv_head, j = pl.program_id(0), pl.program_id(1), pl.program_id(2)
        rows = GQA_GROUP * qb
        qv = q_ref[0].reshape(rows, HEAD_DIM)
        dov = do_ref[0].reshape(rows, HEAD_DIM)
        kcv, vcv = kc_ref[0, 0], vc_ref[0, 0]
        klv = vlv = None
        if has_local:
            klv = jnp.concatenate([r[0, 0] for r in refs[1:4]], axis=0)
            vlv = jnp.concatenate([r[0, 0] for r in refs[4:7]], axis=0)
        p_c, p_l, p_s = _attn_probs(qv, kcv, klv, sink_ref, kv_head, j, qb, seq)
        dp_c = _dot(dov, vcv, _NT)
        delta = jnp.sum(p_c * dp_c, axis=-1, keepdims=True)
        if has_local:
            dp_l = _dot(dov, vlv, _NT)
            delta = delta + jnp.sum(p_l * dp_l, axis=-1, keepdims=True)
        ds_c = p_c * (dp_c - delta) * scale
        dq = _dot(ds_c, kcv, _NN)

        @pl.when(j == 0)
        def _():
            dkc_ref[...] = jnp.zeros_like(dkc_ref)
            dvc_ref[...] = jnp.zeros_like(dvc_ref)
            if has_local:
                outs[1][...] = jnp.zeros_like(outs[1])
                outs[2][...] = jnp.zeros_like(outs[2])

        @pl.when((b == 0) & (kv_head == 0) & (j == 0))
        def _():
            dsink_ref[...] = jnp.zeros_like(dsink_ref)

        dkc_ref[0, 0] += _dot(ds_c, qv, _TN)
        dvc_ref[0, 0] += _dot(p_c, dov, _TN)
        if has_local:
            ds_l = p_l * (dp_l - delta) * scale
            dq = dq + _dot(ds_l, klv, _NN)
            dkl = _dot(ds_l, qv, _TN)
            dvl = _dot(p_l, dov, _TN)
            dk_ref, dv_ref = outs[1], outs[2]
            for t in range(3):
                def add(t=t):
                    start = pl.multiple_of((j - 1 + t) * qb, qb)
                    dk_ref[0, 0, pl.ds(start, qb), :] += dkl[t * qb:(t + 1) * qb]
                    dv_ref[0, 0, pl.ds(start, qb), :] += dvl[t * qb:(t + 1) * qb]
                if t == 0:
                    pl.when(j > 0)(add)
                elif t == 2:
                    pl.when(j < nb - 1)(add)
                else:
                    add()
        dq_ref[0] = dq.reshape(GQA_GROUP, qb, HEAD_DIM)
        dsk = -(p_s * delta)
        sub = lax.broadcasted_iota(jnp.int32, (8, LANES), 0)
        tile = jnp.zeros((8, LANES), F32)
        for gi in range(GQA_GROUP):
            tile = jnp.where(sub == gi, jnp.sum(dsk[gi * qb:(gi + 1) * qb]), tile)
        dsink_ref[pl.ds(pl.multiple_of(kv_head * 8, 8), 8), :] += tile

    full = pl.BlockSpec((1, 1, seq, HEAD_DIM), lambda b, k, j: (b, k, 0, 0))
    operands = [q] + ([k, k, k, v, v, v] if has_local else []) + [kc, vc, sink, do]
    out_specs = [q_spec] + ([full, full] if has_local else []) + [c_spec, c_spec,
                                                                  pl.BlockSpec((32, LANES), lambda b, k, j: (0, 0))]
    out_shape = ([jax.ShapeDtypeStruct(q.shape, F32)]
                 + ([jax.ShapeDtypeStruct(k.shape, F32)] * 2 if has_local else [])
                 + [jax.ShapeDtypeStruct(kc.shape, F32)] * 2 + [jax.ShapeDtypeStruct((32, LANES), F32)])
    return pl.pallas_call(
        body, name=name, grid=(B, N_KV_HEADS, nb),
        in_specs=[q_spec] + local + local + [c_spec, c_spec, _SMEM, q_spec],
        out_specs=tuple(out_specs), out_shape=tuple(out_shape), compiler_params=_params())(*operands)


def _ret_decays(lg, rev):
    n = lax.broadcasted_iota(jnp.int32, (RET_CHUNK, RET_CHUNK), 0).astype(F32)
    m = lax.broadcasted_iota(jnp.int32, (RET_CHUNK, RET_CHUNK), 1).astype(F32)
    pos = lax.broadcasted_iota(jnp.int32, (RET_CHUNK, 1), 0).astype(F32)
    diff = (m - n) if rev else (n - m)
    a_exp = jnp.maximum(diff, 0.0)
    intra = jnp.where(diff >= 0, jnp.exp(lg * a_exp), 0.0)
    q_exp = (RET_CHUNK - pos) if rev else (pos + 1.0)
    k_exp = pos if rev else (RET_CHUNK - 1.0 - pos)
    chunk = jnp.exp(jnp.full((1, 1), RET_CHUNK, F32) * lg)
    return intra, a_exp, jnp.exp(lg * q_exp), q_exp, jnp.exp(lg * k_exp), k_exp, chunk


def _ctx_decay(lg, ctx_len, rev):
    t = lax.broadcasted_iota(jnp.int32, (ctx_len, 1), 0).astype(F32)
    expo = t if rev else (ctx_len - 1.0 - t)
    return jnp.exp(lg * expo), expo


def _ret_specs(B, seq, ctx_len, order):
    nc = seq // RET_CHUNK
    x_blocks = B * seq // ctx_len

    def rows(b, c):
        return b * nc + order(c, nc)

    q_spec = pl.BlockSpec((RET_CHUNK, RET_QK_DIM), lambda b, h, c: (rows(b, c), h))
    k_spec = pl.BlockSpec((RET_CHUNK, RET_QK_DIM), lambda b, h, c: (rows(b, c), RET_HEADS + h))
    v_spec = pl.BlockSpec((RET_CHUNK, RET_V_DIM), lambda b, h, c: (rows(b, c), RET_HEADS + h))
    kc_spec = pl.BlockSpec((ctx_len, RET_QK_DIM), lambda b, h, c: (x_blocks + b, RET_HEADS + h))
    vc_spec = pl.BlockSpec((ctx_len, RET_V_DIM), lambda b, h, c: (x_blocks + b, RET_HEADS + h))
    st_spec = pl.BlockSpec((1, 1, 1, RET_QK_DIM, RET_V_DIM), lambda b, h, c: (b, h, order(c, nc), 0, 0))
    o_spec = pl.BlockSpec((RET_CHUNK, RET_V_DIM), lambda b, h, c: (rows(b, c), h))
    return nc, q_spec, k_spec, v_spec, kc_spec, vc_spec, st_spec, o_spec


def _ret_fwd(qk, qkvg, log_g, B, seq, ctx_len, rev, name):
    direction = 1 if rev else 0
    order = (lambda c, nc: nc - 1 - c) if rev else (lambda c, nc: c)
    nc, q_spec, k_spec, v_spec, kc_spec, vc_spec, st_spec, o_spec = _ret_specs(B, seq, ctx_len, order)

    def body(lg_ref, q_ref, k_ref, v_ref, kc_ref, vc_ref, o_ref, st_ref, state):
        h, c = pl.program_id(1), pl.program_id(2)
        lg = lg_ref[direction, h]
        intra, _, q_dec, _, k_dec, _, chunk_dec = _ret_decays(lg, rev)

        @pl.when(c == 0)
        def _():
            dec, _ = _ctx_decay(lg, ctx_len, rev)
            state[...] = _dot(kc_ref[...] * dec, vc_ref[...], _TN)

        qv, kv, vv = q_ref[...], k_ref[...], v_ref[...]
        s_in = state[...]
        st_ref[0, 0, 0] = s_in
        w = _dot(qv, kv, _NT) * intra
        o_ref[...] = _dot(w, vv, _NN) + _dot(qv, s_in, _NN) * q_dec
        state[...] = s_in * chunk_dec + _dot(kv * k_dec, vv, _TN)

    return pl.pallas_call(
        body, name=name, grid=(B, RET_HEADS, nc),
        in_specs=[_SMEM, q_spec, k_spec, v_spec, kc_spec, vc_spec],
        out_specs=(o_spec, st_spec),
        out_shape=(jax.ShapeDtypeStruct((B * seq, RET_VWIDTH), F32),
                   jax.ShapeDtypeStruct((B, RET_HEADS, nc, RET_QK_DIM, RET_V_DIM), F32)),
        scratch_shapes=[pltpu.VMEM((RET_QK_DIM, RET_V_DIM), F32)],
        compiler_params=_params())(log_g, qk, qk, qkvg, qk, qkvg)


def _ret_bwd(qk, qkvg, log_g, states, do, B, seq, ctx_len, rev, name):
    direction = 1 if rev else 0
    order = (lambda c, nc: c) if rev else (lambda c, nc: nc - 1 - c)
    nc, q_spec, k_spec, v_spec, kc_spec, vc_spec, st_spec, o_spec = _ret_specs(B, seq, ctx_len, order)

    def body(lg_ref, q_ref, k_ref, v_ref, kc_ref, vc_ref, st_ref, do_ref,
             dq_ref, dk_ref, dv_ref, dkc_ref, dvc_ref, dlg_ref, dstate):
        h, c = pl.program_id(1), pl.program_id(2)
        lg = lg_ref[direction, h]
        intra, a_exp, q_dec, q_exp, k_dec, k_exp, chunk_dec = _ret_decays(lg, rev)

        @pl.when(c == 0)
        def _():
            dstate[...] = jnp.zeros_like(dstate)
            dlg_ref[...] = jnp.zeros_like(dlg_ref)

        qv, kv, vv, dov = q_ref[...], k_ref[...], v_ref[...], do_ref[...]
        s_in, ds_out = st_ref[0, 0, 0], dstate[...]
        p = _dot(qv, kv, _NT)
        w = p * intra
        dw = _dot(dov, vv, _NT)
        dp = dw * intra
        do_dec = dov * q_dec
        kd = kv * k_dec
        v_ds = _dot(vv, ds_out, _NT)
        dq_ref[...] = _dot(dp, kv, _NN) + _dot(do_dec, s_in, _NT)
        dk_ref[...] = _dot(dp, qv, _TN) + v_ds * k_dec
        dv_ref[...] = _dot(w, dov, _TN) + _dot(kd, ds_out, _NN)
        q_s = _dot(qv, s_in, _NN)
        dlg = (jnp.sum(dw * w * a_exp)
               + jnp.sum(q_exp * q_dec * jnp.sum(dov * q_s, axis=-1, keepdims=True))
               + jnp.sum(k_exp * k_dec * jnp.sum(kv * v_ds, axis=-1, keepdims=True))
               + RET_CHUNK * jnp.sum(chunk_dec * (ds_out * s_in)))
        ds_in = ds_out * chunk_dec + _dot(qv, do_dec, _TN)
        dstate[...] = ds_in
        dlg_ref[...] += dlg

        @pl.when(c == nc - 1)
        def _():
            dec, expo = _ctx_decay(lg, ctx_len, rev)
            kcv, vcv = kc_ref[...], vc_ref[...]
            vc_ds = _dot(vcv, ds_in, _NT)
            dkc_ref[...] = vc_ds * dec
            dvc_ref[...] = _dot(kcv * dec, ds_in, _NN)
            dlg_ref[...] += jnp.sum(expo * dec * jnp.sum(kcv * vc_ds, axis=-1, keepdims=True))

    dq_spec = pl.BlockSpec((RET_CHUNK, RET_QK_DIM), q_spec.index_map)
    return pl.pallas_call(
        body, name=name, grid=(B, RET_HEADS, nc),
        in_specs=[_SMEM, q_spec, k_spec, v_spec, kc_spec, vc_spec, st_spec, o_spec],
        out_specs=(dq_spec, dq_spec, o_spec,
                   pl.BlockSpec((ctx_len, RET_QK_DIM), lambda b, h, c: (b, h)),
                   pl.BlockSpec((ctx_len, RET_V_DIM), lambda b, h, c: (b, h)),
                   pl.BlockSpec((1, 1, 8, LANES), lambda b, h, c: (b, h, 0, 0))),
        out_shape=(jax.ShapeDtypeStruct((B * seq, RET_HEADS * RET_QK_DIM), F32),
                   jax.ShapeDtypeStruct((B * seq, RET_HEADS * RET_QK_DIM), F32),
                   jax.ShapeDtypeStruct((B * seq, RET_VWIDTH), F32),
                   jax.ShapeDtypeStruct((B * ctx_len, RET_HEADS * RET_QK_DIM), F32),
                   jax.ShapeDtypeStruct((B * ctx_len, RET_VWIDTH), F32),
                   jax.ShapeDtypeStruct((B, RET_HEADS, 8, LANES), F32)),
        scratch_shapes=[pltpu.VMEM((RET_QK_DIM, RET_V_DIM), F32)],
        compiler_params=_params())(log_g, qk, qk, qkvg, qk, qkvg, states, do)


def _gated_out_fwd(o_f, o_b, qkvg, gn_gain, name):
    T = o_f.shape[0]
    g_off = (2 * RET_HEADS * RET_QK_DIM + RET_VWIDTH) // RET_V_DIM

    def body(of_ref, ob_ref, g_ref, gain_ref, z_ref):
        o = of_ref[...] + ob_ref[...]
        mu = jnp.mean(o, axis=-1, keepdims=True)
        var = jnp.mean(jnp.square(o - mu), axis=-1, keepdims=True)
        y = (o - mu) * lax.rsqrt(var + EPS) * gain_ref[...]
        gv = g_ref[...]
        z_ref[...] = (gv * jax.nn.sigmoid(gv) * y).astype(z_ref.dtype)

    blk = pl.BlockSpec((ROW_TILE, RET_V_DIM), lambda i, h: (i, h))
    return pl.pallas_call(
        body, name=name, grid=(T // ROW_TILE, RET_HEADS),
        in_specs=[blk, blk, pl.BlockSpec((ROW_TILE, RET_V_DIM), lambda i, h: (i, g_off + h)),
                  pl.BlockSpec((1, RET_V_DIM), lambda i, h: (0, h))],
        out_specs=blk, out_shape=jax.ShapeDtypeStruct((T, RET_VWIDTH), MXU_DTYPE),
        compiler_params=_params())(o_f, o_b, qkvg, gn_gain)


def _gated_out_bwd(dz, o_f, o_b, qkvg, gn_gain, name):
    T = o_f.shape[0]
    g_off = (2 * RET_HEADS * RET_QK_DIM + RET_VWIDTH) // RET_V_DIM

    def body(dz_ref, of_ref, ob_ref, g_ref, gain_ref, do_ref, dg_ref, dgain_ref):
        o = of_ref[...] + ob_ref[...]
        mu = jnp.mean(o, axis=-1, keepdims=True)
        var = jnp.mean(jnp.square(o - mu), axis=-1, keepdims=True)
        rstd = lax.rsqrt(var + EPS)
        yhat = (o - mu) * rstd
        gv, dzv = g_ref[...], dz_ref[...]
        sg = jax.nn.sigmoid(gv)
        dg_ref[...] = (dzv * (yhat * gain_ref[...]) * (sg * (1.0 + gv * (1.0 - sg)))).astype(dg_ref.dtype)
        dy = dzv * (gv * sg)

        @pl.when(pl.program_id(1) == 0)
        def _():
            dgain_ref[...] = jnp.zeros_like(dgain_ref)

        dgain_ref[...] += jnp.sum(dy * yhat, axis=0, keepdims=True)
        dyh = dy * gain_ref[...]
        do_ref[...] = rstd * (dyh - jnp.mean(dyh, axis=-1, keepdims=True)
                              - yhat * jnp.mean(dyh * yhat, axis=-1, keepdims=True))

    blk = pl.BlockSpec((ROW_TILE, RET_V_DIM), lambda h, i: (i, h))
    vec = pl.BlockSpec((1, RET_V_DIM), lambda h, i: (0, h))
    return pl.pallas_call(
        body, name=name, grid=(RET_HEADS, T // ROW_TILE),
        in_specs=[blk, blk, blk, pl.BlockSpec((ROW_TILE, RET_V_DIM), lambda h, i: (i, g_off + h)), vec],
        out_specs=(blk, blk, vec),
        out_shape=(jax.ShapeDtypeStruct((T, RET_VWIDTH), F32), jax.ShapeDtypeStruct((T, RET_VWIDTH), MXU_DTYPE),
                   jax.ShapeDtypeStruct((1, RET_VWIDTH), F32)),
        compiler_params=_params())(dz, o_f, o_b, qkvg, gn_gain)


def _adamw(w, m, v, parts, name):
    R, C = w.shape
    tr = _tile(R, (256, 128, 64, 32, 16, 8))
    n_parts = [p.shape[0] for p in parts]

    def body(*refs):
        w_ref, m_ref, v_ref = refs[:3]
        part_refs = refs[3:3 + len(parts)]
        g_ref, d_ref, nm_ref, nv_ref = refs[3 + len(parts):]
        g = None
        for ref, n in zip(part_refs, n_parts):
            for r in range(n):
                term = ref[r].astype(F32)
                g = term if g is None else g + term
        mn = ADAM_B1 * m_ref[...] + (1.0 - ADAM_B1) * g
        vn = ADAM_B2 * v_ref[...] + (1.0 - ADAM_B2) * jnp.square(g)
        m_hat = mn / (1.0 - ADAM_B1 ** ADAM_STEP)
        v_hat = vn / (1.0 - ADAM_B2 ** ADAM_STEP)
        g_ref[...] = g
        d_ref[...] = -ADAM_LR * (m_hat / (jnp.sqrt(v_hat) + ADAM_EPS) + ADAM_WD * w_ref[...])
        nm_ref[...] = mn
        nv_ref[...] = vn

    blk = pl.BlockSpec((tr, C), lambda i: (i, 0))
    part_specs = [pl.BlockSpec((n, tr, C), lambda i: (0, i, 0)) for n in n_parts]
    shp = jax.ShapeDtypeStruct((R, C), F32)
    return pl.pallas_call(
        body, name=name, grid=(R // tr,), in_specs=[blk, blk, blk] + part_specs,
        out_specs=(blk, blk, blk, blk), out_shape=(shp, shp, shp, shp),
        compiler_params=_params())(w, m, v, *parts)


def _sum_rows(parts, name):
    n, R, C = parts.shape
    tr = _tile(R, (256, 128, 64, 32, 16, 8))

    def body(p_ref, o_ref):
        acc = p_ref[0]
        for r in range(1, n):
            acc = acc + p_ref[r]
        o_ref[...] = acc

    return pl.pallas_call(
        body, name=name, grid=(R // tr,), in_specs=[pl.BlockSpec((n, tr, C), lambda i: (0, i, 0))],
        out_specs=pl.BlockSpec((tr, C), lambda i: (i, 0)), out_shape=jax.ShapeDtypeStruct((R, C), F32),
        compiler_params=_params())(parts)


def _my_coords():
    return lax.axis_index("x"), lax.axis_index("y"), lax.axis_index("c")


def _flip(coord, bit):
    return 1 - coord if bit else coord


def _all_gather(x2d, name):
    R, C = x2d.shape

    def body(x_ref, out_ref, send_sems, recv_sems, local_sem):
        x, y, c = _my_coords()
        me, sibling = (x, y, c), (x, y, 1 - c)
        chips = [(1 - x, y), (x, 1 - y), (1 - x, 1 - y)]

        def rows(px, py, pc):
            return out_ref.at[4 * px + 2 * py + pc]

        def copy(k, block, to, src=None):
            return pltpu.make_async_remote_copy(
                src_ref=rows(*block) if src is None else src, dst_ref=rows(*block),
                send_sem=send_sems.at[k], recv_sem=recv_sems.at[k], device_id=to, device_id_type=MESH)

        mine = pltpu.make_async_copy(x_ref, rows(*me), local_sem)
        mine.start()
        first = [copy(0, me, sibling, src=x_ref)]
        first += [copy(1 + j, me, (*chip, c), src=x_ref) for j, chip in enumerate(chips)]
        for cp in first:
            cp.start()
        passed = [copy(4 + j, (*chip, c), sibling) for j, chip in enumerate(chips)]
        for j, chip in enumerate(chips):
            copy(1 + j, (*chip, c), me).wait_recv()
            passed[j].start()
        copy(0, sibling, me).wait_recv()
        for j, chip in enumerate(chips):
            copy(4 + j, (*chip, 1 - c), me).wait_recv()
        for cp in first + passed:
            cp.wait_send()
        mine.wait()

    return pl.pallas_call(
        body, name=name, out_shape=jax.ShapeDtypeStruct((N_DEV, R, C), x2d.dtype),
        in_specs=[_ANY], out_specs=_ANY,
        scratch_shapes=[pltpu.SemaphoreType.DMA((7,)), pltpu.SemaphoreType.DMA((7,)), pltpu.SemaphoreType.DMA],
    )(x2d)


BIG_WEIGHTS = {
    "ffn_w_in": (2, (2, D_MODEL, 2 * D_FF)),
    "ffn_w_out": (1, (2, D_FF, D_MODEL)),
    "attn_w_qkv": (2, (1, D_MODEL, (N_HEADS + 2 * N_KV_HEADS) * HEAD_DIM)),
    "attn_w_o": (1, (1, N_HEADS * HEAD_DIM, D_MODEL)),
    "ret_w_qkvg": (2, (1, D_MODEL, 2 * D_MODEL + 2 * RET_VWIDTH)),
    "ret_gn_g": (2, (1, 1, RET_VWIDTH)),
    "ret_w_o": (1, (1, RET_VWIDTH, D_MODEL)),
}


def _join_shards(name, stacked):
    axis, full = BIG_WEIGHTS[name]
    if axis == 2:
        stacked = stacked.transpose(0, 2, 1, 3)
    return stacked.reshape(full)


def _split_shards(name, full_arr):
    axis, full = BIG_WEIGHTS[name]
    L, rows, cols = full
    if axis == 2:
        return full_arr.reshape(L, rows, N_DEV, cols // N_DEV).transpose(0, 2, 1, 3)
    return full_arr.reshape(L, N_DEV, rows // N_DEV, cols)


def _gather_shards(shards, name):
    n = len(shards)

    def body(*refs):
        x_refs, out_refs = refs[:n], refs[n:2 * n]
        send_sems, recv_sems, local_sems = refs[2 * n:]
        x, y, c = _my_coords()
        me, sibling = (x, y, c), (x, y, 1 - c)
        chips = [(1 - x, y), (x, 1 - y), (1 - x, 1 - y)]

        def rows(a, px, py, pc):
            return out_refs[a].at[:, 4 * px + 2 * py + pc]

        def copy(a, k, block, to, src=None):
            return pltpu.make_async_remote_copy(
                src_ref=rows(a, *block) if src is None else src, dst_ref=rows(a, *block),
                send_sem=send_sems.at[7 * a + k], recv_sem=recv_sems.at[7 * a + k], device_id=to, device_id_type=MESH)

        mine = [pltpu.make_async_copy(x_refs[a], rows(a, *me), local_sems.at[a]) for a in range(n)]
        for cp in mine:
            cp.start()
        first = []
        for a in range(n):
            first.append(copy(a, 0, me, sibling, src=x_refs[a]))
            first += [copy(a, 1 + j, me, (*chip, c), src=x_refs[a]) for j, chip in enumerate(chips)]
        for cp in first:
            cp.start()
        passed = []
        for j, chip in enumerate(chips):
            for a in range(n):
                copy(a, 1 + j, (*chip, c), me).wait_recv()
                fwd = copy(a, 4 + j, (*chip, c), sibling)
                fwd.start()
                passed.append(fwd)
        for a in range(n):
            copy(a, 0, sibling, me).wait_recv()
            for j, chip in enumerate(chips):
                copy(a, 4 + j, (*chip, 1 - c), me).wait_recv()
        for cp in first + passed:
            cp.wait_send()
        for cp in mine:
            cp.wait()

    return pl.pallas_call(
        body, name=name,
        out_shape=[jax.ShapeDtypeStruct((s.shape[0], N_DEV) + s.shape[1:], s.dtype) for s in shards],
        in_specs=[_ANY] * n, out_specs=[_ANY] * n,
        scratch_shapes=[pltpu.SemaphoreType.DMA((7 * n,)), pltpu.SemaphoreType.DMA((7 * n,)),
                        pltpu.SemaphoreType.DMA((n,))],
    )(*shards)


def _exchange_shards(arrs, masks, src_of, out_tail, name):
    n, nm = len(arrs), len(masks)

    def body(*refs):
        in_refs, out_refs = refs[:n], refs[n:2 * n]
        send_sems, recv_sems = refs[2 * n:]
        x, y, c = _my_coords()
        copies = []
        for a in range(n):
            for k, (bx, by, bc) in enumerate(masks):
                peer = (_flip(x, bx), _flip(y, by), _flip(c, bc))
                copies.append(pltpu.make_async_remote_copy(
                    src_ref=src_of(in_refs[a], peer, (x, y, c)), dst_ref=out_refs[a].at[k],
                    send_sem=send_sems.at[nm * a + k], recv_sem=recv_sems.at[nm * a + k],
                    device_id=peer, device_id_type=MESH))
        for cp in copies:
            cp.start()
        for cp in copies:
            cp.wait()

    return pl.pallas_call(
        body, name=name,
        out_shape=[jax.ShapeDtypeStruct((nm,) + out_tail(s), s.dtype) for s in arrs],
        in_specs=[_ANY] * n, out_specs=[_ANY] * n,
        scratch_shapes=[pltpu.SemaphoreType.DMA((nm * n,)), pltpu.SemaphoreType.DMA((nm * n,))],
    )(*arrs)


def _pair_sum(g, from_sibling, core, out_dtype, name):
    L, _, _, a, b = g.shape
    ta = _tile(a, (256, 128, 64, 32, 16, 8))

    def body(core_ref, g_ref, s_ref, o_ref):
        o_ref[...] = (g_ref[...] + s_ref[...]).astype(out_dtype)

    blk = pl.BlockSpec((1, 1, ta, b), lambda l, q, i, core_ref: (l, q, i, 0))
    return pl.pallas_call(
        body, name=name,
        grid_spec=pltpu.PrefetchScalarGridSpec(
            num_scalar_prefetch=1, grid=(L, 4, a // ta),
            in_specs=[pl.BlockSpec((1, 1, pl.Squeezed(), ta, b), lambda l, q, i, core_ref: (l, q, core_ref[0], i, 0)), blk],
            out_specs=blk),
        out_shape=jax.ShapeDtypeStruct((L, 4, a, b), out_dtype), compiler_params=_params())(core, g, from_sibling)


def _to_heads(a, B, n, heads):
    return a.reshape(B, n, heads, HEAD_DIM).transpose(0, 2, 1, 3)


def _from_heads(a):
    B, heads, n, _ = a.shape
    return a.transpose(0, 2, 1, 3).reshape(B * n, heads * HEAD_DIM)


def _mods(mod_x, mod_c, layer):
    both = jnp.concatenate([mod_x[:, layer], mod_c[layer][None]], axis=0)
    return [both[:, None, k * D_MODEL:(k + 1) * D_MODEL] for k in range(6)]


def _local_step(x, ctx, target, mod_x, mod_c, w, small):
    B, S, _ = x.shape
    L = ctx.shape[1]
    NX, NC = B * S, B * L
    T = NX + NC
    tiles_per_ex = S // ROW_TILE
    nxt = NX // ROW_TILE
    gidx = _group_index(nxt, tiles_per_ex, B)
    gidx_for = lambda rows: _group_index(NX // rows, S // rows, B)
    mm_rows = _tile(S, (MM_ROWS, ROW_TILE))
    tidx = lambda i: jnp.where(i < nxt, i % tiles_per_ex, tiles_per_ex)
    G = B + 1
    x0 = jnp.concatenate([x.reshape(NX, D_MODEL), ctx.reshape(NC, D_MODEL)], axis=0)
    acos, asin = [jnp.tile(t, (1, LANES // HEAD_DIM)) for t in _rope_tables(S, HEAD_DIM)]
    rcos, rsin = _rope_tables(S, RET_QK_DIM)
    sink = small["attn_sink"].reshape(N_KV_HEADS, GQA_GROUP)
    gains = jnp.stack([jnp.tile(small["attn_q_norm"].reshape(1, HEAD_DIM), (1, LANES // HEAD_DIM)),
                       jnp.tile(small["attn_k_norm"].reshape(1, HEAD_DIM), (1, LANES // HEAD_DIM))])
    log_g = jax.nn.log_sigmoid(small["ret_decay_logit"].reshape(2, RET_HEADS))
    n1, n2 = small["norm1_g"], small["norm2_g"]
    qd, kvd = N_HEADS * HEAD_DIM, N_KV_HEADS * HEAD_DIM

    m0 = _mods(mod_x, mod_c, 0)
    h1 = _norm_mod_fwd(x0, n1[0:1], m0[0], m0[1], gidx, "l0_norm1")
    qkv = _mm(h1, w["attn_w_qkv"][0], "nn", F32, "l0_qkv")
    qkv_r = _attn_prep_fwd(qkv, gains, acos, asin, tidx, "l0_qk_prep")
    q_x, k_x, v_x = (_to_heads(qkv_r[:NX, :qd], B, S, N_HEADS), _to_heads(qkv_r[:NX, qd:qd + kvd], B, S, N_KV_HEADS),
                     _to_heads(qkv_r[:NX, qd + kvd:], B, S, N_KV_HEADS))
    q_c, k_c, v_c = (_to_heads(qkv_r[NX:, :qd], B, L, N_HEADS), _to_heads(qkv_r[NX:, qd:qd + kvd], B, L, N_KV_HEADS),
                     _to_heads(qkv_r[NX:, qd + kvd:], B, L, N_KV_HEADS))
    o_x = _attn_fwd(q_x, k_x, v_x, k_c, v_c, sink, "l0_attn_x")
    o_c = _attn_fwd(q_c, None, None, k_c, v_c, sink, "l0_attn_c")
    o0 = jnp.concatenate([_from_heads(o_x), _from_heads(o_c)], axis=0).astype(MXU_DTYPE)
    mo0, x1 = _mm(o0, w["attn_w_o"][0], "nn", F32, "l0_attn_out", res=x0, gate=m0[2], gidx_for=gidx_for, gate_rows=mm_rows)
    h2 = _norm_mod_fwd(x1, n2[0:1], m0[3], m0[4], gidx, "l0_norm2")
    u0 = _mm(h2, w["ffn_w_in"][0], "nn", F32, "l0_ffn_in")
    a0 = _swiglu_fwd(u0, "l0_swiglu")
    f0, x2 = _mm(a0, w["ffn_w_out"][0], "nn", F32, "l0_ffn_out", res=x1, gate=m0[5], gidx_for=gidx_for, gate_rows=mm_rows)

    m1 = _mods(mod_x, mod_c, 1)
    g1 = _norm_mod_fwd(x2, n1[1:2], m1[0], m1[1], gidx, "l1_norm1")
    qkvg = _mm(g1, w["ret_w_qkvg"][0], "nn", F32, "l1_qkvg")
    qk = _ret_rope(qkvg, rcos, rsin, tidx, False, F32, "l1_rope")
    of, st_f = _ret_fwd(qk, qkvg, log_g, B, S, L, False, "l1_ret_fwd")
    ob, st_b = _ret_fwd(qk, qkvg, log_g, B, S, L, True, "l1_ret_rev")
    gn = w["ret_gn_g"].reshape(1, RET_VWIDTH)
    z1 = _gated_out_fwd(of, ob, qkvg, gn, "l1_gated_out")
    xx2 = x2[:NX]
    gx = lambda i: i // tiles_per_ex
    m1x = [t[:B] for t in m1]
    mo1, y1 = _mm(z1, w["ret_w_o"][0], "nn", F32, "l1_ret_out", res=xx2, gate=m1x[2], gidx_for=gidx_for, gate_rows=mm_rows)
    k2 = _norm_mod_fwd(y1, n2[1:2], m1x[3], m1x[4], gx, "l1_norm2")
    u1 = _mm(k2, w["ffn_w_in"][1], "nn", F32, "l1_ffn_in")
    a1 = _swiglu_fwd(u1, "l1_swiglu")
    f1, y2 = _mm(a1, w["ffn_w_out"][1], "nn", F32, "l1_ffn_out", res=y1, gate=m1x[5], gidx_for=gidx_for, gate_rows=mm_rows)

    loss_tile, dy2 = _loss_fwd_bwd(y2, target.reshape(NX, D_MODEL), "loss")

    zg = jnp.zeros((1, 1, D_MODEL), F32)
    dz, dgate5_1 = _gate_bwd(dy2, f1, m1x[5], gx, B, "l1_ffn_gate_bwd")
    gw_ffn_out1 = _mm(a1, dz, "tn", F32, "l1_ffn_out_dw")
    da = _mm(dz, w["ffn_w_out"][1], "nt", F32, "l1_ffn_out_dx")
    du = _swiglu_bwd(da, u1, "l1_swiglu_bwd")
    gw_ffn_in1 = _mm(k2, du, "tn", F32, "l1_ffn_in_dw")
    dk2 = _mm(du, w["ffn_w_in"][1], "nt", F32, "l1_ffn_in_dx")
    dy1, dsh3_1, dsc4_1, dn2_1 = _norm_mod_bwd(dk2, y1, n2[1:2], m1x[4], dy2, gx, B, "l1_norm2_bwd")
    dzo, dgate2_1 = _gate_bwd(dy1, mo1, m1x[2], gx, B, "l1_ret_gate_bwd")
    gw_ret_o = _mm(z1, dzo, "tn", F32, "l1_ret_out_dw")
    dz1 = _mm(dzo, w["ret_w_o"][0], "nt", F32, "l1_ret_out_dx")
    do_r, dg_r, dgn = _gated_out_bwd(dz1, of, ob, qkvg, gn, "l1_gated_out_bwd")
    dq_f, dk_f, dv_f, dkc_f, dvc_f, dlg_f = _ret_bwd(qk, qkvg, log_g, st_f, do_r, B, S, L, False, "l1_ret_fwd_bwd")
    dq_b, dk_b, dv_b, dkc_b, dvc_b, dlg_b = _ret_bwd(qk, qkvg, log_g, st_b, do_r, B, S, L, True, "l1_ret_rev_bwd")
    dqk_rot = jnp.concatenate([
        jnp.concatenate([dq_f + dq_b, dk_f + dk_b], axis=1),
        jnp.concatenate([jnp.zeros((NC, RET_HEADS * RET_QK_DIM), F32), dkc_f + dkc_b], axis=1)], axis=0)
    dqk_pre = _ret_rope(dqk_rot, rcos, rsin, tidx, True, MXU_DTYPE, "l1_rope_bwd")
    dvg = jnp.concatenate([
        jnp.concatenate([(dv_f + dv_b).astype(MXU_DTYPE), dg_r], axis=1),
        jnp.concatenate([(dvc_f + dvc_b).astype(MXU_DTYPE), jnp.zeros((NC, RET_VWIDTH), MXU_DTYPE)], axis=1)], axis=0)
    dqkvg = jnp.concatenate([dqk_pre, dvg], axis=1)
    gw_ret_qkvg = _mm(g1, dqkvg, "tn", F32, "l1_qkvg_dw")
    dg1 = _mm(dqkvg, w["ret_w_qkvg"][0], "nt", F32, "l1_qkvg_dx")
    dres1 = jnp.concatenate([dy1, jnp.zeros((NC, D_MODEL), F32)], axis=0)
    dx2, dsh0_1, dsc1_1, dn1_1 = _norm_mod_bwd(dg1, x2, n1[1:2], m1[1], dres1, gidx, G, "l1_norm1_bwd")
    dlg = jnp.stack([jnp.sum(dlg_f[:, :, 0, 0], axis=0), jnp.sum(dlg_b[:, :, 0, 0], axis=0)])
    d_decay = (dlg * jax.nn.sigmoid(-small["ret_decay_logit"].reshape(2, RET_HEADS))).reshape(1, 2, RET_HEADS)

    dz, dgate5_0 = _gate_bwd(dx2, f0, m0[5], gidx, G, "l0_ffn_gate_bwd")
    gw_ffn_out0 = _mm(a0, dz, "tn", F32, "l0_ffn_out_dw")
    da = _mm(dz, w["ffn_w_out"][0], "nt", F32, "l0_ffn_out_dx")
    du = _swiglu_bwd(da, u0, "l0_swiglu_bwd")
    gw_ffn_in0 = _mm(h2, du, "tn", F32, "l0_ffn_in_dw")
    dh2 = _mm(du, w["ffn_w_in"][0], "nt", F32, "l0_ffn_in_dx")
    dx1, dsh3_0, dsc4_0, dn2_0 = _norm_mod_bwd(dh2, x1, n2[0:1], m0[4], dx2, gidx, G, "l0_norm2_bwd")
    dzo, dgate2_0 = _gate_bwd(dx1, mo0, m0[2], gidx, G, "l0_attn_gate_bwd")
    gw_attn_o = _mm(o0, dzo, "tn", F32, "l0_attn_out_dw")
    do0 = _mm(dzo, w["attn_w_o"][0], "nt", MXU_DTYPE, "l0_attn_out_dx")
    do_x, do_c = _to_heads(do0[:NX], B, S, N_HEADS), _to_heads(do0[NX:], B, L, N_HEADS)
    dq_x, dk_x, dv_x, dkc1, dvc1, dsink_x = _attn_bwd(q_x, k_x, v_x, k_c, v_c, sink, do_x, "l0_attn_x_bwd")
    dq_c, dkc2, dvc2, dsink_c = _attn_bwd(q_c, None, None, k_c, v_c, sink, do_c, "l0_attn_c_bwd")
    dqk = jnp.concatenate([
        jnp.concatenate([_from_heads(dq_x), _from_heads(dk_x)], axis=1),
        jnp.concatenate([_from_heads(dq_c), _from_heads(dkc1 + dkc2)], axis=1)], axis=0)
    dvv = jnp.concatenate([_from_heads(dv_x), _from_heads(dvc1 + dvc2)], axis=0)
    dqkv, dgains = _attn_prep_bwd(dqk, dvv, qkv, gains, acos, asin, tidx, "l0_qk_prep_bwd")
    gw_attn_qkv = _mm(h1, dqkv, "tn", F32, "l0_qkv_dw")
    dh1 = _mm(dqkv, w["attn_w_qkv"][0], "nt", F32, "l0_qkv_dx")
    dx0, dsh0_0, dsc1_0, dn1_0 = _norm_mod_bwd(dh1, x0, n1[0:1], m0[1], dx1, gidx, G, "l0_norm1_bwd")

    dgains = dgains[:, 0, :HEAD_DIM] + dgains[:, 0, HEAD_DIM:]
    dsink = (dsink_x + dsink_c).reshape(N_KV_HEADS, 8, LANES)[:, :GQA_GROUP, 0].reshape(1, N_HEADS)
    grads_big = {
        "ffn_w_in": jnp.stack([gw_ffn_in0, gw_ffn_in1]),
        "ffn_w_out": jnp.stack([gw_ffn_out0, gw_ffn_out1]),
        "attn_w_qkv": gw_attn_qkv[None],
        "attn_w_o": gw_attn_o[None],
        "ret_w_qkvg": gw_ret_qkvg[None],
        "ret_gn_g": dgn,
        "ret_w_o": gw_ret_o[None],
    }
    grads_small = {
        "norm1_g": jnp.concatenate([dn1_0, dn1_1], axis=0),
        "norm2_g": jnp.concatenate([dn2_0, dn2_1], axis=0),
        "attn_q_norm": jnp.sum(dgains[:ATTN_Q_BLOCKS], axis=0)[None],
        "attn_k_norm": jnp.sum(dgains[ATTN_Q_BLOCKS:ATTN_QK_BLOCKS], axis=0)[None],
        "attn_sink": dsink,
        "ret_decay_logit": d_decay,
    }

    def pad_g(t):
        return jnp.concatenate([t, zg], axis=0)

    d0 = jnp.concatenate([dsh0_0, dsc1_0, dgate2_0, dsh3_0, dsc4_0, dgate5_0], axis=2)[:, 0]
    d1 = jnp.concatenate([dsh0_1, dsc1_1, pad_g(dgate2_1), pad_g(dsh3_1), pad_g(dsc4_1), pad_g(dgate5_1)],
                         axis=2)[:, 0]
    dmod_x = jnp.stack([d0[:B], d1[:B]], axis=1)
    dmod_c = jnp.stack([d0[B], d1[B]], axis=0)
    return loss_tile, dx0[:NX].reshape(B, S, D_MODEL), grads_big, grads_small, dmod_x, dmod_c


SMALL_NAMES = ("c_ctx", "ada_b", "norm1_g", "norm2_g", "attn_q_norm", "attn_k_norm", "attn_sink", "ret_decay_logit")
ADA_ROWS = 64


def _pack_small(d, rows):
    flat = jnp.concatenate([d[k].reshape(-1) for k in SMALL_NAMES])
    n = rows * LANES
    return jnp.pad(flat, (0, n - flat.shape[0])).reshape(rows, LANES)


def _unpack_small(packed, shapes):
    flat = packed.reshape(-1)
    out, off = {}, 0
    for k in SMALL_NAMES:
        n = math.prod(shapes[k])
        out[k] = flat[off:off + n].reshape(shapes[k])
        off += n
    return out


def _gather_big_weights(weights):
    gathered = _gather_shards(
        [weights[k].reshape(1, 1, -1) if k == "ret_gn_g" else weights[k].astype(MXU_DTYPE) for k in BIG_WEIGHTS],
        "gather_weights")
    return {k: _join_shards(k, g) for k, g in zip(BIG_WEIGHTS, gathered)}


def _reduce_and_update_big(g_big, weights, mom1, mom2):
    mx_, my_, mc_ = _my_coords()
    my_chip = 2 * mx_ + my_
    names = list(BIG_WEIGHTS)
    split = []
    for k in names:
        s = _split_shards(k, g_big[k])
        split.append(s.reshape(s.shape[0], 4, 2, s.shape[2], s.shape[3]))
    from_sibling = _exchange_shards(
        split, [(0, 0, 1)], lambda ref, peer, me_: ref.at[:, :, peer[2]],
        lambda s: (s.shape[0], 4) + s.shape[3:], "rs_sibling")
    from_sibling = [t[0] for t in from_sibling]
    core = mc_.astype(jnp.int32).reshape(1)
    pair = [_pair_sum(g, s, core, MXU_DTYPE, "rs_pair_" + k) for k, g, s in zip(names, split, from_sibling)]
    from_chips = _exchange_shards(
        pair, [(1, 0, 0), (0, 1, 0), (1, 1, 0)], lambda ref, peer, me_: ref.at[:, 2 * peer[0] + peer[1]],
        lambda s: (s.shape[0],) + s.shape[2:], "rs_chips")
    big = {}
    for k, g, s, r in zip(names, split, from_sibling, from_chips):
        L_, _, _, a_, b_ = g.shape
        own_keep = lax.dynamic_index_in_dim(lax.dynamic_index_in_dim(g, my_chip, axis=1, keepdims=False), mc_, axis=1,
                                            keepdims=False)
        own_sib = lax.dynamic_index_in_dim(s, my_chip, axis=1, keepdims=False)
        rows = L_ * a_
        res = _adamw(weights[k].reshape(rows, b_), mom1[k].reshape(rows, b_), mom2[k].reshape(rows, b_),
                     [own_keep.reshape(1, rows, b_), own_sib.reshape(1, rows, b_), r.reshape(3, rows, b_)],
                     "adamw_" + k)
        big[k] = [t.reshape(weights[k].shape) for t in res]
    return big


def kernel(x, c, ctx, c_ctx, ada_w, ada_b, norm1_g, norm2_g, ffn_w_in, ffn_w_out, attn_w_qkv, attn_q_norm, attn_k_norm, attn_sink, attn_w_o, ret_w_qkvg, ret_decay_logit, ret_gn_g, ret_w_o, loss_target, m_c_ctx, m_ada_w, m_ada_b, m_norm1_g, m_norm2_g, m_ffn_w_in, m_ffn_w_out, m_attn_w_qkv, m_attn_q_norm, m_attn_k_norm, m_attn_sink, m_attn_w_o, m_ret_w_qkvg, m_ret_decay_logit, m_ret_gn_g, m_ret_w_o, v_c_ctx, v_ada_w, v_ada_b, v_norm1_g, v_norm2_g, v_ffn_w_in, v_ffn_w_out, v_attn_w_qkv, v_attn_q_norm, v_attn_k_norm, v_attn_sink, v_attn_w_o, v_ret_w_qkvg, v_ret_decay_logit, v_ret_gn_g, v_ret_w_o):
    weights = dict(c_ctx=c_ctx, ada_w=ada_w, ada_b=ada_b, norm1_g=norm1_g, norm2_g=norm2_g, ffn_w_in=ffn_w_in,
                   ffn_w_out=ffn_w_out, attn_w_qkv=attn_w_qkv, attn_q_norm=attn_q_norm, attn_k_norm=attn_k_norm,
                   attn_sink=attn_sink, attn_w_o=attn_w_o, ret_w_qkvg=ret_w_qkvg, ret_decay_logit=ret_decay_logit,
                   ret_gn_g=ret_gn_g, ret_w_o=ret_w_o)
    mom1 = dict(c_ctx=m_c_ctx, ada_w=m_ada_w, ada_b=m_ada_b, norm1_g=m_norm1_g, norm2_g=m_norm2_g, ffn_w_in=m_ffn_w_in,
                ffn_w_out=m_ffn_w_out, attn_w_qkv=m_attn_w_qkv, attn_q_norm=m_attn_q_norm, attn_k_norm=m_attn_k_norm,
                attn_sink=m_attn_sink, attn_w_o=m_attn_w_o, ret_w_qkvg=m_ret_w_qkvg, ret_decay_logit=m_ret_decay_logit,
                ret_gn_g=m_ret_gn_g, ret_w_o=m_ret_w_o)
    mom2 = dict(c_ctx=v_c_ctx, ada_w=v_ada_w, ada_b=v_ada_b, norm1_g=v_norm1_g, norm2_g=v_norm2_g, ffn_w_in=v_ffn_w_in,
                ffn_w_out=v_ffn_w_out, attn_w_qkv=v_attn_w_qkv, attn_q_norm=v_attn_q_norm, attn_k_norm=v_attn_k_norm,
                attn_sink=v_attn_sink, attn_w_o=v_attn_w_o, ret_w_qkvg=v_ret_w_qkvg, ret_decay_logit=v_ret_decay_logit,
                ret_gn_g=v_ret_gn_g, ret_w_o=v_ret_w_o)
    B = x.shape[0]
    mx_, my_, mc_ = _my_coords()
    me = 4 * mx_ + 2 * my_ + mc_
    my_chip = 2 * mx_ + my_
    ada_cols = ada_w.shape[2]

    w_full = _gather_big_weights(weights)

    c_all = _all_gather(jax.nn.silu(c), "gather_c").reshape(N_DEV * B, D_MODEL)
    cc_act = jax.nn.silu(c_ctx)[None]
    ada_in = jnp.concatenate([c_all, cc_act, jnp.zeros((ADA_ROWS - N_DEV * B - 1, D_MODEL), F32)], axis=0)
    ada_in = ada_in.astype(MXU_DTYPE)
    ada_w2 = jnp.concatenate([ada_w[0], ada_w[1]], axis=1)
    bias = lax.dynamic_slice_in_dim(ada_b.reshape(2, N_DEV, ada_cols), me, 1, axis=1).reshape(1, 2 * ada_cols)
    mod_cols = _mm(ada_in, ada_w2, "nn", F32, "ada_fwd", bias=bias)
    mod_all = _all_gather(mod_cols, "gather_mod")
    mod_all = mod_all.reshape(N_DEV, ADA_ROWS, 2, ada_cols).transpose(1, 2, 0, 3).reshape(ADA_ROWS, 2, N_DEV * ada_cols)
    mod_x = lax.dynamic_slice_in_dim(mod_all, me * B, B, axis=0)
    mod_c = mod_all[N_DEV * B]

    small = {k: weights[k] for k in SMALL_NAMES}
    loss_tile, grad_x, g_big, g_small, dmod_x, dmod_c = _local_step(x, ctx, loss_target, mod_x, mod_c, w_full, small)
    loss = lax.psum(loss_tile[0, 0], ("x", "y", "c"))

    n_mod = 2 * 6 * D_MODEL
    dm_rows = jnp.concatenate([dmod_x.reshape(B, n_mod), dmod_c.reshape(1, n_mod),
                               jnp.zeros((8 - B - 1, n_mod), F32)], axis=0)
    dm_all = _all_gather(dm_rows, "gather_dmod")
    dmc_tot = _sum_rows(dm_all[:, B:B + 1].reshape(N_DEV, 1, n_mod)[:, :, :].reshape(N_DEV, n_mod // LANES, LANES),
                        "sum_dmod_c").reshape(1, n_mod)
    dmod_rows = jnp.concatenate([dm_all[:, :B].reshape(N_DEV * B, n_mod), dmc_tot,
                                 jnp.zeros((ADA_ROWS - N_DEV * B - 1, n_mod), F32)], axis=0)
    dmod_mine = lax.dynamic_slice_in_dim(dmod_rows.reshape(ADA_ROWS, 2, N_DEV, ada_cols), me, 1, axis=2)
    dmod_mine = dmod_mine.reshape(ADA_ROWS, 2 * ada_cols).astype(MXU_DTYPE)
    g_ada2 = _mm(ada_in, dmod_mine, "tn", F32, "ada_dw")
    g_ada_w = jnp.stack([g_ada2[:, :ada_cols], g_ada2[:, ada_cols:]])
    dmc_mine = jnp.concatenate([dmod_mine[N_DEV * B:N_DEV * B + 1], jnp.zeros((7, 2 * ada_cols), MXU_DTYPE)], axis=0)
    dcc_part = _mm(dmc_mine, ada_w2, "nt", F32, "ada_dc")[0:1]
    g_ada_b = _sum_rows(dmod_rows[:, None, :].reshape(ADA_ROWS, n_mod // LANES, LANES), "sum_dmod_b").reshape(2, 6 * D_MODEL)
    sg = jax.nn.sigmoid(c_ctx)
    g_small["c_ctx"] = dcc_part.reshape(D_MODEL) * (sg * (1.0 + c_ctx * (1.0 - sg)))
    g_small["ada_b"] = g_ada_b * (1.0 / N_DEV)

    shapes = {k: weights[k].shape for k in SMALL_NAMES}
    n_small = sum(math.prod(s) for s in shapes.values())
    srows = -(-(-(-n_small // LANES)) // 8) * 8
    gs_all = _all_gather(_pack_small(g_small, srows), "gather_small_grads")
    sm = _adamw(_pack_small({k: weights[k] for k in SMALL_NAMES}, srows), _pack_small({k: mom1[k] for k in SMALL_NAMES}, srows),
                _pack_small({k: mom2[k] for k in SMALL_NAMES}, srows), [gs_all], "adamw_small")
    sm = [_unpack_small(t, shapes) for t in sm]

    ada_shape = ada_w.shape
    r2 = lambda t: t.reshape(ada_shape[0] * ada_shape[1], ada_shape[2])
    ada = [t.reshape(ada_shape) for t in _adamw(r2(ada_w), r2(m_ada_w), r2(v_ada_w), [r2(g_ada_w)[None]], "adamw_ada")]

    big = _reduce_and_update_big(g_big, weights, mom1, mom2)

    def pick(i, name):
        if name in BIG_WEIGHTS:
            return big[name][i]
        if name == "ada_w":
            return ada[i]
        return sm[i][name]

    order = ("c_ctx", "ada_w", "ada_b", "norm1_g", "norm2_g", "ffn_w_in", "ffn_w_out", "attn_w_qkv", "attn_q_norm",
             "attn_k_norm", "attn_sink", "attn_w_o", "ret_w_qkvg", "ret_decay_logit", "ret_gn_g", "ret_w_o")
    outs = [loss, grad_x]
    for i in range(4):
        outs += [pick(i, n) for n in order]
    return tuple(outs)
```

```python
import functools
import math

import jax
import jax.numpy as jnp
from jax import lax
from jax.experimental import pallas as pl
from jax.experimental.pallas import tpu as pltpu

F32 = jnp.float32
MXU_DTYPE = jnp.bfloat16

D_MODEL = 1024
HEAD_DIM = 64
N_HEADS = 16
N_KV_HEADS = 4
GQA_GROUP = 4
WINDOW = 128
ATTN_BLOCK = 128
RET_HEADS = 4
RET_QK_DIM = 256
RET_V_DIM = 512
RET_VWIDTH = 2048
RET_CHUNK = 128
D_FF = 2816
GRID_W = 64
ROPE_BASE = 10000.0
EPS = 1e-6
NEG_INF = -1e30

ADAM_LR = 0.001
ADAM_B1 = 0.9
ADAM_B2 = 0.999
ADAM_EPS = 1e-08
ADAM_WD = 0.01
ADAM_STEP = 10

N_DEV = 8
LANES = 128
ROW_TILE = 512
VMEM_LIMIT = 48 * 1024 * 1024

MESH = pl.DeviceIdType.MESH
_ANY = pl.BlockSpec(memory_space=pl.ANY)
_SMEM = pl.BlockSpec(memory_space=pltpu.SMEM)


def _params(**kw):
    return pltpu.CompilerParams(vmem_limit_bytes=VMEM_LIMIT, **kw)


def _mx(v):
    return v.astype(MXU_DTYPE)


def _dot(a, b, dims):
    return lax.dot_general(_mx(a), _mx(b), (dims, ((), ())), preferred_element_type=F32)


_NN = ((1,), (0,))
_NT = ((1,), (1,))
_TN = ((0,), (0,))


def _tile(n, cands):
    for c in cands:
        if n % c == 0:
            return c
    return n


def _big_tile(n, cap):
    if n <= cap:
        return n
    for t in range(cap - cap % LANES, 0, -LANES):
        if n % t == 0:
            return t
    return n


MM_ROWS = 1024
MM_COLS = 1408
MM_DEPTH = 2048


def _k_tile(k):
    return _big_tile(k, MM_DEPTH)


def _mm(a, b, mode, out_dtype, name, *, bias=None, res=None, gate=None, gidx_for=None, gate_rows=None):
    if mode == "nn":
        (M, K), (_, N) = a.shape, b.shape
    elif mode == "nt":
        (M, K), (N, _) = a.shape, b.shape
    else:
        (K, M), (_, N) = a.shape, b.shape
    if res is not None:
        tm, tn = gate_rows, _big_tile(N, 512)
        gidx = gidx_for(tm)
    else:
        tm = _big_tile(M, MM_COLS if mode == "tn" else MM_ROWS)
        tn = _big_tile(N, MM_COLS)
    tk = _k_tile(K)
    nk = K // tk
    dims = {"nn": _NN, "nt": _NT, "tn": _TN}[mode]
    a_spec = (pl.BlockSpec((tk, tm), lambda i, j, k: (k, i)) if mode == "tn"
              else pl.BlockSpec((tm, tk), lambda i, j, k: (i, k)))
    b_spec = (pl.BlockSpec((tn, tk), lambda i, j, k: (j, k)) if mode == "nt"
              else pl.BlockSpec((tk, tn), lambda i, j, k: (k, j)))
    o_spec = pl.BlockSpec((tm, tn), lambda i, j, k: (i, j))
    in_specs, operands = [a_spec, b_spec], [a, b]
    if bias is not None:
        in_specs.append(pl.BlockSpec((1, tn), lambda i, j, k: (0, j)))
        operands.append(bias)
    if res is not None:
        in_specs += [o_spec, pl.BlockSpec((1, 1, tn), lambda i, j, k: (gidx(i), 0, j))]
        operands += [res, gate]
        out_shape = (jax.ShapeDtypeStruct((M, N), F32), jax.ShapeDtypeStruct((M, N), F32))
        out_specs = (o_spec, o_spec)
    else:
        out_shape = jax.ShapeDtypeStruct((M, N), out_dtype)
        out_specs = o_spec

    def body(*refs):
        a_ref, b_ref = refs[0], refs[1]
        extra = refs[2:len(operands)]
        outs = refs[len(operands):]
        prod = _dot(a_ref[...], b_ref[...], dims)

        def finish(acc):
            if bias is not None:
                outs[0][...] = (acc + extra[0][...]).astype(out_dtype)
            elif res is not None:
                outs[0][...] = acc
                outs[1][...] = extra[0][...] + extra[1][0] * acc
            else:
                outs[0][...] = acc.astype(out_dtype)

        if nk == 1:
            finish(prod)
        else:
            acc_ref = outs[-1]
            outs = outs[:-1]
            k = pl.program_id(2)

            @pl.when(k == 0)
            def _():
                acc_ref[...] = prod

            @pl.when(k > 0)
            def _():
                acc_ref[...] += prod

            @pl.when(k == nk - 1)
            def _():
                finish(acc_ref[...])

    return pl.pallas_call(
        body, name=name, grid=(M // tm, N // tn, nk), in_specs=in_specs, out_specs=out_specs, out_shape=out_shape,
        scratch_shapes=[pltpu.VMEM((tm, tn), F32)] if nk > 1 else [],
        compiler_params=_params())(*operands)


def _group_index(n_x_tiles, tiles_per_example, n_examples):
    def gidx(i):
        return jnp.where(i < n_x_tiles, i // tiles_per_example, n_examples)
    return gidx


def _norm_mod_fwd(x, g, shift, scale, gidx, name):
    T, Dm = x.shape

    def body(x_ref, g_ref, sh_ref, sc_ref, h_ref):
        xv = x_ref[...]
        r = lax.rsqrt(jnp.mean(xv * xv, axis=-1, keepdims=True) + EPS)
        y = xv * r * g_ref[...]
        h_ref[...] = (y * (1.0 + sc_ref[0]) + sh_ref[0]).astype(h_ref.dtype)

    row = pl.BlockSpec((ROW_TILE, Dm), lambda i: (i, 0))
    mod = pl.BlockSpec((1, 1, Dm), lambda i: (gidx(i), 0, 0))
    return pl.pallas_call(
        body, name=name, grid=(T // ROW_TILE,),
        in_specs=[row, pl.BlockSpec((1, Dm), lambda i: (0, 0)), mod, mod],
        out_specs=row, out_shape=jax.ShapeDtypeStruct((T, Dm), MXU_DTYPE),
        compiler_params=_params())(x, g, shift, scale)


def _first_of_group(i, gidx):
    return jnp.logical_or(i == 0, gidx(i) != gidx(jnp.maximum(i - 1, 0)))


def _norm_mod_bwd(dh, x, g, scale, dres, gidx, n_groups, name):
    T, Dm = x.shape

    def body(dh_ref, x_ref, g_ref, sc_ref, dres_ref, dx_ref, dsh_ref, dsc_ref, dg_ref):
        i = pl.program_id(0)
        xv, dhv = x_ref[...], dh_ref[...]
        r = lax.rsqrt(jnp.mean(xv * xv, axis=-1, keepdims=True) + EPS)
        xn = xv * r
        y = xn * g_ref[...]

        @pl.when(_first_of_group(i, gidx))
        def _():
            dsh_ref[...] = jnp.zeros_like(dsh_ref)
            dsc_ref[...] = jnp.zeros_like(dsc_ref)

        @pl.when(i == 0)
        def _():
            dg_ref[...] = jnp.zeros_like(dg_ref)

        dsh_ref[0] += jnp.sum(dhv, axis=0, keepdims=True)
        dsc_ref[0] += jnp.sum(dhv * y, axis=0, keepdims=True)
        dy = dhv * (1.0 + sc_ref[0])
        dg_ref[...] += jnp.sum(dy * xn, axis=0, keepdims=True)
        dxn = dy * g_ref[...]
        dx = r * (dxn - xn * jnp.mean(dxn * xn, axis=-1, keepdims=True))
        dx_ref[...] = dres_ref[...] + dx

    row = pl.BlockSpec((ROW_TILE, Dm), lambda i: (i, 0))
    mod = pl.BlockSpec((1, 1, Dm), lambda i: (gidx(i), 0, 0))
    vec = pl.BlockSpec((1, Dm), lambda i: (0, 0))
    return pl.pallas_call(
        body, name=name, grid=(T // ROW_TILE,),
        in_specs=[row, row, vec, mod, row],
        out_specs=(row, mod, mod, vec),
        out_shape=(jax.ShapeDtypeStruct((T, Dm), F32), jax.ShapeDtypeStruct((n_groups, 1, Dm), F32),
                   jax.ShapeDtypeStruct((n_groups, 1, Dm), F32), jax.ShapeDtypeStruct((1, Dm), F32)),
        compiler_params=_params())(dh, x, g, scale, dres)


def _gate_bwd(dy, f, gate, gidx, n_groups, name):
    T, Dm = dy.shape

    def body(dy_ref, f_ref, gate_ref, dz_ref, dgate_ref):
        i = pl.program_id(0)
        dyv = dy_ref[...]

        @pl.when(_first_of_group(i, gidx))
        def _():
            dgate_ref[...] = jnp.zeros_like(dgate_ref)

        dgate_ref[0] += jnp.sum(dyv * f_ref[...], axis=0, keepdims=True)
        dz_ref[...] = (dyv * gate_ref[0]).astype(dz_ref.dtype)

    row = pl.BlockSpec((ROW_TILE, Dm), lambda i: (i, 0))
    mod = pl.BlockSpec((1, 1, Dm), lambda i: (gidx(i), 0, 0))
    return pl.pallas_call(
        body, name=name, grid=(T // ROW_TILE,), in_specs=[row, row, mod], out_specs=(row, mod),
        out_shape=(jax.ShapeDtypeStruct((T, Dm), MXU_DTYPE), jax.ShapeDtypeStruct((n_groups, 1, Dm), F32)),
        compiler_params=_params())(dy, f, gate)


SWIGLU_ROWS = 256


def _swiglu_fwd(u, name):
    T = u.shape[0]

    def body(u_ref, a_ref):
        gate, up = u_ref[:, :D_FF], u_ref[:, D_FF:]
        a_ref[...] = (gate * jax.nn.sigmoid(gate) * up).astype(a_ref.dtype)

    return pl.pallas_call(
        body, name=name, grid=(T // SWIGLU_ROWS,),
        in_specs=[pl.BlockSpec((SWIGLU_ROWS, 2 * D_FF), lambda i: (i, 0))],
        out_specs=pl.BlockSpec((SWIGLU_ROWS, D_FF), lambda i: (i, 0)),
        out_shape=jax.ShapeDtypeStruct((T, D_FF), MXU_DTYPE), compiler_params=_params())(u)


def _swiglu_bwd(da, u, name):
    T = u.shape[0]

    def body(da_ref, u_ref, du_ref):
        gate, up, dav = u_ref[:, :D_FF], u_ref[:, D_FF:], da_ref[...]
        sg = jax.nn.sigmoid(gate)
        du_ref[:, :D_FF] = (dav * up * (sg * (1.0 + gate * (1.0 - sg)))).astype(du_ref.dtype)
        du_ref[:, D_FF:] = (dav * gate * sg).astype(du_ref.dtype)

    return pl.pallas_call(
        body, name=name, grid=(T // SWIGLU_ROWS,),
        in_specs=[pl.BlockSpec((SWIGLU_ROWS, D_FF), lambda i: (i, 0)),
                  pl.BlockSpec((SWIGLU_ROWS, 2 * D_FF), lambda i: (i, 0))],
        out_specs=pl.BlockSpec((SWIGLU_ROWS, 2 * D_FF), lambda i: (i, 0)),
        out_shape=jax.ShapeDtypeStruct((T, 2 * D_FF), MXU_DTYPE), compiler_params=_params())(da, u)


def _loss_fwd_bwd(y, target, name):
    T, Dm = y.shape

    def body(y_ref, t_ref, loss_ref, dy_ref):
        err = y_ref[...] - t_ref[...]

        @pl.when(pl.program_id(0) == 0)
        def _():
            loss_ref[...] = jnp.zeros_like(loss_ref)

        loss_ref[...] += 0.5 * jnp.sum(jnp.mean(err * err, axis=-1, keepdims=True))
        dy_ref[...] = err * (1.0 / Dm)

    row = pl.BlockSpec((ROW_TILE, Dm), lambda i: (i, 0))
    return pl.pallas_call(
        body, name=name, grid=(T // ROW_TILE,), in_specs=[row, row],
        out_specs=(pl.BlockSpec((8, LANES), lambda i: (0, 0)), row),
        out_shape=(jax.ShapeDtypeStruct((8, LANES), F32), jax.ShapeDtypeStruct((T, Dm), F32)),
        compiler_params=_params())(y, target)


def _rope_tables(seq, head_dim):
    axis_dim = head_dim // 2
    half = axis_dim // 2
    pos = jnp.arange(seq, dtype=jnp.int32)
    row = (pos // GRID_W).astype(F32)[:, None]
    col = (pos % GRID_W).astype(F32)[:, None]
    inv = ROPE_BASE ** (-jnp.arange(0, axis_dim, 2, dtype=F32) / axis_dim)
    lane = jnp.arange(head_dim, dtype=jnp.int32)
    within = lane % axis_dim
    ang = jnp.where((lane // axis_dim == 0)[None, :], row, col) * inv[within % half][None, :]
    cos = jnp.cos(ang)
    sin = jnp.where((within < half)[None, :], -jnp.sin(ang), jnp.sin(ang))
    cos = jnp.concatenate([cos, jnp.ones((ROW_TILE, head_dim), F32)], axis=0)
    sin = jnp.concatenate([sin, jnp.zeros((ROW_TILE, head_dim), F32)], axis=0)
    return cos, sin


def _pair_swap(v, half):
    if 2 * half == LANES:
        return pltpu.roll(v, half, axis=1)
    lane = lax.broadcasted_iota(jnp.int32, v.shape, 1)
    return jnp.where((lane % (2 * half)) < half, pltpu.roll(v, LANES - half, axis=1), pltpu.roll(v, half, axis=1))


def _head_sum(v, ones_ref):
    hi = v.astype(MXU_DTYPE)
    lo = (v - hi.astype(F32)).astype(MXU_DTYPE)
    return (jnp.dot(hi, ones_ref[...], preferred_element_type=F32)
            + jnp.dot(lo, ones_ref[...], preferred_element_type=F32))


def _head_ones():
    lane = jnp.arange(LANES)
    return (lane[:, None] // HEAD_DIM == lane[None, :] // HEAD_DIM).astype(MXU_DTYPE)


ATTN_QK_BLOCKS = (N_HEADS + N_KV_HEADS) * HEAD_DIM // LANES
ATTN_ALL_BLOCKS = (N_HEADS + 2 * N_KV_HEADS) * HEAD_DIM // LANES
ATTN_Q_BLOCKS = N_HEADS * HEAD_DIM // LANES
ATTN_SCALE = HEAD_DIM ** -0.5


def _attn_prep_fwd(qkv, gains, cos, sin, tidx, name):
    T, W = qkv.shape

    def body(x_ref, g_ref, cos_ref, sin_ref, ones_ref, o_ref):
        for cb in range(ATTN_ALL_BLOCKS):
            cols = slice(cb * LANES, (cb + 1) * LANES)
            xv = x_ref[:, cols]
            if cb < ATTN_QK_BLOCKS:
                r = lax.rsqrt(_head_sum(xv * xv, ones_ref) * (1.0 / HEAD_DIM) + EPS)
                y = xv * r * g_ref[0 if cb < ATTN_Q_BLOCKS else 1]
                xv = y * cos_ref[...] + _pair_swap(y, HEAD_DIM // 4) * sin_ref[...]
                if cb < ATTN_Q_BLOCKS:
                    xv = xv * ATTN_SCALE
            o_ref[:, cols] = xv.astype(o_ref.dtype)

    row = pl.BlockSpec((ROW_TILE, W), lambda i: (i, 0))
    tab = pl.BlockSpec((ROW_TILE, LANES), lambda i: (tidx(i), 0))
    return pl.pallas_call(
        body, name=name, grid=(T // ROW_TILE,),
        in_specs=[row, pl.BlockSpec((2, 1, LANES), lambda i: (0, 0, 0)), tab, tab,
                  pl.BlockSpec((LANES, LANES), lambda i: (0, 0))],
        out_specs=row, out_shape=jax.ShapeDtypeStruct(qkv.shape, MXU_DTYPE),
        compiler_params=_params())(qkv, gains, cos, sin, _head_ones())


def _attn_prep_bwd(dqk, dv, qkv, gains, cos, sin, tidx, name):
    T, W = qkv.shape
    qk_w = ATTN_QK_BLOCKS * LANES

    def body(dqk_ref, dv_ref, x_ref, g_ref, cos_ref, sin_ref, ones_ref, o_ref, dg_ref):
        @pl.when(pl.program_id(0) == 0)
        def _():
            dg_ref[...] = jnp.zeros_like(dg_ref)

        for cb in range(ATTN_QK_BLOCKS):
            cols = slice(cb * LANES, (cb + 1) * LANES)
            xv, d = x_ref[:, cols], dqk_ref[:, cols]
            if cb < ATTN_Q_BLOCKS:
                d = d * ATTN_SCALE
            r = lax.rsqrt(_head_sum(xv * xv, ones_ref) * (1.0 / HEAD_DIM) + EPS)
            xn = xv * r
            dy = d * cos_ref[...] + _pair_swap(d * sin_ref[...], HEAD_DIM // 4)
            dg_ref[:, cols] += jnp.sum(dy * xn, axis=0, keepdims=True)
            dxn = dy * g_ref[0 if cb < ATTN_Q_BLOCKS else 1]
            dx = r * (dxn - xn * (_head_sum(dxn * xn, ones_ref) * (1.0 / HEAD_DIM)))
            o_ref[:, cols] = dx.astype(o_ref.dtype)
        o_ref[:, qk_w:] = dv_ref[...].astype(o_ref.dtype)

    row = lambda w: pl.BlockSpec((ROW_TILE, w), lambda i: (i, 0))
    tab = pl.BlockSpec((ROW_TILE, LANES), lambda i: (tidx(i), 0))
    return pl.pallas_call(
        body, name=name, grid=(T // ROW_TILE,),
        in_specs=[row(qk_w), row(W - qk_w), row(W), pl.BlockSpec((2, 1, LANES), lambda i: (0, 0, 0)), tab, tab,
                  pl.BlockSpec((LANES, LANES), lambda i: (0, 0))],
        out_specs=(row(W), pl.BlockSpec((1, qk_w), lambda i: (0, 0))),
        out_shape=(jax.ShapeDtypeStruct(qkv.shape, MXU_DTYPE), jax.ShapeDtypeStruct((1, qk_w), F32)),
        compiler_params=_params())(dqk, dv, qkv, gains, cos, sin, _head_ones())


RET_QK_BLOCKS = 2 * RET_HEADS * RET_QK_DIM // LANES


def _ret_rope(x, cos, sin, tidx, name):
    T = x.shape[0]
    W = RET_QK_BLOCKS * LANES
    k_scale = RET_QK_DIM ** -0.5

    def body(x_ref, cos_ref, sin_ref, o_ref):
        for cb in range(RET_QK_BLOCKS):
            cols = slice(cb * LANES, (cb + 1) * LANES)
            tcols = slice((cb % 2) * LANES, (cb % 2 + 1) * LANES)
            xv = x_ref[:, cols]
            out = xv * cos_ref[:, tcols] + pltpu.roll(xv, LANES // 2, axis=1) * sin_ref[:, tcols]
            if cb >= RET_QK_BLOCKS // 2:
                out = out * k_scale
            o_ref[:, cols] = out

    row = pl.BlockSpec((ROW_TILE, W), lambda i: (i, 0))
    tab = pl.BlockSpec((ROW_TILE, RET_QK_DIM), lambda i: (tidx(i), 0))
    return pl.pallas_call(
        body, name=name, grid=(T // ROW_TILE,), in_specs=[row, tab, tab], out_specs=row,
        out_shape=jax.ShapeDtypeStruct((T, W), F32), compiler_params=_params())(x, cos, sin)


ASSEMBLE_ROWS = 256


def _ret_grad_assemble(x_parts, c_parts, dg, cos, sin, seq, name):
    NX, NC = x_parts[0].shape[0], c_parts[0].shape[0]
    T = NX + NC
    rt = ASSEMBLE_ROWS
    nxt = NX // rt
    qk_w = RET_HEADS * RET_QK_DIM
    k_scale = RET_QK_DIM ** -0.5

    def unrotate(d, cos_ref, sin_ref, scale):
        outs = []
        for cb in range(qk_w // LANES):
            cols = slice(cb * LANES, (cb + 1) * LANES)
            tcols = slice((cb % 2) * LANES, (cb % 2 + 1) * LANES)
            dv_ = d[:, cols]
            o = dv_ * cos_ref[:, tcols] + pltpu.roll(dv_ * sin_ref[:, tcols], LANES // 2, axis=1)
            outs.append(o * scale if scale != 1.0 else o)
        return outs

    def body(dqf, dqb, dkf, dkb, dvf, dvb, dg_ref, dkcf, dkcb, dvcf, dvcb, cos_ref, sin_ref, o_ref):
        i = pl.program_id(0)

        def write_k(parts):
            for cb, o in enumerate(parts):
                o_ref[:, qk_w + cb * LANES:qk_w + (cb + 1) * LANES] = o.astype(o_ref.dtype)

        @pl.when(i < nxt)
        def _():
            for cb, o in enumerate(unrotate(dqf[...] + dqb[...], cos_ref, sin_ref, 1.0)):
                o_ref[:, cb * LANES:(cb + 1) * LANES] = o.astype(o_ref.dtype)
            write_k(unrotate(dkf[...] + dkb[...], cos_ref, sin_ref, k_scale))
            o_ref[:, 2 * qk_w:2 * qk_w + RET_VWIDTH] = (dvf[...] + dvb[...]).astype(o_ref.dtype)
            o_ref[:, 2 * qk_w + RET_VWIDTH:] = dg_ref[...].astype(o_ref.dtype)

        @pl.when(i >= nxt)
        def _():
            o_ref[:, :qk_w] = jnp.zeros((rt, qk_w), o_ref.dtype)
            write_k(unrotate(dkcf[...] + dkcb[...], cos_ref, sin_ref, k_scale))
            o_ref[:, 2 * qk_w:2 * qk_w + RET_VWIDTH] = (dvcf[...] + dvcb[...]).astype(o_ref.dtype)
            o_ref[:, 2 * qk_w + RET_VWIDTH:] = jnp.zeros((rt, RET_VWIDTH), o_ref.dtype)

    xs = lambda w: pl.BlockSpec((rt, w), lambda i: (jnp.minimum(i, nxt - 1), 0))
    cs = lambda w: pl.BlockSpec((rt, w), lambda i: (jnp.maximum(i - nxt, 0), 0))
    tab = pl.BlockSpec((rt, RET_QK_DIM), lambda i: (jnp.where(i < nxt, i % (seq // rt), seq // rt), 0))
    return pl.pallas_call(
        body, name=name, grid=(T // rt,),
        in_specs=[xs(qk_w)] * 4 + [xs(RET_VWIDTH)] * 3 + [cs(qk_w)] * 2 + [cs(RET_VWIDTH)] * 2 + [tab, tab],
        out_specs=pl.BlockSpec((rt, 2 * qk_w + 2 * RET_VWIDTH), lambda i: (i, 0)),
        out_shape=jax.ShapeDtypeStruct((T, 2 * qk_w + 2 * RET_VWIDTH), MXU_DTYPE),
        compiler_params=_params())(*x_parts, dg, *c_parts, cos, sin)


def _band_bias(qb, seq):
    nb = seq // qb
    assert nb >= 2
    i = jnp.arange(GQA_GROUP * qb, dtype=jnp.int32)[:, None] % qb
    n = jnp.arange(3 * qb, dtype=jnp.int32)[None, :]
    in_window = (n >= i) & (n - i <= 2 * WINDOW)
    variants = [in_window & (n >= qb), in_window, in_window & (n < 2 * qb)]
    return jnp.stack([jnp.where(v, 0.0, NEG_INF).astype(F32) for v in variants])


def _attn_probs(qv, kcv, klv, bias, sink_ref, kv_head, qb):
    rows = GQA_GROUP * qb
    s_c = _dot(qv, kcv, _NT)
    g = lax.broadcasted_iota(jnp.int32, (rows, 1), 0) // qb
    sink = jnp.zeros((rows, 1), F32)
    for gi in range(GQA_GROUP):
        sink = jnp.where(g == gi, sink_ref[kv_head, gi], sink)
    m = jnp.maximum(jnp.max(s_c, axis=-1, keepdims=True), sink)
    s_l = None
    if klv is not None:
        s_l = _dot(qv, klv, _NT) + bias
        m = jnp.maximum(m, jnp.max(s_l, axis=-1, keepdims=True))
    e_c = jnp.exp(s_c - m)
    e_s = jnp.exp(sink - m)
    den = jnp.sum(e_c, axis=-1, keepdims=True) + e_s
    e_l = None
    if klv is not None:
        e_l = jnp.exp(s_l - m)
        den = den + jnp.sum(e_l, axis=-1, keepdims=True)
    inv = 1.0 / den
    return e_c * inv, (None if e_l is None else e_l * inv), e_s * inv


def _attn_specs(B, seq, ctx_len, qb, has_local):
    nb = seq // qb
    q_spec = pl.BlockSpec((1, GQA_GROUP, qb, HEAD_DIM), lambda b, k, j: (b, k, j, 0))
    c_spec = pl.BlockSpec((1, 1, ctx_len, HEAD_DIM), lambda b, k, j: (b, k, 0, 0))
    local = []
    if has_local:
        local = [pl.BlockSpec((1, 1, qb, HEAD_DIM), lambda b, k, j: (b, k, jnp.maximum(j - 1, 0), 0)),
                 pl.BlockSpec((1, 1, qb, HEAD_DIM), lambda b, k, j: (b, k, j, 0)),
                 pl.BlockSpec((1, 1, qb, HEAD_DIM), lambda b, k, j: (b, k, jnp.minimum(j + 1, nb - 1), 0))]
        local = local + local + [pl.BlockSpec(
            (1, GQA_GROUP * qb, 3 * qb), lambda b, k, j: (jnp.where(j == 0, 0, jnp.where(j == nb - 1, 2, 1)), 0, 0))]
    return nb, q_spec, c_spec, local


def _attn_fwd(q, k, v, kc, vc, sink, name):
    B, _, seq, _ = q.shape
    ctx_len = kc.shape[2]
    has_local = k is not None
    qb = ATTN_BLOCK if has_local else seq
    nb, q_spec, c_spec, local = _attn_specs(B, seq, ctx_len, qb, has_local)

    def body(*refs):
        q_ref = refs[0]
        kc_ref, vc_ref, sink_ref, o_ref = refs[-4:]
        kv_head, j = pl.program_id(1), pl.program_id(2)
        qv = q_ref[0].reshape(GQA_GROUP * qb, HEAD_DIM)
        klv = vlv = bias = None
        if has_local:
            klv = jnp.concatenate([r[0, 0] for r in refs[1:4]], axis=0)
            vlv = jnp.concatenate([r[0, 0] for r in refs[4:7]], axis=0)
            bias = refs[7][0]
        p_c, p_l, _ = _attn_probs(qv, kc_ref[0, 0], klv, bias, sink_ref, kv_head, qb)
        o = _dot(p_c, vc_ref[0, 0], _NN)
        if has_local:
            o = o + _dot(p_l, vlv, _NN)
        o_ref[0] = o.reshape(GQA_GROUP, qb, HEAD_DIM)

    operands = [q] + ([k, k, k, v, v, v, _band_bias(qb, seq)] if has_local else []) + [kc, vc, sink]
    return pl.pallas_call(
        body, name=name, grid=(B, N_KV_HEADS, nb),
        in_specs=[q_spec] + local + [c_spec, c_spec, _SMEM],
        out_specs=q_spec, out_shape=jax.ShapeDtypeStruct(q.shape, F32), compiler_params=_params())(*operands)


def _attn_bwd(q, k, v, kc, vc, sink, do, name):
    B, _, seq, _ = q.shape
    ctx_len = kc.shape[2]
    has_local = k is not None
    qb = ATTN_BLOCK if has_local else seq
    nb, q_spec, c_spec, local = _attn_specs(B, seq, ctx_len, qb, has_local)

    def body(*refs):
        n_in = 1 + (7 if has_local else 0) + 4
        q_ref = refs[0]
        kc_ref, vc_ref, sink_ref, do_ref = refs[n_in - 4:n_in]
        outs = refs[n_in:]
        dq_ref = outs[0]
        dkc_ref, dvc_ref, dsink_ref = outs[-3:]
        b, kv_head, j = pl.program_id(0), pl.program_id(1), pl.program_id(2)
        rows = GQA_GROUP * qb
        qv = q_ref[0].reshape(rows, HEAD_DIM)
        dov = do_ref[0].reshape(rows, HEAD_DIM)
        kcv, vcv = kc_ref[0, 0], vc_ref[0, 0]
        klv = vlv = bias = None
        if has_local:
            klv = jnp.concatenate([r[0, 0] for r in refs[1:4]], axis=0)
            vlv = jnp.concatenate([r[0, 0] for r in refs[4:7]], axis=0)
            bias = refs[7][0]
        p_c, p_l, p_s = _attn_probs(qv, kcv, klv, bias, sink_ref, kv_head, qb)
        dp_c = _dot(dov, vcv, _NT)
        delta = jnp.sum(p_c * dp_c, axis=-1, keepdims=True)
        if has_local:
            dp_l = _dot(dov, vlv, _NT)
            delta = delta + jnp.sum(p_l * dp_l, axis=-1, keepdims=True)
        ds_c = p_c * (dp_c - delta)
        dq = _dot(ds_c, kcv, _NN)

        @pl.when(j == 0)
        def _():
            dkc_ref[...] = jnp.zeros_like(dkc_ref)
            dvc_ref[...] = jnp.zeros_like(dvc_ref)
            if has_local:
                outs[1][...] = jnp.zeros_like(outs[1])
                outs[2][...] = jnp.zeros_like(outs[2])

        @pl.when((b == 0) & (kv_head == 0) & (j == 0))
        def _():
            dsink_ref[...] = jnp.zeros_like(dsink_ref)

        dkc_ref[0, 0] += _dot(ds_c, qv, _TN)
        dvc_ref[0, 0] += _dot(p_c, dov, _TN)
        if has_local:
            ds_l = p_l * (dp_l - delta)
            dq = dq + _dot(ds_l, klv, _NN)
            dkl = _dot(ds_l, qv, _TN)
            dvl = _dot(p_l, dov, _TN)
            dk_ref, dv_ref = outs[1], outs[2]
            for t in range(3):
                def add(t=t):
                    start = pl.multiple_of((j - 1 + t) * qb, qb)
                    dk_ref[0, 0, pl.ds(start, qb), :] += dkl[t * qb:(t + 1) * qb]
                    dv_ref[0, 0, pl.ds(start, qb), :] += dvl[t * qb:(t + 1) * qb]
                if t == 0:
                    pl.when(j > 0)(add)
                elif t == 2:
                    pl.when(j < nb - 1)(add)
                else:
                    add()
        dq_ref[0] = dq.reshape(GQA_GROUP, qb, HEAD_DIM)
        dsk = -(p_s * delta)
        sub = lax.broadcasted_iota(jnp.int32, (8, LANES), 0)
        tile = jnp.zeros((8, LANES), F32)
        for gi in range(GQA_GROUP):
            tile = jnp.where(sub == gi, jnp.sum(dsk[gi * qb:(gi + 1) * qb]), tile)
        dsink_ref[pl.ds(pl.multiple_of(kv_head * 8, 8), 8), :] += tile

    full = pl.BlockSpec((1, 1, seq, HEAD_DIM), lambda b, k, j: (b, k, 0, 0))
    operands = [q] + ([k, k, k, v, v, v, _band_bias(qb, seq)] if has_local else []) + [kc, vc, sink, do]
    out_specs = [q_spec] + ([full, full] if has_local else []) + [c_spec, c_spec,
                                                                  pl.BlockSpec((32, LANES), lambda b, k, j: (0, 0))]
    out_shape = ([jax.ShapeDtypeStruct(q.shape, F32)]
                 + ([jax.ShapeDtypeStruct(k.shape, F32)] * 2 if has_local else [])
                 + [jax.ShapeDtypeStruct(kc.shape, F32)] * 2 + [jax.ShapeDtypeStruct((32, LANES), F32)])
    return pl.pallas_call(
        body, name=name, grid=(B, N_KV_HEADS, nb),
        in_specs=[q_spec] + local + [c_spec, c_spec, _SMEM, q_spec],
        out_specs=tuple(out_specs), out_shape=tuple(out_shape), compiler_params=_params())(*operands)


def _ret_decays(lg, rev):
    n = lax.broadcasted_iota(jnp.int32, (RET_CHUNK, RET_CHUNK), 0).astype(F32)
    m = lax.broadcasted_iota(jnp.int32, (RET_CHUNK, RET_CHUNK), 1).astype(F32)
    pos = lax.broadcasted_iota(jnp.int32, (RET_CHUNK, 1), 0).astype(F32)
    diff = (m - n) if rev else (n - m)
    a_exp = jnp.maximum(diff, 0.0)
    intra = jnp.where(diff >= 0, jnp.exp(lg * a_exp), 0.0)
    q_exp = (RET_CHUNK - pos) if rev else (pos + 1.0)
    k_exp = pos if rev else (RET_CHUNK - 1.0 - pos)
    chunk = jnp.exp(jnp.full((1, 1), RET_CHUNK, F32) * lg)
    return intra, a_exp, jnp.exp(lg * q_exp), q_exp, jnp.exp(lg * k_exp), k_exp, chunk


def _ctx_decay(lg, ctx_len, rev):
    t = lax.broadcasted_iota(jnp.int32, (ctx_len, 1), 0).astype(F32)
    expo = t if rev else (ctx_len - 1.0 - t)
    return jnp.exp(lg * expo), expo


def _ret_specs(B, seq, ctx_len, order):
    nc = seq // RET_CHUNK
    x_blocks = B * seq // ctx_len

    def rows(b, c):
        return b * nc + order(c, nc)

    q_spec = pl.BlockSpec((RET_CHUNK, RET_QK_DIM), lambda b, h, c: (rows(b, c), h))
    k_spec = pl.BlockSpec((RET_CHUNK, RET_QK_DIM), lambda b, h, c: (rows(b, c), RET_HEADS + h))
    v_spec = pl.BlockSpec((RET_CHUNK, RET_V_DIM), lambda b, h, c: (rows(b, c), RET_HEADS + h))
    kc_spec = pl.BlockSpec((ctx_len, RET_QK_DIM), lambda b, h, c: (x_blocks + b, RET_HEADS + h))
    vc_spec = pl.BlockSpec((ctx_len, RET_V_DIM), lambda b, h, c: (x_blocks + b, RET_HEADS + h))
    st_spec = pl.BlockSpec((1, 1, 1, RET_QK_DIM, RET_V_DIM), lambda b, h, c: (b, h, order(c, nc), 0, 0))
    o_spec = pl.BlockSpec((RET_CHUNK, RET_V_DIM), lambda b, h, c: (rows(b, c), h))
    return nc, q_spec, k_spec, v_spec, kc_spec, vc_spec, st_spec, o_spec


_SCAN_UP = lambda c, nc: c
_SCAN_DOWN = lambda c, nc: nc - 1 - c


def _ret_fwd(qk, qkvg, log_g, B, seq, ctx_len, name):
    nc, qf, kf, vf, kc_spec, vc_spec, stf, of = _ret_specs(B, seq, ctx_len, _SCAN_UP)
    _, qr, kr, vr, _, _, str_, or_ = _ret_specs(B, seq, ctx_len, _SCAN_DOWN)

    def body(lg_ref, qf_ref, kf_ref, vf_ref, qr_ref, kr_ref, vr_ref, kc_ref, vc_ref,
             of_ref, stf_ref, or_ref, str_ref, state_f, state_r):
        h, c = pl.program_id(1), pl.program_id(2)
        dirs = ((False, lg_ref[0, h], qf_ref, kf_ref, vf_ref, of_ref, stf_ref, state_f),
                (True, lg_ref[1, h], qr_ref, kr_ref, vr_ref, or_ref, str_ref, state_r))

        @pl.when(c == 0)
        def _():
            for rev, lg, _, _, _, _, _, state in dirs:
                dec, _ = _ctx_decay(lg, ctx_len, rev)
                state[...] = _dot(kc_ref[...] * dec, vc_ref[...], _TN)

        for rev, lg, q_ref, k_ref, v_ref, o_ref, st_ref, state in dirs:
            intra, _, q_dec, _, k_dec, _, chunk_dec = _ret_decays(lg, rev)
            qv, kv, vv = q_ref[...], k_ref[...], v_ref[...]
            s_in = state[...]
            st_ref[0, 0, 0] = s_in
            w = _dot(qv, kv, _NT) * intra
            o_ref[...] = _dot(w, vv, _NN) + _dot(qv, s_in, _NN) * q_dec
            state[...] = s_in * chunk_dec + _dot(kv * k_dec, vv, _TN)

    o_shape = jax.ShapeDtypeStruct((B * seq, RET_VWIDTH), F32)
    st_shape = jax.ShapeDtypeStruct((B, RET_HEADS, nc, RET_QK_DIM, RET_V_DIM), F32)
    return pl.pallas_call(
        body, name=name, grid=(B, RET_HEADS, nc),
        in_specs=[_SMEM, qf, kf, vf, qr, kr, vr, kc_spec, vc_spec],
        out_specs=(of, stf, or_, str_), out_shape=(o_shape, st_shape, o_shape, st_shape),
        scratch_shapes=[pltpu.VMEM((RET_QK_DIM, RET_V_DIM), F32)] * 2,
        compiler_params=_params())(log_g, qk, qk, qkvg, qk, qk, qkvg, qk, qkvg)


def _ret_bwd_chunk(rev, lg, q_ref, k_ref, v_ref, st_ref, do_ref, dq_ref, dk_ref, dv_ref, dlg_ref, dstate):
    intra, a_exp, q_dec, q_exp, k_dec, k_exp, chunk_dec = _ret_decays(lg, rev)
    qv, kv, vv, dov = q_ref[...], k_ref[...], v_ref[...], do_ref[...]
    s_in, ds_out = st_ref[0, 0, 0], dstate[...]
    p = _dot(qv, kv, _NT)
    w = p * intra
    dw = _dot(dov, vv, _NT)
    dp = dw * intra
    do_dec = dov * q_dec
    kd = kv * k_dec
    v_ds = _dot(vv, ds_out, _NT)
    dq_ref[...] = _dot(dp, kv, _NN) + _dot(do_dec, s_in, _NT)
    dk_ref[...] = _dot(dp, qv, _TN) + v_ds * k_dec
    dv_ref[...] = _dot(w, dov, _TN) + _dot(kd, ds_out, _NN)
    q_s = _dot(qv, s_in, _NN)
    dlg = (jnp.sum(dw * w * a_exp)
           + jnp.sum(q_exp * q_dec * jnp.sum(dov * q_s, axis=-1, keepdims=True))
           + jnp.sum(k_exp * k_dec * jnp.sum(kv * v_ds, axis=-1, keepdims=True))
           + RET_CHUNK * jnp.sum(chunk_dec * (ds_out * s_in)))
    ds_in = ds_out * chunk_dec + _dot(qv, do_dec, _TN)
    dstate[...] = ds_in
    dlg_ref[...] += dlg
    return ds_in


def _ret_bwd(qk, qkvg, log_g, st_f, st_r, do, B, seq, ctx_len, name):
    nc, qf, kf, vf, kc_spec, vc_spec, stf, of = _ret_specs(B, seq, ctx_len, _SCAN_DOWN)
    _, qr, kr, vr, _, _, str_, or_ = _ret_specs(B, seq, ctx_len, _SCAN_UP)

    def body(lg_ref, qf_ref, kf_ref, vf_ref, stf_ref, dof_ref, qr_ref, kr_ref, vr_ref, str_ref, dor_ref, kc_ref, vc_ref,
             dqf, dkf, dvf, dkcf, dvcf, dlgf, dqr, dkr, dvr, dkcr, dvcr, dlgr, dstate_f, dstate_r):
        h, c = pl.program_id(1), pl.program_id(2)
        dirs = ((False, lg_ref[0, h], (qf_ref, kf_ref, vf_ref, stf_ref, dof_ref, dqf, dkf, dvf, dlgf, dstate_f), dkcf, dvcf),
                (True, lg_ref[1, h], (qr_ref, kr_ref, vr_ref, str_ref, dor_ref, dqr, dkr, dvr, dlgr, dstate_r), dkcr, dvcr))

        @pl.when(c == 0)
        def _():
            for _, _, refs, _, _ in dirs:
                refs[-1][...] = jnp.zeros_like(refs[-1])
                refs[-2][...] = jnp.zeros_like(refs[-2])

        ds_first = [_ret_bwd_chunk(rev, lg, *refs) for rev, lg, refs, _, _ in dirs]

        @pl.when(c == nc - 1)
        def _():
            for (rev, lg, refs, dkc_ref, dvc_ref), ds_in in zip(dirs, ds_first):
                dec, expo = _ctx_decay(lg, ctx_len, rev)
                kcv, vcv = kc_ref[...], vc_ref[...]
                vc_ds = _dot(vcv, ds_in, _NT)
                dkc_ref[...] = vc_ds * dec
                dvc_ref[...] = _dot(kcv * dec, ds_in, _NN)
                refs[-2][...] += jnp.sum(expo * dec * jnp.sum(kcv * vc_ds, axis=-1, keepdims=True))

    def outs(q_spec, o_spec):
        return (pl.BlockSpec((RET_CHUNK, RET_QK_DIM), q_spec.index_map),
                pl.BlockSpec((RET_CHUNK, RET_QK_DIM), q_spec.index_map), o_spec,
                pl.BlockSpec((ctx_len, RET_QK_DIM), lambda b, h, c: (b, h)),
                pl.BlockSpec((ctx_len, RET_V_DIM), lambda b, h, c: (b, h)),
                pl.BlockSpec((1, 1, 8, LANES), lambda b, h, c: (b, h, 0, 0)))

    shapes = (jax.ShapeDtypeStruct((B * seq, RET_HEADS * RET_QK_DIM), F32),
              jax.ShapeDtypeStruct((B * seq, RET_HEADS * RET_QK_DIM), F32),
              jax.ShapeDtypeStruct((B * seq, RET_VWIDTH), F32),
              jax.ShapeDtypeStruct((B * ctx_len, RET_HEADS * RET_QK_DIM), F32),
              jax.ShapeDtypeStruct((B * ctx_len, RET_VWIDTH), F32),
              jax.ShapeDtypeStruct((B, RET_HEADS, 8, LANES), F32))
    res = pl.pallas_call(
        body, name=name, grid=(B, RET_HEADS, nc),
        in_specs=[_SMEM, qf, kf, vf, stf, of, qr, kr, vr, str_, or_, kc_spec, vc_spec],
        out_specs=outs(qf, of) + outs(qr, or_), out_shape=shapes + shapes,
        scratch_shapes=[pltpu.VMEM((RET_QK_DIM, RET_V_DIM), F32)] * 2,
        compiler_params=_params())(log_g, qk, qk, qkvg, st_f, do, qk, qk, qkvg, st_r, do, qk, qkvg)
    return res[:6], res[6:]


def _gated_out_fwd(o_f, o_b, qkvg, gn_gain, name):
    T = o_f.shape[0]
    g_off = (2 * RET_HEADS * RET_QK_DIM + RET_VWIDTH) // RET_V_DIM

    def body(of_ref, ob_ref, g_ref, gain_ref, z_ref):
        o = of_ref[...] + ob_ref[...]
        mu = jnp.mean(o, axis=-1, keepdims=True)
        var = jnp.mean(jnp.square(o - mu), axis=-1, keepdims=True)
        y = (o - mu) * lax.rsqrt(var + EPS) * gain_ref[...]
        gv = g_ref[...]
        z_ref[...] = (gv * jax.nn.sigmoid(gv) * y).astype(z_ref.dtype)

    blk = pl.BlockSpec((ROW_TILE, RET_V_DIM), lambda i, h: (i, h))
    return pl.pallas_call(
        body, name=name, grid=(T // ROW_TILE, RET_HEADS),
        in_specs=[blk, blk, pl.BlockSpec((ROW_TILE, RET_V_DIM), lambda i, h: (i, g_off + h)),
                  pl.BlockSpec((1, RET_V_DIM), lambda i, h: (0, h))],
        out_specs=blk, out_shape=jax.ShapeDtypeStruct((T, RET_VWIDTH), MXU_DTYPE),
        compiler_params=_params())(o_f, o_b, qkvg, gn_gain)


def _gated_out_bwd(dz, o_f, o_b, qkvg, gn_gain, name):
    T = o_f.shape[0]
    g_off = (2 * RET_HEADS * RET_QK_DIM + RET_VWIDTH) // RET_V_DIM

    def body(dz_ref, of_ref, ob_ref, g_ref, gain_ref, do_ref, dg_ref, dgain_ref):
        o = of_ref[...] + ob_ref[...]
        mu = jnp.mean(o, axis=-1, keepdims=True)
        var = jnp.mean(jnp.square(o - mu), axis=-1, keepdims=True)
        rstd = lax.rsqrt(var + EPS)
        yhat = (o - mu) * rstd
        gv, dzv = g_ref[...], dz_ref[...]
        sg = jax.nn.sigmoid(gv)
        dg_ref[...] = (dzv * (yhat * gain_ref[...]) * (sg * (1.0 + gv * (1.0 - sg)))).astype(dg_ref.dtype)
        dy = dzv * (gv * sg)

        @pl.when(pl.program_id(1) == 0)
        def _():
            dgain_ref[...] = jnp.zeros_like(dgain_ref)

        dgain_ref[...] += jnp.sum(dy * yhat, axis=0, keepdims=True)
        dyh = dy * gain_ref[...]
        do_ref[...] = rstd * (dyh - jnp.mean(dyh, axis=-1, keepdims=True)
                              - yhat * jnp.mean(dyh * yhat, axis=-1, keepdims=True))

    blk = pl.BlockSpec((ROW_TILE, RET_V_DIM), lambda h, i: (i, h))
    vec = pl.BlockSpec((1, RET_V_DIM), lambda h, i: (0, h))
    return pl.pallas_call(
        body, name=name, grid=(RET_HEADS, T // ROW_TILE),
        in_specs=[blk, blk, blk, pl.BlockSpec((ROW_TILE, RET_V_DIM), lambda h, i: (i, g_off + h)), vec],
        out_specs=(blk, blk, vec),
        out_shape=(jax.ShapeDtypeStruct((T, RET_VWIDTH), F32), jax.ShapeDtypeStruct((T, RET_VWIDTH), MXU_DTYPE),
                   jax.ShapeDtypeStruct((1, RET_VWIDTH), F32)),
        compiler_params=_params())(dz, o_f, o_b, qkvg, gn_gain)


def _adamw(w, m, v, parts, name):
    R, C = w.shape
    tr = _tile(R, (256, 128, 64, 32, 16, 8))
    n_parts = [p.shape[0] for p in parts]

    def body(*refs):
        w_ref, m_ref, v_ref = refs[:3]
        part_refs = refs[3:3 + len(parts)]
        g_ref, d_ref, nm_ref, nv_ref = refs[3 + len(parts):]
        g = None
        for ref, n in zip(part_refs, n_parts):
            for r in range(n):
                term = ref[r].astype(F32)
                g = term if g is None else g + term
        mn = ADAM_B1 * m_ref[...] + (1.0 - ADAM_B1) * g
        vn = ADAM_B2 * v_ref[...] + (1.0 - ADAM_B2) * jnp.square(g)
        m_hat = mn / (1.0 - ADAM_B1 ** ADAM_STEP)
        v_hat = vn / (1.0 - ADAM_B2 ** ADAM_STEP)
        g_ref[...] = g
        d_ref[...] = -ADAM_LR * (m_hat / (jnp.sqrt(v_hat) + ADAM_EPS) + ADAM_WD * w_ref[...])
        nm_ref[...] = mn
        nv_ref[...] = vn

    blk = pl.BlockSpec((tr, C), lambda i: (i, 0))
    part_specs = [pl.BlockSpec((n, tr, C), lambda i: (0, i, 0)) for n in n_parts]
    shp = jax.ShapeDtypeStruct((R, C), F32)
    return pl.pallas_call(
        body, name=name, grid=(R // tr,), in_specs=[blk, blk, blk] + part_specs,
        out_specs=(blk, blk, blk, blk), out_shape=(shp, shp, shp, shp),
        compiler_params=_params())(w, m, v, *parts)


def _sum_rows(parts, name):
    n, R, C = parts.shape
    tr = _tile(R, (256, 128, 64, 32, 16, 8))

    def body(p_ref, o_ref):
        acc = p_ref[0]
        for r in range(1, n):
            acc = acc + p_ref[r]
        o_ref[...] = acc

    return pl.pallas_call(
        body, name=name, grid=(R // tr,), in_specs=[pl.BlockSpec((n, tr, C), lambda i: (0, i, 0))],
        out_specs=pl.BlockSpec((tr, C), lambda i: (i, 0)), out_shape=jax.ShapeDtypeStruct((R, C), F32),
        compiler_params=_params())(parts)


def _my_coords():
    return lax.axis_index("x"), lax.axis_index("y"), lax.axis_index("c")


def _flip(coord, bit):
    return 1 - coord if bit else coord


def _all_gather(x2d, name):
    R, C = x2d.shape

    def body(x_ref, out_ref, send_sems, recv_sems, local_sem):
        x, y, c = _my_coords()
        me, sibling = (x, y, c), (x, y, 1 - c)
        chips = [(1 - x, y), (x, 1 - y), (1 - x, 1 - y)]

        def rows(px, py, pc):
            return out_ref.at[4 * px + 2 * py + pc]

        def copy(k, block, to, src=None):
            return pltpu.make_async_remote_copy(
                src_ref=rows(*block) if src is None else src, dst_ref=rows(*block),
                send_sem=send_sems.at[k], recv_sem=recv_sems.at[k], device_id=to, device_id_type=MESH)

        mine = pltpu.make_async_copy(x_ref, rows(*me), local_sem)
        mine.start()
        first = [copy(0, me, sibling, src=x_ref)]
        first += [copy(1 + j, me, (*chip, c), src=x_ref) for j, chip in enumerate(chips)]
        for cp in first:
            cp.start()
        passed = [copy(4 + j, (*chip, c), sibling) for j, chip in enumerate(chips)]
        for j, chip in enumerate(chips):
            copy(1 + j, (*chip, c), me).wait_recv()
            passed[j].start()
        copy(0, sibling, me).wait_recv()
        for j, chip in enumerate(chips):
            copy(4 + j, (*chip, 1 - c), me).wait_recv()
        for cp in first + passed:
            cp.wait_send()
        mine.wait()

    return pl.pallas_call(
        body, name=name, out_shape=jax.ShapeDtypeStruct((N_DEV, R, C), x2d.dtype),
        in_specs=[_ANY], out_specs=_ANY,
        scratch_shapes=[pltpu.SemaphoreType.DMA((7,)), pltpu.SemaphoreType.DMA((7,)), pltpu.SemaphoreType.DMA],
    )(x2d)


BIG_WEIGHTS = {
    "ffn_w_in": (2, (2, D_MODEL, 2 * D_FF)),
    "ffn_w_out": (1, (2, D_FF, D_MODEL)),
    "attn_w_qkv": (2, (1, D_MODEL, (N_HEADS + 2 * N_KV_HEADS) * HEAD_DIM)),
    "attn_w_o": (1, (1, N_HEADS * HEAD_DIM, D_MODEL)),
    "ret_w_qkvg": (2, (1, D_MODEL, 2 * D_MODEL + 2 * RET_VWIDTH)),
    "ret_gn_g": (2, (1, 1, RET_VWIDTH)),
    "ret_w_o": (1, (1, RET_VWIDTH, D_MODEL)),
}


def _join_shards(name, stacked):
    axis, full = BIG_WEIGHTS[name]
    if axis == 2:
        stacked = stacked.transpose(0, 2, 1, 3)
    return stacked.reshape(full)


def _split_shards(name, full_arr):
    axis, full = BIG_WEIGHTS[name]
    L, rows, cols = full
    if axis == 2:
        return full_arr.reshape(L, rows, N_DEV, cols // N_DEV).transpose(0, 2, 1, 3)
    return full_arr.reshape(L, N_DEV, rows // N_DEV, cols)


def _gather_shards(shards, name):
    n = len(shards)

    def body(*refs):
        x_refs, out_refs = refs[:n], refs[n:2 * n]
        send_sems, recv_sems, local_sems = refs[2 * n:]
        x, y, c = _my_coords()
        me, sibling = (x, y, c), (x, y, 1 - c)
        chips = [(1 - x, y), (x, 1 - y), (1 - x, 1 - y)]

        def rows(a, px, py, pc):
            return out_refs[a].at[:, 4 * px + 2 * py + pc]

        def copy(a, k, block, to, src=None):
            return pltpu.make_async_remote_copy(
                src_ref=rows(a, *block) if src is None else src, dst_ref=rows(a, *block),
                send_sem=send_sems.at[7 * a + k], recv_sem=recv_sems.at[7 * a + k], device_id=to, device_id_type=MESH)

        mine = [pltpu.make_async_copy(x_refs[a], rows(a, *me), local_sems.at[a]) for a in range(n)]
        for cp in mine:
            cp.start()
        first = []
        for a in range(n):
            first.append(copy(a, 0, me, sibling, src=x_refs[a]))
            first += [copy(a, 1 + j, me, (*chip, c), src=x_refs[a]) for j, chip in enumerate(chips)]
        for cp in first:
            cp.start()
        passed = []
        for j, chip in enumerate(chips):
            for a in range(n):
                copy(a, 1 + j, (*chip, c), me).wait_recv()
                fwd = copy(a, 4 + j, (*chip, c), sibling)
                fwd.start()
                passed.append(fwd)
        for a in range(n):
            copy(a, 0, sibling, me).wait_recv()
            for j, chip in enumerate(chips):
                copy(a, 4 + j, (*chip, 1 - c), me).wait_recv()
        for cp in first + passed:
            cp.wait_send()
        for cp in mine:
            cp.wait()

    return pl.pallas_call(
        body, name=name,
        out_shape=[jax.ShapeDtypeStruct((s.shape[0], N_DEV) + s.shape[1:], s.dtype) for s in shards],
        in_specs=[_ANY] * n, out_specs=[_ANY] * n,
        scratch_shapes=[pltpu.SemaphoreType.DMA((7 * n,)), pltpu.SemaphoreType.DMA((7 * n,)),
                        pltpu.SemaphoreType.DMA((n,))],
    )(*shards)


def _exchange_shards(arrs, masks, src_of, out_tail, name):
    n, nm = len(arrs), len(masks)

    def body(*refs):
        in_refs, out_refs = refs[:n], refs[n:2 * n]
        send_sems, recv_sems = refs[2 * n:]
        x, y, c = _my_coords()
        copies = []
        for a in range(n):
            for k, (bx, by, bc) in enumerate(masks):
                peer = (_flip(x, bx), _flip(y, by), _flip(c, bc))
                copies.append(pltpu.make_async_remote_copy(
                    src_ref=src_of(in_refs[a], peer, (x, y, c)), dst_ref=out_refs[a].at[k],
                    send_sem=send_sems.at[nm * a + k], recv_sem=recv_sems.at[nm * a + k],
                    device_id=peer, device_id_type=MESH))
        for cp in copies:
            cp.start()
        for cp in copies:
            cp.wait()

    return pl.pallas_call(
        body, name=name,
        out_shape=[jax.ShapeDtypeStruct((nm,) + out_tail(s), s.dtype) for s in arrs],
        in_specs=[_ANY] * n, out_specs=[_ANY] * n,
        scratch_shapes=[pltpu.SemaphoreType.DMA((nm * n,)), pltpu.SemaphoreType.DMA((nm * n,))],
    )(*arrs)


def _pair_sum(g, from_sibling, core, out_dtype, name):
    L, _, _, a, b = g.shape
    ta = _tile(a, (256, 128, 64, 32, 16, 8))

    def body(core_ref, g_ref, s_ref, o_ref):
        o_ref[...] = (g_ref[...] + s_ref[...]).astype(out_dtype)

    blk = pl.BlockSpec((1, 1, ta, b), lambda l, q, i, core_ref: (l, q, i, 0))
    return pl.pallas_call(
        body, name=name,
        grid_spec=pltpu.PrefetchScalarGridSpec(
            num_scalar_prefetch=1, grid=(L, 4, a // ta),
            in_specs=[pl.BlockSpec((1, 1, pl.Squeezed(), ta, b), lambda l, q, i, core_ref: (l, q, core_ref[0], i, 0)), blk],
            out_specs=blk),
        out_shape=jax.ShapeDtypeStruct((L, 4, a, b), out_dtype), compiler_params=_params())(core, g, from_sibling)


def _to_heads(a, B, n, heads):
    return a.reshape(B, n, heads, HEAD_DIM).transpose(0, 2, 1, 3)


def _from_heads(a):
    B, heads, n, _ = a.shape
    return a.transpose(0, 2, 1, 3).reshape(B * n, heads * HEAD_DIM)


def _mods(mod_x, mod_c, layer):
    both = jnp.concatenate([mod_x[:, layer], mod_c[layer][None]], axis=0)
    return [both[:, None, k * D_MODEL:(k + 1) * D_MODEL] for k in range(6)]


def _local_step(x, ctx, target, mod_x, mod_c, w, small):
    B, S, _ = x.shape
    L = ctx.shape[1]
    NX, NC = B * S, B * L
    T = NX + NC
    tiles_per_ex = S // ROW_TILE
    nxt = NX // ROW_TILE
    gidx = _group_index(nxt, tiles_per_ex, B)
    gidx_for = lambda rows: _group_index(NX // rows, S // rows, B)
    mm_rows = _tile(S, (MM_ROWS, ROW_TILE))
    tidx = lambda i: jnp.where(i < nxt, i % tiles_per_ex, tiles_per_ex)
    G = B + 1
    x0 = jnp.concatenate([x.reshape(NX, D_MODEL), ctx.reshape(NC, D_MODEL)], axis=0)
    acos, asin = [jnp.tile(t, (1, LANES // HEAD_DIM)) for t in _rope_tables(S, HEAD_DIM)]
    rcos, rsin = _rope_tables(S, RET_QK_DIM)
    sink = small["attn_sink"].reshape(N_KV_HEADS, GQA_GROUP)
    gains = jnp.stack([jnp.tile(small["attn_q_norm"].reshape(1, HEAD_DIM), (1, LANES // HEAD_DIM)),
                       jnp.tile(small["attn_k_norm"].reshape(1, HEAD_DIM), (1, LANES // HEAD_DIM))])
    log_g = jax.nn.log_sigmoid(small["ret_decay_logit"].reshape(2, RET_HEADS))
    n1, n2 = small["norm1_g"], small["norm2_g"]
    qd, kvd = N_HEADS * HEAD_DIM, N_KV_HEADS * HEAD_DIM

    m0 = _mods(mod_x, mod_c, 0)
    h1 = _norm_mod_fwd(x0, n1[0:1], m0[0], m0[1], gidx, "l0_norm1")
    qkv = _mm(h1, w["attn_w_qkv"][0], "nn", F32, "l0_qkv")
    qkv_r = _attn_prep_fwd(qkv, gains, acos, asin, tidx, "l0_qk_prep")
    q_x, k_x, v_x = (_to_heads(qkv_r[:NX, :qd], B, S, N_HEADS), _to_heads(qkv_r[:NX, qd:qd + kvd], B, S, N_KV_HEADS),
                     _to_heads(qkv_r[:NX, qd + kvd:], B, S, N_KV_HEADS))
    q_c, k_c, v_c = (_to_heads(qkv_r[NX:, :qd], B, L, N_HEADS), _to_heads(qkv_r[NX:, qd:qd + kvd], B, L, N_KV_HEADS),
                     _to_heads(qkv_r[NX:, qd + kvd:], B, L, N_KV_HEADS))
    o_x = _attn_fwd(q_x, k_x, v_x, k_c, v_c, sink, "l0_attn_x")
    o_c = _attn_fwd(q_c, None, None, k_c, v_c, sink, "l0_attn_c")
    o0 = jnp.concatenate([_from_heads(o_x), _from_heads(o_c)], axis=0).astype(MXU_DTYPE)
    mo0, x1 = _mm(o0, w["attn_w_o"][0], "nn", F32, "l0_attn_out", res=x0, gate=m0[2], gidx_for=gidx_for, gate_rows=mm_rows)
    h2 = _norm_mod_fwd(x1, n2[0:1], m0[3], m0[4], gidx, "l0_norm2")
    u0 = _mm(h2, w["ffn_w_in"][0], "nn", F32, "l0_ffn_in")
    a0 = _swiglu_fwd(u0, "l0_swiglu")
    f0, x2 = _mm(a0, w["ffn_w_out"][0], "nn", F32, "l0_ffn_out", res=x1, gate=m0[5], gidx_for=gidx_for, gate_rows=mm_rows)

    m1 = _mods(mod_x, mod_c, 1)
    g1 = _norm_mod_fwd(x2, n1[1:2], m1[0], m1[1], gidx, "l1_norm1")
    qkvg = _mm(g1, w["ret_w_qkvg"][0], "nn", F32, "l1_qkvg")
    qk = _ret_rope(qkvg, rcos, rsin, tidx, "l1_rope")
    of, st_f, ob, st_b = _ret_fwd(qk, qkvg, log_g, B, S, L, "l1_ret")
    gn = w["ret_gn_g"].reshape(1, RET_VWIDTH)
    z1 = _gated_out_fwd(of, ob, qkvg, gn, "l1_gated_out")
    xx2 = x2[:NX]
    gx = lambda i: i // tiles_per_ex
    m1x = [t[:B] for t in m1]
    mo1, y1 = _mm(z1, w["ret_w_o"][0], "nn", F32, "l1_ret_out", res=xx2, gate=m1x[2], gidx_for=gidx_for, gate_rows=mm_rows)
    k2 = _norm_mod_fwd(y1, n2[1:2], m1x[3], m1x[4], gx, "l1_norm2")
    u1 = _mm(k2, w["ffn_w_in"][1], "nn", F32, "l1_ffn_in")
    a1 = _swiglu_fwd(u1, "l1_swiglu")
    f1, y2 = _mm(a1, w["ffn_w_out"][1], "nn", F32, "l1_ffn_out", res=y1, gate=m1x[5], gidx_for=gidx_for, gate_rows=mm_rows)

    loss_tile, dy2 = _loss_fwd_bwd(y2, target.reshape(NX, D_MODEL), "loss")

    zg = jnp.zeros((1, 1, D_MODEL), F32)
    dz, dgate5_1 = _gate_bwd(dy2, f1, m1x[5], gx, B, "l1_ffn_gate_bwd")
    gw_ffn_out1 = _mm(a1, dz, "tn", F32, "l1_ffn_out_dw")
    da = _mm(dz, w["ffn_w_out"][1], "nt", F32, "l1_ffn_out_dx")
    du = _swiglu_bwd(da, u1, "l1_swiglu_bwd")
    gw_ffn_in1 = _mm(k2, du, "tn", F32, "l1_ffn_in_dw")
    dk2 = _mm(du, w["ffn_w_in"][1], "nt", F32, "l1_ffn_in_dx")
    dy1, dsh3_1, dsc4_1, dn2_1 = _norm_mod_bwd(dk2, y1, n2[1:2], m1x[4], dy2, gx, B, "l1_norm2_bwd")
    dzo, dgate2_1 = _gate_bwd(dy1, mo1, m1x[2], gx, B, "l1_ret_gate_bwd")
    gw_ret_o = _mm(z1, dzo, "tn", F32, "l1_ret_out_dw")
    dz1 = _mm(dzo, w["ret_w_o"][0], "nt", F32, "l1_ret_out_dx")
    do_r, dg_r, dgn = _gated_out_bwd(dz1, of, ob, qkvg, gn, "l1_gated_out_bwd")
    ((dq_f, dk_f, dv_f, dkc_f, dvc_f, dlg_f),
     (dq_b, dk_b, dv_b, dkc_b, dvc_b, dlg_b)) = _ret_bwd(qk, qkvg, log_g, st_f, st_b, do_r, B, S, L, "l1_ret_bwd")
    dqkvg = _ret_grad_assemble((dq_f, dq_b, dk_f, dk_b, dv_f, dv_b), (dkc_f, dkc_b, dvc_f, dvc_b), dg_r, rcos, rsin, S,
                               "l1_qkvg_grad")
    gw_ret_qkvg = _mm(g1, dqkvg, "tn", F32, "l1_qkvg_dw")
    dg1 = _mm(dqkvg, w["ret_w_qkvg"][0], "nt", F32, "l1_qkvg_dx")
    dres1 = jnp.concatenate([dy1, jnp.zeros((NC, D_MODEL), F32)], axis=0)
    dx2, dsh0_1, dsc1_1, dn1_1 = _norm_mod_bwd(dg1, x2, n1[1:2], m1[1], dres1, gidx, G, "l1_norm1_bwd")
    dlg = jnp.stack([jnp.sum(dlg_f[:, :, 0, 0], axis=0), jnp.sum(dlg_b[:, :, 0, 0], axis=0)])
    d_decay = (dlg * jax.nn.sigmoid(-small["ret_decay_logit"].reshape(2, RET_HEADS))).reshape(1, 2, RET_HEADS)

    dz, dgate5_0 = _gate_bwd(dx2, f0, m0[5], gidx, G, "l0_ffn_gate_bwd")
    gw_ffn_out0 = _mm(a0, dz, "tn", F32, "l0_ffn_out_dw")
    da = _mm(dz, w["ffn_w_out"][0], "nt", F32, "l0_ffn_out_dx")
    du = _swiglu_bwd(da, u0, "l0_swiglu_bwd")
    gw_ffn_in0 = _mm(h2, du, "tn", F32, "l0_ffn_in_dw")
    dh2 = _mm(du, w["ffn_w_in"][0], "nt", F32, "l0_ffn_in_dx")
    dx1, dsh3_0, dsc4_0, dn2_0 = _norm_mod_bwd(dh2, x1, n2[0:1], m0[4], dx2, gidx, G, "l0_norm2_bwd")
    dzo, dgate2_0 = _gate_bwd(dx1, mo0, m0[2], gidx, G, "l0_attn_gate_bwd")
    gw_attn_o = _mm(o0, dzo, "tn", F32, "l0_attn_out_dw")
    do0 = _mm(dzo, w["attn_w_o"][0], "nt", MXU_DTYPE, "l0_attn_out_dx")
    do_x, do_c = _to_heads(do0[:NX], B, S, N_HEADS), _to_heads(do0[NX:], B, L, N_HEADS)
    dq_x, dk_x, dv_x, dkc1, dvc1, dsink_x = _attn_bwd(q_x, k_x, v_x, k_c, v_c, sink, do_x, "l0_attn_x_bwd")
    dq_c, dkc2, dvc2, dsink_c = _attn_bwd(q_c, None, None, k_c, v_c, sink, do_c, "l0_attn_c_bwd")
    dqk = jnp.concatenate([
        jnp.concatenate([_from_heads(dq_x), _from_heads(dk_x)], axis=1),
        jnp.concatenate([_from_heads(dq_c), _from_heads(dkc1 + dkc2)], axis=1)], axis=0)
    dvv = jnp.concatenate([_from_heads(dv_x), _from_heads(dvc1 + dvc2)], axis=0)
    dqkv, dgains = _attn_prep_bwd(dqk, dvv, qkv, gains, acos, asin, tidx, "l0_qk_prep_bwd")
    gw_attn_qkv = _mm(h1, dqkv, "tn", F32, "l0_qkv_dw")
    dh1 = _mm(dqkv, w["attn_w_qkv"][0], "nt", F32, "l0_qkv_dx")
    dx0, dsh0_0, dsc1_0, dn1_0 = _norm_mod_bwd(dh1, x0, n1[0:1], m0[1], dx1, gidx, G, "l0_norm1_bwd")

    dgains = jnp.sum(dgains.reshape(ATTN_QK_BLOCKS, LANES // HEAD_DIM, HEAD_DIM), axis=1)
    dsink = (dsink_x + dsink_c).reshape(N_KV_HEADS, 8, LANES)[:, :GQA_GROUP, 0].reshape(1, N_HEADS)
    grads_big = {
        "ffn_w_in": jnp.stack([gw_ffn_in0, gw_ffn_in1]),
        "ffn_w_out": jnp.stack([gw_ffn_out0, gw_ffn_out1]),
        "attn_w_qkv": gw_attn_qkv[None],
        "attn_w_o": gw_attn_o[None],
        "ret_w_qkvg": gw_ret_qkvg[None],
        "ret_gn_g": dgn,
        "ret_w_o": gw_ret_o[None],
    }
    grads_small = {
        "norm1_g": jnp.concatenate([dn1_0, dn1_1], axis=0),
        "norm2_g": jnp.concatenate([dn2_0, dn2_1], axis=0),
        "attn_q_norm": jnp.sum(dgains[:ATTN_Q_BLOCKS], axis=0)[None],
        "attn_k_norm": jnp.sum(dgains[ATTN_Q_BLOCKS:ATTN_QK_BLOCKS], axis=0)[None],
        "attn_sink": dsink,
        "ret_decay_logit": d_decay,
    }

    def pad_g(t):
        return jnp.concatenate([t, zg], axis=0)

    d0 = jnp.concatenate([dsh0_0, dsc1_0, dgate2_0, dsh3_0, dsc4_0, dgate5_0], axis=2)[:, 0]
    d1 = jnp.concatenate([dsh0_1, dsc1_1, pad_g(dgate2_1), pad_g(dsh3_1), pad_g(dsc4_1), pad_g(dgate5_1)],
                         axis=2)[:, 0]
    dmod_x = jnp.stack([d0[:B], d1[:B]], axis=1)
    dmod_c = jnp.stack([d0[B], d1[B]], axis=0)
    return loss_tile, dx0[:NX].reshape(B, S, D_MODEL), grads_big, grads_small, dmod_x, dmod_c


SMALL_NAMES = ("c_ctx", "ada_b", "norm1_g", "norm2_g", "attn_q_norm", "attn_k_norm", "attn_sink", "ret_decay_logit")
ADA_ROWS = 64


def _pack_small(d, rows):
    flat = jnp.concatenate([d[k].reshape(-1) for k in SMALL_NAMES])
    n = rows * LANES
    return jnp.pad(flat, (0, n - flat.shape[0])).reshape(rows, LANES)


def _unpack_small(packed, shapes):
    flat = packed.reshape(-1)
    out, off = {}, 0
    for k in SMALL_NAMES:
        n = math.prod(shapes[k])
        out[k] = flat[off:off + n].reshape(shapes[k])
        off += n
    return out


def _gather_big_weights(weights):
    gathered = _gather_shards(
        [weights[k].reshape(1, 1, -1) if k == "ret_gn_g" else weights[k].astype(MXU_DTYPE) for k in BIG_WEIGHTS],
        "gather_weights")
    return {k: _join_shards(k, g) for k, g in zip(BIG_WEIGHTS, gathered)}


def _reduce_and_update_big(g_big, weights, mom1, mom2):
    mx_, my_, mc_ = _my_coords()
    my_chip = 2 * mx_ + my_
    names = list(BIG_WEIGHTS)
    split = []
    for k in names:
        s = _split_shards(k, g_big[k])
        split.append(s.reshape(s.shape[0], 4, 2, s.shape[2], s.shape[3]))
    from_sibling = _exchange_shards(
        split, [(0, 0, 1)], lambda ref, peer, me_: ref.at[:, :, peer[2]],
        lambda s: (s.shape[0], 4) + s.shape[3:], "rs_sibling")
    from_sibling = [t[0] for t in from_sibling]
    core = mc_.astype(jnp.int32).reshape(1)
    pair = [_pair_sum(g, s, core, MXU_DTYPE, "rs_pair_" + k) for k, g, s in zip(names, split, from_sibling)]
    from_chips = _exchange_shards(
        pair, [(1, 0, 0), (0, 1, 0), (1, 1, 0)], lambda ref, peer, me_: ref.at[:, 2 * peer[0] + peer[1]],
        lambda s: (s.shape[0],) + s.shape[2:], "rs_chips")
    big = {}
    for k, g, s, r in zip(names, split, from_sibling, from_chips):
        L_, _, _, a_, b_ = g.shape
        own_keep = lax.dynamic_index_in_dim(lax.dynamic_index_in_dim(g, my_chip, axis=1, keepdims=False), mc_, axis=1,
                                            keepdims=False)
        own_sib = lax.dynamic_index_in_dim(s, my_chip, axis=1, keepdims=False)
        rows = L_ * a_
        res = _adamw(weights[k].reshape(rows, b_), mom1[k].reshape(rows, b_), mom2[k].reshape(rows, b_),
                     [own_keep.reshape(1, rows, b_), own_sib.reshape(1, rows, b_), r.reshape(3, rows, b_)],
                     "adamw_" + k)
        big[k] = [t.reshape(weights[k].shape) for t in res]
    return big


def kernel(x, c, ctx, c_ctx, ada_w, ada_b, norm1_g, norm2_g, ffn_w_in, ffn_w_out, attn_w_qkv, attn_q_norm, attn_k_norm, attn_sink, attn_w_o, ret_w_qkvg, ret_decay_logit, ret_gn_g, ret_w_o, loss_target, m_c_ctx, m_ada_w, m_ada_b, m_norm1_g, m_norm2_g, m_ffn_w_in, m_ffn_w_out, m_attn_w_qkv, m_attn_q_norm, m_attn_k_norm, m_attn_sink, m_attn_w_o, m_ret_w_qkvg, m_ret_decay_logit, m_ret_gn_g, m_ret_w_o, v_c_ctx, v_ada_w, v_ada_b, v_norm1_g, v_norm2_g, v_ffn_w_in, v_ffn_w_out, v_attn_w_qkv, v_attn_q_norm, v_attn_k_norm, v_attn_sink, v_attn_w_o, v_ret_w_qkvg, v_ret_decay_logit, v_ret_gn_g, v_ret_w_o):
    weights = dict(c_ctx=c_ctx, ada_w=ada_w, ada_b=ada_b, norm1_g=norm1_g, norm2_g=norm2_g, ffn_w_in=ffn_w_in,
                   ffn_w_out=ffn_w_out, attn_w_qkv=attn_w_qkv, attn_q_norm=attn_q_norm, attn_k_norm=attn_k_norm,
                   attn_sink=attn_sink, attn_w_o=attn_w_o, ret_w_qkvg=ret_w_qkvg, ret_decay_logit=ret_decay_logit,
                   ret_gn_g=ret_gn_g, ret_w_o=ret_w_o)
    mom1 = dict(c_ctx=m_c_ctx, ada_w=m_ada_w, ada_b=m_ada_b, norm1_g=m_norm1_g, norm2_g=m_norm2_g, ffn_w_in=m_ffn_w_in,
                ffn_w_out=m_ffn_w_out, attn_w_qkv=m_attn_w_qkv, attn_q_norm=m_attn_q_norm, attn_k_norm=m_attn_k_norm,
                attn_sink=m_attn_sink, attn_w_o=m_attn_w_o, ret_w_qkvg=m_ret_w_qkvg, ret_decay_logit=m_ret_decay_logit,
                ret_gn_g=m_ret_gn_g, ret_w_o=m_ret_w_o)
    mom2 = dict(c_ctx=v_c_ctx, ada_w=v_ada_w, ada_b=v_ada_b, norm1_g=v_norm1_g, norm2_g=v_norm2_g, ffn_w_in=v_ffn_w_in,
                ffn_w_out=v_ffn_w_out, attn_w_qkv=v_attn_w_qkv, attn_q_norm=v_attn_q_norm, attn_k_norm=v_attn_k_norm,
                attn_sink=v_attn_sink, attn_w_o=v_attn_w_o, ret_w_qkvg=v_ret_w_qkvg, ret_decay_logit=v_ret_decay_logit,
                ret_gn_g=v_ret_gn_g, ret_w_o=v_ret_w_o)
    B = x.shape[0]
    mx_, my_, mc_ = _my_coords()
    me = 4 * mx_ + 2 * my_ + mc_
    my_chip = 2 * mx_ + my_
    ada_cols = ada_w.shape[2]

    w_full = _gather_big_weights(weights)

    c_all = _all_gather(jax.nn.silu(c), "gather_c").reshape(N_DEV * B, D_MODEL)
    cc_act = jax.nn.silu(c_ctx)[None]
    ada_in = jnp.concatenate([c_all, cc_act, jnp.zeros((ADA_ROWS - N_DEV * B - 1, D_MODEL), F32)], axis=0)
    ada_in = ada_in.astype(MXU_DTYPE)
    ada_w2 = jnp.concatenate([ada_w[0], ada_w[1]], axis=1)
    bias = lax.dynamic_slice_in_dim(ada_b.reshape(2, N_DEV, ada_cols), me, 1, axis=1).reshape(1, 2 * ada_cols)
    mod_cols = _mm(ada_in, ada_w2, "nn", F32, "ada_fwd", bias=bias)
    mod_all = _all_gather(mod_cols, "gather_mod")
    mod_all = mod_all.reshape(N_DEV, ADA_ROWS, 2, ada_cols).transpose(1, 2, 0, 3).reshape(ADA_ROWS, 2, N_DEV * ada_cols)
    mod_x = lax.dynamic_slice_in_dim(mod_all, me * B, B, axis=0)
    mod_c = mod_all[N_DEV * B]

    small = {k: weights[k] for k in SMALL_NAMES}
    loss_tile, grad_x, g_big, g_small, dmod_x, dmod_c = _local_step(x, ctx, loss_target, mod_x, mod_c, w_full, small)
    loss = lax.psum(loss_tile[0, 0], ("x", "y", "c"))

    n_mod = 2 * 6 * D_MODEL
    dm_rows = jnp.concatenate([dmod_x.reshape(B, n_mod), dmod_c.reshape(1, n_mod),
                               jnp.zeros((8 - B - 1, n_mod), F32)], axis=0)
    dm_all = _all_gather(dm_rows, "gather_dmod")
    dmc_tot = _sum_rows(dm_all[:, B:B + 1].reshape(N_DEV, 1, n_mod)[:, :, :].reshape(N_DEV, n_mod // LANES, LANES),
                        "sum_dmod_c").reshape(1, n_mod)
    dmod_rows = jnp.concatenate([dm_all[:, :B].reshape(N_DEV * B, n_mod), dmc_tot,
                                 jnp.zeros((ADA_ROWS - N_DEV * B - 1, n_mod), F32)], axis=0)
    dmod_mine = lax.dynamic_slice_in_dim(dmod_rows.reshape(ADA_ROWS, 2, N_DEV, ada_cols), me, 1, axis=2)
    dmod_mine = dmod_mine.reshape(ADA_ROWS, 2 * ada_cols).astype(MXU_DTYPE)
    g_ada2 = _mm(ada_in, dmod_mine, "tn", F32, "ada_dw")
    g_ada_w = jnp.stack([g_ada2[:, :ada_cols], g_ada2[:, ada_cols:]])
    dmc_mine = jnp.concatenate([dmod_mine[N_DEV * B:N_DEV * B + 1], jnp.zeros((7, 2 * ada_cols), MXU_DTYPE)], axis=0)
    dcc_part = _mm(dmc_mine, ada_w2, "nt", F32, "ada_dc")[0:1]
    g_ada_b = _sum_rows(dmod_rows[:, None, :].reshape(ADA_ROWS, n_mod // LANES, LANES), "sum_dmod_b").reshape(2, 6 * D_MODEL)
    sg = jax.nn.sigmoid(c_ctx)
    g_small["c_ctx"] = dcc_part.reshape(D_MODEL) * (sg * (1.0 + c_ctx * (1.0 - sg)))
    g_small["ada_b"] = g_ada_b * (1.0 / N_DEV)

    shapes = {k: weights[k].shape for k in SMALL_NAMES}
    n_small = sum(math.prod(s) for s in shapes.values())
    srows = -(-(-(-n_small // LANES)) // 8) * 8
    gs_all = _all_gather(_pack_small(g_small, srows), "gather_small_grads")
    sm = _adamw(_pack_small({k: weights[k] for k in SMALL_NAMES}, srows), _pack_small({k: mom1[k] for k in SMALL_NAMES}, srows),
                _pack_small({k: mom2[k] for k in SMALL_NAMES}, srows), [gs_all], "adamw_small")
    sm = [_unpack_small(t, shapes) for t in sm]

    ada_shape = ada_w.shape
    r2 = lambda t: t.reshape(ada_shape[0] * ada_shape[1], ada_shape[2])
    ada = [t.reshape(ada_shape) for t in _adamw(r2(ada_w), r2(m_ada_w), r2(v_ada_w), [r2(g_ada_w)[None]], "adamw_ada")]

    big = _reduce_and_update_big(g_big, weights, mom1, mom2)

    def pick(i, name):
        if name in BIG_WEIGHTS:
            return big[name][i]
        if name == "ada_w":
            return ada[i]
        return sm[i][name]

    order = ("c_ctx", "ada_w", "ada_b", "norm1_g", "norm2_g", "ffn_w_in", "ffn_w_out", "attn_w_qkv", "attn_q_norm",
             "attn_k_norm", "attn_sink", "attn_w_o", "ret_w_qkvg", "ret_decay_logit", "ret_gn_g", "ret_w_o")
    outs = [loss, grad_x]
    for i in range(4):
        outs += [pick(i, n) for n in order]
    return tuple(outs)
```

```python
import functools
import math

import jax
import jax.numpy as jnp
from jax import lax
from jax.experimental import pallas as pl
from jax.experimental.pallas import tpu as pltpu

F32 = jnp.float32
MXU_DTYPE = jnp.bfloat16

D_MODEL = 1024
HEAD_DIM = 64
N_HEADS = 16
N_KV_HEADS = 4
GQA_GROUP = 4
WINDOW = 128
ATTN_BLOCK = 128
RET_HEADS = 4
RET_QK_DIM = 256
RET_V_DIM = 512
RET_VWIDTH = 2048
RET_CHUNK = 128
D_FF = 2816
GRID_W = 64
ROPE_BASE = 10000.0
EPS = 1e-6
NEG_INF = -1e30

ADAM_LR = 0.001
ADAM_B1 = 0.9
ADAM_B2 = 0.999
ADAM_EPS = 1e-08
ADAM_WD = 0.01
ADAM_STEP = 10

N_DEV = 8
LANES = 128
ROW_TILE = 512
VMEM_LIMIT = 48 * 1024 * 1024

MESH = pl.DeviceIdType.MESH
_ANY = pl.BlockSpec(memory_space=pl.ANY)
_SMEM = pl.BlockSpec(memory_space=pltpu.SMEM)


def _params(**kw):
    return pltpu.CompilerParams(vmem_limit_bytes=VMEM_LIMIT, **kw)


def _mx(v):
    return v.astype(MXU_DTYPE)


def _dot(a, b, dims):
    return lax.dot_general(_mx(a), _mx(b), (dims, ((), ())), preferred_element_type=F32)


_NN = ((1,), (0,))
_NT = ((1,), (1,))
_TN = ((0,), (0,))


def _tile(n, cands):
    for c in cands:
        if n % c == 0:
            return c
    return n


def _big_tile(n, cap):
    if n <= cap:
        return n
    for t in range(cap - cap % LANES, 0, -LANES):
        if n % t == 0:
            return t
    return n


MM_ROWS = 1024
MM_COLS = 1408
MM_DEPTH = 2048


def _k_tile(k):
    return _big_tile(k, MM_DEPTH)


def _mm(a, b, mode, out_dtype, name, *, bias=None, res=None, gate=None, gidx_for=None, gate_rows=None):
    if mode == "nn":
        (M, K), (_, N) = a.shape, b.shape
    elif mode == "nt":
        (M, K), (N, _) = a.shape, b.shape
    else:
        (K, M), (_, N) = a.shape, b.shape
    if res is not None:
        tm, tn = gate_rows, _big_tile(N, 512)
        gidx = gidx_for(tm)
    else:
        tm = _big_tile(M, MM_COLS if mode == "tn" else MM_ROWS)
        tn = _big_tile(N, MM_COLS)
    tk = _k_tile(K)
    nk = K // tk
    dims = {"nn": _NN, "nt": _NT, "tn": _TN}[mode]
    a_spec = (pl.BlockSpec((tk, tm), lambda i, j, k: (k, i)) if mode == "tn"
              else pl.BlockSpec((tm, tk), lambda i, j, k: (i, k)))
    b_spec = (pl.BlockSpec((tn, tk), lambda i, j, k: (j, k)) if mode == "nt"
              else pl.BlockSpec((tk, tn), lambda i, j, k: (k, j)))
    o_spec = pl.BlockSpec((tm, tn), lambda i, j, k: (i, j))
    in_specs, operands = [a_spec, b_spec], [a, b]
    if bias is not None:
        in_specs.append(pl.BlockSpec((1, tn), lambda i, j, k: (0, j)))
        operands.append(bias)
    if res is not None:
        in_specs += [o_spec, pl.BlockSpec((1, 1, tn), lambda i, j, k: (gidx(i), 0, j))]
        operands += [res, gate]
        out_shape = (jax.ShapeDtypeStruct((M, N), F32), jax.ShapeDtypeStruct((M, N), F32))
        out_specs = (o_spec, o_spec)
    else:
        out_shape = jax.ShapeDtypeStruct((M, N), out_dtype)
        out_specs = o_spec

    def body(*refs):
        a_ref, b_ref = refs[0], refs[1]
        extra = refs[2:len(operands)]
        outs = refs[len(operands):]
        prod = _dot(a_ref[...], b_ref[...], dims)

        def finish(acc):
            if bias is not None:
                outs[0][...] = (acc + extra[0][...]).astype(out_dtype)
            elif res is not None:
                outs[0][...] = acc
                outs[1][...] = extra[0][...] + extra[1][0] * acc
            else:
                outs[0][...] = acc.astype(out_dtype)

        if nk == 1:
            finish(prod)
        else:
            acc_ref = outs[-1]
            outs = outs[:-1]
            k = pl.program_id(2)

            @pl.when(k == 0)
            def _():
                acc_ref[...] = prod

            @pl.when(k > 0)
            def _():
                acc_ref[...] += prod

            @pl.when(k == nk - 1)
            def _():
                finish(acc_ref[...])

    return pl.pallas_call(
        body, name=name, grid=(M // tm, N // tn, nk), in_specs=in_specs, out_specs=out_specs, out_shape=out_shape,
        scratch_shapes=[pltpu.VMEM((tm, tn), F32)] if nk > 1 else [],
        compiler_params=_params())(*operands)


def _group_index(n_x_tiles, tiles_per_example, n_examples):
    def gidx(i):
        return jnp.where(i < n_x_tiles, i // tiles_per_example, n_examples)
    return gidx


def _norm_mod_fwd(x, g, shift, scale, gidx, name):
    T, Dm = x.shape

    def body(x_ref, g_ref, sh_ref, sc_ref, h_ref):
        xv = x_ref[...]
        r = lax.rsqrt(jnp.mean(xv * xv, axis=-1, keepdims=True) + EPS)
        y = xv * r * g_ref[...]
        h_ref[...] = (y * (1.0 + sc_ref[0]) + sh_ref[0]).astype(h_ref.dtype)

    row = pl.BlockSpec((ROW_TILE, Dm), lambda i: (i, 0))
    mod = pl.BlockSpec((1, 1, Dm), lambda i: (gidx(i), 0, 0))
    return pl.pallas_call(
        body, name=name, grid=(T // ROW_TILE,),
        in_specs=[row, pl.BlockSpec((1, Dm), lambda i: (0, 0)), mod, mod],
        out_specs=row, out_shape=jax.ShapeDtypeStruct((T, Dm), MXU_DTYPE),
        compiler_params=_params())(x, g, shift, scale)


def _first_of_group(i, gidx):
    return jnp.logical_or(i == 0, gidx(i) != gidx(jnp.maximum(i - 1, 0)))


def _norm_mod_bwd(dh, x, g, scale, dres, gidx, n_groups, name):
    T, Dm = x.shape

    def body(dh_ref, x_ref, g_ref, sc_ref, dres_ref, dx_ref, dsh_ref, dsc_ref, dg_ref):
        i = pl.program_id(0)
        xv, dhv = x_ref[...], dh_ref[...]
        r = lax.rsqrt(jnp.mean(xv * xv, axis=-1, keepdims=True) + EPS)
        xn = xv * r
        y = xn * g_ref[...]

        @pl.when(_first_of_group(i, gidx))
        def _():
            dsh_ref[...] = jnp.zeros_like(dsh_ref)
            dsc_ref[...] = jnp.zeros_like(dsc_ref)

        @pl.when(i == 0)
        def _():
            dg_ref[...] = jnp.zeros_like(dg_ref)

        dsh_ref[0] += jnp.sum(dhv, axis=0, keepdims=True)
        dsc_ref[0] += jnp.sum(dhv * y, axis=0, keepdims=True)
        dy = dhv * (1.0 + sc_ref[0])
        dg_ref[...] += jnp.sum(dy * xn, axis=0, keepdims=True)
        dxn = dy * g_ref[...]
        dx = r * (dxn - xn * jnp.mean(dxn * xn, axis=-1, keepdims=True))
        dx_ref[...] = dres_ref[...] + dx

    row = pl.BlockSpec((ROW_TILE, Dm), lambda i: (i, 0))
    mod = pl.BlockSpec((1, 1, Dm), lambda i: (gidx(i), 0, 0))
    vec = pl.BlockSpec((1, Dm), lambda i: (0, 0))
    return pl.pallas_call(
        body, name=name, grid=(T // ROW_TILE,),
        in_specs=[row, row, vec, mod, row],
        out_specs=(row, mod, mod, vec),
        out_shape=(jax.ShapeDtypeStruct((T, Dm), F32), jax.ShapeDtypeStruct((n_groups, 1, Dm), F32),
                   jax.ShapeDtypeStruct((n_groups, 1, Dm), F32), jax.ShapeDtypeStruct((1, Dm), F32)),
        compiler_params=_params())(dh, x, g, scale, dres)


def _gate_bwd(dy, f, gate, gidx, n_groups, name):
    T, Dm = dy.shape

    def body(dy_ref, f_ref, gate_ref, dz_ref, dgate_ref):
        i = pl.program_id(0)
        dyv = dy_ref[...]

        @pl.when(_first_of_group(i, gidx))
        def _():
            dgate_ref[...] = jnp.zeros_like(dgate_ref)

        dgate_ref[0] += jnp.sum(dyv * f_ref[...], axis=0, keepdims=True)
        dz_ref[...] = (dyv * gate_ref[0]).astype(dz_ref.dtype)

    row = pl.BlockSpec((ROW_TILE, Dm), lambda i: (i, 0))
    mod = pl.BlockSpec((1, 1, Dm), lambda i: (gidx(i), 0, 0))
    return pl.pallas_call(
        body, name=name, grid=(T // ROW_TILE,), in_specs=[row, row, mod], out_specs=(row, mod),
        out_shape=(jax.ShapeDtypeStruct((T, Dm), MXU_DTYPE), jax.ShapeDtypeStruct((n_groups, 1, Dm), F32)),
        compiler_params=_params())(dy, f, gate)


SWIGLU_ROWS = 256


def _swiglu_fwd(u, name):
    T = u.shape[0]

    def body(u_ref, a_ref):
        gate, up = u_ref[:, :D_FF], u_ref[:, D_FF:]
        a_ref[...] = (gate * jax.nn.sigmoid(gate) * up).astype(a_ref.dtype)

    return pl.pallas_call(
        body, name=name, grid=(T // SWIGLU_ROWS,),
        in_specs=[pl.BlockSpec((SWIGLU_ROWS, 2 * D_FF), lambda i: (i, 0))],
        out_specs=pl.BlockSpec((SWIGLU_ROWS, D_FF), lambda i: (i, 0)),
        out_shape=jax.ShapeDtypeStruct((T, D_FF), MXU_DTYPE), compiler_params=_params())(u)


def _swiglu_bwd(da, u, name):
    T = u.shape[0]

    def body(da_ref, u_ref, du_ref):
        gate, up, dav = u_ref[:, :D_FF], u_ref[:, D_FF:], da_ref[...]
        sg = jax.nn.sigmoid(gate)
        du_ref[:, :D_FF] = (dav * up * (sg * (1.0 + gate * (1.0 - sg)))).astype(du_ref.dtype)
        du_ref[:, D_FF:] = (dav * gate * sg).astype(du_ref.dtype)

    return pl.pallas_call(
        body, name=name, grid=(T // SWIGLU_ROWS,),
        in_specs=[pl.BlockSpec((SWIGLU_ROWS, D_FF), lambda i: (i, 0)),
                  pl.BlockSpec((SWIGLU_ROWS, 2 * D_FF), lambda i: (i, 0))],
        out_specs=pl.BlockSpec((SWIGLU_ROWS, 2 * D_FF), lambda i: (i, 0)),
        out_shape=jax.ShapeDtypeStruct((T, 2 * D_FF), MXU_DTYPE), compiler_params=_params())(da, u)


def _loss_fwd_bwd(y, target, name):
    T, Dm = y.shape

    def body(y_ref, t_ref, loss_ref, dy_ref):
        err = y_ref[...] - t_ref[...]

        @pl.when(pl.program_id(0) == 0)
        def _():
            loss_ref[...] = jnp.zeros_like(loss_ref)

        loss_ref[...] += 0.5 * jnp.sum(jnp.mean(err * err, axis=-1, keepdims=True))
        dy_ref[...] = err * (1.0 / Dm)

    row = pl.BlockSpec((ROW_TILE, Dm), lambda i: (i, 0))
    return pl.pallas_call(
        body, name=name, grid=(T // ROW_TILE,), in_specs=[row, row],
        out_specs=(pl.BlockSpec((8, LANES), lambda i: (0, 0)), row),
        out_shape=(jax.ShapeDtypeStruct((8, LANES), F32), jax.ShapeDtypeStruct((T, Dm), F32)),
        compiler_params=_params())(y, target)


def _rope_tables(seq, head_dim):
    axis_dim = head_dim // 2
    half = axis_dim // 2
    pos = jnp.arange(seq, dtype=jnp.int32)
    row = (pos // GRID_W).astype(F32)[:, None]
    col = (pos % GRID_W).astype(F32)[:, None]
    inv = ROPE_BASE ** (-jnp.arange(0, axis_dim, 2, dtype=F32) / axis_dim)
    lane = jnp.arange(head_dim, dtype=jnp.int32)
    within = lane % axis_dim
    ang = jnp.where((lane // axis_dim == 0)[None, :], row, col) * inv[within % half][None, :]
    cos = jnp.cos(ang)
    sin = jnp.where((within < half)[None, :], -jnp.sin(ang), jnp.sin(ang))
    cos = jnp.concatenate([cos, jnp.ones((ROW_TILE, head_dim), F32)], axis=0)
    sin = jnp.concatenate([sin, jnp.zeros((ROW_TILE, head_dim), F32)], axis=0)
    return cos, sin


def _pair_swap(v, half):
    if 2 * half == LANES:
        return pltpu.roll(v, half, axis=1)
    lane = lax.broadcasted_iota(jnp.int32, v.shape, 1)
    return jnp.where((lane % (2 * half)) < half, pltpu.roll(v, LANES - half, axis=1), pltpu.roll(v, half, axis=1))


def _head_sum(v, ones_ref):
    hi = v.astype(MXU_DTYPE)
    lo = (v - hi.astype(F32)).astype(MXU_DTYPE)
    return (jnp.dot(hi, ones_ref[...], preferred_element_type=F32)
            + jnp.dot(lo, ones_ref[...], preferred_element_type=F32))


def _head_ones():
    lane = jnp.arange(LANES)
    return (lane[:, None] // HEAD_DIM == lane[None, :] // HEAD_DIM).astype(MXU_DTYPE)


ATTN_QK_BLOCKS = (N_HEADS + N_KV_HEADS) * HEAD_DIM // LANES
ATTN_ALL_BLOCKS = (N_HEADS + 2 * N_KV_HEADS) * HEAD_DIM // LANES
ATTN_Q_BLOCKS = N_HEADS * HEAD_DIM // LANES
ATTN_SCALE = HEAD_DIM ** -0.5


def _attn_prep_fwd(qkv, gains, cos, sin, tidx, name):
    T, W = qkv.shape

    def body(x_ref, g_ref, cos_ref, sin_ref, ones_ref, o_ref):
        for cb in range(ATTN_ALL_BLOCKS):
            cols = slice(cb * LANES, (cb + 1) * LANES)
            xv = x_ref[:, cols]
            if cb < ATTN_QK_BLOCKS:
                r = lax.rsqrt(_head_sum(xv * xv, ones_ref) * (1.0 / HEAD_DIM) + EPS)
                y = xv * r * g_ref[0 if cb < ATTN_Q_BLOCKS else 1]
                xv = y * cos_ref[...] + _pair_swap(y, HEAD_DIM // 4) * sin_ref[...]
                if cb < ATTN_Q_BLOCKS:
                    xv = xv * ATTN_SCALE
            o_ref[:, cols] = xv.astype(o_ref.dtype)

    row = pl.BlockSpec((ROW_TILE, W), lambda i: (i, 0))
    tab = pl.BlockSpec((ROW_TILE, LANES), lambda i: (tidx(i), 0))
    return pl.pallas_call(
        body, name=name, grid=(T // ROW_TILE,),
        in_specs=[row, pl.BlockSpec((2, 1, LANES), lambda i: (0, 0, 0)), tab, tab,
                  pl.BlockSpec((LANES, LANES), lambda i: (0, 0))],
        out_specs=row, out_shape=jax.ShapeDtypeStruct(qkv.shape, MXU_DTYPE),
        compiler_params=_params())(qkv, gains, cos, sin, _head_ones())


def _attn_prep_bwd(dqk, dv, qkv, gains, cos, sin, tidx, name):
    T, W = qkv.shape
    qk_w = ATTN_QK_BLOCKS * LANES

    def body(dqk_ref, dv_ref, x_ref, g_ref, cos_ref, sin_ref, ones_ref, o_ref, dg_ref):
        @pl.when(pl.program_id(0) == 0)
        def _():
            dg_ref[...] = jnp.zeros_like(dg_ref)

        for cb in range(ATTN_QK_BLOCKS):
            cols = slice(cb * LANES, (cb + 1) * LANES)
            xv, d = x_ref[:, cols], dqk_ref[:, cols]
            if cb < ATTN_Q_BLOCKS:
                d = d * ATTN_SCALE
            r = lax.rsqrt(_head_sum(xv * xv, ones_ref) * (1.0 / HEAD_DIM) + EPS)
            xn = xv * r
            dy = d * cos_ref[...] + _pair_swap(d * sin_ref[...], HEAD_DIM // 4)
            dg_ref[:, cols] += jnp.sum(dy * xn, axis=0, keepdims=True)
            dxn = dy * g_ref[0 if cb < ATTN_Q_BLOCKS else 1]
            dx = r * (dxn - xn * (_head_sum(dxn * xn, ones_ref) * (1.0 / HEAD_DIM)))
            o_ref[:, cols] = dx.astype(o_ref.dtype)
        o_ref[:, qk_w:] = dv_ref[...].astype(o_ref.dtype)

    row = lambda w: pl.BlockSpec((ROW_TILE, w), lambda i: (i, 0))
    tab = pl.BlockSpec((ROW_TILE, LANES), lambda i: (tidx(i), 0))
    return pl.pallas_call(
        body, name=name, grid=(T // ROW_TILE,),
        in_specs=[row(qk_w), row(W - qk_w), row(W), pl.BlockSpec((2, 1, LANES), lambda i: (0, 0, 0)), tab, tab,
                  pl.BlockSpec((LANES, LANES), lambda i: (0, 0))],
        out_specs=(row(W), pl.BlockSpec((1, qk_w), lambda i: (0, 0))),
        out_shape=(jax.ShapeDtypeStruct(qkv.shape, MXU_DTYPE), jax.ShapeDtypeStruct((1, qk_w), F32)),
        compiler_params=_params())(dqk, dv, qkv, gains, cos, sin, _head_ones())


RET_QK_BLOCKS = 2 * RET_HEADS * RET_QK_DIM // LANES


def _ret_rope(x, cos, sin, tidx, name):
    T = x.shape[0]
    W = RET_QK_BLOCKS * LANES
    k_scale = RET_QK_DIM ** -0.5

    def body(x_ref, cos_ref, sin_ref, o_ref):
        for cb in range(RET_QK_BLOCKS):
            cols = slice(cb * LANES, (cb + 1) * LANES)
            tcols = slice((cb % 2) * LANES, (cb % 2 + 1) * LANES)
            xv = x_ref[:, cols]
            out = xv * cos_ref[:, tcols] + pltpu.roll(xv, LANES // 2, axis=1) * sin_ref[:, tcols]
            if cb >= RET_QK_BLOCKS // 2:
                out = out * k_scale
            o_ref[:, cols] = out

    row = pl.BlockSpec((ROW_TILE, W), lambda i: (i, 0))
    tab = pl.BlockSpec((ROW_TILE, RET_QK_DIM), lambda i: (tidx(i), 0))
    return pl.pallas_call(
        body, name=name, grid=(T // ROW_TILE,), in_specs=[row, tab, tab], out_specs=row,
        out_shape=jax.ShapeDtypeStruct((T, W), F32), compiler_params=_params())(x, cos, sin)


ASSEMBLE_ROWS = 256


def _ret_grad_assemble(x_parts, c_parts, dg, cos, sin, seq, name):
    NX, NC = x_parts[0].shape[0], c_parts[0].shape[0]
    T = NX + NC
    rt = ASSEMBLE_ROWS
    nxt = NX // rt
    qk_w = RET_HEADS * RET_QK_DIM
    k_scale = RET_QK_DIM ** -0.5

    def unrotate(d, cos_ref, sin_ref, scale):
        outs = []
        for cb in range(qk_w // LANES):
            cols = slice(cb * LANES, (cb + 1) * LANES)
            tcols = slice((cb % 2) * LANES, (cb % 2 + 1) * LANES)
            dv_ = d[:, cols]
            o = dv_ * cos_ref[:, tcols] + pltpu.roll(dv_ * sin_ref[:, tcols], LANES // 2, axis=1)
            outs.append(o * scale if scale != 1.0 else o)
        return outs

    def body(dqf, dqb, dkf, dkb, dvf, dvb, dg_ref, dkcf, dkcb, dvcf, dvcb, cos_ref, sin_ref, o_ref):
        i = pl.program_id(0)

        def write_k(parts):
            for cb, o in enumerate(parts):
                o_ref[:, qk_w + cb * LANES:qk_w + (cb + 1) * LANES] = o.astype(o_ref.dtype)

        @pl.when(i < nxt)
        def _():
            for cb, o in enumerate(unrotate(dqf[...] + dqb[...], cos_ref, sin_ref, 1.0)):
                o_ref[:, cb * LANES:(cb + 1) * LANES] = o.astype(o_ref.dtype)
            write_k(unrotate(dkf[...] + dkb[...], cos_ref, sin_ref, k_scale))
            o_ref[:, 2 * qk_w:2 * qk_w + RET_VWIDTH] = (dvf[...] + dvb[...]).astype(o_ref.dtype)
            o_ref[:, 2 * qk_w + RET_VWIDTH:] = dg_ref[...].astype(o_ref.dtype)

        @pl.when(i >= nxt)
        def _():
            o_ref[:, :qk_w] = jnp.zeros((rt, qk_w), o_ref.dtype)
            write_k(unrotate(dkcf[...] + dkcb[...], cos_ref, sin_ref, k_scale))
            o_ref[:, 2 * qk_w:2 * qk_w + RET_VWIDTH] = (dvcf[...] + dvcb[...]).astype(o_ref.dtype)
            o_ref[:, 2 * qk_w + RET_VWIDTH:] = jnp.zeros((rt, RET_VWIDTH), o_ref.dtype)

    xs = lambda w: pl.BlockSpec((rt, w), lambda i: (jnp.minimum(i, nxt - 1), 0))
    cs = lambda w: pl.BlockSpec((rt, w), lambda i: (jnp.maximum(i - nxt, 0), 0))
    tab = pl.BlockSpec((rt, RET_QK_DIM), lambda i: (jnp.where(i < nxt, i % (seq // rt), seq // rt), 0))
    return pl.pallas_call(
        body, name=name, grid=(T // rt,),
        in_specs=[xs(qk_w)] * 4 + [xs(RET_VWIDTH)] * 3 + [cs(qk_w)] * 2 + [cs(RET_VWIDTH)] * 2 + [tab, tab],
        out_specs=pl.BlockSpec((rt, 2 * qk_w + 2 * RET_VWIDTH), lambda i: (i, 0)),
        out_shape=jax.ShapeDtypeStruct((T, 2 * qk_w + 2 * RET_VWIDTH), MXU_DTYPE),
        compiler_params=_params())(*x_parts, dg, *c_parts, cos, sin)


def _band_bias(qb, seq):
    nb = seq // qb
    assert nb >= 2
    i = jnp.arange(GQA_GROUP * qb, dtype=jnp.int32)[:, None] % qb
    n = jnp.arange(3 * qb, dtype=jnp.int32)[None, :]
    in_window = (n >= i) & (n - i <= 2 * WINDOW)
    variants = [in_window & (n >= qb), in_window, in_window & (n < 2 * qb)]
    return jnp.stack([jnp.where(v, 0.0, NEG_INF).astype(F32) for v in variants])


def _attn_probs(qv, kcv, klv, bias, sink_ref, kv_head, qb):
    rows = GQA_GROUP * qb
    s_c = _dot(qv, kcv, _NT)
    g = lax.broadcasted_iota(jnp.int32, (rows, 1), 0) // qb
    sink = jnp.zeros((rows, 1), F32)
    for gi in range(GQA_GROUP):
        sink = jnp.where(g == gi, sink_ref[kv_head, gi], sink)
    m = jnp.maximum(jnp.max(s_c, axis=-1, keepdims=True), sink)
    s_l = None
    if klv is not None:
        s_l = _dot(qv, klv, _NT) + bias
        m = jnp.maximum(m, jnp.max(s_l, axis=-1, keepdims=True))
    e_c = jnp.exp(s_c - m)
    e_s = jnp.exp(sink - m)
    den = jnp.sum(e_c, axis=-1, keepdims=True) + e_s
    e_l = None
    if klv is not None:
        e_l = jnp.exp(s_l - m)
        den = den + jnp.sum(e_l, axis=-1, keepdims=True)
    inv = 1.0 / den
    return e_c * inv, (None if e_l is None else e_l * inv), e_s * inv


GROUP_W = GQA_GROUP * HEAD_DIM
K_LANE_BLOCK = N_HEADS * HEAD_DIM // LANES
V_LANE_BLOCK = K_LANE_BLOCK + N_KV_HEADS * HEAD_DIM // LANES


def _split_heads(blk):
    return jnp.concatenate([blk[:, g * HEAD_DIM:(g + 1) * HEAD_DIM] for g in range(GQA_GROUP)], axis=0)


def _merge_heads(v, rows):
    return jnp.concatenate([v[g * rows:(g + 1) * rows] for g in range(GQA_GROUP)], axis=1)


def _pick_half(blk, odd):
    return jnp.where(odd, blk[:, HEAD_DIM:], blk[:, :HEAD_DIM])


def _place_half(u, odd):
    z = jnp.zeros_like(u)
    return jnp.where(odd, jnp.concatenate([z, u], axis=1), jnp.concatenate([u, z], axis=1))


def _attn_specs(B, seq, ctx_len, ctx_queries):
    ctx0 = B * seq // ctx_len
    if ctx_queries:
        qb, nb = ctx_len, 1
        qrow = lambda b, j: ctx0 + b
    else:
        qb, nb = ATTN_BLOCK, seq // ATTN_BLOCK
        qrow = lambda b, j: b * nb + j
    q_spec = pl.BlockSpec((qb, GROUP_W), lambda b, k, j: (qrow(b, j), k))
    c_specs = [pl.BlockSpec((ctx_len, LANES), lambda b, k, j: (ctx0 + b, K_LANE_BLOCK + k // 2)),
               pl.BlockSpec((ctx_len, LANES), lambda b, k, j: (ctx0 + b, V_LANE_BLOCK + k // 2))]
    local = []
    if not ctx_queries:
        near = [lambda j: jnp.maximum(j - 1, 0), lambda j: j, lambda j: jnp.minimum(j + 1, nb - 1)]
        for lane0 in (K_LANE_BLOCK, V_LANE_BLOCK):
            for f in near:
                local.append(pl.BlockSpec((qb, LANES), lambda b, k, j, f=f, lane0=lane0: (b * nb + f(j), lane0 + k // 2)))
        local.append(pl.BlockSpec(
            (1, GQA_GROUP * qb, 3 * qb), lambda b, k, j: (jnp.where(j == 0, 0, jnp.where(j == nb - 1, 2, 1)), 0, 0)))
    return qb, nb, qrow, q_spec, c_specs, local


def _attn_operands(refs, has_local, kv_head):
    odd = (kv_head % 2) == 1
    n_local = 7 if has_local else 0
    qv = _split_heads(refs[0][...])
    kcv = _pick_half(refs[1 + n_local][...], odd)
    vcv = _pick_half(refs[2 + n_local][...], odd)
    klv = vlv = bias = None
    if has_local:
        klv = _pick_half(jnp.concatenate([r[...] for r in refs[1:4]], axis=0), odd)
        vlv = _pick_half(jnp.concatenate([r[...] for r in refs[4:7]], axis=0), odd)
        bias = refs[7][0]
    return odd, qv, kcv, vcv, klv, vlv, bias


def _attn_fwd(qkv, sink, B, seq, ctx_len, ctx_queries, name):
    has_local = not ctx_queries
    qb, nb, _, q_spec, c_specs, local = _attn_specs(B, seq, ctx_len, ctx_queries)
    n_rows = B * (ctx_len if ctx_queries else seq)

    def body(*refs):
        sink_ref, o_ref = refs[-2:]
        kv_head = pl.program_id(1)
        _, qv, kcv, vcv, klv, vlv, bias = _attn_operands(refs, has_local, kv_head)
        p_c, p_l, _ = _attn_probs(qv, kcv, klv, bias, sink_ref, kv_head, qb)
        o = _dot(p_c, vcv, _NN)
        if has_local:
            o = o + _dot(p_l, vlv, _NN)
        o_ref[...] = _merge_heads(o, qb).astype(o_ref.dtype)

    operands = [qkv] + ([qkv] * 6 + [_band_bias(qb, seq)] if has_local else []) + [qkv, qkv, sink]
    return pl.pallas_call(
        body, name=name, grid=(B, N_KV_HEADS, nb),
        in_specs=[q_spec] + local + c_specs + [_SMEM],
        out_specs=pl.BlockSpec((qb, GROUP_W), lambda b, k, j: (b * nb + j, k)),
        out_shape=jax.ShapeDtypeStruct((n_rows, N_HEADS * HEAD_DIM), MXU_DTYPE), compiler_params=_params())(*operands)


def _attn_bwd(qkv, sink, do, B, seq, ctx_len, ctx_queries, name):
    has_local = not ctx_queries
    qb, nb, qrow, q_spec, c_specs, local = _attn_specs(B, seq, ctx_len, ctx_queries)
    n_rows = B * (ctx_len if ctx_queries else seq)

    def body(*refs):
        n_in = 1 + (7 if has_local else 0) + 4
        sink_ref, do_ref = refs[n_in - 2:n_in]
        outs = refs[n_in:]
        dq_ref = outs[0]
        dkc_ref, dvc_ref, dsink_ref = outs[-3:]
        b, kv_head, j = pl.program_id(0), pl.program_id(1), pl.program_id(2)
        odd, qv, kcv, vcv, klv, vlv, bias = _attn_operands(refs, has_local, kv_head)
        dov = _split_heads(do_ref[...])
        p_c, p_l, p_s = _attn_probs(qv, kcv, klv, bias, sink_ref, kv_head, qb)
        dp_c = _dot(dov, vcv, _NT)
        delta = jnp.sum(p_c * dp_c, axis=-1, keepdims=True)
        if has_local:
            dp_l = _dot(dov, vlv, _NT)
            delta = delta + jnp.sum(p_l * dp_l, axis=-1, keepdims=True)
        ds_c = p_c * (dp_c - delta)
        dq = _dot(ds_c, kcv, _NN)

        @pl.when((kv_head % 2 == 0) & (j == 0))
        def _():
            dkc_ref[...] = jnp.zeros_like(dkc_ref)
            dvc_ref[...] = jnp.zeros_like(dvc_ref)
            if has_local:
                outs[1][...] = jnp.zeros_like(outs[1])
                outs[2][...] = jnp.zeros_like(outs[2])

        @pl.when((b == 0) & (kv_head == 0) & (j == 0))
        def _():
            dsink_ref[...] = jnp.zeros_like(dsink_ref)

        dkc_ref[...] += _place_half(_dot(ds_c, qv, _TN), odd)
        dvc_ref[...] += _place_half(_dot(p_c, dov, _TN), odd)
        if has_local:
            ds_l = p_l * (dp_l - delta)
            dq = dq + _dot(ds_l, klv, _NN)
            dkl = _place_half(_dot(ds_l, qv, _TN), odd)
            dvl = _place_half(_dot(p_l, dov, _TN), odd)
            dk_ref, dv_ref = outs[1], outs[2]
            for t in range(3):
                def add(t=t):
                    start = pl.multiple_of((j - 1 + t) * qb, qb)
                    dk_ref[pl.ds(start, qb), :] += dkl[t * qb:(t + 1) * qb]
                    dv_ref[pl.ds(start, qb), :] += dvl[t * qb:(t + 1) * qb]
                if t == 0:
                    pl.when(j > 0)(add)
                elif t == 2:
                    pl.when(j < nb - 1)(add)
                else:
                    add()
        dq_ref[...] = _merge_heads(dq, qb)
        dsk = -(p_s * delta)
        sub = lax.broadcasted_iota(jnp.int32, (8, LANES), 0)
        tile = jnp.zeros((8, LANES), F32)
        for gi in range(GQA_GROUP):
            tile = jnp.where(sub == gi, jnp.sum(dsk[gi * qb:(gi + 1) * qb]), tile)
        dsink_ref[pl.ds(pl.multiple_of(kv_head * 8, 8), 8), :] += tile

    kv_w = N_KV_HEADS * HEAD_DIM
    seq_spec = pl.BlockSpec((seq, LANES), lambda b, k, j: (b, k // 2))
    ctx_spec = pl.BlockSpec((ctx_len, LANES), lambda b, k, j: (b, k // 2))
    do_spec = pl.BlockSpec((qb, GROUP_W), lambda b, k, j: (qrow(b, j), k))
    operands = [qkv] + ([qkv] * 6 + [_band_bias(qb, seq)] if has_local else []) + [qkv, qkv, sink, do]
    out_specs = ([pl.BlockSpec((qb, GROUP_W), lambda b, k, j: (b * nb + j, k))] + ([seq_spec, seq_spec] if has_local else [])
                 + [ctx_spec, ctx_spec, pl.BlockSpec((32, LANES), lambda b, k, j: (0, 0))])
    out_shape = ([jax.ShapeDtypeStruct((n_rows, N_HEADS * HEAD_DIM), F32)]
                 + ([jax.ShapeDtypeStruct((B * seq, kv_w), F32)] * 2 if has_local else [])
                 + [jax.ShapeDtypeStruct((B * ctx_len, kv_w), F32)] * 2 + [jax.ShapeDtypeStruct((32, LANES), F32)])
    return pl.pallas_call(
        body, name=name, grid=(B, N_KV_HEADS, nb),
        in_specs=[q_spec] + local + c_specs + [_SMEM, do_spec],
        out_specs=tuple(out_specs), out_shape=tuple(out_shape), compiler_params=_params())(*operands)


def _ret_decays(lg, rev):
    n = lax.broadcasted_iota(jnp.int32, (RET_CHUNK, RET_CHUNK), 0).astype(F32)
    m = lax.broadcasted_iota(jnp.int32, (RET_CHUNK, RET_CHUNK), 1).astype(F32)
    pos = lax.broadcasted_iota(jnp.int32, (RET_CHUNK, 1), 0).astype(F32)
    diff = (m - n) if rev else (n - m)
    a_exp = jnp.maximum(diff, 0.0)
    intra = jnp.where(diff >= 0, jnp.exp(lg * a_exp), 0.0)
    q_exp = (RET_CHUNK - pos) if rev else (pos + 1.0)
    k_exp = pos if rev else (RET_CHUNK - 1.0 - pos)
    chunk = jnp.exp(jnp.full((1, 1), RET_CHUNK, F32) * lg)
    return intra, a_exp, jnp.exp(lg * q_exp), q_exp, jnp.exp(lg * k_exp), k_exp, chunk


def _ctx_decay(lg, ctx_len, rev):
    t = lax.broadcasted_iota(jnp.int32, (ctx_len, 1), 0).astype(F32)
    expo = t if rev else (ctx_len - 1.0 - t)
    return jnp.exp(lg * expo), expo


def _ret_specs(B, seq, ctx_len, order):
    nc = seq // RET_CHUNK
    x_blocks = B * seq // ctx_len

    def rows(b, c):
        return b * nc + order(c, nc)

    q_spec = pl.BlockSpec((RET_CHUNK, RET_QK_DIM), lambda b, h, c: (rows(b, c), h))
    k_spec = pl.BlockSpec((RET_CHUNK, RET_QK_DIM), lambda b, h, c: (rows(b, c), RET_HEADS + h))
    v_spec = pl.BlockSpec((RET_CHUNK, RET_V_DIM), lambda b, h, c: (rows(b, c), RET_HEADS + h))
    kc_spec = pl.BlockSpec((ctx_len, RET_QK_DIM), lambda b, h, c: (x_blocks + b, RET_HEADS + h))
    vc_spec = pl.BlockSpec((ctx_len, RET_V_DIM), lambda b, h, c: (x_blocks + b, RET_HEADS + h))
    st_spec = pl.BlockSpec((1, 1, 1, RET_QK_DIM, RET_V_DIM), lambda b, h, c: (b, h, order(c, nc), 0, 0))
    o_spec = pl.BlockSpec((RET_CHUNK, RET_V_DIM), lambda b, h, c: (rows(b, c), h))
    return nc, q_spec, k_spec, v_spec, kc_spec, vc_spec, st_spec, o_spec


_SCAN_UP = lambda c, nc: c
_SCAN_DOWN = lambda c, nc: nc - 1 - c


def _ret_fwd(qk, qkvg, log_g, B, seq, ctx_len, name):
    nc, qf, kf, vf, kc_spec, vc_spec, stf, of = _ret_specs(B, seq, ctx_len, _SCAN_UP)
    _, qr, kr, vr, _, _, str_, or_ = _ret_specs(B, seq, ctx_len, _SCAN_DOWN)

    def body(lg_ref, qf_ref, kf_ref, vf_ref, qr_ref, kr_ref, vr_ref, kc_ref, vc_ref,
             of_ref, stf_ref, or_ref, str_ref, state_f, state_r):
        h, c = pl.program_id(1), pl.program_id(2)
        dirs = ((False, lg_ref[0, h], qf_ref, kf_ref, vf_ref, of_ref, stf_ref, state_f),
                (True, lg_ref[1, h], qr_ref, kr_ref, vr_ref, or_ref, str_ref, state_r))

        @pl.when(c == 0)
        def _():
            for rev, lg, _, _, _, _, _, state in dirs:
                dec, _ = _ctx_decay(lg, ctx_len, rev)
                state[...] = _dot(kc_ref[...] * dec, vc_ref[...], _TN)

        for rev, lg, q_ref, k_ref, v_ref, o_ref, st_ref, state in dirs:
            intra, _, q_dec, _, k_dec, _, chunk_dec = _ret_decays(lg, rev)
            qv, kv, vv = q_ref[...], k_ref[...], v_ref[...]
            s_in = state[...]
            st_ref[0, 0, 0] = s_in
            w = _dot(qv, kv, _NT) * intra
            o_ref[...] = _dot(w, vv, _NN) + _dot(qv, s_in, _NN) * q_dec
            state[...] = s_in * chunk_dec + _dot(kv * k_dec, vv, _TN)

    o_shape = jax.ShapeDtypeStruct((B * seq, RET_VWIDTH), F32)
    st_shape = jax.ShapeDtypeStruct((B, RET_HEADS, nc, RET_QK_DIM, RET_V_DIM), F32)
    return pl.pallas_call(
        body, name=name, grid=(B, RET_HEADS, nc),
        in_specs=[_SMEM, qf, kf, vf, qr, kr, vr, kc_spec, vc_spec],
        out_specs=(of, stf, or_, str_), out_shape=(o_shape, st_shape, o_shape, st_shape),
        scratch_shapes=[pltpu.VMEM((RET_QK_DIM, RET_V_DIM), F32)] * 2,
        compiler_params=_params())(log_g, qk, qk, qkvg, qk, qk, qkvg, qk, qkvg)


def _ret_bwd_chunk(rev, lg, q_ref, k_ref, v_ref, st_ref, do_ref, dq_ref, dk_ref, dv_ref, dlg_ref, dstate):
    intra, a_exp, q_dec, q_exp, k_dec, k_exp, chunk_dec = _ret_decays(lg, rev)
    qv, kv, vv, dov = q_ref[...], k_ref[...], v_ref[...], do_ref[...]
    s_in, ds_out = st_ref[0, 0, 0], dstate[...]
    p = _dot(qv, kv, _NT)
    w = p * intra
    dw = _dot(dov, vv, _NT)
    dp = dw * intra
    do_dec = dov * q_dec
    kd = kv * k_dec
    v_ds = _dot(vv, ds_out, _NT)
    dq_ref[...] = _dot(dp, kv, _NN) + _dot(do_dec, s_in, _NT)
    dk_ref[...] = _dot(dp, qv, _TN) + v_ds * k_dec
    dv_ref[...] = _dot(w, dov, _TN) + _dot(kd, ds_out, _NN)
    q_s = _dot(qv, s_in, _NN)
    dlg = (jnp.sum(dw * w * a_exp)
           + jnp.sum(q_exp * q_dec * jnp.sum(dov * q_s, axis=-1, keepdims=True))
           + jnp.sum(k_exp * k_dec * jnp.sum(kv * v_ds, axis=-1, keepdims=True))
           + RET_CHUNK * jnp.sum(chunk_dec * (ds_out * s_in)))
    ds_in = ds_out * chunk_dec + _dot(qv, do_dec, _TN)
    dstate[...] = ds_in
    dlg_ref[...] += dlg
    return ds_in


def _ret_bwd(qk, qkvg, log_g, st_f, st_r, do, B, seq, ctx_len, name):
    nc, qf, kf, vf, kc_spec, vc_spec, stf, of = _ret_specs(B, seq, ctx_len, _SCAN_DOWN)
    _, qr, kr, vr, _, _, str_, or_ = _ret_specs(B, seq, ctx_len, _SCAN_UP)

    def body(lg_ref, qf_ref, kf_ref, vf_ref, stf_ref, dof_ref, qr_ref, kr_ref, vr_ref, str_ref, dor_ref, kc_ref, vc_ref,
             dqf, dkf, dvf, dkcf, dvcf, dlgf, dqr, dkr, dvr, dkcr, dvcr, dlgr, dstate_f, dstate_r):
        h, c = pl.program_id(1), pl.program_id(2)
        dirs = ((False, lg_ref[0, h], (qf_ref, kf_ref, vf_ref, stf_ref, dof_ref, dqf, dkf, dvf, dlgf, dstate_f), dkcf, dvcf),
                (True, lg_ref[1, h], (qr_ref, kr_ref, vr_ref, str_ref, dor_ref, dqr, dkr, dvr, dlgr, dstate_r), dkcr, dvcr))

        @pl.when(c == 0)
        def _():
            for _, _, refs, _, _ in dirs:
                refs[-1][...] = jnp.zeros_like(refs[-1])
                refs[-2][...] = jnp.zeros_like(refs[-2])

        ds_first = [_ret_bwd_chunk(rev, lg, *refs) for rev, lg, refs, _, _ in dirs]

        @pl.when(c == nc - 1)
        def _():
            for (rev, lg, refs, dkc_ref, dvc_ref), ds_in in zip(dirs, ds_first):
                dec, expo = _ctx_decay(lg, ctx_len, rev)
                kcv, vcv = kc_ref[...], vc_ref[...]
                vc_ds = _dot(vcv, ds_in, _NT)
                dkc_ref[...] = vc_ds * dec
                dvc_ref[...] = _dot(kcv * dec, ds_in, _NN)
                refs[-2][...] += jnp.sum(expo * dec * jnp.sum(kcv * vc_ds, axis=-1, keepdims=True))

    def outs(q_spec, o_spec):
        return (pl.BlockSpec((RET_CHUNK, RET_QK_DIM), q_spec.index_map),
                pl.BlockSpec((RET_CHUNK, RET_QK_DIM), q_spec.index_map), o_spec,
                pl.BlockSpec((ctx_len, RET_QK_DIM), lambda b, h, c: (b, h)),
                pl.BlockSpec((ctx_len, RET_V_DIM), lambda b, h, c: (b, h)),
                pl.BlockSpec((1, 1, 8, LANES), lambda b, h, c: (b, h, 0, 0)))

    shapes = (jax.ShapeDtypeStruct((B * seq, RET_HEADS * RET_QK_DIM), F32),
              jax.ShapeDtypeStruct((B * seq, RET_HEADS * RET_QK_DIM), F32),
              jax.ShapeDtypeStruct((B * seq, RET_VWIDTH), F32),
              jax.ShapeDtypeStruct((B * ctx_len, RET_HEADS * RET_QK_DIM), F32),
              jax.ShapeDtypeStruct((B * ctx_len, RET_VWIDTH), F32),
              jax.ShapeDtypeStruct((B, RET_HEADS, 8, LANES), F32))
    res = pl.pallas_call(
        body, name=name, grid=(B, RET_HEADS, nc),
        in_specs=[_SMEM, qf, kf, vf, stf, of, qr, kr, vr, str_, or_, kc_spec, vc_spec],
        out_specs=outs(qf, of) + outs(qr, or_), out_shape=shapes + shapes,
        scratch_shapes=[pltpu.VMEM((RET_QK_DIM, RET_V_DIM), F32)] * 2,
        compiler_params=_params())(log_g, qk, qk, qkvg, st_f, do, qk, qk, qkvg, st_r, do, qk, qkvg)
    return res[:6], res[6:]


def _gated_out_fwd(o_f, o_b, qkvg, gn_gain, name):
    T = o_f.shape[0]
    g_off = (2 * RET_HEADS * RET_QK_DIM + RET_VWIDTH) // RET_V_DIM

    def body(of_ref, ob_ref, g_ref, gain_ref, z_ref):
        o = of_ref[...] + ob_ref[...]
        mu = jnp.mean(o, axis=-1, keepdims=True)
        var = jnp.mean(jnp.square(o - mu), axis=-1, keepdims=True)
        y = (o - mu) * lax.rsqrt(var + EPS) * gain_ref[...]
        gv = g_ref[...]
        z_ref[...] = (gv * jax.nn.sigmoid(gv) * y).astype(z_ref.dtype)

    blk = pl.BlockSpec((ROW_TILE, RET_V_DIM), lambda i, h: (i, h))
    return pl.pallas_call(
        body, name=name, grid=(T // ROW_TILE, RET_HEADS),
        in_specs=[blk, blk, pl.BlockSpec((ROW_TILE, RET_V_DIM), lambda i, h: (i, g_off + h)),
                  pl.BlockSpec((1, RET_V_DIM), lambda i, h: (0, h))],
        out_specs=blk, out_shape=jax.ShapeDtypeStruct((T, RET_VWIDTH), MXU_DTYPE),
        compiler_params=_params())(o_f, o_b, qkvg, gn_gain)


def _gated_out_bwd(dz, o_f, o_b, qkvg, gn_gain, name):
    T = o_f.shape[0]
    g_off = (2 * RET_HEADS * RET_QK_DIM + RET_VWIDTH) // RET_V_DIM

    def body(dz_ref, of_ref, ob_ref, g_ref, gain_ref, do_ref, dg_ref, dgain_ref):
        o = of_ref[...] + ob_ref[...]
        mu = jnp.mean(o, axis=-1, keepdims=True)
        var = jnp.mean(jnp.square(o - mu), axis=-1, keepdims=True)
        rstd = lax.rsqrt(var + EPS)
        yhat = (o - mu) * rstd
        gv, dzv = g_ref[...], dz_ref[...]
        sg = jax.nn.sigmoid(gv)
        dg_ref[...] = (dzv * (yhat * gain_ref[...]) * (sg * (1.0 + gv * (1.0 - sg)))).astype(dg_ref.dtype)
        dy = dzv * (gv * sg)

        @pl.when(pl.program_id(1) == 0)
        def _():
            dgain_ref[...] = jnp.zeros_like(dgain_ref)

        dgain_ref[...] += jnp.sum(dy * yhat, axis=0, keepdims=True)
        dyh = dy * gain_ref[...]
        do_ref[...] = rstd * (dyh - jnp.mean(dyh, axis=-1, keepdims=True)
                              - yhat * jnp.mean(dyh * yhat, axis=-1, keepdims=True))

    blk = pl.BlockSpec((ROW_TILE, RET_V_DIM), lambda h, i: (i, h))
    vec = pl.BlockSpec((1, RET_V_DIM), lambda h, i: (0, h))
    return pl.pallas_call(
        body, name=name, grid=(RET_HEADS, T // ROW_TILE),
        in_specs=[blk, blk, blk, pl.BlockSpec((ROW_TILE, RET_V_DIM), lambda h, i: (i, g_off + h)), vec],
        out_specs=(blk, blk, vec),
        out_shape=(jax.ShapeDtypeStruct((T, RET_VWIDTH), F32), jax.ShapeDtypeStruct((T, RET_VWIDTH), MXU_DTYPE),
                   jax.ShapeDtypeStruct((1, RET_VWIDTH), F32)),
        compiler_params=_params())(dz, o_f, o_b, qkvg, gn_gain)


def _adamw(w, m, v, parts, name):
    R, C = w.shape
    tr = _tile(R, (256, 128, 64, 32, 16, 8))
    n_parts = [p.shape[0] for p in parts]

    def body(*refs):
        w_ref, m_ref, v_ref = refs[:3]
        part_refs = refs[3:3 + len(parts)]
        g_ref, d_ref, nm_ref, nv_ref = refs[3 + len(parts):]
        g = None
        for ref, n in zip(part_refs, n_parts):
            for r in range(n):
                term = ref[r].astype(F32)
                g = term if g is None else g + term
        mn = ADAM_B1 * m_ref[...] + (1.0 - ADAM_B1) * g
        vn = ADAM_B2 * v_ref[...] + (1.0 - ADAM_B2) * jnp.square(g)
        m_hat = mn / (1.0 - ADAM_B1 ** ADAM_STEP)
        v_hat = vn / (1.0 - ADAM_B2 ** ADAM_STEP)
        g_ref[...] = g
        d_ref[...] = -ADAM_LR * (m_hat / (jnp.sqrt(v_hat) + ADAM_EPS) + ADAM_WD * w_ref[...])
        nm_ref[...] = mn
        nv_ref[...] = vn

    blk = pl.BlockSpec((tr, C), lambda i: (i, 0))
    part_specs = [pl.BlockSpec((n, tr, C), lambda i: (0, i, 0)) for n in n_parts]
    shp = jax.ShapeDtypeStruct((R, C), F32)
    return pl.pallas_call(
        body, name=name, grid=(R // tr,), in_specs=[blk, blk, blk] + part_specs,
        out_specs=(blk, blk, blk, blk), out_shape=(shp, shp, shp, shp),
        compiler_params=_params())(w, m, v, *parts)


def _sum_rows(parts, name):
    n, R, C = parts.shape
    tr = _tile(R, (256, 128, 64, 32, 16, 8))

    def body(p_ref, o_ref):
        acc = p_ref[0]
        for r in range(1, n):
            acc = acc + p_ref[r]
        o_ref[...] = acc

    return pl.pallas_call(
        body, name=name, grid=(R // tr,), in_specs=[pl.BlockSpec((n, tr, C), lambda i: (0, i, 0))],
        out_specs=pl.BlockSpec((tr, C), lambda i: (i, 0)), out_shape=jax.ShapeDtypeStruct((R, C), F32),
        compiler_params=_params())(parts)


def _my_coords():
    return lax.axis_index("x"), lax.axis_index("y"), lax.axis_index("c")


def _flip(coord, bit):
    return 1 - coord if bit else coord


def _all_gather(x2d, name):
    R, C = x2d.shape

    def body(x_ref, out_ref, send_sems, recv_sems, local_sem):
        x, y, c = _my_coords()
        me, sibling = (x, y, c), (x, y, 1 - c)
        chips = [(1 - x, y), (x, 1 - y), (1 - x, 1 - y)]

        def rows(px, py, pc):
            return out_ref.at[4 * px + 2 * py + pc]

        def copy(k, block, to, src=None):
            return pltpu.make_async_remote_copy(
                src_ref=rows(*block) if src is None else src, dst_ref=rows(*block),
                send_sem=send_sems.at[k], recv_sem=recv_sems.at[k], device_id=to, device_id_type=MESH)

        mine = pltpu.make_async_copy(x_ref, rows(*me), local_sem)
        mine.start()
        first = [copy(0, me, sibling, src=x_ref)]
        first += [copy(1 + j, me, (*chip, c), src=x_ref) for j, chip in enumerate(chips)]
        for cp in first:
            cp.start()
        passed = [copy(4 + j, (*chip, c), sibling) for j, chip in enumerate(chips)]
        for j, chip in enumerate(chips):
            copy(1 + j, (*chip, c), me).wait_recv()
            passed[j].start()
        copy(0, sibling, me).wait_recv()
        for j, chip in enumerate(chips):
            copy(4 + j, (*chip, 1 - c), me).wait_recv()
        for cp in first + passed:
            cp.wait_send()
        mine.wait()

    return pl.pallas_call(
        body, name=name, out_shape=jax.ShapeDtypeStruct((N_DEV, R, C), x2d.dtype),
        in_specs=[_ANY], out_specs=_ANY,
        scratch_shapes=[pltpu.SemaphoreType.DMA((7,)), pltpu.SemaphoreType.DMA((7,)), pltpu.SemaphoreType.DMA],
    )(x2d)


BIG_WEIGHTS = {
    "ffn_w_in": (2, (2, D_MODEL, 2 * D_FF)),
    "ffn_w_out": (1, (2, D_FF, D_MODEL)),
    "attn_w_qkv": (2, (1, D_MODEL, (N_HEADS + 2 * N_KV_HEADS) * HEAD_DIM)),
    "attn_w_o": (1, (1, N_HEADS * HEAD_DIM, D_MODEL)),
    "ret_w_qkvg": (2, (1, D_MODEL, 2 * D_MODEL + 2 * RET_VWIDTH)),
    "ret_gn_g": (2, (1, 1, RET_VWIDTH)),
    "ret_w_o": (1, (1, RET_VWIDTH, D_MODEL)),
}


def _join_shards(name, stacked):
    axis, full = BIG_WEIGHTS[name]
    if axis == 2:
        stacked = stacked.transpose(0, 2, 1, 3)
    return stacked.reshape(full)


def _split_shards(name, full_arr):
    axis, full = BIG_WEIGHTS[name]
    L, rows, cols = full
    if axis == 2:
        return full_arr.reshape(L, rows, N_DEV, cols // N_DEV).transpose(0, 2, 1, 3)
    return full_arr.reshape(L, N_DEV, rows // N_DEV, cols)


def _gather_shards(shards, name):
    n = len(shards)

    def body(*refs):
        x_refs, out_refs = refs[:n], refs[n:2 * n]
        send_sems, recv_sems, local_sems = refs[2 * n:]
        x, y, c = _my_coords()
        me, sibling = (x, y, c), (x, y, 1 - c)
        chips = [(1 - x, y), (x, 1 - y), (1 - x, 1 - y)]

        def rows(a, px, py, pc):
            return out_refs[a].at[:, 4 * px + 2 * py + pc]

        def copy(a, k, block, to, src=None):
            return pltpu.make_async_remote_copy(
                src_ref=rows(a, *block) if src is None else src, dst_ref=rows(a, *block),
                send_sem=send_sems.at[7 * a + k], recv_sem=recv_sems.at[7 * a + k], device_id=to, device_id_type=MESH)

        mine = [pltpu.make_async_copy(x_refs[a], rows(a, *me), local_sems.at[a]) for a in range(n)]
        for cp in mine:
            cp.start()
        first = []
        for a in range(n):
            first.append(copy(a, 0, me, sibling, src=x_refs[a]))
            first += [copy(a, 1 + j, me, (*chip, c), src=x_refs[a]) for j, chip in enumerate(chips)]
        for cp in first:
            cp.start()
        passed = []
        for j, chip in enumerate(chips):
            for a in range(n):
                copy(a, 1 + j, (*chip, c), me).wait_recv()
                fwd = copy(a, 4 + j, (*chip, c), sibling)
                fwd.start()
                passed.append(fwd)
        for a in range(n):
            copy(a, 0, sibling, me).wait_recv()
            for j, chip in enumerate(chips):
                copy(a, 4 + j, (*chip, 1 - c), me).wait_recv()
        for cp in first + passed:
            cp.wait_send()
        for cp in mine:
            cp.wait()

    return pl.pallas_call(
        body, name=name,
        out_shape=[jax.ShapeDtypeStruct((s.shape[0], N_DEV) + s.shape[1:], s.dtype) for s in shards],
        in_specs=[_ANY] * n, out_specs=[_ANY] * n,
        scratch_shapes=[pltpu.SemaphoreType.DMA((7 * n,)), pltpu.SemaphoreType.DMA((7 * n,)),
                        pltpu.SemaphoreType.DMA((n,))],
    )(*shards)


def _exchange_shards(arrs, masks, src_of, out_tail, name):
    n, nm = len(arrs), len(masks)

    def body(*refs):
        in_refs, out_refs = refs[:n], refs[n:2 * n]
        send_sems, recv_sems = refs[2 * n:]
        x, y, c = _my_coords()
        copies = []
        for a in range(n):
            for k, (bx, by, bc) in enumerate(masks):
                peer = (_flip(x, bx), _flip(y, by), _flip(c, bc))
                copies.append(pltpu.make_async_remote_copy(
                    src_ref=src_of(in_refs[a], peer, (x, y, c)), dst_ref=out_refs[a].at[k],
                    send_sem=send_sems.at[nm * a + k], recv_sem=recv_sems.at[nm * a + k],
                    device_id=peer, device_id_type=MESH))
        for cp in copies:
            cp.start()
        for cp in copies:
            cp.wait()

    return pl.pallas_call(
        body, name=name,
        out_shape=[jax.ShapeDtypeStruct((nm,) + out_tail(s), s.dtype) for s in arrs],
        in_specs=[_ANY] * n, out_specs=[_ANY] * n,
        scratch_shapes=[pltpu.SemaphoreType.DMA((nm * n,)), pltpu.SemaphoreType.DMA((nm * n,))],
    )(*arrs)


def _pair_sum(g, from_sibling, core, out_dtype, name):
    L, _, _, a, b = g.shape
    ta = a

    def body(core_ref, g_ref, s_ref, o_ref):
        o_ref[...] = (g_ref[...] + s_ref[...]).astype(out_dtype)

    blk = pl.BlockSpec((1, 1, ta, b), lambda l, q, i, core_ref: (l, q, i, 0))
    return pl.pallas_call(
        body, name=name,
        grid_spec=pltpu.PrefetchScalarGridSpec(
            num_scalar_prefetch=1, grid=(L, 4, a // ta),
            in_specs=[pl.BlockSpec((1, 1, pl.Squeezed(), ta, b), lambda l, q, i, core_ref: (l, q, core_ref[0], i, 0)), blk],
            out_specs=blk),
        out_shape=jax.ShapeDtypeStruct((L, 4, a, b), out_dtype), compiler_params=_params())(core, g, from_sibling)


def _mods(mod_x, mod_c, layer):
    both = jnp.concatenate([mod_x[:, layer], mod_c[layer][None]], axis=0)
    return [both[:, None, k * D_MODEL:(k + 1) * D_MODEL] for k in range(6)]


def _local_step(x, ctx, target, mod_x, mod_c, w, small):
    B, S, _ = x.shape
    L = ctx.shape[1]
    NX, NC = B * S, B * L
    T = NX + NC
    tiles_per_ex = S // ROW_TILE
    nxt = NX // ROW_TILE
    gidx = _group_index(nxt, tiles_per_ex, B)
    gidx_for = lambda rows: _group_index(NX // rows, S // rows, B)
    mm_rows = _tile(S, (MM_ROWS, ROW_TILE))
    tidx = lambda i: jnp.where(i < nxt, i % tiles_per_ex, tiles_per_ex)
    G = B + 1
    x0 = jnp.concatenate([x.reshape(NX, D_MODEL), ctx.reshape(NC, D_MODEL)], axis=0)
    acos, asin = [jnp.tile(t, (1, LANES // HEAD_DIM)) for t in _rope_tables(S, HEAD_DIM)]
    rcos, rsin = _rope_tables(S, RET_QK_DIM)
    sink = small["attn_sink"].reshape(N_KV_HEADS, GQA_GROUP)
    gains = jnp.stack([jnp.tile(small["attn_q_norm"].reshape(1, HEAD_DIM), (1, LANES // HEAD_DIM)),
                       jnp.tile(small["attn_k_norm"].reshape(1, HEAD_DIM), (1, LANES // HEAD_DIM))])
    log_g = jax.nn.log_sigmoid(small["ret_decay_logit"].reshape(2, RET_HEADS))
    n1, n2 = small["norm1_g"], small["norm2_g"]

    m0 = _mods(mod_x, mod_c, 0)
    h1 = _norm_mod_fwd(x0, n1[0:1], m0[0], m0[1], gidx, "l0_norm1")
    qkv = _mm(h1, w["attn_w_qkv"][0], "nn", F32, "l0_qkv")
    qkv_r = _attn_prep_fwd(qkv, gains, acos, asin, tidx, "l0_qk_prep")
    o_x = _attn_fwd(qkv_r, sink, B, S, L, False, "l0_attn_x")
    o_c = _attn_fwd(qkv_r, sink, B, S, L, True, "l0_attn_c")
    o0 = jnp.concatenate([o_x, o_c], axis=0)
    mo0, x1 = _mm(o0, w["attn_w_o"][0], "nn", F32, "l0_attn_out", res=x0, gate=m0[2], gidx_for=gidx_for, gate_rows=mm_rows)
    h2 = _norm_mod_fwd(x1, n2[0:1], m0[3], m0[4], gidx, "l0_norm2")
    u0 = _mm(h2, w["ffn_w_in"][0], "nn", F32, "l0_ffn_in")
    a0 = _swiglu_fwd(u0, "l0_swiglu")
    f0, x2 = _mm(a0, w["ffn_w_out"][0], "nn", F32, "l0_ffn_out", res=x1, gate=m0[5], gidx_for=gidx_for, gate_rows=mm_rows)

    m1 = _mods(mod_x, mod_c, 1)
    g1 = _norm_mod_fwd(x2, n1[1:2], m1[0], m1[1], gidx, "l1_norm1")
    qkvg = _mm(g1, w["ret_w_qkvg"][0], "nn", F32, "l1_qkvg")
    qk = _ret_rope(qkvg, rcos, rsin, tidx, "l1_rope")
    of, st_f, ob, st_b = _ret_fwd(qk, qkvg, log_g, B, S, L, "l1_ret")
    gn = w["ret_gn_g"].reshape(1, RET_VWIDTH)
    z1 = _gated_out_fwd(of, ob, qkvg, gn, "l1_gated_out")
    xx2 = x2[:NX]
    gx = lambda i: i // tiles_per_ex
    m1x = [t[:B] for t in m1]
    mo1, y1 = _mm(z1, w["ret_w_o"][0], "nn", F32, "l1_ret_out", res=xx2, gate=m1x[2], gidx_for=gidx_for, gate_rows=mm_rows)
    k2 = _norm_mod_fwd(y1, n2[1:2], m1x[3], m1x[4], gx, "l1_norm2")
    u1 = _mm(k2, w["ffn_w_in"][1], "nn", F32, "l1_ffn_in")
    a1 = _swiglu_fwd(u1, "l1_swiglu")
    f1, y2 = _mm(a1, w["ffn_w_out"][1], "nn", F32, "l1_ffn_out", res=y1, gate=m1x[5], gidx_for=gidx_for, gate_rows=mm_rows)

    loss_tile, dy2 = _loss_fwd_bwd(y2, target.reshape(NX, D_MODEL), "loss")

    zg = jnp.zeros((1, 1, D_MODEL), F32)
    dz, dgate5_1 = _gate_bwd(dy2, f1, m1x[5], gx, B, "l1_ffn_gate_bwd")
    gw_ffn_out1 = _mm(a1, dz, "tn", F32, "l1_ffn_out_dw")
    da = _mm(dz, w["ffn_w_out"][1], "nt", F32, "l1_ffn_out_dx")
    du = _swiglu_bwd(da, u1, "l1_swiglu_bwd")
    gw_ffn_in1 = _mm(k2, du, "tn", F32, "l1_ffn_in_dw")
    dk2 = _mm(du, w["ffn_w_in"][1], "nt", F32, "l1_ffn_in_dx")
    dy1, dsh3_1, dsc4_1, dn2_1 = _norm_mod_bwd(dk2, y1, n2[1:2], m1x[4], dy2, gx, B, "l1_norm2_bwd")
    dzo, dgate2_1 = _gate_bwd(dy1, mo1, m1x[2], gx, B, "l1_ret_gate_bwd")
    gw_ret_o = _mm(z1, dzo, "tn", F32, "l1_ret_out_dw")
    dz1 = _mm(dzo, w["ret_w_o"][0], "nt", F32, "l1_ret_out_dx")
    do_r, dg_r, dgn = _gated_out_bwd(dz1, of, ob, qkvg, gn, "l1_gated_out_bwd")
    ((dq_f, dk_f, dv_f, dkc_f, dvc_f, dlg_f),
     (dq_b, dk_b, dv_b, dkc_b, dvc_b, dlg_b)) = _ret_bwd(qk, qkvg, log_g, st_f, st_b, do_r, B, S, L, "l1_ret_bwd")
    dqkvg = _ret_grad_assemble((dq_f, dq_b, dk_f, dk_b, dv_f, dv_b), (dkc_f, dkc_b, dvc_f, dvc_b), dg_r, rcos, rsin, S,
                               "l1_qkvg_grad")
    gw_ret_qkvg = _mm(g1, dqkvg, "tn", F32, "l1_qkvg_dw")
    dg1 = _mm(dqkvg, w["ret_w_qkvg"][0], "nt", F32, "l1_qkvg_dx")
    dres1 = jnp.concatenate([dy1, jnp.zeros((NC, D_MODEL), F32)], axis=0)
    dx2, dsh0_1, dsc1_1, dn1_1 = _norm_mod_bwd(dg1, x2, n1[1:2], m1[1], dres1, gidx, G, "l1_norm1_bwd")
    dlg = jnp.stack([jnp.sum(dlg_f[:, :, 0, 0], axis=0), jnp.sum(dlg_b[:, :, 0, 0], axis=0)])
    d_decay = (dlg * jax.nn.sigmoid(-small["ret_decay_logit"].reshape(2, RET_HEADS))).reshape(1, 2, RET_HEADS)

    dz, dgate5_0 = _gate_bwd(dx2, f0, m0[5], gidx, G, "l0_ffn_gate_bwd")
    gw_ffn_out0 = _mm(a0, dz, "tn", F32, "l0_ffn_out_dw")
    da = _mm(dz, w["ffn_w_out"][0], "nt", F32, "l0_ffn_out_dx")
    du = _swiglu_bwd(da, u0, "l0_swiglu_bwd")
    gw_ffn_in0 = _mm(h2, du, "tn", F32, "l0_ffn_in_dw")
    dh2 = _mm(du, w["ffn_w_in"][0], "nt", F32, "l0_ffn_in_dx")
    dx1, dsh3_0, dsc4_0, dn2_0 = _norm_mod_bwd(dh2, x1, n2[0:1], m0[4], dx2, gidx, G, "l0_norm2_bwd")
    dzo, dgate2_0 = _gate_bwd(dx1, mo0, m0[2], gidx, G, "l0_attn_gate_bwd")
    gw_attn_o = _mm(o0, dzo, "tn", F32, "l0_attn_out_dw")
    do0 = _mm(dzo, w["attn_w_o"][0], "nt", MXU_DTYPE, "l0_attn_out_dx")
    dq_x, dk_x, dv_x, dkc1, dvc1, dsink_x = _attn_bwd(qkv_r, sink, do0, B, S, L, False, "l0_attn_x_bwd")
    dq_c, dkc2, dvc2, dsink_c = _attn_bwd(qkv_r, sink, do0, B, S, L, True, "l0_attn_c_bwd")
    dqk = jnp.concatenate([jnp.concatenate([dq_x, dk_x], axis=1), jnp.concatenate([dq_c, dkc1 + dkc2], axis=1)], axis=0)
    dvv = jnp.concatenate([dv_x, dvc1 + dvc2], axis=0)
    dqkv, dgains = _attn_prep_bwd(dqk, dvv, qkv, gains, acos, asin, tidx, "l0_qk_prep_bwd")
    gw_attn_qkv = _mm(h1, dqkv, "tn", F32, "l0_qkv_dw")
    dh1 = _mm(dqkv, w["attn_w_qkv"][0], "nt", F32, "l0_qkv_dx")
    dx0, dsh0_0, dsc1_0, dn1_0 = _norm_mod_bwd(dh1, x0, n1[0:1], m0[1], dx1, gidx, G, "l0_norm1_bwd")

    dgains = jnp.sum(dgains.reshape(ATTN_QK_BLOCKS, LANES // HEAD_DIM, HEAD_DIM), axis=1)
    dsink = (dsink_x + dsink_c).reshape(N_KV_HEADS, 8, LANES)[:, :GQA_GROUP, 0].reshape(1, N_HEADS)
    grads_big = {
        "ffn_w_in": jnp.stack([gw_ffn_in0, gw_ffn_in1]),
        "ffn_w_out": jnp.stack([gw_ffn_out0, gw_ffn_out1]),
        "attn_w_qkv": gw_attn_qkv[None],
        "attn_w_o": gw_attn_o[None],
        "ret_w_qkvg": gw_ret_qkvg[None],
        "ret_gn_g": dgn,
        "ret_w_o": gw_ret_o[None],
    }
    grads_small = {
        "norm1_g": jnp.concatenate([dn1_0, dn1_1], axis=0),
        "norm2_g": jnp.concatenate([dn2_0, dn2_1], axis=0),
        "attn_q_norm": jnp.sum(dgains[:ATTN_Q_BLOCKS], axis=0)[None],
        "attn_k_norm": jnp.sum(dgains[ATTN_Q_BLOCKS:ATTN_QK_BLOCKS], axis=0)[None],
        "attn_sink": dsink,
        "ret_decay_logit": d_decay,
    }

    def pad_g(t):
        return jnp.concatenate([t, zg], axis=0)

    d0 = jnp.concatenate([dsh0_0, dsc1_0, dgate2_0, dsh3_0, dsc4_0, dgate5_0], axis=2)[:, 0]
    d1 = jnp.concatenate([dsh0_1, dsc1_1, pad_g(dgate2_1), pad_g(dsh3_1), pad_g(dsc4_1), pad_g(dgate5_1)],
                         axis=2)[:, 0]
    dmod_x = jnp.stack([d0[:B], d1[:B]], axis=1)
    dmod_c = jnp.stack([d0[B], d1[B]], axis=0)
    return loss_tile, dx0[:NX].reshape(B, S, D_MODEL), grads_big, grads_small, dmod_x, dmod_c


SMALL_NAMES = ("c_ctx", "ada_b", "norm1_g", "norm2_g", "attn_q_norm", "attn_k_norm", "attn_sink", "ret_decay_logit")
ADA_ROWS = 64


def _pack_small(d, rows):
    flat = jnp.concatenate([d[k].reshape(-1) for k in SMALL_NAMES])
    n = rows * LANES
    return jnp.pad(flat, (0, n - flat.shape[0])).reshape(rows, LANES)


def _unpack_small(packed, shapes):
    flat = packed.reshape(-1)
    out, off = {}, 0
    for k in SMALL_NAMES:
        n = math.prod(shapes[k])
        out[k] = flat[off:off + n].reshape(shapes[k])
        off += n
    return out


def _gather_big_weights(weights):
    gathered = _gather_shards(
        [weights[k].reshape(1, 1, -1) if k == "ret_gn_g" else weights[k].astype(MXU_DTYPE) for k in BIG_WEIGHTS],
        "gather_weights")
    return {k: _join_shards(k, g) for k, g in zip(BIG_WEIGHTS, gathered)}


def _reduce_and_update_big(g_big, weights, mom1, mom2):
    mx_, my_, mc_ = _my_coords()
    my_chip = 2 * mx_ + my_
    names = list(BIG_WEIGHTS)
    split = []
    for k in names:
        s = _split_shards(k, g_big[k])
        split.append(s.reshape(s.shape[0], 4, 2, s.shape[2], s.shape[3]))
    from_sibling = _exchange_shards(
        split, [(0, 0, 1)], lambda ref, peer, me_: ref.at[:, :, peer[2]],
        lambda s: (s.shape[0], 4) + s.shape[3:], "rs_sibling")
    from_sibling = [t[0] for t in from_sibling]
    core = mc_.astype(jnp.int32).reshape(1)
    pair = [_pair_sum(g, s, core, MXU_DTYPE, "rs_pair_" + k) for k, g, s in zip(names, split, from_sibling)]
    from_chips = _exchange_shards(
        pair, [(1, 0, 0), (0, 1, 0), (1, 1, 0)], lambda ref, peer, me_: ref.at[:, 2 * peer[0] + peer[1]],
        lambda s: (s.shape[0],) + s.shape[2:], "rs_chips")
    big = {}
    for k, g, s, r in zip(names, split, from_sibling, from_chips):
        L_, _, _, a_, b_ = g.shape
        own_keep = lax.dynamic_index_in_dim(lax.dynamic_index_in_dim(g, my_chip, axis=1, keepdims=False), mc_, axis=1,
                                            keepdims=False)
        own_sib = lax.dynamic_index_in_dim(s, my_chip, axis=1, keepdims=False)
        rows = L_ * a_
        res = _adamw(weights[k].reshape(rows, b_), mom1[k].reshape(rows, b_), mom2[k].reshape(rows, b_),
                     [own_keep.reshape(1, rows, b_), own_sib.reshape(1, rows, b_), r.reshape(3, rows, b_)],
                     "adamw_" + k)
        big[k] = [t.reshape(weights[k].shape) for t in res]
    return big


def kernel(x, c, ctx, c_ctx, ada_w, ada_b, norm1_g, norm2_g, ffn_w_in, ffn_w_out, attn_w_qkv, attn_q_norm, attn_k_norm, attn_sink, attn_w_o, ret_w_qkvg, ret_decay_logit, ret_gn_g, ret_w_o, loss_target, m_c_ctx, m_ada_w, m_ada_b, m_norm1_g, m_norm2_g, m_ffn_w_in, m_ffn_w_out, m_attn_w_qkv, m_attn_q_norm, m_attn_k_norm, m_attn_sink, m_attn_w_o, m_ret_w_qkvg, m_ret_decay_logit, m_ret_gn_g, m_ret_w_o, v_c_ctx, v_ada_w, v_ada_b, v_norm1_g, v_norm2_g, v_ffn_w_in, v_ffn_w_out, v_attn_w_qkv, v_attn_q_norm, v_attn_k_norm, v_attn_sink, v_attn_w_o, v_ret_w_qkvg, v_ret_decay_logit, v_ret_gn_g, v_ret_w_o):
    weights = dict(c_ctx=c_ctx, ada_w=ada_w, ada_b=ada_b, norm1_g=norm1_g, norm2_g=norm2_g, ffn_w_in=ffn_w_in,
                   ffn_w_out=ffn_w_out, attn_w_qkv=attn_w_qkv, attn_q_norm=attn_q_norm, attn_k_norm=attn_k_norm,
                   attn_sink=attn_sink, attn_w_o=attn_w_o, ret_w_qkvg=ret_w_qkvg, ret_decay_logit=ret_decay_logit,
                   ret_gn_g=ret_gn_g, ret_w_o=ret_w_o)
    mom1 = dict(c_ctx=m_c_ctx, ada_w=m_ada_w, ada_b=m_ada_b, norm1_g=m_norm1_g, norm2_g=m_norm2_g, ffn_w_in=m_ffn_w_in,
                ffn_w_out=m_ffn_w_out, attn_w_qkv=m_attn_w_qkv, attn_q_norm=m_attn_q_norm, attn_k_norm=m_attn_k_norm,
                attn_sink=m_attn_sink, attn_w_o=m_attn_w_o, ret_w_qkvg=m_ret_w_qkvg, ret_decay_logit=m_ret_decay_logit,
                ret_gn_g=m_ret_gn_g, ret_w_o=m_ret_w_o)
    mom2 = dict(c_ctx=v_c_ctx, ada_w=v_ada_w, ada_b=v_ada_b, norm1_g=v_norm1_g, norm2_g=v_norm2_g, ffn_w_in=v_ffn_w_in,
                ffn_w_out=v_ffn_w_out, attn_w_qkv=v_attn_w_qkv, attn_q_norm=v_attn_q_norm, attn_k_norm=v_attn_k_norm,
                attn_sink=v_attn_sink, attn_w_o=v_attn_w_o, ret_w_qkvg=v_ret_w_qkvg, ret_decay_logit=v_ret_decay_logit,
                ret_gn_g=v_ret_gn_g, ret_w_o=v_ret_w_o)
    B = x.shape[0]
    mx_, my_, mc_ = _my_coords()
    me = 4 * mx_ + 2 * my_ + mc_
    ada_cols = ada_w.shape[2]

    w_full = _gather_big_weights(weights)

    c_all = _all_gather(jax.nn.silu(c), "gather_c").reshape(N_DEV * B, D_MODEL)
    cc_act = jax.nn.silu(c_ctx)[None]
    ada_in = jnp.concatenate([c_all, cc_act, jnp.zeros((ADA_ROWS - N_DEV * B - 1, D_MODEL), F32)], axis=0)
    ada_in = ada_in.astype(MXU_DTYPE)
    ada_w2 = jnp.concatenate([ada_w[0], ada_w[1]], axis=1)
    bias = lax.dynamic_slice_in_dim(ada_b.reshape(2, N_DEV, ada_cols), me, 1, axis=1).reshape(1, 2 * ada_cols)
    mod_cols = _mm(ada_in, ada_w2, "nn", F32, "ada_fwd", bias=bias)
    mod_all = _all_gather(mod_cols, "gather_mod")
    mod_all = mod_all.reshape(N_DEV, ADA_ROWS, 2, ada_cols).transpose(1, 2, 0, 3).reshape(ADA_ROWS, 2, N_DEV * ada_cols)
    mod_x = lax.dynamic_slice_in_dim(mod_all, me * B, B, axis=0)
    mod_c = mod_all[N_DEV * B]

    small = {k: weights[k] for k in SMALL_NAMES}
    loss_tile, grad_x, g_big, g_small, dmod_x, dmod_c = _local_step(x, ctx, loss_target, mod_x, mod_c, w_full, small)
    loss = lax.psum(loss_tile[0, 0], ("x", "y", "c"))

    n_mod = 2 * 6 * D_MODEL
    dm_rows = jnp.concatenate([dmod_x.reshape(B, n_mod), dmod_c.reshape(1, n_mod),
                               jnp.zeros((8 - B - 1, n_mod), F32)], axis=0)
    dm_all = _all_gather(dm_rows, "gather_dmod")
    dmc_tot = _sum_rows(dm_all[:, B:B + 1].reshape(N_DEV, 1, n_mod)[:, :, :].reshape(N_DEV, n_mod // LANES, LANES),
                        "sum_dmod_c").reshape(1, n_mod)
    dmod_rows = jnp.concatenate([dm_all[:, :B].reshape(N_DEV * B, n_mod), dmc_tot,
                                 jnp.zeros((ADA_ROWS - N_DEV * B - 1, n_mod), F32)], axis=0)
    dmod_mine = lax.dynamic_slice_in_dim(dmod_rows.reshape(ADA_ROWS, 2, N_DEV, ada_cols), me, 1, axis=2)
    dmod_mine = dmod_mine.reshape(ADA_ROWS, 2 * ada_cols).astype(MXU_DTYPE)
    g_ada2 = _mm(ada_in, dmod_mine, "tn", F32, "ada_dw")
    g_ada_w = jnp.stack([g_ada2[:, :ada_cols], g_ada2[:, ada_cols:]])
    dmc_mine = jnp.concatenate([dmod_mine[N_DEV * B:N_DEV * B + 1], jnp.zeros((7, 2 * ada_cols), MXU_DTYPE)], axis=0)
    dcc_part = _mm(dmc_mine, ada_w2, "nt", F32, "ada_dc")[0:1]
    g_ada_b = _sum_rows(dmod_rows[:, None, :].reshape(ADA_ROWS, n_mod // LANES, LANES), "sum_dmod_b").reshape(2, 6 * D_MODEL)
    sg = jax.nn.sigmoid(c_ctx)
    g_small["c_ctx"] = dcc_part.reshape(D_MODEL) * (sg * (1.0 + c_ctx * (1.0 - sg)))
    g_small["ada_b"] = g_ada_b * (1.0 / N_DEV)

    shapes = {k: weights[k].shape for k in SMALL_NAMES}
    n_small = sum(math.prod(s) for s in shapes.values())
    srows = -(-(-(-n_small // LANES)) // 8) * 8
    gs_all = _all_gather(_pack_small(g_small, srows), "gather_small_grads")
    sm = _adamw(_pack_small({k: weights[k] for k in SMALL_NAMES}, srows), _pack_small({k: mom1[k] for k in SMALL_NAMES}, srows),
                _pack_small({k: mom2[k] for k in SMALL_NAMES}, srows), [gs_all], "adamw_small")
    sm = [_unpack_small(t, shapes) for t in sm]

    ada_shape = ada_w.shape
    r2 = lambda t: t.reshape(ada_shape[0] * ada_shape[1], ada_shape[2])
    ada = [t.reshape(ada_shape) for t in _adamw(r2(ada_w), r2(m_ada_w), r2(v_ada_w), [r2(g_ada_w)[None]], "adamw_ada")]

    big = _reduce_and_update_big(g_big, weights, mom1, mom2)

    def pick(i, name):
        if name in BIG_WEIGHTS:
            return big[name][i]
        if name == "ada_w":
            return ada[i]
        return sm[i][name]

    order = ("c_ctx", "ada_w", "ada_b", "norm1_g", "norm2_g", "ffn_w_in", "ffn_w_out", "attn_w_qkv", "attn_q_norm",
             "attn_k_norm", "attn_sink", "attn_w_o", "ret_w_qkvg", "ret_decay_logit", "ret_gn_g", "ret_w_o")
    outs = [loss, grad_x]
    for i in range(4):
        outs += [pick(i, n) for n in order]
    return tuple(outs)
```

```python
import functools
import math

import jax
import jax.numpy as jnp
from jax import lax
from jax.experimental import pallas as pl
from jax.experimental.pallas import tpu as pltpu

F32 = jnp.float32
MXU_DTYPE = jnp.bfloat16

D_MODEL = 1024
HEAD_DIM = 64
N_HEADS = 16
N_KV_HEADS = 4
GQA_GROUP = 4
WINDOW = 128
ATTN_BLOCK = 128
RET_HEADS = 4
RET_QK_DIM = 256
RET_V_DIM = 512
RET_VWIDTH = 2048
RET_CHUNK = 256
D_FF = 2816
GRID_W = 64
ROPE_BASE = 10000.0
EPS = 1e-6
NEG_INF = -1e30

ADAM_LR = 0.001
ADAM_B1 = 0.9
ADAM_B2 = 0.999
ADAM_EPS = 1e-08
ADAM_WD = 0.01
ADAM_STEP = 10

N_DEV = 8
LANES = 128
ROW_TILE = 512
VMEM_LIMIT = 48 * 1024 * 1024

MESH = pl.DeviceIdType.MESH
_ANY = pl.BlockSpec(memory_space=pl.ANY)
_SMEM = pl.BlockSpec(memory_space=pltpu.SMEM)


def _params(**kw):
    return pltpu.CompilerParams(vmem_limit_bytes=VMEM_LIMIT, **kw)


def _mx(v):
    return v.astype(MXU_DTYPE)


def _dot(a, b, dims):
    return lax.dot_general(_mx(a), _mx(b), (dims, ((), ())), preferred_element_type=F32)


_NN = ((1,), (0,))
_NT = ((1,), (1,))
_TN = ((0,), (0,))


def _tile(n, cands):
    for c in cands:
        if n % c == 0:
            return c
    return n


def _big_tile(n, cap):
    if n <= cap:
        return n
    for t in range(cap - cap % LANES, 0, -LANES):
        if n % t == 0:
            return t
    return n


MM_ROWS = 1024
MM_COLS = 1408
MM_DEPTH = 2048


def _k_tile(k):
    return _big_tile(k, MM_DEPTH)


def _mm(a, b, mode, out_dtype, name, *, bias=None, res=None, gate=None, gidx_for=None, gate_rows=None):
    if mode == "nn":
        (M, K), (_, N) = a.shape, b.shape
    elif mode == "nt":
        (M, K), (N, _) = a.shape, b.shape
    else:
        (K, M), (_, N) = a.shape, b.shape
    if res is not None:
        tm, tn = gate_rows, _big_tile(N, 512)
        gidx = gidx_for(tm)
    else:
        tm = _big_tile(M, MM_COLS if mode == "tn" else MM_ROWS)
        tn = _big_tile(N, MM_COLS)
    tk = _k_tile(K)
    nk = K // tk
    dims = {"nn": _NN, "nt": _NT, "tn": _TN}[mode]
    a_spec = (pl.BlockSpec((tk, tm), lambda i, j, k: (k, i)) if mode == "tn"
              else pl.BlockSpec((tm, tk), lambda i, j, k: (i, k)))
    b_spec = (pl.BlockSpec((tn, tk), lambda i, j, k: (j, k)) if mode == "nt"
              else pl.BlockSpec((tk, tn), lambda i, j, k: (k, j)))
    o_spec = pl.BlockSpec((tm, tn), lambda i, j, k: (i, j))
    in_specs, operands = [a_spec, b_spec], [a, b]
    if bias is not None:
        in_specs.append(pl.BlockSpec((1, tn), lambda i, j, k: (0, j)))
        operands.append(bias)
    if res is not None:
        in_specs += [o_spec, pl.BlockSpec((1, 1, tn), lambda i, j, k: (gidx(i), 0, j))]
        operands += [res, gate]
        out_shape = (jax.ShapeDtypeStruct((M, N), F32), jax.ShapeDtypeStruct((M, N), F32))
        out_specs = (o_spec, o_spec)
    else:
        out_shape = jax.ShapeDtypeStruct((M, N), out_dtype)
        out_specs = o_spec

    def body(*refs):
        a_ref, b_ref = refs[0], refs[1]
        extra = refs[2:len(operands)]
        outs = refs[len(operands):]
        prod = _dot(a_ref[...], b_ref[...], dims)

        def finish(acc):
            if bias is not None:
                outs[0][...] = (acc + extra[0][...]).astype(out_dtype)
            elif res is not None:
                outs[0][...] = acc
                outs[1][...] = extra[0][...] + extra[1][0] * acc
            else:
                outs[0][...] = acc.astype(out_dtype)

        if nk == 1:
            finish(prod)
        else:
            acc_ref = outs[-1]
            outs = outs[:-1]
            k = pl.program_id(2)

            @pl.when(k == 0)
            def _():
                acc_ref[...] = prod

            @pl.when(k > 0)
            def _():
                acc_ref[...] += prod

            @pl.when(k == nk - 1)
            def _():
                finish(acc_ref[...])

    return pl.pallas_call(
        body, name=name, grid=(M // tm, N // tn, nk), in_specs=in_specs, out_specs=out_specs, out_shape=out_shape,
        scratch_shapes=[pltpu.VMEM((tm, tn), F32)] if nk > 1 else [],
        compiler_params=_params())(*operands)


def _group_index(n_x_tiles, tiles_per_example, n_examples):
    def gidx(i):
        return jnp.where(i < n_x_tiles, i // tiles_per_example, n_examples)
    return gidx


def _norm_mod_fwd(x, g, shift, scale, gidx, name):
    T, Dm = x.shape

    def body(x_ref, g_ref, sh_ref, sc_ref, h_ref):
        xv = x_ref[...]
        r = lax.rsqrt(jnp.mean(xv * xv, axis=-1, keepdims=True) + EPS)
        y = xv * r * g_ref[...]
        h_ref[...] = (y * (1.0 + sc_ref[0]) + sh_ref[0]).astype(h_ref.dtype)

    row = pl.BlockSpec((ROW_TILE, Dm), lambda i: (i, 0))
    mod = pl.BlockSpec((1, 1, Dm), lambda i: (gidx(i), 0, 0))
    return pl.pallas_call(
        body, name=name, grid=(T // ROW_TILE,),
        in_specs=[row, pl.BlockSpec((1, Dm), lambda i: (0, 0)), mod, mod],
        out_specs=row, out_shape=jax.ShapeDtypeStruct((T, Dm), MXU_DTYPE),
        compiler_params=_params())(x, g, shift, scale)


def _first_of_group(i, gidx):
    return jnp.logical_or(i == 0, gidx(i) != gidx(jnp.maximum(i - 1, 0)))


def _norm_mod_bwd(dh, x, g, scale, dres, gidx, n_groups, name):
    T, Dm = x.shape

    def body(dh_ref, x_ref, g_ref, sc_ref, dres_ref, dx_ref, dsh_ref, dsc_ref, dg_ref):
        i = pl.program_id(0)
        xv, dhv = x_ref[...], dh_ref[...]
        r = lax.rsqrt(jnp.mean(xv * xv, axis=-1, keepdims=True) + EPS)
        xn = xv * r
        y = xn * g_ref[...]

        @pl.when(_first_of_group(i, gidx))
        def _():
            dsh_ref[...] = jnp.zeros_like(dsh_ref)
            dsc_ref[...] = jnp.zeros_like(dsc_ref)

        @pl.when(i == 0)
        def _():
            dg_ref[...] = jnp.zeros_like(dg_ref)

        dsh_ref[0] += jnp.sum(dhv, axis=0, keepdims=True)
        dsc_ref[0] += jnp.sum(dhv * y, axis=0, keepdims=True)
        dy = dhv * (1.0 + sc_ref[0])
        dg_ref[...] += jnp.sum(dy * xn, axis=0, keepdims=True)
        dxn = dy * g_ref[...]
        dx = r * (dxn - xn * jnp.mean(dxn * xn, axis=-1, keepdims=True))
        dx_ref[...] = dres_ref[...] + dx

    row = pl.BlockSpec((ROW_TILE, Dm), lambda i: (i, 0))
    mod = pl.BlockSpec((1, 1, Dm), lambda i: (gidx(i), 0, 0))
    vec = pl.BlockSpec((1, Dm), lambda i: (0, 0))
    return pl.pallas_call(
        body, name=name, grid=(T // ROW_TILE,),
        in_specs=[row, row, vec, mod, row],
        out_specs=(row, mod, mod, vec),
        out_shape=(jax.ShapeDtypeStruct((T, Dm), F32), jax.ShapeDtypeStruct((n_groups, 1, Dm), F32),
                   jax.ShapeDtypeStruct((n_groups, 1, Dm), F32), jax.ShapeDtypeStruct((1, Dm), F32)),
        compiler_params=_params())(dh, x, g, scale, dres)


def _gate_bwd(dy, f, gate, gidx, n_groups, name):
    T, Dm = dy.shape

    def body(dy_ref, f_ref, gate_ref, dz_ref, dgate_ref):
        i = pl.program_id(0)
        dyv = dy_ref[...]

        @pl.when(_first_of_group(i, gidx))
        def _():
            dgate_ref[...] = jnp.zeros_like(dgate_ref)

        dgate_ref[0] += jnp.sum(dyv * f_ref[...], axis=0, keepdims=True)
        dz_ref[...] = (dyv * gate_ref[0]).astype(dz_ref.dtype)

    row = pl.BlockSpec((ROW_TILE, Dm), lambda i: (i, 0))
    mod = pl.BlockSpec((1, 1, Dm), lambda i: (gidx(i), 0, 0))
    return pl.pallas_call(
        body, name=name, grid=(T // ROW_TILE,), in_specs=[row, row, mod], out_specs=(row, mod),
        out_shape=(jax.ShapeDtypeStruct((T, Dm), MXU_DTYPE), jax.ShapeDtypeStruct((n_groups, 1, Dm), F32)),
        compiler_params=_params())(dy, f, gate)


SWIGLU_ROWS = 256


def _swiglu_fwd(u, name):
    T = u.shape[0]

    def body(u_ref, a_ref):
        gate, up = u_ref[:, :D_FF], u_ref[:, D_FF:]
        a_ref[...] = (gate * jax.nn.sigmoid(gate) * up).astype(a_ref.dtype)

    return pl.pallas_call(
        body, name=name, grid=(T // SWIGLU_ROWS,),
        in_specs=[pl.BlockSpec((SWIGLU_ROWS, 2 * D_FF), lambda i: (i, 0))],
        out_specs=pl.BlockSpec((SWIGLU_ROWS, D_FF), lambda i: (i, 0)),
        out_shape=jax.ShapeDtypeStruct((T, D_FF), MXU_DTYPE), compiler_params=_params())(u)


def _swiglu_bwd(da, u, name):
    T = u.shape[0]

    def body(da_ref, u_ref, du_ref):
        gate, up, dav = u_ref[:, :D_FF], u_ref[:, D_FF:], da_ref[...]
        sg = jax.nn.sigmoid(gate)
        du_ref[:, :D_FF] = (dav * up * (sg * (1.0 + gate * (1.0 - sg)))).astype(du_ref.dtype)
        du_ref[:, D_FF:] = (dav * gate * sg).astype(du_ref.dtype)

    return pl.pallas_call(
        body, name=name, grid=(T // SWIGLU_ROWS,),
        in_specs=[pl.BlockSpec((SWIGLU_ROWS, D_FF), lambda i: (i, 0)),
                  pl.BlockSpec((SWIGLU_ROWS, 2 * D_FF), lambda i: (i, 0))],
        out_specs=pl.BlockSpec((SWIGLU_ROWS, 2 * D_FF), lambda i: (i, 0)),
        out_shape=jax.ShapeDtypeStruct((T, 2 * D_FF), MXU_DTYPE), compiler_params=_params())(da, u)


def _loss_fwd_bwd(y, target, name):
    T, Dm = y.shape

    def body(y_ref, t_ref, loss_ref, dy_ref):
        err = y_ref[...] - t_ref[...]

        @pl.when(pl.program_id(0) == 0)
        def _():
            loss_ref[...] = jnp.zeros_like(loss_ref)

        loss_ref[...] += 0.5 * jnp.sum(jnp.mean(err * err, axis=-1, keepdims=True))
        dy_ref[...] = err * (1.0 / Dm)

    row = pl.BlockSpec((ROW_TILE, Dm), lambda i: (i, 0))
    return pl.pallas_call(
        body, name=name, grid=(T // ROW_TILE,), in_specs=[row, row],
        out_specs=(pl.BlockSpec((8, LANES), lambda i: (0, 0)), row),
        out_shape=(jax.ShapeDtypeStruct((8, LANES), F32), jax.ShapeDtypeStruct((T, Dm), F32)),
        compiler_params=_params())(y, target)


def _rope_tables(seq, head_dim):
    axis_dim = head_dim // 2
    half = axis_dim // 2
    pos = jnp.arange(seq, dtype=jnp.int32)
    row = (pos // GRID_W).astype(F32)[:, None]
    col = (pos % GRID_W).astype(F32)[:, None]
    inv = ROPE_BASE ** (-jnp.arange(0, axis_dim, 2, dtype=F32) / axis_dim)
    lane = jnp.arange(head_dim, dtype=jnp.int32)
    within = lane % axis_dim
    ang = jnp.where((lane // axis_dim == 0)[None, :], row, col) * inv[within % half][None, :]
    cos = jnp.cos(ang)
    sin = jnp.where((within < half)[None, :], -jnp.sin(ang), jnp.sin(ang))
    cos = jnp.concatenate([cos, jnp.ones((ROW_TILE, head_dim), F32)], axis=0)
    sin = jnp.concatenate([sin, jnp.zeros((ROW_TILE, head_dim), F32)], axis=0)
    return cos, sin


def _pair_swap(v, half):
    if 2 * half == LANES:
        return pltpu.roll(v, half, axis=1)
    lane = lax.broadcasted_iota(jnp.int32, v.shape, 1)
    return jnp.where((lane % (2 * half)) < half, pltpu.roll(v, LANES - half, axis=1), pltpu.roll(v, half, axis=1))


def _head_sum(v, ones_ref):
    hi = v.astype(MXU_DTYPE)
    lo = (v - hi.astype(F32)).astype(MXU_DTYPE)
    return (jnp.dot(hi, ones_ref[...], preferred_element_type=F32)
            + jnp.dot(lo, ones_ref[...], preferred_element_type=F32))


def _head_ones():
    lane = jnp.arange(LANES)
    return (lane[:, None] // HEAD_DIM == lane[None, :] // HEAD_DIM).astype(MXU_DTYPE)


ATTN_QK_BLOCKS = (N_HEADS + N_KV_HEADS) * HEAD_DIM // LANES
ATTN_ALL_BLOCKS = (N_HEADS + 2 * N_KV_HEADS) * HEAD_DIM // LANES
ATTN_Q_BLOCKS = N_HEADS * HEAD_DIM // LANES
ATTN_SCALE = HEAD_DIM ** -0.5


def _attn_prep_fwd(qkv, gains, cos, sin, tidx, name):
    T, W = qkv.shape

    def body(x_ref, g_ref, cos_ref, sin_ref, ones_ref, o_ref):
        for cb in range(ATTN_ALL_BLOCKS):
            cols = slice(cb * LANES, (cb + 1) * LANES)
            xv = x_ref[:, cols]
            if cb < ATTN_QK_BLOCKS:
                r = lax.rsqrt(_head_sum(xv * xv, ones_ref) * (1.0 / HEAD_DIM) + EPS)
                y = xv * r * g_ref[0 if cb < ATTN_Q_BLOCKS else 1]
                xv = y * cos_ref[...] + _pair_swap(y, HEAD_DIM // 4) * sin_ref[...]
                if cb < ATTN_Q_BLOCKS:
                    xv = xv * ATTN_SCALE
            o_ref[:, cols] = xv.astype(o_ref.dtype)

    row = pl.BlockSpec((ROW_TILE, W), lambda i: (i, 0))
    tab = pl.BlockSpec((ROW_TILE, LANES), lambda i: (tidx(i), 0))
    return pl.pallas_call(
        body, name=name, grid=(T // ROW_TILE,),
        in_specs=[row, pl.BlockSpec((2, 1, LANES), lambda i: (0, 0, 0)), tab, tab,
                  pl.BlockSpec((LANES, LANES), lambda i: (0, 0))],
        out_specs=row, out_shape=jax.ShapeDtypeStruct(qkv.shape, MXU_DTYPE),
        compiler_params=_params())(qkv, gains, cos, sin, _head_ones())


def _attn_prep_bwd(dqk, dv, qkv, gains, cos, sin, tidx, name):
    T, W = qkv.shape
    qk_w = ATTN_QK_BLOCKS * LANES

    def body(dqk_ref, dv_ref, x_ref, g_ref, cos_ref, sin_ref, ones_ref, o_ref, dg_ref):
        @pl.when(pl.program_id(0) == 0)
        def _():
            dg_ref[...] = jnp.zeros_like(dg_ref)

        for cb in range(ATTN_QK_BLOCKS):
            cols = slice(cb * LANES, (cb + 1) * LANES)
            xv, d = x_ref[:, cols], dqk_ref[:, cols]
            if cb < ATTN_Q_BLOCKS:
                d = d * ATTN_SCALE
            r = lax.rsqrt(_head_sum(xv * xv, ones_ref) * (1.0 / HEAD_DIM) + EPS)
            xn = xv * r
            dy = d * cos_ref[...] + _pair_swap(d * sin_ref[...], HEAD_DIM // 4)
            dg_ref[:, cols] += jnp.sum(dy * xn, axis=0, keepdims=True)
            dxn = dy * g_ref[0 if cb < ATTN_Q_BLOCKS else 1]
            dx = r * (dxn - xn * (_head_sum(dxn * xn, ones_ref) * (1.0 / HEAD_DIM)))
            o_ref[:, cols] = dx.astype(o_ref.dtype)
        o_ref[:, qk_w:] = dv_ref[...].astype(o_ref.dtype)

    row = lambda w: pl.BlockSpec((ROW_TILE, w), lambda i: (i, 0))
    tab = pl.BlockSpec((ROW_TILE, LANES), lambda i: (tidx(i), 0))
    return pl.pallas_call(
        body, name=name, grid=(T // ROW_TILE,),
        in_specs=[row(qk_w), row(W - qk_w), row(W), pl.BlockSpec((2, 1, LANES), lambda i: (0, 0, 0)), tab, tab,
                  pl.BlockSpec((LANES, LANES), lambda i: (0, 0))],
        out_specs=(row(W), pl.BlockSpec((1, qk_w), lambda i: (0, 0))),
        out_shape=(jax.ShapeDtypeStruct(qkv.shape, MXU_DTYPE), jax.ShapeDtypeStruct((1, qk_w), F32)),
        compiler_params=_params())(dqk, dv, qkv, gains, cos, sin, _head_ones())


RET_QK_BLOCKS = 2 * RET_HEADS * RET_QK_DIM // LANES


def _ret_rope(x, cos, sin, tidx, name):
    T = x.shape[0]
    W = RET_QK_BLOCKS * LANES
    k_scale = RET_QK_DIM ** -0.5

    def body(x_ref, cos_ref, sin_ref, o_ref):
        for cb in range(RET_QK_BLOCKS):
            cols = slice(cb * LANES, (cb + 1) * LANES)
            tcols = slice((cb % 2) * LANES, (cb % 2 + 1) * LANES)
            xv = x_ref[:, cols]
            out = xv * cos_ref[:, tcols] + pltpu.roll(xv, LANES // 2, axis=1) * sin_ref[:, tcols]
            if cb >= RET_QK_BLOCKS // 2:
                out = out * k_scale
            o_ref[:, cols] = out

    row = pl.BlockSpec((ROW_TILE, W), lambda i: (i, 0))
    tab = pl.BlockSpec((ROW_TILE, RET_QK_DIM), lambda i: (tidx(i), 0))
    return pl.pallas_call(
        body, name=name, grid=(T // ROW_TILE,), in_specs=[row, tab, tab], out_specs=row,
        out_shape=jax.ShapeDtypeStruct((T, W), F32), compiler_params=_params())(x, cos, sin)


ASSEMBLE_ROWS = 256


def _ret_grad_assemble(x_parts, c_parts, dg, cos, sin, seq, name):
    NX, NC = x_parts[0].shape[0], c_parts[0].shape[0]
    T = NX + NC
    rt = ASSEMBLE_ROWS
    nxt = NX // rt
    qk_w = RET_HEADS * RET_QK_DIM
    k_scale = RET_QK_DIM ** -0.5

    def unrotate(d, cos_ref, sin_ref, scale):
        outs = []
        for cb in range(qk_w // LANES):
            cols = slice(cb * LANES, (cb + 1) * LANES)
            tcols = slice((cb % 2) * LANES, (cb % 2 + 1) * LANES)
            dv_ = d[:, cols]
            o = dv_ * cos_ref[:, tcols] + pltpu.roll(dv_ * sin_ref[:, tcols], LANES // 2, axis=1)
            outs.append(o * scale if scale != 1.0 else o)
        return outs

    def body(dqf, dqb, dkf, dkb, dvf, dvb, dg_ref, dkcf, dkcb, dvcf, dvcb, cos_ref, sin_ref, o_ref):
        i = pl.program_id(0)

        def write_k(parts):
            for cb, o in enumerate(parts):
                o_ref[:, qk_w + cb * LANES:qk_w + (cb + 1) * LANES] = o.astype(o_ref.dtype)

        @pl.when(i < nxt)
        def _():
            for cb, o in enumerate(unrotate(dqf[...] + dqb[...], cos_ref, sin_ref, 1.0)):
                o_ref[:, cb * LANES:(cb + 1) * LANES] = o.astype(o_ref.dtype)
            write_k(unrotate(dkf[...] + dkb[...], cos_ref, sin_ref, k_scale))
            o_ref[:, 2 * qk_w:2 * qk_w + RET_VWIDTH] = (dvf[...] + dvb[...]).astype(o_ref.dtype)
            o_ref[:, 2 * qk_w + RET_VWIDTH:] = dg_ref[...].astype(o_ref.dtype)

        @pl.when(i >= nxt)
        def _():
            o_ref[:, :qk_w] = jnp.zeros((rt, qk_w), o_ref.dtype)
            write_k(unrotate(dkcf[...] + dkcb[...], cos_ref, sin_ref, k_scale))
            o_ref[:, 2 * qk_w:2 * qk_w + RET_VWIDTH] = (dvcf[...] + dvcb[...]).astype(o_ref.dtype)
            o_ref[:, 2 * qk_w + RET_VWIDTH:] = jnp.zeros((rt, RET_VWIDTH), o_ref.dtype)

    xs = lambda w: pl.BlockSpec((rt, w), lambda i: (jnp.minimum(i, nxt - 1), 0))
    cs = lambda w: pl.BlockSpec((rt, w), lambda i: (jnp.maximum(i - nxt, 0), 0))
    tab = pl.BlockSpec((rt, RET_QK_DIM), lambda i: (jnp.where(i < nxt, i % (seq // rt), seq // rt), 0))
    return pl.pallas_call(
        body, name=name, grid=(T // rt,),
        in_specs=[xs(qk_w)] * 4 + [xs(RET_VWIDTH)] * 3 + [cs(qk_w)] * 2 + [cs(RET_VWIDTH)] * 2 + [tab, tab],
        out_specs=pl.BlockSpec((rt, 2 * qk_w + 2 * RET_VWIDTH), lambda i: (i, 0)),
        out_shape=jax.ShapeDtypeStruct((T, 2 * qk_w + 2 * RET_VWIDTH), MXU_DTYPE),
        compiler_params=_params())(*x_parts, dg, *c_parts, cos, sin)


def _band_bias(qb, seq):
    nb = seq // qb
    assert nb >= 2
    i = jnp.arange(GQA_GROUP * qb, dtype=jnp.int32)[:, None] % qb
    n = jnp.arange(3 * qb, dtype=jnp.int32)[None, :]
    in_window = (n >= i) & (n - i <= 2 * WINDOW)
    variants = [in_window & (n >= qb), in_window, in_window & (n < 2 * qb)]
    return jnp.stack([jnp.where(v, 0.0, NEG_INF).astype(F32) for v in variants])


GROUP_ORDER = (0, 2, 1, 3)


def _stack_halves(blk):
    return jnp.concatenate([blk[:, :LANES], blk[:, LANES:]], axis=0)


def _unstack_halves(v, rows):
    return jnp.concatenate([v[:rows], v[rows:]], axis=1)


def _align_head(pair, odd):
    lane = lax.broadcasted_iota(jnp.int32, pair.shape, 1)
    mine = jnp.where((lane >= HEAD_DIM) == odd, pair, jnp.zeros_like(pair))
    rolled = pltpu.roll(mine, HEAD_DIM, axis=1)
    return jnp.where(odd, rolled, mine), jnp.where(odd, mine, rolled)


def _scores(q2, x_eo):
    return jnp.concatenate([_dot(q2, x_eo[0], _NT), _dot(q2, x_eo[1], _NT)], axis=0)


def _apply(p, x_eo):
    half = p.shape[0] // 2
    return _dot(p[:half], x_eo[0], _NN) + _dot(p[half:], x_eo[1], _NN)


def _kv_grad(a, q2, odd):
    half = a.shape[0] // 2
    even_part = _dot(a[:half], q2, _TN)
    odd_part = _dot(a[half:], q2, _TN)
    lane = lax.broadcasted_iota(jnp.int32, even_part.shape, 1)
    low = (jnp.where(lane < HEAD_DIM, even_part, 0.0)
           + pltpu.roll(jnp.where(lane >= HEAD_DIM, odd_part, 0.0), HEAD_DIM, axis=1))
    return jnp.where(odd, pltpu.roll(low, HEAD_DIM, axis=1), low)


def _attn_probs(q2, kc_eo, kl_eo, bias, sink_ref, kv_head, qb):
    rows = GQA_GROUP * qb
    s_c = _scores(q2, kc_eo)
    blk = lax.broadcasted_iota(jnp.int32, (rows, 1), 0) // qb
    sink = jnp.zeros((rows, 1), F32)
    for t, gi in enumerate(GROUP_ORDER):
        sink = jnp.where(blk == t, sink_ref[kv_head, gi], sink)
    m = jnp.maximum(jnp.max(s_c, axis=-1, keepdims=True), sink)
    s_l = None
    if kl_eo is not None:
        s_l = _scores(q2, kl_eo) + bias
        m = jnp.maximum(m, jnp.max(s_l, axis=-1, keepdims=True))
    e_c = jnp.exp(s_c - m)
    e_s = jnp.exp(sink - m)
    den = jnp.sum(e_c, axis=-1, keepdims=True) + e_s
    e_l = None
    if kl_eo is not None:
        e_l = jnp.exp(s_l - m)
        den = den + jnp.sum(e_l, axis=-1, keepdims=True)
    inv = 1.0 / den
    return e_c * inv, (None if e_l is None else e_l * inv), e_s * inv


GROUP_W = GQA_GROUP * HEAD_DIM
K_LANE_BLOCK = N_HEADS * HEAD_DIM // LANES
V_LANE_BLOCK = K_LANE_BLOCK + N_KV_HEADS * HEAD_DIM // LANES


def _attn_specs(B, seq, ctx_len, ctx_queries):
    ctx0 = B * seq // ctx_len
    if ctx_queries:
        qb, nb = ctx_len, 1
        qrow = lambda b, j: ctx0 + b
    else:
        qb, nb = ATTN_BLOCK, seq // ATTN_BLOCK
        qrow = lambda b, j: b * nb + j
    q_spec = pl.BlockSpec((qb, GROUP_W), lambda b, k, j: (qrow(b, j), k))
    c_specs = [pl.BlockSpec((ctx_len, LANES), lambda b, k, j: (ctx0 + b, K_LANE_BLOCK + k // 2)),
               pl.BlockSpec((ctx_len, LANES), lambda b, k, j: (ctx0 + b, V_LANE_BLOCK + k // 2))]
    local = []
    if not ctx_queries:
        near = [lambda j: jnp.maximum(j - 1, 0), lambda j: j, lambda j: jnp.minimum(j + 1, nb - 1)]
        for lane0 in (K_LANE_BLOCK, V_LANE_BLOCK):
            for f in near:
                local.append(pl.BlockSpec((qb, LANES), lambda b, k, j, f=f, lane0=lane0: (b * nb + f(j), lane0 + k // 2)))
        local.append(pl.BlockSpec(
            (1, GQA_GROUP * qb, 3 * qb), lambda b, k, j: (jnp.where(j == 0, 0, jnp.where(j == nb - 1, 2, 1)), 0, 0)))
    return qb, nb, qrow, q_spec, c_specs, local


def _attn_operands(refs, has_local, kv_head):
    odd = (kv_head % 2) == 1
    n_local = 7 if has_local else 0
    q2 = _stack_halves(refs[0][...])
    kc = _align_head(refs[1 + n_local][...], odd)
    vc = _align_head(refs[2 + n_local][...], odd)
    kl = vl = bias = None
    if has_local:
        kl = _align_head(jnp.concatenate([r[...] for r in refs[1:4]], axis=0), odd)
        vl = _align_head(jnp.concatenate([r[...] for r in refs[4:7]], axis=0), odd)
        bias = refs[7][0]
    return odd, q2, kc, vc, kl, vl, bias


def _attn_fwd(qkv, sink, B, seq, ctx_len, ctx_queries, name):
    has_local = not ctx_queries
    qb, nb, _, q_spec, c_specs, local = _attn_specs(B, seq, ctx_len, ctx_queries)
    n_rows = B * (ctx_len if ctx_queries else seq)

    def body(*refs):
        sink_ref, o_ref = refs[-2:]
        kv_head = pl.program_id(1)
        _, q2, kc, vc, kl, vl, bias = _attn_operands(refs, has_local, kv_head)
        p_c, p_l, _ = _attn_probs(q2, kc, kl, bias, sink_ref, kv_head, qb)
        o2 = _apply(p_c, vc)
        if has_local:
            o2 = o2 + _apply(p_l, vl)
        o_ref[...] = _unstack_halves(o2, qb).astype(o_ref.dtype)

    operands = [qkv] + ([qkv] * 6 + [_band_bias(qb, seq)] if has_local else []) + [qkv, qkv, sink]
    return pl.pallas_call(
        body, name=name, grid=(B, N_KV_HEADS, nb),
        in_specs=[q_spec] + local + c_specs + [_SMEM],
        out_specs=pl.BlockSpec((qb, GROUP_W), lambda b, k, j: (b * nb + j, k)),
        out_shape=jax.ShapeDtypeStruct((n_rows, N_HEADS * HEAD_DIM), MXU_DTYPE), compiler_params=_params())(*operands)


def _attn_bwd(qkv, sink, do, B, seq, ctx_len, ctx_queries, name):
    has_local = not ctx_queries
    qb, nb, qrow, q_spec, c_specs, local = _attn_specs(B, seq, ctx_len, ctx_queries)
    n_rows = B * (ctx_len if ctx_queries else seq)

    def body(*refs):
        n_in = 1 + (7 if has_local else 0) + 4
        sink_ref, do_ref = refs[n_in - 2:n_in]
        outs = refs[n_in:]
        dq_ref = outs[0]
        dkc_ref, dvc_ref, dsink_ref = outs[-3:]
        b, kv_head, j = pl.program_id(0), pl.program_id(1), pl.program_id(2)
        odd, q2, kc, vc, kl, vl, bias = _attn_operands(refs, has_local, kv_head)
        do2 = _stack_halves(do_ref[...])
        p_c, p_l, p_s = _attn_probs(q2, kc, kl, bias, sink_ref, kv_head, qb)
        dp_c = _scores(do2, vc)
        delta = jnp.sum(p_c * dp_c, axis=-1, keepdims=True)
        if has_local:
            dp_l = _scores(do2, vl)
            delta = delta + jnp.sum(p_l * dp_l, axis=-1, keepdims=True)
        ds_c = p_c * (dp_c - delta)
        dq2 = _apply(ds_c, kc)

        @pl.when((kv_head % 2 == 0) & (j == 0))
        def _():
            dkc_ref[...] = jnp.zeros_like(dkc_ref)
            dvc_ref[...] = jnp.zeros_like(dvc_ref)
            if has_local:
                outs[1][...] = jnp.zeros_like(outs[1])
                outs[2][...] = jnp.zeros_like(outs[2])

        @pl.when((b == 0) & (kv_head == 0) & (j == 0))
        def _():
            dsink_ref[...] = jnp.zeros_like(dsink_ref)

        dkc_ref[...] += _kv_grad(ds_c, q2, odd)
        dvc_ref[...] += _kv_grad(p_c, do2, odd)
        if has_local:
            ds_l = p_l * (dp_l - delta)
            dq2 = dq2 + _apply(ds_l, kl)
            dkl = _kv_grad(ds_l, q2, odd)
            dvl = _kv_grad(p_l, do2, odd)
            dk_ref, dv_ref = outs[1], outs[2]
            for t in range(3):
                def add(t=t):
                    start = pl.multiple_of((j - 1 + t) * qb, qb)
                    dk_ref[pl.ds(start, qb), :] += dkl[t * qb:(t + 1) * qb]
                    dv_ref[pl.ds(start, qb), :] += dvl[t * qb:(t + 1) * qb]
                if t == 0:
                    pl.when(j > 0)(add)
                elif t == 2:
                    pl.when(j < nb - 1)(add)
                else:
                    add()
        dq_ref[...] = _unstack_halves(dq2, qb)
        dsk = -(p_s * delta)
        sub = lax.broadcasted_iota(jnp.int32, (8, LANES), 0)
        tile = jnp.zeros((8, LANES), F32)
        for t, gi in enumerate(GROUP_ORDER):
            tile = jnp.where(sub == gi, jnp.sum(dsk[t * qb:(t + 1) * qb]), tile)
        dsink_ref[pl.ds(pl.multiple_of(kv_head * 8, 8), 8), :] += tile

    kv_w = N_KV_HEADS * HEAD_DIM
    seq_spec = pl.BlockSpec((seq, LANES), lambda b, k, j: (b, k // 2))
    ctx_spec = pl.BlockSpec((ctx_len, LANES), lambda b, k, j: (b, k // 2))
    do_spec = pl.BlockSpec((qb, GROUP_W), lambda b, k, j: (qrow(b, j), k))
    operands = [qkv] + ([qkv] * 6 + [_band_bias(qb, seq)] if has_local else []) + [qkv, qkv, sink, do]
    out_specs = ([pl.BlockSpec((qb, GROUP_W), lambda b, k, j: (b * nb + j, k))] + ([seq_spec, seq_spec] if has_local else [])
                 + [ctx_spec, ctx_spec, pl.BlockSpec((32, LANES), lambda b, k, j: (0, 0))])
    out_shape = ([jax.ShapeDtypeStruct((n_rows, N_HEADS * HEAD_DIM), F32)]
                 + ([jax.ShapeDtypeStruct((B * seq, kv_w), F32)] * 2 if has_local else [])
                 + [jax.ShapeDtypeStruct((B * ctx_len, kv_w), F32)] * 2 + [jax.ShapeDtypeStruct((32, LANES), F32)])
    return pl.pallas_call(
        body, name=name, grid=(B, N_KV_HEADS, nb),
        in_specs=[q_spec] + local + c_specs + [_SMEM, do_spec],
        out_specs=tuple(out_specs), out_shape=tuple(out_shape), compiler_params=_params())(*operands)


def _ret_decays(lg, rev):
    n = lax.broadcasted_iota(jnp.int32, (RET_CHUNK, RET_CHUNK), 0).astype(F32)
    m = lax.broadcasted_iota(jnp.int32, (RET_CHUNK, RET_CHUNK), 1).astype(F32)
    pos = lax.broadcasted_iota(jnp.int32, (RET_CHUNK, 1), 0).astype(F32)
    diff = (m - n) if rev else (n - m)
    a_exp = jnp.maximum(diff, 0.0)
    intra = jnp.where(diff >= 0, jnp.exp(lg * a_exp), 0.0)
    q_exp = (RET_CHUNK - pos) if rev else (pos + 1.0)
    k_exp = pos if rev else (RET_CHUNK - 1.0 - pos)
    chunk = jnp.exp(jnp.full((1, 1), RET_CHUNK, F32) * lg)
    return intra, a_exp, jnp.exp(lg * q_exp), q_exp, jnp.exp(lg * k_exp), k_exp, chunk


def _ctx_decay(lg, ctx_len, rev):
    t = lax.broadcasted_iota(jnp.int32, (ctx_len, 1), 0).astype(F32)
    expo = t if rev else (ctx_len - 1.0 - t)
    return jnp.exp(lg * expo), expo


def _ret_specs(B, seq, ctx_len, order):
    nc = seq // RET_CHUNK
    x_blocks = B * seq // ctx_len

    def rows(b, c):
        return b * nc + order(c, nc)

    q_spec = pl.BlockSpec((RET_CHUNK, RET_QK_DIM), lambda b, h, c: (rows(b, c), h))
    k_spec = pl.BlockSpec((RET_CHUNK, RET_QK_DIM), lambda b, h, c: (rows(b, c), RET_HEADS + h))
    v_spec = pl.BlockSpec((RET_CHUNK, RET_V_DIM), lambda b, h, c: (rows(b, c), RET_HEADS + h))
    kc_spec = pl.BlockSpec((ctx_len, RET_QK_DIM), lambda b, h, c: (x_blocks + b, RET_HEADS + h))
    vc_spec = pl.BlockSpec((ctx_len, RET_V_DIM), lambda b, h, c: (x_blocks + b, RET_HEADS + h))
    st_spec = pl.BlockSpec((1, 1, 1, RET_QK_DIM, RET_V_DIM), lambda b, h, c: (b, h, order(c, nc), 0, 0))
    o_spec = pl.BlockSpec((RET_CHUNK, RET_V_DIM), lambda b, h, c: (rows(b, c), h))
    return nc, q_spec, k_spec, v_spec, kc_spec, vc_spec, st_spec, o_spec


_SCAN_UP = lambda c, nc: c
_SCAN_DOWN = lambda c, nc: nc - 1 - c


def _ret_fwd(qk, qkvg, log_g, B, seq, ctx_len, name):
    nc, qf, kf, vf, kc_spec, vc_spec, stf, of = _ret_specs(B, seq, ctx_len, _SCAN_UP)
    _, qr, kr, vr, _, _, str_, or_ = _ret_specs(B, seq, ctx_len, _SCAN_DOWN)

    def body(lg_ref, qf_ref, kf_ref, vf_ref, qr_ref, kr_ref, vr_ref, kc_ref, vc_ref,
             of_ref, stf_ref, or_ref, str_ref, state_f, state_r):
        h, c = pl.program_id(1), pl.program_id(2)
        dirs = ((False, lg_ref[0, h], qf_ref, kf_ref, vf_ref, of_ref, stf_ref, state_f),
                (True, lg_ref[1, h], qr_ref, kr_ref, vr_ref, or_ref, str_ref, state_r))

        @pl.when(c == 0)
        def _():
            for rev, lg, _, _, _, _, _, state in dirs:
                dec, _ = _ctx_decay(lg, ctx_len, rev)
                state[...] = _dot(kc_ref[...] * dec, vc_ref[...], _TN)

        for rev, lg, q_ref, k_ref, v_ref, o_ref, st_ref, state in dirs:
            intra, _, q_dec, _, k_dec, _, chunk_dec = _ret_decays(lg, rev)
            qv, kv, vv = q_ref[...], k_ref[...], v_ref[...]
            s_in = state[...]
            st_ref[0, 0, 0] = s_in
            w = _dot(qv, kv, _NT) * intra
            o_ref[...] = _dot(w, vv, _NN) + _dot(qv, s_in, _NN) * q_dec
            state[...] = s_in * chunk_dec + _dot(kv * k_dec, vv, _TN)

    o_shape = jax.ShapeDtypeStruct((B * seq, RET_VWIDTH), F32)
    st_shape = jax.ShapeDtypeStruct((B, RET_HEADS, nc, RET_QK_DIM, RET_V_DIM), F32)
    return pl.pallas_call(
        body, name=name, grid=(B, RET_HEADS, nc),
        in_specs=[_SMEM, qf, kf, vf, qr, kr, vr, kc_spec, vc_spec],
        out_specs=(of, stf, or_, str_), out_shape=(o_shape, st_shape, o_shape, st_shape),
        scratch_shapes=[pltpu.VMEM((RET_QK_DIM, RET_V_DIM), F32)] * 2,
        compiler_params=_params())(log_g, qk, qk, qkvg, qk, qk, qkvg, qk, qkvg)


def _ret_bwd_chunk(rev, lg, q_ref, k_ref, v_ref, st_ref, do_ref, dq_ref, dk_ref, dv_ref, dlg_ref, dstate):
    intra, a_exp, q_dec, q_exp, k_dec, k_exp, chunk_dec = _ret_decays(lg, rev)
    qv, kv, vv, dov = q_ref[...], k_ref[...], v_ref[...], do_ref[...]
    s_in, ds_out = st_ref[0, 0, 0], dstate[...]
    p = _dot(qv, kv, _NT)
    w = p * intra
    dw = _dot(dov, vv, _NT)
    dp = dw * intra
    do_dec = dov * q_dec
    kd = kv * k_dec
    v_ds = _dot(vv, ds_out, _NT)
    dq_ref[...] = _dot(dp, kv, _NN) + _dot(do_dec, s_in, _NT)
    dk_ref[...] = _dot(dp, qv, _TN) + v_ds * k_dec
    dv_ref[...] = _dot(w, dov, _TN) + _dot(kd, ds_out, _NN)
    q_s = _dot(qv, s_in, _NN)
    dlg = (jnp.sum(dw * w * a_exp)
           + jnp.sum(q_exp * q_dec * jnp.sum(dov * q_s, axis=-1, keepdims=True))
           + jnp.sum(k_exp * k_dec * jnp.sum(kv * v_ds, axis=-1, keepdims=True))
           + RET_CHUNK * jnp.sum(chunk_dec * (ds_out * s_in)))
    ds_in = ds_out * chunk_dec + _dot(qv, do_dec, _TN)
    dstate[...] = ds_in
    dlg_ref[...] += dlg
    return ds_in


def _ret_bwd(qk, qkvg, log_g, st_f, st_r, do, B, seq, ctx_len, name):
    nc, qf, kf, vf, kc_spec, vc_spec, stf, of = _ret_specs(B, seq, ctx_len, _SCAN_DOWN)
    _, qr, kr, vr, _, _, str_, or_ = _ret_specs(B, seq, ctx_len, _SCAN_UP)

    def body(lg_ref, qf_ref, kf_ref, vf_ref, stf_ref, dof_ref, qr_ref, kr_ref, vr_ref, str_ref, dor_ref, kc_ref, vc_ref,
             dqf, dkf, dvf, dkcf, dvcf, dlgf, dqr, dkr, dvr, dkcr, dvcr, dlgr, dstate_f, dstate_r):
        h, c = pl.program_id(1), pl.program_id(2)
        dirs = ((False, lg_ref[0, h], (qf_ref, kf_ref, vf_ref, stf_ref, dof_ref, dqf, dkf, dvf, dlgf, dstate_f), dkcf, dvcf),
                (True, lg_ref[1, h], (qr_ref, kr_ref, vr_ref, str_ref, dor_ref, dqr, dkr, dvr, dlgr, dstate_r), dkcr, dvcr))

        @pl.when(c == 0)
        def _():
            for _, _, refs, _, _ in dirs:
                refs[-1][...] = jnp.zeros_like(refs[-1])
                refs[-2][...] = jnp.zeros_like(refs[-2])

        ds_first = [_ret_bwd_chunk(rev, lg, *refs) for rev, lg, refs, _, _ in dirs]

        @pl.when(c == nc - 1)
        def _():
            for (rev, lg, refs, dkc_ref, dvc_ref), ds_in in zip(dirs, ds_first):
                dec, expo = _ctx_decay(lg, ctx_len, rev)
                kcv, vcv = kc_ref[...], vc_ref[...]
                vc_ds = _dot(vcv, ds_in, _NT)
                dkc_ref[...] = vc_ds * dec
                dvc_ref[...] = _dot(kcv * dec, ds_in, _NN)
                refs[-2][...] += jnp.sum(expo * dec * jnp.sum(kcv * vc_ds, axis=-1, keepdims=True))

    def outs(q_spec, o_spec):
        return (pl.BlockSpec((RET_CHUNK, RET_QK_DIM), q_spec.index_map),
                pl.BlockSpec((RET_CHUNK, RET_QK_DIM), q_spec.index_map), o_spec,
                pl.BlockSpec((ctx_len, RET_QK_DIM), lambda b, h, c: (b, h)),
                pl.BlockSpec((ctx_len, RET_V_DIM), lambda b, h, c: (b, h)),
                pl.BlockSpec((1, 1, 8, LANES), lambda b, h, c: (b, h, 0, 0)))

    shapes = (jax.ShapeDtypeStruct((B * seq, RET_HEADS * RET_QK_DIM), F32),
              jax.ShapeDtypeStruct((B * seq, RET_HEADS * RET_QK_DIM), F32),
              jax.ShapeDtypeStruct((B * seq, RET_VWIDTH), F32),
              jax.ShapeDtypeStruct((B * ctx_len, RET_HEADS * RET_QK_DIM), F32),
              jax.ShapeDtypeStruct((B * ctx_len, RET_VWIDTH), F32),
              jax.ShapeDtypeStruct((B, RET_HEADS, 8, LANES), F32))
    res = pl.pallas_call(
        body, name=name, grid=(B, RET_HEADS, nc),
        in_specs=[_SMEM, qf, kf, vf, stf, of, qr, kr, vr, str_, or_, kc_spec, vc_spec],
        out_specs=outs(qf, of) + outs(qr, or_), out_shape=shapes + shapes,
        scratch_shapes=[pltpu.VMEM((RET_QK_DIM, RET_V_DIM), F32)] * 2,
        compiler_params=_params())(log_g, qk, qk, qkvg, st_f, do, qk, qk, qkvg, st_r, do, qk, qkvg)
    return res[:6], res[6:]


def _gated_out_fwd(o_f, o_b, qkvg, gn_gain, name):
    T = o_f.shape[0]
    g_off = (2 * RET_HEADS * RET_QK_DIM + RET_VWIDTH) // RET_V_DIM

    def body(of_ref, ob_ref, g_ref, gain_ref, z_ref):
        o = of_ref[...] + ob_ref[...]
        mu = jnp.mean(o, axis=-1, keepdims=True)
        var = jnp.mean(jnp.square(o - mu), axis=-1, keepdims=True)
        y = (o - mu) * lax.rsqrt(var + EPS) * gain_ref[...]
        gv = g_ref[...]
        z_ref[...] = (gv * jax.nn.sigmoid(gv) * y).astype(z_ref.dtype)

    blk = pl.BlockSpec((ROW_TILE, RET_V_DIM), lambda i, h: (i, h))
    return pl.pallas_call(
        body, name=name, grid=(T // ROW_TILE, RET_HEADS),
        in_specs=[blk, blk, pl.BlockSpec((ROW_TILE, RET_V_DIM), lambda i, h: (i, g_off + h)),
                  pl.BlockSpec((1, RET_V_DIM), lambda i, h: (0, h))],
        out_specs=blk, out_shape=jax.ShapeDtypeStruct((T, RET_VWIDTH), MXU_DTYPE),
        compiler_params=_params())(o_f, o_b, qkvg, gn_gain)


def _gated_out_bwd(dz, o_f, o_b, qkvg, gn_gain, name):
    T = o_f.shape[0]
    g_off = (2 * RET_HEADS * RET_QK_DIM + RET_VWIDTH) // RET_V_DIM

    def body(dz_ref, of_ref, ob_ref, g_ref, gain_ref, do_ref, dg_ref, dgain_ref):
        o = of_ref[...] + ob_ref[...]
        mu = jnp.mean(o, axis=-1, keepdims=True)
        var = jnp.mean(jnp.square(o - mu), axis=-1, keepdims=True)
        rstd = lax.rsqrt(var + EPS)
        yhat = (o - mu) * rstd
        gv, dzv = g_ref[...], dz_ref[...]
        sg = jax.nn.sigmoid(gv)
        dg_ref[...] = (dzv * (yhat * gain_ref[...]) * (sg * (1.0 + gv * (1.0 - sg)))).astype(dg_ref.dtype)
        dy = dzv * (gv * sg)

        @pl.when(pl.program_id(1) == 0)
        def _():
            dgain_ref[...] = jnp.zeros_like(dgain_ref)

        dgain_ref[...] += jnp.sum(dy * yhat, axis=0, keepdims=True)
        dyh = dy * gain_ref[...]
        do_ref[...] = rstd * (dyh - jnp.mean(dyh, axis=-1, keepdims=True)
                              - yhat * jnp.mean(dyh * yhat, axis=-1, keepdims=True))

    blk = pl.BlockSpec((ROW_TILE, RET_V_DIM), lambda h, i: (i, h))
    vec = pl.BlockSpec((1, RET_V_DIM), lambda h, i: (0, h))
    return pl.pallas_call(
        body, name=name, grid=(RET_HEADS, T // ROW_TILE),
        in_specs=[blk, blk, blk, pl.BlockSpec((ROW_TILE, RET_V_DIM), lambda h, i: (i, g_off + h)), vec],
        out_specs=(blk, blk, vec),
        out_shape=(jax.ShapeDtypeStruct((T, RET_VWIDTH), F32), jax.ShapeDtypeStruct((T, RET_VWIDTH), MXU_DTYPE),
                   jax.ShapeDtypeStruct((1, RET_VWIDTH), F32)),
        compiler_params=_params())(dz, o_f, o_b, qkvg, gn_gain)


def _adamw(w, m, v, parts, name):
    R, C = w.shape
    tr = _tile(R, (256, 128, 64, 32, 16, 8))
    n_parts = [p.shape[0] for p in parts]

    def body(*refs):
        w_ref, m_ref, v_ref = refs[:3]
        part_refs = refs[3:3 + len(parts)]
        g_ref, d_ref, nm_ref, nv_ref = refs[3 + len(parts):]
        g = None
        for ref, n in zip(part_refs, n_parts):
            for r in range(n):
                term = ref[r].astype(F32)
                g = term if g is None else g + term
        mn = ADAM_B1 * m_ref[...] + (1.0 - ADAM_B1) * g
        vn = ADAM_B2 * v_ref[...] + (1.0 - ADAM_B2) * jnp.square(g)
        m_hat = mn / (1.0 - ADAM_B1 ** ADAM_STEP)
        v_hat = vn / (1.0 - ADAM_B2 ** ADAM_STEP)
        g_ref[...] = g
        d_ref[...] = -ADAM_LR * (m_hat / (jnp.sqrt(v_hat) + ADAM_EPS) + ADAM_WD * w_ref[...])
        nm_ref[...] = mn
        nv_ref[...] = vn

    blk = pl.BlockSpec((tr, C), lambda i: (i, 0))
    part_specs = [pl.BlockSpec((n, tr, C), lambda i: (0, i, 0)) for n in n_parts]
    shp = jax.ShapeDtypeStruct((R, C), F32)
    return pl.pallas_call(
        body, name=name, grid=(R // tr,), in_specs=[blk, blk, blk] + part_specs,
        out_specs=(blk, blk, blk, blk), out_shape=(shp, shp, shp, shp),
        compiler_params=_params())(w, m, v, *parts)


def _sum_rows(parts, name):
    n, R, C = parts.shape
    tr = _tile(R, (256, 128, 64, 32, 16, 8))

    def body(p_ref, o_ref):
        acc = p_ref[0]
        for r in range(1, n):
            acc = acc + p_ref[r]
        o_ref[...] = acc

    return pl.pallas_call(
        body, name=name, grid=(R // tr,), in_specs=[pl.BlockSpec((n, tr, C), lambda i: (0, i, 0))],
        out_specs=pl.BlockSpec((tr, C), lambda i: (i, 0)), out_shape=jax.ShapeDtypeStruct((R, C), F32),
        compiler_params=_params())(parts)


def _my_coords():
    return lax.axis_index("x"), lax.axis_index("y"), lax.axis_index("c")


def _flip(coord, bit):
    return 1 - coord if bit else coord


def _all_gather(x2d, name):
    R, C = x2d.shape

    def body(x_ref, out_ref, send_sems, recv_sems, local_sem):
        x, y, c = _my_coords()
        me, sibling = (x, y, c), (x, y, 1 - c)
        chips = [(1 - x, y), (x, 1 - y), (1 - x, 1 - y)]

        def rows(px, py, pc):
            return out_ref.at[4 * px + 2 * py + pc]

        def copy(k, block, to, src=None):
            return pltpu.make_async_remote_copy(
                src_ref=rows(*block) if src is None else src, dst_ref=rows(*block),
                send_sem=send_sems.at[k], recv_sem=recv_sems.at[k], device_id=to, device_id_type=MESH)

        mine = pltpu.make_async_copy(x_ref, rows(*me), local_sem)
        mine.start()
        first = [copy(0, me, sibling, src=x_ref)]
        first += [copy(1 + j, me, (*chip, c), src=x_ref) for j, chip in enumerate(chips)]
        for cp in first:
            cp.start()
        passed = [copy(4 + j, (*chip, c), sibling) for j, chip in enumerate(chips)]
        for j, chip in enumerate(chips):
            copy(1 + j, (*chip, c), me).wait_recv()
            passed[j].start()
        copy(0, sibling, me).wait_recv()
        for j, chip in enumerate(chips):
            copy(4 + j, (*chip, 1 - c), me).wait_recv()
        for cp in first + passed:
            cp.wait_send()
        mine.wait()

    return pl.pallas_call(
        body, name=name, out_shape=jax.ShapeDtypeStruct((N_DEV, R, C), x2d.dtype),
        in_specs=[_ANY], out_specs=_ANY,
        scratch_shapes=[pltpu.SemaphoreType.DMA((7,)), pltpu.SemaphoreType.DMA((7,)), pltpu.SemaphoreType.DMA],
    )(x2d)


BIG_WEIGHTS = {
    "ffn_w_in": (2, (2, D_MODEL, 2 * D_FF)),
    "ffn_w_out": (1, (2, D_FF, D_MODEL)),
    "attn_w_qkv": (2, (1, D_MODEL, (N_HEADS + 2 * N_KV_HEADS) * HEAD_DIM)),
    "attn_w_o": (1, (1, N_HEADS * HEAD_DIM, D_MODEL)),
    "ret_w_qkvg": (2, (1, D_MODEL, 2 * D_MODEL + 2 * RET_VWIDTH)),
    "ret_gn_g": (2, (1, 1, RET_VWIDTH)),
    "ret_w_o": (1, (1, RET_VWIDTH, D_MODEL)),
}


def _join_shards(name, stacked):
    axis, full = BIG_WEIGHTS[name]
    if axis == 2:
        stacked = stacked.transpose(0, 2, 1, 3)
    return stacked.reshape(full)


def _split_shards(name, full_arr):
    axis, full = BIG_WEIGHTS[name]
    L, rows, cols = full
    if axis == 2:
        return full_arr.reshape(L, rows, N_DEV, cols // N_DEV).transpose(0, 2, 1, 3)
    return full_arr.reshape(L, N_DEV, rows // N_DEV, cols)


def _gather_shards(shards, name):
    n = len(shards)

    def body(*refs):
        x_refs, out_refs = refs[:n], refs[n:2 * n]
        send_sems, recv_sems, local_sems = refs[2 * n:]
        x, y, c = _my_coords()
        me, sibling = (x, y, c), (x, y, 1 - c)
        chips = [(1 - x, y), (x, 1 - y), (1 - x, 1 - y)]

        def rows(a, px, py, pc):
            return out_refs[a].at[:, 4 * px + 2 * py + pc]

        def copy(a, k, block, to, src=None):
            return pltpu.make_async_remote_copy(
                src_ref=rows(a, *block) if src is None else src, dst_ref=rows(a, *block),
                send_sem=send_sems.at[7 * a + k], recv_sem=recv_sems.at[7 * a + k], device_id=to, device_id_type=MESH)

        mine = [pltpu.make_async_copy(x_refs[a], rows(a, *me), local_sems.at[a]) for a in range(n)]
        for cp in mine:
            cp.start()
        first = []
        for a in range(n):
            first.append(copy(a, 0, me, sibling, src=x_refs[a]))
            first += [copy(a, 1 + j, me, (*chip, c), src=x_refs[a]) for j, chip in enumerate(chips)]
        for cp in first:
            cp.start()
        passed = []
        for j, chip in enumerate(chips):
            for a in range(n):
                copy(a, 1 + j, (*chip, c), me).wait_recv()
                fwd = copy(a, 4 + j, (*chip, c), sibling)
                fwd.start()
                passed.append(fwd)
        for a in range(n):
            copy(a, 0, sibling, me).wait_recv()
            for j, chip in enumerate(chips):
                copy(a, 4 + j, (*chip, 1 - c), me).wait_recv()
        for cp in first + passed:
            cp.wait_send()
        for cp in mine:
            cp.wait()

    return pl.pallas_call(
        body, name=name,
        out_shape=[jax.ShapeDtypeStruct((s.shape[0], N_DEV) + s.shape[1:], s.dtype) for s in shards],
        in_specs=[_ANY] * n, out_specs=[_ANY] * n,
        scratch_shapes=[pltpu.SemaphoreType.DMA((7 * n,)), pltpu.SemaphoreType.DMA((7 * n,)),
                        pltpu.SemaphoreType.DMA((n,))],
    )(*shards)


def _exchange_shards(arrs, masks, src_of, out_tail, name):
    n, nm = len(arrs), len(masks)

    def body(*refs):
        in_refs, out_refs = refs[:n], refs[n:2 * n]
        send_sems, recv_sems = refs[2 * n:]
        x, y, c = _my_coords()
        copies = []
        for a in range(n):
            for k, (bx, by, bc) in enumerate(masks):
                peer = (_flip(x, bx), _flip(y, by), _flip(c, bc))
                copies.append(pltpu.make_async_remote_copy(
                    src_ref=src_of(in_refs[a], peer, (x, y, c)), dst_ref=out_refs[a].at[k],
                    send_sem=send_sems.at[nm * a + k], recv_sem=recv_sems.at[nm * a + k],
                    device_id=peer, device_id_type=MESH))
        for cp in copies:
            cp.start()
        for cp in copies:
            cp.wait()

    return pl.pallas_call(
        body, name=name,
        out_shape=[jax.ShapeDtypeStruct((nm,) + out_tail(s), s.dtype) for s in arrs],
        in_specs=[_ANY] * n, out_specs=[_ANY] * n,
        scratch_shapes=[pltpu.SemaphoreType.DMA((nm * n,)), pltpu.SemaphoreType.DMA((nm * n,))],
    )(*arrs)


def _pair_sum(g, from_sibling, core, out_dtype, name):
    L, _, _, a, b = g.shape
    ta = a

    def body(core_ref, g_ref, s_ref, o_ref):
        o_ref[...] = (g_ref[...] + s_ref[...]).astype(out_dtype)

    blk = pl.BlockSpec((1, 1, ta, b), lambda l, q, i, core_ref: (l, q, i, 0))
    return pl.pallas_call(
        body, name=name,
        grid_spec=pltpu.PrefetchScalarGridSpec(
            num_scalar_prefetch=1, grid=(L, 4, a // ta),
            in_specs=[pl.BlockSpec((1, 1, pl.Squeezed(), ta, b), lambda l, q, i, core_ref: (l, q, core_ref[0], i, 0)), blk],
            out_specs=blk),
        out_shape=jax.ShapeDtypeStruct((L, 4, a, b), out_dtype), compiler_params=_params())(core, g, from_sibling)


def _mods(mod_x, mod_c, layer):
    both = jnp.concatenate([mod_x[:, layer], mod_c[layer][None]], axis=0)
    return [both[:, None, k * D_MODEL:(k + 1) * D_MODEL] for k in range(6)]


def _local_step(x, ctx, target, mod_x, mod_c, w, small):
    B, S, _ = x.shape
    L = ctx.shape[1]
    NX, NC = B * S, B * L
    T = NX + NC
    tiles_per_ex = S // ROW_TILE
    nxt = NX // ROW_TILE
    gidx = _group_index(nxt, tiles_per_ex, B)
    gidx_for = lambda rows: _group_index(NX // rows, S // rows, B)
    mm_rows = _tile(S, (MM_ROWS, ROW_TILE))
    tidx = lambda i: jnp.where(i < nxt, i % tiles_per_ex, tiles_per_ex)
    G = B + 1
    x0 = jnp.concatenate([x.reshape(NX, D_MODEL), ctx.reshape(NC, D_MODEL)], axis=0)
    acos, asin = [jnp.tile(t, (1, LANES // HEAD_DIM)) for t in _rope_tables(S, HEAD_DIM)]
    rcos, rsin = _rope_tables(S, RET_QK_DIM)
    sink = small["attn_sink"].reshape(N_KV_HEADS, GQA_GROUP)
    gains = jnp.stack([jnp.tile(small["attn_q_norm"].reshape(1, HEAD_DIM), (1, LANES // HEAD_DIM)),
                       jnp.tile(small["attn_k_norm"].reshape(1, HEAD_DIM), (1, LANES // HEAD_DIM))])
    log_g = jax.nn.log_sigmoid(small["ret_decay_logit"].reshape(2, RET_HEADS))
    n1, n2 = small["norm1_g"], small["norm2_g"]

    m0 = _mods(mod_x, mod_c, 0)
    h1 = _norm_mod_fwd(x0, n1[0:1], m0[0], m0[1], gidx, "l0_norm1")
    qkv = _mm(h1, w["attn_w_qkv"][0], "nn", F32, "l0_qkv")
    qkv_r = _attn_prep_fwd(qkv, gains, acos, asin, tidx, "l0_qk_prep")
    o_x = _attn_fwd(qkv_r, sink, B, S, L, False, "l0_attn_x")
    o_c = _attn_fwd(qkv_r, sink, B, S, L, True, "l0_attn_c")
    o0 = jnp.concatenate([o_x, o_c], axis=0)
    mo0, x1 = _mm(o0, w["attn_w_o"][0], "nn", F32, "l0_attn_out", res=x0, gate=m0[2], gidx_for=gidx_for, gate_rows=mm_rows)
    h2 = _norm_mod_fwd(x1, n2[0:1], m0[3], m0[4], gidx, "l0_norm2")
    u0 = _mm(h2, w["ffn_w_in"][0], "nn", F32, "l0_ffn_in")
    a0 = _swiglu_fwd(u0, "l0_swiglu")
    f0, x2 = _mm(a0, w["ffn_w_out"][0], "nn", F32, "l0_ffn_out", res=x1, gate=m0[5], gidx_for=gidx_for, gate_rows=mm_rows)

    m1 = _mods(mod_x, mod_c, 1)
    g1 = _norm_mod_fwd(x2, n1[1:2], m1[0], m1[1], gidx, "l1_norm1")
    qkvg = _mm(g1, w["ret_w_qkvg"][0], "nn", F32, "l1_qkvg")
    qk = _ret_rope(qkvg, rcos, rsin, tidx, "l1_rope")
    of, st_f, ob, st_b = _ret_fwd(qk, qkvg, log_g, B, S, L, "l1_ret")
    gn = w["ret_gn_g"].reshape(1, RET_VWIDTH)
    z1 = _gated_out_fwd(of, ob, qkvg, gn, "l1_gated_out")
    xx2 = x2[:NX]
    gx = lambda i: i // tiles_per_ex
    m1x = [t[:B] for t in m1]
    mo1, y1 = _mm(z1, w["ret_w_o"][0], "nn", F32, "l1_ret_out", res=xx2, gate=m1x[2], gidx_for=gidx_for, gate_rows=mm_rows)
    k2 = _norm_mod_fwd(y1, n2[1:2], m1x[3], m1x[4], gx, "l1_norm2")
    u1 = _mm(k2, w["ffn_w_in"][1], "nn", F32, "l1_ffn_in")
    a1 = _swiglu_fwd(u1, "l1_swiglu")
    f1, y2 = _mm(a1, w["ffn_w_out"][1], "nn", F32, "l1_ffn_out", res=y1, gate=m1x[5], gidx_for=gidx_for, gate_rows=mm_rows)

    loss_tile, dy2 = _loss_fwd_bwd(y2, target.reshape(NX, D_MODEL), "loss")

    zg = jnp.zeros((1, 1, D_MODEL), F32)
    dz, dgate5_1 = _gate_bwd(dy2, f1, m1x[5], gx, B, "l1_ffn_gate_bwd")
    gw_ffn_out1 = _mm(a1, dz, "tn", F32, "l1_ffn_out_dw")
    da = _mm(dz, w["ffn_w_out"][1], "nt", F32, "l1_ffn_out_dx")
    du = _swiglu_bwd(da, u1, "l1_swiglu_bwd")
    gw_ffn_in1 = _mm(k2, du, "tn", F32, "l1_ffn_in_dw")
    dk2 = _mm(du, w["ffn_w_in"][1], "nt", F32, "l1_ffn_in_dx")
    dy1, dsh3_1, dsc4_1, dn2_1 = _norm_mod_bwd(dk2, y1, n2[1:2], m1x[4], dy2, gx, B, "l1_norm2_bwd")
    dzo, dgate2_1 = _gate_bwd(dy1, mo1, m1x[2], gx, B, "l1_ret_gate_bwd")
    gw_ret_o = _mm(z1, dzo, "tn", F32, "l1_ret_out_dw")
    dz1 = _mm(dzo, w["ret_w_o"][0], "nt", F32, "l1_ret_out_dx")
    do_r, dg_r, dgn = _gated_out_bwd(dz1, of, ob, qkvg, gn, "l1_gated_out_bwd")
    ((dq_f, dk_f, dv_f, dkc_f, dvc_f, dlg_f),
     (dq_b, dk_b, dv_b, dkc_b, dvc_b, dlg_b)) = _ret_bwd(qk, qkvg, log_g, st_f, st_b, do_r, B, S, L, "l1_ret_bwd")
    dqkvg = _ret_grad_assemble((dq_f, dq_b, dk_f, dk_b, dv_f, dv_b), (dkc_f, dkc_b, dvc_f, dvc_b), dg_r, rcos, rsin, S,
                               "l1_qkvg_grad")
    gw_ret_qkvg = _mm(g1, dqkvg, "tn", F32, "l1_qkvg_dw")
    dg1 = _mm(dqkvg, w["ret_w_qkvg"][0], "nt", F32, "l1_qkvg_dx")
    dres1 = jnp.concatenate([dy1, jnp.zeros((NC, D_MODEL), F32)], axis=0)
    dx2, dsh0_1, dsc1_1, dn1_1 = _norm_mod_bwd(dg1, x2, n1[1:2], m1[1], dres1, gidx, G, "l1_norm1_bwd")
    dlg = jnp.stack([jnp.sum(dlg_f[:, :, 0, 0], axis=0), jnp.sum(dlg_b[:, :, 0, 0], axis=0)])
    d_decay = (dlg * jax.nn.sigmoid(-small["ret_decay_logit"].reshape(2, RET_HEADS))).reshape(1, 2, RET_HEADS)

    dz, dgate5_0 = _gate_bwd(dx2, f0, m0[5], gidx, G, "l0_ffn_gate_bwd")
    gw_ffn_out0 = _mm(a0, dz, "tn", F32, "l0_ffn_out_dw")
    da = _mm(dz, w["ffn_w_out"][0], "nt", F32, "l0_ffn_out_dx")
    du = _swiglu_bwd(da, u0, "l0_swiglu_bwd")
    gw_ffn_in0 = _mm(h2, du, "tn", F32, "l0_ffn_in_dw")
    dh2 = _mm(du, w["ffn_w_in"][0], "nt", F32, "l0_ffn_in_dx")
    dx1, dsh3_0, dsc4_0, dn2_0 = _norm_mod_bwd(dh2, x1, n2[0:1], m0[4], dx2, gidx, G, "l0_norm2_bwd")
    dzo, dgate2_0 = _gate_bwd(dx1, mo0, m0[2], gidx, G, "l0_attn_gate_bwd")
    gw_attn_o = _mm(o0, dzo, "tn", F32, "l0_attn_out_dw")
    do0 = _mm(dzo, w["attn_w_o"][0], "nt", MXU_DTYPE, "l0_attn_out_dx")
    dq_x, dk_x, dv_x, dkc1, dvc1, dsink_x = _attn_bwd(qkv_r, sink, do0, B, S, L, False, "l0_attn_x_bwd")
    dq_c, dkc2, dvc2, dsink_c = _attn_bwd(qkv_r, sink, do0, B, S, L, True, "l0_attn_c_bwd")
    dqk = jnp.concatenate([jnp.concatenate([dq_x, dk_x], axis=1), jnp.concatenate([dq_c, dkc1 + dkc2], axis=1)], axis=0)
    dvv = jnp.concatenate([dv_x, dvc1 + dvc2], axis=0)
    dqkv, dgains = _attn_prep_bwd(dqk, dvv, qkv, gains, acos, asin, tidx, "l0_qk_prep_bwd")
    gw_attn_qkv = _mm(h1, dqkv, "tn", F32, "l0_qkv_dw")
    dh1 = _mm(dqkv, w["attn_w_qkv"][0], "nt", F32, "l0_qkv_dx")
    dx0, dsh0_0, dsc1_0, dn1_0 = _norm_mod_bwd(dh1, x0, n1[0:1], m0[1], dx1, gidx, G, "l0_norm1_bwd")

    dgains = jnp.sum(dgains.reshape(ATTN_QK_BLOCKS, LANES // HEAD_DIM, HEAD_DIM), axis=1)
    dsink = (dsink_x + dsink_c).reshape(N_KV_HEADS, 8, LANES)[:, :GQA_GROUP, 0].reshape(1, N_HEADS)
    grads_big = {
        "ffn_w_in": jnp.stack([gw_ffn_in0, gw_ffn_in1]),
        "ffn_w_out": jnp.stack([gw_ffn_out0, gw_ffn_out1]),
        "attn_w_qkv": gw_attn_qkv[None],
        "attn_w_o": gw_attn_o[None],
        "ret_w_qkvg": gw_ret_qkvg[None],
        "ret_gn_g": dgn,
        "ret_w_o": gw_ret_o[None],
    }
    grads_small = {
        "norm1_g": jnp.concatenate([dn1_0, dn1_1], axis=0),
        "norm2_g": jnp.concatenate([dn2_0, dn2_1], axis=0),
        "attn_q_norm": jnp.sum(dgains[:ATTN_Q_BLOCKS], axis=0)[None],
        "attn_k_norm": jnp.sum(dgains[ATTN_Q_BLOCKS:ATTN_QK_BLOCKS], axis=0)[None],
        "attn_sink": dsink,
        "ret_decay_logit": d_decay,
    }

    def pad_g(t):
        return jnp.concatenate([t, zg], axis=0)

    d0 = jnp.concatenate([dsh0_0, dsc1_0, dgate2_0, dsh3_0, dsc4_0, dgate5_0], axis=2)[:, 0]
    d1 = jnp.concatenate([dsh0_1, dsc1_1, pad_g(dgate2_1), pad_g(dsh3_1), pad_g(dsc4_1), pad_g(dgate5_1)],
                         axis=2)[:, 0]
    dmod_x = jnp.stack([d0[:B], d1[:B]], axis=1)
    dmod_c = jnp.stack([d0[B], d1[B]], axis=0)
    return loss_tile, dx0[:NX].reshape(B, S, D_MODEL), grads_big, grads_small, dmod_x, dmod_c


SMALL_NAMES = ("c_ctx", "ada_b", "norm1_g", "norm2_g", "attn_q_norm", "attn_k_norm", "attn_sink", "ret_decay_logit")
ADA_ROWS = 64


def _pack_small(d, rows):
    flat = jnp.concatenate([d[k].reshape(-1) for k in SMALL_NAMES])
    n = rows * LANES
    return jnp.pad(flat, (0, n - flat.shape[0])).reshape(rows, LANES)


def _unpack_small(packed, shapes):
    flat = packed.reshape(-1)
    out, off = {}, 0
    for k in SMALL_NAMES:
        n = math.prod(shapes[k])
        out[k] = flat[off:off + n].reshape(shapes[k])
        off += n
    return out


def _gather_big_weights(weights):
    gathered = _gather_shards(
        [weights[k].reshape(1, 1, -1) if k == "ret_gn_g" else weights[k].astype(MXU_DTYPE) for k in BIG_WEIGHTS],
        "gather_weights")
    return {k: _join_shards(k, g) for k, g in zip(BIG_WEIGHTS, gathered)}


def _reduce_and_update_big(g_big, weights, mom1, mom2):
    mx_, my_, mc_ = _my_coords()
    my_chip = 2 * mx_ + my_
    names = list(BIG_WEIGHTS)
    split = []
    for k in names:
        s = _split_shards(k, g_big[k])
        split.append(s.reshape(s.shape[0], 4, 2, s.shape[2], s.shape[3]))
    from_sibling = _exchange_shards(
        split, [(0, 0, 1)], lambda ref, peer, me_: ref.at[:, :, peer[2]],
        lambda s: (s.shape[0], 4) + s.shape[3:], "rs_sibling")
    from_sibling = [t[0] for t in from_sibling]
    core = mc_.astype(jnp.int32).reshape(1)
    pair = [_pair_sum(g, s, core, MXU_DTYPE, "rs_pair_" + k) for k, g, s in zip(names, split, from_sibling)]
    from_chips = _exchange_shards(
        pair, [(1, 0, 0), (0, 1, 0), (1, 1, 0)], lambda ref, peer, me_: ref.at[:, 2 * peer[0] + peer[1]],
        lambda s: (s.shape[0],) + s.shape[2:], "rs_chips")
    big = {}
    for k, g, s, r in zip(names, split, from_sibling, from_chips):
        L_, _, _, a_, b_ = g.shape
        own_keep = lax.dynamic_index_in_dim(lax.dynamic_index_in_dim(g, my_chip, axis=1, keepdims=False), mc_, axis=1,
                                            keepdims=False)
        own_sib = lax.dynamic_index_in_dim(s, my_chip, axis=1, keepdims=False)
        rows = L_ * a_
        res = _adamw(weights[k].reshape(rows, b_), mom1[k].reshape(rows, b_), mom2[k].reshape(rows, b_),
                     [own_keep.reshape(1, rows, b_), own_sib.reshape(1, rows, b_), r.reshape(3, rows, b_)],
                     "adamw_" + k)
        big[k] = [t.reshape(weights[k].shape) for t in res]
    return big


def kernel(x, c, ctx, c_ctx, ada_w, ada_b, norm1_g, norm2_g, ffn_w_in, ffn_w_out, attn_w_qkv, attn_q_norm, attn_k_norm, attn_sink, attn_w_o, ret_w_qkvg, ret_decay_logit, ret_gn_g, ret_w_o, loss_target, m_c_ctx, m_ada_w, m_ada_b, m_norm1_g, m_norm2_g, m_ffn_w_in, m_ffn_w_out, m_attn_w_qkv, m_attn_q_norm, m_attn_k_norm, m_attn_sink, m_attn_w_o, m_ret_w_qkvg, m_ret_decay_logit, m_ret_gn_g, m_ret_w_o, v_c_ctx, v_ada_w, v_ada_b, v_norm1_g, v_norm2_g, v_ffn_w_in, v_ffn_w_out, v_attn_w_qkv, v_attn_q_norm, v_attn_k_norm, v_attn_sink, v_attn_w_o, v_ret_w_qkvg, v_ret_decay_logit, v_ret_gn_g, v_ret_w_o):
    weights = dict(c_ctx=c_ctx, ada_w=ada_w, ada_b=ada_b, norm1_g=norm1_g, norm2_g=norm2_g, ffn_w_in=ffn_w_in,
                   ffn_w_out=ffn_w_out, attn_w_qkv=attn_w_qkv, attn_q_norm=attn_q_norm, attn_k_norm=attn_k_norm,
                   attn_sink=attn_sink, attn_w_o=attn_w_o, ret_w_qkvg=ret_w_qkvg, ret_decay_logit=ret_decay_logit,
                   ret_gn_g=ret_gn_g, ret_w_o=ret_w_o)
    mom1 = dict(c_ctx=m_c_ctx, ada_w=m_ada_w, ada_b=m_ada_b, norm1_g=m_norm1_g, norm2_g=m_norm2_g, ffn_w_in=m_ffn_w_in,
                ffn_w_out=m_ffn_w_out, attn_w_qkv=m_attn_w_qkv, attn_q_norm=m_attn_q_norm, attn_k_norm=m_attn_k_norm,
                attn_sink=m_attn_sink, attn_w_o=m_attn_w_o, ret_w_qkvg=m_ret_w_qkvg, ret_decay_logit=m_ret_decay_logit,
                ret_gn_g=m_ret_gn_g, ret_w_o=m_ret_w_o)
    mom2 = dict(c_ctx=v_c_ctx, ada_w=v_ada_w, ada_b=v_ada_b, norm1_g=v_norm1_g, norm2_g=v_norm2_g, ffn_w_in=v_ffn_w_in,
                ffn_w_out=v_ffn_w_out, attn_w_qkv=v_attn_w_qkv, attn_q_norm=v_attn_q_norm, attn_k_norm=v_attn_k_norm,
                attn_sink=v_attn_sink, attn_w_o=v_attn_w_o, ret_w_qkvg=v_ret_w_qkvg, ret_decay_logit=v_ret_decay_logit,
                ret_gn_g=v_ret_gn_g, ret_w_o=v_ret_w_o)
    B = x.shape[0]
    mx_, my_, mc_ = _my_coords()
    me = 4 * mx_ + 2 * my_ + mc_
    ada_cols = ada_w.shape[2]

    w_full = _gather_big_weights(weights)

    c_all = _all_gather(jax.nn.silu(c), "gather_c").reshape(N_DEV * B, D_MODEL)
    cc_act = jax.nn.silu(c_ctx)[None]
    ada_in = jnp.concatenate([c_all, cc_act, jnp.zeros((ADA_ROWS - N_DEV * B - 1, D_MODEL), F32)], axis=0)
    ada_in = ada_in.astype(MXU_DTYPE)
    ada_w2 = jnp.concatenate([ada_w[0], ada_w[1]], axis=1)
    bias = lax.dynamic_slice_in_dim(ada_b.reshape(2, N_DEV, ada_cols), me, 1, axis=1).reshape(1, 2 * ada_cols)
    mod_cols = _mm(ada_in, ada_w2, "nn", F32, "ada_fwd", bias=bias)
    mod_all = _all_gather(mod_cols, "gather_mod")
    mod_all = mod_all.reshape(N_DEV, ADA_ROWS, 2, ada_cols).transpose(1, 2, 0, 3).reshape(ADA_ROWS, 2, N_DEV * ada_cols)
    mod_x = lax.dynamic_slice_in_dim(mod_all, me * B, B, axis=0)
    mod_c = mod_all[N_DEV * B]

    small = {k: weights[k] for k in SMALL_NAMES}
    loss_tile, grad_x, g_big, g_small, dmod_x, dmod_c = _local_step(x, ctx, loss_target, mod_x, mod_c, w_full, small)
    loss = lax.psum(loss_tile[0, 0], ("x", "y", "c"))

    n_mod = 2 * 6 * D_MODEL
    dm_rows = jnp.concatenate([dmod_x.reshape(B, n_mod), dmod_c.reshape(1, n_mod),
                               jnp.zeros((8 - B - 1, n_mod), F32)], axis=0)
    dm_all = _all_gather(dm_rows, "gather_dmod")
    dmc_tot = _sum_rows(dm_all[:, B:B + 1].reshape(N_DEV, 1, n_mod)[:, :, :].reshape(N_DEV, n_mod // LANES, LANES),
                        "sum_dmod_c").reshape(1, n_mod)
    dmod_rows = jnp.concatenate([dm_all[:, :B].reshape(N_DEV * B, n_mod), dmc_tot,
                                 jnp.zeros((ADA_ROWS - N_DEV * B - 1, n_mod), F32)], axis=0)
    dmod_mine = lax.dynamic_slice_in_dim(dmod_rows.reshape(ADA_ROWS, 2, N_DEV, ada_cols), me, 1, axis=2)
    dmod_mine = dmod_mine.reshape(ADA_ROWS, 2 * ada_cols).astype(MXU_DTYPE)
    g_ada2 = _mm(ada_in, dmod_mine, "tn", F32, "ada_dw")
    g_ada_w = jnp.stack([g_ada2[:, :ada_cols], g_ada2[:, ada_cols:]])
    dmc_mine = jnp.concatenate([dmod_mine[N_DEV * B:N_DEV * B + 1], jnp.zeros((7, 2 * ada_cols), MXU_DTYPE)], axis=0)
    dcc_part = _mm(dmc_mine, ada_w2, "nt", F32, "ada_dc")[0:1]
    g_ada_b = _sum_rows(dmod_rows[:, None, :].reshape(ADA_ROWS, n_mod // LANES, LANES), "sum_dmod_b").reshape(2, 6 * D_MODEL)
    sg = jax.nn.sigmoid(c_ctx)
    g_small["c_ctx"] = dcc_part.reshape(D_MODEL) * (sg * (1.0 + c_ctx * (1.0 - sg)))
    g_small["ada_b"] = g_ada_b * (1.0 / N_DEV)

    shapes = {k: weights[k].shape for k in SMALL_NAMES}
    n_small = sum(math.prod(s) for s in shapes.values())
    srows = -(-(-(-n_small // LANES)) // 8) * 8
    gs_all = _all_gather(_pack_small(g_small, srows), "gather_small_grads")
    sm = _adamw(_pack_small({k: weights[k] for k in SMALL_NAMES}, srows), _pack_small({k: mom1[k] for k in SMALL_NAMES}, srows),
                _pack_small({k: mom2[k] for k in SMALL_NAMES}, srows), [gs_all], "adamw_small")
    sm = [_unpack_small(t, shapes) for t in sm]

    ada_shape = ada_w.shape
    r2 = lambda t: t.reshape(ada_shape[0] * ada_shape[1], ada_shape[2])
    ada = [t.reshape(ada_shape) for t in _adamw(r2(ada_w), r2(m_ada_w), r2(v_ada_w), [r2(g_ada_w)[None]], "adamw_ada")]

    big = _reduce_and_update_big(g_big, weights, mom1, mom2)

    def pick(i, name):
        if name in BIG_WEIGHTS:
            return big[name][i]
        if name == "ada_w":
            return ada[i]
        return sm[i][name]

    order = ("c_ctx", "ada_w", "ada_b", "norm1_g", "norm2_g", "ffn_w_in", "ffn_w_out", "attn_w_qkv", "attn_q_norm",
             "attn_k_norm", "attn_sink", "attn_w_o", "ret_w_qkvg", "ret_decay_logit", "ret_gn_g", "ret_w_o")
    outs = [loss, grad_x]
    for i in range(4):
        outs += [pick(i, n) for n in order]
    return tuple(outs)
```

```python
import functools
import math

import jax
import jax.numpy as jnp
from jax import lax
from jax.experimental import pallas as pl
from jax.experimental.pallas import tpu as pltpu

F32 = jnp.float32
MXU_DTYPE = jnp.bfloat16

D_MODEL = 1024
HEAD_DIM = 64
N_HEADS = 16
N_KV_HEADS = 4
GQA_GROUP = 4
WINDOW = 128
ATTN_BLOCK = 128
RET_HEADS = 4
RET_QK_DIM = 256
RET_V_DIM = 512
RET_VWIDTH = 2048
RET_CHUNK = 256
D_FF = 2816
GRID_W = 64
ROPE_BASE = 10000.0
EPS = 1e-6
NEG_INF = -1e30

ADAM_LR = 0.001
ADAM_B1 = 0.9
ADAM_B2 = 0.999
ADAM_EPS = 1e-08
ADAM_WD = 0.01
ADAM_STEP = 10

N_DEV = 8
LANES = 128
ROW_TILE = 512
VMEM_LIMIT = 48 * 1024 * 1024

MESH = pl.DeviceIdType.MESH
_ANY = pl.BlockSpec(memory_space=pl.ANY)
_SMEM = pl.BlockSpec(memory_space=pltpu.SMEM)


def _params(**kw):
    return pltpu.CompilerParams(vmem_limit_bytes=VMEM_LIMIT, **kw)


def _mx(v):
    return v.astype(MXU_DTYPE)


def _dot(a, b, dims):
    return lax.dot_general(_mx(a), _mx(b), (dims, ((), ())), preferred_element_type=F32)


_NN = ((1,), (0,))
_NT = ((1,), (1,))
_TN = ((0,), (0,))


def _tile(n, cands):
    for c in cands:
        if n % c == 0:
            return c
    return n


def _big_tile(n, cap):
    if n <= cap:
        return n
    for t in range(cap - cap % LANES, 0, -LANES):
        if n % t == 0:
            return t
    return n


MM_ROWS = 1024
MM_COLS = 1408
MM_DEPTH = 2048


def _k_tile(k):
    return _big_tile(k, MM_DEPTH)


def _mm(a, b, mode, out_dtype, name, *, bias=None, res=None, gate=None, gidx_for=None, gate_rows=None):
    if mode == "nn":
        (M, K), (_, N) = a.shape, b.shape
    elif mode == "nt":
        (M, K), (N, _) = a.shape, b.shape
    else:
        (K, M), (_, N) = a.shape, b.shape
    if res is not None:
        tm, tn = gate_rows, _big_tile(N, 512)
        gidx = gidx_for(tm)
    else:
        tm = _big_tile(M, MM_COLS if mode == "tn" else MM_ROWS)
        tn = _big_tile(N, MM_COLS)
    tk = _k_tile(K)
    nk = K // tk
    dims = {"nn": _NN, "nt": _NT, "tn": _TN}[mode]
    a_spec = (pl.BlockSpec((tk, tm), lambda i, j, k: (k, i)) if mode == "tn"
              else pl.BlockSpec((tm, tk), lambda i, j, k: (i, k)))
    b_spec = (pl.BlockSpec((tn, tk), lambda i, j, k: (j, k)) if mode == "nt"
              else pl.BlockSpec((tk, tn), lambda i, j, k: (k, j)))
    o_spec = pl.BlockSpec((tm, tn), lambda i, j, k: (i, j))
    in_specs, operands = [a_spec, b_spec], [a, b]
    if bias is not None:
        in_specs.append(pl.BlockSpec((1, tn), lambda i, j, k: (0, j)))
        operands.append(bias)
    if res is not None:
        in_specs += [o_spec, pl.BlockSpec((1, 1, tn), lambda i, j, k: (gidx(i), 0, j))]
        operands += [res, gate]
        out_shape = (jax.ShapeDtypeStruct((M, N), F32), jax.ShapeDtypeStruct((M, N), F32))
        out_specs = (o_spec, o_spec)
    else:
        out_shape = jax.ShapeDtypeStruct((M, N), out_dtype)
        out_specs = o_spec

    def body(*refs):
        a_ref, b_ref = refs[0], refs[1]
        extra = refs[2:len(operands)]
        outs = refs[len(operands):]
        prod = _dot(a_ref[...], b_ref[...], dims)

        def finish(acc):
            if bias is not None:
                outs[0][...] = (acc + extra[0][...]).astype(out_dtype)
            elif res is not None:
                outs[0][...] = acc
                outs[1][...] = extra[0][...] + extra[1][0] * acc
            else:
                outs[0][...] = acc.astype(out_dtype)

        if nk == 1:
            finish(prod)
        else:
            acc_ref = outs[-1]
            outs = outs[:-1]
            k = pl.program_id(2)

            @pl.when(k == 0)
            def _():
                acc_ref[...] = prod

            @pl.when(k > 0)
            def _():
                acc_ref[...] += prod

            @pl.when(k == nk - 1)
            def _():
                finish(acc_ref[...])

    return pl.pallas_call(
        body, name=name, grid=(M // tm, N // tn, nk), in_specs=in_specs, out_specs=out_specs, out_shape=out_shape,
        scratch_shapes=[pltpu.VMEM((tm, tn), F32)] if nk > 1 else [],
        compiler_params=_params())(*operands)


def _group_index(n_x_tiles, tiles_per_example, n_examples):
    def gidx(i):
        return jnp.where(i < n_x_tiles, i // tiles_per_example, n_examples)
    return gidx


def _norm_mod_fwd(x, g, shift, scale, gidx, name):
    T, Dm = x.shape

    def body(x_ref, g_ref, sh_ref, sc_ref, h_ref):
        xv = x_ref[...]
        r = lax.rsqrt(jnp.mean(xv * xv, axis=-1, keepdims=True) + EPS)
        y = xv * r * g_ref[...]
        h_ref[...] = (y * (1.0 + sc_ref[0]) + sh_ref[0]).astype(h_ref.dtype)

    row = pl.BlockSpec((ROW_TILE, Dm), lambda i: (i, 0))
    mod = pl.BlockSpec((1, 1, Dm), lambda i: (gidx(i), 0, 0))
    return pl.pallas_call(
        body, name=name, grid=(T // ROW_TILE,),
        in_specs=[row, pl.BlockSpec((1, Dm), lambda i: (0, 0)), mod, mod],
        out_specs=row, out_shape=jax.ShapeDtypeStruct((T, Dm), MXU_DTYPE),
        compiler_params=_params())(x, g, shift, scale)


def _first_of_group(i, gidx):
    return jnp.logical_or(i == 0, gidx(i) != gidx(jnp.maximum(i - 1, 0)))


def _norm_mod_bwd(dh, x, g, scale, dres, gidx, n_groups, name):
    T, Dm = x.shape

    def body(dh_ref, x_ref, g_ref, sc_ref, dres_ref, dx_ref, dsh_ref, dsc_ref, dg_ref):
        i = pl.program_id(0)
        xv, dhv = x_ref[...], dh_ref[...]
        r = lax.rsqrt(jnp.mean(xv * xv, axis=-1, keepdims=True) + EPS)
        xn = xv * r
        y = xn * g_ref[...]

        @pl.when(_first_of_group(i, gidx))
        def _():
            dsh_ref[...] = jnp.zeros_like(dsh_ref)
            dsc_ref[...] = jnp.zeros_like(dsc_ref)

        @pl.when(i == 0)
        def _():
            dg_ref[...] = jnp.zeros_like(dg_ref)

        dsh_ref[0] += jnp.sum(dhv, axis=0, keepdims=True)
        dsc_ref[0] += jnp.sum(dhv * y, axis=0, keepdims=True)
        dy = dhv * (1.0 + sc_ref[0])
        dg_ref[...] += jnp.sum(dy * xn, axis=0, keepdims=True)
        dxn = dy * g_ref[...]
        dx = r * (dxn - xn * jnp.mean(dxn * xn, axis=-1, keepdims=True))
        dx_ref[...] = dres_ref[...] + dx

    row = pl.BlockSpec((ROW_TILE, Dm), lambda i: (i, 0))
    mod = pl.BlockSpec((1, 1, Dm), lambda i: (gidx(i), 0, 0))
    vec = pl.BlockSpec((1, Dm), lambda i: (0, 0))
    return pl.pallas_call(
        body, name=name, grid=(T // ROW_TILE,),
        in_specs=[row, row, vec, mod, row],
        out_specs=(row, mod, mod, vec),
        out_shape=(jax.ShapeDtypeStruct((T, Dm), F32), jax.ShapeDtypeStruct((n_groups, 1, Dm), F32),
                   jax.ShapeDtypeStruct((n_groups, 1, Dm), F32), jax.ShapeDtypeStruct((1, Dm), F32)),
        compiler_params=_params())(dh, x, g, scale, dres)


def _gate_bwd(dy, f, gate, gidx, n_groups, name):
    T, Dm = dy.shape

    def body(dy_ref, f_ref, gate_ref, dz_ref, dgate_ref):
        i = pl.program_id(0)
        dyv = dy_ref[...]

        @pl.when(_first_of_group(i, gidx))
        def _():
            dgate_ref[...] = jnp.zeros_like(dgate_ref)

        dgate_ref[0] += jnp.sum(dyv * f_ref[...], axis=0, keepdims=True)
        dz_ref[...] = (dyv * gate_ref[0]).astype(dz_ref.dtype)

    row = pl.BlockSpec((ROW_TILE, Dm), lambda i: (i, 0))
    mod = pl.BlockSpec((1, 1, Dm), lambda i: (gidx(i), 0, 0))
    return pl.pallas_call(
        body, name=name, grid=(T // ROW_TILE,), in_specs=[row, row, mod], out_specs=(row, mod),
        out_shape=(jax.ShapeDtypeStruct((T, Dm), MXU_DTYPE), jax.ShapeDtypeStruct((n_groups, 1, Dm), F32)),
        compiler_params=_params())(dy, f, gate)


SWIGLU_ROWS = 256


def _swiglu_fwd(u, name):
    T = u.shape[0]

    def body(u_ref, a_ref):
        gate, up = u_ref[:, :D_FF], u_ref[:, D_FF:]
        a_ref[...] = (gate * jax.nn.sigmoid(gate) * up).astype(a_ref.dtype)

    return pl.pallas_call(
        body, name=name, grid=(T // SWIGLU_ROWS,),
        in_specs=[pl.BlockSpec((SWIGLU_ROWS, 2 * D_FF), lambda i: (i, 0))],
        out_specs=pl.BlockSpec((SWIGLU_ROWS, D_FF), lambda i: (i, 0)),
        out_shape=jax.ShapeDtypeStruct((T, D_FF), MXU_DTYPE), compiler_params=_params())(u)


def _swiglu_bwd(da, u, name):
    T = u.shape[0]

    def body(da_ref, u_ref, du_ref):
        gate, up, dav = u_ref[:, :D_FF], u_ref[:, D_FF:], da_ref[...]
        sg = jax.nn.sigmoid(gate)
        du_ref[:, :D_FF] = (dav * up * (sg * (1.0 + gate * (1.0 - sg)))).astype(du_ref.dtype)
        du_ref[:, D_FF:] = (dav * gate * sg).astype(du_ref.dtype)

    return pl.pallas_call(
        body, name=name, grid=(T // SWIGLU_ROWS,),
        in_specs=[pl.BlockSpec((SWIGLU_ROWS, D_FF), lambda i: (i, 0)),
                  pl.BlockSpec((SWIGLU_ROWS, 2 * D_FF), lambda i: (i, 0))],
        out_specs=pl.BlockSpec((SWIGLU_ROWS, 2 * D_FF), lambda i: (i, 0)),
        out_shape=jax.ShapeDtypeStruct((T, 2 * D_FF), MXU_DTYPE), compiler_params=_params())(da, u)


def _loss_fwd_bwd(y, target, name):
    T, Dm = y.shape

    def body(y_ref, t_ref, loss_ref, dy_ref):
        err = y_ref[...] - t_ref[...]

        @pl.when(pl.program_id(0) == 0)
        def _():
            loss_ref[...] = jnp.zeros_like(loss_ref)

        loss_ref[...] += 0.5 * jnp.sum(jnp.mean(err * err, axis=-1, keepdims=True))
        dy_ref[...] = err * (1.0 / Dm)

    row = pl.BlockSpec((ROW_TILE, Dm), lambda i: (i, 0))
    return pl.pallas_call(
        body, name=name, grid=(T // ROW_TILE,), in_specs=[row, row],
        out_specs=(pl.BlockSpec((8, LANES), lambda i: (0, 0)), row),
        out_shape=(jax.ShapeDtypeStruct((8, LANES), F32), jax.ShapeDtypeStruct((T, Dm), F32)),
        compiler_params=_params())(y, target)


def _rope_tables(seq, head_dim):
    axis_dim = head_dim // 2
    half = axis_dim // 2
    pos = jnp.arange(seq, dtype=jnp.int32)
    row = (pos // GRID_W).astype(F32)[:, None]
    col = (pos % GRID_W).astype(F32)[:, None]
    inv = ROPE_BASE ** (-jnp.arange(0, axis_dim, 2, dtype=F32) / axis_dim)
    lane = jnp.arange(head_dim, dtype=jnp.int32)
    within = lane % axis_dim
    ang = jnp.where((lane // axis_dim == 0)[None, :], row, col) * inv[within % half][None, :]
    cos = jnp.cos(ang)
    sin = jnp.where((within < half)[None, :], -jnp.sin(ang), jnp.sin(ang))
    cos = jnp.concatenate([cos, jnp.ones((ROW_TILE, head_dim), F32)], axis=0)
    sin = jnp.concatenate([sin, jnp.zeros((ROW_TILE, head_dim), F32)], axis=0)
    return cos, sin


def _pair_swap(v, half):
    if 2 * half == LANES:
        return pltpu.roll(v, half, axis=1)
    lane = lax.broadcasted_iota(jnp.int32, v.shape, 1)
    return jnp.where((lane % (2 * half)) < half, pltpu.roll(v, LANES - half, axis=1), pltpu.roll(v, half, axis=1))


def _head_sum(v, ones_ref):
    hi = v.astype(MXU_DTYPE)
    lo = (v - hi.astype(F32)).astype(MXU_DTYPE)
    return (jnp.dot(hi, ones_ref[...], preferred_element_type=F32)
            + jnp.dot(lo, ones_ref[...], preferred_element_type=F32))


def _head_ones():
    lane = jnp.arange(LANES)
    return (lane[:, None] // HEAD_DIM == lane[None, :] // HEAD_DIM).astype(MXU_DTYPE)


ATTN_QK_BLOCKS = (N_HEADS + N_KV_HEADS) * HEAD_DIM // LANES
ATTN_ALL_BLOCKS = (N_HEADS + 2 * N_KV_HEADS) * HEAD_DIM // LANES
ATTN_Q_BLOCKS = N_HEADS * HEAD_DIM // LANES
ATTN_SCALE = HEAD_DIM ** -0.5


def _attn_prep_fwd(qkv, gains, cos, sin, tidx, name):
    T, W = qkv.shape

    def body(x_ref, g_ref, cos_ref, sin_ref, ones_ref, o_ref):
        for cb in range(ATTN_ALL_BLOCKS):
            cols = slice(cb * LANES, (cb + 1) * LANES)
            xv = x_ref[:, cols]
            if cb < ATTN_QK_BLOCKS:
                r = lax.rsqrt(_head_sum(xv * xv, ones_ref) * (1.0 / HEAD_DIM) + EPS)
                y = xv * r * g_ref[0 if cb < ATTN_Q_BLOCKS else 1]
                xv = y * cos_ref[...] + _pair_swap(y, HEAD_DIM // 4) * sin_ref[...]
                if cb < ATTN_Q_BLOCKS:
                    xv = xv * ATTN_SCALE
            o_ref[:, cols] = xv.astype(o_ref.dtype)

    row = pl.BlockSpec((ROW_TILE, W), lambda i: (i, 0))
    tab = pl.BlockSpec((ROW_TILE, LANES), lambda i: (tidx(i), 0))
    return pl.pallas_call(
        body, name=name, grid=(T // ROW_TILE,),
        in_specs=[row, pl.BlockSpec((2, 1, LANES), lambda i: (0, 0, 0)), tab, tab,
                  pl.BlockSpec((LANES, LANES), lambda i: (0, 0))],
        out_specs=row, out_shape=jax.ShapeDtypeStruct(qkv.shape, MXU_DTYPE),
        compiler_params=_params())(qkv, gains, cos, sin, _head_ones())


def _attn_prep_bwd(dqk, dv, qkv, gains, cos, sin, tidx, name):
    T, W = qkv.shape
    qk_w = ATTN_QK_BLOCKS * LANES

    def body(dqk_ref, dv_ref, x_ref, g_ref, cos_ref, sin_ref, ones_ref, o_ref, dg_ref):
        @pl.when(pl.program_id(0) == 0)
        def _():
            dg_ref[...] = jnp.zeros_like(dg_ref)

        for cb in range(ATTN_QK_BLOCKS):
            cols = slice(cb * LANES, (cb + 1) * LANES)
            xv, d = x_ref[:, cols], dqk_ref[:, cols]
            if cb < ATTN_Q_BLOCKS:
                d = d * ATTN_SCALE
            r = lax.rsqrt(_head_sum(xv * xv, ones_ref) * (1.0 / HEAD_DIM) + EPS)
            xn = xv * r
            dy = d * cos_ref[...] + _pair_swap(d * sin_ref[...], HEAD_DIM // 4)
            dg_ref[:, cols] += jnp.sum(dy * xn, axis=0, keepdims=True)
            dxn = dy * g_ref[0 if cb < ATTN_Q_BLOCKS else 1]
            dx = r * (dxn - xn * (_head_sum(dxn * xn, ones_ref) * (1.0 / HEAD_DIM)))
            o_ref[:, cols] = dx.astype(o_ref.dtype)
        o_ref[:, qk_w:] = dv_ref[...].astype(o_ref.dtype)

    row = lambda w: pl.BlockSpec((ROW_TILE, w), lambda i: (i, 0))
    tab = pl.BlockSpec((ROW_TILE, LANES), lambda i: (tidx(i), 0))
    return pl.pallas_call(
        body, name=name, grid=(T // ROW_TILE,),
        in_specs=[row(qk_w), row(W - qk_w), row(W), pl.BlockSpec((2, 1, LANES), lambda i: (0, 0, 0)), tab, tab,
                  pl.BlockSpec((LANES, LANES), lambda i: (0, 0))],
        out_specs=(row(W), pl.BlockSpec((1, qk_w), lambda i: (0, 0))),
        out_shape=(jax.ShapeDtypeStruct(qkv.shape, MXU_DTYPE), jax.ShapeDtypeStruct((1, qk_w), F32)),
        compiler_params=_params())(dqk, dv, qkv, gains, cos, sin, _head_ones())


RET_QK_BLOCKS = 2 * RET_HEADS * RET_QK_DIM // LANES


def _ret_rope(x, cos, sin, tidx, name):
    T = x.shape[0]
    W = RET_QK_BLOCKS * LANES
    k_scale = RET_QK_DIM ** -0.5

    def body(x_ref, cos_ref, sin_ref, o_ref):
        for cb in range(RET_QK_BLOCKS):
            cols = slice(cb * LANES, (cb + 1) * LANES)
            tcols = slice((cb % 2) * LANES, (cb % 2 + 1) * LANES)
            xv = x_ref[:, cols]
            out = xv * cos_ref[:, tcols] + pltpu.roll(xv, LANES // 2, axis=1) * sin_ref[:, tcols]
            if cb >= RET_QK_BLOCKS // 2:
                out = out * k_scale
            o_ref[:, cols] = out

    row = pl.BlockSpec((ROW_TILE, W), lambda i: (i, 0))
    tab = pl.BlockSpec((ROW_TILE, RET_QK_DIM), lambda i: (tidx(i), 0))
    return pl.pallas_call(
        body, name=name, grid=(T // ROW_TILE,), in_specs=[row, tab, tab], out_specs=row,
        out_shape=jax.ShapeDtypeStruct((T, W), F32), compiler_params=_params())(x, cos, sin)


ASSEMBLE_ROWS = 256


def _ret_grad_assemble(x_parts, c_parts, dg, cos, sin, seq, name):
    NX, NC = x_parts[0].shape[0], c_parts[0].shape[0]
    T = NX + NC
    rt = ASSEMBLE_ROWS
    nxt = NX // rt
    qk_w = RET_HEADS * RET_QK_DIM
    k_scale = RET_QK_DIM ** -0.5

    def unrotate(d, cos_ref, sin_ref, scale):
        outs = []
        for cb in range(qk_w // LANES):
            cols = slice(cb * LANES, (cb + 1) * LANES)
            tcols = slice((cb % 2) * LANES, (cb % 2 + 1) * LANES)
            dv_ = d[:, cols]
            o = dv_ * cos_ref[:, tcols] + pltpu.roll(dv_ * sin_ref[:, tcols], LANES // 2, axis=1)
            outs.append(o * scale if scale != 1.0 else o)
        return outs

    def body(dqf, dqb, dkf, dkb, dvf, dvb, dg_ref, dkcf, dkcb, dvcf, dvcb, cos_ref, sin_ref, o_ref):
        i = pl.program_id(0)

        def write_k(parts):
            for cb, o in enumerate(parts):
                o_ref[:, qk_w + cb * LANES:qk_w + (cb + 1) * LANES] = o.astype(o_ref.dtype)

        @pl.when(i < nxt)
        def _():
            for cb, o in enumerate(unrotate(dqf[...] + dqb[...], cos_ref, sin_ref, 1.0)):
                o_ref[:, cb * LANES:(cb + 1) * LANES] = o.astype(o_ref.dtype)
            write_k(unrotate(dkf[...] + dkb[...], cos_ref, sin_ref, k_scale))
            o_ref[:, 2 * qk_w:2 * qk_w + RET_VWIDTH] = (dvf[...] + dvb[...]).astype(o_ref.dtype)
            o_ref[:, 2 * qk_w + RET_VWIDTH:] = dg_ref[...].astype(o_ref.dtype)

        @pl.when(i >= nxt)
        def _():
            o_ref[:, :qk_w] = jnp.zeros((rt, qk_w), o_ref.dtype)
            write_k(unrotate(dkcf[...] + dkcb[...], cos_ref, sin_ref, k_scale))
            o_ref[:, 2 * qk_w:2 * qk_w + RET_VWIDTH] = (dvcf[...] + dvcb[...]).astype(o_ref.dtype)
            o_ref[:, 2 * qk_w + RET_VWIDTH:] = jnp.zeros((rt, RET_VWIDTH), o_ref.dtype)

    xs = lambda w: pl.BlockSpec((rt, w), lambda i: (jnp.minimum(i, nxt - 1), 0))
    cs = lambda w: pl.BlockSpec((rt, w), lambda i: (jnp.maximum(i - nxt, 0), 0))
    tab = pl.BlockSpec((rt, RET_QK_DIM), lambda i: (jnp.where(i < nxt, i % (seq // rt), seq // rt), 0))
    return pl.pallas_call(
        body, name=name, grid=(T // rt,),
        in_specs=[xs(qk_w)] * 4 + [xs(RET_VWIDTH)] * 3 + [cs(qk_w)] * 2 + [cs(RET_VWIDTH)] * 2 + [tab, tab],
        out_specs=pl.BlockSpec((rt, 2 * qk_w + 2 * RET_VWIDTH), lambda i: (i, 0)),
        out_shape=jax.ShapeDtypeStruct((T, 2 * qk_w + 2 * RET_VWIDTH), MXU_DTYPE),
        compiler_params=_params())(*x_parts, dg, *c_parts, cos, sin)


def _band_bias(qb, seq):
    nb = seq // qb
    assert nb >= 2
    i = jnp.arange(GQA_GROUP * qb, dtype=jnp.int32)[:, None] % qb
    n = jnp.arange(3 * qb, dtype=jnp.int32)[None, :]
    in_window = (n >= i) & (n - i <= 2 * WINDOW)
    variants = [in_window & (n >= qb), in_window, in_window & (n < 2 * qb)]
    return jnp.stack([jnp.where(v, 0.0, NEG_INF).astype(F32) for v in variants])


GROUP_ORDER = (0, 2, 1, 3)


def _stack_halves(blk):
    return jnp.concatenate([blk[:, :LANES], blk[:, LANES:]], axis=0)


def _unstack_halves(v, rows):
    return jnp.concatenate([v[:rows], v[rows:]], axis=1)


def _align_head(pair, odd):
    lane = lax.broadcasted_iota(jnp.int32, pair.shape, 1)
    mine = jnp.where((lane >= HEAD_DIM) == odd, pair, jnp.zeros_like(pair))
    rolled = pltpu.roll(mine, HEAD_DIM, axis=1)
    return jnp.where(odd, rolled, mine), jnp.where(odd, mine, rolled)


def _scores(q2, x_eo):
    return jnp.concatenate([_dot(q2, x_eo[0], _NT), _dot(q2, x_eo[1], _NT)], axis=0)


def _apply(p, x_eo):
    half = p.shape[0] // 2
    return _dot(p[:half], x_eo[0], _NN) + _dot(p[half:], x_eo[1], _NN)


def _kv_grad(a, q2, odd):
    half = a.shape[0] // 2
    even_part = _dot(a[:half], q2, _TN)
    odd_part = _dot(a[half:], q2, _TN)
    lane = lax.broadcasted_iota(jnp.int32, even_part.shape, 1)
    low = (jnp.where(lane < HEAD_DIM, even_part, 0.0)
           + pltpu.roll(jnp.where(lane >= HEAD_DIM, odd_part, 0.0), HEAD_DIM, axis=1))
    return jnp.where(odd, pltpu.roll(low, HEAD_DIM, axis=1), low)


def _attn_probs(q2, kc_eo, kl_eo, bias, sink_ref, kv_head, qb):
    rows = GQA_GROUP * qb
    s_c = _scores(q2, kc_eo)
    blk = lax.broadcasted_iota(jnp.int32, (rows, 1), 0) // qb
    sink = jnp.zeros((rows, 1), F32)
    for t, gi in enumerate(GROUP_ORDER):
        sink = jnp.where(blk == t, sink_ref[kv_head, gi], sink)
    m = jnp.maximum(jnp.max(s_c, axis=-1, keepdims=True), sink)
    s_l = None
    if kl_eo is not None:
        s_l = _scores(q2, kl_eo) + bias
        m = jnp.maximum(m, jnp.max(s_l, axis=-1, keepdims=True))
    e_c = jnp.exp(s_c - m)
    e_s = jnp.exp(sink - m)
    den = jnp.sum(e_c, axis=-1, keepdims=True) + e_s
    e_l = None
    if kl_eo is not None:
        e_l = jnp.exp(s_l - m)
        den = den + jnp.sum(e_l, axis=-1, keepdims=True)
    inv = 1.0 / den
    return e_c * inv, (None if e_l is None else e_l * inv), e_s * inv


GROUP_W = GQA_GROUP * HEAD_DIM
K_LANE_BLOCK = N_HEADS * HEAD_DIM // LANES
V_LANE_BLOCK = K_LANE_BLOCK + N_KV_HEADS * HEAD_DIM // LANES


def _attn_specs(B, seq, ctx_len, ctx_queries):
    ctx0 = B * seq // ctx_len
    if ctx_queries:
        qb, nb = ctx_len, 1
        qrow = lambda b, j: ctx0 + b
    else:
        qb, nb = ATTN_BLOCK, seq // ATTN_BLOCK
        qrow = lambda b, j: b * nb + j
    q_spec = pl.BlockSpec((qb, GROUP_W), lambda b, k, j: (qrow(b, j), k))
    c_specs = [pl.BlockSpec((ctx_len, LANES), lambda b, k, j: (ctx0 + b, K_LANE_BLOCK + k // 2)),
               pl.BlockSpec((ctx_len, LANES), lambda b, k, j: (ctx0 + b, V_LANE_BLOCK + k // 2))]
    local = []
    if not ctx_queries:
        near = [lambda j: jnp.maximum(j - 1, 0), lambda j: j, lambda j: jnp.minimum(j + 1, nb - 1)]
        for lane0 in (K_LANE_BLOCK, V_LANE_BLOCK):
            for f in near:
                local.append(pl.BlockSpec((qb, LANES), lambda b, k, j, f=f, lane0=lane0: (b * nb + f(j), lane0 + k // 2)))
        local.append(pl.BlockSpec(
            (1, GQA_GROUP * qb, 3 * qb), lambda b, k, j: (jnp.where(j == 0, 0, jnp.where(j == nb - 1, 2, 1)), 0, 0)))
    return qb, nb, qrow, q_spec, c_specs, local


def _attn_operands(refs, has_local, kv_head):
    odd = (kv_head % 2) == 1
    n_local = 7 if has_local else 0
    q2 = _stack_halves(refs[0][...])
    kc = _align_head(refs[1 + n_local][...], odd)
    vc = _align_head(refs[2 + n_local][...], odd)
    kl = vl = bias = None
    if has_local:
        kl = _align_head(jnp.concatenate([r[...] for r in refs[1:4]], axis=0), odd)
        vl = _align_head(jnp.concatenate([r[...] for r in refs[4:7]], axis=0), odd)
        bias = refs[7][0]
    return odd, q2, kc, vc, kl, vl, bias


def _attn_fwd(qkv, sink, B, seq, ctx_len, ctx_queries, name):
    has_local = not ctx_queries
    qb, nb, _, q_spec, c_specs, local = _attn_specs(B, seq, ctx_len, ctx_queries)
    n_rows = B * (ctx_len if ctx_queries else seq)

    def body(*refs):
        sink_ref, o_ref = refs[-2:]
        kv_head = pl.program_id(1)
        _, q2, kc, vc, kl, vl, bias = _attn_operands(refs, has_local, kv_head)
        p_c, p_l, _ = _attn_probs(q2, kc, kl, bias, sink_ref, kv_head, qb)
        o2 = _apply(p_c, vc)
        if has_local:
            o2 = o2 + _apply(p_l, vl)
        o_ref[...] = _unstack_halves(o2, qb).astype(o_ref.dtype)

    operands = [qkv] + ([qkv] * 6 + [_band_bias(qb, seq)] if has_local else []) + [qkv, qkv, sink]
    return pl.pallas_call(
        body, name=name, grid=(B, N_KV_HEADS, nb),
        in_specs=[q_spec] + local + c_specs + [_SMEM],
        out_specs=pl.BlockSpec((qb, GROUP_W), lambda b, k, j: (b * nb + j, k)),
        out_shape=jax.ShapeDtypeStruct((n_rows, N_HEADS * HEAD_DIM), MXU_DTYPE), compiler_params=_params())(*operands)


def _attn_bwd(qkv, sink, do, B, seq, ctx_len, ctx_queries, name):
    has_local = not ctx_queries
    qb, nb, qrow, q_spec, c_specs, local = _attn_specs(B, seq, ctx_len, ctx_queries)
    n_rows = B * (ctx_len if ctx_queries else seq)

    def body(*refs):
        n_in = 1 + (7 if has_local else 0) + 4
        sink_ref, do_ref = refs[n_in - 2:n_in]
        outs = refs[n_in:]
        dq_ref = outs[0]
        dkc_ref, dvc_ref, dsink_ref = outs[-3:]
        b, kv_head, j = pl.program_id(0), pl.program_id(1), pl.program_id(2)
        odd, q2, kc, vc, kl, vl, bias = _attn_operands(refs, has_local, kv_head)
        do2 = _stack_halves(do_ref[...])
        p_c, p_l, p_s = _attn_probs(q2, kc, kl, bias, sink_ref, kv_head, qb)
        dp_c = _scores(do2, vc)
        delta = jnp.sum(p_c * dp_c, axis=-1, keepdims=True)
        if has_local:
            dp_l = _scores(do2, vl)
            delta = delta + jnp.sum(p_l * dp_l, axis=-1, keepdims=True)
        ds_c = p_c * (dp_c - delta)
        dq2 = _apply(ds_c, kc)

        @pl.when((kv_head % 2 == 0) & (j == 0))
        def _():
            dkc_ref[...] = jnp.zeros_like(dkc_ref)
            dvc_ref[...] = jnp.zeros_like(dvc_ref)
            if has_local:
                outs[1][...] = jnp.zeros_like(outs[1])
                outs[2][...] = jnp.zeros_like(outs[2])

        @pl.when((b == 0) & (kv_head == 0) & (j == 0))
        def _():
            dsink_ref[...] = jnp.zeros_like(dsink_ref)

        dkc_ref[...] += _kv_grad(ds_c, q2, odd)
        dvc_ref[...] += _kv_grad(p_c, do2, odd)
        if has_local:
            ds_l = p_l * (dp_l - delta)
            dq2 = dq2 + _apply(ds_l, kl)
            dkl = _kv_grad(ds_l, q2, odd)
            dvl = _kv_grad(p_l, do2, odd)
            dk_ref, dv_ref = outs[1], outs[2]
            for t in range(3):
                def add(t=t):
                    start = pl.multiple_of((j - 1 + t) * qb, qb)
                    dk_ref[pl.ds(start, qb), :] += dkl[t * qb:(t + 1) * qb]
                    dv_ref[pl.ds(start, qb), :] += dvl[t * qb:(t + 1) * qb]
                if t == 0:
                    pl.when(j > 0)(add)
                elif t == 2:
                    pl.when(j < nb - 1)(add)
                else:
                    add()
        dq_ref[...] = _unstack_halves(dq2, qb)
        dsk = -(p_s * delta)
        sub = lax.broadcasted_iota(jnp.int32, (8, LANES), 0)
        tile = jnp.zeros((8, LANES), F32)
        for t, gi in enumerate(GROUP_ORDER):
            tile = jnp.where(sub == gi, jnp.sum(dsk[t * qb:(t + 1) * qb]), tile)
        dsink_ref[pl.ds(pl.multiple_of(kv_head * 8, 8), 8), :] += tile

    kv_w = N_KV_HEADS * HEAD_DIM
    seq_spec = pl.BlockSpec((seq, LANES), lambda b, k, j: (b, k // 2))
    ctx_spec = pl.BlockSpec((ctx_len, LANES), lambda b, k, j: (b, k // 2))
    do_spec = pl.BlockSpec((qb, GROUP_W), lambda b, k, j: (qrow(b, j), k))
    operands = [qkv] + ([qkv] * 6 + [_band_bias(qb, seq)] if has_local else []) + [qkv, qkv, sink, do]
    out_specs = ([pl.BlockSpec((qb, GROUP_W), lambda b, k, j: (b * nb + j, k))] + ([seq_spec, seq_spec] if has_local else [])
                 + [ctx_spec, ctx_spec, pl.BlockSpec((32, LANES), lambda b, k, j: (0, 0))])
    out_shape = ([jax.ShapeDtypeStruct((n_rows, N_HEADS * HEAD_DIM), F32)]
                 + ([jax.ShapeDtypeStruct((B * seq, kv_w), F32)] * 2 if has_local else [])
                 + [jax.ShapeDtypeStruct((B * ctx_len, kv_w), F32)] * 2 + [jax.ShapeDtypeStruct((32, LANES), F32)])
    return pl.pallas_call(
        body, name=name, grid=(B, N_KV_HEADS, nb),
        in_specs=[q_spec] + local + c_specs + [_SMEM, do_spec],
        out_specs=tuple(out_specs), out_shape=tuple(out_shape), compiler_params=_params())(*operands)


def _ret_decays(lg, rev):
    n = lax.broadcasted_iota(jnp.int32, (RET_CHUNK, RET_CHUNK), 0).astype(F32)
    m = lax.broadcasted_iota(jnp.int32, (RET_CHUNK, RET_CHUNK), 1).astype(F32)
    pos = lax.broadcasted_iota(jnp.int32, (RET_CHUNK, 1), 0).astype(F32)
    diff = (m - n) if rev else (n - m)
    a_exp = jnp.maximum(diff, 0.0)
    intra = jnp.where(diff >= 0, jnp.exp(lg * a_exp), 0.0)
    q_exp = (RET_CHUNK - pos) if rev else (pos + 1.0)
    k_exp = pos if rev else (RET_CHUNK - 1.0 - pos)
    chunk = jnp.exp(jnp.full((1, 1), RET_CHUNK, F32) * lg)
    return intra, a_exp, jnp.exp(lg * q_exp), q_exp, jnp.exp(lg * k_exp), k_exp, chunk


def _ctx_decay(lg, ctx_len, rev):
    t = lax.broadcasted_iota(jnp.int32, (ctx_len, 1), 0).astype(F32)
    expo = t if rev else (ctx_len - 1.0 - t)
    return jnp.exp(lg * expo), expo


def _ret_specs(B, seq, ctx_len, order):
    nc = seq // RET_CHUNK
    x_blocks = B * seq // ctx_len

    def rows(b, c):
        return b * nc + order(c, nc)

    q_spec = pl.BlockSpec((RET_CHUNK, RET_QK_DIM), lambda b, h, c: (rows(b, c), h))
    k_spec = pl.BlockSpec((RET_CHUNK, RET_QK_DIM), lambda b, h, c: (rows(b, c), RET_HEADS + h))
    v_spec = pl.BlockSpec((RET_CHUNK, RET_V_DIM), lambda b, h, c: (rows(b, c), RET_HEADS + h))
    kc_spec = pl.BlockSpec((ctx_len, RET_QK_DIM), lambda b, h, c: (x_blocks + b, RET_HEADS + h))
    vc_spec = pl.BlockSpec((ctx_len, RET_V_DIM), lambda b, h, c: (x_blocks + b, RET_HEADS + h))
    st_spec = pl.BlockSpec((1, 1, 1, RET_QK_DIM, RET_V_DIM), lambda b, h, c: (b, h, order(c, nc), 0, 0))
    o_spec = pl.BlockSpec((RET_CHUNK, RET_V_DIM), lambda b, h, c: (rows(b, c), h))
    return nc, q_spec, k_spec, v_spec, kc_spec, vc_spec, st_spec, o_spec


_SCAN_UP = lambda c, nc: c
_SCAN_DOWN = lambda c, nc: nc - 1 - c


def _ret_fwd(qk, qkvg, log_g, B, seq, ctx_len, name):
    nc, qf, kf, vf, kc_spec, vc_spec, stf, of = _ret_specs(B, seq, ctx_len, _SCAN_UP)
    _, qr, kr, vr, _, _, str_, or_ = _ret_specs(B, seq, ctx_len, _SCAN_DOWN)

    def body(lg_ref, qf_ref, kf_ref, vf_ref, qr_ref, kr_ref, vr_ref, kc_ref, vc_ref,
             of_ref, stf_ref, or_ref, str_ref, state_f, state_r):
        h, c = pl.program_id(1), pl.program_id(2)
        dirs = ((False, lg_ref[0, h], qf_ref, kf_ref, vf_ref, of_ref, stf_ref, state_f),
                (True, lg_ref[1, h], qr_ref, kr_ref, vr_ref, or_ref, str_ref, state_r))

        @pl.when(c == 0)
        def _():
            for rev, lg, _, _, _, _, _, state in dirs:
                dec, _ = _ctx_decay(lg, ctx_len, rev)
                state[...] = _dot(kc_ref[...] * dec, vc_ref[...], _TN)

        for rev, lg, q_ref, k_ref, v_ref, o_ref, st_ref, state in dirs:
            intra, _, q_dec, _, k_dec, _, chunk_dec = _ret_decays(lg, rev)
            qv, kv, vv = q_ref[...], k_ref[...], v_ref[...]
            s_in = state[...]
            st_ref[0, 0, 0] = s_in
            w = _dot(qv, kv, _NT) * intra
            o_ref[...] = _dot(w, vv, _NN) + _dot(qv, s_in, _NN) * q_dec
            state[...] = s_in * chunk_dec + _dot(kv * k_dec, vv, _TN)

    o_shape = jax.ShapeDtypeStruct((B * seq, RET_VWIDTH), F32)
    st_shape = jax.ShapeDtypeStruct((B, RET_HEADS, nc, RET_QK_DIM, RET_V_DIM), F32)
    return pl.pallas_call(
        body, name=name, grid=(B, RET_HEADS, nc),
        in_specs=[_SMEM, qf, kf, vf, qr, kr, vr, kc_spec, vc_spec],
        out_specs=(of, stf, or_, str_), out_shape=(o_shape, st_shape, o_shape, st_shape),
        scratch_shapes=[pltpu.VMEM((RET_QK_DIM, RET_V_DIM), F32)] * 2,
        compiler_params=_params())(log_g, qk, qk, qkvg, qk, qk, qkvg, qk, qkvg)


def _ret_bwd_chunk(rev, lg, q_ref, k_ref, v_ref, st_ref, do_ref, dq_ref, dk_ref, dv_ref, dlg_ref, dstate):
    intra, a_exp, q_dec, q_exp, k_dec, k_exp, chunk_dec = _ret_decays(lg, rev)
    qv, kv, vv, dov = q_ref[...], k_ref[...], v_ref[...], do_ref[...]
    s_in, ds_out = st_ref[0, 0, 0], dstate[...]
    p = _dot(qv, kv, _NT)
    w = p * intra
    dw = _dot(dov, vv, _NT)
    dp = dw * intra
    do_dec = dov * q_dec
    kd = kv * k_dec
    v_ds = _dot(vv, ds_out, _NT)
    dq_ref[...] = _dot(dp, kv, _NN) + _dot(do_dec, s_in, _NT)
    dk_ref[...] = _dot(dp, qv, _TN) + v_ds * k_dec
    dv_ref[...] = _dot(w, dov, _TN) + _dot(kd, ds_out, _NN)
    q_s = _dot(qv, s_in, _NN)
    dlg = (jnp.sum(dw * w * a_exp)
           + jnp.sum(q_exp * q_dec * jnp.sum(dov * q_s, axis=-1, keepdims=True))
           + jnp.sum(k_exp * k_dec * jnp.sum(kv * v_ds, axis=-1, keepdims=True))
           + RET_CHUNK * jnp.sum(chunk_dec * (ds_out * s_in)))
    ds_in = ds_out * chunk_dec + _dot(qv, do_dec, _TN)
    dstate[...] = ds_in
    dlg_ref[...] += dlg
    return ds_in


def _ret_bwd(qk, qkvg, log_g, st_f, st_r, do, B, seq, ctx_len, name):
    nc, qf, kf, vf, kc_spec, vc_spec, stf, of = _ret_specs(B, seq, ctx_len, _SCAN_DOWN)
    _, qr, kr, vr, _, _, str_, or_ = _ret_specs(B, seq, ctx_len, _SCAN_UP)

    def body(lg_ref, qf_ref, kf_ref, vf_ref, stf_ref, dof_ref, qr_ref, kr_ref, vr_ref, str_ref, dor_ref, kc_ref, vc_ref,
             dqf, dkf, dvf, dkcf, dvcf, dlgf, dqr, dkr, dvr, dkcr, dvcr, dlgr, dstate_f, dstate_r):
        h, c = pl.program_id(1), pl.program_id(2)
        dirs = ((False, lg_ref[0, h], (qf_ref, kf_ref, vf_ref, stf_ref, dof_ref, dqf, dkf, dvf, dlgf, dstate_f), dkcf, dvcf),
                (True, lg_ref[1, h], (qr_ref, kr_ref, vr_ref, str_ref, dor_ref, dqr, dkr, dvr, dlgr, dstate_r), dkcr, dvcr))

        @pl.when(c == 0)
        def _():
            for _, _, refs, _, _ in dirs:
                refs[-1][...] = jnp.zeros_like(refs[-1])
                refs[-2][...] = jnp.zeros_like(refs[-2])

        ds_first = [_ret_bwd_chunk(rev, lg, *refs) for rev, lg, refs, _, _ in dirs]

        @pl.when(c == nc - 1)
        def _():
            for (rev, lg, refs, dkc_ref, dvc_ref), ds_in in zip(dirs, ds_first):
                dec, expo = _ctx_decay(lg, ctx_len, rev)
                kcv, vcv = kc_ref[...], vc_ref[...]
                vc_ds = _dot(vcv, ds_in, _NT)
                dkc_ref[...] = vc_ds * dec
                dvc_ref[...] = _dot(kcv * dec, ds_in, _NN)
                refs[-2][...] += jnp.sum(expo * dec * jnp.sum(kcv * vc_ds, axis=-1, keepdims=True))

    def outs(q_spec, o_spec):
        return (pl.BlockSpec((RET_CHUNK, RET_QK_DIM), q_spec.index_map),
                pl.BlockSpec((RET_CHUNK, RET_QK_DIM), q_spec.index_map), o_spec,
                pl.BlockSpec((ctx_len, RET_QK_DIM), lambda b, h, c: (b, h)),
                pl.BlockSpec((ctx_len, RET_V_DIM), lambda b, h, c: (b, h)),
                pl.BlockSpec((1, 1, 8, LANES), lambda b, h, c: (b, h, 0, 0)))

    shapes = (jax.ShapeDtypeStruct((B * seq, RET_HEADS * RET_QK_DIM), F32),
              jax.ShapeDtypeStruct((B * seq, RET_HEADS * RET_QK_DIM), F32),
              jax.ShapeDtypeStruct((B * seq, RET_VWIDTH), F32),
              jax.ShapeDtypeStruct((B * ctx_len, RET_HEADS * RET_QK_DIM), F32),
              jax.ShapeDtypeStruct((B * ctx_len, RET_VWIDTH), F32),
              jax.ShapeDtypeStruct((B, RET_HEADS, 8, LANES), F32))
    res = pl.pallas_call(
        body, name=name, grid=(B, RET_HEADS, nc),
        in_specs=[_SMEM, qf, kf, vf, stf, of, qr, kr, vr, str_, or_, kc_spec, vc_spec],
        out_specs=outs(qf, of) + outs(qr, or_), out_shape=shapes + shapes,
        scratch_shapes=[pltpu.VMEM((RET_QK_DIM, RET_V_DIM), F32)] * 2,
        compiler_params=_params())(log_g, qk, qk, qkvg, st_f, do, qk, qk, qkvg, st_r, do, qk, qkvg)
    return res[:6], res[6:]


def _gated_out_fwd(o_f, o_b, qkvg, gn_gain, name):
    T = o_f.shape[0]
    g_off = (2 * RET_HEADS * RET_QK_DIM + RET_VWIDTH) // RET_V_DIM

    def body(of_ref, ob_ref, g_ref, gain_ref, z_ref):
        o = of_ref[...] + ob_ref[...]
        mu = jnp.mean(o, axis=-1, keepdims=True)
        var = jnp.mean(jnp.square(o - mu), axis=-1, keepdims=True)
        y = (o - mu) * lax.rsqrt(var + EPS) * gain_ref[...]
        gv = g_ref[...]
        z_ref[...] = (gv * jax.nn.sigmoid(gv) * y).astype(z_ref.dtype)

    blk = pl.BlockSpec((ROW_TILE, RET_V_DIM), lambda i, h: (i, h))
    return pl.pallas_call(
        body, name=name, grid=(T // ROW_TILE, RET_HEADS),
        in_specs=[blk, blk, pl.BlockSpec((ROW_TILE, RET_V_DIM), lambda i, h: (i, g_off + h)),
                  pl.BlockSpec((1, RET_V_DIM), lambda i, h: (0, h))],
        out_specs=blk, out_shape=jax.ShapeDtypeStruct((T, RET_VWIDTH), MXU_DTYPE),
        compiler_params=_params())(o_f, o_b, qkvg, gn_gain)


def _gated_out_bwd(dz, o_f, o_b, qkvg, gn_gain, name):
    T = o_f.shape[0]
    g_off = (2 * RET_HEADS * RET_QK_DIM + RET_VWIDTH) // RET_V_DIM

    def body(dz_ref, of_ref, ob_ref, g_ref, gain_ref, do_ref, dg_ref, dgain_ref):
        o = of_ref[...] + ob_ref[...]
        mu = jnp.mean(o, axis=-1, keepdims=True)
        var = jnp.mean(jnp.square(o - mu), axis=-1, keepdims=True)
        rstd = lax.rsqrt(var + EPS)
        yhat = (o - mu) * rstd
        gv, dzv = g_ref[...], dz_ref[...]
        sg = jax.nn.sigmoid(gv)
        dg_ref[...] = (dzv * (yhat * gain_ref[...]) * (sg * (1.0 + gv * (1.0 - sg)))).astype(dg_ref.dtype)
        dy = dzv * (gv * sg)

        @pl.when(pl.program_id(1) == 0)
        def _():
            dgain_ref[...] = jnp.zeros_like(dgain_ref)

        dgain_ref[...] += jnp.sum(dy * yhat, axis=0, keepdims=True)
        dyh = dy * gain_ref[...]
        do_ref[...] = rstd * (dyh - jnp.mean(dyh, axis=-1, keepdims=True)
                              - yhat * jnp.mean(dyh * yhat, axis=-1, keepdims=True))

    blk = pl.BlockSpec((ROW_TILE, RET_V_DIM), lambda h, i: (i, h))
    vec = pl.BlockSpec((1, RET_V_DIM), lambda h, i: (0, h))
    return pl.pallas_call(
        body, name=name, grid=(RET_HEADS, T // ROW_TILE),
        in_specs=[blk, blk, blk, pl.BlockSpec((ROW_TILE, RET_V_DIM), lambda h, i: (i, g_off + h)), vec],
        out_specs=(blk, blk, vec),
        out_shape=(jax.ShapeDtypeStruct((T, RET_VWIDTH), F32), jax.ShapeDtypeStruct((T, RET_VWIDTH), MXU_DTYPE),
                   jax.ShapeDtypeStruct((1, RET_VWIDTH), F32)),
        compiler_params=_params())(dz, o_f, o_b, qkvg, gn_gain)


def _adamw(w, m, v, parts, name):
    R, C = w.shape
    tr = _tile(R, (256, 128, 64, 32, 16, 8))
    n_parts = [p.shape[0] for p in parts]

    def body(*refs):
        w_ref, m_ref, v_ref = refs[:3]
        part_refs = refs[3:3 + len(parts)]
        g_ref, d_ref, nm_ref, nv_ref = refs[3 + len(parts):]
        g = None
        for ref, n in zip(part_refs, n_parts):
            for r in range(n):
                term = ref[r].astype(F32)
                g = term if g is None else g + term
        mn = ADAM_B1 * m_ref[...] + (1.0 - ADAM_B1) * g
        vn = ADAM_B2 * v_ref[...] + (1.0 - ADAM_B2) * jnp.square(g)
        m_hat = mn / (1.0 - ADAM_B1 ** ADAM_STEP)
        v_hat = vn / (1.0 - ADAM_B2 ** ADAM_STEP)
        g_ref[...] = g
        d_ref[...] = -ADAM_LR * (m_hat / (jnp.sqrt(v_hat) + ADAM_EPS) + ADAM_WD * w_ref[...])
        nm_ref[...] = mn
        nv_ref[...] = vn

    blk = pl.BlockSpec((tr, C), lambda i: (i, 0))
    part_specs = [pl.BlockSpec((n, tr, C), lambda i: (0, i, 0)) for n in n_parts]
    shp = jax.ShapeDtypeStruct((R, C), F32)
    return pl.pallas_call(
        body, name=name, grid=(R // tr,), in_specs=[blk, blk, blk] + part_specs,
        out_specs=(blk, blk, blk, blk), out_shape=(shp, shp, shp, shp),
        compiler_params=_params())(w, m, v, *parts)


def _sum_rows(parts, name):
    n, R, C = parts.shape
    tr = _tile(R, (256, 128, 64, 32, 16, 8))

    def body(p_ref, o_ref):
        acc = p_ref[0]
        for r in range(1, n):
            acc = acc + p_ref[r]
        o_ref[...] = acc

    return pl.pallas_call(
        body, name=name, grid=(R // tr,), in_specs=[pl.BlockSpec((n, tr, C), lambda i: (0, i, 0))],
        out_specs=pl.BlockSpec((tr, C), lambda i: (i, 0)), out_shape=jax.ShapeDtypeStruct((R, C), F32),
        compiler_params=_params())(parts)


def _my_coords():
    return lax.axis_index("x"), lax.axis_index("y"), lax.axis_index("c")


def _flip(coord, bit):
    return 1 - coord if bit else coord


def _all_gather(x2d, name):
    R, C = x2d.shape

    def body(x_ref, out_ref, send_sems, recv_sems, local_sem):
        x, y, c = _my_coords()
        me, sibling = (x, y, c), (x, y, 1 - c)
        chips = [(1 - x, y), (x, 1 - y), (1 - x, 1 - y)]

        def rows(px, py, pc):
            return out_ref.at[4 * px + 2 * py + pc]

        def copy(k, block, to, src=None):
            return pltpu.make_async_remote_copy(
                src_ref=rows(*block) if src is None else src, dst_ref=rows(*block),
                send_sem=send_sems.at[k], recv_sem=recv_sems.at[k], device_id=to, device_id_type=MESH)

        mine = pltpu.make_async_copy(x_ref, rows(*me), local_sem)
        mine.start()
        first = [copy(0, me, sibling, src=x_ref)]
        first += [copy(1 + j, me, (*chip, c), src=x_ref) for j, chip in enumerate(chips)]
        for cp in first:
            cp.start()
        passed = [copy(4 + j, (*chip, c), sibling) for j, chip in enumerate(chips)]
        for j, chip in enumerate(chips):
            copy(1 + j, (*chip, c), me).wait_recv()
            passed[j].start()
        copy(0, sibling, me).wait_recv()
        for j, chip in enumerate(chips):
            copy(4 + j, (*chip, 1 - c), me).wait_recv()
        for cp in first + passed:
            cp.wait_send()
        mine.wait()

    return pl.pallas_call(
        body, name=name, out_shape=jax.ShapeDtypeStruct((N_DEV, R, C), x2d.dtype),
        in_specs=[_ANY], out_specs=_ANY,
        scratch_shapes=[pltpu.SemaphoreType.DMA((7,)), pltpu.SemaphoreType.DMA((7,)), pltpu.SemaphoreType.DMA],
    )(x2d)


BIG_WEIGHTS = {
    "ffn_w_in": (2, (2, D_MODEL, 2 * D_FF)),
    "ffn_w_out": (1, (2, D_FF, D_MODEL)),
    "attn_w_qkv": (2, (1, D_MODEL, (N_HEADS + 2 * N_KV_HEADS) * HEAD_DIM)),
    "attn_w_o": (1, (1, N_HEADS * HEAD_DIM, D_MODEL)),
    "ret_w_qkvg": (2, (1, D_MODEL, 2 * D_MODEL + 2 * RET_VWIDTH)),
    "ret_gn_g": (2, (1, 1, RET_VWIDTH)),
    "ret_w_o": (1, (1, RET_VWIDTH, D_MODEL)),
}


def _join_shards(name, stacked):
    axis, full = BIG_WEIGHTS[name]
    if axis == 2:
        stacked = stacked.transpose(0, 2, 1, 3)
    return stacked.reshape(full)


def _split_shards(name, full_arr):
    axis, full = BIG_WEIGHTS[name]
    L, rows, cols = full
    if axis == 2:
        return full_arr.reshape(L, rows, N_DEV, cols // N_DEV).transpose(0, 2, 1, 3)
    return full_arr.reshape(L, N_DEV, rows // N_DEV, cols)


def _gather_shards(shards, name):
    n = len(shards)

    def body(*refs):
        x_refs, out_refs = refs[:n], refs[n:2 * n]
        send_sems, recv_sems, local_sems = refs[2 * n:]
        x, y, c = _my_coords()
        me, sibling = (x, y, c), (x, y, 1 - c)
        chips = [(1 - x, y), (x, 1 - y), (1 - x, 1 - y)]

        def rows(a, px, py, pc):
            return out_refs[a].at[:, 4 * px + 2 * py + pc]

        def copy(a, k, block, to, src=None):
            return pltpu.make_async_remote_copy(
                src_ref=rows(a, *block) if src is None else src, dst_ref=rows(a, *block),
                send_sem=send_sems.at[7 * a + k], recv_sem=recv_sems.at[7 * a + k], device_id=to, device_id_type=MESH)

        mine = [pltpu.make_async_copy(x_refs[a], rows(a, *me), local_sems.at[a]) for a in range(n)]
        for cp in mine:
            cp.start()
        first = []
        for a in range(n):
            first.append(copy(a, 0, me, sibling, src=x_refs[a]))
            first += [copy(a, 1 + j, me, (*chip, c), src=x_refs[a]) for j, chip in enumerate(chips)]
        for cp in first:
            cp.start()
        passed = []
        for j, chip in enumerate(chips):
            for a in range(n):
                copy(a, 1 + j, (*chip, c), me).wait_recv()
                fwd = copy(a, 4 + j, (*chip, c), sibling)
                fwd.start()
                passed.append(fwd)
        for a in range(n):
            copy(a, 0, sibling, me).wait_recv()
            for j, chip in enumerate(chips):
                copy(a, 4 + j, (*chip, 1 - c), me).wait_recv()
        for cp in first + passed:
            cp.wait_send()
        for cp in mine:
            cp.wait()

    return pl.pallas_call(
        body, name=name,
        out_shape=[jax.ShapeDtypeStruct((s.shape[0], N_DEV) + s.shape[1:], s.dtype) for s in shards],
        in_specs=[_ANY] * n, out_specs=[_ANY] * n,
        scratch_shapes=[pltpu.SemaphoreType.DMA((7 * n,)), pltpu.SemaphoreType.DMA((7 * n,)),
                        pltpu.SemaphoreType.DMA((n,))],
    )(*shards)


def _exchange_shards(arrs, masks, src_of, out_tail, name):
    n, nm = len(arrs), len(masks)

    def body(*refs):
        in_refs, out_refs = refs[:n], refs[n:2 * n]
        send_sems, recv_sems = refs[2 * n:]
        x, y, c = _my_coords()
        copies = []
        for a in range(n):
            for k, (bx, by, bc) in enumerate(masks):
                peer = (_flip(x, bx), _flip(y, by), _flip(c, bc))
                copies.append(pltpu.make_async_remote_copy(
                    src_ref=src_of(in_refs[a], peer, (x, y, c)), dst_ref=out_refs[a].at[k],
                    send_sem=send_sems.at[nm * a + k], recv_sem=recv_sems.at[nm * a + k],
                    device_id=peer, device_id_type=MESH))
        for cp in copies:
            cp.start()
        for cp in copies:
            cp.wait()

    return pl.pallas_call(
        body, name=name,
        out_shape=[jax.ShapeDtypeStruct((nm,) + out_tail(s), s.dtype) for s in arrs],
        in_specs=[_ANY] * n, out_specs=[_ANY] * n,
        scratch_shapes=[pltpu.SemaphoreType.DMA((nm * n,)), pltpu.SemaphoreType.DMA((nm * n,))],
    )(*arrs)


def _pair_sum(g, from_sibling, core, out_dtype, name):
    L, _, _, a, b = g.shape
    ta = a

    def body(core_ref, g_ref, s_ref, o_ref):
        o_ref[...] = (g_ref[...] + s_ref[...]).astype(out_dtype)

    blk = pl.BlockSpec((1, 1, ta, b), lambda l, q, i, core_ref: (l, q, i, 0))
    return pl.pallas_call(
        body, name=name,
        grid_spec=pltpu.PrefetchScalarGridSpec(
            num_scalar_prefetch=1, grid=(L, 4, a // ta),
            in_specs=[pl.BlockSpec((1, 1, pl.Squeezed(), ta, b), lambda l, q, i, core_ref: (l, q, core_ref[0], i, 0)), blk],
            out_specs=blk),
        out_shape=jax.ShapeDtypeStruct((L, 4, a, b), out_dtype), compiler_params=_params())(core, g, from_sibling)


def _mods(mod_x, mod_c, layer):
    both = jnp.concatenate([mod_x[:, layer], mod_c[layer][None]], axis=0)
    return [both[:, None, k * D_MODEL:(k + 1) * D_MODEL] for k in range(6)]


def _local_step(x, ctx, target, mod_x, mod_c, w, small, late_weights=None):
    B, S, _ = x.shape
    L = ctx.shape[1]
    NX, NC = B * S, B * L
    T = NX + NC
    tiles_per_ex = S // ROW_TILE
    nxt = NX // ROW_TILE
    gidx = _group_index(nxt, tiles_per_ex, B)
    gidx_for = lambda rows: _group_index(NX // rows, S // rows, B)
    mm_rows = _tile(S, (MM_ROWS, ROW_TILE))
    tidx = lambda i: jnp.where(i < nxt, i % tiles_per_ex, tiles_per_ex)
    G = B + 1
    x0 = jnp.concatenate([x.reshape(NX, D_MODEL), ctx.reshape(NC, D_MODEL)], axis=0)
    acos, asin = [jnp.tile(t, (1, LANES // HEAD_DIM)) for t in _rope_tables(S, HEAD_DIM)]
    rcos, rsin = _rope_tables(S, RET_QK_DIM)
    sink = small["attn_sink"].reshape(N_KV_HEADS, GQA_GROUP)
    gains = jnp.stack([jnp.tile(small["attn_q_norm"].reshape(1, HEAD_DIM), (1, LANES // HEAD_DIM)),
                       jnp.tile(small["attn_k_norm"].reshape(1, HEAD_DIM), (1, LANES // HEAD_DIM))])
    log_g = jax.nn.log_sigmoid(small["ret_decay_logit"].reshape(2, RET_HEADS))
    n1, n2 = small["norm1_g"], small["norm2_g"]

    m0 = _mods(mod_x, mod_c, 0)
    h1 = _norm_mod_fwd(x0, n1[0:1], m0[0], m0[1], gidx, "l0_norm1")
    qkv = _mm(h1, w["attn_w_qkv"][0], "nn", F32, "l0_qkv")
    qkv_r = _attn_prep_fwd(qkv, gains, acos, asin, tidx, "l0_qk_prep")
    o_x = _attn_fwd(qkv_r, sink, B, S, L, False, "l0_attn_x")
    o_c = _attn_fwd(qkv_r, sink, B, S, L, True, "l0_attn_c")
    o0 = jnp.concatenate([o_x, o_c], axis=0)
    mo0, x1 = _mm(o0, w["attn_w_o"][0], "nn", F32, "l0_attn_out", res=x0, gate=m0[2], gidx_for=gidx_for, gate_rows=mm_rows)
    h2 = _norm_mod_fwd(x1, n2[0:1], m0[3], m0[4], gidx, "l0_norm2")
    if late_weights is not None:
        w = {**w, **late_weights(x1)}
    u0 = _mm(h2, w["ffn_w_in"][0], "nn", F32, "l0_ffn_in")
    a0 = _swiglu_fwd(u0, "l0_swiglu")
    f0, x2 = _mm(a0, w["ffn_w_out"][0], "nn", F32, "l0_ffn_out", res=x1, gate=m0[5], gidx_for=gidx_for, gate_rows=mm_rows)

    m1 = _mods(mod_x, mod_c, 1)
    g1 = _norm_mod_fwd(x2, n1[1:2], m1[0], m1[1], gidx, "l1_norm1")
    qkvg = _mm(g1, w["ret_w_qkvg"][0], "nn", F32, "l1_qkvg")
    qk = _ret_rope(qkvg, rcos, rsin, tidx, "l1_rope")
    of, st_f, ob, st_b = _ret_fwd(qk, qkvg, log_g, B, S, L, "l1_ret")
    gn = w["ret_gn_g"].reshape(1, RET_VWIDTH)
    z1 = _gated_out_fwd(of, ob, qkvg, gn, "l1_gated_out")
    xx2 = x2[:NX]
    gx = lambda i: i // tiles_per_ex
    m1x = [t[:B] for t in m1]
    mo1, y1 = _mm(z1, w["ret_w_o"][0], "nn", F32, "l1_ret_out", res=xx2, gate=m1x[2], gidx_for=gidx_for, gate_rows=mm_rows)
    k2 = _norm_mod_fwd(y1, n2[1:2], m1x[3], m1x[4], gx, "l1_norm2")
    u1 = _mm(k2, w["ffn_w_in"][1], "nn", F32, "l1_ffn_in")
    a1 = _swiglu_fwd(u1, "l1_swiglu")
    f1, y2 = _mm(a1, w["ffn_w_out"][1], "nn", F32, "l1_ffn_out", res=y1, gate=m1x[5], gidx_for=gidx_for, gate_rows=mm_rows)

    loss_tile, dy2 = _loss_fwd_bwd(y2, target.reshape(NX, D_MODEL), "loss")

    zg = jnp.zeros((1, 1, D_MODEL), F32)
    dz, dgate5_1 = _gate_bwd(dy2, f1, m1x[5], gx, B, "l1_ffn_gate_bwd")
    gw_ffn_out1 = _mm(a1, dz, "tn", F32, "l1_ffn_out_dw")
    da = _mm(dz, w["ffn_w_out"][1], "nt", F32, "l1_ffn_out_dx")
    du = _swiglu_bwd(da, u1, "l1_swiglu_bwd")
    gw_ffn_in1 = _mm(k2, du, "tn", F32, "l1_ffn_in_dw")
    dk2 = _mm(du, w["ffn_w_in"][1], "nt", F32, "l1_ffn_in_dx")
    dy1, dsh3_1, dsc4_1, dn2_1 = _norm_mod_bwd(dk2, y1, n2[1:2], m1x[4], dy2, gx, B, "l1_norm2_bwd")
    dzo, dgate2_1 = _gate_bwd(dy1, mo1, m1x[2], gx, B, "l1_ret_gate_bwd")
    gw_ret_o = _mm(z1, dzo, "tn", F32, "l1_ret_out_dw")
    dz1 = _mm(dzo, w["ret_w_o"][0], "nt", F32, "l1_ret_out_dx")
    do_r, dg_r, dgn = _gated_out_bwd(dz1, of, ob, qkvg, gn, "l1_gated_out_bwd")
    ((dq_f, dk_f, dv_f, dkc_f, dvc_f, dlg_f),
     (dq_b, dk_b, dv_b, dkc_b, dvc_b, dlg_b)) = _ret_bwd(qk, qkvg, log_g, st_f, st_b, do_r, B, S, L, "l1_ret_bwd")
    dqkvg = _ret_grad_assemble((dq_f, dq_b, dk_f, dk_b, dv_f, dv_b), (dkc_f, dkc_b, dvc_f, dvc_b), dg_r, rcos, rsin, S,
                               "l1_qkvg_grad")
    gw_ret_qkvg = _mm(g1, dqkvg, "tn", F32, "l1_qkvg_dw")
    dg1 = _mm(dqkvg, w["ret_w_qkvg"][0], "nt", F32, "l1_qkvg_dx")
    dres1 = jnp.concatenate([dy1, jnp.zeros((NC, D_MODEL), F32)], axis=0)
    dx2, dsh0_1, dsc1_1, dn1_1 = _norm_mod_bwd(dg1, x2, n1[1:2], m1[1], dres1, gidx, G, "l1_norm1_bwd")
    dlg = jnp.stack([jnp.sum(dlg_f[:, :, 0, 0], axis=0), jnp.sum(dlg_b[:, :, 0, 0], axis=0)])
    d_decay = (dlg * jax.nn.sigmoid(-small["ret_decay_logit"].reshape(2, RET_HEADS))).reshape(1, 2, RET_HEADS)

    dz, dgate5_0 = _gate_bwd(dx2, f0, m0[5], gidx, G, "l0_ffn_gate_bwd")
    gw_ffn_out0 = _mm(a0, dz, "tn", F32, "l0_ffn_out_dw")
    da = _mm(dz, w["ffn_w_out"][0], "nt", F32, "l0_ffn_out_dx")
    du = _swiglu_bwd(da, u0, "l0_swiglu_bwd")
    gw_ffn_in0 = _mm(h2, du, "tn", F32, "l0_ffn_in_dw")
    dh2 = _mm(du, w["ffn_w_in"][0], "nt", F32, "l0_ffn_in_dx")
    dx1, dsh3_0, dsc4_0, dn2_0 = _norm_mod_bwd(dh2, x1, n2[0:1], m0[4], dx2, gidx, G, "l0_norm2_bwd")
    dzo, dgate2_0 = _gate_bwd(dx1, mo0, m0[2], gidx, G, "l0_attn_gate_bwd")
    gw_attn_o = _mm(o0, dzo, "tn", F32, "l0_attn_out_dw")
    do0 = _mm(dzo, w["attn_w_o"][0], "nt", MXU_DTYPE, "l0_attn_out_dx")
    dq_x, dk_x, dv_x, dkc1, dvc1, dsink_x = _attn_bwd(qkv_r, sink, do0, B, S, L, False, "l0_attn_x_bwd")
    dq_c, dkc2, dvc2, dsink_c = _attn_bwd(qkv_r, sink, do0, B, S, L, True, "l0_attn_c_bwd")
    dqk = jnp.concatenate([jnp.concatenate([dq_x, dk_x], axis=1), jnp.concatenate([dq_c, dkc1 + dkc2], axis=1)], axis=0)
    dvv = jnp.concatenate([dv_x, dvc1 + dvc2], axis=0)
    dqkv, dgains = _attn_prep_bwd(dqk, dvv, qkv, gains, acos, asin, tidx, "l0_qk_prep_bwd")
    gw_attn_qkv = _mm(h1, dqkv, "tn", F32, "l0_qkv_dw")
    dh1 = _mm(dqkv, w["attn_w_qkv"][0], "nt", F32, "l0_qkv_dx")
    dx0, dsh0_0, dsc1_0, dn1_0 = _norm_mod_bwd(dh1, x0, n1[0:1], m0[1], dx1, gidx, G, "l0_norm1_bwd")

    dgains = jnp.sum(dgains.reshape(ATTN_QK_BLOCKS, LANES // HEAD_DIM, HEAD_DIM), axis=1)
    dsink = (dsink_x + dsink_c).reshape(N_KV_HEADS, 8, LANES)[:, :GQA_GROUP, 0].reshape(1, N_HEADS)
    grads_big = {
        "ffn_w_in": jnp.stack([gw_ffn_in0, gw_ffn_in1]),
        "ffn_w_out": jnp.stack([gw_ffn_out0, gw_ffn_out1]),
        "attn_w_qkv": gw_attn_qkv[None],
        "attn_w_o": gw_attn_o[None],
        "ret_w_qkvg": gw_ret_qkvg[None],
        "ret_gn_g": dgn,
        "ret_w_o": gw_ret_o[None],
    }
    grads_small = {
        "norm1_g": jnp.concatenate([dn1_0, dn1_1], axis=0),
        "norm2_g": jnp.concatenate([dn2_0, dn2_1], axis=0),
        "attn_q_norm": jnp.sum(dgains[:ATTN_Q_BLOCKS], axis=0)[None],
        "attn_k_norm": jnp.sum(dgains[ATTN_Q_BLOCKS:ATTN_QK_BLOCKS], axis=0)[None],
        "attn_sink": dsink,
        "ret_decay_logit": d_decay,
    }

    def pad_g(t):
        return jnp.concatenate([t, zg], axis=0)

    d0 = jnp.concatenate([dsh0_0, dsc1_0, dgate2_0, dsh3_0, dsc4_0, dgate5_0], axis=2)[:, 0]
    d1 = jnp.concatenate([dsh0_1, dsc1_1, pad_g(dgate2_1), pad_g(dsh3_1), pad_g(dsc4_1), pad_g(dgate5_1)],
                         axis=2)[:, 0]
    dmod_x = jnp.stack([d0[:B], d1[:B]], axis=1)
    dmod_c = jnp.stack([d0[B], d1[B]], axis=0)
    return loss_tile, dx0[:NX].reshape(B, S, D_MODEL), grads_big, grads_small, dmod_x, dmod_c


SMALL_NAMES = ("c_ctx", "ada_b", "norm1_g", "norm2_g", "attn_q_norm", "attn_k_norm", "attn_sink", "ret_decay_logit")
ADA_ROWS = 64


def _pack_small(d, rows):
    flat = jnp.concatenate([d[k].reshape(-1) for k in SMALL_NAMES])
    n = rows * LANES
    return jnp.pad(flat, (0, n - flat.shape[0])).reshape(rows, LANES)


def _unpack_small(packed, shapes):
    flat = packed.reshape(-1)
    out, off = {}, 0
    for k in SMALL_NAMES:
        n = math.prod(shapes[k])
        out[k] = flat[off:off + n].reshape(shapes[k])
        off += n
    return out


EARLY_WEIGHTS = ("attn_w_qkv", "attn_w_o")
LATE_WEIGHTS = tuple(k for k in BIG_WEIGHTS if k not in EARLY_WEIGHTS)

_HBM = pl.BlockSpec(memory_space=pltpu.HBM)
_SEM = pl.BlockSpec(memory_space=pltpu.SEMAPHORE)
_DATAFLOW = pltpu.SideEffectType.DATAFLOW_SIDE_EFFECTING
_PEER_FLIPS = ((0, 0, 1), (0, 1, 0), (0, 1, 1), (1, 0, 0), (1, 0, 1), (1, 1, 0), (1, 1, 1))


def _wire_shard(name, t):
    return t.reshape(1, 1, -1) if name == "ret_gn_g" else t.astype(MXU_DTYPE)


def _direct_copies(x_refs, land_refs, send_sems, recv_sems, landing):
    x, y, c = _my_coords()
    out = []
    for a in range(len(x_refs)):
        for k, (bx, by, bc) in enumerate(_PEER_FLIPS):
            peer = (_flip(x, bx), _flip(y, by), _flip(c, bc))
            slot = (4 * peer[0] + 2 * peer[1] + peer[2]) if landing else (4 * x + 2 * y + c)
            out.append(pltpu.make_async_remote_copy(
                src_ref=x_refs[a], dst_ref=land_refs[a].at[:, slot], send_sem=send_sems.at[7 * a + k],
                recv_sem=recv_sems.at[7 * a + k], device_id=peer, device_id_type=MESH))
    return out


def _gather_start(shards, name):
    n = len(shards)
    lands = [lax.empty((s.shape[0], N_DEV) + s.shape[1:], s.dtype) for s in shards]

    def body(*refs):
        send_sems, recv_sems = refs[2 * n], refs[2 * n + 1]
        x_refs, land_refs = refs[2 * n + 2:3 * n + 2], refs[3 * n + 2:4 * n + 2]
        for cp in _direct_copies(x_refs, land_refs, send_sems, recv_sems, landing=False):
            cp.start()
        refs[-1][...] = jnp.zeros_like(refs[-1])

    hbm = lambda t: pltpu.with_memory_space_constraint(t, pltpu.HBM)
    res = pl.pallas_call(
        body, name=name,
        out_shape=(pltpu.SemaphoreType.DMA((7 * n,)), pltpu.SemaphoreType.DMA((7 * n,)))
        + tuple(pltpu.HBM(t.shape, t.dtype) for t in shards + lands) + (jax.ShapeDtypeStruct((8, LANES), F32),),
        in_specs=[_HBM] * (2 * n), out_specs=(_SEM, _SEM) + (_HBM,) * (2 * n) + (pl.BlockSpec(memory_space=pltpu.VMEM),),
        input_output_aliases={i: 2 + i for i in range(2 * n)},
        compiler_params=pltpu.CompilerParams(has_side_effects=_DATAFLOW))(*[hbm(t) for t in shards + lands])
    return res[0], res[1], list(res[2:2 + n]), list(res[2 + n:2 + 2 * n]), res[-1]


def _gather_wait(send_sems, recv_sems, shards, lands, after, name):
    n = len(shards)

    def body(*refs):
        x_refs, land_refs = refs[:n], refs[n:2 * n]
        for cp in _direct_copies(x_refs, land_refs, refs[2 * n], refs[2 * n + 1], landing=True):
            cp.wait_send()
            cp.wait_recv()

    res = pl.pallas_call(
        body, name=name, out_shape=tuple(pltpu.HBM(t.shape, t.dtype) for t in shards + lands),
        in_specs=[_HBM] * (2 * n) + [_SEM, _SEM, _ANY], out_specs=(_HBM,) * (2 * n),
        input_output_aliases={i: i for i in range(2 * n)},
        compiler_params=pltpu.CompilerParams(has_side_effects=_DATAFLOW))(*shards, *lands, send_sems, recv_sems, after)
    return list(res[n:])


def _gather_big_weights(weights, names, name):
    gathered = _gather_shards([_wire_shard(k, weights[k]) for k in names], name)
    return {k: _join_shards(k, g) for k, g in zip(names, gathered)}


def _reduce_and_update_big(g_big, weights, mom1, mom2):
    mx_, my_, mc_ = _my_coords()
    my_chip = 2 * mx_ + my_
    names = list(BIG_WEIGHTS)
    split = []
    for k in names:
        s = _split_shards(k, g_big[k])
        split.append(s.reshape(s.shape[0], 4, 2, s.shape[2], s.shape[3]))
    from_sibling = _exchange_shards(
        split, [(0, 0, 1)], lambda ref, peer, me_: ref.at[:, :, peer[2]],
        lambda s: (s.shape[0], 4) + s.shape[3:], "rs_sibling")
    from_sibling = [t[0] for t in from_sibling]
    core = mc_.astype(jnp.int32).reshape(1)
    pair = [_pair_sum(g, s, core, MXU_DTYPE, "rs_pair_" + k) for k, g, s in zip(names, split, from_sibling)]
    from_chips = _exchange_shards(
        pair, [(1, 0, 0), (0, 1, 0), (1, 1, 0)], lambda ref, peer, me_: ref.at[:, 2 * peer[0] + peer[1]],
        lambda s: (s.shape[0],) + s.shape[2:], "rs_chips")
    big = {}
    for k, g, s, r in zip(names, split, from_sibling, from_chips):
        L_, _, _, a_, b_ = g.shape
        own_keep = lax.dynamic_index_in_dim(lax.dynamic_index_in_dim(g, my_chip, axis=1, keepdims=False), mc_, axis=1,
                                            keepdims=False)
        own_sib = lax.dynamic_index_in_dim(s, my_chip, axis=1, keepdims=False)
        rows = L_ * a_
        res = _adamw(weights[k].reshape(rows, b_), mom1[k].reshape(rows, b_), mom2[k].reshape(rows, b_),
                     [own_keep.reshape(1, rows, b_), own_sib.reshape(1, rows, b_), r.reshape(3, rows, b_)],
                     "adamw_" + k)
        big[k] = [t.reshape(weights[k].shape) for t in res]
    return big


def kernel(x, c, ctx, c_ctx, ada_w, ada_b, norm1_g, norm2_g, ffn_w_in, ffn_w_out, attn_w_qkv, attn_q_norm, attn_k_norm, attn_sink, attn_w_o, ret_w_qkvg, ret_decay_logit, ret_gn_g, ret_w_o, loss_target, m_c_ctx, m_ada_w, m_ada_b, m_norm1_g, m_norm2_g, m_ffn_w_in, m_ffn_w_out, m_attn_w_qkv, m_attn_q_norm, m_attn_k_norm, m_attn_sink, m_attn_w_o, m_ret_w_qkvg, m_ret_decay_logit, m_ret_gn_g, m_ret_w_o, v_c_ctx, v_ada_w, v_ada_b, v_norm1_g, v_norm2_g, v_ffn_w_in, v_ffn_w_out, v_attn_w_qkv, v_attn_q_norm, v_attn_k_norm, v_attn_sink, v_attn_w_o, v_ret_w_qkvg, v_ret_decay_logit, v_ret_gn_g, v_ret_w_o):
    weights = dict(c_ctx=c_ctx, ada_w=ada_w, ada_b=ada_b, norm1_g=norm1_g, norm2_g=norm2_g, ffn_w_in=ffn_w_in,
                   ffn_w_out=ffn_w_out, attn_w_qkv=attn_w_qkv, attn_q_norm=attn_q_norm, attn_k_norm=attn_k_norm,
                   attn_sink=attn_sink, attn_w_o=attn_w_o, ret_w_qkvg=ret_w_qkvg, ret_decay_logit=ret_decay_logit,
                   ret_gn_g=ret_gn_g, ret_w_o=ret_w_o)
    mom1 = dict(c_ctx=m_c_ctx, ada_w=m_ada_w, ada_b=m_ada_b, norm1_g=m_norm1_g, norm2_g=m_norm2_g, ffn_w_in=m_ffn_w_in,
                ffn_w_out=m_ffn_w_out, attn_w_qkv=m_attn_w_qkv, attn_q_norm=m_attn_q_norm, attn_k_norm=m_attn_k_norm,
                attn_sink=m_attn_sink, attn_w_o=m_attn_w_o, ret_w_qkvg=m_ret_w_qkvg, ret_decay_logit=m_ret_decay_logit,
                ret_gn_g=m_ret_gn_g, ret_w_o=m_ret_w_o)
    mom2 = dict(c_ctx=v_c_ctx, ada_w=v_ada_w, ada_b=v_ada_b, norm1_g=v_norm1_g, norm2_g=v_norm2_g, ffn_w_in=v_ffn_w_in,
                ffn_w_out=v_ffn_w_out, attn_w_qkv=v_attn_w_qkv, attn_q_norm=v_attn_q_norm, attn_k_norm=v_attn_k_norm,
                attn_sink=v_attn_sink, attn_w_o=v_attn_w_o, ret_w_qkvg=v_ret_w_qkvg, ret_decay_logit=v_ret_decay_logit,
                ret_gn_g=v_ret_gn_g, ret_w_o=v_ret_w_o)
    B = x.shape[0]
    mx_, my_, mc_ = _my_coords()
    me = 4 * mx_ + 2 * my_ + mc_
    ada_cols = ada_w.shape[2]

    late_shards = [_wire_shard(k, weights[k]) for k in LATE_WEIGHTS]
    send_sems, recv_sems, late_thru, late_lands, token = _gather_start(late_shards, "gather_late_start")
    c = c + token[0, 0]

    def late_weights(after):
        lands = _gather_wait(send_sems, recv_sems, late_thru, late_lands, after, "gather_late_wait")
        own = [lax.dynamic_update_index_in_dim(land, shard, me, axis=1) for land, shard in zip(lands, late_shards)]
        return {k: _join_shards(k, g) for k, g in zip(LATE_WEIGHTS, own)}

    w_full = _gather_big_weights({k: weights[k] + token[0, 0] for k in EARLY_WEIGHTS}, EARLY_WEIGHTS, "gather_early")

    c_all = _all_gather(jax.nn.silu(c), "gather_c").reshape(N_DEV * B, D_MODEL)
    cc_act = jax.nn.silu(c_ctx)[None]
    ada_in = jnp.concatenate([c_all, cc_act, jnp.zeros((ADA_ROWS - N_DEV * B - 1, D_MODEL), F32)], axis=0)
    ada_in = ada_in.astype(MXU_DTYPE)
    ada_w2 = jnp.concatenate([ada_w[0], ada_w[1]], axis=1)
    bias = lax.dynamic_slice_in_dim(ada_b.reshape(2, N_DEV, ada_cols), me, 1, axis=1).reshape(1, 2 * ada_cols)
    mod_cols = _mm(ada_in, ada_w2, "nn", F32, "ada_fwd", bias=bias)
    mod_all = _all_gather(mod_cols, "gather_mod")
    mod_all = mod_all.reshape(N_DEV, ADA_ROWS, 2, ada_cols).transpose(1, 2, 0, 3).reshape(ADA_ROWS, 2, N_DEV * ada_cols)
    mod_x = lax.dynamic_slice_in_dim(mod_all, me * B, B, axis=0)
    mod_c = mod_all[N_DEV * B]

    small = {k: weights[k] for k in SMALL_NAMES}
    loss_tile, grad_x, g_big, g_small, dmod_x, dmod_c = _local_step(x, ctx, loss_target, mod_x, mod_c, w_full, small,
                                                                     late_weights)
    loss = lax.psum(loss_tile[0, 0], ("x", "y", "c"))

    n_mod = 2 * 6 * D_MODEL
    dm_rows = jnp.concatenate([dmod_x.reshape(B, n_mod), dmod_c.reshape(1, n_mod),
                               jnp.zeros((8 - B - 1, n_mod), F32)], axis=0)
    dm_all = _all_gather(dm_rows, "gather_dmod")
    dmc_tot = _sum_rows(dm_all[:, B:B + 1].reshape(N_DEV, 1, n_mod)[:, :, :].reshape(N_DEV, n_mod // LANES, LANES),
                        "sum_dmod_c").reshape(1, n_mod)
    dmod_rows = jnp.concatenate([dm_all[:, :B].reshape(N_DEV * B, n_mod), dmc_tot,
                                 jnp.zeros((ADA_ROWS - N_DEV * B - 1, n_mod), F32)], axis=0)
    dmod_mine = lax.dynamic_slice_in_dim(dmod_rows.reshape(ADA_ROWS, 2, N_DEV, ada_cols), me, 1, axis=2)
    dmod_mine = dmod_mine.reshape(ADA_ROWS, 2 * ada_cols).astype(MXU_DTYPE)
    g_ada2 = _mm(ada_in, dmod_mine, "tn", F32, "ada_dw")
    g_ada_w = jnp.stack([g_ada2[:, :ada_cols], g_ada2[:, ada_cols:]])
    dmc_mine = jnp.concatenate([dmod_mine[N_DEV * B:N_DEV * B + 1], jnp.zeros((7, 2 * ada_cols), MXU_DTYPE)], axis=0)
    dcc_part = _mm(dmc_mine, ada_w2, "nt", F32, "ada_dc")[0:1]
    g_ada_b = _sum_rows(dmod_rows[:, None, :].reshape(ADA_ROWS, n_mod // LANES, LANES), "sum_dmod_b").reshape(2, 6 * D_MODEL)
    sg = jax.nn.sigmoid(c_ctx)
    g_small["c_ctx"] = dcc_part.reshape(D_MODEL) * (sg * (1.0 + c_ctx * (1.0 - sg)))
    g_small["ada_b"] = g_ada_b * (1.0 / N_DEV)

    shapes = {k: weights[k].shape for k in SMALL_NAMES}
    n_small = sum(math.prod(s) for s in shapes.values())
    srows = -(-(-(-n_small // LANES)) // 8) * 8
    gs_all = _all_gather(_pack_small(g_small, srows), "gather_small_grads")
    sm = _adamw(_pack_small({k: weights[k] for k in SMALL_NAMES}, srows), _pack_small({k: mom1[k] for k in SMALL_NAMES}, srows),
                _pack_small({k: mom2[k] for k in SMALL_NAMES}, srows), [gs_all], "adamw_small")
    sm = [_unpack_small(t, shapes) for t in sm]

    ada_shape = ada_w.shape
    r2 = lambda t: t.reshape(ada_shape[0] * ada_shape[1], ada_shape[2])
    ada = [t.reshape(ada_shape) for t in _adamw(r2(ada_w), r2(m_ada_w), r2(v_ada_w), [r2(g_ada_w)[None]], "adamw_ada")]

    big = _reduce_and_update_big(g_big, weights, mom1, mom2)

    def pick(i, name):
        if name in BIG_WEIGHTS:
            return big[name][i]
        if name == "ada_w":
            return ada[i]
        return sm[i][name]

    order = ("c_ctx", "ada_w", "ada_b", "norm1_g", "norm2_g", "ffn_w_in", "ffn_w_out", "attn_w_qkv", "attn_q_norm",
             "attn_k_norm", "attn_sink", "attn_w_o", "ret_w_qkvg", "ret_decay_logit", "ret_gn_g", "ret_w_o")
    outs = [loss, grad_x]
    for i in range(4):
        outs += [pick(i, n) for n in order]
    return tuple(outs)
```

```python
import functools
import math

import jax
import jax.numpy as jnp
from jax import lax
from jax.experimental import pallas as pl
from jax.experimental.pallas import tpu as pltpu

F32 = jnp.float32
MXU_DTYPE = jnp.bfloat16

D_MODEL = 1024
HEAD_DIM = 64
N_HEADS = 16
N_KV_HEADS = 4
GQA_GROUP = 4
WINDOW = 128
ATTN_BLOCK = 128
RET_HEADS = 4
RET_QK_DIM = 256
RET_V_DIM = 512
RET_VWIDTH = 2048
RET_CHUNK = 256
D_FF = 2816
GRID_W = 64
ROPE_BASE = 10000.0
EPS = 1e-6
NEG_INF = -1e30

ADAM_LR = 0.001
ADAM_B1 = 0.9
ADAM_B2 = 0.999
ADAM_EPS = 1e-08
ADAM_WD = 0.01
ADAM_STEP = 10

N_DEV = 8
LANES = 128
ROW_TILE = 512
VMEM_LIMIT = 48 * 1024 * 1024

MESH = pl.DeviceIdType.MESH
_ANY = pl.BlockSpec(memory_space=pl.ANY)
_SMEM = pl.BlockSpec(memory_space=pltpu.SMEM)


def _params(**kw):
    return pltpu.CompilerParams(vmem_limit_bytes=VMEM_LIMIT, **kw)


def _mx(v):
    return v.astype(MXU_DTYPE)


def _dot(a, b, dims):
    return lax.dot_general(_mx(a), _mx(b), (dims, ((), ())), preferred_element_type=F32)


_NN = ((1,), (0,))
_NT = ((1,), (1,))
_TN = ((0,), (0,))


def _tile(n, cands):
    for c in cands:
        if n % c == 0:
            return c
    return n


def _big_tile(n, cap):
    if n <= cap:
        return n
    for t in range(cap - cap % LANES, 0, -LANES):
        if n % t == 0:
            return t
    return n


MM_ROWS = 1024
MM_COLS = 1408
MM_DEPTH = 2048


def _k_tile(k):
    return _big_tile(k, MM_DEPTH)


def _mm(a, b, mode, out_dtype, name, *, bias=None, res=None, gate=None, gidx_for=None, gate_rows=None):
    if mode == "nn":
        (M, K), (_, N) = a.shape, b.shape
    elif mode == "nt":
        (M, K), (N, _) = a.shape, b.shape
    else:
        (K, M), (_, N) = a.shape, b.shape
    if res is not None:
        tm, tn = gate_rows, _big_tile(N, 512)
        gidx = gidx_for(tm)
    else:
        tm = _big_tile(M, MM_COLS if mode == "tn" else MM_ROWS)
        tn = _big_tile(N, MM_COLS)
    tk = _k_tile(K)
    nk = K // tk
    dims = {"nn": _NN, "nt": _NT, "tn": _TN}[mode]
    a_spec = (pl.BlockSpec((tk, tm), lambda i, j, k: (k, i)) if mode == "tn"
              else pl.BlockSpec((tm, tk), lambda i, j, k: (i, k)))
    b_spec = (pl.BlockSpec((tn, tk), lambda i, j, k: (j, k)) if mode == "nt"
              else pl.BlockSpec((tk, tn), lambda i, j, k: (k, j)))
    o_spec = pl.BlockSpec((tm, tn), lambda i, j, k: (i, j))
    in_specs, operands = [a_spec, b_spec], [a, b]
    if bias is not None:
        in_specs.append(pl.BlockSpec((1, tn), lambda i, j, k: (0, j)))
        operands.append(bias)
    if res is not None:
        in_specs += [o_spec, pl.BlockSpec((1, 1, tn), lambda i, j, k: (gidx(i), 0, j))]
        operands += [res, gate]
        out_shape = (jax.ShapeDtypeStruct((M, N), F32), jax.ShapeDtypeStruct((M, N), F32))
        out_specs = (o_spec, o_spec)
    else:
        out_shape = jax.ShapeDtypeStruct((M, N), out_dtype)
        out_specs = o_spec

    def body(*refs):
        a_ref, b_ref = refs[0], refs[1]
        extra = refs[2:len(operands)]
        outs = refs[len(operands):]
        prod = _dot(a_ref[...], b_ref[...], dims)

        def finish(acc):
            if bias is not None:
                outs[0][...] = (acc + extra[0][...]).astype(out_dtype)
            elif res is not None:
                outs[0][...] = acc
                outs[1][...] = extra[0][...] + extra[1][0] * acc
            else:
                outs[0][...] = acc.astype(out_dtype)

        if nk == 1:
            finish(prod)
        else:
            acc_ref = outs[-1]
            outs = outs[:-1]
            k = pl.program_id(2)

            @pl.when(k == 0)
            def _():
                acc_ref[...] = prod

            @pl.when(k > 0)
            def _():
                acc_ref[...] += prod

            @pl.when(k == nk - 1)
            def _():
                finish(acc_ref[...])

    return pl.pallas_call(
        body, name=name, grid=(M // tm, N // tn, nk), in_specs=in_specs, out_specs=out_specs, out_shape=out_shape,
        scratch_shapes=[pltpu.VMEM((tm, tn), F32)] if nk > 1 else [],
        compiler_params=_params())(*operands)


def _group_index(n_x_tiles, tiles_per_example, n_examples):
    def gidx(i):
        return jnp.where(i < n_x_tiles, i // tiles_per_example, n_examples)
    return gidx


def _norm_mod_fwd(x, g, shift, scale, gidx, name):
    T, Dm = x.shape

    def body(x_ref, g_ref, sh_ref, sc_ref, h_ref):
        xv = x_ref[...]
        r = lax.rsqrt(jnp.mean(xv * xv, axis=-1, keepdims=True) + EPS)
        y = xv * r * g_ref[...]
        h_ref[...] = (y * (1.0 + sc_ref[0]) + sh_ref[0]).astype(h_ref.dtype)

    row = pl.BlockSpec((ROW_TILE, Dm), lambda i: (i, 0))
    mod = pl.BlockSpec((1, 1, Dm), lambda i: (gidx(i), 0, 0))
    return pl.pallas_call(
        body, name=name, grid=(T // ROW_TILE,),
        in_specs=[row, pl.BlockSpec((1, Dm), lambda i: (0, 0)), mod, mod],
        out_specs=row, out_shape=jax.ShapeDtypeStruct((T, Dm), MXU_DTYPE),
        compiler_params=_params())(x, g, shift, scale)


def _first_of_group(i, gidx):
    return jnp.logical_or(i == 0, gidx(i) != gidx(jnp.maximum(i - 1, 0)))


def _norm_mod_bwd(dh, x, g, scale, dres, gidx, n_groups, name):
    T, Dm = x.shape

    def body(dh_ref, x_ref, g_ref, sc_ref, dres_ref, dx_ref, dsh_ref, dsc_ref, dg_ref):
        i = pl.program_id(0)
        xv, dhv = x_ref[...], dh_ref[...]
        r = lax.rsqrt(jnp.mean(xv * xv, axis=-1, keepdims=True) + EPS)
        xn = xv * r
        y = xn * g_ref[...]

        @pl.when(_first_of_group(i, gidx))
        def _():
            dsh_ref[...] = jnp.zeros_like(dsh_ref)
            dsc_ref[...] = jnp.zeros_like(dsc_ref)

        @pl.when(i == 0)
        def _():
            dg_ref[...] = jnp.zeros_like(dg_ref)

        dsh_ref[0] += jnp.sum(dhv, axis=0, keepdims=True)
        dsc_ref[0] += jnp.sum(dhv * y, axis=0, keepdims=True)
        dy = dhv * (1.0 + sc_ref[0])
        dg_ref[...] += jnp.sum(dy * xn, axis=0, keepdims=True)
        dxn = dy * g_ref[...]
        dx = r * (dxn - xn * jnp.mean(dxn * xn, axis=-1, keepdims=True))
        dx_ref[...] = dres_ref[...] + dx

    row = pl.BlockSpec((ROW_TILE, Dm), lambda i: (i, 0))
    mod = pl.BlockSpec((1, 1, Dm), lambda i: (gidx(i), 0, 0))
    vec = pl.BlockSpec((1, Dm), lambda i: (0, 0))
    return pl.pallas_call(
        body, name=name, grid=(T // ROW_TILE,),
        in_specs=[row, row, vec, mod, row],
        out_specs=(row, mod, mod, vec),
        out_shape=(jax.ShapeDtypeStruct((T, Dm), F32), jax.ShapeDtypeStruct((n_groups, 1, Dm), F32),
                   jax.ShapeDtypeStruct((n_groups, 1, Dm), F32), jax.ShapeDtypeStruct((1, Dm), F32)),
        compiler_params=_params())(dh, x, g, scale, dres)


def _gate_bwd(dy, f, gate, gidx, n_groups, name):
    T, Dm = dy.shape

    def body(dy_ref, f_ref, gate_ref, dz_ref, dgate_ref):
        i = pl.program_id(0)
        dyv = dy_ref[...]

        @pl.when(_first_of_group(i, gidx))
        def _():
            dgate_ref[...] = jnp.zeros_like(dgate_ref)

        dgate_ref[0] += jnp.sum(dyv * f_ref[...], axis=0, keepdims=True)
        dz_ref[...] = (dyv * gate_ref[0]).astype(dz_ref.dtype)

    row = pl.BlockSpec((ROW_TILE, Dm), lambda i: (i, 0))
    mod = pl.BlockSpec((1, 1, Dm), lambda i: (gidx(i), 0, 0))
    return pl.pallas_call(
        body, name=name, grid=(T // ROW_TILE,), in_specs=[row, row, mod], out_specs=(row, mod),
        out_shape=(jax.ShapeDtypeStruct((T, Dm), MXU_DTYPE), jax.ShapeDtypeStruct((n_groups, 1, Dm), F32)),
        compiler_params=_params())(dy, f, gate)


SWIGLU_ROWS = 256


def _swiglu_fwd(u, name):
    T = u.shape[0]

    def body(u_ref, a_ref):
        gate, up = u_ref[:, :D_FF], u_ref[:, D_FF:]
        a_ref[...] = (gate * jax.nn.sigmoid(gate) * up).astype(a_ref.dtype)

    return pl.pallas_call(
        body, name=name, grid=(T // SWIGLU_ROWS,),
        in_specs=[pl.BlockSpec((SWIGLU_ROWS, 2 * D_FF), lambda i: (i, 0))],
        out_specs=pl.BlockSpec((SWIGLU_ROWS, D_FF), lambda i: (i, 0)),
        out_shape=jax.ShapeDtypeStruct((T, D_FF), MXU_DTYPE), compiler_params=_params())(u)


def _swiglu_bwd(da, u, name):
    T = u.shape[0]

    def body(da_ref, u_ref, du_ref):
        gate, up, dav = u_ref[:, :D_FF], u_ref[:, D_FF:], da_ref[...]
        sg = jax.nn.sigmoid(gate)
        du_ref[:, :D_FF] = (dav * up * (sg * (1.0 + gate * (1.0 - sg)))).astype(du_ref.dtype)
        du_ref[:, D_FF:] = (dav * gate * sg).astype(du_ref.dtype)

    return pl.pallas_call(
        body, name=name, grid=(T // SWIGLU_ROWS,),
        in_specs=[pl.BlockSpec((SWIGLU_ROWS, D_FF), lambda i: (i, 0)),
                  pl.BlockSpec((SWIGLU_ROWS, 2 * D_FF), lambda i: (i, 0))],
        out_specs=pl.BlockSpec((SWIGLU_ROWS, 2 * D_FF), lambda i: (i, 0)),
        out_shape=jax.ShapeDtypeStruct((T, 2 * D_FF), MXU_DTYPE), compiler_params=_params())(da, u)


def _loss_fwd_bwd(y, target, name):
    T, Dm = y.shape

    def body(y_ref, t_ref, loss_ref, dy_ref):
        err = y_ref[...] - t_ref[...]

        @pl.when(pl.program_id(0) == 0)
        def _():
            loss_ref[...] = jnp.zeros_like(loss_ref)

        loss_ref[...] += 0.5 * jnp.sum(jnp.mean(err * err, axis=-1, keepdims=True))
        dy_ref[...] = err * (1.0 / Dm)

    row = pl.BlockSpec((ROW_TILE, Dm), lambda i: (i, 0))
    return pl.pallas_call(
        body, name=name, grid=(T // ROW_TILE,), in_specs=[row, row],
        out_specs=(pl.BlockSpec((8, LANES), lambda i: (0, 0)), row),
        out_shape=(jax.ShapeDtypeStruct((8, LANES), F32), jax.ShapeDtypeStruct((T, Dm), F32)),
        compiler_params=_params())(y, target)


def _rope_tables(seq, head_dim):
    axis_dim = head_dim // 2
    half = axis_dim // 2
    pos = jnp.arange(seq, dtype=jnp.int32)
    row = (pos // GRID_W).astype(F32)[:, None]
    col = (pos % GRID_W).astype(F32)[:, None]
    inv = ROPE_BASE ** (-jnp.arange(0, axis_dim, 2, dtype=F32) / axis_dim)
    lane = jnp.arange(head_dim, dtype=jnp.int32)
    within = lane % axis_dim
    ang = jnp.where((lane // axis_dim == 0)[None, :], row, col) * inv[within % half][None, :]
    cos = jnp.cos(ang)
    sin = jnp.where((within < half)[None, :], -jnp.sin(ang), jnp.sin(ang))
    cos = jnp.concatenate([cos, jnp.ones((ROW_TILE, head_dim), F32)], axis=0)
    sin = jnp.concatenate([sin, jnp.zeros((ROW_TILE, head_dim), F32)], axis=0)
    return cos, sin


def _pair_swap(v, half):
    if 2 * half == LANES:
        return pltpu.roll(v, half, axis=1)
    lane = lax.broadcasted_iota(jnp.int32, v.shape, 1)
    return jnp.where((lane % (2 * half)) < half, pltpu.roll(v, LANES - half, axis=1), pltpu.roll(v, half, axis=1))


def _head_sum(v, ones_ref):
    hi = v.astype(MXU_DTYPE)
    lo = (v - hi.astype(F32)).astype(MXU_DTYPE)
    return (jnp.dot(hi, ones_ref[...], preferred_element_type=F32)
            + jnp.dot(lo, ones_ref[...], preferred_element_type=F32))


def _head_ones():
    lane = jnp.arange(LANES)
    return (lane[:, None] // HEAD_DIM == lane[None, :] // HEAD_DIM).astype(MXU_DTYPE)


ATTN_QK_BLOCKS = (N_HEADS + N_KV_HEADS) * HEAD_DIM // LANES
ATTN_ALL_BLOCKS = (N_HEADS + 2 * N_KV_HEADS) * HEAD_DIM // LANES
ATTN_Q_BLOCKS = N_HEADS * HEAD_DIM // LANES
ATTN_SCALE = HEAD_DIM ** -0.5


def _attn_prep_fwd(qkv, gains, cos, sin, tidx, name):
    T, W = qkv.shape

    def body(x_ref, g_ref, cos_ref, sin_ref, ones_ref, o_ref):
        for cb in range(ATTN_ALL_BLOCKS):
            cols = slice(cb * LANES, (cb + 1) * LANES)
            xv = x_ref[:, cols]
            if cb < ATTN_QK_BLOCKS:
                r = lax.rsqrt(_head_sum(xv * xv, ones_ref) * (1.0 / HEAD_DIM) + EPS)
                y = xv * r * g_ref[0 if cb < ATTN_Q_BLOCKS else 1]
                xv = y * cos_ref[...] + _pair_swap(y, HEAD_DIM // 4) * sin_ref[...]
                if cb < ATTN_Q_BLOCKS:
                    xv = xv * ATTN_SCALE
            o_ref[:, cols] = xv.astype(o_ref.dtype)

    row = pl.BlockSpec((ROW_TILE, W), lambda i: (i, 0))
    tab = pl.BlockSpec((ROW_TILE, LANES), lambda i: (tidx(i), 0))
    return pl.pallas_call(
        body, name=name, grid=(T // ROW_TILE,),
        in_specs=[row, pl.BlockSpec((2, 1, LANES), lambda i: (0, 0, 0)), tab, tab,
                  pl.BlockSpec((LANES, LANES), lambda i: (0, 0))],
        out_specs=row, out_shape=jax.ShapeDtypeStruct(qkv.shape, MXU_DTYPE),
        compiler_params=_params())(qkv, gains, cos, sin, _head_ones())


def _attn_prep_bwd(dqk, dv, qkv, gains, cos, sin, tidx, name):
    T, W = qkv.shape
    qk_w = ATTN_QK_BLOCKS * LANES

    def body(dqk_ref, dv_ref, x_ref, g_ref, cos_ref, sin_ref, ones_ref, o_ref, dg_ref):
        @pl.when(pl.program_id(0) == 0)
        def _():
            dg_ref[...] = jnp.zeros_like(dg_ref)

        for cb in range(ATTN_QK_BLOCKS):
            cols = slice(cb * LANES, (cb + 1) * LANES)
            xv, d = x_ref[:, cols], dqk_ref[:, cols]
            if cb < ATTN_Q_BLOCKS:
                d = d * ATTN_SCALE
            r = lax.rsqrt(_head_sum(xv * xv, ones_ref) * (1.0 / HEAD_DIM) + EPS)
            xn = xv * r
            dy = d * cos_ref[...] + _pair_swap(d * sin_ref[...], HEAD_DIM // 4)
            dg_ref[:, cols] += jnp.sum(dy * xn, axis=0, keepdims=True)
            dxn = dy * g_ref[0 if cb < ATTN_Q_BLOCKS else 1]
            dx = r * (dxn - xn * (_head_sum(dxn * xn, ones_ref) * (1.0 / HEAD_DIM)))
            o_ref[:, cols] = dx.astype(o_ref.dtype)
        o_ref[:, qk_w:] = dv_ref[...].astype(o_ref.dtype)

    row = lambda w: pl.BlockSpec((ROW_TILE, w), lambda i: (i, 0))
    tab = pl.BlockSpec((ROW_TILE, LANES), lambda i: (tidx(i), 0))
    return pl.pallas_call(
        body, name=name, grid=(T // ROW_TILE,),
        in_specs=[row(qk_w), row(W - qk_w), row(W), pl.BlockSpec((2, 1, LANES), lambda i: (0, 0, 0)), tab, tab,
                  pl.BlockSpec((LANES, LANES), lambda i: (0, 0))],
        out_specs=(row(W), pl.BlockSpec((1, qk_w), lambda i: (0, 0))),
        out_shape=(jax.ShapeDtypeStruct(qkv.shape, MXU_DTYPE), jax.ShapeDtypeStruct((1, qk_w), F32)),
        compiler_params=_params())(dqk, dv, qkv, gains, cos, sin, _head_ones())


RET_QK_BLOCKS = 2 * RET_HEADS * RET_QK_DIM // LANES


def _ret_rope(x, cos, sin, tidx, name):
    T = x.shape[0]
    W = RET_QK_BLOCKS * LANES
    k_scale = RET_QK_DIM ** -0.5

    def body(x_ref, cos_ref, sin_ref, o_ref):
        for cb in range(RET_QK_BLOCKS):
            cols = slice(cb * LANES, (cb + 1) * LANES)
            tcols = slice((cb % 2) * LANES, (cb % 2 + 1) * LANES)
            xv = x_ref[:, cols]
            out = xv * cos_ref[:, tcols] + pltpu.roll(xv, LANES // 2, axis=1) * sin_ref[:, tcols]
            if cb >= RET_QK_BLOCKS // 2:
                out = out * k_scale
            o_ref[:, cols] = out

    row = pl.BlockSpec((ROW_TILE, W), lambda i: (i, 0))
    tab = pl.BlockSpec((ROW_TILE, RET_QK_DIM), lambda i: (tidx(i), 0))
    return pl.pallas_call(
        body, name=name, grid=(T // ROW_TILE,), in_specs=[row, tab, tab], out_specs=row,
        out_shape=jax.ShapeDtypeStruct((T, W), F32), compiler_params=_params())(x, cos, sin)


ASSEMBLE_ROWS = 256


def _ret_grad_assemble(x_parts, c_parts, dg, cos, sin, seq, name):
    NX, NC = x_parts[0].shape[0], c_parts[0].shape[0]
    T = NX + NC
    rt = ASSEMBLE_ROWS
    nxt = NX // rt
    qk_w = RET_HEADS * RET_QK_DIM
    k_scale = RET_QK_DIM ** -0.5

    def unrotate(d, cos_ref, sin_ref, scale):
        outs = []
        for cb in range(qk_w // LANES):
            cols = slice(cb * LANES, (cb + 1) * LANES)
            tcols = slice((cb % 2) * LANES, (cb % 2 + 1) * LANES)
            dv_ = d[:, cols]
            o = dv_ * cos_ref[:, tcols] + pltpu.roll(dv_ * sin_ref[:, tcols], LANES // 2, axis=1)
            outs.append(o * scale if scale != 1.0 else o)
        return outs

    def body(dqf, dqb, dkf, dkb, dvf, dvb, dg_ref, dkcf, dkcb, dvcf, dvcb, cos_ref, sin_ref, o_ref):
        i = pl.program_id(0)

        def write_k(parts):
            for cb, o in enumerate(parts):
                o_ref[:, qk_w + cb * LANES:qk_w + (cb + 1) * LANES] = o.astype(o_ref.dtype)

        @pl.when(i < nxt)
        def _():
            for cb, o in enumerate(unrotate(dqf[...] + dqb[...], cos_ref, sin_ref, 1.0)):
                o_ref[:, cb * LANES:(cb + 1) * LANES] = o.astype(o_ref.dtype)
            write_k(unrotate(dkf[...] + dkb[...], cos_ref, sin_ref, k_scale))
            o_ref[:, 2 * qk_w:2 * qk_w + RET_VWIDTH] = (dvf[...] + dvb[...]).astype(o_ref.dtype)
            o_ref[:, 2 * qk_w + RET_VWIDTH:] = dg_ref[...].astype(o_ref.dtype)

        @pl.when(i >= nxt)
        def _():
            o_ref[:, :qk_w] = jnp.zeros((rt, qk_w), o_ref.dtype)
            write_k(unrotate(dkcf[...] + dkcb[...], cos_ref, sin_ref, k_scale))
            o_ref[:, 2 * qk_w:2 * qk_w + RET_VWIDTH] = (dvcf[...] + dvcb[...]).astype(o_ref.dtype)
            o_ref[:, 2 * qk_w + RET_VWIDTH:] = jnp.zeros((rt, RET_VWIDTH), o_ref.dtype)

    xs = lambda w: pl.BlockSpec((rt, w), lambda i: (jnp.minimum(i, nxt - 1), 0))
    cs = lambda w: pl.BlockSpec((rt, w), lambda i: (jnp.maximum(i - nxt, 0), 0))
    tab = pl.BlockSpec((rt, RET_QK_DIM), lambda i: (jnp.where(i < nxt, i % (seq // rt), seq // rt), 0))
    return pl.pallas_call(
        body, name=name, grid=(T // rt,),
        in_specs=[xs(qk_w)] * 4 + [xs(RET_VWIDTH)] * 3 + [cs(qk_w)] * 2 + [cs(RET_VWIDTH)] * 2 + [tab, tab],
        out_specs=pl.BlockSpec((rt, 2 * qk_w + 2 * RET_VWIDTH), lambda i: (i, 0)),
        out_shape=jax.ShapeDtypeStruct((T, 2 * qk_w + 2 * RET_VWIDTH), MXU_DTYPE),
        compiler_params=_params())(*x_parts, dg, *c_parts, cos, sin)


def _band_bias(qb, seq):
    nb = seq // qb
    assert nb >= 2
    i = jnp.arange(GQA_GROUP * qb, dtype=jnp.int32)[:, None] % qb
    n = jnp.arange(3 * qb, dtype=jnp.int32)[None, :]
    in_window = (n >= i) & (n - i <= 2 * WINDOW)
    variants = [in_window & (n >= qb), in_window, in_window & (n < 2 * qb)]
    return jnp.stack([jnp.where(v, 0.0, NEG_INF).astype(F32) for v in variants])


GROUP_ORDER = (0, 2, 1, 3)


def _stack_halves(blk):
    return jnp.concatenate([blk[:, :LANES], blk[:, LANES:]], axis=0)


def _unstack_halves(v, rows):
    return jnp.concatenate([v[:rows], v[rows:]], axis=1)


def _align_head(pair, odd):
    lane = lax.broadcasted_iota(jnp.int32, pair.shape, 1)
    mine = jnp.where((lane >= HEAD_DIM) == odd, pair, jnp.zeros_like(pair))
    rolled = pltpu.roll(mine, HEAD_DIM, axis=1)
    return jnp.where(odd, rolled, mine), jnp.where(odd, mine, rolled)


def _scores(q2, x_eo):
    return jnp.concatenate([_dot(q2, x_eo[0], _NT), _dot(q2, x_eo[1], _NT)], axis=0)


def _apply(p, x_eo):
    half = p.shape[0] // 2
    return _dot(p[:half], x_eo[0], _NN) + _dot(p[half:], x_eo[1], _NN)


def _kv_grad(a, q2, odd):
    half = a.shape[0] // 2
    even_part = _dot(a[:half], q2, _TN)
    odd_part = _dot(a[half:], q2, _TN)
    lane = lax.broadcasted_iota(jnp.int32, even_part.shape, 1)
    low = (jnp.where(lane < HEAD_DIM, even_part, 0.0)
           + pltpu.roll(jnp.where(lane >= HEAD_DIM, odd_part, 0.0), HEAD_DIM, axis=1))
    return jnp.where(odd, pltpu.roll(low, HEAD_DIM, axis=1), low)


def _attn_probs(q2, kc_eo, kl_eo, bias, sink_ref, kv_head, qb):
    rows = GQA_GROUP * qb
    s_c = _scores(q2, kc_eo)
    blk = lax.broadcasted_iota(jnp.int32, (rows, 1), 0) // qb
    sink = jnp.zeros((rows, 1), F32)
    for t, gi in enumerate(GROUP_ORDER):
        sink = jnp.where(blk == t, sink_ref[kv_head, gi], sink)
    m = jnp.maximum(jnp.max(s_c, axis=-1, keepdims=True), sink)
    s_l = None
    if kl_eo is not None:
        s_l = _scores(q2, kl_eo) + bias
        m = jnp.maximum(m, jnp.max(s_l, axis=-1, keepdims=True))
    e_c = jnp.exp(s_c - m)
    e_s = jnp.exp(sink - m)
    den = jnp.sum(e_c, axis=-1, keepdims=True) + e_s
    e_l = None
    if kl_eo is not None:
        e_l = jnp.exp(s_l - m)
        den = den + jnp.sum(e_l, axis=-1, keepdims=True)
    inv = 1.0 / den
    return e_c * inv, (None if e_l is None else e_l * inv), e_s * inv


GROUP_W = GQA_GROUP * HEAD_DIM
K_LANE_BLOCK = N_HEADS * HEAD_DIM // LANES
V_LANE_BLOCK = K_LANE_BLOCK + N_KV_HEADS * HEAD_DIM // LANES


def _attn_specs(B, seq, ctx_len, ctx_queries):
    ctx0 = B * seq // ctx_len
    if ctx_queries:
        qb, nb = ctx_len, 1
        qrow = lambda b, j: ctx0 + b
    else:
        qb, nb = ATTN_BLOCK, seq // ATTN_BLOCK
        qrow = lambda b, j: b * nb + j
    q_spec = pl.BlockSpec((qb, GROUP_W), lambda b, k, j: (qrow(b, j), k))
    c_specs = [pl.BlockSpec((ctx_len, LANES), lambda b, k, j: (ctx0 + b, K_LANE_BLOCK + k // 2)),
               pl.BlockSpec((ctx_len, LANES), lambda b, k, j: (ctx0 + b, V_LANE_BLOCK + k // 2))]
    local = []
    if not ctx_queries:
        near = [lambda j: jnp.maximum(j - 1, 0), lambda j: j, lambda j: jnp.minimum(j + 1, nb - 1)]
        for lane0 in (K_LANE_BLOCK, V_LANE_BLOCK):
            for f in near:
                local.append(pl.BlockSpec((qb, LANES), lambda b, k, j, f=f, lane0=lane0: (b * nb + f(j), lane0 + k // 2)))
        local.append(pl.BlockSpec(
            (1, GQA_GROUP * qb, 3 * qb), lambda b, k, j: (jnp.where(j == 0, 0, jnp.where(j == nb - 1, 2, 1)), 0, 0)))
    return qb, nb, qrow, q_spec, c_specs, local


def _attn_operands(refs, has_local, kv_head):
    odd = (kv_head % 2) == 1
    n_local = 7 if has_local else 0
    q2 = _stack_halves(refs[0][...])
    kc = _align_head(refs[1 + n_local][...], odd)
    vc = _align_head(refs[2 + n_local][...], odd)
    kl = vl = bias = None
    if has_local:
        kl = _align_head(jnp.concatenate([r[...] for r in refs[1:4]], axis=0), odd)
        vl = _align_head(jnp.concatenate([r[...] for r in refs[4:7]], axis=0), odd)
        bias = refs[7][0]
    return odd, q2, kc, vc, kl, vl, bias


def _attn_fwd(qkv, sink, B, seq, ctx_len, ctx_queries, name):
    has_local = not ctx_queries
    qb, nb, _, q_spec, c_specs, local = _attn_specs(B, seq, ctx_len, ctx_queries)
    n_rows = B * (ctx_len if ctx_queries else seq)

    def body(*refs):
        sink_ref, o_ref = refs[-2:]
        kv_head = pl.program_id(1)
        _, q2, kc, vc, kl, vl, bias = _attn_operands(refs, has_local, kv_head)
        p_c, p_l, _ = _attn_probs(q2, kc, kl, bias, sink_ref, kv_head, qb)
        o2 = _apply(p_c, vc)
        if has_local:
            o2 = o2 + _apply(p_l, vl)
        o_ref[...] = _unstack_halves(o2, qb).astype(o_ref.dtype)

    operands = [qkv] + ([qkv] * 6 + [_band_bias(qb, seq)] if has_local else []) + [qkv, qkv, sink]
    return pl.pallas_call(
        body, name=name, grid=(B, N_KV_HEADS, nb),
        in_specs=[q_spec] + local + c_specs + [_SMEM],
        out_specs=pl.BlockSpec((qb, GROUP_W), lambda b, k, j: (b * nb + j, k)),
        out_shape=jax.ShapeDtypeStruct((n_rows, N_HEADS * HEAD_DIM), MXU_DTYPE), compiler_params=_params())(*operands)


def _attn_bwd(qkv, sink, do, B, seq, ctx_len, ctx_queries, name):
    has_local = not ctx_queries
    qb, nb, qrow, q_spec, c_specs, local = _attn_specs(B, seq, ctx_len, ctx_queries)
    n_rows = B * (ctx_len if ctx_queries else seq)

    def body(*refs):
        n_in = 1 + (7 if has_local else 0) + 4
        sink_ref, do_ref = refs[n_in - 2:n_in]
        outs = refs[n_in:]
        dq_ref = outs[0]
        dkc_ref, dvc_ref, dsink_ref = outs[-3:]
        b, kv_head, j = pl.program_id(0), pl.program_id(1), pl.program_id(2)
        odd, q2, kc, vc, kl, vl, bias = _attn_operands(refs, has_local, kv_head)
        do2 = _stack_halves(do_ref[...])
        p_c, p_l, p_s = _attn_probs(q2, kc, kl, bias, sink_ref, kv_head, qb)
        dp_c = _scores(do2, vc)
        delta = jnp.sum(p_c * dp_c, axis=-1, keepdims=True)
        if has_local:
            dp_l = _scores(do2, vl)
            delta = delta + jnp.sum(p_l * dp_l, axis=-1, keepdims=True)
        ds_c = p_c * (dp_c - delta)
        dq2 = _apply(ds_c, kc)

        @pl.when((kv_head % 2 == 0) & (j == 0))
        def _():
            dkc_ref[...] = jnp.zeros_like(dkc_ref)
            dvc_ref[...] = jnp.zeros_like(dvc_ref)
            if has_local:
                outs[1][...] = jnp.zeros_like(outs[1])
                outs[2][...] = jnp.zeros_like(outs[2])

        @pl.when((b == 0) & (kv_head == 0) & (j == 0))
        def _():
            dsink_ref[...] = jnp.zeros_like(dsink_ref)

        dkc_ref[...] += _kv_grad(ds_c, q2, odd)
        dvc_ref[...] += _kv_grad(p_c, do2, odd)
        if has_local:
            ds_l = p_l * (dp_l - delta)
            dq2 = dq2 + _apply(ds_l, kl)
            dkl = _kv_grad(ds_l, q2, odd)
            dvl = _kv_grad(p_l, do2, odd)
            dk_ref, dv_ref = outs[1], outs[2]
            for t in range(3):
                def add(t=t):
                    start = pl.multiple_of((j - 1 + t) * qb, qb)
                    dk_ref[pl.ds(start, qb), :] += dkl[t * qb:(t + 1) * qb]
                    dv_ref[pl.ds(start, qb), :] += dvl[t * qb:(t + 1) * qb]
                if t == 0:
                    pl.when(j > 0)(add)
                elif t == 2:
                    pl.when(j < nb - 1)(add)
                else:
                    add()
        dq_ref[...] = _unstack_halves(dq2, qb)
        dsk = -(p_s * delta)
        sub = lax.broadcasted_iota(jnp.int32, (8, LANES), 0)
        tile = jnp.zeros((8, LANES), F32)
        for t, gi in enumerate(GROUP_ORDER):
            tile = jnp.where(sub == gi, jnp.sum(dsk[t * qb:(t + 1) * qb]), tile)
        dsink_ref[pl.ds(pl.multiple_of(kv_head * 8, 8), 8), :] += tile

    kv_w = N_KV_HEADS * HEAD_DIM
    seq_spec = pl.BlockSpec((seq, LANES), lambda b, k, j: (b, k // 2))
    ctx_spec = pl.BlockSpec((ctx_len, LANES), lambda b, k, j: (b, k // 2))
    do_spec = pl.BlockSpec((qb, GROUP_W), lambda b, k, j: (qrow(b, j), k))
    operands = [qkv] + ([qkv] * 6 + [_band_bias(qb, seq)] if has_local else []) + [qkv, qkv, sink, do]
    out_specs = ([pl.BlockSpec((qb, GROUP_W), lambda b, k, j: (b * nb + j, k))] + ([seq_spec, seq_spec] if has_local else [])
                 + [ctx_spec, ctx_spec, pl.BlockSpec((32, LANES), lambda b, k, j: (0, 0))])
    out_shape = ([jax.ShapeDtypeStruct((n_rows, N_HEADS * HEAD_DIM), F32)]
                 + ([jax.ShapeDtypeStruct((B * seq, kv_w), F32)] * 2 if has_local else [])
                 + [jax.ShapeDtypeStruct((B * ctx_len, kv_w), F32)] * 2 + [jax.ShapeDtypeStruct((32, LANES), F32)])
    return pl.pallas_call(
        body, name=name, grid=(B, N_KV_HEADS, nb),
        in_specs=[q_spec] + local + c_specs + [_SMEM, do_spec],
        out_specs=tuple(out_specs), out_shape=tuple(out_shape), compiler_params=_params())(*operands)


def _ret_decays(lg, rev):
    n = lax.broadcasted_iota(jnp.int32, (RET_CHUNK, RET_CHUNK), 0).astype(F32)
    m = lax.broadcasted_iota(jnp.int32, (RET_CHUNK, RET_CHUNK), 1).astype(F32)
    pos = lax.broadcasted_iota(jnp.int32, (RET_CHUNK, 1), 0).astype(F32)
    diff = (m - n) if rev else (n - m)
    a_exp = jnp.maximum(diff, 0.0)
    intra = jnp.where(diff >= 0, jnp.exp(lg * a_exp), 0.0)
    q_exp = (RET_CHUNK - pos) if rev else (pos + 1.0)
    k_exp = pos if rev else (RET_CHUNK - 1.0 - pos)
    chunk = jnp.exp(jnp.full((1, 1), RET_CHUNK, F32) * lg)
    return intra, a_exp, jnp.exp(lg * q_exp), q_exp, jnp.exp(lg * k_exp), k_exp, chunk


def _ctx_decay(lg, ctx_len, rev):
    t = lax.broadcasted_iota(jnp.int32, (ctx_len, 1), 0).astype(F32)
    expo = t if rev else (ctx_len - 1.0 - t)
    return jnp.exp(lg * expo), expo


def _ret_specs(B, seq, ctx_len, order):
    nc = seq // RET_CHUNK
    x_blocks = B * seq // ctx_len

    def rows(b, c):
        return b * nc + order(c, nc)

    q_spec = pl.BlockSpec((RET_CHUNK, RET_QK_DIM), lambda b, h, c: (rows(b, c), h))
    k_spec = pl.BlockSpec((RET_CHUNK, RET_QK_DIM), lambda b, h, c: (rows(b, c), RET_HEADS + h))
    v_spec = pl.BlockSpec((RET_CHUNK, RET_V_DIM), lambda b, h, c: (rows(b, c), RET_HEADS + h))
    kc_spec = pl.BlockSpec((ctx_len, RET_QK_DIM), lambda b, h, c: (x_blocks + b, RET_HEADS + h))
    vc_spec = pl.BlockSpec((ctx_len, RET_V_DIM), lambda b, h, c: (x_blocks + b, RET_HEADS + h))
    st_spec = pl.BlockSpec((1, 1, 1, RET_QK_DIM, RET_V_DIM), lambda b, h, c: (b, h, order(c, nc), 0, 0))
    o_spec = pl.BlockSpec((RET_CHUNK, RET_V_DIM), lambda b, h, c: (rows(b, c), h))
    return nc, q_spec, k_spec, v_spec, kc_spec, vc_spec, st_spec, o_spec


_SCAN_UP = lambda c, nc: c
_SCAN_DOWN = lambda c, nc: nc - 1 - c


def _ret_fwd(qk, qkvg, log_g, B, seq, ctx_len, name):
    nc, qf, kf, vf, kc_spec, vc_spec, stf, of = _ret_specs(B, seq, ctx_len, _SCAN_UP)
    _, qr, kr, vr, _, _, str_, or_ = _ret_specs(B, seq, ctx_len, _SCAN_DOWN)

    def body(lg_ref, qf_ref, kf_ref, vf_ref, qr_ref, kr_ref, vr_ref, kc_ref, vc_ref,
             of_ref, stf_ref, or_ref, str_ref, state_f, state_r):
        h, c = pl.program_id(1), pl.program_id(2)
        dirs = ((False, lg_ref[0, h], qf_ref, kf_ref, vf_ref, of_ref, stf_ref, state_f),
                (True, lg_ref[1, h], qr_ref, kr_ref, vr_ref, or_ref, str_ref, state_r))

        @pl.when(c == 0)
        def _():
            for rev, lg, _, _, _, _, _, state in dirs:
                dec, _ = _ctx_decay(lg, ctx_len, rev)
                state[...] = _dot(kc_ref[...] * dec, vc_ref[...], _TN)

        for rev, lg, q_ref, k_ref, v_ref, o_ref, st_ref, state in dirs:
            intra, _, q_dec, _, k_dec, _, chunk_dec = _ret_decays(lg, rev)
            qv, kv, vv = q_ref[...], k_ref[...], v_ref[...]
            s_in = state[...]
            st_ref[0, 0, 0] = s_in
            w = _dot(qv, kv, _NT) * intra
            o_ref[...] = _dot(w, vv, _NN) + _dot(qv, s_in, _NN) * q_dec
            state[...] = s_in * chunk_dec + _dot(kv * k_dec, vv, _TN)

    o_shape = jax.ShapeDtypeStruct((B * seq, RET_VWIDTH), F32)
    st_shape = jax.ShapeDtypeStruct((B, RET_HEADS, nc, RET_QK_DIM, RET_V_DIM), F32)
    return pl.pallas_call(
        body, name=name, grid=(B, RET_HEADS, nc),
        in_specs=[_SMEM, qf, kf, vf, qr, kr, vr, kc_spec, vc_spec],
        out_specs=(of, stf, or_, str_), out_shape=(o_shape, st_shape, o_shape, st_shape),
        scratch_shapes=[pltpu.VMEM((RET_QK_DIM, RET_V_DIM), F32)] * 2,
        compiler_params=_params())(log_g, qk, qk, qkvg, qk, qk, qkvg, qk, qkvg)


def _ret_bwd_chunk(rev, lg, q_ref, k_ref, v_ref, st_ref, do_ref, dq_ref, dk_ref, dv_ref, dlg_ref, dstate):
    intra, a_exp, q_dec, q_exp, k_dec, k_exp, chunk_dec = _ret_decays(lg, rev)
    qv, kv, vv, dov = q_ref[...], k_ref[...], v_ref[...], do_ref[...]
    s_in, ds_out = st_ref[0, 0, 0], dstate[...]
    p = _dot(qv, kv, _NT)
    w = p * intra
    dw = _dot(dov, vv, _NT)
    dp = dw * intra
    do_dec = dov * q_dec
    kd = kv * k_dec
    v_ds = _dot(vv, ds_out, _NT)
    dq_ref[...] = _dot(dp, kv, _NN) + _dot(do_dec, s_in, _NT)
    dk_ref[...] = _dot(dp, qv, _TN) + v_ds * k_dec
    dv_ref[...] = _dot(w, dov, _TN) + _dot(kd, ds_out, _NN)
    q_s = _dot(qv, s_in, _NN)
    dlg = (jnp.sum(dw * w * a_exp)
           + jnp.sum(q_exp * q_dec * jnp.sum(dov * q_s, axis=-1, keepdims=True))
           + jnp.sum(k_exp * k_dec * jnp.sum(kv * v_ds, axis=-1, keepdims=True))
           + RET_CHUNK * jnp.sum(chunk_dec * (ds_out * s_in)))
    ds_in = ds_out * chunk_dec + _dot(qv, do_dec, _TN)
    dstate[...] = ds_in
    dlg_ref[...] += dlg
    return ds_in


def _ret_bwd(qk, qkvg, log_g, st_f, st_r, do, B, seq, ctx_len, name):
    nc, qf, kf, vf, kc_spec, vc_spec, stf, of = _ret_specs(B, seq, ctx_len, _SCAN_DOWN)
    _, qr, kr, vr, _, _, str_, or_ = _ret_specs(B, seq, ctx_len, _SCAN_UP)

    def body(lg_ref, qf_ref, kf_ref, vf_ref, stf_ref, dof_ref, qr_ref, kr_ref, vr_ref, str_ref, dor_ref, kc_ref, vc_ref,
             dqf, dkf, dvf, dkcf, dvcf, dlgf, dqr, dkr, dvr, dkcr, dvcr, dlgr, dstate_f, dstate_r):
        h, c = pl.program_id(1), pl.program_id(2)
        dirs = ((False, lg_ref[0, h], (qf_ref, kf_ref, vf_ref, stf_ref, dof_ref, dqf, dkf, dvf, dlgf, dstate_f), dkcf, dvcf),
                (True, lg_ref[1, h], (qr_ref, kr_ref, vr_ref, str_ref, dor_ref, dqr, dkr, dvr, dlgr, dstate_r), dkcr, dvcr))

        @pl.when(c == 0)
        def _():
            for _, _, refs, _, _ in dirs:
                refs[-1][...] = jnp.zeros_like(refs[-1])
                refs[-2][...] = jnp.zeros_like(refs[-2])

        ds_first = [_ret_bwd_chunk(rev, lg, *refs) for rev, lg, refs, _, _ in dirs]

        @pl.when(c == nc - 1)
        def _():
            for (rev, lg, refs, dkc_ref, dvc_ref), ds_in in zip(dirs, ds_first):
                dec, expo = _ctx_decay(lg, ctx_len, rev)
                kcv, vcv = kc_ref[...], vc_ref[...]
                vc_ds = _dot(vcv, ds_in, _NT)
                dkc_ref[...] = vc_ds * dec
                dvc_ref[...] = _dot(kcv * dec, ds_in, _NN)
                refs[-2][...] += jnp.sum(expo * dec * jnp.sum(kcv * vc_ds, axis=-1, keepdims=True))

    def outs(q_spec, o_spec):
        return (pl.BlockSpec((RET_CHUNK, RET_QK_DIM), q_spec.index_map),
                pl.BlockSpec((RET_CHUNK, RET_QK_DIM), q_spec.index_map), o_spec,
                pl.BlockSpec((ctx_len, RET_QK_DIM), lambda b, h, c: (b, h)),
                pl.BlockSpec((ctx_len, RET_V_DIM), lambda b, h, c: (b, h)),
                pl.BlockSpec((1, 1, 8, LANES), lambda b, h, c: (b, h, 0, 0)))

    shapes = (jax.ShapeDtypeStruct((B * seq, RET_HEADS * RET_QK_DIM), F32),
              jax.ShapeDtypeStruct((B * seq, RET_HEADS * RET_QK_DIM), F32),
              jax.ShapeDtypeStruct((B * seq, RET_VWIDTH), F32),
              jax.ShapeDtypeStruct((B * ctx_len, RET_HEADS * RET_QK_DIM), F32),
              jax.ShapeDtypeStruct((B * ctx_len, RET_VWIDTH), F32),
              jax.ShapeDtypeStruct((B, RET_HEADS, 8, LANES), F32))
    res = pl.pallas_call(
        body, name=name, grid=(B, RET_HEADS, nc),
        in_specs=[_SMEM, qf, kf, vf, stf, of, qr, kr, vr, str_, or_, kc_spec, vc_spec],
        out_specs=outs(qf, of) + outs(qr, or_), out_shape=shapes + shapes,
        scratch_shapes=[pltpu.VMEM((RET_QK_DIM, RET_V_DIM), F32)] * 2,
        compiler_params=_params())(log_g, qk, qk, qkvg, st_f, do, qk, qk, qkvg, st_r, do, qk, qkvg)
    return res[:6], res[6:]


def _gated_out_fwd(o_f, o_b, qkvg, gn_gain, name):
    T = o_f.shape[0]
    g_off = (2 * RET_HEADS * RET_QK_DIM + RET_VWIDTH) // RET_V_DIM

    def body(of_ref, ob_ref, g_ref, gain_ref, z_ref):
        o = of_ref[...] + ob_ref[...]
        mu = jnp.mean(o, axis=-1, keepdims=True)
        var = jnp.mean(jnp.square(o - mu), axis=-1, keepdims=True)
        y = (o - mu) * lax.rsqrt(var + EPS) * gain_ref[...]
        gv = g_ref[...]
        z_ref[...] = (gv * jax.nn.sigmoid(gv) * y).astype(z_ref.dtype)

    blk = pl.BlockSpec((ROW_TILE, RET_V_DIM), lambda i, h: (i, h))
    return pl.pallas_call(
        body, name=name, grid=(T // ROW_TILE, RET_HEADS),
        in_specs=[blk, blk, pl.BlockSpec((ROW_TILE, RET_V_DIM), lambda i, h: (i, g_off + h)),
                  pl.BlockSpec((1, RET_V_DIM), lambda i, h: (0, h))],
        out_specs=blk, out_shape=jax.ShapeDtypeStruct((T, RET_VWIDTH), MXU_DTYPE),
        compiler_params=_params())(o_f, o_b, qkvg, gn_gain)


def _gated_out_bwd(dz, o_f, o_b, qkvg, gn_gain, name):
    T = o_f.shape[0]
    g_off = (2 * RET_HEADS * RET_QK_DIM + RET_VWIDTH) // RET_V_DIM

    def body(dz_ref, of_ref, ob_ref, g_ref, gain_ref, do_ref, dg_ref, dgain_ref):
        o = of_ref[...] + ob_ref[...]
        mu = jnp.mean(o, axis=-1, keepdims=True)
        var = jnp.mean(jnp.square(o - mu), axis=-1, keepdims=True)
        rstd = lax.rsqrt(var + EPS)
        yhat = (o - mu) * rstd
        gv, dzv = g_ref[...], dz_ref[...]
        sg = jax.nn.sigmoid(gv)
        dg_ref[...] = (dzv * (yhat * gain_ref[...]) * (sg * (1.0 + gv * (1.0 - sg)))).astype(dg_ref.dtype)
        dy = dzv * (gv * sg)

        @pl.when(pl.program_id(1) == 0)
        def _():
            dgain_ref[...] = jnp.zeros_like(dgain_ref)

        dgain_ref[...] += jnp.sum(dy * yhat, axis=0, keepdims=True)
        dyh = dy * gain_ref[...]
        do_ref[...] = rstd * (dyh - jnp.mean(dyh, axis=-1, keepdims=True)
                              - yhat * jnp.mean(dyh * yhat, axis=-1, keepdims=True))

    blk = pl.BlockSpec((ROW_TILE, RET_V_DIM), lambda h, i: (i, h))
    vec = pl.BlockSpec((1, RET_V_DIM), lambda h, i: (0, h))
    return pl.pallas_call(
        body, name=name, grid=(RET_HEADS, T // ROW_TILE),
        in_specs=[blk, blk, blk, pl.BlockSpec((ROW_TILE, RET_V_DIM), lambda h, i: (i, g_off + h)), vec],
        out_specs=(blk, blk, vec),
        out_shape=(jax.ShapeDtypeStruct((T, RET_VWIDTH), F32), jax.ShapeDtypeStruct((T, RET_VWIDTH), MXU_DTYPE),
                   jax.ShapeDtypeStruct((1, RET_VWIDTH), F32)),
        compiler_params=_params())(dz, o_f, o_b, qkvg, gn_gain)


def _adamw(w, m, v, parts, name):
    R, C = w.shape
    tr = _tile(R, (256, 128, 64, 32, 16, 8))
    n_parts = [p.shape[0] for p in parts]

    def body(*refs):
        w_ref, m_ref, v_ref = refs[:3]
        part_refs = refs[3:3 + len(parts)]
        g_ref, d_ref, nm_ref, nv_ref = refs[3 + len(parts):]
        g = None
        for ref, n in zip(part_refs, n_parts):
            for r in range(n):
                term = ref[r].astype(F32)
                g = term if g is None else g + term
        mn = ADAM_B1 * m_ref[...] + (1.0 - ADAM_B1) * g
        vn = ADAM_B2 * v_ref[...] + (1.0 - ADAM_B2) * jnp.square(g)
        m_hat = mn / (1.0 - ADAM_B1 ** ADAM_STEP)
        v_hat = vn / (1.0 - ADAM_B2 ** ADAM_STEP)
        g_ref[...] = g
        d_ref[...] = -ADAM_LR * (m_hat / (jnp.sqrt(v_hat) + ADAM_EPS) + ADAM_WD * w_ref[...])
        nm_ref[...] = mn
        nv_ref[...] = vn

    blk = pl.BlockSpec((tr, C), lambda i: (i, 0))
    part_specs = [pl.BlockSpec((n, tr, C), lambda i: (0, i, 0)) for n in n_parts]
    shp = jax.ShapeDtypeStruct((R, C), F32)
    return pl.pallas_call(
        body, name=name, grid=(R // tr,), in_specs=[blk, blk, blk] + part_specs,
        out_specs=(blk, blk, blk, blk), out_shape=(shp, shp, shp, shp),
        compiler_params=_params())(w, m, v, *parts)


def _sum_rows(parts, name):
    n, R, C = parts.shape
    tr = _tile(R, (256, 128, 64, 32, 16, 8))

    def body(p_ref, o_ref):
        acc = p_ref[0]
        for r in range(1, n):
            acc = acc + p_ref[r]
        o_ref[...] = acc

    return pl.pallas_call(
        body, name=name, grid=(R // tr,), in_specs=[pl.BlockSpec((n, tr, C), lambda i: (0, i, 0))],
        out_specs=pl.BlockSpec((tr, C), lambda i: (i, 0)), out_shape=jax.ShapeDtypeStruct((R, C), F32),
        compiler_params=_params())(parts)


def _my_coords():
    return lax.axis_index("x"), lax.axis_index("y"), lax.axis_index("c")


def _flip(coord, bit):
    return 1 - coord if bit else coord


def _all_gather(x2d, name):
    R, C = x2d.shape

    def body(x_ref, out_ref, send_sems, recv_sems, local_sem):
        x, y, c = _my_coords()
        me, sibling = (x, y, c), (x, y, 1 - c)
        chips = [(1 - x, y), (x, 1 - y), (1 - x, 1 - y)]

        def rows(px, py, pc):
            return out_ref.at[4 * px + 2 * py + pc]

        def copy(k, block, to, src=None):
            return pltpu.make_async_remote_copy(
                src_ref=rows(*block) if src is None else src, dst_ref=rows(*block),
                send_sem=send_sems.at[k], recv_sem=recv_sems.at[k], device_id=to, device_id_type=MESH)

        mine = pltpu.make_async_copy(x_ref, rows(*me), local_sem)
        mine.start()
        first = [copy(0, me, sibling, src=x_ref)]
        first += [copy(1 + j, me, (*chip, c), src=x_ref) for j, chip in enumerate(chips)]
        for cp in first:
            cp.start()
        passed = [copy(4 + j, (*chip, c), sibling) for j, chip in enumerate(chips)]
        for j, chip in enumerate(chips):
            copy(1 + j, (*chip, c), me).wait_recv()
            passed[j].start()
        copy(0, sibling, me).wait_recv()
        for j, chip in enumerate(chips):
            copy(4 + j, (*chip, 1 - c), me).wait_recv()
        for cp in first + passed:
            cp.wait_send()
        mine.wait()

    return pl.pallas_call(
        body, name=name, out_shape=jax.ShapeDtypeStruct((N_DEV, R, C), x2d.dtype),
        in_specs=[_ANY], out_specs=_ANY,
        scratch_shapes=[pltpu.SemaphoreType.DMA((7,)), pltpu.SemaphoreType.DMA((7,)), pltpu.SemaphoreType.DMA],
    )(x2d)


BIG_WEIGHTS = {
    "ffn_w_in": (2, (2, D_MODEL, 2 * D_FF)),
    "ffn_w_out": (1, (2, D_FF, D_MODEL)),
    "attn_w_qkv": (2, (1, D_MODEL, (N_HEADS + 2 * N_KV_HEADS) * HEAD_DIM)),
    "attn_w_o": (1, (1, N_HEADS * HEAD_DIM, D_MODEL)),
    "ret_w_qkvg": (2, (1, D_MODEL, 2 * D_MODEL + 2 * RET_VWIDTH)),
    "ret_gn_g": (2, (1, 1, RET_VWIDTH)),
    "ret_w_o": (1, (1, RET_VWIDTH, D_MODEL)),
}


def _join_shards(name, stacked):
    axis, full = BIG_WEIGHTS[name]
    if axis == 2:
        stacked = stacked.transpose(0, 2, 1, 3)
    return stacked.reshape(full)


def _split_shards(name, full_arr):
    axis, full = BIG_WEIGHTS[name]
    L, rows, cols = full
    if axis == 2:
        return full_arr.reshape(L, rows, N_DEV, cols // N_DEV).transpose(0, 2, 1, 3)
    return full_arr.reshape(L, N_DEV, rows // N_DEV, cols)


def _gather_shards(shards, name):
    n = len(shards)

    def body(*refs):
        x_refs, out_refs = refs[:n], refs[n:2 * n]
        send_sems, recv_sems, local_sems = refs[2 * n:]
        x, y, c = _my_coords()
        me, sibling = (x, y, c), (x, y, 1 - c)
        chips = [(1 - x, y), (x, 1 - y), (1 - x, 1 - y)]

        def rows(a, px, py, pc):
            return out_refs[a].at[:, 4 * px + 2 * py + pc]

        def copy(a, k, block, to, src=None):
            return pltpu.make_async_remote_copy(
                src_ref=rows(a, *block) if src is None else src, dst_ref=rows(a, *block),
                send_sem=send_sems.at[7 * a + k], recv_sem=recv_sems.at[7 * a + k], device_id=to, device_id_type=MESH)

        mine = [pltpu.make_async_copy(x_refs[a], rows(a, *me), local_sems.at[a]) for a in range(n)]
        for cp in mine:
            cp.start()
        first = []
        for a in range(n):
            first.append(copy(a, 0, me, sibling, src=x_refs[a]))
            first += [copy(a, 1 + j, me, (*chip, c), src=x_refs[a]) for j, chip in enumerate(chips)]
        for cp in first:
            cp.start()
        passed = []
        for j, chip in enumerate(chips):
            for a in range(n):
                copy(a, 1 + j, (*chip, c), me).wait_recv()
                fwd = copy(a, 4 + j, (*chip, c), sibling)
                fwd.start()
                passed.append(fwd)
        for a in range(n):
            copy(a, 0, sibling, me).wait_recv()
            for j, chip in enumerate(chips):
                copy(a, 4 + j, (*chip, 1 - c), me).wait_recv()
        for cp in first + passed:
            cp.wait_send()
        for cp in mine:
            cp.wait()

    return pl.pallas_call(
        body, name=name,
        out_shape=[jax.ShapeDtypeStruct((s.shape[0], N_DEV) + s.shape[1:], s.dtype) for s in shards],
        in_specs=[_ANY] * n, out_specs=[_ANY] * n,
        scratch_shapes=[pltpu.SemaphoreType.DMA((7 * n,)), pltpu.SemaphoreType.DMA((7 * n,)),
                        pltpu.SemaphoreType.DMA((n,))],
    )(*shards)


def _exchange_shards(arrs, masks, src_of, out_tail, name):
    n, nm = len(arrs), len(masks)

    def body(*refs):
        in_refs, out_refs = refs[:n], refs[n:2 * n]
        send_sems, recv_sems = refs[2 * n:]
        x, y, c = _my_coords()
        copies = []
        for a in range(n):
            for k, (bx, by, bc) in enumerate(masks):
                peer = (_flip(x, bx), _flip(y, by), _flip(c, bc))
                copies.append(pltpu.make_async_remote_copy(
                    src_ref=src_of(in_refs[a], peer, (x, y, c)), dst_ref=out_refs[a].at[k],
                    send_sem=send_sems.at[nm * a + k], recv_sem=recv_sems.at[nm * a + k],
                    device_id=peer, device_id_type=MESH))
        for cp in copies:
            cp.start()
        for cp in copies:
            cp.wait()

    return pl.pallas_call(
        body, name=name,
        out_shape=[jax.ShapeDtypeStruct((nm,) + out_tail(s), s.dtype) for s in arrs],
        in_specs=[_ANY] * n, out_specs=[_ANY] * n,
        scratch_shapes=[pltpu.SemaphoreType.DMA((nm * n,)), pltpu.SemaphoreType.DMA((nm * n,))],
    )(*arrs)


def _pair_sum(g, from_sibling, core, out_dtype, name):
    L, _, _, a, b = g.shape
    ta = a

    def body(core_ref, g_ref, s_ref, o_ref):
        o_ref[...] = (g_ref[...] + s_ref[...]).astype(out_dtype)

    blk = pl.BlockSpec((1, 1, ta, b), lambda l, q, i, core_ref: (l, q, i, 0))
    return pl.pallas_call(
        body, name=name,
        grid_spec=pltpu.PrefetchScalarGridSpec(
            num_scalar_prefetch=1, grid=(L, 4, a // ta),
            in_specs=[pl.BlockSpec((1, 1, pl.Squeezed(), ta, b), lambda l, q, i, core_ref: (l, q, core_ref[0], i, 0)), blk],
            out_specs=blk),
        out_shape=jax.ShapeDtypeStruct((L, 4, a, b), out_dtype), compiler_params=_params())(core, g, from_sibling)


def _mods(mod_x, mod_c, layer):
    both = jnp.concatenate([mod_x[:, layer], mod_c[layer][None]], axis=0)
    return [both[:, None, k * D_MODEL:(k + 1) * D_MODEL] for k in range(6)]


def _local_step(x, ctx, target, mod_x, mod_c, w, small, late_weights=None):
    B, S, _ = x.shape
    L = ctx.shape[1]
    NX, NC = B * S, B * L
    T = NX + NC
    tiles_per_ex = S // ROW_TILE
    nxt = NX // ROW_TILE
    gidx = _group_index(nxt, tiles_per_ex, B)
    gidx_for = lambda rows: _group_index(NX // rows, S // rows, B)
    mm_rows = _tile(S, (MM_ROWS, ROW_TILE))
    tidx = lambda i: jnp.where(i < nxt, i % tiles_per_ex, tiles_per_ex)
    G = B + 1
    x0 = jnp.concatenate([x.reshape(NX, D_MODEL), ctx.reshape(NC, D_MODEL)], axis=0)
    acos, asin = [jnp.tile(t, (1, LANES // HEAD_DIM)) for t in _rope_tables(S, HEAD_DIM)]
    rcos, rsin = _rope_tables(S, RET_QK_DIM)
    sink = small["attn_sink"].reshape(N_KV_HEADS, GQA_GROUP)
    gains = jnp.stack([jnp.tile(small["attn_q_norm"].reshape(1, HEAD_DIM), (1, LANES // HEAD_DIM)),
                       jnp.tile(small["attn_k_norm"].reshape(1, HEAD_DIM), (1, LANES // HEAD_DIM))])
    log_g = jax.nn.log_sigmoid(small["ret_decay_logit"].reshape(2, RET_HEADS))
    n1, n2 = small["norm1_g"], small["norm2_g"]

    m0 = _mods(mod_x, mod_c, 0)
    h1 = _norm_mod_fwd(x0, n1[0:1], m0[0], m0[1], gidx, "l0_norm1")
    qkv = _mm(h1, w["attn_w_qkv"][0], "nn", F32, "l0_qkv")
    qkv_r = _attn_prep_fwd(qkv, gains, acos, asin, tidx, "l0_qk_prep")
    o_x = _attn_fwd(qkv_r, sink, B, S, L, False, "l0_attn_x")
    o_c = _attn_fwd(qkv_r, sink, B, S, L, True, "l0_attn_c")
    o0 = jnp.concatenate([o_x, o_c], axis=0)
    mo0, x1 = _mm(o0, w["attn_w_o"][0], "nn", F32, "l0_attn_out", res=x0, gate=m0[2], gidx_for=gidx_for, gate_rows=mm_rows)
    h2 = _norm_mod_fwd(x1, n2[0:1], m0[3], m0[4], gidx, "l0_norm2")
    if late_weights is not None:
        w = {**w, **late_weights(x1)}
    u0 = _mm(h2, w["ffn_w_in"][0], "nn", F32, "l0_ffn_in")
    a0 = _swiglu_fwd(u0, "l0_swiglu")
    f0, x2 = _mm(a0, w["ffn_w_out"][0], "nn", F32, "l0_ffn_out", res=x1, gate=m0[5], gidx_for=gidx_for, gate_rows=mm_rows)

    m1 = _mods(mod_x, mod_c, 1)
    g1 = _norm_mod_fwd(x2, n1[1:2], m1[0], m1[1], gidx, "l1_norm1")
    qkvg = _mm(g1, w["ret_w_qkvg"][0], "nn", F32, "l1_qkvg")
    qk = _ret_rope(qkvg, rcos, rsin, tidx, "l1_rope")
    of, st_f, ob, st_b = _ret_fwd(qk, qkvg, log_g, B, S, L, "l1_ret")
    gn = w["ret_gn_g"].reshape(1, RET_VWIDTH)
    z1 = _gated_out_fwd(of, ob, qkvg, gn, "l1_gated_out")
    xx2 = x2[:NX]
    gx = lambda i: i // tiles_per_ex
    m1x = [t[:B] for t in m1]
    mo1, y1 = _mm(z1, w["ret_w_o"][0], "nn", F32, "l1_ret_out", res=xx2, gate=m1x[2], gidx_for=gidx_for, gate_rows=mm_rows)
    k2 = _norm_mod_fwd(y1, n2[1:2], m1x[3], m1x[4], gx, "l1_norm2")
    u1 = _mm(k2, w["ffn_w_in"][1], "nn", F32, "l1_ffn_in")
    a1 = _swiglu_fwd(u1, "l1_swiglu")
    f1, y2 = _mm(a1, w["ffn_w_out"][1], "nn", F32, "l1_ffn_out", res=y1, gate=m1x[5], gidx_for=gidx_for, gate_rows=mm_rows)

    loss_tile, dy2 = _loss_fwd_bwd(y2, target.reshape(NX, D_MODEL), "loss")

    zg = jnp.zeros((1, 1, D_MODEL), F32)
    dz, dgate5_1 = _gate_bwd(dy2, f1, m1x[5], gx, B, "l1_ffn_gate_bwd")
    gw_ffn_out1 = _mm(a1, dz, "tn", F32, "l1_ffn_out_dw")
    da = _mm(dz, w["ffn_w_out"][1], "nt", F32, "l1_ffn_out_dx")
    du = _swiglu_bwd(da, u1, "l1_swiglu_bwd")
    gw_ffn_in1 = _mm(k2, du, "tn", F32, "l1_ffn_in_dw")
    dk2 = _mm(du, w["ffn_w_in"][1], "nt", F32, "l1_ffn_in_dx")
    dy1, dsh3_1, dsc4_1, dn2_1 = _norm_mod_bwd(dk2, y1, n2[1:2], m1x[4], dy2, gx, B, "l1_norm2_bwd")
    dzo, dgate2_1 = _gate_bwd(dy1, mo1, m1x[2], gx, B, "l1_ret_gate_bwd")
    gw_ret_o = _mm(z1, dzo, "tn", F32, "l1_ret_out_dw")
    dz1 = _mm(dzo, w["ret_w_o"][0], "nt", F32, "l1_ret_out_dx")
    do_r, dg_r, dgn = _gated_out_bwd(dz1, of, ob, qkvg, gn, "l1_gated_out_bwd")
    ((dq_f, dk_f, dv_f, dkc_f, dvc_f, dlg_f),
     (dq_b, dk_b, dv_b, dkc_b, dvc_b, dlg_b)) = _ret_bwd(qk, qkvg, log_g, st_f, st_b, do_r, B, S, L, "l1_ret_bwd")
    dqkvg = _ret_grad_assemble((dq_f, dq_b, dk_f, dk_b, dv_f, dv_b), (dkc_f, dkc_b, dvc_f, dvc_b), dg_r, rcos, rsin, S,
                               "l1_qkvg_grad")
    gw_ret_qkvg = _mm(g1, dqkvg, "tn", F32, "l1_qkvg_dw")
    dg1 = _mm(dqkvg, w["ret_w_qkvg"][0], "nt", F32, "l1_qkvg_dx")
    dres1 = jnp.concatenate([dy1, jnp.zeros((NC, D_MODEL), F32)], axis=0)
    dx2, dsh0_1, dsc1_1, dn1_1 = _norm_mod_bwd(dg1, x2, n1[1:2], m1[1], dres1, gidx, G, "l1_norm1_bwd")
    dlg = jnp.stack([jnp.sum(dlg_f[:, :, 0, 0], axis=0), jnp.sum(dlg_b[:, :, 0, 0], axis=0)])
    d_decay = (dlg * jax.nn.sigmoid(-small["ret_decay_logit"].reshape(2, RET_HEADS))).reshape(1, 2, RET_HEADS)

    dz, dgate5_0 = _gate_bwd(dx2, f0, m0[5], gidx, G, "l0_ffn_gate_bwd")
    gw_ffn_out0 = _mm(a0, dz, "tn", F32, "l0_ffn_out_dw")
    da = _mm(dz, w["ffn_w_out"][0], "nt", F32, "l0_ffn_out_dx")
    du = _swiglu_bwd(da, u0, "l0_swiglu_bwd")
    gw_ffn_in0 = _mm(h2, du, "tn", F32, "l0_ffn_in_dw")
    dh2 = _mm(du, w["ffn_w_in"][0], "nt", F32, "l0_ffn_in_dx")
    dx1, dsh3_0, dsc4_0, dn2_0 = _norm_mod_bwd(dh2, x1, n2[0:1], m0[4], dx2, gidx, G, "l0_norm2_bwd")
    dzo, dgate2_0 = _gate_bwd(dx1, mo0, m0[2], gidx, G, "l0_attn_gate_bwd")
    gw_attn_o = _mm(o0, dzo, "tn", F32, "l0_attn_out_dw")
    do0 = _mm(dzo, w["attn_w_o"][0], "nt", MXU_DTYPE, "l0_attn_out_dx")
    dq_x, dk_x, dv_x, dkc1, dvc1, dsink_x = _attn_bwd(qkv_r, sink, do0, B, S, L, False, "l0_attn_x_bwd")
    dq_c, dkc2, dvc2, dsink_c = _attn_bwd(qkv_r, sink, do0, B, S, L, True, "l0_attn_c_bwd")
    dqk = jnp.concatenate([jnp.concatenate([dq_x, dk_x], axis=1), jnp.concatenate([dq_c, dkc1 + dkc2], axis=1)], axis=0)
    dvv = jnp.concatenate([dv_x, dvc1 + dvc2], axis=0)
    dqkv, dgains = _attn_prep_bwd(dqk, dvv, qkv, gains, acos, asin, tidx, "l0_qk_prep_bwd")
    gw_attn_qkv = _mm(h1, dqkv, "tn", F32, "l0_qkv_dw")
    dh1 = _mm(dqkv, w["attn_w_qkv"][0], "nt", F32, "l0_qkv_dx")
    dx0, dsh0_0, dsc1_0, dn1_0 = _norm_mod_bwd(dh1, x0, n1[0:1], m0[1], dx1, gidx, G, "l0_norm1_bwd")

    dgains = jnp.sum(dgains.reshape(ATTN_QK_BLOCKS, LANES // HEAD_DIM, HEAD_DIM), axis=1)
    dsink = (dsink_x + dsink_c).reshape(N_KV_HEADS, 8, LANES)[:, :GQA_GROUP, 0].reshape(1, N_HEADS)
    grads_big = {
        "ffn_w_in": jnp.stack([gw_ffn_in0, gw_ffn_in1]),
        "ffn_w_out": jnp.stack([gw_ffn_out0, gw_ffn_out1]),
        "attn_w_qkv": gw_attn_qkv[None],
        "attn_w_o": gw_attn_o[None],
        "ret_w_qkvg": gw_ret_qkvg[None],
        "ret_gn_g": dgn,
        "ret_w_o": gw_ret_o[None],
    }
    grads_small = {
        "norm1_g": jnp.concatenate([dn1_0, dn1_1], axis=0),
        "norm2_g": jnp.concatenate([dn2_0, dn2_1], axis=0),
        "attn_q_norm": jnp.sum(dgains[:ATTN_Q_BLOCKS], axis=0)[None],
        "attn_k_norm": jnp.sum(dgains[ATTN_Q_BLOCKS:ATTN_QK_BLOCKS], axis=0)[None],
        "attn_sink": dsink,
        "ret_decay_logit": d_decay,
    }

    def pad_g(t):
        return jnp.concatenate([t, zg], axis=0)

    d0 = jnp.concatenate([dsh0_0, dsc1_0, dgate2_0, dsh3_0, dsc4_0, dgate5_0], axis=2)[:, 0]
    d1 = jnp.concatenate([dsh0_1, dsc1_1, pad_g(dgate2_1), pad_g(dsh3_1), pad_g(dsc4_1), pad_g(dgate5_1)],
                         axis=2)[:, 0]
    dmod_x = jnp.stack([d0[:B], d1[:B]], axis=1)
    dmod_c = jnp.stack([d0[B], d1[B]], axis=0)
    return loss_tile, dx0[:NX].reshape(B, S, D_MODEL), grads_big, grads_small, dmod_x, dmod_c


SMALL_NAMES = ("c_ctx", "ada_b", "norm1_g", "norm2_g", "attn_q_norm", "attn_k_norm", "attn_sink", "ret_decay_logit")
ADA_ROWS = 64


def _pack_small(d, rows):
    flat = jnp.concatenate([d[k].reshape(-1) for k in SMALL_NAMES])
    n = rows * LANES
    return jnp.pad(flat, (0, n - flat.shape[0])).reshape(rows, LANES)


def _unpack_small(packed, shapes):
    flat = packed.reshape(-1)
    out, off = {}, 0
    for k in SMALL_NAMES:
        n = math.prod(shapes[k])
        out[k] = flat[off:off + n].reshape(shapes[k])
        off += n
    return out


EARLY_WEIGHTS = ("attn_w_qkv", "attn_w_o")
LATE_WEIGHTS = tuple(k for k in BIG_WEIGHTS if k not in EARLY_WEIGHTS)

_HBM = pl.BlockSpec(memory_space=pltpu.HBM)
_SEM = pl.BlockSpec(memory_space=pltpu.SEMAPHORE)
_DATAFLOW = pltpu.SideEffectType.DATAFLOW_SIDE_EFFECTING
_PEER_FLIPS = ((0, 0, 1), (0, 1, 0), (0, 1, 1), (1, 0, 0), (1, 0, 1), (1, 1, 0), (1, 1, 1))


def _wire_shard(name, t):
    return t.reshape(1, 1, -1) if name == "ret_gn_g" else t.astype(MXU_DTYPE)


def _direct_copies(x_refs, land_refs, send_sems, recv_sems, landing):
    x, y, c = _my_coords()
    out = []
    for a in range(len(x_refs)):
        for k, (bx, by, bc) in enumerate(_PEER_FLIPS):
            peer = (_flip(x, bx), _flip(y, by), _flip(c, bc))
            slot = (4 * peer[0] + 2 * peer[1] + peer[2]) if landing else (4 * x + 2 * y + c)
            out.append(pltpu.make_async_remote_copy(
                src_ref=x_refs[a], dst_ref=land_refs[a].at[:, slot], send_sem=send_sems.at[7 * a + k],
                recv_sem=recv_sems.at[7 * a + k], device_id=peer, device_id_type=MESH))
    return out


def _gather_start(shards, name):
    n = len(shards)
    lands = [lax.empty((s.shape[0], N_DEV) + s.shape[1:], s.dtype) for s in shards]

    def body(*refs):
        send_sems, recv_sems = refs[2 * n], refs[2 * n + 1]
        x_refs, land_refs = refs[2 * n + 2:3 * n + 2], refs[3 * n + 2:4 * n + 2]
        for cp in _direct_copies(x_refs, land_refs, send_sems, recv_sems, landing=False):
            cp.start()
        refs[-1][...] = jnp.zeros_like(refs[-1])

    hbm = lambda t: pltpu.with_memory_space_constraint(t, pltpu.HBM)
    res = pl.pallas_call(
        body, name=name,
        out_shape=(pltpu.SemaphoreType.DMA((7 * n,)), pltpu.SemaphoreType.DMA((7 * n,)))
        + tuple(pltpu.HBM(t.shape, t.dtype) for t in shards + lands) + (jax.ShapeDtypeStruct((8, LANES), F32),),
        in_specs=[_HBM] * (2 * n), out_specs=(_SEM, _SEM) + (_HBM,) * (2 * n) + (pl.BlockSpec(memory_space=pltpu.VMEM),),
        input_output_aliases={i: 2 + i for i in range(2 * n)},
        compiler_params=pltpu.CompilerParams(has_side_effects=_DATAFLOW))(*[hbm(t) for t in shards + lands])
    return res[0], res[1], list(res[2:2 + n]), list(res[2 + n:2 + 2 * n]), res[-1]


def _gather_wait(send_sems, recv_sems, shards, lands, after, name):
    n = len(shards)

    def body(*refs):
        x_refs, land_refs = refs[:n], refs[n:2 * n]
        for cp in _direct_copies(x_refs, land_refs, refs[2 * n], refs[2 * n + 1], landing=True):
            cp.wait_send()
            cp.wait_recv()

    res = pl.pallas_call(
        body, name=name, out_shape=tuple(pltpu.HBM(t.shape, t.dtype) for t in shards + lands),
        in_specs=[_HBM] * (2 * n) + [_SEM, _SEM, _ANY], out_specs=(_HBM,) * (2 * n),
        input_output_aliases={i: i for i in range(2 * n)},
        compiler_params=pltpu.CompilerParams(has_side_effects=_DATAFLOW))(*shards, *lands, send_sems, recv_sems, after)
    return list(res[n:])


def _gather_big_weights(weights, names, name):
    gathered = _gather_shards([_wire_shard(k, weights[k]) for k in names], name)
    return {k: _join_shards(k, g) for k, g in zip(names, gathered)}


def _reduce_and_update_big(g_big, weights, mom1, mom2):
    mx_, my_, mc_ = _my_coords()
    my_chip = 2 * mx_ + my_
    names = list(BIG_WEIGHTS)
    split = []
    for k in names:
        s = _split_shards(k, g_big[k])
        split.append(s.reshape(s.shape[0], 4, 2, s.shape[2], s.shape[3]))
    from_sibling = _exchange_shards(
        split, [(0, 0, 1)], lambda ref, peer, me_: ref.at[:, :, peer[2]],
        lambda s: (s.shape[0], 4) + s.shape[3:], "rs_sibling")
    from_sibling = [t[0] for t in from_sibling]
    core = mc_.astype(jnp.int32).reshape(1)
    pair = [_pair_sum(g, s, core, MXU_DTYPE, "rs_pair_" + k) for k, g, s in zip(names, split, from_sibling)]
    from_chips = _exchange_shards(
        pair, [(1, 0, 0), (0, 1, 0), (1, 1, 0)], lambda ref, peer, me_: ref.at[:, 2 * peer[0] + peer[1]],
        lambda s: (s.shape[0],) + s.shape[2:], "rs_chips")
    big = {}
    for k, g, s, r in zip(names, split, from_sibling, from_chips):
        L_, _, _, a_, b_ = g.shape
        own_keep = lax.dynamic_index_in_dim(lax.dynamic_index_in_dim(g, my_chip, axis=1, keepdims=False), mc_, axis=1,
                                            keepdims=False)
        own_sib = lax.dynamic_index_in_dim(s, my_chip, axis=1, keepdims=False)
        rows = L_ * a_
        res = _adamw(weights[k].reshape(rows, b_), mom1[k].reshape(rows, b_), mom2[k].reshape(rows, b_),
                     [own_keep.reshape(1, rows, b_), own_sib.reshape(1, rows, b_), r.reshape(3, rows, b_)],
                     "adamw_" + k)
        big[k] = [t.reshape(weights[k].shape) for t in res]
    return big


def kernel(x, c, ctx, c_ctx, ada_w, ada_b, norm1_g, norm2_g, ffn_w_in, ffn_w_out, attn_w_qkv, attn_q_norm, attn_k_norm, attn_sink, attn_w_o, ret_w_qkvg, ret_decay_logit, ret_gn_g, ret_w_o, loss_target, m_c_ctx, m_ada_w, m_ada_b, m_norm1_g, m_norm2_g, m_ffn_w_in, m_ffn_w_out, m_attn_w_qkv, m_attn_q_norm, m_attn_k_norm, m_attn_sink, m_attn_w_o, m_ret_w_qkvg, m_ret_decay_logit, m_ret_gn_g, m_ret_w_o, v_c_ctx, v_ada_w, v_ada_b, v_norm1_g, v_norm2_g, v_ffn_w_in, v_ffn_w_out, v_attn_w_qkv, v_attn_q_norm, v_attn_k_norm, v_attn_sink, v_attn_w_o, v_ret_w_qkvg, v_ret_decay_logit, v_ret_gn_g, v_ret_w_o):
    weights = dict(c_ctx=c_ctx, ada_w=ada_w, ada_b=ada_b, norm1_g=norm1_g, norm2_g=norm2_g, ffn_w_in=ffn_w_in,
                   ffn_w_out=ffn_w_out, attn_w_qkv=attn_w_qkv, attn_q_norm=attn_q_norm, attn_k_norm=attn_k_norm,
                   attn_sink=attn_sink, attn_w_o=attn_w_o, ret_w_qkvg=ret_w_qkvg, ret_decay_logit=ret_decay_logit,
                   ret_gn_g=ret_gn_g, ret_w_o=ret_w_o)
    mom1 = dict(c_ctx=m_c_ctx, ada_w=m_ada_w, ada_b=m_ada_b, norm1_g=m_norm1_g, norm2_g=m_norm2_g, ffn_w_in=m_ffn_w_in,
                ffn_w_out=m_ffn_w_out, attn_w_qkv=m_attn_w_qkv, attn_q_norm=m_attn_q_norm, attn_k_norm=m_attn_k_norm,
                attn_sink=m_attn_sink, attn_w_o=m_attn_w_o, ret_w_qkvg=m_ret_w_qkvg, ret_decay_logit=m_ret_decay_logit,
                ret_gn_g=m_ret_gn_g, ret_w_o=m_ret_w_o)
    mom2 = dict(c_ctx=v_c_ctx, ada_w=v_ada_w, ada_b=v_ada_b, norm1_g=v_norm1_g, norm2_g=v_norm2_g, ffn_w_in=v_ffn_w_in,
                ffn_w_out=v_ffn_w_out, attn_w_qkv=v_attn_w_qkv, attn_q_norm=v_attn_q_norm, attn_k_norm=v_attn_k_norm,
                attn_sink=v_attn_sink, attn_w_o=v_attn_w_o, ret_w_qkvg=v_ret_w_qkvg, ret_decay_logit=v_ret_decay_logit,
                ret_gn_g=v_ret_gn_g, ret_w_o=v_ret_w_o)
    B = x.shape[0]
    mx_, my_, mc_ = _my_coords()
    me = 4 * mx_ + 2 * my_ + mc_
    ada_cols = ada_w.shape[2]

    w_full = _gather_big_weights(weights, EARLY_WEIGHTS, "gather_early")

    c_all = _all_gather(jax.nn.silu(c), "gather_c").reshape(N_DEV * B, D_MODEL)
    cc_act = jax.nn.silu(c_ctx)[None]
    ada_in = jnp.concatenate([c_all, cc_act, jnp.zeros((ADA_ROWS - N_DEV * B - 1, D_MODEL), F32)], axis=0)
    ada_in = ada_in.astype(MXU_DTYPE)
    ada_w2 = jnp.concatenate([ada_w[0], ada_w[1]], axis=1)
    bias = lax.dynamic_slice_in_dim(ada_b.reshape(2, N_DEV, ada_cols), me, 1, axis=1).reshape(1, 2 * ada_cols)
    mod_cols = _mm(ada_in, ada_w2, "nn", F32, "ada_fwd", bias=bias)
    mod_all = _all_gather(mod_cols, "gather_mod")
    mod_all = mod_all.reshape(N_DEV, ADA_ROWS, 2, ada_cols).transpose(1, 2, 0, 3).reshape(ADA_ROWS, 2, N_DEV * ada_cols)
    mod_x = lax.dynamic_slice_in_dim(mod_all, me * B, B, axis=0)
    mod_c = mod_all[N_DEV * B]

    order = 0.0 * (mod_c[0, 0] + w_full["attn_w_o"][0, 0, 0].astype(F32))
    late_shards = [_wire_shard(k, weights[k] + order if k == "ret_gn_g" else weights[k]) for k in LATE_WEIGHTS]
    send_sems, recv_sems, late_thru, late_lands, token = _gather_start(late_shards, "gather_late_start")
    mod_x = mod_x + token[0, 0]

    def late_weights(after):
        lands = _gather_wait(send_sems, recv_sems, late_thru, late_lands, after, "gather_late_wait")
        own = [lax.dynamic_update_index_in_dim(land, shard, me, axis=1) for land, shard in zip(lands, late_shards)]
        return {k: _join_shards(k, g) for k, g in zip(LATE_WEIGHTS, own)}

    small = {k: weights[k] for k in SMALL_NAMES}
    loss_tile, grad_x, g_big, g_small, dmod_x, dmod_c = _local_step(x, ctx, loss_target, mod_x, mod_c, w_full, small,
                                                                     late_weights)
    loss = lax.psum(loss_tile[0, 0], ("x", "y", "c"))

    n_mod = 2 * 6 * D_MODEL
    dm_rows = jnp.concatenate([dmod_x.reshape(B, n_mod), dmod_c.reshape(1, n_mod),
                               jnp.zeros((8 - B - 1, n_mod), F32)], axis=0)
    dm_all = _all_gather(dm_rows, "gather_dmod")
    dmc_tot = _sum_rows(dm_all[:, B:B + 1].reshape(N_DEV, 1, n_mod)[:, :, :].reshape(N_DEV, n_mod // LANES, LANES),
                        "sum_dmod_c").reshape(1, n_mod)
    dmod_rows = jnp.concatenate([dm_all[:, :B].reshape(N_DEV * B, n_mod), dmc_tot,
                                 jnp.zeros((ADA_ROWS - N_DEV * B - 1, n_mod), F32)], axis=0)
    dmod_mine = lax.dynamic_slice_in_dim(dmod_rows.reshape(ADA_ROWS, 2, N_DEV, ada_cols), me, 1, axis=2)
    dmod_mine = dmod_mine.reshape(ADA_ROWS, 2 * ada_cols).astype(MXU_DTYPE)
    g_ada2 = _mm(ada_in, dmod_mine, "tn", F32, "ada_dw")
    g_ada_w = jnp.stack([g_ada2[:, :ada_cols], g_ada2[:, ada_cols:]])
    dmc_mine = jnp.concatenate([dmod_mine[N_DEV * B:N_DEV * B + 1], jnp.zeros((7, 2 * ada_cols), MXU_DTYPE)], axis=0)
    dcc_part = _mm(dmc_mine, ada_w2, "nt", F32, "ada_dc")[0:1]
    g_ada_b = _sum_rows(dmod_rows[:, None, :].reshape(ADA_ROWS, n_mod // LANES, LANES), "sum_dmod_b").reshape(2, 6 * D_MODEL)
    sg = jax.nn.sigmoid(c_ctx)
    g_small["c_ctx"] = dcc_part.reshape(D_MODEL) * (sg * (1.0 + c_ctx * (1.0 - sg)))
    g_small["ada_b"] = g_ada_b * (1.0 / N_DEV)

    shapes = {k: weights[k].shape for k in SMALL_NAMES}
    n_small = sum(math.prod(s) for s in shapes.values())
    srows = -(-(-(-n_small // LANES)) // 8) * 8
    gs_all = _all_gather(_pack_small(g_small, srows), "gather_small_grads")
    sm = _adamw(_pack_small({k: weights[k] for k in SMALL_NAMES}, srows), _pack_small({k: mom1[k] for k in SMALL_NAMES}, srows),
                _pack_small({k: mom2[k] for k in SMALL_NAMES}, srows), [gs_all], "adamw_small")
    sm = [_unpack_small(t, shapes) for t in sm]

    ada_shape = ada_w.shape
    r2 = lambda t: t.reshape(ada_shape[0] * ada_shape[1], ada_shape[2])
    ada = [t.reshape(ada_shape) for t in _adamw(r2(ada_w), r2(m_ada_w), r2(v_ada_w), [r2(g_ada_w)[None]], "adamw_ada")]

    big = _reduce_and_update_big(g_big, weights, mom1, mom2)

    def pick(i, name):
        if name in BIG_WEIGHTS:
            return big[name][i]
        if name == "ada_w":
            return ada[i]
        return sm[i][name]

    order = ("c_ctx", "ada_w", "ada_b", "norm1_g", "norm2_g", "ffn_w_in", "ffn_w_out", "attn_w_qkv", "attn_q_norm",
             "attn_k_norm", "attn_sink", "attn_w_o", "ret_w_qkvg", "ret_decay_logit", "ret_gn_g", "ret_w_o")
    outs = [loss, grad_x]
    for i in range(4):
        outs += [pick(i, n) for n in order]
    return tuple(outs)
```

```python
import functools
import math

import jax
import jax.numpy as jnp
from jax import lax
from jax.experimental import pallas as pl
from jax.experimental.pallas import tpu as pltpu

F32 = jnp.float32
MXU_DTYPE = jnp.bfloat16

D_MODEL = 1024
HEAD_DIM = 64
N_HEADS = 16
N_KV_HEADS = 4
GQA_GROUP = 4
WINDOW = 128
ATTN_BLOCK = 128
RET_HEADS = 4
RET_QK_DIM = 256
RET_V_DIM = 512
RET_VWIDTH = 2048
RET_CHUNK = 256
D_FF = 2816
GRID_W = 64
ROPE_BASE = 10000.0
EPS = 1e-6
NEG_INF = -1e30

ADAM_LR = 0.001
ADAM_B1 = 0.9
ADAM_B2 = 0.999
ADAM_EPS = 1e-08
ADAM_WD = 0.01
ADAM_STEP = 10

N_DEV = 8
LANES = 128
ROW_TILE = 512
VMEM_LIMIT = 48 * 1024 * 1024

MESH = pl.DeviceIdType.MESH
_ANY = pl.BlockSpec(memory_space=pl.ANY)
_SMEM = pl.BlockSpec(memory_space=pltpu.SMEM)


def _params(**kw):
    return pltpu.CompilerParams(vmem_limit_bytes=VMEM_LIMIT, **kw)


def _mx(v):
    return v.astype(MXU_DTYPE)


def _dot(a, b, dims):
    return lax.dot_general(_mx(a), _mx(b), (dims, ((), ())), preferred_element_type=F32)


_NN = ((1,), (0,))
_NT = ((1,), (1,))
_TN = ((0,), (0,))


def _tile(n, cands):
    for c in cands:
        if n % c == 0:
            return c
    return n


def _big_tile(n, cap):
    if n <= cap:
        return n
    for t in range(cap - cap % LANES, 0, -LANES):
        if n % t == 0:
            return t
    return n


MM_ROWS = 1024
MM_COLS = 1408
MM_DEPTH = 2048


def _k_tile(k):
    return _big_tile(k, MM_DEPTH)


def _mm(a, b, mode, out_dtype, name, *, bias=None, res=None, gate=None, gidx_for=None, gate_rows=None):
    if mode == "nn":
        (M, K), (_, N) = a.shape, b.shape
    elif mode == "nt":
        (M, K), (N, _) = a.shape, b.shape
    else:
        (K, M), (_, N) = a.shape, b.shape
    if res is not None:
        tm, tn = gate_rows, _big_tile(N, 512)
        gidx = gidx_for(tm)
    else:
        tm = _big_tile(M, MM_COLS if mode == "tn" else MM_ROWS)
        tn = _big_tile(N, MM_COLS)
    tk = _k_tile(K)
    nk = K // tk
    dims = {"nn": _NN, "nt": _NT, "tn": _TN}[mode]
    a_spec = (pl.BlockSpec((tk, tm), lambda i, j, k: (k, i)) if mode == "tn"
              else pl.BlockSpec((tm, tk), lambda i, j, k: (i, k)))
    b_spec = (pl.BlockSpec((tn, tk), lambda i, j, k: (j, k)) if mode == "nt"
              else pl.BlockSpec((tk, tn), lambda i, j, k: (k, j)))
    o_spec = pl.BlockSpec((tm, tn), lambda i, j, k: (i, j))
    in_specs, operands = [a_spec, b_spec], [a, b]
    if bias is not None:
        in_specs.append(pl.BlockSpec((1, tn), lambda i, j, k: (0, j)))
        operands.append(bias)
    if res is not None:
        in_specs += [o_spec, pl.BlockSpec((1, 1, tn), lambda i, j, k: (gidx(i), 0, j))]
        operands += [res, gate]
        out_shape = (jax.ShapeDtypeStruct((M, N), F32), jax.ShapeDtypeStruct((M, N), F32))
        out_specs = (o_spec, o_spec)
    else:
        out_shape = jax.ShapeDtypeStruct((M, N), out_dtype)
        out_specs = o_spec

    def body(*refs):
        a_ref, b_ref = refs[0], refs[1]
        extra = refs[2:len(operands)]
        outs = refs[len(operands):]
        prod = _dot(a_ref[...], b_ref[...], dims)

        def finish(acc):
            if bias is not None:
                outs[0][...] = (acc + extra[0][...]).astype(out_dtype)
            elif res is not None:
                outs[0][...] = acc
                outs[1][...] = extra[0][...] + extra[1][0] * acc
            else:
                outs[0][...] = acc.astype(out_dtype)

        if nk == 1:
            finish(prod)
        else:
            acc_ref = outs[-1]
            outs = outs[:-1]
            k = pl.program_id(2)

            @pl.when(k == 0)
            def _():
                acc_ref[...] = prod

            @pl.when(k > 0)
            def _():
                acc_ref[...] += prod

            @pl.when(k == nk - 1)
            def _():
                finish(acc_ref[...])

    return pl.pallas_call(
        body, name=name, grid=(M // tm, N // tn, nk), in_specs=in_specs, out_specs=out_specs, out_shape=out_shape,
        scratch_shapes=[pltpu.VMEM((tm, tn), F32)] if nk > 1 else [],
        compiler_params=_params())(*operands)


def _group_index(n_x_tiles, tiles_per_example, n_examples):
    def gidx(i):
        return jnp.where(i < n_x_tiles, i // tiles_per_example, n_examples)
    return gidx


def _norm_mod_fwd(x, g, shift, scale, gidx, name):
    T, Dm = x.shape

    def body(x_ref, g_ref, sh_ref, sc_ref, h_ref):
        xv = x_ref[...]
        r = lax.rsqrt(jnp.mean(xv * xv, axis=-1, keepdims=True) + EPS)
        y = xv * r * g_ref[...]
        h_ref[...] = (y * (1.0 + sc_ref[0]) + sh_ref[0]).astype(h_ref.dtype)

    row = pl.BlockSpec((ROW_TILE, Dm), lambda i: (i, 0))
    mod = pl.BlockSpec((1, 1, Dm), lambda i: (gidx(i), 0, 0))
    return pl.pallas_call(
        body, name=name, grid=(T // ROW_TILE,),
        in_specs=[row, pl.BlockSpec((1, Dm), lambda i: (0, 0)), mod, mod],
        out_specs=row, out_shape=jax.ShapeDtypeStruct((T, Dm), MXU_DTYPE),
        compiler_params=_params())(x, g, shift, scale)


def _first_of_group(i, gidx):
    return jnp.logical_or(i == 0, gidx(i) != gidx(jnp.maximum(i - 1, 0)))


def _norm_mod_bwd(dh, x, g, scale, dres, gidx, n_groups, name):
    T, Dm = x.shape

    def body(dh_ref, x_ref, g_ref, sc_ref, dres_ref, dx_ref, dsh_ref, dsc_ref, dg_ref):
        i = pl.program_id(0)
        xv, dhv = x_ref[...], dh_ref[...]
        r = lax.rsqrt(jnp.mean(xv * xv, axis=-1, keepdims=True) + EPS)
        xn = xv * r
        y = xn * g_ref[...]

        @pl.when(_first_of_group(i, gidx))
        def _():
            dsh_ref[...] = jnp.zeros_like(dsh_ref)
            dsc_ref[...] = jnp.zeros_like(dsc_ref)

        @pl.when(i == 0)
        def _():
            dg_ref[...] = jnp.zeros_like(dg_ref)

        dsh_ref[0] += jnp.sum(dhv, axis=0, keepdims=True)
        dsc_ref[0] += jnp.sum(dhv * y, axis=0, keepdims=True)
        dy = dhv * (1.0 + sc_ref[0])
        dg_ref[...] += jnp.sum(dy * xn, axis=0, keepdims=True)
        dxn = dy * g_ref[...]
        dx = r * (dxn - xn * jnp.mean(dxn * xn, axis=-1, keepdims=True))
        dx_ref[...] = dres_ref[...] + dx

    row = pl.BlockSpec((ROW_TILE, Dm), lambda i: (i, 0))
    mod = pl.BlockSpec((1, 1, Dm), lambda i: (gidx(i), 0, 0))
    vec = pl.BlockSpec((1, Dm), lambda i: (0, 0))
    return pl.pallas_call(
        body, name=name, grid=(T // ROW_TILE,),
        in_specs=[row, row, vec, mod, row],
        out_specs=(row, mod, mod, vec),
        out_shape=(jax.ShapeDtypeStruct((T, Dm), F32), jax.ShapeDtypeStruct((n_groups, 1, Dm), F32),
                   jax.ShapeDtypeStruct((n_groups, 1, Dm), F32), jax.ShapeDtypeStruct((1, Dm), F32)),
        compiler_params=_params())(dh, x, g, scale, dres)


def _gate_bwd(dy, f, gate, gidx, n_groups, name):
    T, Dm = dy.shape

    def body(dy_ref, f_ref, gate_ref, dz_ref, dgate_ref):
        i = pl.program_id(0)
        dyv = dy_ref[...]

        @pl.when(_first_of_group(i, gidx))
        def _():
            dgate_ref[...] = jnp.zeros_like(dgate_ref)

        dgate_ref[0] += jnp.sum(dyv * f_ref[...], axis=0, keepdims=True)
        dz_ref[...] = (dyv * gate_ref[0]).astype(dz_ref.dtype)

    row = pl.BlockSpec((ROW_TILE, Dm), lambda i: (i, 0))
    mod = pl.BlockSpec((1, 1, Dm), lambda i: (gidx(i), 0, 0))
    return pl.pallas_call(
        body, name=name, grid=(T // ROW_TILE,), in_specs=[row, row, mod], out_specs=(row, mod),
        out_shape=(jax.ShapeDtypeStruct((T, Dm), MXU_DTYPE), jax.ShapeDtypeStruct((n_groups, 1, Dm), F32)),
        compiler_params=_params())(dy, f, gate)


SWIGLU_ROWS = 256


def _swiglu_fwd(u, name):
    T = u.shape[0]

    def body(u_ref, a_ref):
        gate, up = u_ref[:, :D_FF], u_ref[:, D_FF:]
        a_ref[...] = (gate * jax.nn.sigmoid(gate) * up).astype(a_ref.dtype)

    return pl.pallas_call(
        body, name=name, grid=(T // SWIGLU_ROWS,),
        in_specs=[pl.BlockSpec((SWIGLU_ROWS, 2 * D_FF), lambda i: (i, 0))],
        out_specs=pl.BlockSpec((SWIGLU_ROWS, D_FF), lambda i: (i, 0)),
        out_shape=jax.ShapeDtypeStruct((T, D_FF), MXU_DTYPE), compiler_params=_params())(u)


def _swiglu_bwd(da, u, name):
    T = u.shape[0]

    def body(da_ref, u_ref, du_ref):
        gate, up, dav = u_ref[:, :D_FF], u_ref[:, D_FF:], da_ref[...]
        sg = jax.nn.sigmoid(gate)
        du_ref[:, :D_FF] = (dav * up * (sg * (1.0 + gate * (1.0 - sg)))).astype(du_ref.dtype)
        du_ref[:, D_FF:] = (dav * gate * sg).astype(du_ref.dtype)

    return pl.pallas_call(
        body, name=name, grid=(T // SWIGLU_ROWS,),
        in_specs=[pl.BlockSpec((SWIGLU_ROWS, D_FF), lambda i: (i, 0)),
                  pl.BlockSpec((SWIGLU_ROWS, 2 * D_FF), lambda i: (i, 0))],
        out_specs=pl.BlockSpec((SWIGLU_ROWS, 2 * D_FF), lambda i: (i, 0)),
        out_shape=jax.ShapeDtypeStruct((T, 2 * D_FF), MXU_DTYPE), compiler_params=_params())(da, u)


def _loss_fwd_bwd(y, target, name):
    T, Dm = y.shape

    def body(y_ref, t_ref, loss_ref, dy_ref):
        err = y_ref[...] - t_ref[...]

        @pl.when(pl.program_id(0) == 0)
        def _():
            loss_ref[...] = jnp.zeros_like(loss_ref)

        loss_ref[...] += 0.5 * jnp.sum(jnp.mean(err * err, axis=-1, keepdims=True))
        dy_ref[...] = err * (1.0 / Dm)

    row = pl.BlockSpec((ROW_TILE, Dm), lambda i: (i, 0))
    return pl.pallas_call(
        body, name=name, grid=(T // ROW_TILE,), in_specs=[row, row],
        out_specs=(pl.BlockSpec((8, LANES), lambda i: (0, 0)), row),
        out_shape=(jax.ShapeDtypeStruct((8, LANES), F32), jax.ShapeDtypeStruct((T, Dm), F32)),
        compiler_params=_params())(y, target)


def _rope_tables(seq, head_dim):
    axis_dim = head_dim // 2
    half = axis_dim // 2
    pos = jnp.arange(seq, dtype=jnp.int32)
    row = (pos // GRID_W).astype(F32)[:, None]
    col = (pos % GRID_W).astype(F32)[:, None]
    inv = ROPE_BASE ** (-jnp.arange(0, axis_dim, 2, dtype=F32) / axis_dim)
    lane = jnp.arange(head_dim, dtype=jnp.int32)
    within = lane % axis_dim
    ang = jnp.where((lane // axis_dim == 0)[None, :], row, col) * inv[within % half][None, :]
    cos = jnp.cos(ang)
    sin = jnp.where((within < half)[None, :], -jnp.sin(ang), jnp.sin(ang))
    cos = jnp.concatenate([cos, jnp.ones((ROW_TILE, head_dim), F32)], axis=0)
    sin = jnp.concatenate([sin, jnp.zeros((ROW_TILE, head_dim), F32)], axis=0)
    return cos, sin


def _pair_swap(v, half):
    if 2 * half == LANES:
        return pltpu.roll(v, half, axis=1)
    lane = lax.broadcasted_iota(jnp.int32, v.shape, 1)
    return jnp.where((lane % (2 * half)) < half, pltpu.roll(v, LANES - half, axis=1), pltpu.roll(v, half, axis=1))


def _head_sum(v, ones_ref):
    hi = v.astype(MXU_DTYPE)
    lo = (v - hi.astype(F32)).astype(MXU_DTYPE)
    return (jnp.dot(hi, ones_ref[...], preferred_element_type=F32)
            + jnp.dot(lo, ones_ref[...], preferred_element_type=F32))


def _head_ones():
    lane = jnp.arange(LANES)
    return (lane[:, None] // HEAD_DIM == lane[None, :] // HEAD_DIM).astype(MXU_DTYPE)


ATTN_QK_BLOCKS = (N_HEADS + N_KV_HEADS) * HEAD_DIM // LANES
ATTN_ALL_BLOCKS = (N_HEADS + 2 * N_KV_HEADS) * HEAD_DIM // LANES
ATTN_Q_BLOCKS = N_HEADS * HEAD_DIM // LANES
ATTN_SCALE = HEAD_DIM ** -0.5


def _attn_prep_fwd(qkv, gains, cos, sin, tidx, name):
    T, W = qkv.shape

    def body(x_ref, g_ref, cos_ref, sin_ref, ones_ref, o_ref):
        for cb in range(ATTN_ALL_BLOCKS):
            cols = slice(cb * LANES, (cb + 1) * LANES)
            xv = x_ref[:, cols]
            if cb < ATTN_QK_BLOCKS:
                r = lax.rsqrt(_head_sum(xv * xv, ones_ref) * (1.0 / HEAD_DIM) + EPS)
                y = xv * r * g_ref[0 if cb < ATTN_Q_BLOCKS else 1]
                xv = y * cos_ref[...] + _pair_swap(y, HEAD_DIM // 4) * sin_ref[...]
                if cb < ATTN_Q_BLOCKS:
                    xv = xv * ATTN_SCALE
            o_ref[:, cols] = xv.astype(o_ref.dtype)

    row = pl.BlockSpec((ROW_TILE, W), lambda i: (i, 0))
    tab = pl.BlockSpec((ROW_TILE, LANES), lambda i: (tidx(i), 0))
    return pl.pallas_call(
        body, name=name, grid=(T // ROW_TILE,),
        in_specs=[row, pl.BlockSpec((2, 1, LANES), lambda i: (0, 0, 0)), tab, tab,
                  pl.BlockSpec((LANES, LANES), lambda i: (0, 0))],
        out_specs=row, out_shape=jax.ShapeDtypeStruct(qkv.shape, MXU_DTYPE),
        compiler_params=_params())(qkv, gains, cos, sin, _head_ones())


def _attn_prep_bwd(dqk, dv, qkv, gains, cos, sin, tidx, name):
    T, W = qkv.shape
    qk_w = ATTN_QK_BLOCKS * LANES

    def body(dqk_ref, dv_ref, x_ref, g_ref, cos_ref, sin_ref, ones_ref, o_ref, dg_ref):
        @pl.when(pl.program_id(0) == 0)
        def _():
            dg_ref[...] = jnp.zeros_like(dg_ref)

        for cb in range(ATTN_QK_BLOCKS):
            cols = slice(cb * LANES, (cb + 1) * LANES)
            xv, d = x_ref[:, cols], dqk_ref[:, cols]
            if cb < ATTN_Q_BLOCKS:
                d = d * ATTN_SCALE
            r = lax.rsqrt(_head_sum(xv * xv, ones_ref) * (1.0 / HEAD_DIM) + EPS)
            xn = xv * r
            dy = d * cos_ref[...] + _pair_swap(d * sin_ref[...], HEAD_DIM // 4)
            dg_ref[:, cols] += jnp.sum(dy * xn, axis=0, keepdims=True)
            dxn = dy * g_ref[0 if cb < ATTN_Q_BLOCKS else 1]
            dx = r * (dxn - xn * (_head_sum(dxn * xn, ones_ref) * (1.0 / HEAD_DIM)))
            o_ref[:, cols] = dx.astype(o_ref.dtype)
        o_ref[:, qk_w:] = dv_ref[...].astype(o_ref.dtype)

    row = lambda w: pl.BlockSpec((ROW_TILE, w), lambda i: (i, 0))
    tab = pl.BlockSpec((ROW_TILE, LANES), lambda i: (tidx(i), 0))
    return pl.pallas_call(
        body, name=name, grid=(T // ROW_TILE,),
        in_specs=[row(qk_w), row(W - qk_w), row(W), pl.BlockSpec((2, 1, LANES), lambda i: (0, 0, 0)), tab, tab,
                  pl.BlockSpec((LANES, LANES), lambda i: (0, 0))],
        out_specs=(row(W), pl.BlockSpec((1, qk_w), lambda i: (0, 0))),
        out_shape=(jax.ShapeDtypeStruct(qkv.shape, MXU_DTYPE), jax.ShapeDtypeStruct((1, qk_w), F32)),
        compiler_params=_params())(dqk, dv, qkv, gains, cos, sin, _head_ones())


RET_QK_BLOCKS = 2 * RET_HEADS * RET_QK_DIM // LANES


def _ret_rope(x, cos, sin, tidx, name):
    T = x.shape[0]
    W = RET_QK_BLOCKS * LANES
    k_scale = RET_QK_DIM ** -0.5

    def body(x_ref, cos_ref, sin_ref, o_ref):
        for cb in range(RET_QK_BLOCKS):
            cols = slice(cb * LANES, (cb + 1) * LANES)
            tcols = slice((cb % 2) * LANES, (cb % 2 + 1) * LANES)
            xv = x_ref[:, cols]
            out = xv * cos_ref[:, tcols] + pltpu.roll(xv, LANES // 2, axis=1) * sin_ref[:, tcols]
            if cb >= RET_QK_BLOCKS // 2:
                out = out * k_scale
            o_ref[:, cols] = out

    row = pl.BlockSpec((ROW_TILE, W), lambda i: (i, 0))
    tab = pl.BlockSpec((ROW_TILE, RET_QK_DIM), lambda i: (tidx(i), 0))
    return pl.pallas_call(
        body, name=name, grid=(T // ROW_TILE,), in_specs=[row, tab, tab], out_specs=row,
        out_shape=jax.ShapeDtypeStruct((T, W), F32), compiler_params=_params())(x, cos, sin)


ASSEMBLE_ROWS = 256


def _ret_grad_assemble(x_parts, c_parts, dg, cos, sin, seq, name):
    NX, NC = x_parts[0].shape[0], c_parts[0].shape[0]
    T = NX + NC
    rt = ASSEMBLE_ROWS
    nxt = NX // rt
    qk_w = RET_HEADS * RET_QK_DIM
    k_scale = RET_QK_DIM ** -0.5

    def unrotate(d, cos_ref, sin_ref, scale):
        outs = []
        for cb in range(qk_w // LANES):
            cols = slice(cb * LANES, (cb + 1) * LANES)
            tcols = slice((cb % 2) * LANES, (cb % 2 + 1) * LANES)
            dv_ = d[:, cols]
            o = dv_ * cos_ref[:, tcols] + pltpu.roll(dv_ * sin_ref[:, tcols], LANES // 2, axis=1)
            outs.append(o * scale if scale != 1.0 else o)
        return outs

    def body(dqf, dqb, dkf, dkb, dvf, dvb, dg_ref, dkcf, dkcb, dvcf, dvcb, cos_ref, sin_ref, o_ref):
        i = pl.program_id(0)

        def write_k(parts):
            for cb, o in enumerate(parts):
                o_ref[:, qk_w + cb * LANES:qk_w + (cb + 1) * LANES] = o.astype(o_ref.dtype)

        @pl.when(i < nxt)
        def _():
            for cb, o in enumerate(unrotate(dqf[...] + dqb[...], cos_ref, sin_ref, 1.0)):
                o_ref[:, cb * LANES:(cb + 1) * LANES] = o.astype(o_ref.dtype)
            write_k(unrotate(dkf[...] + dkb[...], cos_ref, sin_ref, k_scale))
            o_ref[:, 2 * qk_w:2 * qk_w + RET_VWIDTH] = (dvf[...] + dvb[...]).astype(o_ref.dtype)
            o_ref[:, 2 * qk_w + RET_VWIDTH:] = dg_ref[...].astype(o_ref.dtype)

        @pl.when(i >= nxt)
        def _():
            o_ref[:, :qk_w] = jnp.zeros((rt, qk_w), o_ref.dtype)
            write_k(unrotate(dkcf[...] + dkcb[...], cos_ref, sin_ref, k_scale))
            o_ref[:, 2 * qk_w:2 * qk_w + RET_VWIDTH] = (dvcf[...] + dvcb[...]).astype(o_ref.dtype)
            o_ref[:, 2 * qk_w + RET_VWIDTH:] = jnp.zeros((rt, RET_VWIDTH), o_ref.dtype)

    xs = lambda w: pl.BlockSpec((rt, w), lambda i: (jnp.minimum(i, nxt - 1), 0))
    cs = lambda w: pl.BlockSpec((rt, w), lambda i: (jnp.maximum(i - nxt, 0), 0))
    tab = pl.BlockSpec((rt, RET_QK_DIM), lambda i: (jnp.where(i < nxt, i % (seq // rt), seq // rt), 0))
    return pl.pallas_call(
        body, name=name, grid=(T // rt,),
        in_specs=[xs(qk_w)] * 4 + [xs(RET_VWIDTH)] * 3 + [cs(qk_w)] * 2 + [cs(RET_VWIDTH)] * 2 + [tab, tab],
        out_specs=pl.BlockSpec((rt, 2 * qk_w + 2 * RET_VWIDTH), lambda i: (i, 0)),
        out_shape=jax.ShapeDtypeStruct((T, 2 * qk_w + 2 * RET_VWIDTH), MXU_DTYPE),
        compiler_params=_params())(*x_parts, dg, *c_parts, cos, sin)


def _band_bias(qb, seq):
    nb = seq // qb
    assert nb >= 2
    i = jnp.arange(GQA_GROUP * qb, dtype=jnp.int32)[:, None] % qb
    n = jnp.arange(3 * qb, dtype=jnp.int32)[None, :]
    in_window = (n >= i) & (n - i <= 2 * WINDOW)
    variants = [in_window & (n >= qb), in_window, in_window & (n < 2 * qb)]
    return jnp.stack([jnp.where(v, 0.0, NEG_INF).astype(F32) for v in variants])


GROUP_ORDER = (0, 2, 1, 3)


def _stack_halves(blk):
    return jnp.concatenate([blk[:, :LANES], blk[:, LANES:]], axis=0)


def _unstack_halves(v, rows):
    return jnp.concatenate([v[:rows], v[rows:]], axis=1)


def _align_head(pair, odd):
    lane = lax.broadcasted_iota(jnp.int32, pair.shape, 1)
    mine = jnp.where((lane >= HEAD_DIM) == odd, pair, jnp.zeros_like(pair))
    rolled = pltpu.roll(mine, HEAD_DIM, axis=1)
    return jnp.where(odd, rolled, mine), jnp.where(odd, mine, rolled)


def _scores(q2, x_eo):
    return jnp.concatenate([_dot(q2, x_eo[0], _NT), _dot(q2, x_eo[1], _NT)], axis=0)


def _apply(p, x_eo):
    half = p.shape[0] // 2
    return _dot(p[:half], x_eo[0], _NN) + _dot(p[half:], x_eo[1], _NN)


def _kv_grad(a, q2, odd):
    half = a.shape[0] // 2
    even_part = _dot(a[:half], q2, _TN)
    odd_part = _dot(a[half:], q2, _TN)
    lane = lax.broadcasted_iota(jnp.int32, even_part.shape, 1)
    low = (jnp.where(lane < HEAD_DIM, even_part, 0.0)
           + pltpu.roll(jnp.where(lane >= HEAD_DIM, odd_part, 0.0), HEAD_DIM, axis=1))
    return jnp.where(odd, pltpu.roll(low, HEAD_DIM, axis=1), low)


def _attn_probs(q2, kc_eo, kl_eo, bias, sink_ref, kv_head, qb):
    rows = GQA_GROUP * qb
    s_c = _scores(q2, kc_eo)
    blk = lax.broadcasted_iota(jnp.int32, (rows, 1), 0) // qb
    sink = jnp.zeros((rows, 1), F32)
    for t, gi in enumerate(GROUP_ORDER):
        sink = jnp.where(blk == t, sink_ref[kv_head, gi], sink)
    m = jnp.maximum(jnp.max(s_c, axis=-1, keepdims=True), sink)
    s_l = None
    if kl_eo is not None:
        s_l = _scores(q2, kl_eo) + bias
        m = jnp.maximum(m, jnp.max(s_l, axis=-1, keepdims=True))
    e_c = jnp.exp(s_c - m)
    e_s = jnp.exp(sink - m)
    den = jnp.sum(e_c, axis=-1, keepdims=True) + e_s
    e_l = None
    if kl_eo is not None:
        e_l = jnp.exp(s_l - m)
        den = den + jnp.sum(e_l, axis=-1, keepdims=True)
    inv = 1.0 / den
    return e_c * inv, (None if e_l is None else e_l * inv), e_s * inv


GROUP_W = GQA_GROUP * HEAD_DIM
K_LANE_BLOCK = N_HEADS * HEAD_DIM // LANES
V_LANE_BLOCK = K_LANE_BLOCK + N_KV_HEADS * HEAD_DIM // LANES


def _attn_specs(B, seq, ctx_len, ctx_queries):
    ctx0 = B * seq // ctx_len
    if ctx_queries:
        qb, nb = ctx_len, 1
        qrow = lambda b, j: ctx0 + b
    else:
        qb, nb = ATTN_BLOCK, seq // ATTN_BLOCK
        qrow = lambda b, j: b * nb + j
    q_spec = pl.BlockSpec((qb, GROUP_W), lambda b, k, j: (qrow(b, j), k))
    c_specs = [pl.BlockSpec((ctx_len, LANES), lambda b, k, j: (ctx0 + b, K_LANE_BLOCK + k // 2)),
               pl.BlockSpec((ctx_len, LANES), lambda b, k, j: (ctx0 + b, V_LANE_BLOCK + k // 2))]
    local = []
    if not ctx_queries:
        near = [lambda j: jnp.maximum(j - 1, 0), lambda j: j, lambda j: jnp.minimum(j + 1, nb - 1)]
        for lane0 in (K_LANE_BLOCK, V_LANE_BLOCK):
            for f in near:
                local.append(pl.BlockSpec((qb, LANES), lambda b, k, j, f=f, lane0=lane0: (b * nb + f(j), lane0 + k // 2)))
        local.append(pl.BlockSpec(
            (1, GQA_GROUP * qb, 3 * qb), lambda b, k, j: (jnp.where(j == 0, 0, jnp.where(j == nb - 1, 2, 1)), 0, 0)))
    return qb, nb, qrow, q_spec, c_specs, local


def _attn_operands(refs, has_local, kv_head):
    odd = (kv_head % 2) == 1
    n_local = 7 if has_local else 0
    q2 = _stack_halves(refs[0][...])
    kc = _align_head(refs[1 + n_local][...], odd)
    vc = _align_head(refs[2 + n_local][...], odd)
    kl = vl = bias = None
    if has_local:
        kl = _align_head(jnp.concatenate([r[...] for r in refs[1:4]], axis=0), odd)
        vl = _align_head(jnp.concatenate([r[...] for r in refs[4:7]], axis=0), odd)
        bias = refs[7][0]
    return odd, q2, kc, vc, kl, vl, bias


def _attn_fwd(qkv, sink, B, seq, ctx_len, ctx_queries, name):
    has_local = not ctx_queries
    qb, nb, _, q_spec, c_specs, local = _attn_specs(B, seq, ctx_len, ctx_queries)
    n_rows = B * (ctx_len if ctx_queries else seq)

    def body(*refs):
        sink_ref, o_ref = refs[-2:]
        kv_head = pl.program_id(1)
        _, q2, kc, vc, kl, vl, bias = _attn_operands(refs, has_local, kv_head)
        p_c, p_l, _ = _attn_probs(q2, kc, kl, bias, sink_ref, kv_head, qb)
        o2 = _apply(p_c, vc)
        if has_local:
            o2 = o2 + _apply(p_l, vl)
        o_ref[...] = _unstack_halves(o2, qb).astype(o_ref.dtype)

    operands = [qkv] + ([qkv] * 6 + [_band_bias(qb, seq)] if has_local else []) + [qkv, qkv, sink]
    return pl.pallas_call(
        body, name=name, grid=(B, N_KV_HEADS, nb),
        in_specs=[q_spec] + local + c_specs + [_SMEM],
        out_specs=pl.BlockSpec((qb, GROUP_W), lambda b, k, j: (b * nb + j, k)),
        out_shape=jax.ShapeDtypeStruct((n_rows, N_HEADS * HEAD_DIM), MXU_DTYPE), compiler_params=_params())(*operands)


def _attn_bwd(qkv, sink, do, B, seq, ctx_len, ctx_queries, name):
    has_local = not ctx_queries
    qb, nb, qrow, q_spec, c_specs, local = _attn_specs(B, seq, ctx_len, ctx_queries)
    n_rows = B * (ctx_len if ctx_queries else seq)

    def body(*refs):
        n_in = 1 + (7 if has_local else 0) + 4
        sink_ref, do_ref = refs[n_in - 2:n_in]
        outs = refs[n_in:]
        dq_ref = outs[0]
        dkc_ref, dvc_ref, dsink_ref = outs[-3:]
        b, kv_head, j = pl.program_id(0), pl.program_id(1), pl.program_id(2)
        odd, q2, kc, vc, kl, vl, bias = _attn_operands(refs, has_local, kv_head)
        do2 = _stack_halves(do_ref[...])
        p_c, p_l, p_s = _attn_probs(q2, kc, kl, bias, sink_ref, kv_head, qb)
        dp_c = _scores(do2, vc)
        delta = jnp.sum(p_c * dp_c, axis=-1, keepdims=True)
        if has_local:
            dp_l = _scores(do2, vl)
            delta = delta + jnp.sum(p_l * dp_l, axis=-1, keepdims=True)
        ds_c = p_c * (dp_c - delta)
        dq2 = _apply(ds_c, kc)

        @pl.when((kv_head % 2 == 0) & (j == 0))
        def _():
            dkc_ref[...] = jnp.zeros_like(dkc_ref)
            dvc_ref[...] = jnp.zeros_like(dvc_ref)
            if has_local:
                outs[1][...] = jnp.zeros_like(outs[1])
                outs[2][...] = jnp.zeros_like(outs[2])

        @pl.when((b == 0) & (kv_head == 0) & (j == 0))
        def _():
            dsink_ref[...] = jnp.zeros_like(dsink_ref)

        dkc_ref[...] += _kv_grad(ds_c, q2, odd)
        dvc_ref[...] += _kv_grad(p_c, do2, odd)
        if has_local:
            ds_l = p_l * (dp_l - delta)
            dq2 = dq2 + _apply(ds_l, kl)
            dkl = _kv_grad(ds_l, q2, odd)
            dvl = _kv_grad(p_l, do2, odd)
            dk_ref, dv_ref = outs[1], outs[2]
            for t in range(3):
                def add(t=t):
                    start = pl.multiple_of((j - 1 + t) * qb, qb)
                    dk_ref[pl.ds(start, qb), :] += dkl[t * qb:(t + 1) * qb]
                    dv_ref[pl.ds(start, qb), :] += dvl[t * qb:(t + 1) * qb]
                if t == 0:
                    pl.when(j > 0)(add)
                elif t == 2:
                    pl.when(j < nb - 1)(add)
                else:
                    add()
        dq_ref[...] = _unstack_halves(dq2, qb)
        dsk = -(p_s * delta)
        sub = lax.broadcasted_iota(jnp.int32, (8, LANES), 0)
        tile = jnp.zeros((8, LANES), F32)
        for t, gi in enumerate(GROUP_ORDER):
            tile = jnp.where(sub == gi, jnp.sum(dsk[t * qb:(t + 1) * qb]), tile)
        dsink_ref[pl.ds(pl.multiple_of(kv_head * 8, 8), 8), :] += tile

    kv_w = N_KV_HEADS * HEAD_DIM
    seq_spec = pl.BlockSpec((seq, LANES), lambda b, k, j: (b, k // 2))
    ctx_spec = pl.BlockSpec((ctx_len, LANES), lambda b, k, j: (b, k // 2))
    do_spec = pl.BlockSpec((qb, GROUP_W), lambda b, k, j: (qrow(b, j), k))
    operands = [qkv] + ([qkv] * 6 + [_band_bias(qb, seq)] if has_local else []) + [qkv, qkv, sink, do]
    out_specs = ([pl.BlockSpec((qb, GROUP_W), lambda b, k, j: (b * nb + j, k))] + ([seq_spec, seq_spec] if has_local else [])
                 + [ctx_spec, ctx_spec, pl.BlockSpec((32, LANES), lambda b, k, j: (0, 0))])
    out_shape = ([jax.ShapeDtypeStruct((n_rows, N_HEADS * HEAD_DIM), F32)]
                 + ([jax.ShapeDtypeStruct((B * seq, kv_w), F32)] * 2 if has_local else [])
                 + [jax.ShapeDtypeStruct((B * ctx_len, kv_w), F32)] * 2 + [jax.ShapeDtypeStruct((32, LANES), F32)])
    return pl.pallas_call(
        body, name=name, grid=(B, N_KV_HEADS, nb),
        in_specs=[q_spec] + local + c_specs + [_SMEM, do_spec],
        out_specs=tuple(out_specs), out_shape=tuple(out_shape), compiler_params=_params())(*operands)


def _ret_decays(lg, rev):
    n = lax.broadcasted_iota(jnp.int32, (RET_CHUNK, RET_CHUNK), 0).astype(F32)
    m = lax.broadcasted_iota(jnp.int32, (RET_CHUNK, RET_CHUNK), 1).astype(F32)
    pos = lax.broadcasted_iota(jnp.int32, (RET_CHUNK, 1), 0).astype(F32)
    diff = (m - n) if rev else (n - m)
    a_exp = jnp.maximum(diff, 0.0)
    intra = jnp.where(diff >= 0, jnp.exp(lg * a_exp), 0.0)
    q_exp = (RET_CHUNK - pos) if rev else (pos + 1.0)
    k_exp = pos if rev else (RET_CHUNK - 1.0 - pos)
    chunk = jnp.exp(jnp.full((1, 1), RET_CHUNK, F32) * lg)
    return intra, a_exp, jnp.exp(lg * q_exp), q_exp, jnp.exp(lg * k_exp), k_exp, chunk


def _ctx_decay(lg, ctx_len, rev):
    t = lax.broadcasted_iota(jnp.int32, (ctx_len, 1), 0).astype(F32)
    expo = t if rev else (ctx_len - 1.0 - t)
    return jnp.exp(lg * expo), expo


def _ret_specs(B, seq, ctx_len, order):
    nc = seq // RET_CHUNK
    x_blocks = B * seq // ctx_len

    def rows(b, c):
        return b * nc + order(c, nc)

    q_spec = pl.BlockSpec((RET_CHUNK, RET_QK_DIM), lambda b, h, c: (rows(b, c), h))
    k_spec = pl.BlockSpec((RET_CHUNK, RET_QK_DIM), lambda b, h, c: (rows(b, c), RET_HEADS + h))
    v_spec = pl.BlockSpec((RET_CHUNK, RET_V_DIM), lambda b, h, c: (rows(b, c), RET_HEADS + h))
    kc_spec = pl.BlockSpec((ctx_len, RET_QK_DIM), lambda b, h, c: (x_blocks + b, RET_HEADS + h))
    vc_spec = pl.BlockSpec((ctx_len, RET_V_DIM), lambda b, h, c: (x_blocks + b, RET_HEADS + h))
    st_spec = pl.BlockSpec((1, 1, 1, RET_QK_DIM, RET_V_DIM), lambda b, h, c: (b, h, order(c, nc), 0, 0))
    o_spec = pl.BlockSpec((RET_CHUNK, RET_V_DIM), lambda b, h, c: (rows(b, c), h))
    return nc, q_spec, k_spec, v_spec, kc_spec, vc_spec, st_spec, o_spec


_SCAN_UP = lambda c, nc: c
_SCAN_DOWN = lambda c, nc: nc - 1 - c


def _ret_fwd(qk, qkvg, log_g, B, seq, ctx_len, name):
    nc, qf, kf, vf, kc_spec, vc_spec, stf, of = _ret_specs(B, seq, ctx_len, _SCAN_UP)
    _, qr, kr, vr, _, _, str_, or_ = _ret_specs(B, seq, ctx_len, _SCAN_DOWN)

    def body(lg_ref, qf_ref, kf_ref, vf_ref, qr_ref, kr_ref, vr_ref, kc_ref, vc_ref,
             of_ref, stf_ref, or_ref, str_ref, state_f, state_r):
        h, c = pl.program_id(1), pl.program_id(2)
        dirs = ((False, lg_ref[0, h], qf_ref, kf_ref, vf_ref, of_ref, stf_ref, state_f),
                (True, lg_ref[1, h], qr_ref, kr_ref, vr_ref, or_ref, str_ref, state_r))

        @pl.when(c == 0)
        def _():
            for rev, lg, _, _, _, _, _, state in dirs:
                dec, _ = _ctx_decay(lg, ctx_len, rev)
                state[...] = _dot(kc_ref[...] * dec, vc_ref[...], _TN)

        for rev, lg, q_ref, k_ref, v_ref, o_ref, st_ref, state in dirs:
            intra, _, q_dec, _, k_dec, _, chunk_dec = _ret_decays(lg, rev)
            qv, kv, vv = q_ref[...], k_ref[...], v_ref[...]
            s_in = state[...]
            st_ref[0, 0, 0] = s_in
            w = _dot(qv, kv, _NT) * intra
            o_ref[...] = _dot(w, vv, _NN) + _dot(qv, s_in, _NN) * q_dec
            state[...] = s_in * chunk_dec + _dot(kv * k_dec, vv, _TN)

    o_shape = jax.ShapeDtypeStruct((B * seq, RET_VWIDTH), F32)
    st_shape = jax.ShapeDtypeStruct((B, RET_HEADS, nc, RET_QK_DIM, RET_V_DIM), F32)
    return pl.pallas_call(
        body, name=name, grid=(B, RET_HEADS, nc),
        in_specs=[_SMEM, qf, kf, vf, qr, kr, vr, kc_spec, vc_spec],
        out_specs=(of, stf, or_, str_), out_shape=(o_shape, st_shape, o_shape, st_shape),
        scratch_shapes=[pltpu.VMEM((RET_QK_DIM, RET_V_DIM), F32)] * 2,
        compiler_params=_params())(log_g, qk, qk, qkvg, qk, qk, qkvg, qk, qkvg)


def _ret_bwd_chunk(rev, lg, q_ref, k_ref, v_ref, st_ref, do_ref, dq_ref, dk_ref, dv_ref, dlg_ref, dstate):
    intra, a_exp, q_dec, q_exp, k_dec, k_exp, chunk_dec = _ret_decays(lg, rev)
    qv, kv, vv, dov = q_ref[...], k_ref[...], v_ref[...], do_ref[...]
    s_in, ds_out = st_ref[0, 0, 0], dstate[...]
    p = _dot(qv, kv, _NT)
    w = p * intra
    dw = _dot(dov, vv, _NT)
    dp = dw * intra
    do_dec = dov * q_dec
    kd = kv * k_dec
    v_ds = _dot(vv, ds_out, _NT)
    dq_ref[...] = _dot(dp, kv, _NN) + _dot(do_dec, s_in, _NT)
    dk_ref[...] = _dot(dp, qv, _TN) + v_ds * k_dec
    dv_ref[...] = _dot(w, dov, _TN) + _dot(kd, ds_out, _NN)
    q_s = _dot(qv, s_in, _NN)
    dlg = (jnp.sum(dw * w * a_exp)
           + jnp.sum(q_exp * q_dec * jnp.sum(dov * q_s, axis=-1, keepdims=True))
           + jnp.sum(k_exp * k_dec * jnp.sum(kv * v_ds, axis=-1, keepdims=True))
           + RET_CHUNK * jnp.sum(chunk_dec * (ds_out * s_in)))
    ds_in = ds_out * chunk_dec + _dot(qv, do_dec, _TN)
    dstate[...] = ds_in
    dlg_ref[...] += dlg
    return ds_in


def _ret_bwd(qk, qkvg, log_g, st_f, st_r, do, B, seq, ctx_len, name):
    nc, qf, kf, vf, kc_spec, vc_spec, stf, of = _ret_specs(B, seq, ctx_len, _SCAN_DOWN)
    _, qr, kr, vr, _, _, str_, or_ = _ret_specs(B, seq, ctx_len, _SCAN_UP)

    def body(lg_ref, qf_ref, kf_ref, vf_ref, stf_ref, dof_ref, qr_ref, kr_ref, vr_ref, str_ref, dor_ref, kc_ref, vc_ref,
             dqf, dkf, dvf, dkcf, dvcf, dlgf, dqr, dkr, dvr, dkcr, dvcr, dlgr, dstate_f, dstate_r):
        h, c = pl.program_id(1), pl.program_id(2)
        dirs = ((False, lg_ref[0, h], (qf_ref, kf_ref, vf_ref, stf_ref, dof_ref, dqf, dkf, dvf, dlgf, dstate_f), dkcf, dvcf),
                (True, lg_ref[1, h], (qr_ref, kr_ref, vr_ref, str_ref, dor_ref, dqr, dkr, dvr, dlgr, dstate_r), dkcr, dvcr))

        @pl.when(c == 0)
        def _():
            for _, _, refs, _, _ in dirs:
                refs[-1][...] = jnp.zeros_like(refs[-1])
                refs[-2][...] = jnp.zeros_like(refs[-2])

        ds_first = [_ret_bwd_chunk(rev, lg, *refs) for rev, lg, refs, _, _ in dirs]

        @pl.when(c == nc - 1)
        def _():
            for (rev, lg, refs, dkc_ref, dvc_ref), ds_in in zip(dirs, ds_first):
                dec, expo = _ctx_decay(lg, ctx_len, rev)
                kcv, vcv = kc_ref[...], vc_ref[...]
                vc_ds = _dot(vcv, ds_in, _NT)
                dkc_ref[...] = vc_ds * dec
                dvc_ref[...] = _dot(kcv * dec, ds_in, _NN)
                refs[-2][...] += jnp.sum(expo * dec * jnp.sum(kcv * vc_ds, axis=-1, keepdims=True))

    def outs(q_spec, o_spec):
        return (pl.BlockSpec((RET_CHUNK, RET_QK_DIM), q_spec.index_map),
                pl.BlockSpec((RET_CHUNK, RET_QK_DIM), q_spec.index_map), o_spec,
                pl.BlockSpec((ctx_len, RET_QK_DIM), lambda b, h, c: (b, h)),
                pl.BlockSpec((ctx_len, RET_V_DIM), lambda b, h, c: (b, h)),
                pl.BlockSpec((1, 1, 8, LANES), lambda b, h, c: (b, h, 0, 0)))

    shapes = (jax.ShapeDtypeStruct((B * seq, RET_HEADS * RET_QK_DIM), F32),
              jax.ShapeDtypeStruct((B * seq, RET_HEADS * RET_QK_DIM), F32),
              jax.ShapeDtypeStruct((B * seq, RET_VWIDTH), F32),
              jax.ShapeDtypeStruct((B * ctx_len, RET_HEADS * RET_QK_DIM), F32),
              jax.ShapeDtypeStruct((B * ctx_len, RET_VWIDTH), F32),
              jax.ShapeDtypeStruct((B, RET_HEADS, 8, LANES), F32))
    res = pl.pallas_call(
        body, name=name, grid=(B, RET_HEADS, nc),
        in_specs=[_SMEM, qf, kf, vf, stf, of, qr, kr, vr, str_, or_, kc_spec, vc_spec],
        out_specs=outs(qf, of) + outs(qr, or_), out_shape=shapes + shapes,
        scratch_shapes=[pltpu.VMEM((RET_QK_DIM, RET_V_DIM), F32)] * 2,
        compiler_params=_params())(log_g, qk, qk, qkvg, st_f, do, qk, qk, qkvg, st_r, do, qk, qkvg)
    return res[:6], res[6:]


def _gated_out_fwd(o_f, o_b, qkvg, gn_gain, name):
    T = o_f.shape[0]
    g_off = (2 * RET_HEADS * RET_QK_DIM + RET_VWIDTH) // RET_V_DIM

    def body(of_ref, ob_ref, g_ref, gain_ref, z_ref):
        o = of_ref[...] + ob_ref[...]
        mu = jnp.mean(o, axis=-1, keepdims=True)
        var = jnp.mean(jnp.square(o - mu), axis=-1, keepdims=True)
        y = (o - mu) * lax.rsqrt(var + EPS) * gain_ref[...]
        gv = g_ref[...]
        z_ref[...] = (gv * jax.nn.sigmoid(gv) * y).astype(z_ref.dtype)

    blk = pl.BlockSpec((ROW_TILE, RET_V_DIM), lambda i, h: (i, h))
    return pl.pallas_call(
        body, name=name, grid=(T // ROW_TILE, RET_HEADS),
        in_specs=[blk, blk, pl.BlockSpec((ROW_TILE, RET_V_DIM), lambda i, h: (i, g_off + h)),
                  pl.BlockSpec((1, RET_V_DIM), lambda i, h: (0, h))],
        out_specs=blk, out_shape=jax.ShapeDtypeStruct((T, RET_VWIDTH), MXU_DTYPE),
        compiler_params=_params())(o_f, o_b, qkvg, gn_gain)


def _gated_out_bwd(dz, o_f, o_b, qkvg, gn_gain, name):
    T = o_f.shape[0]
    g_off = (2 * RET_HEADS * RET_QK_DIM + RET_VWIDTH) // RET_V_DIM

    def body(dz_ref, of_ref, ob_ref, g_ref, gain_ref, do_ref, dg_ref, dgain_ref):
        o = of_ref[...] + ob_ref[...]
        mu = jnp.mean(o, axis=-1, keepdims=True)
        var = jnp.mean(jnp.square(o - mu), axis=-1, keepdims=True)
        rstd = lax.rsqrt(var + EPS)
        yhat = (o - mu) * rstd
        gv, dzv = g_ref[...], dz_ref[...]
        sg = jax.nn.sigmoid(gv)
        dg_ref[...] = (dzv * (yhat * gain_ref[...]) * (sg * (1.0 + gv * (1.0 - sg)))).astype(dg_ref.dtype)
        dy = dzv * (gv * sg)

        @pl.when(pl.program_id(1) == 0)
        def _():
            dgain_ref[...] = jnp.zeros_like(dgain_ref)

        dgain_ref[...] += jnp.sum(dy * yhat, axis=0, keepdims=True)
        dyh = dy * gain_ref[...]
        do_ref[...] = rstd * (dyh - jnp.mean(dyh, axis=-1, keepdims=True)
                              - yhat * jnp.mean(dyh * yhat, axis=-1, keepdims=True))

    blk = pl.BlockSpec((ROW_TILE, RET_V_DIM), lambda h, i: (i, h))
    vec = pl.BlockSpec((1, RET_V_DIM), lambda h, i: (0, h))
    return pl.pallas_call(
        body, name=name, grid=(RET_HEADS, T // ROW_TILE),
        in_specs=[blk, blk, blk, pl.BlockSpec((ROW_TILE, RET_V_DIM), lambda h, i: (i, g_off + h)), vec],
        out_specs=(blk, blk, vec),
        out_shape=(jax.ShapeDtypeStruct((T, RET_VWIDTH), F32), jax.ShapeDtypeStruct((T, RET_VWIDTH), MXU_DTYPE),
                   jax.ShapeDtypeStruct((1, RET_VWIDTH), F32)),
        compiler_params=_params())(dz, o_f, o_b, qkvg, gn_gain)


def _adamw(w, m, v, parts, name):
    R, C = w.shape
    tr = _tile(R, (256, 128, 64, 32, 16, 8))
    n_parts = [p.shape[0] for p in parts]

    def body(*refs):
        w_ref, m_ref, v_ref = refs[:3]
        part_refs = refs[3:3 + len(parts)]
        g_ref, d_ref, nm_ref, nv_ref = refs[3 + len(parts):]
        g = None
        for ref, n in zip(part_refs, n_parts):
            for r in range(n):
                term = ref[r].astype(F32)
                g = term if g is None else g + term
        mn = ADAM_B1 * m_ref[...] + (1.0 - ADAM_B1) * g
        vn = ADAM_B2 * v_ref[...] + (1.0 - ADAM_B2) * jnp.square(g)
        m_hat = mn / (1.0 - ADAM_B1 ** ADAM_STEP)
        v_hat = vn / (1.0 - ADAM_B2 ** ADAM_STEP)
        g_ref[...] = g
        d_ref[...] = -ADAM_LR * (m_hat / (jnp.sqrt(v_hat) + ADAM_EPS) + ADAM_WD * w_ref[...])
        nm_ref[...] = mn
        nv_ref[...] = vn

    blk = pl.BlockSpec((tr, C), lambda i: (i, 0))
    part_specs = [pl.BlockSpec((n, tr, C), lambda i: (0, i, 0)) for n in n_parts]
    shp = jax.ShapeDtypeStruct((R, C), F32)
    return pl.pallas_call(
        body, name=name, grid=(R // tr,), in_specs=[blk, blk, blk] + part_specs,
        out_specs=(blk, blk, blk, blk), out_shape=(shp, shp, shp, shp),
        compiler_params=_params())(w, m, v, *parts)


def _sum_rows(parts, name):
    n, R, C = parts.shape
    tr = _tile(R, (256, 128, 64, 32, 16, 8))

    def body(p_ref, o_ref):
        acc = p_ref[0]
        for r in range(1, n):
            acc = acc + p_ref[r]
        o_ref[...] = acc

    return pl.pallas_call(
        body, name=name, grid=(R // tr,), in_specs=[pl.BlockSpec((n, tr, C), lambda i: (0, i, 0))],
        out_specs=pl.BlockSpec((tr, C), lambda i: (i, 0)), out_shape=jax.ShapeDtypeStruct((R, C), F32),
        compiler_params=_params())(parts)


def _my_coords():
    return lax.axis_index("x"), lax.axis_index("y"), lax.axis_index("c")


def _flip(coord, bit):
    return 1 - coord if bit else coord


def _all_gather(x2d, name):
    R, C = x2d.shape

    def body(x_ref, out_ref, send_sems, recv_sems, local_sem):
        x, y, c = _my_coords()
        me, sibling = (x, y, c), (x, y, 1 - c)
        chips = [(1 - x, y), (x, 1 - y), (1 - x, 1 - y)]

        def rows(px, py, pc):
            return out_ref.at[4 * px + 2 * py + pc]

        def copy(k, block, to, src=None):
            return pltpu.make_async_remote_copy(
                src_ref=rows(*block) if src is None else src, dst_ref=rows(*block),
                send_sem=send_sems.at[k], recv_sem=recv_sems.at[k], device_id=to, device_id_type=MESH)

        mine = pltpu.make_async_copy(x_ref, rows(*me), local_sem)
        mine.start()
        first = [copy(0, me, sibling, src=x_ref)]
        first += [copy(1 + j, me, (*chip, c), src=x_ref) for j, chip in enumerate(chips)]
        for cp in first:
            cp.start()
        passed = [copy(4 + j, (*chip, c), sibling) for j, chip in enumerate(chips)]
        for j, chip in enumerate(chips):
            copy(1 + j, (*chip, c), me).wait_recv()
            passed[j].start()
        copy(0, sibling, me).wait_recv()
        for j, chip in enumerate(chips):
            copy(4 + j, (*chip, 1 - c), me).wait_recv()
        for cp in first + passed:
            cp.wait_send()
        mine.wait()

    return pl.pallas_call(
        body, name=name, out_shape=jax.ShapeDtypeStruct((N_DEV, R, C), x2d.dtype),
        in_specs=[_ANY], out_specs=_ANY,
        scratch_shapes=[pltpu.SemaphoreType.DMA((7,)), pltpu.SemaphoreType.DMA((7,)), pltpu.SemaphoreType.DMA],
    )(x2d)


BIG_WEIGHTS = {
    "ffn_w_in": (2, (2, D_MODEL, 2 * D_FF)),
    "ffn_w_out": (1, (2, D_FF, D_MODEL)),
    "attn_w_qkv": (2, (1, D_MODEL, (N_HEADS + 2 * N_KV_HEADS) * HEAD_DIM)),
    "attn_w_o": (1, (1, N_HEADS * HEAD_DIM, D_MODEL)),
    "ret_w_qkvg": (2, (1, D_MODEL, 2 * D_MODEL + 2 * RET_VWIDTH)),
    "ret_gn_g": (2, (1, 1, RET_VWIDTH)),
    "ret_w_o": (1, (1, RET_VWIDTH, D_MODEL)),
}


def _join_shards(name, stacked):
    axis, full = BIG_WEIGHTS[name]
    if axis == 2:
        stacked = stacked.transpose(0, 2, 1, 3)
    return stacked.reshape(full)


def _split_shards(name, full_arr):
    axis, (_, rows, cols) = BIG_WEIGHTS[name]
    L = full_arr.shape[0]
    if axis == 2:
        return full_arr.reshape(L, rows, N_DEV, cols // N_DEV).transpose(0, 2, 1, 3)
    return full_arr.reshape(L, N_DEV, rows // N_DEV, cols)


def _gather_shards(shards, name):
    n = len(shards)

    def body(*refs):
        x_refs, out_refs = refs[:n], refs[n:2 * n]
        send_sems, recv_sems, local_sems = refs[2 * n:]
        x, y, c = _my_coords()
        me, sibling = (x, y, c), (x, y, 1 - c)
        chips = [(1 - x, y), (x, 1 - y), (1 - x, 1 - y)]

        def rows(a, px, py, pc):
            return out_refs[a].at[:, 4 * px + 2 * py + pc]

        def copy(a, k, block, to, src=None):
            return pltpu.make_async_remote_copy(
                src_ref=rows(a, *block) if src is None else src, dst_ref=rows(a, *block),
                send_sem=send_sems.at[7 * a + k], recv_sem=recv_sems.at[7 * a + k], device_id=to, device_id_type=MESH)

        mine = [pltpu.make_async_copy(x_refs[a], rows(a, *me), local_sems.at[a]) for a in range(n)]
        for cp in mine:
            cp.start()
        first = []
        for a in range(n):
            first.append(copy(a, 0, me, sibling, src=x_refs[a]))
            first += [copy(a, 1 + j, me, (*chip, c), src=x_refs[a]) for j, chip in enumerate(chips)]
        for cp in first:
            cp.start()
        passed = []
        for j, chip in enumerate(chips):
            for a in range(n):
                copy(a, 1 + j, (*chip, c), me).wait_recv()
                fwd = copy(a, 4 + j, (*chip, c), sibling)
                fwd.start()
                passed.append(fwd)
        for a in range(n):
            copy(a, 0, sibling, me).wait_recv()
            for j, chip in enumerate(chips):
                copy(a, 4 + j, (*chip, 1 - c), me).wait_recv()
        for cp in first + passed:
            cp.wait_send()
        for cp in mine:
            cp.wait()

    return pl.pallas_call(
        body, name=name,
        out_shape=[jax.ShapeDtypeStruct((s.shape[0], N_DEV) + s.shape[1:], s.dtype) for s in shards],
        in_specs=[_ANY] * n, out_specs=[_ANY] * n,
        scratch_shapes=[pltpu.SemaphoreType.DMA((7 * n,)), pltpu.SemaphoreType.DMA((7 * n,)),
                        pltpu.SemaphoreType.DMA((n,))],
    )(*shards)


def _exchange_shards(arrs, masks, src_of, out_tail, name):
    n, nm = len(arrs), len(masks)

    def body(*refs):
        in_refs, out_refs = refs[:n], refs[n:2 * n]
        send_sems, recv_sems = refs[2 * n:]
        x, y, c = _my_coords()
        copies = []
        for a in range(n):
            for k, (bx, by, bc) in enumerate(masks):
                peer = (_flip(x, bx), _flip(y, by), _flip(c, bc))
                copies.append(pltpu.make_async_remote_copy(
                    src_ref=src_of(in_refs[a], peer, (x, y, c)), dst_ref=out_refs[a].at[k],
                    send_sem=send_sems.at[nm * a + k], recv_sem=recv_sems.at[nm * a + k],
                    device_id=peer, device_id_type=MESH))
        for cp in copies:
            cp.start()
        for cp in copies:
            cp.wait()

    return pl.pallas_call(
        body, name=name,
        out_shape=[jax.ShapeDtypeStruct((nm,) + out_tail(s), s.dtype) for s in arrs],
        in_specs=[_ANY] * n, out_specs=[_ANY] * n,
        scratch_shapes=[pltpu.SemaphoreType.DMA((nm * n,)), pltpu.SemaphoreType.DMA((nm * n,))],
    )(*arrs)


def _pair_sum(g, from_sibling, core, out_dtype, name):
    L, _, _, a, b = g.shape
    ta = a

    def body(core_ref, g_ref, s_ref, o_ref):
        o_ref[...] = (g_ref[...] + s_ref[...]).astype(out_dtype)

    blk = pl.BlockSpec((1, 1, ta, b), lambda l, q, i, core_ref: (l, q, i, 0))
    return pl.pallas_call(
        body, name=name,
        grid_spec=pltpu.PrefetchScalarGridSpec(
            num_scalar_prefetch=1, grid=(L, 4, a // ta),
            in_specs=[pl.BlockSpec((1, 1, pl.Squeezed(), ta, b), lambda l, q, i, core_ref: (l, q, core_ref[0], i, 0)), blk],
            out_specs=blk),
        out_shape=jax.ShapeDtypeStruct((L, 4, a, b), out_dtype), compiler_params=_params())(core, g, from_sibling)


def _mods(mod_x, mod_c, layer):
    both = jnp.concatenate([mod_x[:, layer], mod_c[layer][None]], axis=0)
    return [both[:, None, k * D_MODEL:(k + 1) * D_MODEL] for k in range(6)]


def _local_step(x, ctx, target, mod_x, mod_c, w, small, late_weights=None, hooks=None):
    B, S, _ = x.shape
    L = ctx.shape[1]
    NX, NC = B * S, B * L
    T = NX + NC
    tiles_per_ex = S // ROW_TILE
    nxt = NX // ROW_TILE
    gidx = _group_index(nxt, tiles_per_ex, B)
    gidx_for = lambda rows: _group_index(NX // rows, S // rows, B)
    mm_rows = _tile(S, (MM_ROWS, ROW_TILE))
    tidx = lambda i: jnp.where(i < nxt, i % tiles_per_ex, tiles_per_ex)
    G = B + 1
    x0 = jnp.concatenate([x.reshape(NX, D_MODEL), ctx.reshape(NC, D_MODEL)], axis=0)
    acos, asin = [jnp.tile(t, (1, LANES // HEAD_DIM)) for t in _rope_tables(S, HEAD_DIM)]
    rcos, rsin = _rope_tables(S, RET_QK_DIM)
    sink = small["attn_sink"].reshape(N_KV_HEADS, GQA_GROUP)
    gains = jnp.stack([jnp.tile(small["attn_q_norm"].reshape(1, HEAD_DIM), (1, LANES // HEAD_DIM)),
                       jnp.tile(small["attn_k_norm"].reshape(1, HEAD_DIM), (1, LANES // HEAD_DIM))])
    log_g = jax.nn.log_sigmoid(small["ret_decay_logit"].reshape(2, RET_HEADS))
    n1, n2 = small["norm1_g"], small["norm2_g"]

    m0 = _mods(mod_x, mod_c, 0)
    h1 = _norm_mod_fwd(x0, n1[0:1], m0[0], m0[1], gidx, "l0_norm1")
    qkv = _mm(h1, w["attn_w_qkv"][0], "nn", F32, "l0_qkv")
    qkv_r = _attn_prep_fwd(qkv, gains, acos, asin, tidx, "l0_qk_prep")
    o_x = _attn_fwd(qkv_r, sink, B, S, L, False, "l0_attn_x")
    o_c = _attn_fwd(qkv_r, sink, B, S, L, True, "l0_attn_c")
    o0 = jnp.concatenate([o_x, o_c], axis=0)
    mo0, x1 = _mm(o0, w["attn_w_o"][0], "nn", F32, "l0_attn_out", res=x0, gate=m0[2], gidx_for=gidx_for, gate_rows=mm_rows)
    h2 = _norm_mod_fwd(x1, n2[0:1], m0[3], m0[4], gidx, "l0_norm2")
    if late_weights is not None:
        w = {**w, **late_weights(x1)}
    u0 = _mm(h2, w["ffn_w_in"][0], "nn", F32, "l0_ffn_in")
    a0 = _swiglu_fwd(u0, "l0_swiglu")
    f0, x2 = _mm(a0, w["ffn_w_out"][0], "nn", F32, "l0_ffn_out", res=x1, gate=m0[5], gidx_for=gidx_for, gate_rows=mm_rows)

    m1 = _mods(mod_x, mod_c, 1)
    g1 = _norm_mod_fwd(x2, n1[1:2], m1[0], m1[1], gidx, "l1_norm1")
    qkvg = _mm(g1, w["ret_w_qkvg"][0], "nn", F32, "l1_qkvg")
    qk = _ret_rope(qkvg, rcos, rsin, tidx, "l1_rope")
    of, st_f, ob, st_b = _ret_fwd(qk, qkvg, log_g, B, S, L, "l1_ret")
    gn = w["ret_gn_g"].reshape(1, RET_VWIDTH)
    z1 = _gated_out_fwd(of, ob, qkvg, gn, "l1_gated_out")
    xx2 = x2[:NX]
    gx = lambda i: i // tiles_per_ex
    m1x = [t[:B] for t in m1]
    mo1, y1 = _mm(z1, w["ret_w_o"][0], "nn", F32, "l1_ret_out", res=xx2, gate=m1x[2], gidx_for=gidx_for, gate_rows=mm_rows)
    k2 = _norm_mod_fwd(y1, n2[1:2], m1x[3], m1x[4], gx, "l1_norm2")
    u1 = _mm(k2, w["ffn_w_in"][1], "nn", F32, "l1_ffn_in")
    a1 = _swiglu_fwd(u1, "l1_swiglu")
    f1, y2 = _mm(a1, w["ffn_w_out"][1], "nn", F32, "l1_ffn_out", res=y1, gate=m1x[5], gidx_for=gidx_for, gate_rows=mm_rows)

    loss_tile, dy2 = _loss_fwd_bwd(y2, target.reshape(NX, D_MODEL), "loss")

    zg = jnp.zeros((1, 1, D_MODEL), F32)
    dz, dgate5_1 = _gate_bwd(dy2, f1, m1x[5], gx, B, "l1_ffn_gate_bwd")
    gw_ffn_out1 = _mm(a1, dz, "tn", F32, "l1_ffn_out_dw")
    da = _mm(dz, w["ffn_w_out"][1], "nt", F32, "l1_ffn_out_dx")
    du = _swiglu_bwd(da, u1, "l1_swiglu_bwd")
    gw_ffn_in1 = _mm(k2, du, "tn", F32, "l1_ffn_in_dw")
    dk2 = _mm(du, w["ffn_w_in"][1], "nt", F32, "l1_ffn_in_dx")
    dy1, dsh3_1, dsc4_1, dn2_1 = _norm_mod_bwd(dk2, y1, n2[1:2], m1x[4], dy2, gx, B, "l1_norm2_bwd")
    dzo, dgate2_1 = _gate_bwd(dy1, mo1, m1x[2], gx, B, "l1_ret_gate_bwd")
    gw_ret_o = _mm(z1, dzo, "tn", F32, "l1_ret_out_dw")
    dz1 = _mm(dzo, w["ret_w_o"][0], "nt", F32, "l1_ret_out_dx")
    do_r, dg_r, dgn = _gated_out_bwd(dz1, of, ob, qkvg, gn, "l1_gated_out_bwd")
    ((dq_f, dk_f, dv_f, dkc_f, dvc_f, dlg_f),
     (dq_b, dk_b, dv_b, dkc_b, dvc_b, dlg_b)) = _ret_bwd(qk, qkvg, log_g, st_f, st_b, do_r, B, S, L, "l1_ret_bwd")
    dqkvg = _ret_grad_assemble((dq_f, dq_b, dk_f, dk_b, dv_f, dv_b), (dkc_f, dkc_b, dvc_f, dvc_b), dg_r, rcos, rsin, S,
                               "l1_qkvg_grad")
    gw_ret_qkvg = _mm(g1, dqkvg, "tn", F32, "l1_qkvg_dw")
    grads_layer1 = {
        "ffn_w_in": gw_ffn_in1[None],
        "ffn_w_out": gw_ffn_out1[None],
        "ret_w_qkvg": gw_ret_qkvg[None],
        "ret_gn_g": dgn.reshape(1, 1, RET_VWIDTH),
        "ret_w_o": gw_ret_o[None],
    }
    if hooks is not None:
        m0[5] = hooks.layer1_grads(grads_layer1, m0[5])
    dg1 = _mm(dqkvg, w["ret_w_qkvg"][0], "nt", F32, "l1_qkvg_dx")
    dres1 = jnp.concatenate([dy1, jnp.zeros((NC, D_MODEL), F32)], axis=0)
    dx2, dsh0_1, dsc1_1, dn1_1 = _norm_mod_bwd(dg1, x2, n1[1:2], m1[1], dres1, gidx, G, "l1_norm1_bwd")
    dlg = jnp.stack([jnp.sum(dlg_f[:, :, 0, 0], axis=0), jnp.sum(dlg_b[:, :, 0, 0], axis=0)])
    d_decay = (dlg * jax.nn.sigmoid(-small["ret_decay_logit"].reshape(2, RET_HEADS))).reshape(1, 2, RET_HEADS)

    dz, dgate5_0 = _gate_bwd(dx2, f0, m0[5], gidx, G, "l0_ffn_gate_bwd")
    gw_ffn_out0 = _mm(a0, dz, "tn", F32, "l0_ffn_out_dw")
    da = _mm(dz, w["ffn_w_out"][0], "nt", F32, "l0_ffn_out_dx")
    du = _swiglu_bwd(da, u0, "l0_swiglu_bwd")
    if hooks is not None:
        m0[4] = hooks.mid_backward(du, m0[4])
    gw_ffn_in0 = _mm(h2, du, "tn", F32, "l0_ffn_in_dw")
    dh2 = _mm(du, w["ffn_w_in"][0], "nt", F32, "l0_ffn_in_dx")
    dx1, dsh3_0, dsc4_0, dn2_0 = _norm_mod_bwd(dh2, x1, n2[0:1], m0[4], dx2, gidx, G, "l0_norm2_bwd")
    dzo, dgate2_0 = _gate_bwd(dx1, mo0, m0[2], gidx, G, "l0_attn_gate_bwd")
    gw_attn_o = _mm(o0, dzo, "tn", F32, "l0_attn_out_dw")
    do0 = _mm(dzo, w["attn_w_o"][0], "nt", MXU_DTYPE, "l0_attn_out_dx")
    dq_x, dk_x, dv_x, dkc1, dvc1, dsink_x = _attn_bwd(qkv_r, sink, do0, B, S, L, False, "l0_attn_x_bwd")
    dq_c, dkc2, dvc2, dsink_c = _attn_bwd(qkv_r, sink, do0, B, S, L, True, "l0_attn_c_bwd")
    dqk = jnp.concatenate([jnp.concatenate([dq_x, dk_x], axis=1), jnp.concatenate([dq_c, dkc1 + dkc2], axis=1)], axis=0)
    dvv = jnp.concatenate([dv_x, dvc1 + dvc2], axis=0)
    dqkv, dgains = _attn_prep_bwd(dqk, dvv, qkv, gains, acos, asin, tidx, "l0_qk_prep_bwd")
    gw_attn_qkv = _mm(h1, dqkv, "tn", F32, "l0_qkv_dw")
    dh1 = _mm(dqkv, w["attn_w_qkv"][0], "nt", F32, "l0_qkv_dx")
    dx0, dsh0_0, dsc1_0, dn1_0 = _norm_mod_bwd(dh1, x0, n1[0:1], m0[1], dx1, gidx, G, "l0_norm1_bwd")

    dgains = jnp.sum(dgains.reshape(ATTN_QK_BLOCKS, LANES // HEAD_DIM, HEAD_DIM), axis=1)
    dsink = (dsink_x + dsink_c).reshape(N_KV_HEADS, 8, LANES)[:, :GQA_GROUP, 0].reshape(1, N_HEADS)
    grads_layer0 = {
        "ffn_w_in": gw_ffn_in0[None],
        "ffn_w_out": gw_ffn_out0[None],
        "attn_w_qkv": gw_attn_qkv[None],
        "attn_w_o": gw_attn_o[None],
    }
    grads_small = {
        "norm1_g": jnp.concatenate([dn1_0, dn1_1], axis=0),
        "norm2_g": jnp.concatenate([dn2_0, dn2_1], axis=0),
        "attn_q_norm": jnp.sum(dgains[:ATTN_Q_BLOCKS], axis=0)[None],
        "attn_k_norm": jnp.sum(dgains[ATTN_Q_BLOCKS:ATTN_QK_BLOCKS], axis=0)[None],
        "attn_sink": dsink,
        "ret_decay_logit": d_decay,
    }

    def pad_g(t):
        return jnp.concatenate([t, zg], axis=0)

    d0 = jnp.concatenate([dsh0_0, dsc1_0, dgate2_0, dsh3_0, dsc4_0, dgate5_0], axis=2)[:, 0]
    d1 = jnp.concatenate([dsh0_1, dsc1_1, pad_g(dgate2_1), pad_g(dsh3_1), pad_g(dsc4_1), pad_g(dgate5_1)],
                         axis=2)[:, 0]
    dmod_x = jnp.stack([d0[:B], d1[:B]], axis=1)
    dmod_c = jnp.stack([d0[B], d1[B]], axis=0)
    return loss_tile, dx0[:NX].reshape(B, S, D_MODEL), (grads_layer0, grads_layer1), grads_small, dmod_x, dmod_c


SMALL_NAMES = ("c_ctx", "ada_b", "norm1_g", "norm2_g", "attn_q_norm", "attn_k_norm", "attn_sink", "ret_decay_logit")
ADA_ROWS = 64


def _pack_small(d, rows):
    flat = jnp.concatenate([d[k].reshape(-1) for k in SMALL_NAMES])
    n = rows * LANES
    return jnp.pad(flat, (0, n - flat.shape[0])).reshape(rows, LANES)


def _unpack_small(packed, shapes):
    flat = packed.reshape(-1)
    out, off = {}, 0
    for k in SMALL_NAMES:
        n = math.prod(shapes[k])
        out[k] = flat[off:off + n].reshape(shapes[k])
        off += n
    return out


EARLY_WEIGHTS = ("attn_w_qkv", "attn_w_o")
LATE_WEIGHTS = tuple(k for k in BIG_WEIGHTS if k not in EARLY_WEIGHTS)

_HBM = pl.BlockSpec(memory_space=pltpu.HBM)
_SEM = pl.BlockSpec(memory_space=pltpu.SEMAPHORE)
_DATAFLOW = pltpu.SideEffectType.DATAFLOW_SIDE_EFFECTING
_PEER_FLIPS = ((0, 0, 1), (0, 1, 0), (0, 1, 1), (1, 0, 0), (1, 0, 1), (1, 1, 0), (1, 1, 1))


def _wire_shard(name, t):
    return t.reshape(1, 1, -1) if name == "ret_gn_g" else t.astype(MXU_DTYPE)


def _direct_copies(x_refs, land_refs, send_sems, recv_sems, landing):
    x, y, c = _my_coords()
    out = []
    for a in range(len(x_refs)):
        for k, (bx, by, bc) in enumerate(_PEER_FLIPS):
            peer = (_flip(x, bx), _flip(y, by), _flip(c, bc))
            slot = (4 * peer[0] + 2 * peer[1] + peer[2]) if landing else (4 * x + 2 * y + c)
            out.append(pltpu.make_async_remote_copy(
                src_ref=x_refs[a], dst_ref=land_refs[a].at[:, slot], send_sem=send_sems.at[7 * a + k],
                recv_sem=recv_sems.at[7 * a + k], device_id=peer, device_id_type=MESH))
    return out


def _gather_start(shards, name):
    n = len(shards)
    lands = [lax.empty((s.shape[0], N_DEV) + s.shape[1:], s.dtype) for s in shards]

    def body(*refs):
        send_sems, recv_sems = refs[2 * n], refs[2 * n + 1]
        x_refs, land_refs = refs[2 * n + 2:3 * n + 2], refs[3 * n + 2:4 * n + 2]
        for cp in _direct_copies(x_refs, land_refs, send_sems, recv_sems, landing=False):
            cp.start()
        refs[-1][...] = jnp.zeros_like(refs[-1])

    hbm = lambda t: pltpu.with_memory_space_constraint(t, pltpu.HBM)
    res = pl.pallas_call(
        body, name=name,
        out_shape=(pltpu.SemaphoreType.DMA((7 * n,)), pltpu.SemaphoreType.DMA((7 * n,)))
        + tuple(pltpu.HBM(t.shape, t.dtype) for t in shards + lands) + (jax.ShapeDtypeStruct((8, LANES), F32),),
        in_specs=[_HBM] * (2 * n), out_specs=(_SEM, _SEM) + (_HBM,) * (2 * n) + (pl.BlockSpec(memory_space=pltpu.VMEM),),
        input_output_aliases={i: 2 + i for i in range(2 * n)},
        compiler_params=pltpu.CompilerParams(has_side_effects=_DATAFLOW))(*[hbm(t) for t in shards + lands])
    return res[0], res[1], list(res[2:2 + n]), list(res[2 + n:2 + 2 * n]), res[-1]


def _gather_wait(send_sems, recv_sems, shards, lands, after, name):
    n = len(shards)

    def body(*refs):
        x_refs, land_refs = refs[:n], refs[n:2 * n]
        for cp in _direct_copies(x_refs, land_refs, refs[2 * n], refs[2 * n + 1], landing=True):
            cp.wait_send()
            cp.wait_recv()

    res = pl.pallas_call(
        body, name=name, out_shape=tuple(pltpu.HBM(t.shape, t.dtype) for t in shards + lands),
        in_specs=[_HBM] * (2 * n) + [_SEM, _SEM, _ANY], out_specs=(_HBM,) * (2 * n),
        input_output_aliases={i: i for i in range(2 * n)},
        compiler_params=pltpu.CompilerParams(has_side_effects=_DATAFLOW))(*shards, *lands, send_sems, recv_sems, after)
    return list(res[n:])


def _gather_big_weights(weights, names, name):
    gathered = _gather_shards([_wire_shard(k, weights[k]) for k in names], name)
    return {k: _join_shards(k, g) for k, g in zip(names, gathered)}


_SIBLING = ((0, 0, 1),)
_CHIPS = ((1, 0, 0), (0, 1, 0), (1, 1, 0))
_to_sibling = lambda ref, peer: ref.at[:, :, peer[2]]
_to_chip = lambda ref, peer: ref.at[:, 2 * peer[0] + peer[1]]
_sibling_tail = lambda s: (s.shape[0], 4) + s.shape[3:]
_chip_tail = lambda s: (s.shape[0],) + s.shape[2:]


def _rs_split(grads):
    names = list(grads)
    split = []
    for k in names:
        s = _split_shards(k, grads[k])
        split.append(s.reshape(s.shape[0], 4, 2, s.shape[2], s.shape[3]))
    return names, split


def _rs_pair_sums(names, split, from_sibling, tag):
    core = lax.axis_index("c").astype(jnp.int32).reshape(1)
    return [_pair_sum(g, s, core, MXU_DTYPE, tag + k) for k, g, s in zip(names, split, from_sibling)]


def _rs_parts(names, split, from_sibling, from_chips):
    mx_, my_, mc_ = _my_coords()
    my_chip = 2 * mx_ + my_
    parts = {}
    for k, g, s, r in zip(names, split, from_sibling, from_chips):
        own_keep = lax.dynamic_index_in_dim(lax.dynamic_index_in_dim(g, my_chip, axis=1, keepdims=False), mc_, axis=1,
                                            keepdims=False)
        parts[k] = (own_keep, lax.dynamic_index_in_dim(s, my_chip, axis=1, keepdims=False), r)
    return parts


def _reduce_scatter_in_call(grads, tag):
    names, split = _rs_split(grads)
    from_sibling = [t[0] for t in _exchange_shards(split, _SIBLING, lambda ref, peer, me_: _to_sibling(ref, peer),
                                                   _sibling_tail, tag + "sibling")]
    pair = _rs_pair_sums(names, split, from_sibling, tag + "pair_")
    from_chips = _exchange_shards(pair, _CHIPS, lambda ref, peer, me_: _to_chip(ref, peer), _chip_tail, tag + "chips")
    return _rs_parts(names, split, from_sibling, from_chips)


def _exchange_copies(in_refs, land_refs, send_sems, recv_sems, masks, src_of):
    x, y, c = _my_coords()
    nm = len(masks)
    out = []
    for a in range(len(in_refs)):
        for k, (bx, by, bc) in enumerate(masks):
            peer = (_flip(x, bx), _flip(y, by), _flip(c, bc))
            out.append(pltpu.make_async_remote_copy(
                src_ref=src_of(in_refs[a], peer), dst_ref=land_refs[a].at[k], send_sem=send_sems.at[nm * a + k],
                recv_sem=recv_sems.at[nm * a + k], device_id=peer, device_id_type=MESH))
    return out


def _exchange_start(arrs, masks, src_of, out_tail, name):
    n, nm = len(arrs), len(masks)
    lands = [lax.empty((nm,) + out_tail(s), s.dtype) for s in arrs]

    def body(*refs):
        send_sems, recv_sems = refs[2 * n], refs[2 * n + 1]
        in_refs, land_refs = refs[2 * n + 2:3 * n + 2], refs[3 * n + 2:4 * n + 2]
        for cp in _exchange_copies(in_refs, land_refs, send_sems, recv_sems, masks, src_of):
            cp.start()
        refs[-1][...] = jnp.zeros_like(refs[-1])

    hbm = lambda t: pltpu.with_memory_space_constraint(t, pltpu.HBM)
    res = pl.pallas_call(
        body, name=name,
        out_shape=(pltpu.SemaphoreType.DMA((nm * n,)), pltpu.SemaphoreType.DMA((nm * n,)))
        + tuple(pltpu.HBM(t.shape, t.dtype) for t in list(arrs) + lands) + (jax.ShapeDtypeStruct((8, LANES), F32),),
        in_specs=[_HBM] * (2 * n), out_specs=(_SEM, _SEM) + (_HBM,) * (2 * n) + (pl.BlockSpec(memory_space=pltpu.VMEM),),
        input_output_aliases={i: 2 + i for i in range(2 * n)},
        compiler_params=pltpu.CompilerParams(has_side_effects=_DATAFLOW))(*[hbm(t) for t in list(arrs) + lands])
    return (res[0], res[1], list(res[2:2 + n]), list(res[2 + n:2 + 2 * n]), masks, src_of), res[-1]


def _exchange_wait(state, after, name):
    send_sems, recv_sems, arrs, lands, masks, src_of = state
    n = len(arrs)

    def body(*refs):
        for cp in _exchange_copies(refs[:n], refs[n:2 * n], refs[2 * n], refs[2 * n + 1], masks, src_of):
            cp.wait_send()
            cp.wait_recv()

    res = pl.pallas_call(
        body, name=name, out_shape=tuple(pltpu.HBM(t.shape, t.dtype) for t in arrs + lands),
        in_specs=[_HBM] * (2 * n) + [_SEM, _SEM, _ANY], out_specs=(_HBM,) * (2 * n),
        input_output_aliases={i: i for i in range(2 * n)},
        compiler_params=pltpu.CompilerParams(has_side_effects=_DATAFLOW))(*arrs, *lands, send_sems, recv_sems, after)
    return list(res[:n]), list(res[n:])


def _adamw_big(weights, mom1, mom2, part_groups):
    big = {}
    for k in BIG_WEIGHTS:
        parts = [g[k] for g in part_groups if k in g]
        own_keep = jnp.concatenate([p[0] for p in parts], axis=0)
        own_sib = jnp.concatenate([p[1] for p in parts], axis=0)
        recv = jnp.concatenate([p[2] for p in parts], axis=1)
        L_, a_, b_ = own_keep.shape
        rows = L_ * a_
        res = _adamw(weights[k].reshape(rows, b_), mom1[k].reshape(rows, b_), mom2[k].reshape(rows, b_),
                     [own_keep.reshape(1, rows, b_), own_sib.reshape(1, rows, b_), recv.reshape(3, rows, b_)],
                     "adamw_" + k)
        big[k] = [t.reshape(weights[k].shape) for t in res]
    return big


def kernel(x, c, ctx, c_ctx, ada_w, ada_b, norm1_g, norm2_g, ffn_w_in, ffn_w_out, attn_w_qkv, attn_q_norm, attn_k_norm, attn_sink, attn_w_o, ret_w_qkvg, ret_decay_logit, ret_gn_g, ret_w_o, loss_target, m_c_ctx, m_ada_w, m_ada_b, m_norm1_g, m_norm2_g, m_ffn_w_in, m_ffn_w_out, m_attn_w_qkv, m_attn_q_norm, m_attn_k_norm, m_attn_sink, m_attn_w_o, m_ret_w_qkvg, m_ret_decay_logit, m_ret_gn_g, m_ret_w_o, v_c_ctx, v_ada_w, v_ada_b, v_norm1_g, v_norm2_g, v_ffn_w_in, v_ffn_w_out, v_attn_w_qkv, v_attn_q_norm, v_attn_k_norm, v_attn_sink, v_attn_w_o, v_ret_w_qkvg, v_ret_decay_logit, v_ret_gn_g, v_ret_w_o):
    weights = dict(c_ctx=c_ctx, ada_w=ada_w, ada_b=ada_b, norm1_g=norm1_g, norm2_g=norm2_g, ffn_w_in=ffn_w_in,
                   ffn_w_out=ffn_w_out, attn_w_qkv=attn_w_qkv, attn_q_norm=attn_q_norm, attn_k_norm=attn_k_norm,
                   attn_sink=attn_sink, attn_w_o=attn_w_o, ret_w_qkvg=ret_w_qkvg, ret_decay_logit=ret_decay_logit,
                   ret_gn_g=ret_gn_g, ret_w_o=ret_w_o)
    mom1 = dict(c_ctx=m_c_ctx, ada_w=m_ada_w, ada_b=m_ada_b, norm1_g=m_norm1_g, norm2_g=m_norm2_g, ffn_w_in=m_ffn_w_in,
                ffn_w_out=m_ffn_w_out, attn_w_qkv=m_attn_w_qkv, attn_q_norm=m_attn_q_norm, attn_k_norm=m_attn_k_norm,
                attn_sink=m_attn_sink, attn_w_o=m_attn_w_o, ret_w_qkvg=m_ret_w_qkvg, ret_decay_logit=m_ret_decay_logit,
                ret_gn_g=m_ret_gn_g, ret_w_o=m_ret_w_o)
    mom2 = dict(c_ctx=v_c_ctx, ada_w=v_ada_w, ada_b=v_ada_b, norm1_g=v_norm1_g, norm2_g=v_norm2_g, ffn_w_in=v_ffn_w_in,
                ffn_w_out=v_ffn_w_out, attn_w_qkv=v_attn_w_qkv, attn_q_norm=v_attn_q_norm, attn_k_norm=v_attn_k_norm,
                attn_sink=v_attn_sink, attn_w_o=v_attn_w_o, ret_w_qkvg=v_ret_w_qkvg, ret_decay_logit=v_ret_decay_logit,
                ret_gn_g=v_ret_gn_g, ret_w_o=v_ret_w_o)
    B = x.shape[0]
    mx_, my_, mc_ = _my_coords()
    me = 4 * mx_ + 2 * my_ + mc_
    ada_cols = ada_w.shape[2]

    w_full = _gather_big_weights(weights, EARLY_WEIGHTS, "gather_early")

    c_all = _all_gather(jax.nn.silu(c), "gather_c").reshape(N_DEV * B, D_MODEL)
    cc_act = jax.nn.silu(c_ctx)[None]
    ada_in = jnp.concatenate([c_all, cc_act, jnp.zeros((ADA_ROWS - N_DEV * B - 1, D_MODEL), F32)], axis=0)
    ada_in = ada_in.astype(MXU_DTYPE)
    ada_w2 = jnp.concatenate([ada_w[0], ada_w[1]], axis=1)
    bias = lax.dynamic_slice_in_dim(ada_b.reshape(2, N_DEV, ada_cols), me, 1, axis=1).reshape(1, 2 * ada_cols)
    mod_cols = _mm(ada_in, ada_w2, "nn", F32, "ada_fwd", bias=bias)
    mod_all = _all_gather(mod_cols, "gather_mod")
    mod_all = mod_all.reshape(N_DEV, ADA_ROWS, 2, ada_cols).transpose(1, 2, 0, 3).reshape(ADA_ROWS, 2, N_DEV * ada_cols)
    mod_x = lax.dynamic_slice_in_dim(mod_all, me * B, B, axis=0)
    mod_c = mod_all[N_DEV * B]

    order = 0.0 * (mod_c[0, 0] + w_full["attn_w_o"][0, 0, 0].astype(F32))
    late_shards = [_wire_shard(k, weights[k] + order if k == "ret_gn_g" else weights[k]) for k in LATE_WEIGHTS]
    send_sems, recv_sems, late_thru, late_lands, token = _gather_start(late_shards, "gather_late_start")
    mod_x = mod_x + token[0, 0]

    def late_weights(after):
        lands = _gather_wait(send_sems, recv_sems, late_thru, late_lands, after, "gather_late_wait")
        own = [lax.dynamic_update_index_in_dim(land, shard, me, axis=1) for land, shard in zip(lands, late_shards)]
        return {k: _join_shards(k, g) for k, g in zip(LATE_WEIGHTS, own)}

    rs1 = {}

    class Hooks:
        @staticmethod
        def layer1_grads(grads, order_through):
            rs1["names"], split = _rs_split(grads)
            rs1["sibling"], tok = _exchange_start(split, _SIBLING, _to_sibling, _sibling_tail, "rs1_sibling_start")
            return order_through + tok[0, 0]

        @staticmethod
        def mid_backward(after, order_through):
            rs1["split"], lands = _exchange_wait(rs1["sibling"], after, "rs1_sibling_wait")
            rs1["from_sibling"] = [t[0] for t in lands]
            pair = _rs_pair_sums(rs1["names"], rs1["split"], rs1["from_sibling"], "rs1_pair_")
            rs1["chips"], tok = _exchange_start(pair, _CHIPS, _to_chip, _chip_tail, "rs1_chips_start")
            return order_through + tok[0, 0]

    small = {k: weights[k] for k in SMALL_NAMES}
    loss_tile, grad_x, (g_layer0, _), g_small, dmod_x, dmod_c = _local_step(
        x, ctx, loss_target, mod_x, mod_c, w_full, small, late_weights, Hooks)
    _, from_chips1 = _exchange_wait(rs1["chips"], grad_x, "rs1_chips_wait")
    parts1 = _rs_parts(rs1["names"], rs1["split"], rs1["from_sibling"], from_chips1)
    loss = lax.psum(loss_tile[0, 0], ("x", "y", "c"))

    n_mod = 2 * 6 * D_MODEL
    dm_rows = jnp.concatenate([dmod_x.reshape(B, n_mod), dmod_c.reshape(1, n_mod),
                               jnp.zeros((8 - B - 1, n_mod), F32)], axis=0)
    dm_all = _all_gather(dm_rows, "gather_dmod")
    dmc_tot = _sum_rows(dm_all[:, B:B + 1].reshape(N_DEV, 1, n_mod)[:, :, :].reshape(N_DEV, n_mod // LANES, LANES),
                        "sum_dmod_c").reshape(1, n_mod)
    dmod_rows = jnp.concatenate([dm_all[:, :B].reshape(N_DEV * B, n_mod), dmc_tot,
                                 jnp.zeros((ADA_ROWS - N_DEV * B - 1, n_mod), F32)], axis=0)
    dmod_mine = lax.dynamic_slice_in_dim(dmod_rows.reshape(ADA_ROWS, 2, N_DEV, ada_cols), me, 1, axis=2)
    dmod_mine = dmod_mine.reshape(ADA_ROWS, 2 * ada_cols).astype(MXU_DTYPE)
    g_ada2 = _mm(ada_in, dmod_mine, "tn", F32, "ada_dw")
    g_ada_w = jnp.stack([g_ada2[:, :ada_cols], g_ada2[:, ada_cols:]])
    dmc_mine = jnp.concatenate([dmod_mine[N_DEV * B:N_DEV * B + 1], jnp.zeros((7, 2 * ada_cols), MXU_DTYPE)], axis=0)
    dcc_part = _mm(dmc_mine, ada_w2, "nt", F32, "ada_dc")[0:1]
    g_ada_b = _sum_rows(dmod_rows[:, None, :].reshape(ADA_ROWS, n_mod // LANES, LANES), "sum_dmod_b").reshape(2, 6 * D_MODEL)
    sg = jax.nn.sigmoid(c_ctx)
    g_small["c_ctx"] = dcc_part.reshape(D_MODEL) * (sg * (1.0 + c_ctx * (1.0 - sg)))
    g_small["ada_b"] = g_ada_b * (1.0 / N_DEV)

    shapes = {k: weights[k].shape for k in SMALL_NAMES}
    n_small = sum(math.prod(s) for s in shapes.values())
    srows = -(-(-(-n_small // LANES)) // 8) * 8
    gs_all = _all_gather(_pack_small(g_small, srows), "gather_small_grads")
    sm = _adamw(_pack_small({k: weights[k] for k in SMALL_NAMES}, srows), _pack_small({k: mom1[k] for k in SMALL_NAMES}, srows),
                _pack_small({k: mom2[k] for k in SMALL_NAMES}, srows), [gs_all], "adamw_small")
    sm = [_unpack_small(t, shapes) for t in sm]

    ada_shape = ada_w.shape
    r2 = lambda t: t.reshape(ada_shape[0] * ada_shape[1], ada_shape[2])
    ada = [t.reshape(ada_shape) for t in _adamw(r2(ada_w), r2(m_ada_w), r2(v_ada_w), [r2(g_ada_w)[None]], "adamw_ada")]

    big = _adamw_big(weights, mom1, mom2, [_reduce_scatter_in_call(g_layer0, "rs_"), parts1])

    def pick(i, name):
        if name in BIG_WEIGHTS:
            return big[name][i]
        if name == "ada_w":
            return ada[i]
        return sm[i][name]

    order = ("c_ctx", "ada_w", "ada_b", "norm1_g", "norm2_g", "ffn_w_in", "ffn_w_out", "attn_w_qkv", "attn_q_norm",
             "attn_k_norm", "attn_sink", "attn_w_o", "ret_w_qkvg", "ret_decay_logit", "ret_gn_g", "ret_w_o")
    outs = [loss, grad_x]
    for i in range(4):
        outs += [pick(i, n) for n in order]
    return tuple(outs)
```

```python
import functools
import math

import jax
import jax.numpy as jnp
from jax import lax
from jax.experimental import pallas as pl
from jax.experimental.pallas import tpu as pltpu

F32 = jnp.float32
MXU_DTYPE = jnp.bfloat16

D_MODEL = 1024
HEAD_DIM = 64
N_HEADS = 16
N_KV_HEADS = 4
GQA_GROUP = 4
WINDOW = 128
ATTN_BLOCK = 128
RET_HEADS = 4
RET_QK_DIM = 256
RET_V_DIM = 512
RET_VWIDTH = 2048
RET_CHUNK = 256
D_FF = 2816
GRID_W = 64
ROPE_BASE = 10000.0
EPS = 1e-6
NEG_INF = -1e30

ADAM_LR = 0.001
ADAM_B1 = 0.9
ADAM_B2 = 0.999
ADAM_EPS = 1e-08
ADAM_WD = 0.01
ADAM_STEP = 10

N_DEV = 8
LANES = 128
ROW_TILE = 512
VMEM_LIMIT = 48 * 1024 * 1024

MESH = pl.DeviceIdType.MESH
_ANY = pl.BlockSpec(memory_space=pl.ANY)
_SMEM = pl.BlockSpec(memory_space=pltpu.SMEM)


def _params(**kw):
    return pltpu.CompilerParams(vmem_limit_bytes=VMEM_LIMIT, **kw)


def _mx(v):
    return v.astype(MXU_DTYPE)


def _dot(a, b, dims):
    return lax.dot_general(_mx(a), _mx(b), (dims, ((), ())), preferred_element_type=F32)


_NN = ((1,), (0,))
_NT = ((1,), (1,))
_TN = ((0,), (0,))


def _tile(n, cands):
    for c in cands:
        if n % c == 0:
            return c
    return n


def _big_tile(n, cap):
    if n <= cap:
        return n
    for t in range(cap - cap % LANES, 0, -LANES):
        if n % t == 0:
            return t
    return n


MM_ROWS = 1024
MM_COLS = 1408
MM_DEPTH = 2048


def _k_tile(k):
    return _big_tile(k, MM_DEPTH)


def _mm(a, b, mode, out_dtype, name, *, bias=None, res=None, gate=None, gidx_for=None, gate_rows=None):
    if mode == "nn":
        (M, K), (_, N) = a.shape, b.shape
    elif mode == "nt":
        (M, K), (N, _) = a.shape, b.shape
    else:
        (K, M), (_, N) = a.shape, b.shape
    if res is not None:
        tm, tn = gate_rows, _big_tile(N, 512)
        gidx = gidx_for(tm)
    else:
        tm = _big_tile(M, MM_COLS if mode == "tn" else MM_ROWS)
        tn = _big_tile(N, MM_COLS)
    tk = _k_tile(K)
    nk = K // tk
    dims = {"nn": _NN, "nt": _NT, "tn": _TN}[mode]
    a_spec = (pl.BlockSpec((tk, tm), lambda i, j, k: (k, i)) if mode == "tn"
              else pl.BlockSpec((tm, tk), lambda i, j, k: (i, k)))
    b_spec = (pl.BlockSpec((tn, tk), lambda i, j, k: (j, k)) if mode == "nt"
              else pl.BlockSpec((tk, tn), lambda i, j, k: (k, j)))
    o_spec = pl.BlockSpec((tm, tn), lambda i, j, k: (i, j))
    in_specs, operands = [a_spec, b_spec], [a, b]
    if bias is not None:
        in_specs.append(pl.BlockSpec((1, tn), lambda i, j, k: (0, j)))
        operands.append(bias)
    if res is not None:
        in_specs += [o_spec, pl.BlockSpec((1, 1, tn), lambda i, j, k: (gidx(i), 0, j))]
        operands += [res, gate]
        out_shape = (jax.ShapeDtypeStruct((M, N), F32), jax.ShapeDtypeStruct((M, N), F32))
        out_specs = (o_spec, o_spec)
    else:
        out_shape = jax.ShapeDtypeStruct((M, N), out_dtype)
        out_specs = o_spec

    def body(*refs):
        a_ref, b_ref = refs[0], refs[1]
        extra = refs[2:len(operands)]
        outs = refs[len(operands):]
        prod = _dot(a_ref[...], b_ref[...], dims)

        def finish(acc):
            if bias is not None:
                outs[0][...] = (acc + extra[0][...]).astype(out_dtype)
            elif res is not None:
                outs[0][...] = acc
                outs[1][...] = extra[0][...] + extra[1][0] * acc
            else:
                outs[0][...] = acc.astype(out_dtype)

        if nk == 1:
            finish(prod)
        else:
            acc_ref = outs[-1]
            outs = outs[:-1]
            k = pl.program_id(2)

            @pl.when(k == 0)
            def _():
                acc_ref[...] = prod

            @pl.when(k > 0)
            def _():
                acc_ref[...] += prod

            @pl.when(k == nk - 1)
            def _():
                finish(acc_ref[...])

    return pl.pallas_call(
        body, name=name, grid=(M // tm, N // tn, nk), in_specs=in_specs, out_specs=out_specs, out_shape=out_shape,
        scratch_shapes=[pltpu.VMEM((tm, tn), F32)] if nk > 1 else [],
        compiler_params=_params())(*operands)


def _group_index(n_x_tiles, tiles_per_example, n_examples):
    def gidx(i):
        return jnp.where(i < n_x_tiles, i // tiles_per_example, n_examples)
    return gidx


def _norm_mod_fwd(x, g, shift, scale, gidx, name):
    T, Dm = x.shape

    def body(x_ref, g_ref, sh_ref, sc_ref, h_ref):
        xv = x_ref[...]
        r = lax.rsqrt(jnp.mean(xv * xv, axis=-1, keepdims=True) + EPS)
        y = xv * r * g_ref[...]
        h_ref[...] = (y * (1.0 + sc_ref[0]) + sh_ref[0]).astype(h_ref.dtype)

    row = pl.BlockSpec((ROW_TILE, Dm), lambda i: (i, 0))
    mod = pl.BlockSpec((1, 1, Dm), lambda i: (gidx(i), 0, 0))
    return pl.pallas_call(
        body, name=name, grid=(T // ROW_TILE,),
        in_specs=[row, pl.BlockSpec((1, Dm), lambda i: (0, 0)), mod, mod],
        out_specs=row, out_shape=jax.ShapeDtypeStruct((T, Dm), MXU_DTYPE),
        compiler_params=_params())(x, g, shift, scale)


def _first_of_group(i, gidx):
    return jnp.logical_or(i == 0, gidx(i) != gidx(jnp.maximum(i - 1, 0)))


def _norm_mod_bwd(dh, x, g, scale, dres, gidx, n_groups, name):
    T, Dm = x.shape

    def body(dh_ref, x_ref, g_ref, sc_ref, dres_ref, dx_ref, dsh_ref, dsc_ref, dg_ref):
        i = pl.program_id(0)
        xv, dhv = x_ref[...], dh_ref[...]
        r = lax.rsqrt(jnp.mean(xv * xv, axis=-1, keepdims=True) + EPS)
        xn = xv * r
        y = xn * g_ref[...]

        @pl.when(_first_of_group(i, gidx))
        def _():
            dsh_ref[...] = jnp.zeros_like(dsh_ref)
            dsc_ref[...] = jnp.zeros_like(dsc_ref)

        @pl.when(i == 0)
        def _():
            dg_ref[...] = jnp.zeros_like(dg_ref)

        dsh_ref[0] += jnp.sum(dhv, axis=0, keepdims=True)
        dsc_ref[0] += jnp.sum(dhv * y, axis=0, keepdims=True)
        dy = dhv * (1.0 + sc_ref[0])
        dg_ref[...] += jnp.sum(dy * xn, axis=0, keepdims=True)
        dxn = dy * g_ref[...]
        dx = r * (dxn - xn * jnp.mean(dxn * xn, axis=-1, keepdims=True))
        dx_ref[...] = dres_ref[...] + dx

    row = pl.BlockSpec((ROW_TILE, Dm), lambda i: (i, 0))
    mod = pl.BlockSpec((1, 1, Dm), lambda i: (gidx(i), 0, 0))
    vec = pl.BlockSpec((1, Dm), lambda i: (0, 0))
    return pl.pallas_call(
        body, name=name, grid=(T // ROW_TILE,),
        in_specs=[row, row, vec, mod, row],
        out_specs=(row, mod, mod, vec),
        out_shape=(jax.ShapeDtypeStruct((T, Dm), F32), jax.ShapeDtypeStruct((n_groups, 1, Dm), F32),
                   jax.ShapeDtypeStruct((n_groups, 1, Dm), F32), jax.ShapeDtypeStruct((1, Dm), F32)),
        compiler_params=_params())(dh, x, g, scale, dres)


def _gate_bwd(dy, f, gate, gidx, n_groups, name):
    T, Dm = dy.shape

    def body(dy_ref, f_ref, gate_ref, dz_ref, dgate_ref):
        i = pl.program_id(0)
        dyv = dy_ref[...]

        @pl.when(_first_of_group(i, gidx))
        def _():
            dgate_ref[...] = jnp.zeros_like(dgate_ref)

        dgate_ref[0] += jnp.sum(dyv * f_ref[...], axis=0, keepdims=True)
        dz_ref[...] = (dyv * gate_ref[0]).astype(dz_ref.dtype)

    row = pl.BlockSpec((ROW_TILE, Dm), lambda i: (i, 0))
    mod = pl.BlockSpec((1, 1, Dm), lambda i: (gidx(i), 0, 0))
    return pl.pallas_call(
        body, name=name, grid=(T // ROW_TILE,), in_specs=[row, row, mod], out_specs=(row, mod),
        out_shape=(jax.ShapeDtypeStruct((T, Dm), MXU_DTYPE), jax.ShapeDtypeStruct((n_groups, 1, Dm), F32)),
        compiler_params=_params())(dy, f, gate)


SWIGLU_ROWS = 256


def _swiglu_fwd(u, name):
    T = u.shape[0]

    def body(u_ref, a_ref):
        gate, up = u_ref[:, :D_FF], u_ref[:, D_FF:]
        a_ref[...] = (gate * jax.nn.sigmoid(gate) * up).astype(a_ref.dtype)

    return pl.pallas_call(
        body, name=name, grid=(T // SWIGLU_ROWS,),
        in_specs=[pl.BlockSpec((SWIGLU_ROWS, 2 * D_FF), lambda i: (i, 0))],
        out_specs=pl.BlockSpec((SWIGLU_ROWS, D_FF), lambda i: (i, 0)),
        out_shape=jax.ShapeDtypeStruct((T, D_FF), MXU_DTYPE), compiler_params=_params())(u)


def _swiglu_bwd(da, u, name):
    T = u.shape[0]

    def body(da_ref, u_ref, du_ref):
        gate, up, dav = u_ref[:, :D_FF], u_ref[:, D_FF:], da_ref[...]
        sg = jax.nn.sigmoid(gate)
        du_ref[:, :D_FF] = (dav * up * (sg * (1.0 + gate * (1.0 - sg)))).astype(du_ref.dtype)
        du_ref[:, D_FF:] = (dav * gate * sg).astype(du_ref.dtype)

    return pl.pallas_call(
        body, name=name, grid=(T // SWIGLU_ROWS,),
        in_specs=[pl.BlockSpec((SWIGLU_ROWS, D_FF), lambda i: (i, 0)),
                  pl.BlockSpec((SWIGLU_ROWS, 2 * D_FF), lambda i: (i, 0))],
        out_specs=pl.BlockSpec((SWIGLU_ROWS, 2 * D_FF), lambda i: (i, 0)),
        out_shape=jax.ShapeDtypeStruct((T, 2 * D_FF), MXU_DTYPE), compiler_params=_params())(da, u)


def _loss_fwd_bwd(y, target, name):
    T, Dm = y.shape

    def body(y_ref, t_ref, loss_ref, dy_ref):
        err = y_ref[...] - t_ref[...]

        @pl.when(pl.program_id(0) == 0)
        def _():
            loss_ref[...] = jnp.zeros_like(loss_ref)

        loss_ref[...] += 0.5 * jnp.sum(jnp.mean(err * err, axis=-1, keepdims=True))
        dy_ref[...] = err * (1.0 / Dm)

    row = pl.BlockSpec((ROW_TILE, Dm), lambda i: (i, 0))
    return pl.pallas_call(
        body, name=name, grid=(T // ROW_TILE,), in_specs=[row, row],
        out_specs=(pl.BlockSpec((8, LANES), lambda i: (0, 0)), row),
        out_shape=(jax.ShapeDtypeStruct((8, LANES), F32), jax.ShapeDtypeStruct((T, Dm), F32)),
        compiler_params=_params())(y, target)


def _rope_tables(seq, head_dim):
    axis_dim = head_dim // 2
    half = axis_dim // 2
    pos = jnp.arange(seq, dtype=jnp.int32)
    row = (pos // GRID_W).astype(F32)[:, None]
    col = (pos % GRID_W).astype(F32)[:, None]
    inv = ROPE_BASE ** (-jnp.arange(0, axis_dim, 2, dtype=F32) / axis_dim)
    lane = jnp.arange(head_dim, dtype=jnp.int32)
    within = lane % axis_dim
    ang = jnp.where((lane // axis_dim == 0)[None, :], row, col) * inv[within % half][None, :]
    cos = jnp.cos(ang)
    sin = jnp.where((within < half)[None, :], -jnp.sin(ang), jnp.sin(ang))
    cos = jnp.concatenate([cos, jnp.ones((ROW_TILE, head_dim), F32)], axis=0)
    sin = jnp.concatenate([sin, jnp.zeros((ROW_TILE, head_dim), F32)], axis=0)
    return cos, sin


def _pair_swap(v, half):
    if 2 * half == LANES:
        return pltpu.roll(v, half, axis=1)
    lane = lax.broadcasted_iota(jnp.int32, v.shape, 1)
    return jnp.where((lane % (2 * half)) < half, pltpu.roll(v, LANES - half, axis=1), pltpu.roll(v, half, axis=1))


def _head_sum(v, ones_ref):
    hi = v.astype(MXU_DTYPE)
    lo = (v - hi.astype(F32)).astype(MXU_DTYPE)
    return (jnp.dot(hi, ones_ref[...], preferred_element_type=F32)
            + jnp.dot(lo, ones_ref[...], preferred_element_type=F32))


def _head_ones():
    lane = jnp.arange(LANES)
    return (lane[:, None] // HEAD_DIM == lane[None, :] // HEAD_DIM).astype(MXU_DTYPE)


ATTN_QK_BLOCKS = (N_HEADS + N_KV_HEADS) * HEAD_DIM // LANES
ATTN_ALL_BLOCKS = (N_HEADS + 2 * N_KV_HEADS) * HEAD_DIM // LANES
ATTN_Q_BLOCKS = N_HEADS * HEAD_DIM // LANES
ATTN_SCALE = HEAD_DIM ** -0.5


def _attn_prep_fwd(qkv, gains, cos, sin, tidx, name):
    T, W = qkv.shape

    def body(x_ref, g_ref, cos_ref, sin_ref, ones_ref, o_ref):
        for cb in range(ATTN_ALL_BLOCKS):
            cols = slice(cb * LANES, (cb + 1) * LANES)
            xv = x_ref[:, cols]
            if cb < ATTN_QK_BLOCKS:
                r = lax.rsqrt(_head_sum(xv * xv, ones_ref) * (1.0 / HEAD_DIM) + EPS)
                y = xv * r * g_ref[0 if cb < ATTN_Q_BLOCKS else 1]
                xv = y * cos_ref[...] + _pair_swap(y, HEAD_DIM // 4) * sin_ref[...]
                if cb < ATTN_Q_BLOCKS:
                    xv = xv * ATTN_SCALE
            o_ref[:, cols] = xv.astype(o_ref.dtype)

    row = pl.BlockSpec((ROW_TILE, W), lambda i: (i, 0))
    tab = pl.BlockSpec((ROW_TILE, LANES), lambda i: (tidx(i), 0))
    return pl.pallas_call(
        body, name=name, grid=(T // ROW_TILE,),
        in_specs=[row, pl.BlockSpec((2, 1, LANES), lambda i: (0, 0, 0)), tab, tab,
                  pl.BlockSpec((LANES, LANES), lambda i: (0, 0))],
        out_specs=row, out_shape=jax.ShapeDtypeStruct(qkv.shape, MXU_DTYPE),
        compiler_params=_params())(qkv, gains, cos, sin, _head_ones())


def _attn_prep_bwd(dqk, dv, qkv, gains, cos, sin, tidx, name):
    T, W = qkv.shape
    qk_w = ATTN_QK_BLOCKS * LANES

    def body(dqk_ref, dv_ref, x_ref, g_ref, cos_ref, sin_ref, ones_ref, o_ref, dg_ref):
        @pl.when(pl.program_id(0) == 0)
        def _():
            dg_ref[...] = jnp.zeros_like(dg_ref)

        for cb in range(ATTN_QK_BLOCKS):
            cols = slice(cb * LANES, (cb + 1) * LANES)
            xv, d = x_ref[:, cols], dqk_ref[:, cols]
            if cb < ATTN_Q_BLOCKS:
                d = d * ATTN_SCALE
            r = lax.rsqrt(_head_sum(xv * xv, ones_ref) * (1.0 / HEAD_DIM) + EPS)
            xn = xv * r
            dy = d * cos_ref[...] + _pair_swap(d * sin_ref[...], HEAD_DIM // 4)
            dg_ref[:, cols] += jnp.sum(dy * xn, axis=0, keepdims=True)
            dxn = dy * g_ref[0 if cb < ATTN_Q_BLOCKS else 1]
            dx = r * (dxn - xn * (_head_sum(dxn * xn, ones_ref) * (1.0 / HEAD_DIM)))
            o_ref[:, cols] = dx.astype(o_ref.dtype)
        o_ref[:, qk_w:] = dv_ref[...].astype(o_ref.dtype)

    row = lambda w: pl.BlockSpec((ROW_TILE, w), lambda i: (i, 0))
    tab = pl.BlockSpec((ROW_TILE, LANES), lambda i: (tidx(i), 0))
    return pl.pallas_call(
        body, name=name, grid=(T // ROW_TILE,),
        in_specs=[row(qk_w), row(W - qk_w), row(W), pl.BlockSpec((2, 1, LANES), lambda i: (0, 0, 0)), tab, tab,
                  pl.BlockSpec((LANES, LANES), lambda i: (0, 0))],
        out_specs=(row(W), pl.BlockSpec((1, qk_w), lambda i: (0, 0))),
        out_shape=(jax.ShapeDtypeStruct(qkv.shape, MXU_DTYPE), jax.ShapeDtypeStruct((1, qk_w), F32)),
        compiler_params=_params())(dqk, dv, qkv, gains, cos, sin, _head_ones())


RET_QK_BLOCKS = 2 * RET_HEADS * RET_QK_DIM // LANES


def _ret_rope(x, cos, sin, tidx, name):
    T = x.shape[0]
    W = RET_QK_BLOCKS * LANES
    k_scale = RET_QK_DIM ** -0.5

    def body(x_ref, cos_ref, sin_ref, o_ref):
        for cb in range(RET_QK_BLOCKS):
            cols = slice(cb * LANES, (cb + 1) * LANES)
            tcols = slice((cb % 2) * LANES, (cb % 2 + 1) * LANES)
            xv = x_ref[:, cols]
            out = xv * cos_ref[:, tcols] + pltpu.roll(xv, LANES // 2, axis=1) * sin_ref[:, tcols]
            if cb >= RET_QK_BLOCKS // 2:
                out = out * k_scale
            o_ref[:, cols] = out

    row = pl.BlockSpec((ROW_TILE, W), lambda i: (i, 0))
    tab = pl.BlockSpec((ROW_TILE, RET_QK_DIM), lambda i: (tidx(i), 0))
    return pl.pallas_call(
        body, name=name, grid=(T // ROW_TILE,), in_specs=[row, tab, tab], out_specs=row,
        out_shape=jax.ShapeDtypeStruct((T, W), F32), compiler_params=_params())(x, cos, sin)


ASSEMBLE_ROWS = 256


def _ret_grad_assemble(x_parts, c_parts, dg, cos, sin, seq, name):
    NX, NC = x_parts[0].shape[0], c_parts[0].shape[0]
    T = NX + NC
    rt = ASSEMBLE_ROWS
    nxt = NX // rt
    qk_w = RET_HEADS * RET_QK_DIM
    k_scale = RET_QK_DIM ** -0.5

    def unrotate(d, cos_ref, sin_ref, scale):
        outs = []
        for cb in range(qk_w // LANES):
            cols = slice(cb * LANES, (cb + 1) * LANES)
            tcols = slice((cb % 2) * LANES, (cb % 2 + 1) * LANES)
            dv_ = d[:, cols]
            o = dv_ * cos_ref[:, tcols] + pltpu.roll(dv_ * sin_ref[:, tcols], LANES // 2, axis=1)
            outs.append(o * scale if scale != 1.0 else o)
        return outs

    def body(dqf, dqb, dkf, dkb, dvf, dvb, dg_ref, dkcf, dkcb, dvcf, dvcb, cos_ref, sin_ref, o_ref):
        i = pl.program_id(0)

        def write_k(parts):
            for cb, o in enumerate(parts):
                o_ref[:, qk_w + cb * LANES:qk_w + (cb + 1) * LANES] = o.astype(o_ref.dtype)

        @pl.when(i < nxt)
        def _():
            for cb, o in enumerate(unrotate(dqf[...] + dqb[...], cos_ref, sin_ref, 1.0)):
                o_ref[:, cb * LANES:(cb + 1) * LANES] = o.astype(o_ref.dtype)
            write_k(unrotate(dkf[...] + dkb[...], cos_ref, sin_ref, k_scale))
            o_ref[:, 2 * qk_w:2 * qk_w + RET_VWIDTH] = (dvf[...] + dvb[...]).astype(o_ref.dtype)
            o_ref[:, 2 * qk_w + RET_VWIDTH:] = dg_ref[...].astype(o_ref.dtype)

        @pl.when(i >= nxt)
        def _():
            o_ref[:, :qk_w] = jnp.zeros((rt, qk_w), o_ref.dtype)
            write_k(unrotate(dkcf[...] + dkcb[...], cos_ref, sin_ref, k_scale))
            o_ref[:, 2 * qk_w:2 * qk_w + RET_VWIDTH] = (dvcf[...] + dvcb[...]).astype(o_ref.dtype)
            o_ref[:, 2 * qk_w + RET_VWIDTH:] = jnp.zeros((rt, RET_VWIDTH), o_ref.dtype)

    xs = lambda w: pl.BlockSpec((rt, w), lambda i: (jnp.minimum(i, nxt - 1), 0))
    cs = lambda w: pl.BlockSpec((rt, w), lambda i: (jnp.maximum(i - nxt, 0), 0))
    tab = pl.BlockSpec((rt, RET_QK_DIM), lambda i: (jnp.where(i < nxt, i % (seq // rt), seq // rt), 0))
    return pl.pallas_call(
        body, name=name, grid=(T // rt,),
        in_specs=[xs(qk_w)] * 4 + [xs(RET_VWIDTH)] * 3 + [cs(qk_w)] * 2 + [cs(RET_VWIDTH)] * 2 + [tab, tab],
        out_specs=pl.BlockSpec((rt, 2 * qk_w + 2 * RET_VWIDTH), lambda i: (i, 0)),
        out_shape=jax.ShapeDtypeStruct((T, 2 * qk_w + 2 * RET_VWIDTH), MXU_DTYPE),
        compiler_params=_params())(*x_parts, dg, *c_parts, cos, sin)


def _band_bias(qb, seq):
    nb = seq // qb
    assert nb >= 2
    i = jnp.arange(GQA_GROUP * qb, dtype=jnp.int32)[:, None] % qb
    n = jnp.arange(3 * qb, dtype=jnp.int32)[None, :]
    in_window = (n >= i) & (n - i <= 2 * WINDOW)
    variants = [in_window & (n >= qb), in_window, in_window & (n < 2 * qb)]
    return jnp.stack([jnp.where(v, 0.0, NEG_INF).astype(F32) for v in variants])


GROUP_ORDER = (0, 2, 1, 3)


def _stack_halves(blk):
    return jnp.concatenate([blk[:, :LANES], blk[:, LANES:]], axis=0)


def _unstack_halves(v, rows):
    return jnp.concatenate([v[:rows], v[rows:]], axis=1)


def _align_head(pair, odd):
    lane = lax.broadcasted_iota(jnp.int32, pair.shape, 1)
    mine = jnp.where((lane >= HEAD_DIM) == odd, pair, jnp.zeros_like(pair))
    rolled = pltpu.roll(mine, HEAD_DIM, axis=1)
    return jnp.where(odd, rolled, mine), jnp.where(odd, mine, rolled)


def _scores(out_ref, q2, x_eo):
    half = q2.shape[0]
    out_ref[:half, :] = _dot(q2, x_eo[0], _NT)
    out_ref[half:, :] = _dot(q2, x_eo[1], _NT)


def _apply(p_ref, x_eo):
    half = p_ref.shape[0] // 2
    return _dot(p_ref[:half, :], x_eo[0], _NN) + _dot(p_ref[half:, :], x_eo[1], _NN)


def _kv_grad(a_ref, q2, odd):
    half = a_ref.shape[0] // 2
    even_t = _dot(q2, a_ref[:half, :], _TN)
    odd_t = _dot(q2, a_ref[half:, :], _TN)
    mine = even_t[:HEAD_DIM] + odd_t[HEAD_DIM:]
    zero = jnp.zeros_like(mine)
    placed = jnp.where(odd, jnp.concatenate([zero, mine], axis=0), jnp.concatenate([mine, zero], axis=0))
    return placed.T


ATTN_ROW_CHUNK = 32


def _softmax_chunks(s_c_ref, s_l_ref, bias_ref, sink_ref, kv_head, qb, emit):
    for r0 in range(0, GQA_GROUP * qb, ATTN_ROW_CHUNK):
        rows = slice(r0, r0 + ATTN_ROW_CHUNK)
        t = r0 // qb
        sink = jnp.full((ATTN_ROW_CHUNK, 1), sink_ref[kv_head, GROUP_ORDER[t]], F32)
        s_c = s_c_ref[rows, :]
        m = jnp.maximum(jnp.max(s_c, axis=-1, keepdims=True), sink)
        s_l = None
        if s_l_ref is not None:
            s_l = s_l_ref[rows, :] + bias_ref[0, rows, :]
            m = jnp.maximum(m, jnp.max(s_l, axis=-1, keepdims=True))
        e_c = jnp.exp(s_c - m)
        e_s = jnp.exp(sink - m)
        den = jnp.sum(e_c, axis=-1, keepdims=True) + e_s
        e_l = None
        if s_l_ref is not None:
            e_l = jnp.exp(s_l - m)
            den = den + jnp.sum(e_l, axis=-1, keepdims=True)
        inv = 1.0 / den
        emit(t, rows, e_c * inv, (None if e_l is None else e_l * inv), e_s * inv)


GROUP_W = GQA_GROUP * HEAD_DIM
K_LANE_BLOCK = N_HEADS * HEAD_DIM // LANES
V_LANE_BLOCK = K_LANE_BLOCK + N_KV_HEADS * HEAD_DIM // LANES


def _attn_specs(B, seq, ctx_len, ctx_queries):
    ctx0 = B * seq // ctx_len
    if ctx_queries:
        qb, nb = ctx_len, 1
        qrow = lambda b, j: ctx0 + b
    else:
        qb, nb = ATTN_BLOCK, seq // ATTN_BLOCK
        qrow = lambda b, j: b * nb + j
    q_spec = pl.BlockSpec((qb, GROUP_W), lambda b, k, j: (qrow(b, j), k))
    c_specs = [pl.BlockSpec((ctx_len, LANES), lambda b, k, j: (ctx0 + b, K_LANE_BLOCK + k // 2)),
               pl.BlockSpec((ctx_len, LANES), lambda b, k, j: (ctx0 + b, V_LANE_BLOCK + k // 2))]
    local = []
    if not ctx_queries:
        near = [lambda j: jnp.maximum(j - 1, 0), lambda j: j, lambda j: jnp.minimum(j + 1, nb - 1)]
        for lane0 in (K_LANE_BLOCK, V_LANE_BLOCK):
            for f in near:
                local.append(pl.BlockSpec((qb, LANES), lambda b, k, j, f=f, lane0=lane0: (b * nb + f(j), lane0 + k // 2)))
        local.append(pl.BlockSpec(
            (1, GQA_GROUP * qb, 3 * qb), lambda b, k, j: (jnp.where(j == 0, 0, jnp.where(j == nb - 1, 2, 1)), 0, 0)))
    return qb, nb, qrow, q_spec, c_specs, local


def _attn_operands(refs, has_local, kv_head):
    odd = (kv_head % 2) == 1
    n_local = 7 if has_local else 0
    q2 = _stack_halves(refs[0][...])
    kc = _align_head(refs[1 + n_local][...], odd)
    vc = _align_head(refs[2 + n_local][...], odd)
    kl = vl = bias_ref = None
    if has_local:
        kl = _align_head(jnp.concatenate([r[...] for r in refs[1:4]], axis=0), odd)
        vl = _align_head(jnp.concatenate([r[...] for r in refs[4:7]], axis=0), odd)
        bias_ref = refs[7]
    return odd, q2, kc, vc, kl, vl, bias_ref


def _score_scratch(qb, ctx_len, has_local, dtypes):
    rows = GQA_GROUP * qb
    out = []
    for dt in dtypes:
        out.append(pltpu.VMEM((rows, ctx_len), dt))
        if has_local:
            out.append(pltpu.VMEM((rows, 3 * qb), dt))
    return out


def _score_bufs(scratch, has_local):
    if has_local:
        return [(scratch[i], scratch[i + 1]) for i in range(0, len(scratch), 2)]
    return [(s, None) for s in scratch]


def _attn_fwd(qkv, sink, B, seq, ctx_len, ctx_queries, name):
    has_local = not ctx_queries
    qb, nb, _, q_spec, c_specs, local = _attn_specs(B, seq, ctx_len, ctx_queries)
    n_rows = B * (ctx_len if ctx_queries else seq)
    n_in = 1 + (7 if has_local else 0) + 3

    def body(*refs):
        sink_ref, o_ref = refs[n_in - 1], refs[n_in]
        (s_c_ref, s_l_ref), (p_c_ref, p_l_ref) = _score_bufs(refs[n_in + 1:], has_local)
        kv_head = pl.program_id(1)
        _, q2, kc, vc, kl, vl, bias_ref = _attn_operands(refs, has_local, kv_head)
        _scores(s_c_ref, q2, kc)
        if has_local:
            _scores(s_l_ref, q2, kl)

        def emit(t, rows, p_c, p_l, p_s):
            p_c_ref[rows, :] = p_c.astype(p_c_ref.dtype)
            if has_local:
                p_l_ref[rows, :] = p_l.astype(p_l_ref.dtype)

        _softmax_chunks(s_c_ref, s_l_ref, bias_ref, sink_ref, kv_head, qb, emit)
        o2 = _apply(p_c_ref, vc)
        if has_local:
            o2 = o2 + _apply(p_l_ref, vl)
        o_ref[...] = _unstack_halves(o2, qb).astype(o_ref.dtype)

    operands = [qkv] + ([qkv] * 6 + [_band_bias(qb, seq)] if has_local else []) + [qkv, qkv, sink]
    return pl.pallas_call(
        body, name=name, grid=(B, N_KV_HEADS, nb),
        in_specs=[q_spec] + local + c_specs + [_SMEM],
        out_specs=pl.BlockSpec((qb, GROUP_W), lambda b, k, j: (b * nb + j, k)),
        out_shape=jax.ShapeDtypeStruct((n_rows, N_HEADS * HEAD_DIM), MXU_DTYPE),
        scratch_shapes=_score_scratch(qb, ctx_len, has_local, (F32, MXU_DTYPE)),
        compiler_params=_params())(*operands)


def _attn_bwd(qkv, sink, do, B, seq, ctx_len, ctx_queries, name):
    has_local = not ctx_queries
    qb, nb, qrow, q_spec, c_specs, local = _attn_specs(B, seq, ctx_len, ctx_queries)
    n_rows = B * (ctx_len if ctx_queries else seq)

    n_out = 6 if has_local else 4

    def body(*refs):
        n_in = 1 + (7 if has_local else 0) + 4
        sink_ref, do_ref = refs[n_in - 2:n_in]
        outs = refs[n_in:n_in + n_out]
        (s_c_ref, s_l_ref), (dp_c_ref, dp_l_ref), (p_c_ref, p_l_ref), (ds_c_ref, ds_l_ref) = _score_bufs(
            refs[n_in + n_out:], has_local)
        dq_ref = outs[0]
        dkc_ref, dvc_ref, dsink_ref = outs[-3:]
        b, kv_head, j = pl.program_id(0), pl.program_id(1), pl.program_id(2)
        odd, q2, kc, vc, kl, vl, bias_ref = _attn_operands(refs, has_local, kv_head)
        do2 = _stack_halves(do_ref[...])
        _scores(s_c_ref, q2, kc)
        _scores(dp_c_ref, do2, vc)
        if has_local:
            _scores(s_l_ref, q2, kl)
            _scores(dp_l_ref, do2, vl)
        dsink_parts = [jnp.zeros((), F32)] * GQA_GROUP

        def emit(t, rows, p_c, p_l, p_s):
            dp_c = dp_c_ref[rows, :]
            delta = jnp.sum(p_c * dp_c, axis=-1, keepdims=True)
            if has_local:
                dp_l = dp_l_ref[rows, :]
                delta = delta + jnp.sum(p_l * dp_l, axis=-1, keepdims=True)
                p_l_ref[rows, :] = p_l.astype(p_l_ref.dtype)
                ds_l_ref[rows, :] = (p_l * (dp_l - delta)).astype(ds_l_ref.dtype)
            p_c_ref[rows, :] = p_c.astype(p_c_ref.dtype)
            ds_c_ref[rows, :] = (p_c * (dp_c - delta)).astype(ds_c_ref.dtype)
            dsink_parts[t] = dsink_parts[t] - jnp.sum(p_s * delta)

        _softmax_chunks(s_c_ref, s_l_ref, bias_ref, sink_ref, kv_head, qb, emit)
        dq2 = _apply(ds_c_ref, kc)

        @pl.when((kv_head % 2 == 0) & (j == 0))
        def _():
            dkc_ref[...] = jnp.zeros_like(dkc_ref)
            dvc_ref[...] = jnp.zeros_like(dvc_ref)
            if has_local:
                outs[1][...] = jnp.zeros_like(outs[1])
                outs[2][...] = jnp.zeros_like(outs[2])

        @pl.when((b == 0) & (kv_head == 0) & (j == 0))
        def _():
            dsink_ref[...] = jnp.zeros_like(dsink_ref)

        dkc_ref[...] += _kv_grad(ds_c_ref, q2, odd)
        dvc_ref[...] += _kv_grad(p_c_ref, do2, odd)
        if has_local:
            dq2 = dq2 + _apply(ds_l_ref, kl)
            dkl = _kv_grad(ds_l_ref, q2, odd)
            dvl = _kv_grad(p_l_ref, do2, odd)
            dk_ref, dv_ref = outs[1], outs[2]
            for t in range(3):
                def add(t=t):
                    start = pl.multiple_of((j - 1 + t) * qb, qb)
                    dk_ref[pl.ds(start, qb), :] += dkl[t * qb:(t + 1) * qb]
                    dv_ref[pl.ds(start, qb), :] += dvl[t * qb:(t + 1) * qb]
                if t == 0:
                    pl.when(j > 0)(add)
                elif t == 2:
                    pl.when(j < nb - 1)(add)
                else:
                    add()
        dq_ref[...] = _unstack_halves(dq2, qb)
        sub = lax.broadcasted_iota(jnp.int32, (8, LANES), 0)
        tile = jnp.zeros((8, LANES), F32)
        for t, gi in enumerate(GROUP_ORDER):
            tile = jnp.where(sub == gi, dsink_parts[t], tile)
        dsink_ref[pl.ds(pl.multiple_of(kv_head * 8, 8), 8), :] += tile

    kv_w = N_KV_HEADS * HEAD_DIM
    seq_spec = pl.BlockSpec((seq, LANES), lambda b, k, j: (b, k // 2))
    ctx_spec = pl.BlockSpec((ctx_len, LANES), lambda b, k, j: (b, k // 2))
    do_spec = pl.BlockSpec((qb, GROUP_W), lambda b, k, j: (qrow(b, j), k))
    operands = [qkv] + ([qkv] * 6 + [_band_bias(qb, seq)] if has_local else []) + [qkv, qkv, sink, do]
    out_specs = ([pl.BlockSpec((qb, GROUP_W), lambda b, k, j: (b * nb + j, k))] + ([seq_spec, seq_spec] if has_local else [])
                 + [ctx_spec, ctx_spec, pl.BlockSpec((32, LANES), lambda b, k, j: (0, 0))])
    out_shape = ([jax.ShapeDtypeStruct((n_rows, N_HEADS * HEAD_DIM), F32)]
                 + ([jax.ShapeDtypeStruct((B * seq, kv_w), F32)] * 2 if has_local else [])
                 + [jax.ShapeDtypeStruct((B * ctx_len, kv_w), F32)] * 2 + [jax.ShapeDtypeStruct((32, LANES), F32)])
    return pl.pallas_call(
        body, name=name, grid=(B, N_KV_HEADS, nb),
        in_specs=[q_spec] + local + c_specs + [_SMEM, do_spec],
        out_specs=tuple(out_specs), out_shape=tuple(out_shape),
        scratch_shapes=_score_scratch(qb, ctx_len, has_local, (F32, F32, MXU_DTYPE, MXU_DTYPE)),
        compiler_params=_params())(*operands)


def _ret_decays(lg, rev):
    n = lax.broadcasted_iota(jnp.int32, (RET_CHUNK, RET_CHUNK), 0).astype(F32)
    m = lax.broadcasted_iota(jnp.int32, (RET_CHUNK, RET_CHUNK), 1).astype(F32)
    pos = lax.broadcasted_iota(jnp.int32, (RET_CHUNK, 1), 0).astype(F32)
    diff = (m - n) if rev else (n - m)
    a_exp = jnp.maximum(diff, 0.0)
    intra = jnp.where(diff >= 0, jnp.exp(lg * a_exp), 0.0)
    q_exp = (RET_CHUNK - pos) if rev else (pos + 1.0)
    k_exp = pos if rev else (RET_CHUNK - 1.0 - pos)
    chunk = jnp.exp(jnp.full((1, 1), RET_CHUNK, F32) * lg)
    return intra, a_exp, jnp.exp(lg * q_exp), q_exp, jnp.exp(lg * k_exp), k_exp, chunk


def _ctx_decay(lg, ctx_len, rev):
    t = lax.broadcasted_iota(jnp.int32, (ctx_len, 1), 0).astype(F32)
    expo = t if rev else (ctx_len - 1.0 - t)
    return jnp.exp(lg * expo), expo


def _ret_specs(B, seq, ctx_len, order):
    nc = seq // RET_CHUNK
    x_blocks = B * seq // ctx_len

    def rows(b, c):
        return b * nc + order(c, nc)

    q_spec = pl.BlockSpec((RET_CHUNK, RET_QK_DIM), lambda b, h, c: (rows(b, c), h))
    k_spec = pl.BlockSpec((RET_CHUNK, RET_QK_DIM), lambda b, h, c: (rows(b, c), RET_HEADS + h))
    v_spec = pl.BlockSpec((RET_CHUNK, RET_V_DIM), lambda b, h, c: (rows(b, c), RET_HEADS + h))
    kc_spec = pl.BlockSpec((ctx_len, RET_QK_DIM), lambda b, h, c: (x_blocks + b, RET_HEADS + h))
    vc_spec = pl.BlockSpec((ctx_len, RET_V_DIM), lambda b, h, c: (x_blocks + b, RET_HEADS + h))
    st_spec = pl.BlockSpec((1, 1, 1, RET_QK_DIM, RET_V_DIM), lambda b, h, c: (b, h, order(c, nc), 0, 0))
    o_spec = pl.BlockSpec((RET_CHUNK, RET_V_DIM), lambda b, h, c: (rows(b, c), h))
    return nc, q_spec, k_spec, v_spec, kc_spec, vc_spec, st_spec, o_spec


_SCAN_UP = lambda c, nc: c
_SCAN_DOWN = lambda c, nc: nc - 1 - c


def _ret_fwd(qk, qkvg, log_g, B, seq, ctx_len, name):
    nc, qf, kf, vf, kc_spec, vc_spec, stf, of = _ret_specs(B, seq, ctx_len, _SCAN_UP)
    _, qr, kr, vr, _, _, str_, or_ = _ret_specs(B, seq, ctx_len, _SCAN_DOWN)

    def body(lg_ref, qf_ref, kf_ref, vf_ref, qr_ref, kr_ref, vr_ref, kc_ref, vc_ref,
             of_ref, stf_ref, or_ref, str_ref, state_f, state_r):
        h, c = pl.program_id(1), pl.program_id(2)
        dirs = ((False, lg_ref[0, h], qf_ref, kf_ref, vf_ref, of_ref, stf_ref, state_f),
                (True, lg_ref[1, h], qr_ref, kr_ref, vr_ref, or_ref, str_ref, state_r))

        @pl.when(c == 0)
        def _():
            for rev, lg, _, _, _, _, _, state in dirs:
                dec, _ = _ctx_decay(lg, ctx_len, rev)
                state[...] = _dot(kc_ref[...] * dec, vc_ref[...], _TN)

        for rev, lg, q_ref, k_ref, v_ref, o_ref, st_ref, state in dirs:
            intra, _, q_dec, _, k_dec, _, chunk_dec = _ret_decays(lg, rev)
            qv, kv, vv = q_ref[...], k_ref[...], v_ref[...]
            s_in = state[...]
            st_ref[0, 0, 0] = s_in
            w = _dot(qv, kv, _NT) * intra
            o_ref[...] = _dot(w, vv, _NN) + _dot(qv, s_in, _NN) * q_dec
            state[...] = s_in * chunk_dec + _dot(kv * k_dec, vv, _TN)

    o_shape = jax.ShapeDtypeStruct((B * seq, RET_VWIDTH), F32)
    st_shape = jax.ShapeDtypeStruct((B, RET_HEADS, nc, RET_QK_DIM, RET_V_DIM), F32)
    return pl.pallas_call(
        body, name=name, grid=(B, RET_HEADS, nc),
        in_specs=[_SMEM, qf, kf, vf, qr, kr, vr, kc_spec, vc_spec],
        out_specs=(of, stf, or_, str_), out_shape=(o_shape, st_shape, o_shape, st_shape),
        scratch_shapes=[pltpu.VMEM((RET_QK_DIM, RET_V_DIM), F32)] * 2,
        compiler_params=_params())(log_g, qk, qk, qkvg, qk, qk, qkvg, qk, qkvg)


def _ret_bwd_chunk(rev, lg, q_ref, k_ref, v_ref, st_ref, do_ref, dq_ref, dk_ref, dv_ref, dlg_ref, dstate):
    intra, a_exp, q_dec, q_exp, k_dec, k_exp, chunk_dec = _ret_decays(lg, rev)
    qv, kv, vv, dov = q_ref[...], k_ref[...], v_ref[...], do_ref[...]
    s_in, ds_out = st_ref[0, 0, 0], dstate[...]
    p = _dot(qv, kv, _NT)
    w = p * intra
    dw = _dot(dov, vv, _NT)
    dp = dw * intra
    do_dec = dov * q_dec
    kd = kv * k_dec
    v_ds = _dot(vv, ds_out, _NT)
    dq_ref[...] = _dot(dp, kv, _NN) + _dot(do_dec, s_in, _NT)
    dk_ref[...] = _dot(dp, qv, _TN) + v_ds * k_dec
    dv_ref[...] = _dot(w, dov, _TN) + _dot(kd, ds_out, _NN)
    q_s = _dot(qv, s_in, _NN)
    dlg = (jnp.sum(dw * w * a_exp)
           + jnp.sum(q_exp * q_dec * jnp.sum(dov * q_s, axis=-1, keepdims=True))
           + jnp.sum(k_exp * k_dec * jnp.sum(kv * v_ds, axis=-1, keepdims=True))
           + RET_CHUNK * jnp.sum(chunk_dec * (ds_out * s_in)))
    ds_in = ds_out * chunk_dec + _dot(qv, do_dec, _TN)
    dstate[...] = ds_in
    dlg_ref[...] += dlg
    return ds_in


def _ret_bwd(qk, qkvg, log_g, st_f, st_r, do, B, seq, ctx_len, name):
    nc, qf, kf, vf, kc_spec, vc_spec, stf, of = _ret_specs(B, seq, ctx_len, _SCAN_DOWN)
    _, qr, kr, vr, _, _, str_, or_ = _ret_specs(B, seq, ctx_len, _SCAN_UP)

    def body(lg_ref, qf_ref, kf_ref, vf_ref, stf_ref, dof_ref, qr_ref, kr_ref, vr_ref, str_ref, dor_ref, kc_ref, vc_ref,
             dqf, dkf, dvf, dkcf, dvcf, dlgf, dqr, dkr, dvr, dkcr, dvcr, dlgr, dstate_f, dstate_r):
        h, c = pl.program_id(1), pl.program_id(2)
        dirs = ((False, lg_ref[0, h], (qf_ref, kf_ref, vf_ref, stf_ref, dof_ref, dqf, dkf, dvf, dlgf, dstate_f), dkcf, dvcf),
                (True, lg_ref[1, h], (qr_ref, kr_ref, vr_ref, str_ref, dor_ref, dqr, dkr, dvr, dlgr, dstate_r), dkcr, dvcr))

        @pl.when(c == 0)
        def _():
            for _, _, refs, _, _ in dirs:
                refs[-1][...] = jnp.zeros_like(refs[-1])
                refs[-2][...] = jnp.zeros_like(refs[-2])

        ds_first = [_ret_bwd_chunk(rev, lg, *refs) for rev, lg, refs, _, _ in dirs]

        @pl.when(c == nc - 1)
        def _():
            for (rev, lg, refs, dkc_ref, dvc_ref), ds_in in zip(dirs, ds_first):
                dec, expo = _ctx_decay(lg, ctx_len, rev)
                kcv, vcv = kc_ref[...], vc_ref[...]
                vc_ds = _dot(vcv, ds_in, _NT)
                dkc_ref[...] = vc_ds * dec
                dvc_ref[...] = _dot(kcv * dec, ds_in, _NN)
                refs[-2][...] += jnp.sum(expo * dec * jnp.sum(kcv * vc_ds, axis=-1, keepdims=True))

    def outs(q_spec, o_spec):
        return (pl.BlockSpec((RET_CHUNK, RET_QK_DIM), q_spec.index_map),
                pl.BlockSpec((RET_CHUNK, RET_QK_DIM), q_spec.index_map), o_spec,
                pl.BlockSpec((ctx_len, RET_QK_DIM), lambda b, h, c: (b, h)),
                pl.BlockSpec((ctx_len, RET_V_DIM), lambda b, h, c: (b, h)),
                pl.BlockSpec((1, 1, 8, LANES), lambda b, h, c: (b, h, 0, 0)))

    shapes = (jax.ShapeDtypeStruct((B * seq, RET_HEADS * RET_QK_DIM), F32),
              jax.ShapeDtypeStruct((B * seq, RET_HEADS * RET_QK_DIM), F32),
              jax.ShapeDtypeStruct((B * seq, RET_VWIDTH), F32),
              jax.ShapeDtypeStruct((B * ctx_len, RET_HEADS * RET_QK_DIM), F32),
              jax.ShapeDtypeStruct((B * ctx_len, RET_VWIDTH), F32),
              jax.ShapeDtypeStruct((B, RET_HEADS, 8, LANES), F32))
    res = pl.pallas_call(
        body, name=name, grid=(B, RET_HEADS, nc),
        in_specs=[_SMEM, qf, kf, vf, stf, of, qr, kr, vr, str_, or_, kc_spec, vc_spec],
        out_specs=outs(qf, of) + outs(qr, or_), out_shape=shapes + shapes,
        scratch_shapes=[pltpu.VMEM((RET_QK_DIM, RET_V_DIM), F32)] * 2,
        compiler_params=_params())(log_g, qk, qk, qkvg, st_f, do, qk, qk, qkvg, st_r, do, qk, qkvg)
    return res[:6], res[6:]


def _gated_out_fwd(o_f, o_b, qkvg, gn_gain, name):
    T = o_f.shape[0]
    g_off = (2 * RET_HEADS * RET_QK_DIM + RET_VWIDTH) // RET_V_DIM

    def body(of_ref, ob_ref, g_ref, gain_ref, z_ref):
        o = of_ref[...] + ob_ref[...]
        mu = jnp.mean(o, axis=-1, keepdims=True)
        var = jnp.mean(jnp.square(o - mu), axis=-1, keepdims=True)
        y = (o - mu) * lax.rsqrt(var + EPS) * gain_ref[...]
        gv = g_ref[...]
        z_ref[...] = (gv * jax.nn.sigmoid(gv) * y).astype(z_ref.dtype)

    blk = pl.BlockSpec((ROW_TILE, RET_V_DIM), lambda i, h: (i, h))
    return pl.pallas_call(
        body, name=name, grid=(T // ROW_TILE, RET_HEADS),
        in_specs=[blk, blk, pl.BlockSpec((ROW_TILE, RET_V_DIM), lambda i, h: (i, g_off + h)),
                  pl.BlockSpec((1, RET_V_DIM), lambda i, h: (0, h))],
        out_specs=blk, out_shape=jax.ShapeDtypeStruct((T, RET_VWIDTH), MXU_DTYPE),
        compiler_params=_params())(o_f, o_b, qkvg, gn_gain)


def _gated_out_bwd(dz, o_f, o_b, qkvg, gn_gain, name):
    T = o_f.shape[0]
    g_off = (2 * RET_HEADS * RET_QK_DIM + RET_VWIDTH) // RET_V_DIM

    def body(dz_ref, of_ref, ob_ref, g_ref, gain_ref, do_ref, dg_ref, dgain_ref):
        o = of_ref[...] + ob_ref[...]
        mu = jnp.mean(o, axis=-1, keepdims=True)
        var = jnp.mean(jnp.square(o - mu), axis=-1, keepdims=True)
        rstd = lax.rsqrt(var + EPS)
        yhat = (o - mu) * rstd
        gv, dzv = g_ref[...], dz_ref[...]
        sg = jax.nn.sigmoid(gv)
        dg_ref[...] = (dzv * (yhat * gain_ref[...]) * (sg * (1.0 + gv * (1.0 - sg)))).astype(dg_ref.dtype)
        dy = dzv * (gv * sg)

        @pl.when(pl.program_id(1) == 0)
        def _():
            dgain_ref[...] = jnp.zeros_like(dgain_ref)

        dgain_ref[...] += jnp.sum(dy * yhat, axis=0, keepdims=True)
        dyh = dy * gain_ref[...]
        do_ref[...] = rstd * (dyh - jnp.mean(dyh, axis=-1, keepdims=True)
                              - yhat * jnp.mean(dyh * yhat, axis=-1, keepdims=True))

    blk = pl.BlockSpec((ROW_TILE, RET_V_DIM), lambda h, i: (i, h))
    vec = pl.BlockSpec((1, RET_V_DIM), lambda h, i: (0, h))
    return pl.pallas_call(
        body, name=name, grid=(RET_HEADS, T // ROW_TILE),
        in_specs=[blk, blk, blk, pl.BlockSpec((ROW_TILE, RET_V_DIM), lambda h, i: (i, g_off + h)), vec],
        out_specs=(blk, blk, vec),
        out_shape=(jax.ShapeDtypeStruct((T, RET_VWIDTH), F32), jax.ShapeDtypeStruct((T, RET_VWIDTH), MXU_DTYPE),
                   jax.ShapeDtypeStruct((1, RET_VWIDTH), F32)),
        compiler_params=_params())(dz, o_f, o_b, qkvg, gn_gain)


def _adamw(w, m, v, parts, name):
    R, C = w.shape
    tr = _tile(R, (256, 128, 64, 32, 16, 8))
    n_parts = [p.shape[0] for p in parts]

    def body(*refs):
        w_ref, m_ref, v_ref = refs[:3]
        part_refs = refs[3:3 + len(parts)]
        g_ref, d_ref, nm_ref, nv_ref = refs[3 + len(parts):]
        g = None
        for ref, n in zip(part_refs, n_parts):
            for r in range(n):
                term = ref[r].astype(F32)
                g = term if g is None else g + term
        mn = ADAM_B1 * m_ref[...] + (1.0 - ADAM_B1) * g
        vn = ADAM_B2 * v_ref[...] + (1.0 - ADAM_B2) * jnp.square(g)
        m_hat = mn / (1.0 - ADAM_B1 ** ADAM_STEP)
        v_hat = vn / (1.0 - ADAM_B2 ** ADAM_STEP)
        g_ref[...] = g
        d_ref[...] = -ADAM_LR * (m_hat / (jnp.sqrt(v_hat) + ADAM_EPS) + ADAM_WD * w_ref[...])
        nm_ref[...] = mn
        nv_ref[...] = vn

    blk = pl.BlockSpec((tr, C), lambda i: (i, 0))
    part_specs = [pl.BlockSpec((n, tr, C), lambda i: (0, i, 0)) for n in n_parts]
    shp = jax.ShapeDtypeStruct((R, C), F32)
    return pl.pallas_call(
        body, name=name, grid=(R // tr,), in_specs=[blk, blk, blk] + part_specs,
        out_specs=(blk, blk, blk, blk), out_shape=(shp, shp, shp, shp),
        compiler_params=_params())(w, m, v, *parts)


def _sum_rows(parts, name):
    n, R, C = parts.shape
    tr = _tile(R, (256, 128, 64, 32, 16, 8))

    def body(p_ref, o_ref):
        acc = p_ref[0]
        for r in range(1, n):
            acc = acc + p_ref[r]
        o_ref[...] = acc

    return pl.pallas_call(
        body, name=name, grid=(R // tr,), in_specs=[pl.BlockSpec((n, tr, C), lambda i: (0, i, 0))],
        out_specs=pl.BlockSpec((tr, C), lambda i: (i, 0)), out_shape=jax.ShapeDtypeStruct((R, C), F32),
        compiler_params=_params())(parts)


def _my_coords():
    return lax.axis_index("x"), lax.axis_index("y"), lax.axis_index("c")


def _flip(coord, bit):
    return 1 - coord if bit else coord


def _all_gather(x2d, name):
    R, C = x2d.shape

    def body(x_ref, out_ref, send_sems, recv_sems, local_sem):
        x, y, c = _my_coords()
        me, sibling = (x, y, c), (x, y, 1 - c)
        chips = [(1 - x, y), (x, 1 - y), (1 - x, 1 - y)]

        def rows(px, py, pc):
            return out_ref.at[4 * px + 2 * py + pc]

        def copy(k, block, to, src=None):
            return pltpu.make_async_remote_copy(
                src_ref=rows(*block) if src is None else src, dst_ref=rows(*block),
                send_sem=send_sems.at[k], recv_sem=recv_sems.at[k], device_id=to, device_id_type=MESH)

        mine = pltpu.make_async_copy(x_ref, rows(*me), local_sem)
        mine.start()
        first = [copy(0, me, sibling, src=x_ref)]
        first += [copy(1 + j, me, (*chip, c), src=x_ref) for j, chip in enumerate(chips)]
        for cp in first:
            cp.start()
        passed = [copy(4 + j, (*chip, c), sibling) for j, chip in enumerate(chips)]
        for j, chip in enumerate(chips):
            copy(1 + j, (*chip, c), me).wait_recv()
            passed[j].start()
        copy(0, sibling, me).wait_recv()
        for j, chip in enumerate(chips):
            copy(4 + j, (*chip, 1 - c), me).wait_recv()
        for cp in first + passed:
            cp.wait_send()
        mine.wait()

    return pl.pallas_call(
        body, name=name, out_shape=jax.ShapeDtypeStruct((N_DEV, R, C), x2d.dtype),
        in_specs=[_ANY], out_specs=_ANY,
        scratch_shapes=[pltpu.SemaphoreType.DMA((7,)), pltpu.SemaphoreType.DMA((7,)), pltpu.SemaphoreType.DMA],
    )(x2d)


BIG_WEIGHTS = {
    "ffn_w_in": (2, (2, D_MODEL, 2 * D_FF)),
    "ffn_w_out": (1, (2, D_FF, D_MODEL)),
    "attn_w_qkv": (2, (1, D_MODEL, (N_HEADS + 2 * N_KV_HEADS) * HEAD_DIM)),
    "attn_w_o": (1, (1, N_HEADS * HEAD_DIM, D_MODEL)),
    "ret_w_qkvg": (2, (1, D_MODEL, 2 * D_MODEL + 2 * RET_VWIDTH)),
    "ret_gn_g": (2, (1, 1, RET_VWIDTH)),
    "ret_w_o": (1, (1, RET_VWIDTH, D_MODEL)),
}


def _join_shards(name, stacked):
    axis, full = BIG_WEIGHTS[name]
    if axis == 2:
        stacked = stacked.transpose(0, 2, 1, 3)
    return stacked.reshape(full)


def _split_shards(name, full_arr):
    axis, (_, rows, cols) = BIG_WEIGHTS[name]
    L = full_arr.shape[0]
    if axis == 2:
        return full_arr.reshape(L, rows, N_DEV, cols // N_DEV).transpose(0, 2, 1, 3)
    return full_arr.reshape(L, N_DEV, rows // N_DEV, cols)


def _gather_shards(shards, name):
    n = len(shards)

    def body(*refs):
        x_refs, out_refs = refs[:n], refs[n:2 * n]
        send_sems, recv_sems, local_sems = refs[2 * n:]
        x, y, c = _my_coords()
        me, sibling = (x, y, c), (x, y, 1 - c)
        chips = [(1 - x, y), (x, 1 - y), (1 - x, 1 - y)]

        def rows(a, px, py, pc):
            return out_refs[a].at[:, 4 * px + 2 * py + pc]

        def copy(a, k, block, to, src=None):
            return pltpu.make_async_remote_copy(
                src_ref=rows(a, *block) if src is None else src, dst_ref=rows(a, *block),
                send_sem=send_sems.at[7 * a + k], recv_sem=recv_sems.at[7 * a + k], device_id=to, device_id_type=MESH)

        mine = [pltpu.make_async_copy(x_refs[a], rows(a, *me), local_sems.at[a]) for a in range(n)]
        for cp in mine:
            cp.start()
        first = []
        for a in range(n):
            first.append(copy(a, 0, me, sibling, src=x_refs[a]))
            first += [copy(a, 1 + j, me, (*chip, c), src=x_refs[a]) for j, chip in enumerate(chips)]
        for cp in first:
            cp.start()
        passed = []
        for j, chip in enumerate(chips):
            for a in range(n):
                copy(a, 1 + j, (*chip, c), me).wait_recv()
                fwd = copy(a, 4 + j, (*chip, c), sibling)
                fwd.start()
                passed.append(fwd)
        for a in range(n):
            copy(a, 0, sibling, me).wait_recv()
            for j, chip in enumerate(chips):
                copy(a, 4 + j, (*chip, 1 - c), me).wait_recv()
        for cp in first + passed:
            cp.wait_send()
        for cp in mine:
            cp.wait()

    return pl.pallas_call(
        body, name=name,
        out_shape=[jax.ShapeDtypeStruct((s.shape[0], N_DEV) + s.shape[1:], s.dtype) for s in shards],
        in_specs=[_ANY] * n, out_specs=[_ANY] * n,
        scratch_shapes=[pltpu.SemaphoreType.DMA((7 * n,)), pltpu.SemaphoreType.DMA((7 * n,)),
                        pltpu.SemaphoreType.DMA((n,))],
    )(*shards)


def _exchange_shards(arrs, masks, src_of, out_tail, name):
    n, nm = len(arrs), len(masks)

    def body(*refs):
        in_refs, out_refs = refs[:n], refs[n:2 * n]
        send_sems, recv_sems = refs[2 * n:]
        x, y, c = _my_coords()
        copies = []
        for a in range(n):
            for k, (bx, by, bc) in enumerate(masks):
                peer = (_flip(x, bx), _flip(y, by), _flip(c, bc))
                copies.append(pltpu.make_async_remote_copy(
                    src_ref=src_of(in_refs[a], peer, (x, y, c)), dst_ref=out_refs[a].at[k],
                    send_sem=send_sems.at[nm * a + k], recv_sem=recv_sems.at[nm * a + k],
                    device_id=peer, device_id_type=MESH))
        for cp in copies:
            cp.start()
        for cp in copies:
            cp.wait()

    return pl.pallas_call(
        body, name=name,
        out_shape=[jax.ShapeDtypeStruct((nm,) + out_tail(s), s.dtype) for s in arrs],
        in_specs=[_ANY] * n, out_specs=[_ANY] * n,
        scratch_shapes=[pltpu.SemaphoreType.DMA((nm * n,)), pltpu.SemaphoreType.DMA((nm * n,))],
    )(*arrs)


def _pair_sum(g, from_sibling, core, out_dtype, name):
    L, _, _, a, b = g.shape
    ta = a

    def body(core_ref, g_ref, s_ref, o_ref):
        o_ref[...] = (g_ref[...] + s_ref[...]).astype(out_dtype)

    blk = pl.BlockSpec((1, 1, ta, b), lambda l, q, i, core_ref: (l, q, i, 0))
    return pl.pallas_call(
        body, name=name,
        grid_spec=pltpu.PrefetchScalarGridSpec(
            num_scalar_prefetch=1, grid=(L, 4, a // ta),
            in_specs=[pl.BlockSpec((1, 1, pl.Squeezed(), ta, b), lambda l, q, i, core_ref: (l, q, core_ref[0], i, 0)), blk],
            out_specs=blk),
        out_shape=jax.ShapeDtypeStruct((L, 4, a, b), out_dtype), compiler_params=_params())(core, g, from_sibling)


def _mods(mod_x, mod_c, layer):
    both = jnp.concatenate([mod_x[:, layer], mod_c[layer][None]], axis=0)
    return [both[:, None, k * D_MODEL:(k + 1) * D_MODEL] for k in range(6)]


def _local_step(x, ctx, target, mod_x, mod_c, w, small, late_weights=None, hooks=None):
    B, S, _ = x.shape
    L = ctx.shape[1]
    NX, NC = B * S, B * L
    T = NX + NC
    tiles_per_ex = S // ROW_TILE
    nxt = NX // ROW_TILE
    gidx = _group_index(nxt, tiles_per_ex, B)
    gidx_for = lambda rows: _group_index(NX // rows, S // rows, B)
    mm_rows = _tile(S, (MM_ROWS, ROW_TILE))
    tidx = lambda i: jnp.where(i < nxt, i % tiles_per_ex, tiles_per_ex)
    G = B + 1
    x0 = jnp.concatenate([x.reshape(NX, D_MODEL), ctx.reshape(NC, D_MODEL)], axis=0)
    acos, asin = [jnp.tile(t, (1, LANES // HEAD_DIM)) for t in _rope_tables(S, HEAD_DIM)]
    rcos, rsin = _rope_tables(S, RET_QK_DIM)
    sink = small["attn_sink"].reshape(N_KV_HEADS, GQA_GROUP)
    gains = jnp.stack([jnp.tile(small["attn_q_norm"].reshape(1, HEAD_DIM), (1, LANES // HEAD_DIM)),
                       jnp.tile(small["attn_k_norm"].reshape(1, HEAD_DIM), (1, LANES // HEAD_DIM))])
    log_g = jax.nn.log_sigmoid(small["ret_decay_logit"].reshape(2, RET_HEADS))
    n1, n2 = small["norm1_g"], small["norm2_g"]

    m0 = _mods(mod_x, mod_c, 0)
    h1 = _norm_mod_fwd(x0, n1[0:1], m0[0], m0[1], gidx, "l0_norm1")
    qkv = _mm(h1, w["attn_w_qkv"][0], "nn", F32, "l0_qkv")
    qkv_r = _attn_prep_fwd(qkv, gains, acos, asin, tidx, "l0_qk_prep")
    o_x = _attn_fwd(qkv_r, sink, B, S, L, False, "l0_attn_x")
    o_c = _attn_fwd(qkv_r, sink, B, S, L, True, "l0_attn_c")
    o0 = jnp.concatenate([o_x, o_c], axis=0)
    mo0, x1 = _mm(o0, w["attn_w_o"][0], "nn", F32, "l0_attn_out", res=x0, gate=m0[2], gidx_for=gidx_for, gate_rows=mm_rows)
    h2 = _norm_mod_fwd(x1, n2[0:1], m0[3], m0[4], gidx, "l0_norm2")
    if late_weights is not None:
        w = {**w, **late_weights(x1)}
    u0 = _mm(h2, w["ffn_w_in"][0], "nn", F32, "l0_ffn_in")
    a0 = _swiglu_fwd(u0, "l0_swiglu")
    f0, x2 = _mm(a0, w["ffn_w_out"][0], "nn", F32, "l0_ffn_out", res=x1, gate=m0[5], gidx_for=gidx_for, gate_rows=mm_rows)

    m1 = _mods(mod_x, mod_c, 1)
    g1 = _norm_mod_fwd(x2, n1[1:2], m1[0], m1[1], gidx, "l1_norm1")
    qkvg = _mm(g1, w["ret_w_qkvg"][0], "nn", F32, "l1_qkvg")
    qk = _ret_rope(qkvg, rcos, rsin, tidx, "l1_rope")
    of, st_f, ob, st_b = _ret_fwd(qk, qkvg, log_g, B, S, L, "l1_ret")
    gn = w["ret_gn_g"].reshape(1, RET_VWIDTH)
    z1 = _gated_out_fwd(of, ob, qkvg, gn, "l1_gated_out")
    xx2 = x2[:NX]
    gx = lambda i: i // tiles_per_ex
    m1x = [t[:B] for t in m1]
    mo1, y1 = _mm(z1, w["ret_w_o"][0], "nn", F32, "l1_ret_out", res=xx2, gate=m1x[2], gidx_for=gidx_for, gate_rows=mm_rows)
    k2 = _norm_mod_fwd(y1, n2[1:2], m1x[3], m1x[4], gx, "l1_norm2")
    u1 = _mm(k2, w["ffn_w_in"][1], "nn", F32, "l1_ffn_in")
    a1 = _swiglu_fwd(u1, "l1_swiglu")
    f1, y2 = _mm(a1, w["ffn_w_out"][1], "nn", F32, "l1_ffn_out", res=y1, gate=m1x[5], gidx_for=gidx_for, gate_rows=mm_rows)

    loss_tile, dy2 = _loss_fwd_bwd(y2, target.reshape(NX, D_MODEL), "loss")

    zg = jnp.zeros((1, 1, D_MODEL), F32)
    dz, dgate5_1 = _gate_bwd(dy2, f1, m1x[5], gx, B, "l1_ffn_gate_bwd")
    gw_ffn_out1 = _mm(a1, dz, "tn", F32, "l1_ffn_out_dw")
    da = _mm(dz, w["ffn_w_out"][1], "nt", F32, "l1_ffn_out_dx")
    du = _swiglu_bwd(da, u1, "l1_swiglu_bwd")
    gw_ffn_in1 = _mm(k2, du, "tn", F32, "l1_ffn_in_dw")
    dk2 = _mm(du, w["ffn_w_in"][1], "nt", F32, "l1_ffn_in_dx")
    dy1, dsh3_1, dsc4_1, dn2_1 = _norm_mod_bwd(dk2, y1, n2[1:2], m1x[4], dy2, gx, B, "l1_norm2_bwd")
    dzo, dgate2_1 = _gate_bwd(dy1, mo1, m1x[2], gx, B, "l1_ret_gate_bwd")
    gw_ret_o = _mm(z1, dzo, "tn", F32, "l1_ret_out_dw")
    dz1 = _mm(dzo, w["ret_w_o"][0], "nt", F32, "l1_ret_out_dx")
    do_r, dg_r, dgn = _gated_out_bwd(dz1, of, ob, qkvg, gn, "l1_gated_out_bwd")
    ((dq_f, dk_f, dv_f, dkc_f, dvc_f, dlg_f),
     (dq_b, dk_b, dv_b, dkc_b, dvc_b, dlg_b)) = _ret_bwd(qk, qkvg, log_g, st_f, st_b, do_r, B, S, L, "l1_ret_bwd")
    dqkvg = _ret_grad_assemble((dq_f, dq_b, dk_f, dk_b, dv_f, dv_b), (dkc_f, dkc_b, dvc_f, dvc_b), dg_r, rcos, rsin, S,
                               "l1_qkvg_grad")
    gw_ret_qkvg = _mm(g1, dqkvg, "tn", F32, "l1_qkvg_dw")
    grads_layer1 = {
        "ffn_w_in": gw_ffn_in1[None],
        "ffn_w_out": gw_ffn_out1[None],
        "ret_w_qkvg": gw_ret_qkvg[None],
        "ret_gn_g": dgn.reshape(1, 1, RET_VWIDTH),
        "ret_w_o": gw_ret_o[None],
    }
    if hooks is not None:
        m0[5] = hooks.layer1_grads(grads_layer1, m0[5])
    dg1 = _mm(dqkvg, w["ret_w_qkvg"][0], "nt", F32, "l1_qkvg_dx")
    dres1 = jnp.concatenate([dy1, jnp.zeros((NC, D_MODEL), F32)], axis=0)
    dx2, dsh0_1, dsc1_1, dn1_1 = _norm_mod_bwd(dg1, x2, n1[1:2], m1[1], dres1, gidx, G, "l1_norm1_bwd")
    dlg = jnp.stack([jnp.sum(dlg_f[:, :, 0, 0], axis=0), jnp.sum(dlg_b[:, :, 0, 0], axis=0)])
    d_decay = (dlg * jax.nn.sigmoid(-small["ret_decay_logit"].reshape(2, RET_HEADS))).reshape(1, 2, RET_HEADS)

    dz, dgate5_0 = _gate_bwd(dx2, f0, m0[5], gidx, G, "l0_ffn_gate_bwd")
    gw_ffn_out0 = _mm(a0, dz, "tn", F32, "l0_ffn_out_dw")
    da = _mm(dz, w["ffn_w_out"][0], "nt", F32, "l0_ffn_out_dx")
    du = _swiglu_bwd(da, u0, "l0_swiglu_bwd")
    if hooks is not None:
        m0[4] = hooks.mid_ffn0_backward(du, m0[4])
    gw_ffn_in0 = _mm(h2, du, "tn", F32, "l0_ffn_in_dw")
    if hooks is not None:
        m0[2] = hooks.ffn0_grads({"ffn_w_in": gw_ffn_in0[None], "ffn_w_out": gw_ffn_out0[None]}, m0[2])
    dh2 = _mm(du, w["ffn_w_in"][0], "nt", F32, "l0_ffn_in_dx")
    dx1, dsh3_0, dsc4_0, dn2_0 = _norm_mod_bwd(dh2, x1, n2[0:1], m0[4], dx2, gidx, G, "l0_norm2_bwd")
    dzo, dgate2_0 = _gate_bwd(dx1, mo0, m0[2], gidx, G, "l0_attn_gate_bwd")
    gw_attn_o = _mm(o0, dzo, "tn", F32, "l0_attn_out_dw")
    do0 = _mm(dzo, w["attn_w_o"][0], "nt", MXU_DTYPE, "l0_attn_out_dx")
    dq_x, dk_x, dv_x, dkc1, dvc1, dsink_x = _attn_bwd(qkv_r, sink, do0, B, S, L, False, "l0_attn_x_bwd")
    dq_c, dkc2, dvc2, dsink_c = _attn_bwd(qkv_r, sink, do0, B, S, L, True, "l0_attn_c_bwd")
    dqk = jnp.concatenate([jnp.concatenate([dq_x, dk_x], axis=1), jnp.concatenate([dq_c, dkc1 + dkc2], axis=1)], axis=0)
    dvv = jnp.concatenate([dv_x, dvc1 + dvc2], axis=0)
    if hooks is not None:
        gains = hooks.after_attn_backward(dq_x, gains)
    dqkv, dgains = _attn_prep_bwd(dqk, dvv, qkv, gains, acos, asin, tidx, "l0_qk_prep_bwd")
    gw_attn_qkv = _mm(h1, dqkv, "tn", F32, "l0_qkv_dw")
    dh1 = _mm(dqkv, w["attn_w_qkv"][0], "nt", F32, "l0_qkv_dx")
    dx0, dsh0_0, dsc1_0, dn1_0 = _norm_mod_bwd(dh1, x0, n1[0:1], m0[1], dx1, gidx, G, "l0_norm1_bwd")

    dgains = jnp.sum(dgains.reshape(ATTN_QK_BLOCKS, LANES // HEAD_DIM, HEAD_DIM), axis=1)
    dsink = (dsink_x + dsink_c).reshape(N_KV_HEADS, 8, LANES)[:, :GQA_GROUP, 0].reshape(1, N_HEADS)
    grads_layer0 = {
        "ffn_w_in": gw_ffn_in0[None],
        "ffn_w_out": gw_ffn_out0[None],
        "attn_w_qkv": gw_attn_qkv[None],
        "attn_w_o": gw_attn_o[None],
    }
    grads_small = {
        "norm1_g": jnp.concatenate([dn1_0, dn1_1], axis=0),
        "norm2_g": jnp.concatenate([dn2_0, dn2_1], axis=0),
        "attn_q_norm": jnp.sum(dgains[:ATTN_Q_BLOCKS], axis=0)[None],
        "attn_k_norm": jnp.sum(dgains[ATTN_Q_BLOCKS:ATTN_QK_BLOCKS], axis=0)[None],
        "attn_sink": dsink,
        "ret_decay_logit": d_decay,
    }

    def pad_g(t):
        return jnp.concatenate([t, zg], axis=0)

    d0 = jnp.concatenate([dsh0_0, dsc1_0, dgate2_0, dsh3_0, dsc4_0, dgate5_0], axis=2)[:, 0]
    d1 = jnp.concatenate([dsh0_1, dsc1_1, pad_g(dgate2_1), pad_g(dsh3_1), pad_g(dsc4_1), pad_g(dgate5_1)],
                         axis=2)[:, 0]
    dmod_x = jnp.stack([d0[:B], d1[:B]], axis=1)
    dmod_c = jnp.stack([d0[B], d1[B]], axis=0)
    return loss_tile, dx0[:NX].reshape(B, S, D_MODEL), (grads_layer0, grads_layer1), grads_small, dmod_x, dmod_c


SMALL_NAMES = ("c_ctx", "ada_b", "norm1_g", "norm2_g", "attn_q_norm", "attn_k_norm", "attn_sink", "ret_decay_logit")
ADA_ROWS = 64


def _pack_small(d, rows):
    flat = jnp.concatenate([d[k].reshape(-1) for k in SMALL_NAMES])
    n = rows * LANES
    return jnp.pad(flat, (0, n - flat.shape[0])).reshape(rows, LANES)


def _unpack_small(packed, shapes):
    flat = packed.reshape(-1)
    out, off = {}, 0
    for k in SMALL_NAMES:
        n = math.prod(shapes[k])
        out[k] = flat[off:off + n].reshape(shapes[k])
        off += n
    return out


EARLY_WEIGHTS = ("attn_w_qkv", "attn_w_o")
LATE_WEIGHTS = tuple(k for k in BIG_WEIGHTS if k not in EARLY_WEIGHTS)

_HBM = pl.BlockSpec(memory_space=pltpu.HBM)
_SEM = pl.BlockSpec(memory_space=pltpu.SEMAPHORE)
_DATAFLOW = pltpu.SideEffectType.DATAFLOW_SIDE_EFFECTING
_PEER_FLIPS = ((0, 0, 1), (0, 1, 0), (0, 1, 1), (1, 0, 0), (1, 0, 1), (1, 1, 0), (1, 1, 1))


def _wire_shard(name, t):
    return t.reshape(1, 1, -1) if name == "ret_gn_g" else t.astype(MXU_DTYPE)


def _direct_copies(x_refs, land_refs, send_sems, recv_sems, landing):
    x, y, c = _my_coords()
    out = []
    for a in range(len(x_refs)):
        for k, (bx, by, bc) in enumerate(_PEER_FLIPS):
            peer = (_flip(x, bx), _flip(y, by), _flip(c, bc))
            slot = (4 * peer[0] + 2 * peer[1] + peer[2]) if landing else (4 * x + 2 * y + c)
            out.append(pltpu.make_async_remote_copy(
                src_ref=x_refs[a], dst_ref=land_refs[a].at[:, slot], send_sem=send_sems.at[7 * a + k],
                recv_sem=recv_sems.at[7 * a + k], device_id=peer, device_id_type=MESH))
    return out


def _gather_start(shards, name):
    n = len(shards)
    lands = [lax.empty((s.shape[0], N_DEV) + s.shape[1:], s.dtype) for s in shards]

    def body(*refs):
        send_sems, recv_sems = refs[2 * n], refs[2 * n + 1]
        x_refs, land_refs = refs[2 * n + 2:3 * n + 2], refs[3 * n + 2:4 * n + 2]
        for cp in _direct_copies(x_refs, land_refs, send_sems, recv_sems, landing=False):
            cp.start()
        refs[-1][...] = jnp.zeros_like(refs[-1])

    hbm = lambda t: pltpu.with_memory_space_constraint(t, pltpu.HBM)
    res = pl.pallas_call(
        body, name=name,
        out_shape=(pltpu.SemaphoreType.DMA((7 * n,)), pltpu.SemaphoreType.DMA((7 * n,)))
        + tuple(pltpu.HBM(t.shape, t.dtype) for t in shards + lands) + (jax.ShapeDtypeStruct((8, LANES), F32),),
        in_specs=[_HBM] * (2 * n), out_specs=(_SEM, _SEM) + (_HBM,) * (2 * n) + (pl.BlockSpec(memory_space=pltpu.VMEM),),
        input_output_aliases={i: 2 + i for i in range(2 * n)},
        compiler_params=pltpu.CompilerParams(has_side_effects=_DATAFLOW))(*[hbm(t) for t in shards + lands])
    return res[0], res[1], list(res[2:2 + n]), list(res[2 + n:2 + 2 * n]), res[-1]


def _gather_wait(send_sems, recv_sems, shards, lands, after, name):
    n = len(shards)

    def body(*refs):
        x_refs, land_refs = refs[:n], refs[n:2 * n]
        for cp in _direct_copies(x_refs, land_refs, refs[2 * n], refs[2 * n + 1], landing=True):
            cp.wait_send()
            cp.wait_recv()

    res = pl.pallas_call(
        body, name=name, out_shape=tuple(pltpu.HBM(t.shape, t.dtype) for t in shards + lands),
        in_specs=[_HBM] * (2 * n) + [_SEM, _SEM, _ANY], out_specs=(_HBM,) * (2 * n),
        input_output_aliases={i: i for i in range(2 * n)},
        compiler_params=pltpu.CompilerParams(has_side_effects=_DATAFLOW))(*shards, *lands, send_sems, recv_sems, after)
    return list(res[n:])


def _gather_big_weights(weights, names, name):
    gathered = _gather_shards([_wire_shard(k, weights[k]) for k in names], name)
    return {k: _join_shards(k, g) for k, g in zip(names, gathered)}


_SIBLING = ((0, 0, 1),)
_CHIPS = ((1, 0, 0), (0, 1, 0), (1, 1, 0))
_to_sibling = lambda ref, peer: ref.at[:, :, peer[2]]
_to_chip = lambda ref, peer: ref.at[:, 2 * peer[0] + peer[1]]
_sibling_tail = lambda s: (s.shape[0], 4) + s.shape[3:]
_chip_tail = lambda s: (s.shape[0],) + s.shape[2:]


def _rs_split(grads):
    names = list(grads)
    split = []
    for k in names:
        s = _split_shards(k, grads[k])
        split.append(s.reshape(s.shape[0], 4, 2, s.shape[2], s.shape[3]))
    return names, split


def _rs_pair_sums(names, split, from_sibling, tag):
    core = lax.axis_index("c").astype(jnp.int32).reshape(1)
    return [_pair_sum(g, s, core, MXU_DTYPE, tag + k) for k, g, s in zip(names, split, from_sibling)]


def _rs_parts(names, split, from_sibling, from_chips):
    mx_, my_, mc_ = _my_coords()
    my_chip = 2 * mx_ + my_
    parts = {}
    for k, g, s, r in zip(names, split, from_sibling, from_chips):
        own_keep = lax.dynamic_index_in_dim(lax.dynamic_index_in_dim(g, my_chip, axis=1, keepdims=False), mc_, axis=1,
                                            keepdims=False)
        parts[k] = (own_keep, lax.dynamic_index_in_dim(s, my_chip, axis=1, keepdims=False), r)
    return parts


def _reduce_scatter_in_call(grads, tag):
    names, split = _rs_split(grads)
    from_sibling = [t[0] for t in _exchange_shards(split, _SIBLING, lambda ref, peer, me_: _to_sibling(ref, peer),
                                                   _sibling_tail, tag + "sibling")]
    pair = _rs_pair_sums(names, split, from_sibling, tag + "pair_")
    from_chips = _exchange_shards(pair, _CHIPS, lambda ref, peer, me_: _to_chip(ref, peer), _chip_tail, tag + "chips")
    return _rs_parts(names, split, from_sibling, from_chips)


def _exchange_copies(in_refs, land_refs, send_sems, recv_sems, masks, src_of):
    x, y, c = _my_coords()
    nm = len(masks)
    out = []
    for a in range(len(in_refs)):
        for k, (bx, by, bc) in enumerate(masks):
            peer = (_flip(x, bx), _flip(y, by), _flip(c, bc))
            out.append(pltpu.make_async_remote_copy(
                src_ref=src_of(in_refs[a], peer), dst_ref=land_refs[a].at[k], send_sem=send_sems.at[nm * a + k],
                recv_sem=recv_sems.at[nm * a + k], device_id=peer, device_id_type=MESH))
    return out


def _exchange_start(arrs, masks, src_of, out_tail, name):
    n, nm = len(arrs), len(masks)
    lands = [lax.empty((nm,) + out_tail(s), s.dtype) for s in arrs]

    def body(*refs):
        send_sems, recv_sems = refs[2 * n], refs[2 * n + 1]
        in_refs, land_refs = refs[2 * n + 2:3 * n + 2], refs[3 * n + 2:4 * n + 2]
        for cp in _exchange_copies(in_refs, land_refs, send_sems, recv_sems, masks, src_of):
            cp.start()
        refs[-1][...] = jnp.zeros_like(refs[-1])

    hbm = lambda t: pltpu.with_memory_space_constraint(t, pltpu.HBM)
    res = pl.pallas_call(
        body, name=name,
        out_shape=(pltpu.SemaphoreType.DMA((nm * n,)), pltpu.SemaphoreType.DMA((nm * n,)))
        + tuple(pltpu.HBM(t.shape, t.dtype) for t in list(arrs) + lands) + (jax.ShapeDtypeStruct((8, LANES), F32),),
        in_specs=[_HBM] * (2 * n), out_specs=(_SEM, _SEM) + (_HBM,) * (2 * n) + (pl.BlockSpec(memory_space=pltpu.VMEM),),
        input_output_aliases={i: 2 + i for i in range(2 * n)},
        compiler_params=pltpu.CompilerParams(has_side_effects=_DATAFLOW))(*[hbm(t) for t in list(arrs) + lands])
    return (res[0], res[1], list(res[2:2 + n]), list(res[2 + n:2 + 2 * n]), masks, src_of), res[-1]


def _exchange_wait(state, after, name):
    send_sems, recv_sems, arrs, lands, masks, src_of = state
    n = len(arrs)

    def body(*refs):
        for cp in _exchange_copies(refs[:n], refs[n:2 * n], refs[2 * n], refs[2 * n + 1], masks, src_of):
            cp.wait_send()
            cp.wait_recv()

    res = pl.pallas_call(
        body, name=name, out_shape=tuple(pltpu.HBM(t.shape, t.dtype) for t in arrs + lands),
        in_specs=[_HBM] * (2 * n) + [_SEM, _SEM, _ANY], out_specs=(_HBM,) * (2 * n),
        input_output_aliases={i: i for i in range(2 * n)},
        compiler_params=pltpu.CompilerParams(has_side_effects=_DATAFLOW))(*arrs, *lands, send_sems, recv_sems, after)
    return list(res[:n]), list(res[n:])


class _SplitReduce:
    def __init__(self, tag):
        self.tag = tag

    def start(self, grads, order_through):
        self.names, split = _rs_split(grads)
        self.sibling, tok = _exchange_start(split, _SIBLING, _to_sibling, _sibling_tail, self.tag + "sibling_start")
        return order_through + tok[0, 0]

    def middle(self, after, order_through):
        self.split, lands = _exchange_wait(self.sibling, after, self.tag + "sibling_wait")
        self.from_sibling = [t[0] for t in lands]
        pair = _rs_pair_sums(self.names, self.split, self.from_sibling, self.tag + "pair_")
        self.chips, tok = _exchange_start(pair, _CHIPS, _to_chip, _chip_tail, self.tag + "chips_start")
        return order_through + tok[0, 0]

    def finish(self, after):
        _, from_chips = _exchange_wait(self.chips, after, self.tag + "chips_wait")
        return _rs_parts(self.names, self.split, self.from_sibling, from_chips)


def _adamw_big(weights, mom1, mom2, part_groups):
    big = {}
    for k in BIG_WEIGHTS:
        parts = [g[k] for g in part_groups if k in g]
        own_keep = jnp.concatenate([p[0] for p in parts], axis=0)
        own_sib = jnp.concatenate([p[1] for p in parts], axis=0)
        recv = jnp.concatenate([p[2] for p in parts], axis=1)
        L_, a_, b_ = own_keep.shape
        rows = L_ * a_
        res = _adamw(weights[k].reshape(rows, b_), mom1[k].reshape(rows, b_), mom2[k].reshape(rows, b_),
                     [own_keep.reshape(1, rows, b_), own_sib.reshape(1, rows, b_), recv.reshape(3, rows, b_)],
                     "adamw_" + k)
        big[k] = [t.reshape(weights[k].shape) for t in res]
    return big


def kernel(x, c, ctx, c_ctx, ada_w, ada_b, norm1_g, norm2_g, ffn_w_in, ffn_w_out, attn_w_qkv, attn_q_norm, attn_k_norm, attn_sink, attn_w_o, ret_w_qkvg, ret_decay_logit, ret_gn_g, ret_w_o, loss_target, m_c_ctx, m_ada_w, m_ada_b, m_norm1_g, m_norm2_g, m_ffn_w_in, m_ffn_w_out, m_attn_w_qkv, m_attn_q_norm, m_attn_k_norm, m_attn_sink, m_attn_w_o, m_ret_w_qkvg, m_ret_decay_logit, m_ret_gn_g, m_ret_w_o, v_c_ctx, v_ada_w, v_ada_b, v_norm1_g, v_norm2_g, v_ffn_w_in, v_ffn_w_out, v_attn_w_qkv, v_attn_q_norm, v_attn_k_norm, v_attn_sink, v_attn_w_o, v_ret_w_qkvg, v_ret_decay_logit, v_ret_gn_g, v_ret_w_o):
    weights = dict(c_ctx=c_ctx, ada_w=ada_w, ada_b=ada_b, norm1_g=norm1_g, norm2_g=norm2_g, ffn_w_in=ffn_w_in,
                   ffn_w_out=ffn_w_out, attn_w_qkv=attn_w_qkv, attn_q_norm=attn_q_norm, attn_k_norm=attn_k_norm,
                   attn_sink=attn_sink, attn_w_o=attn_w_o, ret_w_qkvg=ret_w_qkvg, ret_decay_logit=ret_decay_logit,
                   ret_gn_g=ret_gn_g, ret_w_o=ret_w_o)
    mom1 = dict(c_ctx=m_c_ctx, ada_w=m_ada_w, ada_b=m_ada_b, norm1_g=m_norm1_g, norm2_g=m_norm2_g, ffn_w_in=m_ffn_w_in,
                ffn_w_out=m_ffn_w_out, attn_w_qkv=m_attn_w_qkv, attn_q_norm=m_attn_q_norm, attn_k_norm=m_attn_k_norm,
                attn_sink=m_attn_sink, attn_w_o=m_attn_w_o, ret_w_qkvg=m_ret_w_qkvg, ret_decay_logit=m_ret_decay_logit,
                ret_gn_g=m_ret_gn_g, ret_w_o=m_ret_w_o)
    mom2 = dict(c_ctx=v_c_ctx, ada_w=v_ada_w, ada_b=v_ada_b, norm1_g=v_norm1_g, norm2_g=v_norm2_g, ffn_w_in=v_ffn_w_in,
                ffn_w_out=v_ffn_w_out, attn_w_qkv=v_attn_w_qkv, attn_q_norm=v_attn_q_norm, attn_k_norm=v_attn_k_norm,
                attn_sink=v_attn_sink, attn_w_o=v_attn_w_o, ret_w_qkvg=v_ret_w_qkvg, ret_decay_logit=v_ret_decay_logit,
                ret_gn_g=v_ret_gn_g, ret_w_o=v_ret_w_o)
    B = x.shape[0]
    mx_, my_, mc_ = _my_coords()
    me = 4 * mx_ + 2 * my_ + mc_
    ada_cols = ada_w.shape[2]

    w_full = _gather_big_weights(weights, EARLY_WEIGHTS, "gather_early")

    c_all = _all_gather(jax.nn.silu(c), "gather_c").reshape(N_DEV * B, D_MODEL)
    cc_act = jax.nn.silu(c_ctx)[None]
    ada_in = jnp.concatenate([c_all, cc_act, jnp.zeros((ADA_ROWS - N_DEV * B - 1, D_MODEL), F32)], axis=0)
    ada_in = ada_in.astype(MXU_DTYPE)
    ada_w2 = jnp.concatenate([ada_w[0], ada_w[1]], axis=1)
    bias = lax.dynamic_slice_in_dim(ada_b.reshape(2, N_DEV, ada_cols), me, 1, axis=1).reshape(1, 2 * ada_cols)
    mod_cols = _mm(ada_in, ada_w2, "nn", F32, "ada_fwd", bias=bias)
    mod_all = _all_gather(mod_cols, "gather_mod")
    mod_all = mod_all.reshape(N_DEV, ADA_ROWS, 2, ada_cols).transpose(1, 2, 0, 3).reshape(ADA_ROWS, 2, N_DEV * ada_cols)
    mod_x = lax.dynamic_slice_in_dim(mod_all, me * B, B, axis=0)
    mod_c = mod_all[N_DEV * B]

    order = 0.0 * (mod_c[0, 0] + w_full["attn_w_o"][0, 0, 0].astype(F32))
    late_shards = [_wire_shard(k, weights[k] + order if k == "ret_gn_g" else weights[k]) for k in LATE_WEIGHTS]
    send_sems, recv_sems, late_thru, late_lands, token = _gather_start(late_shards, "gather_late_start")
    mod_x = mod_x + token[0, 0]

    def late_weights(after):
        lands = _gather_wait(send_sems, recv_sems, late_thru, late_lands, after, "gather_late_wait")
        own = [lax.dynamic_update_index_in_dim(land, shard, me, axis=1) for land, shard in zip(lands, late_shards)]
        return {k: _join_shards(k, g) for k, g in zip(LATE_WEIGHTS, own)}

    rs_layer1, rs_ffn0 = _SplitReduce("rs1_"), _SplitReduce("rs0_")

    class Hooks:
        layer1_grads = rs_layer1.start
        mid_ffn0_backward = rs_layer1.middle
        ffn0_grads = rs_ffn0.start
        after_attn_backward = rs_ffn0.middle

    small = {k: weights[k] for k in SMALL_NAMES}
    loss_tile, grad_x, (g_layer0, _), g_small, dmod_x, dmod_c = _local_step(
        x, ctx, loss_target, mod_x, mod_c, w_full, small, late_weights, Hooks)
    parts1 = rs_layer1.finish(grad_x)
    parts0_ffn = rs_ffn0.finish(grad_x)
    loss = lax.psum(loss_tile[0, 0], ("x", "y", "c"))

    n_mod = 2 * 6 * D_MODEL
    dm_rows = jnp.concatenate([dmod_x.reshape(B, n_mod), dmod_c.reshape(1, n_mod),
                               jnp.zeros((8 - B - 1, n_mod), F32)], axis=0)
    dm_all = _all_gather(dm_rows, "gather_dmod")
    dmc_tot = _sum_rows(dm_all[:, B:B + 1].reshape(N_DEV, 1, n_mod)[:, :, :].reshape(N_DEV, n_mod // LANES, LANES),
                        "sum_dmod_c").reshape(1, n_mod)
    dmod_rows = jnp.concatenate([dm_all[:, :B].reshape(N_DEV * B, n_mod), dmc_tot,
                                 jnp.zeros((ADA_ROWS - N_DEV * B - 1, n_mod), F32)], axis=0)
    dmod_mine = lax.dynamic_slice_in_dim(dmod_rows.reshape(ADA_ROWS, 2, N_DEV, ada_cols), me, 1, axis=2)
    dmod_mine = dmod_mine.reshape(ADA_ROWS, 2 * ada_cols).astype(MXU_DTYPE)
    g_ada2 = _mm(ada_in, dmod_mine, "tn", F32, "ada_dw")
    g_ada_w = jnp.stack([g_ada2[:, :ada_cols], g_ada2[:, ada_cols:]])
    dmc_mine = jnp.concatenate([dmod_mine[N_DEV * B:N_DEV * B + 1], jnp.zeros((7, 2 * ada_cols), MXU_DTYPE)], axis=0)
    dcc_part = _mm(dmc_mine, ada_w2, "nt", F32, "ada_dc")[0:1]
    g_ada_b = _sum_rows(dmod_rows[:, None, :].reshape(ADA_ROWS, n_mod // LANES, LANES), "sum_dmod_b").reshape(2, 6 * D_MODEL)
    sg = jax.nn.sigmoid(c_ctx)
    g_small["c_ctx"] = dcc_part.reshape(D_MODEL) * (sg * (1.0 + c_ctx * (1.0 - sg)))
    g_small["ada_b"] = g_ada_b * (1.0 / N_DEV)

    shapes = {k: weights[k].shape for k in SMALL_NAMES}
    n_small = sum(math.prod(s) for s in shapes.values())
    srows = -(-(-(-n_small // LANES)) // 8) * 8
    gs_all = _all_gather(_pack_small(g_small, srows), "gather_small_grads")
    sm = _adamw(_pack_small({k: weights[k] for k in SMALL_NAMES}, srows), _pack_small({k: mom1[k] for k in SMALL_NAMES}, srows),
                _pack_small({k: mom2[k] for k in SMALL_NAMES}, srows), [gs_all], "adamw_small")
    sm = [_unpack_small(t, shapes) for t in sm]

    ada_shape = ada_w.shape
    r2 = lambda t: t.reshape(ada_shape[0] * ada_shape[1], ada_shape[2])
    ada = [t.reshape(ada_shape) for t in _adamw(r2(ada_w), r2(m_ada_w), r2(v_ada_w), [r2(g_ada_w)[None]], "adamw_ada")]

    attn_grads = {k: g_layer0[k] for k in EARLY_WEIGHTS}
    big = _adamw_big(weights, mom1, mom2, [_reduce_scatter_in_call(attn_grads, "rs_"), parts0_ffn, parts1])

    def pick(i, name):
        if name in BIG_WEIGHTS:
            return big[name][i]
        if name == "ada_w":
            return ada[i]
        return sm[i][name]

    order = ("c_ctx", "ada_w", "ada_b", "norm1_g", "norm2_g", "ffn_w_in", "ffn_w_out", "attn_w_qkv", "attn_q_norm",
             "attn_k_norm", "attn_sink", "attn_w_o", "ret_w_qkvg", "ret_decay_logit", "ret_gn_g", "ret_w_o")
    outs = [loss, grad_x]
    for i in range(4):
        outs += [pick(i, n) for n in order]
    return tuple(outs)
```

```python
import functools
import math

import jax
import jax.numpy as jnp
from jax import lax
from jax.experimental import pallas as pl
from jax.experimental.pallas import tpu as pltpu

F32 = jnp.float32
MXU_DTYPE = jnp.bfloat16

D_MODEL = 1024
HEAD_DIM = 64
N_HEADS = 16
N_KV_HEADS = 4
GQA_GROUP = 4
WINDOW = 128
ATTN_BLOCK = 128
RET_HEADS = 4
RET_QK_DIM = 256
RET_V_DIM = 512
RET_VWIDTH = 2048
RET_CHUNK = 512
D_FF = 2816
GRID_W = 64
ROPE_BASE = 10000.0
EPS = 1e-6
NEG_INF = -1e30

ADAM_LR = 0.001
ADAM_B1 = 0.9
ADAM_B2 = 0.999
ADAM_EPS = 1e-08
ADAM_WD = 0.01
ADAM_STEP = 10

N_DEV = 8
LANES = 128
ROW_TILE = 512
VMEM_LIMIT = 48 * 1024 * 1024

MESH = pl.DeviceIdType.MESH
_ANY = pl.BlockSpec(memory_space=pl.ANY)
_SMEM = pl.BlockSpec(memory_space=pltpu.SMEM)


def _params(**kw):
    return pltpu.CompilerParams(vmem_limit_bytes=VMEM_LIMIT, **kw)


def _mx(v):
    return v.astype(MXU_DTYPE)


def _dot(a, b, dims):
    return lax.dot_general(_mx(a), _mx(b), (dims, ((), ())), preferred_element_type=F32)


_NN = ((1,), (0,))
_NT = ((1,), (1,))
_TN = ((0,), (0,))


def _tile(n, cands):
    for c in cands:
        if n % c == 0:
            return c
    return n


def _big_tile(n, cap):
    if n <= cap:
        return n
    for t in range(cap - cap % LANES, 0, -LANES):
        if n % t == 0:
            return t
    return n


MM_ROWS = 1024
MM_COLS = 1408
MM_DEPTH = 2048


def _k_tile(k):
    return _big_tile(k, MM_DEPTH)


def _mm(a, b, mode, out_dtype, name, *, bias=None, res=None, gate=None, gidx_for=None, gate_rows=None):
    if mode == "nn":
        (M, K), (_, N) = a.shape, b.shape
    elif mode == "nt":
        (M, K), (N, _) = a.shape, b.shape
    else:
        (K, M), (_, N) = a.shape, b.shape
    if res is not None:
        tm, tn = gate_rows, _big_tile(N, 512)
        gidx = gidx_for(tm)
    else:
        tm = _big_tile(M, MM_COLS if mode == "tn" else MM_ROWS)
        tn = _big_tile(N, MM_COLS)
    tk = _k_tile(K)
    nk = K // tk
    dims = {"nn": _NN, "nt": _NT, "tn": _TN}[mode]
    a_spec = (pl.BlockSpec((tk, tm), lambda i, j, k: (k, i)) if mode == "tn"
              else pl.BlockSpec((tm, tk), lambda i, j, k: (i, k)))
    b_spec = (pl.BlockSpec((tn, tk), lambda i, j, k: (j, k)) if mode == "nt"
              else pl.BlockSpec((tk, tn), lambda i, j, k: (k, j)))
    o_spec = pl.BlockSpec((tm, tn), lambda i, j, k: (i, j))
    in_specs, operands = [a_spec, b_spec], [a, b]
    if bias is not None:
        in_specs.append(pl.BlockSpec((1, tn), lambda i, j, k: (0, j)))
        operands.append(bias)
    if res is not None:
        in_specs += [o_spec, pl.BlockSpec((1, 1, tn), lambda i, j, k: (gidx(i), 0, j))]
        operands += [res, gate]
        out_shape = (jax.ShapeDtypeStruct((M, N), F32), jax.ShapeDtypeStruct((M, N), F32))
        out_specs = (o_spec, o_spec)
    else:
        out_shape = jax.ShapeDtypeStruct((M, N), out_dtype)
        out_specs = o_spec

    def body(*refs):
        a_ref, b_ref = refs[0], refs[1]
        extra = refs[2:len(operands)]
        outs = refs[len(operands):]
        prod = _dot(a_ref[...], b_ref[...], dims)

        def finish(acc):
            if bias is not None:
                outs[0][...] = (acc + extra[0][...]).astype(out_dtype)
            elif res is not None:
                outs[0][...] = acc
                outs[1][...] = extra[0][...] + extra[1][0] * acc
            else:
                outs[0][...] = acc.astype(out_dtype)

        if nk == 1:
            finish(prod)
        else:
            acc_ref = outs[-1]
            outs = outs[:-1]
            k = pl.program_id(2)

            @pl.when(k == 0)
            def _():
                acc_ref[...] = prod

            @pl.when(k > 0)
            def _():
                acc_ref[...] += prod

            @pl.when(k == nk - 1)
            def _():
                finish(acc_ref[...])

    return pl.pallas_call(
        body, name=name, grid=(M // tm, N // tn, nk), in_specs=in_specs, out_specs=out_specs, out_shape=out_shape,
        scratch_shapes=[pltpu.VMEM((tm, tn), F32)] if nk > 1 else [],
        compiler_params=_params())(*operands)


def _group_index(n_x_tiles, tiles_per_example, n_examples):
    def gidx(i):
        return jnp.where(i < n_x_tiles, i // tiles_per_example, n_examples)
    return gidx


def _norm_mod_fwd(x, g, shift, scale, gidx, name):
    T, Dm = x.shape

    def body(x_ref, g_ref, sh_ref, sc_ref, h_ref):
        xv = x_ref[...]
        r = lax.rsqrt(jnp.mean(xv * xv, axis=-1, keepdims=True) + EPS)
        y = xv * r * g_ref[...]
        h_ref[...] = (y * (1.0 + sc_ref[0]) + sh_ref[0]).astype(h_ref.dtype)

    row = pl.BlockSpec((ROW_TILE, Dm), lambda i: (i, 0))
    mod = pl.BlockSpec((1, 1, Dm), lambda i: (gidx(i), 0, 0))
    return pl.pallas_call(
        body, name=name, grid=(T // ROW_TILE,),
        in_specs=[row, pl.BlockSpec((1, Dm), lambda i: (0, 0)), mod, mod],
        out_specs=row, out_shape=jax.ShapeDtypeStruct((T, Dm), MXU_DTYPE),
        compiler_params=_params())(x, g, shift, scale)


def _first_of_group(i, gidx):
    return jnp.logical_or(i == 0, gidx(i) != gidx(jnp.maximum(i - 1, 0)))


def _norm_mod_bwd(dh, x, g, scale, dres, gidx, n_groups, name, gated=None):
    T, Dm = x.shape

    def body(*refs):
        dh_ref, x_ref, g_ref, sc_ref, dres_ref = refs[:5]
        n_in = 7 if gated else 5
        dx_ref, dsh_ref, dsc_ref, dg_ref = refs[n_in:n_in + 4]
        i = pl.program_id(0)
        xv, dhv = x_ref[...], dh_ref[...]
        r = lax.rsqrt(jnp.mean(xv * xv, axis=-1, keepdims=True) + EPS)
        xn = xv * r
        y = xn * g_ref[...]

        @pl.when(_first_of_group(i, gidx))
        def _():
            dsh_ref[...] = jnp.zeros_like(dsh_ref)
            dsc_ref[...] = jnp.zeros_like(dsc_ref)

        @pl.when(i == 0)
        def _():
            dg_ref[...] = jnp.zeros_like(dg_ref)

        dsh_ref[0] += jnp.sum(dhv, axis=0, keepdims=True)
        dsc_ref[0] += jnp.sum(dhv * y, axis=0, keepdims=True)
        dy = dhv * (1.0 + sc_ref[0])
        dg_ref[...] += jnp.sum(dy * xn, axis=0, keepdims=True)
        dxn = dy * g_ref[...]
        dx = dres_ref[...] + r * (dxn - xn * jnp.mean(dxn * xn, axis=-1, keepdims=True))
        dx_ref[...] = dx
        if gated:
            f_ref, gate_ref = refs[5:7]
            dz_ref, dgate_ref = refs[n_in + 4:]

            @pl.when(_first_of_group(i, gidx))
            def _():
                dgate_ref[...] = jnp.zeros_like(dgate_ref)

            dgate_ref[0] += jnp.sum(dx * f_ref[...], axis=0, keepdims=True)
            dz_ref[...] = (dx * gate_ref[0]).astype(dz_ref.dtype)

    row = pl.BlockSpec((ROW_TILE, Dm), lambda i: (i, 0))
    mod = pl.BlockSpec((1, 1, Dm), lambda i: (gidx(i), 0, 0))
    vec = pl.BlockSpec((1, Dm), lambda i: (0, 0))
    mod_shape = jax.ShapeDtypeStruct((n_groups, 1, Dm), F32)
    in_specs, operands = [row, row, vec, mod, row], [dh, x, g, scale, dres]
    out_specs = [row, mod, mod, vec]
    out_shape = [jax.ShapeDtypeStruct((T, Dm), F32), mod_shape, mod_shape, jax.ShapeDtypeStruct((1, Dm), F32)]
    if gated:
        in_specs, operands = in_specs + [row, mod], operands + list(gated)
        out_specs, out_shape = out_specs + [row, mod], out_shape + [jax.ShapeDtypeStruct((T, Dm), MXU_DTYPE), mod_shape]
    return pl.pallas_call(
        body, name=name, grid=(T // ROW_TILE,), in_specs=in_specs, out_specs=tuple(out_specs),
        out_shape=tuple(out_shape), compiler_params=_params())(*operands)


def _gate_bwd(dy, f, gate, gidx, n_groups, name):
    T, Dm = dy.shape

    def body(dy_ref, f_ref, gate_ref, dz_ref, dgate_ref):
        i = pl.program_id(0)
        dyv = dy_ref[...]

        @pl.when(_first_of_group(i, gidx))
        def _():
            dgate_ref[...] = jnp.zeros_like(dgate_ref)

        dgate_ref[0] += jnp.sum(dyv * f_ref[...], axis=0, keepdims=True)
        dz_ref[...] = (dyv * gate_ref[0]).astype(dz_ref.dtype)

    row = pl.BlockSpec((ROW_TILE, Dm), lambda i: (i, 0))
    mod = pl.BlockSpec((1, 1, Dm), lambda i: (gidx(i), 0, 0))
    return pl.pallas_call(
        body, name=name, grid=(T // ROW_TILE,), in_specs=[row, row, mod], out_specs=(row, mod),
        out_shape=(jax.ShapeDtypeStruct((T, Dm), MXU_DTYPE), jax.ShapeDtypeStruct((n_groups, 1, Dm), F32)),
        compiler_params=_params())(dy, f, gate)


SWIGLU_ROWS = 256


def _swiglu_fwd(u, name):
    T = u.shape[0]

    def body(u_ref, a_ref):
        gate, up = u_ref[:, :D_FF], u_ref[:, D_FF:]
        a_ref[...] = (gate * jax.nn.sigmoid(gate) * up).astype(a_ref.dtype)

    return pl.pallas_call(
        body, name=name, grid=(T // SWIGLU_ROWS,),
        in_specs=[pl.BlockSpec((SWIGLU_ROWS, 2 * D_FF), lambda i: (i, 0))],
        out_specs=pl.BlockSpec((SWIGLU_ROWS, D_FF), lambda i: (i, 0)),
        out_shape=jax.ShapeDtypeStruct((T, D_FF), MXU_DTYPE), compiler_params=_params())(u)


def _swiglu_bwd(da, u, name):
    T = u.shape[0]

    def body(da_ref, u_ref, du_ref):
        gate, up, dav = u_ref[:, :D_FF], u_ref[:, D_FF:], da_ref[...]
        sg = jax.nn.sigmoid(gate)
        du_ref[:, :D_FF] = (dav * up * (sg * (1.0 + gate * (1.0 - sg)))).astype(du_ref.dtype)
        du_ref[:, D_FF:] = (dav * gate * sg).astype(du_ref.dtype)

    return pl.pallas_call(
        body, name=name, grid=(T // SWIGLU_ROWS,),
        in_specs=[pl.BlockSpec((SWIGLU_ROWS, D_FF), lambda i: (i, 0)),
                  pl.BlockSpec((SWIGLU_ROWS, 2 * D_FF), lambda i: (i, 0))],
        out_specs=pl.BlockSpec((SWIGLU_ROWS, 2 * D_FF), lambda i: (i, 0)),
        out_shape=jax.ShapeDtypeStruct((T, 2 * D_FF), MXU_DTYPE), compiler_params=_params())(da, u)


def _loss_fwd_bwd(y, target, name):
    T, Dm = y.shape

    def body(y_ref, t_ref, loss_ref, dy_ref):
        err = y_ref[...] - t_ref[...]

        @pl.when(pl.program_id(0) == 0)
        def _():
            loss_ref[...] = jnp.zeros_like(loss_ref)

        loss_ref[...] += 0.5 * jnp.sum(jnp.mean(err * err, axis=-1, keepdims=True))
        dy_ref[...] = err * (1.0 / Dm)

    row = pl.BlockSpec((ROW_TILE, Dm), lambda i: (i, 0))
    return pl.pallas_call(
        body, name=name, grid=(T // ROW_TILE,), in_specs=[row, row],
        out_specs=(pl.BlockSpec((8, LANES), lambda i: (0, 0)), row),
        out_shape=(jax.ShapeDtypeStruct((8, LANES), F32), jax.ShapeDtypeStruct((T, Dm), F32)),
        compiler_params=_params())(y, target)


def _rope_tables(seq, head_dim):
    axis_dim = head_dim // 2
    half = axis_dim // 2
    pos = jnp.arange(seq, dtype=jnp.int32)
    row = (pos // GRID_W).astype(F32)[:, None]
    col = (pos % GRID_W).astype(F32)[:, None]
    inv = ROPE_BASE ** (-jnp.arange(0, axis_dim, 2, dtype=F32) / axis_dim)
    lane = jnp.arange(head_dim, dtype=jnp.int32)
    within = lane % axis_dim
    ang = jnp.where((lane // axis_dim == 0)[None, :], row, col) * inv[within % half][None, :]
    cos = jnp.cos(ang)
    sin = jnp.where((within < half)[None, :], -jnp.sin(ang), jnp.sin(ang))
    cos = jnp.concatenate([cos, jnp.ones((ROW_TILE, head_dim), F32)], axis=0)
    sin = jnp.concatenate([sin, jnp.zeros((ROW_TILE, head_dim), F32)], axis=0)
    return cos, sin


def _pair_swap(v, half):
    if 2 * half == LANES:
        return pltpu.roll(v, half, axis=1)
    lane = lax.broadcasted_iota(jnp.int32, v.shape, 1)
    return jnp.where((lane % (2 * half)) < half, pltpu.roll(v, LANES - half, axis=1), pltpu.roll(v, half, axis=1))


def _head_sum(v, ones_ref):
    hi = v.astype(MXU_DTYPE)
    lo = (v - hi.astype(F32)).astype(MXU_DTYPE)
    return (jnp.dot(hi, ones_ref[...], preferred_element_type=F32)
            + jnp.dot(lo, ones_ref[...], preferred_element_type=F32))


def _head_ones():
    lane = jnp.arange(LANES)
    return (lane[:, None] // HEAD_DIM == lane[None, :] // HEAD_DIM).astype(MXU_DTYPE)


ATTN_QK_BLOCKS = (N_HEADS + N_KV_HEADS) * HEAD_DIM // LANES
ATTN_ALL_BLOCKS = (N_HEADS + 2 * N_KV_HEADS) * HEAD_DIM // LANES
ATTN_Q_BLOCKS = N_HEADS * HEAD_DIM // LANES
ATTN_SCALE = HEAD_DIM ** -0.5


def _attn_prep_fwd(qkv, gains, cos, sin, tidx, name):
    T, W = qkv.shape

    def body(x_ref, g_ref, cos_ref, sin_ref, ones_ref, o_ref):
        for cb in range(ATTN_ALL_BLOCKS):
            cols = slice(cb * LANES, (cb + 1) * LANES)
            xv = x_ref[:, cols]
            if cb < ATTN_QK_BLOCKS:
                r = lax.rsqrt(_head_sum(xv * xv, ones_ref) * (1.0 / HEAD_DIM) + EPS)
                y = xv * r * g_ref[0 if cb < ATTN_Q_BLOCKS else 1]
                xv = y * cos_ref[...] + _pair_swap(y, HEAD_DIM // 4) * sin_ref[...]
                if cb < ATTN_Q_BLOCKS:
                    xv = xv * ATTN_SCALE
            o_ref[:, cols] = xv.astype(o_ref.dtype)

    row = pl.BlockSpec((ROW_TILE, W), lambda i: (i, 0))
    tab = pl.BlockSpec((ROW_TILE, LANES), lambda i: (tidx(i), 0))
    return pl.pallas_call(
        body, name=name, grid=(T // ROW_TILE,),
        in_specs=[row, pl.BlockSpec((2, 1, LANES), lambda i: (0, 0, 0)), tab, tab,
                  pl.BlockSpec((LANES, LANES), lambda i: (0, 0))],
        out_specs=row, out_shape=jax.ShapeDtypeStruct(qkv.shape, MXU_DTYPE),
        compiler_params=_params())(qkv, gains, cos, sin, _head_ones())


def _attn_prep_bwd(dqk, dv, qkv, gains, cos, sin, tidx, name):
    T, W = qkv.shape
    qk_w = ATTN_QK_BLOCKS * LANES

    def body(dqk_ref, dv_ref, x_ref, g_ref, cos_ref, sin_ref, ones_ref, o_ref, dg_ref):
        @pl.when(pl.program_id(0) == 0)
        def _():
            dg_ref[...] = jnp.zeros_like(dg_ref)

        for cb in range(ATTN_QK_BLOCKS):
            cols = slice(cb * LANES, (cb + 1) * LANES)
            xv, d = x_ref[:, cols], dqk_ref[:, cols]
            if cb < ATTN_Q_BLOCKS:
                d = d * ATTN_SCALE
            r = lax.rsqrt(_head_sum(xv * xv, ones_ref) * (1.0 / HEAD_DIM) + EPS)
            xn = xv * r
            dy = d * cos_ref[...] + _pair_swap(d * sin_ref[...], HEAD_DIM // 4)
            dg_ref[:, cols] += jnp.sum(dy * xn, axis=0, keepdims=True)
            dxn = dy * g_ref[0 if cb < ATTN_Q_BLOCKS else 1]
            dx = r * (dxn - xn * (_head_sum(dxn * xn, ones_ref) * (1.0 / HEAD_DIM)))
            o_ref[:, cols] = dx.astype(o_ref.dtype)
        o_ref[:, qk_w:] = dv_ref[...].astype(o_ref.dtype)

    row = lambda w: pl.BlockSpec((ROW_TILE, w), lambda i: (i, 0))
    tab = pl.BlockSpec((ROW_TILE, LANES), lambda i: (tidx(i), 0))
    return pl.pallas_call(
        body, name=name, grid=(T // ROW_TILE,),
        in_specs=[row(qk_w), row(W - qk_w), row(W), pl.BlockSpec((2, 1, LANES), lambda i: (0, 0, 0)), tab, tab,
                  pl.BlockSpec((LANES, LANES), lambda i: (0, 0))],
        out_specs=(row(W), pl.BlockSpec((1, qk_w), lambda i: (0, 0))),
        out_shape=(jax.ShapeDtypeStruct(qkv.shape, MXU_DTYPE), jax.ShapeDtypeStruct((1, qk_w), F32)),
        compiler_params=_params())(dqk, dv, qkv, gains, cos, sin, _head_ones())


RET_QK_BLOCKS = 2 * RET_HEADS * RET_QK_DIM // LANES


def _ret_rope(x, cos, sin, tidx, name):
    T = x.shape[0]
    W = RET_QK_BLOCKS * LANES
    k_scale = RET_QK_DIM ** -0.5

    def body(x_ref, cos_ref, sin_ref, o_ref):
        for cb in range(RET_QK_BLOCKS):
            cols = slice(cb * LANES, (cb + 1) * LANES)
            tcols = slice((cb % 2) * LANES, (cb % 2 + 1) * LANES)
            xv = x_ref[:, cols]
            out = xv * cos_ref[:, tcols] + pltpu.roll(xv, LANES // 2, axis=1) * sin_ref[:, tcols]
            if cb >= RET_QK_BLOCKS // 2:
                out = out * k_scale
            o_ref[:, cols] = out

    row = pl.BlockSpec((ROW_TILE, W), lambda i: (i, 0))
    tab = pl.BlockSpec((ROW_TILE, RET_QK_DIM), lambda i: (tidx(i), 0))
    return pl.pallas_call(
        body, name=name, grid=(T // ROW_TILE,), in_specs=[row, tab, tab], out_specs=row,
        out_shape=jax.ShapeDtypeStruct((T, W), F32), compiler_params=_params())(x, cos, sin)


ASSEMBLE_ROWS = 256


def _ret_grad_assemble(x_parts, c_parts, dg, cos, sin, seq, name):
    NX, NC = x_parts[0].shape[0], c_parts[0].shape[0]
    T = NX + NC
    rt = ASSEMBLE_ROWS
    nxt = NX // rt
    qk_w = RET_HEADS * RET_QK_DIM
    k_scale = RET_QK_DIM ** -0.5

    def unrotate(d, cos_ref, sin_ref, scale):
        outs = []
        for cb in range(qk_w // LANES):
            cols = slice(cb * LANES, (cb + 1) * LANES)
            tcols = slice((cb % 2) * LANES, (cb % 2 + 1) * LANES)
            dv_ = d[:, cols]
            o = dv_ * cos_ref[:, tcols] + pltpu.roll(dv_ * sin_ref[:, tcols], LANES // 2, axis=1)
            outs.append(o * scale if scale != 1.0 else o)
        return outs

    def body(dqf, dqb, dkf, dkb, dvf, dvb, dg_ref, dkcf, dkcb, dvcf, dvcb, cos_ref, sin_ref, o_ref):
        i = pl.program_id(0)

        def write_k(parts):
            for cb, o in enumerate(parts):
                o_ref[:, qk_w + cb * LANES:qk_w + (cb + 1) * LANES] = o.astype(o_ref.dtype)

        @pl.when(i < nxt)
        def _():
            for cb, o in enumerate(unrotate(dqf[...] + dqb[...], cos_ref, sin_ref, 1.0)):
                o_ref[:, cb * LANES:(cb + 1) * LANES] = o.astype(o_ref.dtype)
            write_k(unrotate(dkf[...] + dkb[...], cos_ref, sin_ref, k_scale))
            o_ref[:, 2 * qk_w:2 * qk_w + RET_VWIDTH] = (dvf[...] + dvb[...]).astype(o_ref.dtype)
            o_ref[:, 2 * qk_w + RET_VWIDTH:] = dg_ref[...].astype(o_ref.dtype)

        @pl.when(i >= nxt)
        def _():
            o_ref[:, :qk_w] = jnp.zeros((rt, qk_w), o_ref.dtype)
            write_k(unrotate(dkcf[...] + dkcb[...], cos_ref, sin_ref, k_scale))
            o_ref[:, 2 * qk_w:2 * qk_w + RET_VWIDTH] = (dvcf[...] + dvcb[...]).astype(o_ref.dtype)
            o_ref[:, 2 * qk_w + RET_VWIDTH:] = jnp.zeros((rt, RET_VWIDTH), o_ref.dtype)

    xs = lambda w: pl.BlockSpec((rt, w), lambda i: (jnp.minimum(i, nxt - 1), 0))
    cs = lambda w: pl.BlockSpec((rt, w), lambda i: (jnp.maximum(i - nxt, 0), 0))
    tab = pl.BlockSpec((rt, RET_QK_DIM), lambda i: (jnp.where(i < nxt, i % (seq // rt), seq // rt), 0))
    return pl.pallas_call(
        body, name=name, grid=(T // rt,),
        in_specs=[xs(qk_w)] * 4 + [xs(RET_VWIDTH)] * 3 + [cs(qk_w)] * 2 + [cs(RET_VWIDTH)] * 2 + [tab, tab],
        out_specs=pl.BlockSpec((rt, 2 * qk_w + 2 * RET_VWIDTH), lambda i: (i, 0)),
        out_shape=jax.ShapeDtypeStruct((T, 2 * qk_w + 2 * RET_VWIDTH), MXU_DTYPE),
        compiler_params=_params())(*x_parts, dg, *c_parts, cos, sin)


def _band_bias(qb, seq):
    nb = seq // qb
    assert nb >= 2
    i = jnp.arange(GQA_GROUP * qb, dtype=jnp.int32)[:, None] % qb
    n = jnp.arange(3 * qb, dtype=jnp.int32)[None, :]
    in_window = (n >= i) & (n - i <= 2 * WINDOW)
    variants = [in_window & (n >= qb), in_window, in_window & (n < 2 * qb)]
    return jnp.stack([jnp.where(v, 0.0, NEG_INF).astype(F32) for v in variants])


GROUP_ORDER = (0, 2, 1, 3)


def _stack_halves(blk):
    return jnp.concatenate([blk[:, :LANES], blk[:, LANES:]], axis=0)


def _unstack_halves(v, rows):
    return jnp.concatenate([v[:rows], v[rows:]], axis=1)


def _align_head(pair, odd):
    lane = lax.broadcasted_iota(jnp.int32, pair.shape, 1)
    mine = jnp.where((lane >= HEAD_DIM) == odd, pair, jnp.zeros_like(pair))
    rolled = pltpu.roll(mine, HEAD_DIM, axis=1)
    return jnp.where(odd, rolled, mine), jnp.where(odd, mine, rolled)


def _scores(out_ref, q2, x_eo):
    half = q2.shape[0]
    out_ref[:half, :] = _dot(q2, x_eo[0], _NT)
    out_ref[half:, :] = _dot(q2, x_eo[1], _NT)


def _apply(p_ref, x_eo):
    half = p_ref.shape[0] // 2
    return _dot(p_ref[:half, :], x_eo[0], _NN) + _dot(p_ref[half:, :], x_eo[1], _NN)


def _kv_grad(a_ref, q2, odd):
    half = a_ref.shape[0] // 2
    even_t = _dot(q2, a_ref[:half, :], _TN)
    odd_t = _dot(q2, a_ref[half:, :], _TN)
    mine = even_t[:HEAD_DIM] + odd_t[HEAD_DIM:]
    zero = jnp.zeros_like(mine)
    placed = jnp.where(odd, jnp.concatenate([zero, mine], axis=0), jnp.concatenate([mine, zero], axis=0))
    return placed.T


ATTN_ROW_CHUNK = 32


def _softmax_chunks(s_c_ref, s_l_ref, bias_ref, sink_ref, kv_head, qb, emit):
    for r0 in range(0, GQA_GROUP * qb, ATTN_ROW_CHUNK):
        rows = slice(r0, r0 + ATTN_ROW_CHUNK)
        t = r0 // qb
        sink = jnp.full((ATTN_ROW_CHUNK, 1), sink_ref[kv_head, GROUP_ORDER[t]], F32)
        s_c = s_c_ref[rows, :]
        m = jnp.maximum(jnp.max(s_c, axis=-1, keepdims=True), sink)
        s_l = None
        if s_l_ref is not None:
            s_l = s_l_ref[rows, :] + bias_ref[0, rows, :]
            m = jnp.maximum(m, jnp.max(s_l, axis=-1, keepdims=True))
        e_c = jnp.exp(s_c - m)
        e_s = jnp.exp(sink - m)
        den = jnp.sum(e_c, axis=-1, keepdims=True) + e_s
        e_l = None
        if s_l_ref is not None:
            e_l = jnp.exp(s_l - m)
            den = den + jnp.sum(e_l, axis=-1, keepdims=True)
        inv = 1.0 / den
        emit(t, rows, e_c * inv, (None if e_l is None else e_l * inv), e_s * inv)


GROUP_W = GQA_GROUP * HEAD_DIM
K_LANE_BLOCK = N_HEADS * HEAD_DIM // LANES
V_LANE_BLOCK = K_LANE_BLOCK + N_KV_HEADS * HEAD_DIM // LANES


def _attn_specs(B, seq, ctx_len, ctx_queries):
    ctx0 = B * seq // ctx_len
    if ctx_queries:
        qb, nb = ctx_len, 1
        qrow = lambda b, j: ctx0 + b
    else:
        qb, nb = ATTN_BLOCK, seq // ATTN_BLOCK
        qrow = lambda b, j: b * nb + j
    q_spec = pl.BlockSpec((qb, GROUP_W), lambda b, k, j: (qrow(b, j), k))
    c_specs = [pl.BlockSpec((ctx_len, LANES), lambda b, k, j: (ctx0 + b, K_LANE_BLOCK + k // 2)),
               pl.BlockSpec((ctx_len, LANES), lambda b, k, j: (ctx0 + b, V_LANE_BLOCK + k // 2))]
    local = []
    if not ctx_queries:
        near = [lambda j: jnp.maximum(j - 1, 0), lambda j: j, lambda j: jnp.minimum(j + 1, nb - 1)]
        for lane0 in (K_LANE_BLOCK, V_LANE_BLOCK):
            for f in near:
                local.append(pl.BlockSpec((qb, LANES), lambda b, k, j, f=f, lane0=lane0: (b * nb + f(j), lane0 + k // 2)))
        local.append(pl.BlockSpec(
            (1, GQA_GROUP * qb, 3 * qb), lambda b, k, j: (jnp.where(j == 0, 0, jnp.where(j == nb - 1, 2, 1)), 0, 0)))
    return qb, nb, qrow, q_spec, c_specs, local


def _attn_operands(refs, has_local, kv_head):
    odd = (kv_head % 2) == 1
    n_local = 7 if has_local else 0
    q2 = _stack_halves(refs[0][...])
    kc = _align_head(refs[1 + n_local][...], odd)
    vc = _align_head(refs[2 + n_local][...], odd)
    kl = vl = bias_ref = None
    if has_local:
        kl = _align_head(jnp.concatenate([r[...] for r in refs[1:4]], axis=0), odd)
        vl = _align_head(jnp.concatenate([r[...] for r in refs[4:7]], axis=0), odd)
        bias_ref = refs[7]
    return odd, q2, kc, vc, kl, vl, bias_ref


def _score_scratch(qb, ctx_len, has_local, dtypes):
    rows = GQA_GROUP * qb
    out = []
    for dt in dtypes:
        out.append(pltpu.VMEM((rows, ctx_len), dt))
        if has_local:
            out.append(pltpu.VMEM((rows, 3 * qb), dt))
    return out


def _score_bufs(scratch, has_local):
    if has_local:
        return [(scratch[i], scratch[i + 1]) for i in range(0, len(scratch), 2)]
    return [(s, None) for s in scratch]


def _attn_fwd(qkv, sink, B, seq, ctx_len, ctx_queries, name):
    has_local = not ctx_queries
    qb, nb, _, q_spec, c_specs, local = _attn_specs(B, seq, ctx_len, ctx_queries)
    n_rows = B * (ctx_len if ctx_queries else seq)
    n_in = 1 + (7 if has_local else 0) + 3

    def body(*refs):
        sink_ref, o_ref = refs[n_in - 1], refs[n_in]
        (s_c_ref, s_l_ref), (p_c_ref, p_l_ref) = _score_bufs(refs[n_in + 1:], has_local)
        kv_head = pl.program_id(1)
        _, q2, kc, vc, kl, vl, bias_ref = _attn_operands(refs, has_local, kv_head)
        _scores(s_c_ref, q2, kc)
        if has_local:
            _scores(s_l_ref, q2, kl)

        def emit(t, rows, p_c, p_l, p_s):
            p_c_ref[rows, :] = p_c.astype(p_c_ref.dtype)
            if has_local:
                p_l_ref[rows, :] = p_l.astype(p_l_ref.dtype)

        _softmax_chunks(s_c_ref, s_l_ref, bias_ref, sink_ref, kv_head, qb, emit)
        o2 = _apply(p_c_ref, vc)
        if has_local:
            o2 = o2 + _apply(p_l_ref, vl)
        o_ref[...] = _unstack_halves(o2, qb).astype(o_ref.dtype)

    operands = [qkv] + ([qkv] * 6 + [_band_bias(qb, seq)] if has_local else []) + [qkv, qkv, sink]
    return pl.pallas_call(
        body, name=name, grid=(B, N_KV_HEADS, nb),
        in_specs=[q_spec] + local + c_specs + [_SMEM],
        out_specs=pl.BlockSpec((qb, GROUP_W), lambda b, k, j: (b * nb + j, k)),
        out_shape=jax.ShapeDtypeStruct((n_rows, N_HEADS * HEAD_DIM), MXU_DTYPE),
        scratch_shapes=_score_scratch(qb, ctx_len, has_local, (F32, MXU_DTYPE)),
        compiler_params=_params())(*operands)


def _attn_bwd(qkv, sink, do, B, seq, ctx_len, ctx_queries, name):
    has_local = not ctx_queries
    qb, nb, qrow, q_spec, c_specs, local = _attn_specs(B, seq, ctx_len, ctx_queries)
    n_rows = B * (ctx_len if ctx_queries else seq)

    n_out = 6 if has_local else 4

    def body(*refs):
        n_in = 1 + (7 if has_local else 0) + 4
        sink_ref, do_ref = refs[n_in - 2:n_in]
        outs = refs[n_in:n_in + n_out]
        (s_c_ref, s_l_ref), (dp_c_ref, dp_l_ref), (p_c_ref, p_l_ref), (ds_c_ref, ds_l_ref) = _score_bufs(
            refs[n_in + n_out:], has_local)
        dq_ref = outs[0]
        dkc_ref, dvc_ref, dsink_ref = outs[-3:]
        b, kv_head, j = pl.program_id(0), pl.program_id(1), pl.program_id(2)
        odd, q2, kc, vc, kl, vl, bias_ref = _attn_operands(refs, has_local, kv_head)
        do2 = _stack_halves(do_ref[...])
        _scores(s_c_ref, q2, kc)
        _scores(dp_c_ref, do2, vc)
        if has_local:
            _scores(s_l_ref, q2, kl)
            _scores(dp_l_ref, do2, vl)
        dsink_parts = [jnp.zeros((), F32)] * GQA_GROUP

        def emit(t, rows, p_c, p_l, p_s):
            dp_c = dp_c_ref[rows, :]
            delta = jnp.sum(p_c * dp_c, axis=-1, keepdims=True)
            if has_local:
                dp_l = dp_l_ref[rows, :]
                delta = delta + jnp.sum(p_l * dp_l, axis=-1, keepdims=True)
                p_l_ref[rows, :] = p_l.astype(p_l_ref.dtype)
                ds_l_ref[rows, :] = (p_l * (dp_l - delta)).astype(ds_l_ref.dtype)
            p_c_ref[rows, :] = p_c.astype(p_c_ref.dtype)
            ds_c_ref[rows, :] = (p_c * (dp_c - delta)).astype(ds_c_ref.dtype)
            dsink_parts[t] = dsink_parts[t] - jnp.sum(p_s * delta)

        _softmax_chunks(s_c_ref, s_l_ref, bias_ref, sink_ref, kv_head, qb, emit)
        dq2 = _apply(ds_c_ref, kc)

        @pl.when((kv_head % 2 == 0) & (j == 0))
        def _():
            dkc_ref[...] = jnp.zeros_like(dkc_ref)
            dvc_ref[...] = jnp.zeros_like(dvc_ref)
            if has_local:
                outs[1][...] = jnp.zeros_like(outs[1])
                outs[2][...] = jnp.zeros_like(outs[2])

        @pl.when((b == 0) & (kv_head == 0) & (j == 0))
        def _():
            dsink_ref[...] = jnp.zeros_like(dsink_ref)

        dkc_ref[...] += _kv_grad(ds_c_ref, q2, odd)
        dvc_ref[...] += _kv_grad(p_c_ref, do2, odd)
        if has_local:
            dq2 = dq2 + _apply(ds_l_ref, kl)
            dkl = _kv_grad(ds_l_ref, q2, odd)
            dvl = _kv_grad(p_l_ref, do2, odd)
            dk_ref, dv_ref = outs[1], outs[2]
            for t in range(3):
                def add(t=t):
                    start = pl.multiple_of((j - 1 + t) * qb, qb)
                    dk_ref[pl.ds(start, qb), :] += dkl[t * qb:(t + 1) * qb]
                    dv_ref[pl.ds(start, qb), :] += dvl[t * qb:(t + 1) * qb]
                if t == 0:
                    pl.when(j > 0)(add)
                elif t == 2:
                    pl.when(j < nb - 1)(add)
                else:
                    add()
        dq_ref[...] = _unstack_halves(dq2, qb)
        sub = lax.broadcasted_iota(jnp.int32, (8, LANES), 0)
        tile = jnp.zeros((8, LANES), F32)
        for t, gi in enumerate(GROUP_ORDER):
            tile = jnp.where(sub == gi, dsink_parts[t], tile)
        dsink_ref[pl.ds(pl.multiple_of(kv_head * 8, 8), 8), :] += tile

    kv_w = N_KV_HEADS * HEAD_DIM
    seq_spec = pl.BlockSpec((seq, LANES), lambda b, k, j: (b, k // 2))
    ctx_spec = pl.BlockSpec((ctx_len, LANES), lambda b, k, j: (b, k // 2))
    do_spec = pl.BlockSpec((qb, GROUP_W), lambda b, k, j: (qrow(b, j), k))
    operands = [qkv] + ([qkv] * 6 + [_band_bias(qb, seq)] if has_local else []) + [qkv, qkv, sink, do]
    out_specs = ([pl.BlockSpec((qb, GROUP_W), lambda b, k, j: (b * nb + j, k))] + ([seq_spec, seq_spec] if has_local else [])
                 + [ctx_spec, ctx_spec, pl.BlockSpec((32, LANES), lambda b, k, j: (0, 0))])
    out_shape = ([jax.ShapeDtypeStruct((n_rows, N_HEADS * HEAD_DIM), F32)]
                 + ([jax.ShapeDtypeStruct((B * seq, kv_w), F32)] * 2 if has_local else [])
                 + [jax.ShapeDtypeStruct((B * ctx_len, kv_w), F32)] * 2 + [jax.ShapeDtypeStruct((32, LANES), F32)])
    return pl.pallas_call(
        body, name=name, grid=(B, N_KV_HEADS, nb),
        in_specs=[q_spec] + local + c_specs + [_SMEM, do_spec],
        out_specs=tuple(out_specs), out_shape=tuple(out_shape),
        scratch_shapes=_score_scratch(qb, ctx_len, has_local, (F32, F32, MXU_DTYPE, MXU_DTYPE)),
        compiler_params=_params())(*operands)


def _ret_decays(lg, rev):
    n = lax.broadcasted_iota(jnp.int32, (RET_CHUNK, RET_CHUNK), 0).astype(F32)
    m = lax.broadcasted_iota(jnp.int32, (RET_CHUNK, RET_CHUNK), 1).astype(F32)
    pos = lax.broadcasted_iota(jnp.int32, (RET_CHUNK, 1), 0).astype(F32)
    diff = (m - n) if rev else (n - m)
    a_exp = jnp.maximum(diff, 0.0)
    intra = jnp.where(diff >= 0, jnp.exp(lg * a_exp), 0.0)
    q_exp = (RET_CHUNK - pos) if rev else (pos + 1.0)
    k_exp = pos if rev else (RET_CHUNK - 1.0 - pos)
    chunk = jnp.exp(jnp.full((1, 1), RET_CHUNK, F32) * lg)
    return intra, a_exp, jnp.exp(lg * q_exp), q_exp, jnp.exp(lg * k_exp), k_exp, chunk


def _ctx_decay(lg, ctx_len, rev):
    t = lax.broadcasted_iota(jnp.int32, (ctx_len, 1), 0).astype(F32)
    expo = t if rev else (ctx_len - 1.0 - t)
    return jnp.exp(lg * expo), expo


def _ret_specs(B, seq, ctx_len, order):
    nc = seq // RET_CHUNK
    x_blocks = B * seq // ctx_len

    def rows(b, c):
        return b * nc + order(c, nc)

    q_spec = pl.BlockSpec((RET_CHUNK, RET_QK_DIM), lambda b, h, c: (rows(b, c), h))
    k_spec = pl.BlockSpec((RET_CHUNK, RET_QK_DIM), lambda b, h, c: (rows(b, c), RET_HEADS + h))
    v_spec = pl.BlockSpec((RET_CHUNK, RET_V_DIM), lambda b, h, c: (rows(b, c), RET_HEADS + h))
    kc_spec = pl.BlockSpec((ctx_len, RET_QK_DIM), lambda b, h, c: (x_blocks + b, RET_HEADS + h))
    vc_spec = pl.BlockSpec((ctx_len, RET_V_DIM), lambda b, h, c: (x_blocks + b, RET_HEADS + h))
    st_spec = pl.BlockSpec((1, 1, 1, RET_QK_DIM, RET_V_DIM), lambda b, h, c: (b, h, order(c, nc), 0, 0))
    o_spec = pl.BlockSpec((RET_CHUNK, RET_V_DIM), lambda b, h, c: (rows(b, c), h))
    return nc, q_spec, k_spec, v_spec, kc_spec, vc_spec, st_spec, o_spec


_SCAN_UP = lambda c, nc: c
_SCAN_DOWN = lambda c, nc: nc - 1 - c


def _ret_fwd(qk, qkvg, log_g, B, seq, ctx_len, name):
    nc, qf, kf, vf, kc_spec, vc_spec, stf, of = _ret_specs(B, seq, ctx_len, _SCAN_UP)
    _, qr, kr, vr, _, _, str_, or_ = _ret_specs(B, seq, ctx_len, _SCAN_DOWN)

    def body(lg_ref, qf_ref, kf_ref, vf_ref, qr_ref, kr_ref, vr_ref, kc_ref, vc_ref,
             of_ref, stf_ref, or_ref, str_ref, state_f, state_r):
        h, c = pl.program_id(1), pl.program_id(2)
        dirs = ((False, lg_ref[0, h], qf_ref, kf_ref, vf_ref, of_ref, stf_ref, state_f),
                (True, lg_ref[1, h], qr_ref, kr_ref, vr_ref, or_ref, str_ref, state_r))

        @pl.when(c == 0)
        def _():
            for rev, lg, _, _, _, _, _, state in dirs:
                dec, _ = _ctx_decay(lg, ctx_len, rev)
                state[...] = _dot(kc_ref[...] * dec, vc_ref[...], _TN)

        for rev, lg, q_ref, k_ref, v_ref, o_ref, st_ref, state in dirs:
            intra, _, q_dec, _, k_dec, _, chunk_dec = _ret_decays(lg, rev)
            qv, kv, vv = q_ref[...], k_ref[...], v_ref[...]
            s_in = state[...]
            st_ref[0, 0, 0] = s_in
            w = _dot(qv, kv, _NT) * intra
            o_ref[...] = _dot(w, vv, _NN) + _dot(qv, s_in, _NN) * q_dec
            state[...] = s_in * chunk_dec + _dot(kv * k_dec, vv, _TN)

    o_shape = jax.ShapeDtypeStruct((B * seq, RET_VWIDTH), F32)
    st_shape = jax.ShapeDtypeStruct((B, RET_HEADS, nc, RET_QK_DIM, RET_V_DIM), F32)
    return pl.pallas_call(
        body, name=name, grid=(B, RET_HEADS, nc),
        in_specs=[_SMEM, qf, kf, vf, qr, kr, vr, kc_spec, vc_spec],
        out_specs=(of, stf, or_, str_), out_shape=(o_shape, st_shape, o_shape, st_shape),
        scratch_shapes=[pltpu.VMEM((RET_QK_DIM, RET_V_DIM), F32)] * 2,
        compiler_params=_params())(log_g, qk, qk, qkvg, qk, qk, qkvg, qk, qkvg)


def _ret_bwd_chunk(rev, lg, q_ref, k_ref, v_ref, st_ref, do_ref, dq_ref, dk_ref, dv_ref, dlg_ref, dstate):
    intra, a_exp, q_dec, q_exp, k_dec, k_exp, chunk_dec = _ret_decays(lg, rev)
    qv, kv, vv, dov = q_ref[...], k_ref[...], v_ref[...], do_ref[...]
    s_in, ds_out = st_ref[0, 0, 0], dstate[...]
    p = _dot(qv, kv, _NT)
    w = p * intra
    dw = _dot(dov, vv, _NT)
    dp = dw * intra
    do_dec = dov * q_dec
    kd = kv * k_dec
    v_ds = _dot(vv, ds_out, _NT)
    dq_ref[...] = _dot(dp, kv, _NN) + _dot(do_dec, s_in, _NT)
    dk_ref[...] = _dot(dp, qv, _TN) + v_ds * k_dec
    dv_ref[...] = _dot(w, dov, _TN) + _dot(kd, ds_out, _NN)
    q_s = _dot(qv, s_in, _NN)
    dlg = (jnp.sum(dw * w * a_exp)
           + jnp.sum(q_exp * q_dec * jnp.sum(dov * q_s, axis=-1, keepdims=True))
           + jnp.sum(k_exp * k_dec * jnp.sum(kv * v_ds, axis=-1, keepdims=True))
           + RET_CHUNK * jnp.sum(chunk_dec * (ds_out * s_in)))
    ds_in = ds_out * chunk_dec + _dot(qv, do_dec, _TN)
    dstate[...] = ds_in
    dlg_ref[...] += dlg
    return ds_in


def _ret_bwd(qk, qkvg, log_g, st_f, st_r, do, B, seq, ctx_len, name):
    nc, qf, kf, vf, kc_spec, vc_spec, stf, of = _ret_specs(B, seq, ctx_len, _SCAN_DOWN)
    _, qr, kr, vr, _, _, str_, or_ = _ret_specs(B, seq, ctx_len, _SCAN_UP)

    def body(lg_ref, qf_ref, kf_ref, vf_ref, stf_ref, dof_ref, qr_ref, kr_ref, vr_ref, str_ref, dor_ref, kc_ref, vc_ref,
             dqf, dkf, dvf, dkcf, dvcf, dlgf, dqr, dkr, dvr, dkcr, dvcr, dlgr, dstate_f, dstate_r):
        h, c = pl.program_id(1), pl.program_id(2)
        dirs = ((False, lg_ref[0, h], (qf_ref, kf_ref, vf_ref, stf_ref, dof_ref, dqf, dkf, dvf, dlgf, dstate_f), dkcf, dvcf),
                (True, lg_ref[1, h], (qr_ref, kr_ref, vr_ref, str_ref, dor_ref, dqr, dkr, dvr, dlgr, dstate_r), dkcr, dvcr))

        @pl.when(c == 0)
        def _():
            for _, _, refs, _, _ in dirs:
                refs[-1][...] = jnp.zeros_like(refs[-1])
                refs[-2][...] = jnp.zeros_like(refs[-2])

        ds_first = [_ret_bwd_chunk(rev, lg, *refs) for rev, lg, refs, _, _ in dirs]

        @pl.when(c == nc - 1)
        def _():
            for (rev, lg, refs, dkc_ref, dvc_ref), ds_in in zip(dirs, ds_first):
                dec, expo = _ctx_decay(lg, ctx_len, rev)
                kcv, vcv = kc_ref[...], vc_ref[...]
                vc_ds = _dot(vcv, ds_in, _NT)
                dkc_ref[...] = vc_ds * dec
                dvc_ref[...] = _dot(kcv * dec, ds_in, _NN)
                refs[-2][...] += jnp.sum(expo * dec * jnp.sum(kcv * vc_ds, axis=-1, keepdims=True))

    def outs(q_spec, o_spec):
        return (pl.BlockSpec((RET_CHUNK, RET_QK_DIM), q_spec.index_map),
                pl.BlockSpec((RET_CHUNK, RET_QK_DIM), q_spec.index_map), o_spec,
                pl.BlockSpec((ctx_len, RET_QK_DIM), lambda b, h, c: (b, h)),
                pl.BlockSpec((ctx_len, RET_V_DIM), lambda b, h, c: (b, h)),
                pl.BlockSpec((1, 1, 8, LANES), lambda b, h, c: (b, h, 0, 0)))

    shapes = (jax.ShapeDtypeStruct((B * seq, RET_HEADS * RET_QK_DIM), F32),
              jax.ShapeDtypeStruct((B * seq, RET_HEADS * RET_QK_DIM), F32),
              jax.ShapeDtypeStruct((B * seq, RET_VWIDTH), F32),
              jax.ShapeDtypeStruct((B * ctx_len, RET_HEADS * RET_QK_DIM), F32),
              jax.ShapeDtypeStruct((B * ctx_len, RET_VWIDTH), F32),
              jax.ShapeDtypeStruct((B, RET_HEADS, 8, LANES), F32))
    res = pl.pallas_call(
        body, name=name, grid=(B, RET_HEADS, nc),
        in_specs=[_SMEM, qf, kf, vf, stf, of, qr, kr, vr, str_, or_, kc_spec, vc_spec],
        out_specs=outs(qf, of) + outs(qr, or_), out_shape=shapes + shapes,
        scratch_shapes=[pltpu.VMEM((RET_QK_DIM, RET_V_DIM), F32)] * 2,
        compiler_params=_params())(log_g, qk, qk, qkvg, st_f, do, qk, qk, qkvg, st_r, do, qk, qkvg)
    return res[:6], res[6:]


def _gated_out_fwd(o_f, o_b, qkvg, gn_gain, name):
    T = o_f.shape[0]
    g_off = (2 * RET_HEADS * RET_QK_DIM + RET_VWIDTH) // RET_V_DIM

    def body(of_ref, ob_ref, g_ref, gain_ref, z_ref):
        o = of_ref[...] + ob_ref[...]
        mu = jnp.mean(o, axis=-1, keepdims=True)
        var = jnp.mean(jnp.square(o - mu), axis=-1, keepdims=True)
        y = (o - mu) * lax.rsqrt(var + EPS) * gain_ref[...]
        gv = g_ref[...]
        z_ref[...] = (gv * jax.nn.sigmoid(gv) * y).astype(z_ref.dtype)

    blk = pl.BlockSpec((ROW_TILE, RET_V_DIM), lambda i, h: (i, h))
    return pl.pallas_call(
        body, name=name, grid=(T // ROW_TILE, RET_HEADS),
        in_specs=[blk, blk, pl.BlockSpec((ROW_TILE, RET_V_DIM), lambda i, h: (i, g_off + h)),
                  pl.BlockSpec((1, RET_V_DIM), lambda i, h: (0, h))],
        out_specs=blk, out_shape=jax.ShapeDtypeStruct((T, RET_VWIDTH), MXU_DTYPE),
        compiler_params=_params())(o_f, o_b, qkvg, gn_gain)


def _gated_out_bwd(dz, o_f, o_b, qkvg, gn_gain, name):
    T = o_f.shape[0]
    g_off = (2 * RET_HEADS * RET_QK_DIM + RET_VWIDTH) // RET_V_DIM

    def body(dz_ref, of_ref, ob_ref, g_ref, gain_ref, do_ref, dg_ref, dgain_ref):
        o = of_ref[...] + ob_ref[...]
        mu = jnp.mean(o, axis=-1, keepdims=True)
        var = jnp.mean(jnp.square(o - mu), axis=-1, keepdims=True)
        rstd = lax.rsqrt(var + EPS)
        yhat = (o - mu) * rstd
        gv, dzv = g_ref[...], dz_ref[...]
        sg = jax.nn.sigmoid(gv)
        dg_ref[...] = (dzv * (yhat * gain_ref[...]) * (sg * (1.0 + gv * (1.0 - sg)))).astype(dg_ref.dtype)
        dy = dzv * (gv * sg)

        @pl.when(pl.program_id(1) == 0)
        def _():
            dgain_ref[...] = jnp.zeros_like(dgain_ref)

        dgain_ref[...] += jnp.sum(dy * yhat, axis=0, keepdims=True)
        dyh = dy * gain_ref[...]
        do_ref[...] = rstd * (dyh - jnp.mean(dyh, axis=-1, keepdims=True)
                              - yhat * jnp.mean(dyh * yhat, axis=-1, keepdims=True))

    blk = pl.BlockSpec((ROW_TILE, RET_V_DIM), lambda h, i: (i, h))
    vec = pl.BlockSpec((1, RET_V_DIM), lambda h, i: (0, h))
    return pl.pallas_call(
        body, name=name, grid=(RET_HEADS, T // ROW_TILE),
        in_specs=[blk, blk, blk, pl.BlockSpec((ROW_TILE, RET_V_DIM), lambda h, i: (i, g_off + h)), vec],
        out_specs=(blk, blk, vec),
        out_shape=(jax.ShapeDtypeStruct((T, RET_VWIDTH), F32), jax.ShapeDtypeStruct((T, RET_VWIDTH), MXU_DTYPE),
                   jax.ShapeDtypeStruct((1, RET_VWIDTH), F32)),
        compiler_params=_params())(dz, o_f, o_b, qkvg, gn_gain)


def _adamw(w, m, v, parts, name):
    R, C = w.shape
    tr = _tile(R, (256, 128, 64, 32, 16, 8))
    n_parts = [p.shape[0] for p in parts]

    def body(*refs):
        w_ref, m_ref, v_ref = refs[:3]
        part_refs = refs[3:3 + len(parts)]
        g_ref, d_ref, nm_ref, nv_ref = refs[3 + len(parts):]
        g = None
        for ref, n in zip(part_refs, n_parts):
            for r in range(n):
                term = ref[r].astype(F32)
                g = term if g is None else g + term
        mn = ADAM_B1 * m_ref[...] + (1.0 - ADAM_B1) * g
        vn = ADAM_B2 * v_ref[...] + (1.0 - ADAM_B2) * jnp.square(g)
        m_hat = mn / (1.0 - ADAM_B1 ** ADAM_STEP)
        v_hat = vn / (1.0 - ADAM_B2 ** ADAM_STEP)
        g_ref[...] = g
        d_ref[...] = -ADAM_LR * (m_hat / (jnp.sqrt(v_hat) + ADAM_EPS) + ADAM_WD * w_ref[...])
        nm_ref[...] = mn
        nv_ref[...] = vn

    blk = pl.BlockSpec((tr, C), lambda i: (i, 0))
    part_specs = [pl.BlockSpec((n, tr, C), lambda i: (0, i, 0)) for n in n_parts]
    shp = jax.ShapeDtypeStruct((R, C), F32)
    return pl.pallas_call(
        body, name=name, grid=(R // tr,), in_specs=[blk, blk, blk] + part_specs,
        out_specs=(blk, blk, blk, blk), out_shape=(shp, shp, shp, shp),
        compiler_params=_params())(w, m, v, *parts)


def _sum_rows(parts, name):
    n, R, C = parts.shape
    tr = _tile(R, (256, 128, 64, 32, 16, 8))

    def body(p_ref, o_ref):
        acc = p_ref[0]
        for r in range(1, n):
            acc = acc + p_ref[r]
        o_ref[...] = acc

    return pl.pallas_call(
        body, name=name, grid=(R // tr,), in_specs=[pl.BlockSpec((n, tr, C), lambda i: (0, i, 0))],
        out_specs=pl.BlockSpec((tr, C), lambda i: (i, 0)), out_shape=jax.ShapeDtypeStruct((R, C), F32),
        compiler_params=_params())(parts)


def _my_coords():
    return lax.axis_index("x"), lax.axis_index("y"), lax.axis_index("c")


def _flip(coord, bit):
    return 1 - coord if bit else coord


def _all_gather(x2d, name):
    R, C = x2d.shape

    def body(x_ref, out_ref, send_sems, recv_sems, local_sem):
        x, y, c = _my_coords()
        me, sibling = (x, y, c), (x, y, 1 - c)
        chips = [(1 - x, y), (x, 1 - y), (1 - x, 1 - y)]

        def rows(px, py, pc):
            return out_ref.at[4 * px + 2 * py + pc]

        def copy(k, block, to, src=None):
            return pltpu.make_async_remote_copy(
                src_ref=rows(*block) if src is None else src, dst_ref=rows(*block),
                send_sem=send_sems.at[k], recv_sem=recv_sems.at[k], device_id=to, device_id_type=MESH)

        mine = pltpu.make_async_copy(x_ref, rows(*me), local_sem)
        mine.start()
        first = [copy(0, me, sibling, src=x_ref)]
        first += [copy(1 + j, me, (*chip, c), src=x_ref) for j, chip in enumerate(chips)]
        for cp in first:
            cp.start()
        passed = [copy(4 + j, (*chip, c), sibling) for j, chip in enumerate(chips)]
        for j, chip in enumerate(chips):
            copy(1 + j, (*chip, c), me).wait_recv()
            passed[j].start()
        copy(0, sibling, me).wait_recv()
        for j, chip in enumerate(chips):
            copy(4 + j, (*chip, 1 - c), me).wait_recv()
        for cp in first + passed:
            cp.wait_send()
        mine.wait()

    return pl.pallas_call(
        body, name=name, out_shape=jax.ShapeDtypeStruct((N_DEV, R, C), x2d.dtype),
        in_specs=[_ANY], out_specs=_ANY,
        scratch_shapes=[pltpu.SemaphoreType.DMA((7,)), pltpu.SemaphoreType.DMA((7,)), pltpu.SemaphoreType.DMA],
    )(x2d)


BIG_WEIGHTS = {
    "ffn_w_in": (2, (2, D_MODEL, 2 * D_FF)),
    "ffn_w_out": (1, (2, D_FF, D_MODEL)),
    "attn_w_qkv": (2, (1, D_MODEL, (N_HEADS + 2 * N_KV_HEADS) * HEAD_DIM)),
    "attn_w_o": (1, (1, N_HEADS * HEAD_DIM, D_MODEL)),
    "ret_w_qkvg": (2, (1, D_MODEL, 2 * D_MODEL + 2 * RET_VWIDTH)),
    "ret_gn_g": (2, (1, 1, RET_VWIDTH)),
    "ret_w_o": (1, (1, RET_VWIDTH, D_MODEL)),
}


def _join_shards(name, stacked):
    axis, full = BIG_WEIGHTS[name]
    if axis == 2:
        stacked = stacked.transpose(0, 2, 1, 3)
    return stacked.reshape(full)


def _split_shards(name, full_arr):
    axis, (_, rows, cols) = BIG_WEIGHTS[name]
    L = full_arr.shape[0]
    if axis == 2:
        return full_arr.reshape(L, rows, N_DEV, cols // N_DEV).transpose(0, 2, 1, 3)
    return full_arr.reshape(L, N_DEV, rows // N_DEV, cols)


def _gather_shards(shards, name):
    n = len(shards)

    def body(*refs):
        x_refs, out_refs = refs[:n], refs[n:2 * n]
        send_sems, recv_sems, local_sems = refs[2 * n:]
        x, y, c = _my_coords()
        me, sibling = (x, y, c), (x, y, 1 - c)
        chips = [(1 - x, y), (x, 1 - y), (1 - x, 1 - y)]

        def rows(a, px, py, pc):
            return out_refs[a].at[:, 4 * px + 2 * py + pc]

        def copy(a, k, block, to, src=None):
            return pltpu.make_async_remote_copy(
                src_ref=rows(a, *block) if src is None else src, dst_ref=rows(a, *block),
                send_sem=send_sems.at[7 * a + k], recv_sem=recv_sems.at[7 * a + k], device_id=to, device_id_type=MESH)

        mine = [pltpu.make_async_copy(x_refs[a], rows(a, *me), local_sems.at[a]) for a in range(n)]
        for cp in mine:
            cp.start()
        first = []
        for a in range(n):
            first.append(copy(a, 0, me, sibling, src=x_refs[a]))
            first += [copy(a, 1 + j, me, (*chip, c), src=x_refs[a]) for j, chip in enumerate(chips)]
        for cp in first:
            cp.start()
        passed = []
        for j, chip in enumerate(chips):
            for a in range(n):
                copy(a, 1 + j, (*chip, c), me).wait_recv()
                fwd = copy(a, 4 + j, (*chip, c), sibling)
                fwd.start()
                passed.append(fwd)
        for a in range(n):
            copy(a, 0, sibling, me).wait_recv()
            for j, chip in enumerate(chips):
                copy(a, 4 + j, (*chip, 1 - c), me).wait_recv()
        for cp in first + passed:
            cp.wait_send()
        for cp in mine:
            cp.wait()

    return pl.pallas_call(
        body, name=name,
        out_shape=[jax.ShapeDtypeStruct((s.shape[0], N_DEV) + s.shape[1:], s.dtype) for s in shards],
        in_specs=[_ANY] * n, out_specs=[_ANY] * n,
        scratch_shapes=[pltpu.SemaphoreType.DMA((7 * n,)), pltpu.SemaphoreType.DMA((7 * n,)),
                        pltpu.SemaphoreType.DMA((n,))],
    )(*shards)


def _exchange_shards(arrs, masks, src_of, out_tail, name):
    n, nm = len(arrs), len(masks)

    def body(*refs):
        in_refs, out_refs = refs[:n], refs[n:2 * n]
        send_sems, recv_sems = refs[2 * n:]
        x, y, c = _my_coords()
        copies = []
        for a in range(n):
            for k, (bx, by, bc) in enumerate(masks):
                peer = (_flip(x, bx), _flip(y, by), _flip(c, bc))
                copies.append(pltpu.make_async_remote_copy(
                    src_ref=src_of(in_refs[a], peer, (x, y, c)), dst_ref=out_refs[a].at[k],
                    send_sem=send_sems.at[nm * a + k], recv_sem=recv_sems.at[nm * a + k],
                    device_id=peer, device_id_type=MESH))
        for cp in copies:
            cp.start()
        for cp in copies:
            cp.wait()

    return pl.pallas_call(
        body, name=name,
        out_shape=[jax.ShapeDtypeStruct((nm,) + out_tail(s), s.dtype) for s in arrs],
        in_specs=[_ANY] * n, out_specs=[_ANY] * n,
        scratch_shapes=[pltpu.SemaphoreType.DMA((nm * n,)), pltpu.SemaphoreType.DMA((nm * n,))],
    )(*arrs)


def _pair_sum(g, from_sibling, core, out_dtype, name):
    L, _, _, a, b = g.shape
    ta = a

    def body(core_ref, g_ref, s_ref, o_ref):
        o_ref[...] = (g_ref[...] + s_ref[...]).astype(out_dtype)

    blk = pl.BlockSpec((1, 1, ta, b), lambda l, q, i, core_ref: (l, q, i, 0))
    return pl.pallas_call(
        body, name=name,
        grid_spec=pltpu.PrefetchScalarGridSpec(
            num_scalar_prefetch=1, grid=(L, 4, a // ta),
            in_specs=[pl.BlockSpec((1, 1, pl.Squeezed(), ta, b), lambda l, q, i, core_ref: (l, q, core_ref[0], i, 0)), blk],
            out_specs=blk),
        out_shape=jax.ShapeDtypeStruct((L, 4, a, b), out_dtype), compiler_params=_params())(core, g, from_sibling)


def _mods(mod_x, mod_c, layer):
    both = jnp.concatenate([mod_x[:, layer], mod_c[layer][None]], axis=0)
    return [both[:, None, k * D_MODEL:(k + 1) * D_MODEL] for k in range(6)]


def _local_step(x, ctx, target, mod_x, mod_c, w, small, late_weights=None, hooks=None):
    B, S, _ = x.shape
    L = ctx.shape[1]
    NX, NC = B * S, B * L
    T = NX + NC
    tiles_per_ex = S // ROW_TILE
    nxt = NX // ROW_TILE
    gidx = _group_index(nxt, tiles_per_ex, B)
    gidx_for = lambda rows: _group_index(NX // rows, S // rows, B)
    mm_rows = _tile(S, (MM_ROWS, ROW_TILE))
    tidx = lambda i: jnp.where(i < nxt, i % tiles_per_ex, tiles_per_ex)
    G = B + 1
    x0 = jnp.concatenate([x.reshape(NX, D_MODEL), ctx.reshape(NC, D_MODEL)], axis=0)
    acos, asin = [jnp.tile(t, (1, LANES // HEAD_DIM)) for t in _rope_tables(S, HEAD_DIM)]
    rcos, rsin = _rope_tables(S, RET_QK_DIM)
    sink = small["attn_sink"].reshape(N_KV_HEADS, GQA_GROUP)
    gains = jnp.stack([jnp.tile(small["attn_q_norm"].reshape(1, HEAD_DIM), (1, LANES // HEAD_DIM)),
                       jnp.tile(small["attn_k_norm"].reshape(1, HEAD_DIM), (1, LANES // HEAD_DIM))])
    log_g = jax.nn.log_sigmoid(small["ret_decay_logit"].reshape(2, RET_HEADS))
    n1, n2 = small["norm1_g"], small["norm2_g"]

    m0 = _mods(mod_x, mod_c, 0)
    h1 = _norm_mod_fwd(x0, n1[0:1], m0[0], m0[1], gidx, "l0_norm1")
    qkv = _mm(h1, w["attn_w_qkv"][0], "nn", F32, "l0_qkv")
    qkv_r = _attn_prep_fwd(qkv, gains, acos, asin, tidx, "l0_qk_prep")
    o_x = _attn_fwd(qkv_r, sink, B, S, L, False, "l0_attn_x")
    o_c = _attn_fwd(qkv_r, sink, B, S, L, True, "l0_attn_c")
    o0 = jnp.concatenate([o_x, o_c], axis=0)
    mo0, x1 = _mm(o0, w["attn_w_o"][0], "nn", F32, "l0_attn_out", res=x0, gate=m0[2], gidx_for=gidx_for, gate_rows=mm_rows)
    h2 = _norm_mod_fwd(x1, n2[0:1], m0[3], m0[4], gidx, "l0_norm2")
    if late_weights is not None:
        w = {**w, **late_weights(x1)}
    u0 = _mm(h2, w["ffn_w_in"][0], "nn", F32, "l0_ffn_in")
    a0 = _swiglu_fwd(u0, "l0_swiglu")
    f0, x2 = _mm(a0, w["ffn_w_out"][0], "nn", F32, "l0_ffn_out", res=x1, gate=m0[5], gidx_for=gidx_for, gate_rows=mm_rows)

    m1 = _mods(mod_x, mod_c, 1)
    g1 = _norm_mod_fwd(x2, n1[1:2], m1[0], m1[1], gidx, "l1_norm1")
    qkvg = _mm(g1, w["ret_w_qkvg"][0], "nn", F32, "l1_qkvg")
    qk = _ret_rope(qkvg, rcos, rsin, tidx, "l1_rope")
    of, st_f, ob, st_b = _ret_fwd(qk, qkvg, log_g, B, S, L, "l1_ret")
    gn = w["ret_gn_g"].reshape(1, RET_VWIDTH)
    z1 = _gated_out_fwd(of, ob, qkvg, gn, "l1_gated_out")
    xx2 = x2[:NX]
    gx = lambda i: i // tiles_per_ex
    m1x = [t[:B] for t in m1]
    mo1, y1 = _mm(z1, w["ret_w_o"][0], "nn", F32, "l1_ret_out", res=xx2, gate=m1x[2], gidx_for=gidx_for, gate_rows=mm_rows)
    k2 = _norm_mod_fwd(y1, n2[1:2], m1x[3], m1x[4], gx, "l1_norm2")
    u1 = _mm(k2, w["ffn_w_in"][1], "nn", F32, "l1_ffn_in")
    a1 = _swiglu_fwd(u1, "l1_swiglu")
    f1, y2 = _mm(a1, w["ffn_w_out"][1], "nn", F32, "l1_ffn_out", res=y1, gate=m1x[5], gidx_for=gidx_for, gate_rows=mm_rows)

    loss_tile, dy2 = _loss_fwd_bwd(y2, target.reshape(NX, D_MODEL), "loss")

    zg = jnp.zeros((1, 1, D_MODEL), F32)
    dz, dgate5_1 = _gate_bwd(dy2, f1, m1x[5], gx, B, "l1_ffn_gate_bwd")
    gw_ffn_out1 = _mm(a1, dz, "tn", F32, "l1_ffn_out_dw")
    da = _mm(dz, w["ffn_w_out"][1], "nt", F32, "l1_ffn_out_dx")
    du = _swiglu_bwd(da, u1, "l1_swiglu_bwd")
    gw_ffn_in1 = _mm(k2, du, "tn", F32, "l1_ffn_in_dw")
    dk2 = _mm(du, w["ffn_w_in"][1], "nt", F32, "l1_ffn_in_dx")
    dy1, dsh3_1, dsc4_1, dn2_1, dzo, dgate2_1 = _norm_mod_bwd(dk2, y1, n2[1:2], m1x[4], dy2, gx, B, "l1_norm2_bwd",
                                                              gated=(mo1, m1x[2]))
    gw_ret_o = _mm(z1, dzo, "tn", F32, "l1_ret_out_dw")
    dz1 = _mm(dzo, w["ret_w_o"][0], "nt", F32, "l1_ret_out_dx")
    do_r, dg_r, dgn = _gated_out_bwd(dz1, of, ob, qkvg, gn, "l1_gated_out_bwd")
    ((dq_f, dk_f, dv_f, dkc_f, dvc_f, dlg_f),
     (dq_b, dk_b, dv_b, dkc_b, dvc_b, dlg_b)) = _ret_bwd(qk, qkvg, log_g, st_f, st_b, do_r, B, S, L, "l1_ret_bwd")
    dqkvg = _ret_grad_assemble((dq_f, dq_b, dk_f, dk_b, dv_f, dv_b), (dkc_f, dkc_b, dvc_f, dvc_b), dg_r, rcos, rsin, S,
                               "l1_qkvg_grad")
    gw_ret_qkvg = _mm(g1, dqkvg, "tn", F32, "l1_qkvg_dw")
    grads_layer1 = {
        "ffn_w_in": gw_ffn_in1[None],
        "ffn_w_out": gw_ffn_out1[None],
        "ret_w_qkvg": gw_ret_qkvg[None],
        "ret_gn_g": dgn.reshape(1, 1, RET_VWIDTH),
        "ret_w_o": gw_ret_o[None],
    }
    if hooks is not None:
        m0[5] = hooks.layer1_grads(grads_layer1, m0[5])
    dg1 = _mm(dqkvg, w["ret_w_qkvg"][0], "nt", F32, "l1_qkvg_dx")
    dres1 = jnp.concatenate([dy1, jnp.zeros((NC, D_MODEL), F32)], axis=0)
    dx2, dsh0_1, dsc1_1, dn1_1, dz, dgate5_0 = _norm_mod_bwd(dg1, x2, n1[1:2], m1[1], dres1, gidx, G, "l1_norm1_bwd",
                                                             gated=(f0, m0[5]))
    dlg = jnp.stack([jnp.sum(dlg_f[:, :, 0, 0], axis=0), jnp.sum(dlg_b[:, :, 0, 0], axis=0)])
    d_decay = (dlg * jax.nn.sigmoid(-small["ret_decay_logit"].reshape(2, RET_HEADS))).reshape(1, 2, RET_HEADS)

    gw_ffn_out0 = _mm(a0, dz, "tn", F32, "l0_ffn_out_dw")
    da = _mm(dz, w["ffn_w_out"][0], "nt", F32, "l0_ffn_out_dx")
    du = _swiglu_bwd(da, u0, "l0_swiglu_bwd")
    if hooks is not None:
        m0[4] = hooks.mid_ffn0_backward(du, m0[4])
    gw_ffn_in0 = _mm(h2, du, "tn", F32, "l0_ffn_in_dw")
    if hooks is not None:
        m0[2] = hooks.ffn0_grads({"ffn_w_in": gw_ffn_in0[None], "ffn_w_out": gw_ffn_out0[None]}, m0[2])
    dh2 = _mm(du, w["ffn_w_in"][0], "nt", F32, "l0_ffn_in_dx")
    dx1, dsh3_0, dsc4_0, dn2_0, dzo, dgate2_0 = _norm_mod_bwd(dh2, x1, n2[0:1], m0[4], dx2, gidx, G, "l0_norm2_bwd",
                                                              gated=(mo0, m0[2]))
    gw_attn_o = _mm(o0, dzo, "tn", F32, "l0_attn_out_dw")
    do0 = _mm(dzo, w["attn_w_o"][0], "nt", MXU_DTYPE, "l0_attn_out_dx")
    dq_x, dk_x, dv_x, dkc1, dvc1, dsink_x = _attn_bwd(qkv_r, sink, do0, B, S, L, False, "l0_attn_x_bwd")
    dq_c, dkc2, dvc2, dsink_c = _attn_bwd(qkv_r, sink, do0, B, S, L, True, "l0_attn_c_bwd")
    dqk = jnp.concatenate([jnp.concatenate([dq_x, dk_x], axis=1), jnp.concatenate([dq_c, dkc1 + dkc2], axis=1)], axis=0)
    dvv = jnp.concatenate([dv_x, dvc1 + dvc2], axis=0)
    if hooks is not None:
        gains = hooks.after_attn_backward(dq_x, gains)
    dqkv, dgains = _attn_prep_bwd(dqk, dvv, qkv, gains, acos, asin, tidx, "l0_qk_prep_bwd")
    gw_attn_qkv = _mm(h1, dqkv, "tn", F32, "l0_qkv_dw")
    dh1 = _mm(dqkv, w["attn_w_qkv"][0], "nt", F32, "l0_qkv_dx")
    dx0, dsh0_0, dsc1_0, dn1_0 = _norm_mod_bwd(dh1, x0, n1[0:1], m0[1], dx1, gidx, G, "l0_norm1_bwd")

    dgains = jnp.sum(dgains.reshape(ATTN_QK_BLOCKS, LANES // HEAD_DIM, HEAD_DIM), axis=1)
    dsink = (dsink_x + dsink_c).reshape(N_KV_HEADS, 8, LANES)[:, :GQA_GROUP, 0].reshape(1, N_HEADS)
    grads_layer0 = {
        "ffn_w_in": gw_ffn_in0[None],
        "ffn_w_out": gw_ffn_out0[None],
        "attn_w_qkv": gw_attn_qkv[None],
        "attn_w_o": gw_attn_o[None],
    }
    grads_small = {
        "norm1_g": jnp.concatenate([dn1_0, dn1_1], axis=0),
        "norm2_g": jnp.concatenate([dn2_0, dn2_1], axis=0),
        "attn_q_norm": jnp.sum(dgains[:ATTN_Q_BLOCKS], axis=0)[None],
        "attn_k_norm": jnp.sum(dgains[ATTN_Q_BLOCKS:ATTN_QK_BLOCKS], axis=0)[None],
        "attn_sink": dsink,
        "ret_decay_logit": d_decay,
    }

    def pad_g(t):
        return jnp.concatenate([t, zg], axis=0)

    d0 = jnp.concatenate([dsh0_0, dsc1_0, dgate2_0, dsh3_0, dsc4_0, dgate5_0], axis=2)[:, 0]
    d1 = jnp.concatenate([dsh0_1, dsc1_1, pad_g(dgate2_1), pad_g(dsh3_1), pad_g(dsc4_1), pad_g(dgate5_1)],
                         axis=2)[:, 0]
    dmod_x = jnp.stack([d0[:B], d1[:B]], axis=1)
    dmod_c = jnp.stack([d0[B], d1[B]], axis=0)
    return loss_tile, dx0[:NX].reshape(B, S, D_MODEL), (grads_layer0, grads_layer1), grads_small, dmod_x, dmod_c


SMALL_NAMES = ("c_ctx", "ada_b", "norm1_g", "norm2_g", "attn_q_norm", "attn_k_norm", "attn_sink", "ret_decay_logit")
ADA_ROWS = 64


def _pack_small(d, rows):
    flat = jnp.concatenate([d[k].reshape(-1) for k in SMALL_NAMES])
    n = rows * LANES
    return jnp.pad(flat, (0, n - flat.shape[0])).reshape(rows, LANES)


def _unpack_small(packed, shapes):
    flat = packed.reshape(-1)
    out, off = {}, 0
    for k in SMALL_NAMES:
        n = math.prod(shapes[k])
        out[k] = flat[off:off + n].reshape(shapes[k])
        off += n
    return out


EARLY_WEIGHTS = ("attn_w_qkv", "attn_w_o")
LATE_WEIGHTS = tuple(k for k in BIG_WEIGHTS if k not in EARLY_WEIGHTS)

_HBM = pl.BlockSpec(memory_space=pltpu.HBM)
_SEM = pl.BlockSpec(memory_space=pltpu.SEMAPHORE)
_DATAFLOW = pltpu.SideEffectType.DATAFLOW_SIDE_EFFECTING
_PEER_FLIPS = ((0, 0, 1), (0, 1, 0), (0, 1, 1), (1, 0, 0), (1, 0, 1), (1, 1, 0), (1, 1, 1))


def _wire_shard(name, t):
    return t.reshape(1, 1, -1) if name == "ret_gn_g" else t.astype(MXU_DTYPE)


def _direct_copies(x_refs, land_refs, send_sems, recv_sems, landing):
    x, y, c = _my_coords()
    out = []
    for a in range(len(x_refs)):
        for k, (bx, by, bc) in enumerate(_PEER_FLIPS):
            peer = (_flip(x, bx), _flip(y, by), _flip(c, bc))
            slot = (4 * peer[0] + 2 * peer[1] + peer[2]) if landing else (4 * x + 2 * y + c)
            out.append(pltpu.make_async_remote_copy(
                src_ref=x_refs[a], dst_ref=land_refs[a].at[:, slot], send_sem=send_sems.at[7 * a + k],
                recv_sem=recv_sems.at[7 * a + k], device_id=peer, device_id_type=MESH))
    return out


def _gather_start(shards, name):
    n = len(shards)
    lands = [lax.empty((s.shape[0], N_DEV) + s.shape[1:], s.dtype) for s in shards]

    def body(*refs):
        send_sems, recv_sems = refs[2 * n], refs[2 * n + 1]
        x_refs, land_refs = refs[2 * n + 2:3 * n + 2], refs[3 * n + 2:4 * n + 2]
        for cp in _direct_copies(x_refs, land_refs, send_sems, recv_sems, landing=False):
            cp.start()
        refs[-1][...] = jnp.zeros_like(refs[-1])

    hbm = lambda t: pltpu.with_memory_space_constraint(t, pltpu.HBM)
    res = pl.pallas_call(
        body, name=name,
        out_shape=(pltpu.SemaphoreType.DMA((7 * n,)), pltpu.SemaphoreType.DMA((7 * n,)))
        + tuple(pltpu.HBM(t.shape, t.dtype) for t in shards + lands) + (jax.ShapeDtypeStruct((8, LANES), F32),),
        in_specs=[_HBM] * (2 * n), out_specs=(_SEM, _SEM) + (_HBM,) * (2 * n) + (pl.BlockSpec(memory_space=pltpu.VMEM),),
        input_output_aliases={i: 2 + i for i in range(2 * n)},
        compiler_params=pltpu.CompilerParams(has_side_effects=_DATAFLOW))(*[hbm(t) for t in shards + lands])
    return res[0], res[1], list(res[2:2 + n]), list(res[2 + n:2 + 2 * n]), res[-1]


def _gather_wait(send_sems, recv_sems, shards, lands, after, name):
    n = len(shards)

    def body(*refs):
        x_refs, land_refs = refs[:n], refs[n:2 * n]
        for cp in _direct_copies(x_refs, land_refs, refs[2 * n], refs[2 * n + 1], landing=True):
            cp.wait_send()
            cp.wait_recv()

    res = pl.pallas_call(
        body, name=name, out_shape=tuple(pltpu.HBM(t.shape, t.dtype) for t in shards + lands),
        in_specs=[_HBM] * (2 * n) + [_SEM, _SEM, _ANY], out_specs=(_HBM,) * (2 * n),
        input_output_aliases={i: i for i in range(2 * n)},
        compiler_params=pltpu.CompilerParams(has_side_effects=_DATAFLOW))(*shards, *lands, send_sems, recv_sems, after)
    return list(res[n:])


def _gather_big_weights(weights, names, name):
    gathered = _gather_shards([_wire_shard(k, weights[k]) for k in names], name)
    return {k: _join_shards(k, g) for k, g in zip(names, gathered)}


_SIBLING = ((0, 0, 1),)
_CHIPS = ((1, 0, 0), (0, 1, 0), (1, 1, 0))
_to_sibling = lambda ref, peer: ref.at[:, :, peer[2]]
_to_chip = lambda ref, peer: ref.at[:, 2 * peer[0] + peer[1]]
_sibling_tail = lambda s: (s.shape[0], 4) + s.shape[3:]
_chip_tail = lambda s: (s.shape[0],) + s.shape[2:]


def _rs_split(grads):
    names = list(grads)
    split = []
    for k in names:
        s = _split_shards(k, grads[k])
        split.append(s.reshape(s.shape[0], 4, 2, s.shape[2], s.shape[3]))
    return names, split


def _rs_pair_sums(names, split, from_sibling, tag):
    core = lax.axis_index("c").astype(jnp.int32).reshape(1)
    return [_pair_sum(g, s, core, MXU_DTYPE, tag + k) for k, g, s in zip(names, split, from_sibling)]


def _rs_parts(names, split, from_sibling, from_chips):
    mx_, my_, mc_ = _my_coords()
    my_chip = 2 * mx_ + my_
    parts = {}
    for k, g, s, r in zip(names, split, from_sibling, from_chips):
        own_keep = lax.dynamic_index_in_dim(lax.dynamic_index_in_dim(g, my_chip, axis=1, keepdims=False), mc_, axis=1,
                                            keepdims=False)
        parts[k] = (own_keep, lax.dynamic_index_in_dim(s, my_chip, axis=1, keepdims=False), r)
    return parts


def _reduce_scatter_in_call(grads, tag):
    names, split = _rs_split(grads)
    from_sibling = [t[0] for t in _exchange_shards(split, _SIBLING, lambda ref, peer, me_: _to_sibling(ref, peer),
                                                   _sibling_tail, tag + "sibling")]
    pair = _rs_pair_sums(names, split, from_sibling, tag + "pair_")
    from_chips = _exchange_shards(pair, _CHIPS, lambda ref, peer, me_: _to_chip(ref, peer), _chip_tail, tag + "chips")
    return _rs_parts(names, split, from_sibling, from_chips)


def _exchange_copies(in_refs, land_refs, send_sems, recv_sems, masks, src_of):
    x, y, c = _my_coords()
    nm = len(masks)
    out = []
    for a in range(len(in_refs)):
        for k, (bx, by, bc) in enumerate(masks):
            peer = (_flip(x, bx), _flip(y, by), _flip(c, bc))
            out.append(pltpu.make_async_remote_copy(
                src_ref=src_of(in_refs[a], peer), dst_ref=land_refs[a].at[k], send_sem=send_sems.at[nm * a + k],
                recv_sem=recv_sems.at[nm * a + k], device_id=peer, device_id_type=MESH))
    return out


def _exchange_start(arrs, masks, src_of, out_tail, name):
    n, nm = len(arrs), len(masks)
    lands = [lax.empty((nm,) + out_tail(s), s.dtype) for s in arrs]

    def body(*refs):
        send_sems, recv_sems = refs[2 * n], refs[2 * n + 1]
        in_refs, land_refs = refs[2 * n + 2:3 * n + 2], refs[3 * n + 2:4 * n + 2]
        for cp in _exchange_copies(in_refs, land_refs, send_sems, recv_sems, masks, src_of):
            cp.start()
        refs[-1][...] = jnp.zeros_like(refs[-1])

    hbm = lambda t: pltpu.with_memory_space_constraint(t, pltpu.HBM)
    res = pl.pallas_call(
        body, name=name,
        out_shape=(pltpu.SemaphoreType.DMA((nm * n,)), pltpu.SemaphoreType.DMA((nm * n,)))
        + tuple(pltpu.HBM(t.shape, t.dtype) for t in list(arrs) + lands) + (jax.ShapeDtypeStruct((8, LANES), F32),),
        in_specs=[_HBM] * (2 * n), out_specs=(_SEM, _SEM) + (_HBM,) * (2 * n) + (pl.BlockSpec(memory_space=pltpu.VMEM),),
        input_output_aliases={i: 2 + i for i in range(2 * n)},
        compiler_params=pltpu.CompilerParams(has_side_effects=_DATAFLOW))(*[hbm(t) for t in list(arrs) + lands])
    return (res[0], res[1], list(res[2:2 + n]), list(res[2 + n:2 + 2 * n]), masks, src_of), res[-1]


def _exchange_wait(state, after, name):
    send_sems, recv_sems, arrs, lands, masks, src_of = state
    n = len(arrs)

    def body(*refs):
        for cp in _exchange_copies(refs[:n], refs[n:2 * n], refs[2 * n], refs[2 * n + 1], masks, src_of):
            cp.wait_send()
            cp.wait_recv()

    res = pl.pallas_call(
        body, name=name, out_shape=tuple(pltpu.HBM(t.shape, t.dtype) for t in arrs + lands),
        in_specs=[_HBM] * (2 * n) + [_SEM, _SEM, _ANY], out_specs=(_HBM,) * (2 * n),
        input_output_aliases={i: i for i in range(2 * n)},
        compiler_params=pltpu.CompilerParams(has_side_effects=_DATAFLOW))(*arrs, *lands, send_sems, recv_sems, after)
    return list(res[:n]), list(res[n:])


class _SplitReduce:
    def __init__(self, tag):
        self.tag = tag

    def start(self, grads, order_through):
        self.names, split = _rs_split(grads)
        self.sibling, tok = _exchange_start(split, _SIBLING, _to_sibling, _sibling_tail, self.tag + "sibling_start")
        return order_through + tok[0, 0]

    def middle(self, after, order_through):
        self.split, lands = _exchange_wait(self.sibling, after, self.tag + "sibling_wait")
        self.from_sibling = [t[0] for t in lands]
        pair = _rs_pair_sums(self.names, self.split, self.from_sibling, self.tag + "pair_")
        self.chips, tok = _exchange_start(pair, _CHIPS, _to_chip, _chip_tail, self.tag + "chips_start")
        return order_through + tok[0, 0]

    def finish(self, after):
        _, from_chips = _exchange_wait(self.chips, after, self.tag + "chips_wait")
        return _rs_parts(self.names, self.split, self.from_sibling, from_chips)


def _adamw_big(weights, mom1, mom2, part_groups):
    big = {}
    for k in BIG_WEIGHTS:
        parts = [g[k] for g in part_groups if k in g]
        own_keep = jnp.concatenate([p[0] for p in parts], axis=0)
        own_sib = jnp.concatenate([p[1] for p in parts], axis=0)
        recv = jnp.concatenate([p[2] for p in parts], axis=1)
        L_, a_, b_ = own_keep.shape
        rows = L_ * a_
        res = _adamw(weights[k].reshape(rows, b_), mom1[k].reshape(rows, b_), mom2[k].reshape(rows, b_),
                     [own_keep.reshape(1, rows, b_), own_sib.reshape(1, rows, b_), recv.reshape(3, rows, b_)],
                     "adamw_" + k)
        big[k] = [t.reshape(weights[k].shape) for t in res]
    return big


def kernel(x, c, ctx, c_ctx, ada_w, ada_b, norm1_g, norm2_g, ffn_w_in, ffn_w_out, attn_w_qkv, attn_q_norm, attn_k_norm, attn_sink, attn_w_o, ret_w_qkvg, ret_decay_logit, ret_gn_g, ret_w_o, loss_target, m_c_ctx, m_ada_w, m_ada_b, m_norm1_g, m_norm2_g, m_ffn_w_in, m_ffn_w_out, m_attn_w_qkv, m_attn_q_norm, m_attn_k_norm, m_attn_sink, m_attn_w_o, m_ret_w_qkvg, m_ret_decay_logit, m_ret_gn_g, m_ret_w_o, v_c_ctx, v_ada_w, v_ada_b, v_norm1_g, v_norm2_g, v_ffn_w_in, v_ffn_w_out, v_attn_w_qkv, v_attn_q_norm, v_attn_k_norm, v_attn_sink, v_attn_w_o, v_ret_w_qkvg, v_ret_decay_logit, v_ret_gn_g, v_ret_w_o):
    weights = dict(c_ctx=c_ctx, ada_w=ada_w, ada_b=ada_b, norm1_g=norm1_g, norm2_g=norm2_g, ffn_w_in=ffn_w_in,
                   ffn_w_out=ffn_w_out, attn_w_qkv=attn_w_qkv, attn_q_norm=attn_q_norm, attn_k_norm=attn_k_norm,
                   attn_sink=attn_sink, attn_w_o=attn_w_o, ret_w_qkvg=ret_w_qkvg, ret_decay_logit=ret_decay_logit,
                   ret_gn_g=ret_gn_g, ret_w_o=ret_w_o)
    mom1 = dict(c_ctx=m_c_ctx, ada_w=m_ada_w, ada_b=m_ada_b, norm1_g=m_norm1_g, norm2_g=m_norm2_g, ffn_w_in=m_ffn_w_in,
                ffn_w_out=m_ffn_w_out, attn_w_qkv=m_attn_w_qkv, attn_q_norm=m_attn_q_norm, attn_k_norm=m_attn_k_norm,
                attn_sink=m_attn_sink, attn_w_o=m_attn_w_o, ret_w_qkvg=m_ret_w_qkvg, ret_decay_logit=m_ret_decay_logit,
                ret_gn_g=m_ret_gn_g, ret_w_o=m_ret_w_o)
    mom2 = dict(c_ctx=v_c_ctx, ada_w=v_ada_w, ada_b=v_ada_b, norm1_g=v_norm1_g, norm2_g=v_norm2_g, ffn_w_in=v_ffn_w_in,
                ffn_w_out=v_ffn_w_out, attn_w_qkv=v_attn_w_qkv, attn_q_norm=v_attn_q_norm, attn_k_norm=v_attn_k_norm,
                attn_sink=v_attn_sink, attn_w_o=v_attn_w_o, ret_w_qkvg=v_ret_w_qkvg, ret_decay_logit=v_ret_decay_logit,
                ret_gn_g=v_ret_gn_g, ret_w_o=v_ret_w_o)
    B = x.shape[0]
    mx_, my_, mc_ = _my_coords()
    me = 4 * mx_ + 2 * my_ + mc_
    ada_cols = ada_w.shape[2]

    w_full = _gather_big_weights(weights, EARLY_WEIGHTS, "gather_early")

    c_all = _all_gather(jax.nn.silu(c), "gather_c").reshape(N_DEV * B, D_MODEL)
    cc_act = jax.nn.silu(c_ctx)[None]
    ada_in = jnp.concatenate([c_all, cc_act, jnp.zeros((ADA_ROWS - N_DEV * B - 1, D_MODEL), F32)], axis=0)
    ada_in = ada_in.astype(MXU_DTYPE)
    ada_w2 = jnp.concatenate([ada_w[0], ada_w[1]], axis=1)
    bias = lax.dynamic_slice_in_dim(ada_b.reshape(2, N_DEV, ada_cols), me, 1, axis=1).reshape(1, 2 * ada_cols)
    mod_cols = _mm(ada_in, ada_w2, "nn", F32, "ada_fwd", bias=bias)
    mod_all = _all_gather(mod_cols, "gather_mod")
    mod_all = mod_all.reshape(N_DEV, ADA_ROWS, 2, ada_cols).transpose(1, 2, 0, 3).reshape(ADA_ROWS, 2, N_DEV * ada_cols)
    mod_x = lax.dynamic_slice_in_dim(mod_all, me * B, B, axis=0)
    mod_c = mod_all[N_DEV * B]

    order = 0.0 * (mod_c[0, 0] + w_full["attn_w_o"][0, 0, 0].astype(F32))
    late_shards = [_wire_shard(k, weights[k] + order if k == "ret_gn_g" else weights[k]) for k in LATE_WEIGHTS]
    send_sems, recv_sems, late_thru, late_lands, token = _gather_start(late_shards, "gather_late_start")
    mod_x = mod_x + token[0, 0]

    def late_weights(after):
        lands = _gather_wait(send_sems, recv_sems, late_thru, late_lands, after, "gather_late_wait")
        own = [lax.dynamic_update_index_in_dim(land, shard, me, axis=1) for land, shard in zip(lands, late_shards)]
        return {k: _join_shards(k, g) for k, g in zip(LATE_WEIGHTS, own)}

    rs_layer1, rs_ffn0 = _SplitReduce("rs1_"), _SplitReduce("rs0_")

    class Hooks:
        layer1_grads = rs_layer1.start
        mid_ffn0_backward = rs_layer1.middle
        ffn0_grads = rs_ffn0.start
        after_attn_backward = rs_ffn0.middle

    small = {k: weights[k] for k in SMALL_NAMES}
    loss_tile, grad_x, (g_layer0, _), g_small, dmod_x, dmod_c = _local_step(
        x, ctx, loss_target, mod_x, mod_c, w_full, small, late_weights, Hooks)
    parts1 = rs_layer1.finish(grad_x)
    parts0_ffn = rs_ffn0.finish(grad_x)
    loss = lax.psum(loss_tile[0, 0], ("x", "y", "c"))

    n_mod = 2 * 6 * D_MODEL
    dm_rows = jnp.concatenate([dmod_x.reshape(B, n_mod), dmod_c.reshape(1, n_mod),
                               jnp.zeros((8 - B - 1, n_mod), F32)], axis=0)
    dm_all = _all_gather(dm_rows, "gather_dmod")
    dmc_tot = _sum_rows(dm_all[:, B:B + 1].reshape(N_DEV, 1, n_mod)[:, :, :].reshape(N_DEV, n_mod // LANES, LANES),
                        "sum_dmod_c").reshape(1, n_mod)
    dmod_rows = jnp.concatenate([dm_all[:, :B].reshape(N_DEV * B, n_mod), dmc_tot,
                                 jnp.zeros((ADA_ROWS - N_DEV * B - 1, n_mod), F32)], axis=0)
    dmod_mine = lax.dynamic_slice_in_dim(dmod_rows.reshape(ADA_ROWS, 2, N_DEV, ada_cols), me, 1, axis=2)
    dmod_mine = dmod_mine.reshape(ADA_ROWS, 2 * ada_cols).astype(MXU_DTYPE)
    g_ada2 = _mm(ada_in, dmod_mine, "tn", F32, "ada_dw")
    g_ada_w = jnp.stack([g_ada2[:, :ada_cols], g_ada2[:, ada_cols:]])
    dmc_mine = jnp.concatenate([dmod_mine[N_DEV * B:N_DEV * B + 1], jnp.zeros((7, 2 * ada_cols), MXU_DTYPE)], axis=0)
    dcc_part = _mm(dmc_mine, ada_w2, "nt", F32, "ada_dc")[0:1]
    g_ada_b = _sum_rows(dmod_rows[:, None, :].reshape(ADA_ROWS, n_mod // LANES, LANES), "sum_dmod_b").reshape(2, 6 * D_MODEL)
    sg = jax.nn.sigmoid(c_ctx)
    g_small["c_ctx"] = dcc_part.reshape(D_MODEL) * (sg * (1.0 + c_ctx * (1.0 - sg)))
    g_small["ada_b"] = g_ada_b * (1.0 / N_DEV)

    shapes = {k: weights[k].shape for k in SMALL_NAMES}
    n_small = sum(math.prod(s) for s in shapes.values())
    srows = -(-(-(-n_small // LANES)) // 8) * 8
    gs_all = _all_gather(_pack_small(g_small, srows), "gather_small_grads")
    sm = _adamw(_pack_small({k: weights[k] for k in SMALL_NAMES}, srows), _pack_small({k: mom1[k] for k in SMALL_NAMES}, srows),
                _pack_small({k: mom2[k] for k in SMALL_NAMES}, srows), [gs_all], "adamw_small")
    sm = [_unpack_small(t, shapes) for t in sm]

    ada_shape = ada_w.shape
    r2 = lambda t: t.reshape(ada_shape[0] * ada_shape[1], ada_shape[2])
    ada = [t.reshape(ada_shape) for t in _adamw(r2(ada_w), r2(m_ada_w), r2(v_ada_w), [r2(g_ada_w)[None]], "adamw_ada")]

    attn_grads = {k: g_layer0[k] for k in EARLY_WEIGHTS}
    big = _adamw_big(weights, mom1, mom2, [_reduce_scatter_in_call(attn_grads, "rs_"), parts0_ffn, parts1])

    def pick(i, name):
        if name in BIG_WEIGHTS:
            return big[name][i]
        if name == "ada_w":
            return ada[i]
        return sm[i][name]

    order = ("c_ctx", "ada_w", "ada_b", "norm1_g", "norm2_g", "ffn_w_in", "ffn_w_out", "attn_w_qkv", "attn_q_norm",
             "attn_k_norm", "attn_sink", "attn_w_o", "ret_w_qkvg", "ret_decay_logit", "ret_gn_g", "ret_w_o")
    outs = [loss, grad_x]
    for i in range(4):
        outs += [pick(i, n) for n in order]
    return tuple(outs)
```

```python
import functools
import math

import jax
import jax.numpy as jnp
from jax import lax
from jax.experimental import pallas as pl
from jax.experimental.pallas import tpu as pltpu

F32 = jnp.float32
MXU_DTYPE = jnp.bfloat16

D_MODEL = 1024
HEAD_DIM = 64
N_HEADS = 16
N_KV_HEADS = 4
GQA_GROUP = 4
WINDOW = 128
ATTN_BLOCK = 128
RET_HEADS = 4
RET_QK_DIM = 256
RET_V_DIM = 512
RET_VWIDTH = 2048
RET_CHUNK = 512
D_FF = 2816
GRID_W = 64
ROPE_BASE = 10000.0
EPS = 1e-6
NEG_INF = -1e30

ADAM_LR = 0.001
ADAM_B1 = 0.9
ADAM_B2 = 0.999
ADAM_EPS = 1e-08
ADAM_WD = 0.01
ADAM_STEP = 10

N_DEV = 8
LANES = 128
ROW_TILE = 512
VMEM_LIMIT = 48 * 1024 * 1024

MESH = pl.DeviceIdType.MESH
_ANY = pl.BlockSpec(memory_space=pl.ANY)
_SMEM = pl.BlockSpec(memory_space=pltpu.SMEM)


def _params(**kw):
    return pltpu.CompilerParams(vmem_limit_bytes=VMEM_LIMIT, **kw)


def _mx(v):
    return v.astype(MXU_DTYPE)


def _dot(a, b, dims):
    return lax.dot_general(_mx(a), _mx(b), (dims, ((), ())), preferred_element_type=F32)


_NN = ((1,), (0,))
_NT = ((1,), (1,))
_TN = ((0,), (0,))


def _tile(n, cands):
    for c in cands:
        if n % c == 0:
            return c
    return n


def _big_tile(n, cap):
    if n <= cap:
        return n
    for t in range(cap - cap % LANES, 0, -LANES):
        if n % t == 0:
            return t
    return n


MM_ROWS = 1024
MM_COLS = 1408
MM_DEPTH = 2048


def _k_tile(k):
    return _big_tile(k, MM_DEPTH)


def _mm(a, b, mode, out_dtype, name, *, bias=None, res=None, gate=None, gidx_for=None, gate_rows=None):
    if mode == "nn":
        (M, K), (_, N) = a.shape, b.shape
    elif mode == "nt":
        (M, K), (N, _) = a.shape, b.shape
    else:
        (K, M), (_, N) = a.shape, b.shape
    if res is not None:
        tm, tn = gate_rows, _big_tile(N, 512)
        gidx = gidx_for(tm)
    else:
        tm = _big_tile(M, MM_COLS if mode == "tn" else MM_ROWS)
        tn = _big_tile(N, MM_COLS)
    tk = _k_tile(K)
    nk = K // tk
    dims = {"nn": _NN, "nt": _NT, "tn": _TN}[mode]
    a_spec = (pl.BlockSpec((tk, tm), lambda i, j, k: (k, i)) if mode == "tn"
              else pl.BlockSpec((tm, tk), lambda i, j, k: (i, k)))
    b_spec = (pl.BlockSpec((tn, tk), lambda i, j, k: (j, k)) if mode == "nt"
              else pl.BlockSpec((tk, tn), lambda i, j, k: (k, j)))
    o_spec = pl.BlockSpec((tm, tn), lambda i, j, k: (i, j))
    in_specs, operands = [a_spec, b_spec], [a, b]
    if bias is not None:
        in_specs.append(pl.BlockSpec((1, tn), lambda i, j, k: (0, j)))
        operands.append(bias)
    if res is not None:
        in_specs += [o_spec, pl.BlockSpec((1, 1, tn), lambda i, j, k: (gidx(i), 0, j))]
        operands += [res, gate]
        out_shape = (jax.ShapeDtypeStruct((M, N), F32), jax.ShapeDtypeStruct((M, N), F32))
        out_specs = (o_spec, o_spec)
    else:
        out_shape = jax.ShapeDtypeStruct((M, N), out_dtype)
        out_specs = o_spec

    def body(*refs):
        a_ref, b_ref = refs[0], refs[1]
        extra = refs[2:len(operands)]
        outs = refs[len(operands):]
        prod = _dot(a_ref[...], b_ref[...], dims)

        def finish(acc):
            if bias is not None:
                outs[0][...] = (acc + extra[0][...]).astype(out_dtype)
            elif res is not None:
                outs[0][...] = acc
                outs[1][...] = extra[0][...] + extra[1][0] * acc
            else:
                outs[0][...] = acc.astype(out_dtype)

        if nk == 1:
            finish(prod)
        else:
            acc_ref = outs[-1]
            outs = outs[:-1]
            k = pl.program_id(2)

            @pl.when(k == 0)
            def _():
                acc_ref[...] = prod

            @pl.when(k > 0)
            def _():
                acc_ref[...] += prod

            @pl.when(k == nk - 1)
            def _():
                finish(acc_ref[...])

    return pl.pallas_call(
        body, name=name, grid=(M // tm, N // tn, nk), in_specs=in_specs, out_specs=out_specs, out_shape=out_shape,
        scratch_shapes=[pltpu.VMEM((tm, tn), F32)] if nk > 1 else [],
        compiler_params=_params())(*operands)


def _group_index(n_x_tiles, tiles_per_example, n_examples):
    def gidx(i):
        return jnp.where(i < n_x_tiles, i // tiles_per_example, n_examples)
    return gidx


def _norm_mod_fwd(x, g, shift, scale, gidx, name):
    T, Dm = x.shape

    def body(x_ref, g_ref, sh_ref, sc_ref, h_ref):
        xv = x_ref[...]
        r = lax.rsqrt(jnp.mean(xv * xv, axis=-1, keepdims=True) + EPS)
        y = xv * r * g_ref[...]
        h_ref[...] = (y * (1.0 + sc_ref[0]) + sh_ref[0]).astype(h_ref.dtype)

    row = pl.BlockSpec((ROW_TILE, Dm), lambda i: (i, 0))
    mod = pl.BlockSpec((1, 1, Dm), lambda i: (gidx(i), 0, 0))
    return pl.pallas_call(
        body, name=name, grid=(T // ROW_TILE,),
        in_specs=[row, pl.BlockSpec((1, Dm), lambda i: (0, 0)), mod, mod],
        out_specs=row, out_shape=jax.ShapeDtypeStruct((T, Dm), MXU_DTYPE),
        compiler_params=_params())(x, g, shift, scale)


def _first_of_group(i, gidx):
    return jnp.logical_or(i == 0, gidx(i) != gidx(jnp.maximum(i - 1, 0)))


def _norm_mod_bwd(dh, x, g, scale, dres, gidx, n_groups, name, gated=None):
    T, Dm = x.shape

    def body(*refs):
        dh_ref, x_ref, g_ref, sc_ref, dres_ref = refs[:5]
        n_in = 7 if gated else 5
        dx_ref, dsh_ref, dsc_ref, dg_ref = refs[n_in:n_in + 4]
        i = pl.program_id(0)
        xv, dhv = x_ref[...], dh_ref[...]
        r = lax.rsqrt(jnp.mean(xv * xv, axis=-1, keepdims=True) + EPS)
        xn = xv * r
        y = xn * g_ref[...]

        @pl.when(_first_of_group(i, gidx))
        def _():
            dsh_ref[...] = jnp.zeros_like(dsh_ref)
            dsc_ref[...] = jnp.zeros_like(dsc_ref)

        @pl.when(i == 0)
        def _():
            dg_ref[...] = jnp.zeros_like(dg_ref)

        dsh_ref[0] += jnp.sum(dhv, axis=0, keepdims=True)
        dsc_ref[0] += jnp.sum(dhv * y, axis=0, keepdims=True)
        dy = dhv * (1.0 + sc_ref[0])
        dg_ref[...] += jnp.sum(dy * xn, axis=0, keepdims=True)
        dxn = dy * g_ref[...]
        dx = dres_ref[...] + r * (dxn - xn * jnp.mean(dxn * xn, axis=-1, keepdims=True))
        dx_ref[...] = dx
        if gated:
            f_ref, gate_ref = refs[5:7]
            dz_ref, dgate_ref = refs[n_in + 4:]

            @pl.when(_first_of_group(i, gidx))
            def _():
                dgate_ref[...] = jnp.zeros_like(dgate_ref)

            dgate_ref[0] += jnp.sum(dx * f_ref[...], axis=0, keepdims=True)
            dz_ref[...] = (dx * gate_ref[0]).astype(dz_ref.dtype)

    row = pl.BlockSpec((ROW_TILE, Dm), lambda i: (i, 0))
    mod = pl.BlockSpec((1, 1, Dm), lambda i: (gidx(i), 0, 0))
    vec = pl.BlockSpec((1, Dm), lambda i: (0, 0))
    mod_shape = jax.ShapeDtypeStruct((n_groups, 1, Dm), F32)
    in_specs, operands = [row, row, vec, mod, row], [dh, x, g, scale, dres]
    out_specs = [row, mod, mod, vec]
    out_shape = [jax.ShapeDtypeStruct((T, Dm), F32), mod_shape, mod_shape, jax.ShapeDtypeStruct((1, Dm), F32)]
    if gated:
        in_specs, operands = in_specs + [row, mod], operands + list(gated)
        out_specs, out_shape = out_specs + [row, mod], out_shape + [jax.ShapeDtypeStruct((T, Dm), MXU_DTYPE), mod_shape]
    return pl.pallas_call(
        body, name=name, grid=(T // ROW_TILE,), in_specs=in_specs, out_specs=tuple(out_specs),
        out_shape=tuple(out_shape), compiler_params=_params())(*operands)


def _gate_bwd(dy, f, gate, gidx, n_groups, name):
    T, Dm = dy.shape

    def body(dy_ref, f_ref, gate_ref, dz_ref, dgate_ref):
        i = pl.program_id(0)
        dyv = dy_ref[...]

        @pl.when(_first_of_group(i, gidx))
        def _():
            dgate_ref[...] = jnp.zeros_like(dgate_ref)

        dgate_ref[0] += jnp.sum(dyv * f_ref[...], axis=0, keepdims=True)
        dz_ref[...] = (dyv * gate_ref[0]).astype(dz_ref.dtype)

    row = pl.BlockSpec((ROW_TILE, Dm), lambda i: (i, 0))
    mod = pl.BlockSpec((1, 1, Dm), lambda i: (gidx(i), 0, 0))
    return pl.pallas_call(
        body, name=name, grid=(T // ROW_TILE,), in_specs=[row, row, mod], out_specs=(row, mod),
        out_shape=(jax.ShapeDtypeStruct((T, Dm), MXU_DTYPE), jax.ShapeDtypeStruct((n_groups, 1, Dm), F32)),
        compiler_params=_params())(dy, f, gate)


FFN_IN_ROWS = 512
FFN_IN_COLS = 1408
FFN_BWD_ROWS = 256


def _ffn_in_swiglu(h, w_in, name):
    T, Dm = h.shape
    nj = D_FF // FFN_IN_COLS

    def body(h_ref, wg_ref, wu_ref, g_ref, u_ref, a_ref):
        hv = h_ref[...]
        gate = _dot(hv, wg_ref[...], _NN)
        up = _dot(hv, wu_ref[...], _NN)
        g_ref[...] = gate
        u_ref[...] = up
        a_ref[...] = (gate * jax.nn.sigmoid(gate) * up).astype(a_ref.dtype)

    out = pl.BlockSpec((FFN_IN_ROWS, FFN_IN_COLS), lambda i, j: (i, j))
    pre = jax.ShapeDtypeStruct((T, D_FF), F32)
    return pl.pallas_call(
        body, name=name, grid=(T // FFN_IN_ROWS, nj),
        in_specs=[pl.BlockSpec((FFN_IN_ROWS, Dm), lambda i, j: (i, 0)),
                  pl.BlockSpec((Dm, FFN_IN_COLS), lambda i, j: (0, j)),
                  pl.BlockSpec((Dm, FFN_IN_COLS), lambda i, j: (0, nj + j))],
        out_specs=(out, out, out), out_shape=(pre, pre, jax.ShapeDtypeStruct((T, D_FF), MXU_DTYPE)),
        compiler_params=_params())(h, w_in, w_in)


def _ffn_out_bwd_swiglu(dz, w_out, gate, up, name):
    T, Dm = dz.shape

    def body(dz_ref, w_ref, g_ref, u_ref, du_ref):
        da = _dot(dz_ref[...], w_ref[...], _NT)
        gv, uv = g_ref[...], u_ref[...]
        sg = jax.nn.sigmoid(gv)
        du_ref[:, :D_FF] = (da * uv * (sg * (1.0 + gv * (1.0 - sg)))).astype(du_ref.dtype)
        du_ref[:, D_FF:] = (da * gv * sg).astype(du_ref.dtype)

    half = pl.BlockSpec((FFN_BWD_ROWS, D_FF), lambda i: (i, 0))
    return pl.pallas_call(
        body, name=name, grid=(T // FFN_BWD_ROWS,),
        in_specs=[pl.BlockSpec((FFN_BWD_ROWS, Dm), lambda i: (i, 0)), pl.BlockSpec((D_FF, Dm), lambda i: (0, 0)), half, half],
        out_specs=pl.BlockSpec((FFN_BWD_ROWS, 2 * D_FF), lambda i: (i, 0)),
        out_shape=jax.ShapeDtypeStruct((T, 2 * D_FF), MXU_DTYPE), compiler_params=_params())(dz, w_out, gate, up)


def _loss_fwd_bwd(y, target, name):
    T, Dm = y.shape

    def body(y_ref, t_ref, loss_ref, dy_ref):
        err = y_ref[...] - t_ref[...]

        @pl.when(pl.program_id(0) == 0)
        def _():
            loss_ref[...] = jnp.zeros_like(loss_ref)

        loss_ref[...] += 0.5 * jnp.sum(jnp.mean(err * err, axis=-1, keepdims=True))
        dy_ref[...] = err * (1.0 / Dm)

    row = pl.BlockSpec((ROW_TILE, Dm), lambda i: (i, 0))
    return pl.pallas_call(
        body, name=name, grid=(T // ROW_TILE,), in_specs=[row, row],
        out_specs=(pl.BlockSpec((8, LANES), lambda i: (0, 0)), row),
        out_shape=(jax.ShapeDtypeStruct((8, LANES), F32), jax.ShapeDtypeStruct((T, Dm), F32)),
        compiler_params=_params())(y, target)


def _rope_tables(seq, head_dim):
    axis_dim = head_dim // 2
    half = axis_dim // 2
    pos = jnp.arange(seq, dtype=jnp.int32)
    row = (pos // GRID_W).astype(F32)[:, None]
    col = (pos % GRID_W).astype(F32)[:, None]
    inv = ROPE_BASE ** (-jnp.arange(0, axis_dim, 2, dtype=F32) / axis_dim)
    lane = jnp.arange(head_dim, dtype=jnp.int32)
    within = lane % axis_dim
    ang = jnp.where((lane // axis_dim == 0)[None, :], row, col) * inv[within % half][None, :]
    cos = jnp.cos(ang)
    sin = jnp.where((within < half)[None, :], -jnp.sin(ang), jnp.sin(ang))
    cos = jnp.concatenate([cos, jnp.ones((ROW_TILE, head_dim), F32)], axis=0)
    sin = jnp.concatenate([sin, jnp.zeros((ROW_TILE, head_dim), F32)], axis=0)
    return cos, sin


def _pair_swap(v, half):
    if 2 * half == LANES:
        return pltpu.roll(v, half, axis=1)
    lane = lax.broadcasted_iota(jnp.int32, v.shape, 1)
    return jnp.where((lane % (2 * half)) < half, pltpu.roll(v, LANES - half, axis=1), pltpu.roll(v, half, axis=1))


def _head_sum(v, ones_ref):
    hi = v.astype(MXU_DTYPE)
    lo = (v - hi.astype(F32)).astype(MXU_DTYPE)
    return (jnp.dot(hi, ones_ref[...], preferred_element_type=F32)
            + jnp.dot(lo, ones_ref[...], preferred_element_type=F32))


def _head_ones():
    lane = jnp.arange(LANES)
    return (lane[:, None] // HEAD_DIM == lane[None, :] // HEAD_DIM).astype(MXU_DTYPE)


ATTN_QK_BLOCKS = (N_HEADS + N_KV_HEADS) * HEAD_DIM // LANES
ATTN_ALL_BLOCKS = (N_HEADS + 2 * N_KV_HEADS) * HEAD_DIM // LANES
ATTN_Q_BLOCKS = N_HEADS * HEAD_DIM // LANES
ATTN_SCALE = HEAD_DIM ** -0.5


def _attn_prep_fwd(qkv, gains, cos, sin, tidx, name):
    T, W = qkv.shape

    def body(x_ref, g_ref, cos_ref, sin_ref, ones_ref, o_ref):
        for cb in range(ATTN_ALL_BLOCKS):
            cols = slice(cb * LANES, (cb + 1) * LANES)
            xv = x_ref[:, cols]
            if cb < ATTN_QK_BLOCKS:
                r = lax.rsqrt(_head_sum(xv * xv, ones_ref) * (1.0 / HEAD_DIM) + EPS)
                y = xv * r * g_ref[0 if cb < ATTN_Q_BLOCKS else 1]
                xv = y * cos_ref[...] + _pair_swap(y, HEAD_DIM // 4) * sin_ref[...]
                if cb < ATTN_Q_BLOCKS:
                    xv = xv * ATTN_SCALE
            o_ref[:, cols] = xv.astype(o_ref.dtype)

    row = pl.BlockSpec((ROW_TILE, W), lambda i: (i, 0))
    tab = pl.BlockSpec((ROW_TILE, LANES), lambda i: (tidx(i), 0))
    return pl.pallas_call(
        body, name=name, grid=(T // ROW_TILE,),
        in_specs=[row, pl.BlockSpec((2, 1, LANES), lambda i: (0, 0, 0)), tab, tab,
                  pl.BlockSpec((LANES, LANES), lambda i: (0, 0))],
        out_specs=row, out_shape=jax.ShapeDtypeStruct(qkv.shape, MXU_DTYPE),
        compiler_params=_params())(qkv, gains, cos, sin, _head_ones())


def _attn_prep_bwd(dqk, dv, qkv, gains, cos, sin, tidx, name):
    T, W = qkv.shape
    qk_w = ATTN_QK_BLOCKS * LANES

    def body(dqk_ref, dv_ref, x_ref, g_ref, cos_ref, sin_ref, ones_ref, o_ref, dg_ref):
        @pl.when(pl.program_id(0) == 0)
        def _():
            dg_ref[...] = jnp.zeros_like(dg_ref)

        for cb in range(ATTN_QK_BLOCKS):
            cols = slice(cb * LANES, (cb + 1) * LANES)
            xv, d = x_ref[:, cols], dqk_ref[:, cols]
            if cb < ATTN_Q_BLOCKS:
                d = d * ATTN_SCALE
            r = lax.rsqrt(_head_sum(xv * xv, ones_ref) * (1.0 / HEAD_DIM) + EPS)
            xn = xv * r
            dy = d * cos_ref[...] + _pair_swap(d * sin_ref[...], HEAD_DIM // 4)
            dg_ref[:, cols] += jnp.sum(dy * xn, axis=0, keepdims=True)
            dxn = dy * g_ref[0 if cb < ATTN_Q_BLOCKS else 1]
            dx = r * (dxn - xn * (_head_sum(dxn * xn, ones_ref) * (1.0 / HEAD_DIM)))
            o_ref[:, cols] = dx.astype(o_ref.dtype)
        o_ref[:, qk_w:] = dv_ref[...].astype(o_ref.dtype)

    row = lambda w: pl.BlockSpec((ROW_TILE, w), lambda i: (i, 0))
    tab = pl.BlockSpec((ROW_TILE, LANES), lambda i: (tidx(i), 0))
    return pl.pallas_call(
        body, name=name, grid=(T // ROW_TILE,),
        in_specs=[row(qk_w), row(W - qk_w), row(W), pl.BlockSpec((2, 1, LANES), lambda i: (0, 0, 0)), tab, tab,
                  pl.BlockSpec((LANES, LANES), lambda i: (0, 0))],
        out_specs=(row(W), pl.BlockSpec((1, qk_w), lambda i: (0, 0))),
        out_shape=(jax.ShapeDtypeStruct(qkv.shape, MXU_DTYPE), jax.ShapeDtypeStruct((1, qk_w), F32)),
        compiler_params=_params())(dqk, dv, qkv, gains, cos, sin, _head_ones())


RET_QK_BLOCKS = 2 * RET_HEADS * RET_QK_DIM // LANES


def _ret_rope(x, cos, sin, tidx, name):
    T = x.shape[0]
    W = RET_QK_BLOCKS * LANES
    k_scale = RET_QK_DIM ** -0.5

    def body(x_ref, cos_ref, sin_ref, o_ref):
        for cb in range(RET_QK_BLOCKS):
            cols = slice(cb * LANES, (cb + 1) * LANES)
            tcols = slice((cb % 2) * LANES, (cb % 2 + 1) * LANES)
            xv = x_ref[:, cols]
            out = xv * cos_ref[:, tcols] + pltpu.roll(xv, LANES // 2, axis=1) * sin_ref[:, tcols]
            if cb >= RET_QK_BLOCKS // 2:
                out = out * k_scale
            o_ref[:, cols] = out

    row = pl.BlockSpec((ROW_TILE, W), lambda i: (i, 0))
    tab = pl.BlockSpec((ROW_TILE, RET_QK_DIM), lambda i: (tidx(i), 0))
    return pl.pallas_call(
        body, name=name, grid=(T // ROW_TILE,), in_specs=[row, tab, tab], out_specs=row,
        out_shape=jax.ShapeDtypeStruct((T, W), F32), compiler_params=_params())(x, cos, sin)


ASSEMBLE_ROWS = 256


def _ret_grad_assemble(x_parts, c_parts, dg, cos, sin, seq, name):
    NX, NC = x_parts[0].shape[0], c_parts[0].shape[0]
    T = NX + NC
    rt = ASSEMBLE_ROWS
    nxt = NX // rt
    qk_w = RET_HEADS * RET_QK_DIM
    k_scale = RET_QK_DIM ** -0.5

    def unrotate(d, cos_ref, sin_ref, scale):
        outs = []
        for cb in range(qk_w // LANES):
            cols = slice(cb * LANES, (cb + 1) * LANES)
            tcols = slice((cb % 2) * LANES, (cb % 2 + 1) * LANES)
            dv_ = d[:, cols]
            o = dv_ * cos_ref[:, tcols] + pltpu.roll(dv_ * sin_ref[:, tcols], LANES // 2, axis=1)
            outs.append(o * scale if scale != 1.0 else o)
        return outs

    def body(dqf, dqb, dkf, dkb, dvf, dvb, dg_ref, dkcf, dkcb, dvcf, dvcb, cos_ref, sin_ref, o_ref):
        i = pl.program_id(0)

        def write_k(parts):
            for cb, o in enumerate(parts):
                o_ref[:, qk_w + cb * LANES:qk_w + (cb + 1) * LANES] = o.astype(o_ref.dtype)

        @pl.when(i < nxt)
        def _():
            for cb, o in enumerate(unrotate(dqf[...] + dqb[...], cos_ref, sin_ref, 1.0)):
                o_ref[:, cb * LANES:(cb + 1) * LANES] = o.astype(o_ref.dtype)
            write_k(unrotate(dkf[...] + dkb[...], cos_ref, sin_ref, k_scale))
            o_ref[:, 2 * qk_w:2 * qk_w + RET_VWIDTH] = (dvf[...] + dvb[...]).astype(o_ref.dtype)
            o_ref[:, 2 * qk_w + RET_VWIDTH:] = dg_ref[...].astype(o_ref.dtype)

        @pl.when(i >= nxt)
        def _():
            o_ref[:, :qk_w] = jnp.zeros((rt, qk_w), o_ref.dtype)
            write_k(unrotate(dkcf[...] + dkcb[...], cos_ref, sin_ref, k_scale))
            o_ref[:, 2 * qk_w:2 * qk_w + RET_VWIDTH] = (dvcf[...] + dvcb[...]).astype(o_ref.dtype)
            o_ref[:, 2 * qk_w + RET_VWIDTH:] = jnp.zeros((rt, RET_VWIDTH), o_ref.dtype)

    xs = lambda w: pl.BlockSpec((rt, w), lambda i: (jnp.minimum(i, nxt - 1), 0))
    cs = lambda w: pl.BlockSpec((rt, w), lambda i: (jnp.maximum(i - nxt, 0), 0))
    tab = pl.BlockSpec((rt, RET_QK_DIM), lambda i: (jnp.where(i < nxt, i % (seq // rt), seq // rt), 0))
    return pl.pallas_call(
        body, name=name, grid=(T // rt,),
        in_specs=[xs(qk_w)] * 4 + [xs(RET_VWIDTH)] * 3 + [cs(qk_w)] * 2 + [cs(RET_VWIDTH)] * 2 + [tab, tab],
        out_specs=pl.BlockSpec((rt, 2 * qk_w + 2 * RET_VWIDTH), lambda i: (i, 0)),
        out_shape=jax.ShapeDtypeStruct((T, 2 * qk_w + 2 * RET_VWIDTH), MXU_DTYPE),
        compiler_params=_params())(*x_parts, dg, *c_parts, cos, sin)


def _band_bias(qb, seq):
    nb = seq // qb
    assert nb >= 2
    i = jnp.arange(GQA_GROUP * qb, dtype=jnp.int32)[:, None] % qb
    n = jnp.arange(3 * qb, dtype=jnp.int32)[None, :]
    in_window = (n >= i) & (n - i <= 2 * WINDOW)
    variants = [in_window & (n >= qb), in_window, in_window & (n < 2 * qb)]
    return jnp.stack([jnp.where(v, 0.0, NEG_INF).astype(F32) for v in variants])


GROUP_ORDER = (0, 2, 1, 3)


def _stack_halves(blk):
    return jnp.concatenate([blk[:, :LANES], blk[:, LANES:]], axis=0)


def _unstack_halves(v, rows):
    return jnp.concatenate([v[:rows], v[rows:]], axis=1)


def _align_head(pair, odd):
    lane = lax.broadcasted_iota(jnp.int32, pair.shape, 1)
    mine = jnp.where((lane >= HEAD_DIM) == odd, pair, jnp.zeros_like(pair))
    rolled = pltpu.roll(mine, HEAD_DIM, axis=1)
    return jnp.where(odd, rolled, mine), jnp.where(odd, mine, rolled)


def _scores(out_ref, q2, x_eo):
    half = q2.shape[0]
    out_ref[:half, :] = _dot(q2, x_eo[0], _NT)
    out_ref[half:, :] = _dot(q2, x_eo[1], _NT)


def _apply(p_ref, x_eo):
    half = p_ref.shape[0] // 2
    return _dot(p_ref[:half, :], x_eo[0], _NN) + _dot(p_ref[half:, :], x_eo[1], _NN)


def _kv_grad(a_ref, q2, odd):
    half = a_ref.shape[0] // 2
    even_t = _dot(q2, a_ref[:half, :], _TN)
    odd_t = _dot(q2, a_ref[half:, :], _TN)
    mine = even_t[:HEAD_DIM] + odd_t[HEAD_DIM:]
    zero = jnp.zeros_like(mine)
    placed = jnp.where(odd, jnp.concatenate([zero, mine], axis=0), jnp.concatenate([mine, zero], axis=0))
    return placed.T


ATTN_ROW_CHUNK = 32


def _softmax_chunks(s_c_ref, s_l_ref, bias_ref, sink_ref, kv_head, qb, emit):
    for r0 in range(0, GQA_GROUP * qb, ATTN_ROW_CHUNK):
        rows = slice(r0, r0 + ATTN_ROW_CHUNK)
        t = r0 // qb
        sink = jnp.full((ATTN_ROW_CHUNK, 1), sink_ref[kv_head, GROUP_ORDER[t]], F32)
        s_c = s_c_ref[rows, :]
        m = jnp.maximum(jnp.max(s_c, axis=-1, keepdims=True), sink)
        s_l = None
        if s_l_ref is not None:
            s_l = s_l_ref[rows, :] + bias_ref[0, rows, :]
            m = jnp.maximum(m, jnp.max(s_l, axis=-1, keepdims=True))
        e_c = jnp.exp(s_c - m)
        e_s = jnp.exp(sink - m)
        den = jnp.sum(e_c, axis=-1, keepdims=True) + e_s
        e_l = None
        if s_l_ref is not None:
            e_l = jnp.exp(s_l - m)
            den = den + jnp.sum(e_l, axis=-1, keepdims=True)
        inv = 1.0 / den
        emit(t, rows, e_c * inv, (None if e_l is None else e_l * inv), e_s * inv)


GROUP_W = GQA_GROUP * HEAD_DIM
K_LANE_BLOCK = N_HEADS * HEAD_DIM // LANES
V_LANE_BLOCK = K_LANE_BLOCK + N_KV_HEADS * HEAD_DIM // LANES


def _attn_specs(B, seq, ctx_len, ctx_queries):
    ctx0 = B * seq // ctx_len
    if ctx_queries:
        qb, nb = ctx_len, 1
        qrow = lambda b, j: ctx0 + b
    else:
        qb, nb = ATTN_BLOCK, seq // ATTN_BLOCK
        qrow = lambda b, j: b * nb + j
    q_spec = pl.BlockSpec((qb, GROUP_W), lambda b, k, j: (qrow(b, j), k))
    c_specs = [pl.BlockSpec((ctx_len, LANES), lambda b, k, j: (ctx0 + b, K_LANE_BLOCK + k // 2)),
               pl.BlockSpec((ctx_len, LANES), lambda b, k, j: (ctx0 + b, V_LANE_BLOCK + k // 2))]
    local = []
    if not ctx_queries:
        near = [lambda j: jnp.maximum(j - 1, 0), lambda j: j, lambda j: jnp.minimum(j + 1, nb - 1)]
        for lane0 in (K_LANE_BLOCK, V_LANE_BLOCK):
            for f in near:
                local.append(pl.BlockSpec((qb, LANES), lambda b, k, j, f=f, lane0=lane0: (b * nb + f(j), lane0 + k // 2)))
        local.append(pl.BlockSpec(
            (1, GQA_GROUP * qb, 3 * qb), lambda b, k, j: (jnp.where(j == 0, 0, jnp.where(j == nb - 1, 2, 1)), 0, 0)))
    return qb, nb, qrow, q_spec, c_specs, local


def _attn_operands(refs, has_local, kv_head):
    odd = (kv_head % 2) == 1
    n_local = 7 if has_local else 0
    q2 = _stack_halves(refs[0][...])
    kc = _align_head(refs[1 + n_local][...], odd)
    vc = _align_head(refs[2 + n_local][...], odd)
    kl = vl = bias_ref = None
    if has_local:
        kl = _align_head(jnp.concatenate([r[...] for r in refs[1:4]], axis=0), odd)
        vl = _align_head(jnp.concatenate([r[...] for r in refs[4:7]], axis=0), odd)
        bias_ref = refs[7]
    return odd, q2, kc, vc, kl, vl, bias_ref


def _score_scratch(qb, ctx_len, has_local, dtypes):
    rows = GQA_GROUP * qb
    out = []
    for dt in dtypes:
        out.append(pltpu.VMEM((rows, ctx_len), dt))
        if has_local:
            out.append(pltpu.VMEM((rows, 3 * qb), dt))
    return out


def _score_bufs(scratch, has_local):
    if has_local:
        return [(scratch[i], scratch[i + 1]) for i in range(0, len(scratch), 2)]
    return [(s, None) for s in scratch]


def _attn_fwd(qkv, sink, B, seq, ctx_len, ctx_queries, name):
    has_local = not ctx_queries
    qb, nb, _, q_spec, c_specs, local = _attn_specs(B, seq, ctx_len, ctx_queries)
    n_rows = B * (ctx_len if ctx_queries else seq)
    n_in = 1 + (7 if has_local else 0) + 3

    def body(*refs):
        sink_ref, o_ref = refs[n_in - 1], refs[n_in]
        (s_c_ref, s_l_ref), (p_c_ref, p_l_ref) = _score_bufs(refs[n_in + 1:], has_local)
        kv_head = pl.program_id(1)
        _, q2, kc, vc, kl, vl, bias_ref = _attn_operands(refs, has_local, kv_head)
        _scores(s_c_ref, q2, kc)
        if has_local:
            _scores(s_l_ref, q2, kl)

        def emit(t, rows, p_c, p_l, p_s):
            p_c_ref[rows, :] = p_c.astype(p_c_ref.dtype)
            if has_local:
                p_l_ref[rows, :] = p_l.astype(p_l_ref.dtype)

        _softmax_chunks(s_c_ref, s_l_ref, bias_ref, sink_ref, kv_head, qb, emit)
        o2 = _apply(p_c_ref, vc)
        if has_local:
            o2 = o2 + _apply(p_l_ref, vl)
        o_ref[...] = _unstack_halves(o2, qb).astype(o_ref.dtype)

    operands = [qkv] + ([qkv] * 6 + [_band_bias(qb, seq)] if has_local else []) + [qkv, qkv, sink]
    return pl.pallas_call(
        body, name=name, grid=(B, N_KV_HEADS, nb),
        in_specs=[q_spec] + local + c_specs + [_SMEM],
        out_specs=pl.BlockSpec((qb, GROUP_W), lambda b, k, j: (b * nb + j, k)),
        out_shape=jax.ShapeDtypeStruct((n_rows, N_HEADS * HEAD_DIM), MXU_DTYPE),
        scratch_shapes=_score_scratch(qb, ctx_len, has_local, (F32, MXU_DTYPE)),
        compiler_params=_params())(*operands)


def _attn_bwd(qkv, sink, do, B, seq, ctx_len, ctx_queries, name):
    has_local = not ctx_queries
    qb, nb, qrow, q_spec, c_specs, local = _attn_specs(B, seq, ctx_len, ctx_queries)
    n_rows = B * (ctx_len if ctx_queries else seq)

    n_out = 6 if has_local else 4

    def body(*refs):
        n_in = 1 + (7 if has_local else 0) + 4
        sink_ref, do_ref = refs[n_in - 2:n_in]
        outs = refs[n_in:n_in + n_out]
        (s_c_ref, s_l_ref), (dp_c_ref, dp_l_ref), (p_c_ref, p_l_ref), (ds_c_ref, ds_l_ref) = _score_bufs(
            refs[n_in + n_out:], has_local)
        dq_ref = outs[0]
        dkc_ref, dvc_ref, dsink_ref = outs[-3:]
        b, kv_head, j = pl.program_id(0), pl.program_id(1), pl.program_id(2)
        odd, q2, kc, vc, kl, vl, bias_ref = _attn_operands(refs, has_local, kv_head)
        do2 = _stack_halves(do_ref[...])
        _scores(s_c_ref, q2, kc)
        _scores(dp_c_ref, do2, vc)
        if has_local:
            _scores(s_l_ref, q2, kl)
            _scores(dp_l_ref, do2, vl)
        dsink_parts = [jnp.zeros((), F32)] * GQA_GROUP

        def emit(t, rows, p_c, p_l, p_s):
            dp_c = dp_c_ref[rows, :]
            delta = jnp.sum(p_c * dp_c, axis=-1, keepdims=True)
            if has_local:
                dp_l = dp_l_ref[rows, :]
                delta = delta + jnp.sum(p_l * dp_l, axis=-1, keepdims=True)
                p_l_ref[rows, :] = p_l.astype(p_l_ref.dtype)
                ds_l_ref[rows, :] = (p_l * (dp_l - delta)).astype(ds_l_ref.dtype)
            p_c_ref[rows, :] = p_c.astype(p_c_ref.dtype)
            ds_c_ref[rows, :] = (p_c * (dp_c - delta)).astype(ds_c_ref.dtype)
            dsink_parts[t] = dsink_parts[t] - jnp.sum(p_s * delta)

        _softmax_chunks(s_c_ref, s_l_ref, bias_ref, sink_ref, kv_head, qb, emit)
        dq2 = _apply(ds_c_ref, kc)

        @pl.when((kv_head % 2 == 0) & (j == 0))
        def _():
            dkc_ref[...] = jnp.zeros_like(dkc_ref)
            dvc_ref[...] = jnp.zeros_like(dvc_ref)
            if has_local:
                outs[1][...] = jnp.zeros_like(outs[1])
                outs[2][...] = jnp.zeros_like(outs[2])

        @pl.when((b == 0) & (kv_head == 0) & (j == 0))
        def _():
            dsink_ref[...] = jnp.zeros_like(dsink_ref)

        dkc_ref[...] += _kv_grad(ds_c_ref, q2, odd)
        dvc_ref[...] += _kv_grad(p_c_ref, do2, odd)
        if has_local:
            dq2 = dq2 + _apply(ds_l_ref, kl)
            dkl = _kv_grad(ds_l_ref, q2, odd)
            dvl = _kv_grad(p_l_ref, do2, odd)
            dk_ref, dv_ref = outs[1], outs[2]
            for t in range(3):
                def add(t=t):
                    start = pl.multiple_of((j - 1 + t) * qb, qb)
                    dk_ref[pl.ds(start, qb), :] += dkl[t * qb:(t + 1) * qb]
                    dv_ref[pl.ds(start, qb), :] += dvl[t * qb:(t + 1) * qb]
                if t == 0:
                    pl.when(j > 0)(add)
                elif t == 2:
                    pl.when(j < nb - 1)(add)
                else:
                    add()
        dq_ref[...] = _unstack_halves(dq2, qb)
        sub = lax.broadcasted_iota(jnp.int32, (8, LANES), 0)
        tile = jnp.zeros((8, LANES), F32)
        for t, gi in enumerate(GROUP_ORDER):
            tile = jnp.where(sub == gi, dsink_parts[t], tile)
        dsink_ref[pl.ds(pl.multiple_of(kv_head * 8, 8), 8), :] += tile

    kv_w = N_KV_HEADS * HEAD_DIM
    seq_spec = pl.BlockSpec((seq, LANES), lambda b, k, j: (b, k // 2))
    ctx_spec = pl.BlockSpec((ctx_len, LANES), lambda b, k, j: (b, k // 2))
    do_spec = pl.BlockSpec((qb, GROUP_W), lambda b, k, j: (qrow(b, j), k))
    operands = [qkv] + ([qkv] * 6 + [_band_bias(qb, seq)] if has_local else []) + [qkv, qkv, sink, do]
    out_specs = ([pl.BlockSpec((qb, GROUP_W), lambda b, k, j: (b * nb + j, k))] + ([seq_spec, seq_spec] if has_local else [])
                 + [ctx_spec, ctx_spec, pl.BlockSpec((32, LANES), lambda b, k, j: (0, 0))])
    out_shape = ([jax.ShapeDtypeStruct((n_rows, N_HEADS * HEAD_DIM), F32)]
                 + ([jax.ShapeDtypeStruct((B * seq, kv_w), F32)] * 2 if has_local else [])
                 + [jax.ShapeDtypeStruct((B * ctx_len, kv_w), F32)] * 2 + [jax.ShapeDtypeStruct((32, LANES), F32)])
    return pl.pallas_call(
        body, name=name, grid=(B, N_KV_HEADS, nb),
        in_specs=[q_spec] + local + c_specs + [_SMEM, do_spec],
        out_specs=tuple(out_specs), out_shape=tuple(out_shape),
        scratch_shapes=_score_scratch(qb, ctx_len, has_local, (F32, F32, MXU_DTYPE, MXU_DTYPE)),
        compiler_params=_params())(*operands)


def _ret_decays(lg, rev):
    n = lax.broadcasted_iota(jnp.int32, (RET_CHUNK, RET_CHUNK), 0).astype(F32)
    m = lax.broadcasted_iota(jnp.int32, (RET_CHUNK, RET_CHUNK), 1).astype(F32)
    pos = lax.broadcasted_iota(jnp.int32, (RET_CHUNK, 1), 0).astype(F32)
    diff = (m - n) if rev else (n - m)
    a_exp = jnp.maximum(diff, 0.0)
    intra = jnp.where(diff >= 0, jnp.exp(lg * a_exp), 0.0)
    q_exp = (RET_CHUNK - pos) if rev else (pos + 1.0)
    k_exp = pos if rev else (RET_CHUNK - 1.0 - pos)
    chunk = jnp.exp(jnp.full((1, 1), RET_CHUNK, F32) * lg)
    return intra, a_exp, jnp.exp(lg * q_exp), q_exp, jnp.exp(lg * k_exp), k_exp, chunk


def _ctx_decay(lg, ctx_len, rev):
    t = lax.broadcasted_iota(jnp.int32, (ctx_len, 1), 0).astype(F32)
    expo = t if rev else (ctx_len - 1.0 - t)
    return jnp.exp(lg * expo), expo


def _ret_specs(B, seq, ctx_len, order):
    nc = seq // RET_CHUNK
    x_blocks = B * seq // ctx_len

    def rows(b, c):
        return b * nc + order(c, nc)

    q_spec = pl.BlockSpec((RET_CHUNK, RET_QK_DIM), lambda b, h, c: (rows(b, c), h))
    k_spec = pl.BlockSpec((RET_CHUNK, RET_QK_DIM), lambda b, h, c: (rows(b, c), RET_HEADS + h))
    v_spec = pl.BlockSpec((RET_CHUNK, RET_V_DIM), lambda b, h, c: (rows(b, c), RET_HEADS + h))
    kc_spec = pl.BlockSpec((ctx_len, RET_QK_DIM), lambda b, h, c: (x_blocks + b, RET_HEADS + h))
    vc_spec = pl.BlockSpec((ctx_len, RET_V_DIM), lambda b, h, c: (x_blocks + b, RET_HEADS + h))
    st_spec = pl.BlockSpec((1, 1, 1, RET_QK_DIM, RET_V_DIM), lambda b, h, c: (b, h, order(c, nc), 0, 0))
    o_spec = pl.BlockSpec((RET_CHUNK, RET_V_DIM), lambda b, h, c: (rows(b, c), h))
    return nc, q_spec, k_spec, v_spec, kc_spec, vc_spec, st_spec, o_spec


_SCAN_UP = lambda c, nc: c
_SCAN_DOWN = lambda c, nc: nc - 1 - c


def _ret_fwd(qk, qkvg, log_g, B, seq, ctx_len, name):
    nc, qf, kf, vf, kc_spec, vc_spec, stf, of = _ret_specs(B, seq, ctx_len, _SCAN_UP)
    _, qr, kr, vr, _, _, str_, or_ = _ret_specs(B, seq, ctx_len, _SCAN_DOWN)

    def body(lg_ref, qf_ref, kf_ref, vf_ref, qr_ref, kr_ref, vr_ref, kc_ref, vc_ref,
             of_ref, stf_ref, or_ref, str_ref, state_f, state_r):
        h, c = pl.program_id(1), pl.program_id(2)
        dirs = ((False, lg_ref[0, h], qf_ref, kf_ref, vf_ref, of_ref, stf_ref, state_f),
                (True, lg_ref[1, h], qr_ref, kr_ref, vr_ref, or_ref, str_ref, state_r))

        @pl.when(c == 0)
        def _():
            for rev, lg, _, _, _, _, _, state in dirs:
                dec, _ = _ctx_decay(lg, ctx_len, rev)
                state[...] = _dot(kc_ref[...] * dec, vc_ref[...], _TN)

        for rev, lg, q_ref, k_ref, v_ref, o_ref, st_ref, state in dirs:
            intra, _, q_dec, _, k_dec, _, chunk_dec = _ret_decays(lg, rev)
            qv, kv, vv = q_ref[...], k_ref[...], v_ref[...]
            s_in = state[...]
            st_ref[0, 0, 0] = s_in
            w = _dot(qv, kv, _NT) * intra
            o_ref[...] = _dot(w, vv, _NN) + _dot(qv, s_in, _NN) * q_dec
            state[...] = s_in * chunk_dec + _dot(kv * k_dec, vv, _TN)

    o_shape = jax.ShapeDtypeStruct((B * seq, RET_VWIDTH), F32)
    st_shape = jax.ShapeDtypeStruct((B, RET_HEADS, nc, RET_QK_DIM, RET_V_DIM), F32)
    return pl.pallas_call(
        body, name=name, grid=(B, RET_HEADS, nc),
        in_specs=[_SMEM, qf, kf, vf, qr, kr, vr, kc_spec, vc_spec],
        out_specs=(of, stf, or_, str_), out_shape=(o_shape, st_shape, o_shape, st_shape),
        scratch_shapes=[pltpu.VMEM((RET_QK_DIM, RET_V_DIM), F32)] * 2,
        compiler_params=_params())(log_g, qk, qk, qkvg, qk, qk, qkvg, qk, qkvg)


def _ret_bwd_chunk(rev, lg, q_ref, k_ref, v_ref, st_ref, do_ref, dq_ref, dk_ref, dv_ref, dlg_ref, dstate):
    intra, a_exp, q_dec, q_exp, k_dec, k_exp, chunk_dec = _ret_decays(lg, rev)
    qv, kv, vv, dov = q_ref[...], k_ref[...], v_ref[...], do_ref[...]
    s_in, ds_out = st_ref[0, 0, 0], dstate[...]
    p = _dot(qv, kv, _NT)
    w = p * intra
    dw = _dot(dov, vv, _NT)
    dp = dw * intra
    do_dec = dov * q_dec
    kd = kv * k_dec
    v_ds = _dot(vv, ds_out, _NT)
    dq_ref[...] = _dot(dp, kv, _NN) + _dot(do_dec, s_in, _NT)
    dk_ref[...] = _dot(dp, qv, _TN) + v_ds * k_dec
    dv_ref[...] = _dot(w, dov, _TN) + _dot(kd, ds_out, _NN)
    q_s = _dot(qv, s_in, _NN)
    dlg = (jnp.sum(dw * w * a_exp)
           + jnp.sum(q_exp * q_dec * jnp.sum(dov * q_s, axis=-1, keepdims=True))
           + jnp.sum(k_exp * k_dec * jnp.sum(kv * v_ds, axis=-1, keepdims=True))
           + RET_CHUNK * jnp.sum(chunk_dec * (ds_out * s_in)))
    ds_in = ds_out * chunk_dec + _dot(qv, do_dec, _TN)
    dstate[...] = ds_in
    dlg_ref[...] += dlg
    return ds_in


def _ret_bwd(qk, qkvg, log_g, st_f, st_r, do, B, seq, ctx_len, name):
    nc, qf, kf, vf, kc_spec, vc_spec, stf, of = _ret_specs(B, seq, ctx_len, _SCAN_DOWN)
    _, qr, kr, vr, _, _, str_, or_ = _ret_specs(B, seq, ctx_len, _SCAN_UP)

    def body(lg_ref, qf_ref, kf_ref, vf_ref, stf_ref, dof_ref, qr_ref, kr_ref, vr_ref, str_ref, dor_ref, kc_ref, vc_ref,
             dqf, dkf, dvf, dkcf, dvcf, dlgf, dqr, dkr, dvr, dkcr, dvcr, dlgr, dstate_f, dstate_r):
        h, c = pl.program_id(1), pl.program_id(2)
        dirs = ((False, lg_ref[0, h], (qf_ref, kf_ref, vf_ref, stf_ref, dof_ref, dqf, dkf, dvf, dlgf, dstate_f), dkcf, dvcf),
                (True, lg_ref[1, h], (qr_ref, kr_ref, vr_ref, str_ref, dor_ref, dqr, dkr, dvr, dlgr, dstate_r), dkcr, dvcr))

        @pl.when(c == 0)
        def _():
            for _, _, refs, _, _ in dirs:
                refs[-1][...] = jnp.zeros_like(refs[-1])
                refs[-2][...] = jnp.zeros_like(refs[-2])

        ds_first = [_ret_bwd_chunk(rev, lg, *refs) for rev, lg, refs, _, _ in dirs]

        @pl.when(c == nc - 1)
        def _():
            for (rev, lg, refs, dkc_ref, dvc_ref), ds_in in zip(dirs, ds_first):
                dec, expo = _ctx_decay(lg, ctx_len, rev)
                kcv, vcv = kc_ref[...], vc_ref[...]
                vc_ds = _dot(vcv, ds_in, _NT)
                dkc_ref[...] = vc_ds * dec
                dvc_ref[...] = _dot(kcv * dec, ds_in, _NN)
                refs[-2][...] += jnp.sum(expo * dec * jnp.sum(kcv * vc_ds, axis=-1, keepdims=True))

    def outs(q_spec, o_spec):
        return (pl.BlockSpec((RET_CHUNK, RET_QK_DIM), q_spec.index_map),
                pl.BlockSpec((RET_CHUNK, RET_QK_DIM), q_spec.index_map), o_spec,
                pl.BlockSpec((ctx_len, RET_QK_DIM), lambda b, h, c: (b, h)),
                pl.BlockSpec((ctx_len, RET_V_DIM), lambda b, h, c: (b, h)),
                pl.BlockSpec((1, 1, 8, LANES), lambda b, h, c: (b, h, 0, 0)))

    shapes = (jax.ShapeDtypeStruct((B * seq, RET_HEADS * RET_QK_DIM), F32),
              jax.ShapeDtypeStruct((B * seq, RET_HEADS * RET_QK_DIM), F32),
              jax.ShapeDtypeStruct((B * seq, RET_VWIDTH), F32),
              jax.ShapeDtypeStruct((B * ctx_len, RET_HEADS * RET_QK_DIM), F32),
              jax.ShapeDtypeStruct((B * ctx_len, RET_VWIDTH), F32),
              jax.ShapeDtypeStruct((B, RET_HEADS, 8, LANES), F32))
    res = pl.pallas_call(
        body, name=name, grid=(B, RET_HEADS, nc),
        in_specs=[_SMEM, qf, kf, vf, stf, of, qr, kr, vr, str_, or_, kc_spec, vc_spec],
        out_specs=outs(qf, of) + outs(qr, or_), out_shape=shapes + shapes,
        scratch_shapes=[pltpu.VMEM((RET_QK_DIM, RET_V_DIM), F32)] * 2,
        compiler_params=_params())(log_g, qk, qk, qkvg, st_f, do, qk, qk, qkvg, st_r, do, qk, qkvg)
    return res[:6], res[6:]


def _gated_out_fwd(o_f, o_b, qkvg, gn_gain, name):
    T = o_f.shape[0]
    g_off = (2 * RET_HEADS * RET_QK_DIM + RET_VWIDTH) // RET_V_DIM

    def body(of_ref, ob_ref, g_ref, gain_ref, z_ref):
        o = of_ref[...] + ob_ref[...]
        mu = jnp.mean(o, axis=-1, keepdims=True)
        var = jnp.mean(jnp.square(o - mu), axis=-1, keepdims=True)
        y = (o - mu) * lax.rsqrt(var + EPS) * gain_ref[...]
        gv = g_ref[...]
        z_ref[...] = (gv * jax.nn.sigmoid(gv) * y).astype(z_ref.dtype)

    blk = pl.BlockSpec((ROW_TILE, RET_V_DIM), lambda i, h: (i, h))
    return pl.pallas_call(
        body, name=name, grid=(T // ROW_TILE, RET_HEADS),
        in_specs=[blk, blk, pl.BlockSpec((ROW_TILE, RET_V_DIM), lambda i, h: (i, g_off + h)),
                  pl.BlockSpec((1, RET_V_DIM), lambda i, h: (0, h))],
        out_specs=blk, out_shape=jax.ShapeDtypeStruct((T, RET_VWIDTH), MXU_DTYPE),
        compiler_params=_params())(o_f, o_b, qkvg, gn_gain)


def _gated_out_bwd(dz, o_f, o_b, qkvg, gn_gain, name):
    T = o_f.shape[0]
    g_off = (2 * RET_HEADS * RET_QK_DIM + RET_VWIDTH) // RET_V_DIM

    def body(dz_ref, of_ref, ob_ref, g_ref, gain_ref, do_ref, dg_ref, dgain_ref):
        o = of_ref[...] + ob_ref[...]
        mu = jnp.mean(o, axis=-1, keepdims=True)
        var = jnp.mean(jnp.square(o - mu), axis=-1, keepdims=True)
        rstd = lax.rsqrt(var + EPS)
        yhat = (o - mu) * rstd
        gv, dzv = g_ref[...], dz_ref[...]
        sg = jax.nn.sigmoid(gv)
        dg_ref[...] = (dzv * (yhat * gain_ref[...]) * (sg * (1.0 + gv * (1.0 - sg)))).astype(dg_ref.dtype)
        dy = dzv * (gv * sg)

        @pl.when(pl.program_id(1) == 0)
        def _():
            dgain_ref[...] = jnp.zeros_like(dgain_ref)

        dgain_ref[...] += jnp.sum(dy * yhat, axis=0, keepdims=True)
        dyh = dy * gain_ref[...]
        do_ref[...] = rstd * (dyh - jnp.mean(dyh, axis=-1, keepdims=True)
                              - yhat * jnp.mean(dyh * yhat, axis=-1, keepdims=True))

    blk = pl.BlockSpec((ROW_TILE, RET_V_DIM), lambda h, i: (i, h))
    vec = pl.BlockSpec((1, RET_V_DIM), lambda h, i: (0, h))
    return pl.pallas_call(
        body, name=name, grid=(RET_HEADS, T // ROW_TILE),
        in_specs=[blk, blk, blk, pl.BlockSpec((ROW_TILE, RET_V_DIM), lambda h, i: (i, g_off + h)), vec],
        out_specs=(blk, blk, vec),
        out_shape=(jax.ShapeDtypeStruct((T, RET_VWIDTH), F32), jax.ShapeDtypeStruct((T, RET_VWIDTH), MXU_DTYPE),
                   jax.ShapeDtypeStruct((1, RET_VWIDTH), F32)),
        compiler_params=_params())(dz, o_f, o_b, qkvg, gn_gain)


def _adamw(w, m, v, parts, name):
    R, C = w.shape
    tr = _tile(R, (256, 128, 64, 32, 16, 8))
    n_parts = [p.shape[0] for p in parts]

    def body(*refs):
        w_ref, m_ref, v_ref = refs[:3]
        part_refs = refs[3:3 + len(parts)]
        g_ref, d_ref, nm_ref, nv_ref = refs[3 + len(parts):]
        g = None
        for ref, n in zip(part_refs, n_parts):
            for r in range(n):
                term = ref[r].astype(F32)
                g = term if g is None else g + term
        mn = ADAM_B1 * m_ref[...] + (1.0 - ADAM_B1) * g
        vn = ADAM_B2 * v_ref[...] + (1.0 - ADAM_B2) * jnp.square(g)
        m_hat = mn / (1.0 - ADAM_B1 ** ADAM_STEP)
        v_hat = vn / (1.0 - ADAM_B2 ** ADAM_STEP)
        g_ref[...] = g
        d_ref[...] = -ADAM_LR * (m_hat / (jnp.sqrt(v_hat) + ADAM_EPS) + ADAM_WD * w_ref[...])
        nm_ref[...] = mn
        nv_ref[...] = vn

    blk = pl.BlockSpec((tr, C), lambda i: (i, 0))
    part_specs = [pl.BlockSpec((n, tr, C), lambda i: (0, i, 0)) for n in n_parts]
    shp = jax.ShapeDtypeStruct((R, C), F32)
    return pl.pallas_call(
        body, name=name, grid=(R // tr,), in_specs=[blk, blk, blk] + part_specs,
        out_specs=(blk, blk, blk, blk), out_shape=(shp, shp, shp, shp),
        compiler_params=_params())(w, m, v, *parts)


def _sum_rows(parts, name):
    n, R, C = parts.shape
    tr = _tile(R, (256, 128, 64, 32, 16, 8))

    def body(p_ref, o_ref):
        acc = p_ref[0]
        for r in range(1, n):
            acc = acc + p_ref[r]
        o_ref[...] = acc

    return pl.pallas_call(
        body, name=name, grid=(R // tr,), in_specs=[pl.BlockSpec((n, tr, C), lambda i: (0, i, 0))],
        out_specs=pl.BlockSpec((tr, C), lambda i: (i, 0)), out_shape=jax.ShapeDtypeStruct((R, C), F32),
        compiler_params=_params())(parts)


def _my_coords():
    return lax.axis_index("x"), lax.axis_index("y"), lax.axis_index("c")


def _flip(coord, bit):
    return 1 - coord if bit else coord


def _all_gather(x2d, name):
    R, C = x2d.shape

    def body(x_ref, out_ref, send_sems, recv_sems, local_sem):
        x, y, c = _my_coords()
        me, sibling = (x, y, c), (x, y, 1 - c)
        chips = [(1 - x, y), (x, 1 - y), (1 - x, 1 - y)]

        def rows(px, py, pc):
            return out_ref.at[4 * px + 2 * py + pc]

        def copy(k, block, to, src=None):
            return pltpu.make_async_remote_copy(
                src_ref=rows(*block) if src is None else src, dst_ref=rows(*block),
                send_sem=send_sems.at[k], recv_sem=recv_sems.at[k], device_id=to, device_id_type=MESH)

        mine = pltpu.make_async_copy(x_ref, rows(*me), local_sem)
        mine.start()
        first = [copy(0, me, sibling, src=x_ref)]
        first += [copy(1 + j, me, (*chip, c), src=x_ref) for j, chip in enumerate(chips)]
        for cp in first:
            cp.start()
        passed = [copy(4 + j, (*chip, c), sibling) for j, chip in enumerate(chips)]
        for j, chip in enumerate(chips):
            copy(1 + j, (*chip, c), me).wait_recv()
            passed[j].start()
        copy(0, sibling, me).wait_recv()
        for j, chip in enumerate(chips):
            copy(4 + j, (*chip, 1 - c), me).wait_recv()
        for cp in first + passed:
            cp.wait_send()
        mine.wait()

    return pl.pallas_call(
        body, name=name, out_shape=jax.ShapeDtypeStruct((N_DEV, R, C), x2d.dtype),
        in_specs=[_ANY], out_specs=_ANY,
        scratch_shapes=[pltpu.SemaphoreType.DMA((7,)), pltpu.SemaphoreType.DMA((7,)), pltpu.SemaphoreType.DMA],
    )(x2d)


BIG_WEIGHTS = {
    "ffn_w_in": (2, (2, D_MODEL, 2 * D_FF)),
    "ffn_w_out": (1, (2, D_FF, D_MODEL)),
    "attn_w_qkv": (2, (1, D_MODEL, (N_HEADS + 2 * N_KV_HEADS) * HEAD_DIM)),
    "attn_w_o": (1, (1, N_HEADS * HEAD_DIM, D_MODEL)),
    "ret_w_qkvg": (2, (1, D_MODEL, 2 * D_MODEL + 2 * RET_VWIDTH)),
    "ret_gn_g": (2, (1, 1, RET_VWIDTH)),
    "ret_w_o": (1, (1, RET_VWIDTH, D_MODEL)),
}


def _join_shards(name, stacked):
    axis, full = BIG_WEIGHTS[name]
    if axis == 2:
        stacked = stacked.transpose(0, 2, 1, 3)
    return stacked.reshape(full)


def _split_shards(name, full_arr):
    axis, (_, rows, cols) = BIG_WEIGHTS[name]
    L = full_arr.shape[0]
    if axis == 2:
        return full_arr.reshape(L, rows, N_DEV, cols // N_DEV).transpose(0, 2, 1, 3)
    return full_arr.reshape(L, N_DEV, rows // N_DEV, cols)


def _gather_shards(shards, name):
    n = len(shards)

    def body(*refs):
        x_refs, out_refs = refs[:n], refs[n:2 * n]
        send_sems, recv_sems, local_sems = refs[2 * n:]
        x, y, c = _my_coords()
        me, sibling = (x, y, c), (x, y, 1 - c)
        chips = [(1 - x, y), (x, 1 - y), (1 - x, 1 - y)]

        def rows(a, px, py, pc):
            return out_refs[a].at[:, 4 * px + 2 * py + pc]

        def copy(a, k, block, to, src=None):
            return pltpu.make_async_remote_copy(
                src_ref=rows(a, *block) if src is None else src, dst_ref=rows(a, *block),
                send_sem=send_sems.at[7 * a + k], recv_sem=recv_sems.at[7 * a + k], device_id=to, device_id_type=MESH)

        mine = [pltpu.make_async_copy(x_refs[a], rows(a, *me), local_sems.at[a]) for a in range(n)]
        for cp in mine:
            cp.start()
        first = []
        for a in range(n):
            first.append(copy(a, 0, me, sibling, src=x_refs[a]))
            first += [copy(a, 1 + j, me, (*chip, c), src=x_refs[a]) for j, chip in enumerate(chips)]
        for cp in first:
            cp.start()
        passed = []
        for j, chip in enumerate(chips):
            for a in range(n):
                copy(a, 1 + j, (*chip, c), me).wait_recv()
                fwd = copy(a, 4 + j, (*chip, c), sibling)
                fwd.start()
                passed.append(fwd)
        for a in range(n):
            copy(a, 0, sibling, me).wait_recv()
            for j, chip in enumerate(chips):
                copy(a, 4 + j, (*chip, 1 - c), me).wait_recv()
        for cp in first + passed:
            cp.wait_send()
        for cp in mine:
            cp.wait()

    return pl.pallas_call(
        body, name=name,
        out_shape=[jax.ShapeDtypeStruct((s.shape[0], N_DEV) + s.shape[1:], s.dtype) for s in shards],
        in_specs=[_ANY] * n, out_specs=[_ANY] * n,
        scratch_shapes=[pltpu.SemaphoreType.DMA((7 * n,)), pltpu.SemaphoreType.DMA((7 * n,)),
                        pltpu.SemaphoreType.DMA((n,))],
    )(*shards)


def _exchange_shards(arrs, masks, src_of, out_tail, name):
    n, nm = len(arrs), len(masks)

    def body(*refs):
        in_refs, out_refs = refs[:n], refs[n:2 * n]
        send_sems, recv_sems = refs[2 * n:]
        x, y, c = _my_coords()
        copies = []
        for a in range(n):
            for k, (bx, by, bc) in enumerate(masks):
                peer = (_flip(x, bx), _flip(y, by), _flip(c, bc))
                copies.append(pltpu.make_async_remote_copy(
                    src_ref=src_of(in_refs[a], peer, (x, y, c)), dst_ref=out_refs[a].at[k],
                    send_sem=send_sems.at[nm * a + k], recv_sem=recv_sems.at[nm * a + k],
                    device_id=peer, device_id_type=MESH))
        for cp in copies:
            cp.start()
        for cp in copies:
            cp.wait()

    return pl.pallas_call(
        body, name=name,
        out_shape=[jax.ShapeDtypeStruct((nm,) + out_tail(s), s.dtype) for s in arrs],
        in_specs=[_ANY] * n, out_specs=[_ANY] * n,
        scratch_shapes=[pltpu.SemaphoreType.DMA((nm * n,)), pltpu.SemaphoreType.DMA((nm * n,))],
    )(*arrs)


def _pair_sum(g, from_sibling, core, out_dtype, name):
    L, _, _, a, b = g.shape
    ta = a

    def body(core_ref, g_ref, s_ref, o_ref):
        o_ref[...] = (g_ref[...] + s_ref[...]).astype(out_dtype)

    blk = pl.BlockSpec((1, 1, ta, b), lambda l, q, i, core_ref: (l, q, i, 0))
    return pl.pallas_call(
        body, name=name,
        grid_spec=pltpu.PrefetchScalarGridSpec(
            num_scalar_prefetch=1, grid=(L, 4, a // ta),
            in_specs=[pl.BlockSpec((1, 1, pl.Squeezed(), ta, b), lambda l, q, i, core_ref: (l, q, core_ref[0], i, 0)), blk],
            out_specs=blk),
        out_shape=jax.ShapeDtypeStruct((L, 4, a, b), out_dtype), compiler_params=_params())(core, g, from_sibling)


def _mods(mod_x, mod_c, layer):
    both = jnp.concatenate([mod_x[:, layer], mod_c[layer][None]], axis=0)
    return [both[:, None, k * D_MODEL:(k + 1) * D_MODEL] for k in range(6)]


def _local_step(x, ctx, target, mod_x, mod_c, w, small, late_weights=None, hooks=None):
    B, S, _ = x.shape
    L = ctx.shape[1]
    NX, NC = B * S, B * L
    T = NX + NC
    tiles_per_ex = S // ROW_TILE
    nxt = NX // ROW_TILE
    gidx = _group_index(nxt, tiles_per_ex, B)
    gidx_for = lambda rows: _group_index(NX // rows, S // rows, B)
    mm_rows = _tile(S, (MM_ROWS, ROW_TILE))
    tidx = lambda i: jnp.where(i < nxt, i % tiles_per_ex, tiles_per_ex)
    G = B + 1
    x0 = jnp.concatenate([x.reshape(NX, D_MODEL), ctx.reshape(NC, D_MODEL)], axis=0)
    acos, asin = [jnp.tile(t, (1, LANES // HEAD_DIM)) for t in _rope_tables(S, HEAD_DIM)]
    rcos, rsin = _rope_tables(S, RET_QK_DIM)
    sink = small["attn_sink"].reshape(N_KV_HEADS, GQA_GROUP)
    gains = jnp.stack([jnp.tile(small["attn_q_norm"].reshape(1, HEAD_DIM), (1, LANES // HEAD_DIM)),
                       jnp.tile(small["attn_k_norm"].reshape(1, HEAD_DIM), (1, LANES // HEAD_DIM))])
    log_g = jax.nn.log_sigmoid(small["ret_decay_logit"].reshape(2, RET_HEADS))
    n1, n2 = small["norm1_g"], small["norm2_g"]

    m0 = _mods(mod_x, mod_c, 0)
    h1 = _norm_mod_fwd(x0, n1[0:1], m0[0], m0[1], gidx, "l0_norm1")
    qkv = _mm(h1, w["attn_w_qkv"][0], "nn", F32, "l0_qkv")
    qkv_r = _attn_prep_fwd(qkv, gains, acos, asin, tidx, "l0_qk_prep")
    o_x = _attn_fwd(qkv_r, sink, B, S, L, False, "l0_attn_x")
    o_c = _attn_fwd(qkv_r, sink, B, S, L, True, "l0_attn_c")
    o0 = jnp.concatenate([o_x, o_c], axis=0)
    mo0, x1 = _mm(o0, w["attn_w_o"][0], "nn", F32, "l0_attn_out", res=x0, gate=m0[2], gidx_for=gidx_for, gate_rows=mm_rows)
    h2 = _norm_mod_fwd(x1, n2[0:1], m0[3], m0[4], gidx, "l0_norm2")
    if late_weights is not None:
        w = {**w, **late_weights(x1)}
    ug0, uu0, a0 = _ffn_in_swiglu(h2, w["ffn_w_in"][0], "l0_ffn_in")
    f0, x2 = _mm(a0, w["ffn_w_out"][0], "nn", F32, "l0_ffn_out", res=x1, gate=m0[5], gidx_for=gidx_for, gate_rows=mm_rows)

    m1 = _mods(mod_x, mod_c, 1)
    g1 = _norm_mod_fwd(x2, n1[1:2], m1[0], m1[1], gidx, "l1_norm1")
    qkvg = _mm(g1, w["ret_w_qkvg"][0], "nn", F32, "l1_qkvg")
    qk = _ret_rope(qkvg, rcos, rsin, tidx, "l1_rope")
    of, st_f, ob, st_b = _ret_fwd(qk, qkvg, log_g, B, S, L, "l1_ret")
    gn = w["ret_gn_g"].reshape(1, RET_VWIDTH)
    z1 = _gated_out_fwd(of, ob, qkvg, gn, "l1_gated_out")
    xx2 = x2[:NX]
    gx = lambda i: i // tiles_per_ex
    m1x = [t[:B] for t in m1]
    mo1, y1 = _mm(z1, w["ret_w_o"][0], "nn", F32, "l1_ret_out", res=xx2, gate=m1x[2], gidx_for=gidx_for, gate_rows=mm_rows)
    k2 = _norm_mod_fwd(y1, n2[1:2], m1x[3], m1x[4], gx, "l1_norm2")
    ug1, uu1, a1 = _ffn_in_swiglu(k2, w["ffn_w_in"][1], "l1_ffn_in")
    f1, y2 = _mm(a1, w["ffn_w_out"][1], "nn", F32, "l1_ffn_out", res=y1, gate=m1x[5], gidx_for=gidx_for, gate_rows=mm_rows)

    loss_tile, dy2 = _loss_fwd_bwd(y2, target.reshape(NX, D_MODEL), "loss")

    zg = jnp.zeros((1, 1, D_MODEL), F32)
    dz, dgate5_1 = _gate_bwd(dy2, f1, m1x[5], gx, B, "l1_ffn_gate_bwd")
    gw_ffn_out1 = _mm(a1, dz, "tn", F32, "l1_ffn_out_dw")
    du = _ffn_out_bwd_swiglu(dz, w["ffn_w_out"][1], ug1, uu1, "l1_ffn_out_dx")
    gw_ffn_in1 = _mm(k2, du, "tn", F32, "l1_ffn_in_dw")
    dk2 = _mm(du, w["ffn_w_in"][1], "nt", F32, "l1_ffn_in_dx")
    dy1, dsh3_1, dsc4_1, dn2_1, dzo, dgate2_1 = _norm_mod_bwd(dk2, y1, n2[1:2], m1x[4], dy2, gx, B, "l1_norm2_bwd",
                                                              gated=(mo1, m1x[2]))
    gw_ret_o = _mm(z1, dzo, "tn", F32, "l1_ret_out_dw")
    dz1 = _mm(dzo, w["ret_w_o"][0], "nt", F32, "l1_ret_out_dx")
    do_r, dg_r, dgn = _gated_out_bwd(dz1, of, ob, qkvg, gn, "l1_gated_out_bwd")
    ((dq_f, dk_f, dv_f, dkc_f, dvc_f, dlg_f),
     (dq_b, dk_b, dv_b, dkc_b, dvc_b, dlg_b)) = _ret_bwd(qk, qkvg, log_g, st_f, st_b, do_r, B, S, L, "l1_ret_bwd")
    dqkvg = _ret_grad_assemble((dq_f, dq_b, dk_f, dk_b, dv_f, dv_b), (dkc_f, dkc_b, dvc_f, dvc_b), dg_r, rcos, rsin, S,
                               "l1_qkvg_grad")
    gw_ret_qkvg = _mm(g1, dqkvg, "tn", F32, "l1_qkvg_dw")
    grads_layer1 = {
        "ffn_w_in": gw_ffn_in1[None],
        "ffn_w_out": gw_ffn_out1[None],
        "ret_w_qkvg": gw_ret_qkvg[None],
        "ret_gn_g": dgn.reshape(1, 1, RET_VWIDTH),
        "ret_w_o": gw_ret_o[None],
    }
    if hooks is not None:
        m0[5] = hooks.layer1_grads(grads_layer1, m0[5])
    dg1 = _mm(dqkvg, w["ret_w_qkvg"][0], "nt", F32, "l1_qkvg_dx")
    dres1 = jnp.concatenate([dy1, jnp.zeros((NC, D_MODEL), F32)], axis=0)
    dx2, dsh0_1, dsc1_1, dn1_1, dz, dgate5_0 = _norm_mod_bwd(dg1, x2, n1[1:2], m1[1], dres1, gidx, G, "l1_norm1_bwd",
                                                             gated=(f0, m0[5]))
    dlg = jnp.stack([jnp.sum(dlg_f[:, :, 0, 0], axis=0), jnp.sum(dlg_b[:, :, 0, 0], axis=0)])
    d_decay = (dlg * jax.nn.sigmoid(-small["ret_decay_logit"].reshape(2, RET_HEADS))).reshape(1, 2, RET_HEADS)

    gw_ffn_out0 = _mm(a0, dz, "tn", F32, "l0_ffn_out_dw")
    du = _ffn_out_bwd_swiglu(dz, w["ffn_w_out"][0], ug0, uu0, "l0_ffn_out_dx")
    if hooks is not None:
        m0[4] = hooks.mid_ffn0_backward(du, m0[4])
    gw_ffn_in0 = _mm(h2, du, "tn", F32, "l0_ffn_in_dw")
    if hooks is not None:
        m0[2] = hooks.ffn0_grads({"ffn_w_in": gw_ffn_in0[None], "ffn_w_out": gw_ffn_out0[None]}, m0[2])
    dh2 = _mm(du, w["ffn_w_in"][0], "nt", F32, "l0_ffn_in_dx")
    dx1, dsh3_0, dsc4_0, dn2_0, dzo, dgate2_0 = _norm_mod_bwd(dh2, x1, n2[0:1], m0[4], dx2, gidx, G, "l0_norm2_bwd",
                                                              gated=(mo0, m0[2]))
    gw_attn_o = _mm(o0, dzo, "tn", F32, "l0_attn_out_dw")
    do0 = _mm(dzo, w["attn_w_o"][0], "nt", MXU_DTYPE, "l0_attn_out_dx")
    dq_x, dk_x, dv_x, dkc1, dvc1, dsink_x = _attn_bwd(qkv_r, sink, do0, B, S, L, False, "l0_attn_x_bwd")
    dq_c, dkc2, dvc2, dsink_c = _attn_bwd(qkv_r, sink, do0, B, S, L, True, "l0_attn_c_bwd")
    dqk = jnp.concatenate([jnp.concatenate([dq_x, dk_x], axis=1), jnp.concatenate([dq_c, dkc1 + dkc2], axis=1)], axis=0)
    dvv = jnp.concatenate([dv_x, dvc1 + dvc2], axis=0)
    if hooks is not None:
        gains = hooks.after_attn_backward(dq_x, gains)
    dqkv, dgains = _attn_prep_bwd(dqk, dvv, qkv, gains, acos, asin, tidx, "l0_qk_prep_bwd")
    gw_attn_qkv = _mm(h1, dqkv, "tn", F32, "l0_qkv_dw")
    dh1 = _mm(dqkv, w["attn_w_qkv"][0], "nt", F32, "l0_qkv_dx")
    dx0, dsh0_0, dsc1_0, dn1_0 = _norm_mod_bwd(dh1, x0, n1[0:1], m0[1], dx1, gidx, G, "l0_norm1_bwd")

    dgains = jnp.sum(dgains.reshape(ATTN_QK_BLOCKS, LANES // HEAD_DIM, HEAD_DIM), axis=1)
    dsink = (dsink_x + dsink_c).reshape(N_KV_HEADS, 8, LANES)[:, :GQA_GROUP, 0].reshape(1, N_HEADS)
    grads_layer0 = {
        "ffn_w_in": gw_ffn_in0[None],
        "ffn_w_out": gw_ffn_out0[None],
        "attn_w_qkv": gw_attn_qkv[None],
        "attn_w_o": gw_attn_o[None],
    }
    grads_small = {
        "norm1_g": jnp.concatenate([dn1_0, dn1_1], axis=0),
        "norm2_g": jnp.concatenate([dn2_0, dn2_1], axis=0),
        "attn_q_norm": jnp.sum(dgains[:ATTN_Q_BLOCKS], axis=0)[None],
        "attn_k_norm": jnp.sum(dgains[ATTN_Q_BLOCKS:ATTN_QK_BLOCKS], axis=0)[None],
        "attn_sink": dsink,
        "ret_decay_logit": d_decay,
    }

    def pad_g(t):
        return jnp.concatenate([t, zg], axis=0)

    d0 = jnp.concatenate([dsh0_0, dsc1_0, dgate2_0, dsh3_0, dsc4_0, dgate5_0], axis=2)[:, 0]
    d1 = jnp.concatenate([dsh0_1, dsc1_1, pad_g(dgate2_1), pad_g(dsh3_1), pad_g(dsc4_1), pad_g(dgate5_1)],
                         axis=2)[:, 0]
    dmod_x = jnp.stack([d0[:B], d1[:B]], axis=1)
    dmod_c = jnp.stack([d0[B], d1[B]], axis=0)
    return loss_tile, dx0[:NX].reshape(B, S, D_MODEL), (grads_layer0, grads_layer1), grads_small, dmod_x, dmod_c


SMALL_NAMES = ("c_ctx", "ada_b", "norm1_g", "norm2_g", "attn_q_norm", "attn_k_norm", "attn_sink", "ret_decay_logit")
ADA_ROWS = 64


def _pack_small(d, rows):
    flat = jnp.concatenate([d[k].reshape(-1) for k in SMALL_NAMES])
    n = rows * LANES
    return jnp.pad(flat, (0, n - flat.shape[0])).reshape(rows, LANES)


def _unpack_small(packed, shapes):
    flat = packed.reshape(-1)
    out, off = {}, 0
    for k in SMALL_NAMES:
        n = math.prod(shapes[k])
        out[k] = flat[off:off + n].reshape(shapes[k])
        off += n
    return out


EARLY_WEIGHTS = ("attn_w_qkv", "attn_w_o")
LATE_WEIGHTS = tuple(k for k in BIG_WEIGHTS if k not in EARLY_WEIGHTS)

_HBM = pl.BlockSpec(memory_space=pltpu.HBM)
_SEM = pl.BlockSpec(memory_space=pltpu.SEMAPHORE)
_DATAFLOW = pltpu.SideEffectType.DATAFLOW_SIDE_EFFECTING
_PEER_FLIPS = ((0, 0, 1), (0, 1, 0), (0, 1, 1), (1, 0, 0), (1, 0, 1), (1, 1, 0), (1, 1, 1))


def _wire_shard(name, t):
    return t.reshape(1, 1, -1) if name == "ret_gn_g" else t.astype(MXU_DTYPE)


def _direct_copies(x_refs, land_refs, send_sems, recv_sems, landing):
    x, y, c = _my_coords()
    out = []
    for a in range(len(x_refs)):
        for k, (bx, by, bc) in enumerate(_PEER_FLIPS):
            peer = (_flip(x, bx), _flip(y, by), _flip(c, bc))
            slot = (4 * peer[0] + 2 * peer[1] + peer[2]) if landing else (4 * x + 2 * y + c)
            out.append(pltpu.make_async_remote_copy(
                src_ref=x_refs[a], dst_ref=land_refs[a].at[:, slot], send_sem=send_sems.at[7 * a + k],
                recv_sem=recv_sems.at[7 * a + k], device_id=peer, device_id_type=MESH))
    return out


def _gather_start(shards, name):
    n = len(shards)
    lands = [lax.empty((s.shape[0], N_DEV) + s.shape[1:], s.dtype) for s in shards]

    def body(*refs):
        send_sems, recv_sems = refs[2 * n], refs[2 * n + 1]
        x_refs, land_refs = refs[2 * n + 2:3 * n + 2], refs[3 * n + 2:4 * n + 2]
        for cp in _direct_copies(x_refs, land_refs, send_sems, recv_sems, landing=False):
            cp.start()
        refs[-1][...] = jnp.zeros_like(refs[-1])

    hbm = lambda t: pltpu.with_memory_space_constraint(t, pltpu.HBM)
    res = pl.pallas_call(
        body, name=name,
        out_shape=(pltpu.SemaphoreType.DMA((7 * n,)), pltpu.SemaphoreType.DMA((7 * n,)))
        + tuple(pltpu.HBM(t.shape, t.dtype) for t in shards + lands) + (jax.ShapeDtypeStruct((8, LANES), F32),),
        in_specs=[_HBM] * (2 * n), out_specs=(_SEM, _SEM) + (_HBM,) * (2 * n) + (pl.BlockSpec(memory_space=pltpu.VMEM),),
        input_output_aliases={i: 2 + i for i in range(2 * n)},
        compiler_params=pltpu.CompilerParams(has_side_effects=_DATAFLOW))(*[hbm(t) for t in shards + lands])
    return res[0], res[1], list(res[2:2 + n]), list(res[2 + n:2 + 2 * n]), res[-1]


def _gather_wait(send_sems, recv_sems, shards, lands, after, name):
    n = len(shards)

    def body(*refs):
        x_refs, land_refs = refs[:n], refs[n:2 * n]
        for cp in _direct_copies(x_refs, land_refs, refs[2 * n], refs[2 * n + 1], landing=True):
            cp.wait_send()
            cp.wait_recv()

    res = pl.pallas_call(
        body, name=name, out_shape=tuple(pltpu.HBM(t.shape, t.dtype) for t in shards + lands),
        in_specs=[_HBM] * (2 * n) + [_SEM, _SEM, _ANY], out_specs=(_HBM,) * (2 * n),
        input_output_aliases={i: i for i in range(2 * n)},
        compiler_params=pltpu.CompilerParams(has_side_effects=_DATAFLOW))(*shards, *lands, send_sems, recv_sems, after)
    return list(res[n:])


def _gather_big_weights(weights, names, name):
    gathered = _gather_shards([_wire_shard(k, weights[k]) for k in names], name)
    return {k: _join_shards(k, g) for k, g in zip(names, gathered)}


_SIBLING = ((0, 0, 1),)
_CHIPS = ((1, 0, 0), (0, 1, 0), (1, 1, 0))
_to_sibling = lambda ref, peer: ref.at[:, :, peer[2]]
_to_chip = lambda ref, peer: ref.at[:, 2 * peer[0] + peer[1]]
_sibling_tail = lambda s: (s.shape[0], 4) + s.shape[3:]
_chip_tail = lambda s: (s.shape[0],) + s.shape[2:]


def _rs_split(grads):
    names = list(grads)
    split = []
    for k in names:
        s = _split_shards(k, grads[k])
        split.append(s.reshape(s.shape[0], 4, 2, s.shape[2], s.shape[3]))
    return names, split


def _rs_pair_sums(names, split, from_sibling, tag):
    core = lax.axis_index("c").astype(jnp.int32).reshape(1)
    return [_pair_sum(g, s, core, MXU_DTYPE, tag + k) for k, g, s in zip(names, split, from_sibling)]


def _rs_parts(names, split, from_sibling, from_chips):
    mx_, my_, mc_ = _my_coords()
    my_chip = 2 * mx_ + my_
    parts = {}
    for k, g, s, r in zip(names, split, from_sibling, from_chips):
        own_keep = lax.dynamic_index_in_dim(lax.dynamic_index_in_dim(g, my_chip, axis=1, keepdims=False), mc_, axis=1,
                                            keepdims=False)
        parts[k] = (own_keep, lax.dynamic_index_in_dim(s, my_chip, axis=1, keepdims=False), r)
    return parts


def _reduce_scatter_in_call(grads, tag):
    names, split = _rs_split(grads)
    from_sibling = [t[0] for t in _exchange_shards(split, _SIBLING, lambda ref, peer, me_: _to_sibling(ref, peer),
                                                   _sibling_tail, tag + "sibling")]
    pair = _rs_pair_sums(names, split, from_sibling, tag + "pair_")
    from_chips = _exchange_shards(pair, _CHIPS, lambda ref, peer, me_: _to_chip(ref, peer), _chip_tail, tag + "chips")
    return _rs_parts(names, split, from_sibling, from_chips)


def _exchange_copies(in_refs, land_refs, send_sems, recv_sems, masks, src_of):
    x, y, c = _my_coords()
    nm = len(masks)
    out = []
    for a in range(len(in_refs)):
        for k, (bx, by, bc) in enumerate(masks):
            peer = (_flip(x, bx), _flip(y, by), _flip(c, bc))
            out.append(pltpu.make_async_remote_copy(
                src_ref=src_of(in_refs[a], peer), dst_ref=land_refs[a].at[k], send_sem=send_sems.at[nm * a + k],
                recv_sem=recv_sems.at[nm * a + k], device_id=peer, device_id_type=MESH))
    return out


def _exchange_start(arrs, masks, src_of, out_tail, name):
    n, nm = len(arrs), len(masks)
    lands = [lax.empty((nm,) + out_tail(s), s.dtype) for s in arrs]

    def body(*refs):
        send_sems, recv_sems = refs[2 * n], refs[2 * n + 1]
        in_refs, land_refs = refs[2 * n + 2:3 * n + 2], refs[3 * n + 2:4 * n + 2]
        for cp in _exchange_copies(in_refs, land_refs, send_sems, recv_sems, masks, src_of):
            cp.start()
        refs[-1][...] = jnp.zeros_like(refs[-1])

    hbm = lambda t: pltpu.with_memory_space_constraint(t, pltpu.HBM)
    res = pl.pallas_call(
        body, name=name,
        out_shape=(pltpu.SemaphoreType.DMA((nm * n,)), pltpu.SemaphoreType.DMA((nm * n,)))
        + tuple(pltpu.HBM(t.shape, t.dtype) for t in list(arrs) + lands) + (jax.ShapeDtypeStruct((8, LANES), F32),),
        in_specs=[_HBM] * (2 * n), out_specs=(_SEM, _SEM) + (_HBM,) * (2 * n) + (pl.BlockSpec(memory_space=pltpu.VMEM),),
        input_output_aliases={i: 2 + i for i in range(2 * n)},
        compiler_params=pltpu.CompilerParams(has_side_effects=_DATAFLOW))(*[hbm(t) for t in list(arrs) + lands])
    return (res[0], res[1], list(res[2:2 + n]), list(res[2 + n:2 + 2 * n]), masks, src_of), res[-1]


def _exchange_wait(state, after, name):
    send_sems, recv_sems, arrs, lands, masks, src_of = state
    n = len(arrs)

    def body(*refs):
        for cp in _exchange_copies(refs[:n], refs[n:2 * n], refs[2 * n], refs[2 * n + 1], masks, src_of):
            cp.wait_send()
            cp.wait_recv()

    res = pl.pallas_call(
        body, name=name, out_shape=tuple(pltpu.HBM(t.shape, t.dtype) for t in arrs + lands),
        in_specs=[_HBM] * (2 * n) + [_SEM, _SEM, _ANY], out_specs=(_HBM,) * (2 * n),
        input_output_aliases={i: i for i in range(2 * n)},
        compiler_params=pltpu.CompilerParams(has_side_effects=_DATAFLOW))(*arrs, *lands, send_sems, recv_sems, after)
    return list(res[:n]), list(res[n:])


class _SplitReduce:
    def __init__(self, tag):
        self.tag = tag

    def start(self, grads, order_through):
        self.names, split = _rs_split(grads)
        self.sibling, tok = _exchange_start(split, _SIBLING, _to_sibling, _sibling_tail, self.tag + "sibling_start")
        return order_through + tok[0, 0]

    def middle(self, after, order_through):
        self.split, lands = _exchange_wait(self.sibling, after, self.tag + "sibling_wait")
        self.from_sibling = [t[0] for t in lands]
        pair = _rs_pair_sums(self.names, self.split, self.from_sibling, self.tag + "pair_")
        self.chips, tok = _exchange_start(pair, _CHIPS, _to_chip, _chip_tail, self.tag + "chips_start")
        return order_through + tok[0, 0]

    def finish(self, after):
        _, from_chips = _exchange_wait(self.chips, after, self.tag + "chips_wait")
        return _rs_parts(self.names, self.split, self.from_sibling, from_chips)


def _adamw_big(weights, mom1, mom2, part_groups):
    big = {}
    for k in BIG_WEIGHTS:
        parts = [g[k] for g in part_groups if k in g]
        own_keep = jnp.concatenate([p[0] for p in parts], axis=0)
        own_sib = jnp.concatenate([p[1] for p in parts], axis=0)
        recv = jnp.concatenate([p[2] for p in parts], axis=1)
        L_, a_, b_ = own_keep.shape
        rows = L_ * a_
        res = _adamw(weights[k].reshape(rows, b_), mom1[k].reshape(rows, b_), mom2[k].reshape(rows, b_),
                     [own_keep.reshape(1, rows, b_), own_sib.reshape(1, rows, b_), recv.reshape(3, rows, b_)],
                     "adamw_" + k)
        big[k] = [t.reshape(weights[k].shape) for t in res]
    return big


def kernel(x, c, ctx, c_ctx, ada_w, ada_b, norm1_g, norm2_g, ffn_w_in, ffn_w_out, attn_w_qkv, attn_q_norm, attn_k_norm, attn_sink, attn_w_o, ret_w_qkvg, ret_decay_logit, ret_gn_g, ret_w_o, loss_target, m_c_ctx, m_ada_w, m_ada_b, m_norm1_g, m_norm2_g, m_ffn_w_in, m_ffn_w_out, m_attn_w_qkv, m_attn_q_norm, m_attn_k_norm, m_attn_sink, m_attn_w_o, m_ret_w_qkvg, m_ret_decay_logit, m_ret_gn_g, m_ret_w_o, v_c_ctx, v_ada_w, v_ada_b, v_norm1_g, v_norm2_g, v_ffn_w_in, v_ffn_w_out, v_attn_w_qkv, v_attn_q_norm, v_attn_k_norm, v_attn_sink, v_attn_w_o, v_ret_w_qkvg, v_ret_decay_logit, v_ret_gn_g, v_ret_w_o):
    weights = dict(c_ctx=c_ctx, ada_w=ada_w, ada_b=ada_b, norm1_g=norm1_g, norm2_g=norm2_g, ffn_w_in=ffn_w_in,
                   ffn_w_out=ffn_w_out, attn_w_qkv=attn_w_qkv, attn_q_norm=attn_q_norm, attn_k_norm=attn_k_norm,
                   attn_sink=attn_sink, attn_w_o=attn_w_o, ret_w_qkvg=ret_w_qkvg, ret_decay_logit=ret_decay_logit,
                   ret_gn_g=ret_gn_g, ret_w_o=ret_w_o)
    mom1 = dict(c_ctx=m_c_ctx, ada_w=m_ada_w, ada_b=m_ada_b, norm1_g=m_norm1_g, norm2_g=m_norm2_g, ffn_w_in=m_ffn_w_in,
                ffn_w_out=m_ffn_w_out, attn_w_qkv=m_attn_w_qkv, attn_q_norm=m_attn_q_norm, attn_k_norm=m_attn_k_norm,
                attn_sink=m_attn_sink, attn_w_o=m_attn_w_o, ret_w_qkvg=m_ret_w_qkvg, ret_decay_logit=m_ret_decay_logit,
                ret_gn_g=m_ret_gn_g, ret_w_o=m_ret_w_o)
    mom2 = dict(c_ctx=v_c_ctx, ada_w=v_ada_w, ada_b=v_ada_b, norm1_g=v_norm1_g, norm2_g=v_norm2_g, ffn_w_in=v_ffn_w_in,
                ffn_w_out=v_ffn_w_out, attn_w_qkv=v_attn_w_qkv, attn_q_norm=v_attn_q_norm, attn_k_norm=v_attn_k_norm,
                attn_sink=v_attn_sink, attn_w_o=v_attn_w_o, ret_w_qkvg=v_ret_w_qkvg, ret_decay_logit=v_ret_decay_logit,
                ret_gn_g=v_ret_gn_g, ret_w_o=v_ret_w_o)
    B = x.shape[0]
    mx_, my_, mc_ = _my_coords()
    me = 4 * mx_ + 2 * my_ + mc_
    ada_cols = ada_w.shape[2]

    w_full = _gather_big_weights(weights, EARLY_WEIGHTS, "gather_early")

    c_all = _all_gather(jax.nn.silu(c), "gather_c").reshape(N_DEV * B, D_MODEL)
    cc_act = jax.nn.silu(c_ctx)[None]
    ada_in = jnp.concatenate([c_all, cc_act, jnp.zeros((ADA_ROWS - N_DEV * B - 1, D_MODEL), F32)], axis=0)
    ada_in = ada_in.astype(MXU_DTYPE)
    ada_w2 = jnp.concatenate([ada_w[0], ada_w[1]], axis=1)
    bias = lax.dynamic_slice_in_dim(ada_b.reshape(2, N_DEV, ada_cols), me, 1, axis=1).reshape(1, 2 * ada_cols)
    mod_cols = _mm(ada_in, ada_w2, "nn", F32, "ada_fwd", bias=bias)
    mod_all = _all_gather(mod_cols, "gather_mod")
    mod_all = mod_all.reshape(N_DEV, ADA_ROWS, 2, ada_cols).transpose(1, 2, 0, 3).reshape(ADA_ROWS, 2, N_DEV * ada_cols)
    mod_x = lax.dynamic_slice_in_dim(mod_all, me * B, B, axis=0)
    mod_c = mod_all[N_DEV * B]

    order = 0.0 * (mod_c[0, 0] + w_full["attn_w_o"][0, 0, 0].astype(F32))
    late_shards = [_wire_shard(k, weights[k] + order if k == "ret_gn_g" else weights[k]) for k in LATE_WEIGHTS]
    send_sems, recv_sems, late_thru, late_lands, token = _gather_start(late_shards, "gather_late_start")
    mod_x = mod_x + token[0, 0]

    def late_weights(after):
        lands = _gather_wait(send_sems, recv_sems, late_thru, late_lands, after, "gather_late_wait")
        own = [lax.dynamic_update_index_in_dim(land, shard, me, axis=1) for land, shard in zip(lands, late_shards)]
        return {k: _join_shards(k, g) for k, g in zip(LATE_WEIGHTS, own)}

    rs_layer1, rs_ffn0 = _SplitReduce("rs1_"), _SplitReduce("rs0_")

    class Hooks:
        layer1_grads = rs_layer1.start
        mid_ffn0_backward = rs_layer1.middle
        ffn0_grads = rs_ffn0.start
        after_attn_backward = rs_ffn0.middle

    small = {k: weights[k] for k in SMALL_NAMES}
    loss_tile, grad_x, (g_layer0, _), g_small, dmod_x, dmod_c = _local_step(
        x, ctx, loss_target, mod_x, mod_c, w_full, small, late_weights, Hooks)
    parts1 = rs_layer1.finish(grad_x)
    parts0_ffn = rs_ffn0.finish(grad_x)
    loss = lax.psum(loss_tile[0, 0], ("x", "y", "c"))

    n_mod = 2 * 6 * D_MODEL
    dm_rows = jnp.concatenate([dmod_x.reshape(B, n_mod), dmod_c.reshape(1, n_mod),
                               jnp.zeros((8 - B - 1, n_mod), F32)], axis=0)
    dm_all = _all_gather(dm_rows, "gather_dmod")
    dmc_tot = _sum_rows(dm_all[:, B:B + 1].reshape(N_DEV, 1, n_mod)[:, :, :].reshape(N_DEV, n_mod // LANES, LANES),
                        "sum_dmod_c").reshape(1, n_mod)
    dmod_rows = jnp.concatenate([dm_all[:, :B].reshape(N_DEV * B, n_mod), dmc_tot,
                                 jnp.zeros((ADA_ROWS - N_DEV * B - 1, n_mod), F32)], axis=0)
    dmod_mine = lax.dynamic_slice_in_dim(dmod_rows.reshape(ADA_ROWS, 2, N_DEV, ada_cols), me, 1, axis=2)
    dmod_mine = dmod_mine.reshape(ADA_ROWS, 2 * ada_cols).astype(MXU_DTYPE)
    g_ada2 = _mm(ada_in, dmod_mine, "tn", F32, "ada_dw")
    g_ada_w = jnp.stack([g_ada2[:, :ada_cols], g_ada2[:, ada_cols:]])
    dmc_mine = jnp.concatenate([dmod_mine[N_DEV * B:N_DEV * B + 1], jnp.zeros((7, 2 * ada_cols), MXU_DTYPE)], axis=0)
    dcc_part = _mm(dmc_mine, ada_w2, "nt", F32, "ada_dc")[0:1]
    g_ada_b = _sum_rows(dmod_rows[:, None, :].reshape(ADA_ROWS, n_mod // LANES, LANES), "sum_dmod_b").reshape(2, 6 * D_MODEL)
    sg = jax.nn.sigmoid(c_ctx)
    g_small["c_ctx"] = dcc_part.reshape(D_MODEL) * (sg * (1.0 + c_ctx * (1.0 - sg)))
    g_small["ada_b"] = g_ada_b * (1.0 / N_DEV)

    shapes = {k: weights[k].shape for k in SMALL_NAMES}
    n_small = sum(math.prod(s) for s in shapes.values())
    srows = -(-(-(-n_small // LANES)) // 8) * 8
    gs_all = _all_gather(_pack_small(g_small, srows), "gather_small_grads")
    sm = _adamw(_pack_small({k: weights[k] for k in SMALL_NAMES}, srows), _pack_small({k: mom1[k] for k in SMALL_NAMES}, srows),
                _pack_small({k: mom2[k] for k in SMALL_NAMES}, srows), [gs_all], "adamw_small")
    sm = [_unpack_small(t, shapes) for t in sm]

    ada_shape = ada_w.shape
    r2 = lambda t: t.reshape(ada_shape[0] * ada_shape[1], ada_shape[2])
    ada = [t.reshape(ada_shape) for t in _adamw(r2(ada_w), r2(m_ada_w), r2(v_ada_w), [r2(g_ada_w)[None]], "adamw_ada")]

    attn_grads = {k: g_layer0[k] for k in EARLY_WEIGHTS}
    big = _adamw_big(weights, mom1, mom2, [_reduce_scatter_in_call(attn_grads, "rs_"), parts0_ffn, parts1])

    def pick(i, name):
        if name in BIG_WEIGHTS:
            return big[name][i]
        if name == "ada_w":
            return ada[i]
        return sm[i][name]

    order = ("c_ctx", "ada_w", "ada_b", "norm1_g", "norm2_g", "ffn_w_in", "ffn_w_out", "attn_w_qkv", "attn_q_norm",
             "attn_k_norm", "attn_sink", "attn_w_o", "ret_w_qkvg", "ret_decay_logit", "ret_gn_g", "ret_w_o")
    outs = [loss, grad_x]
    for i in range(4):
        outs += [pick(i, n) for n in order]
    return tuple(outs)
```

```python
import functools
import math

import jax
import jax.numpy as jnp
from jax import lax
from jax.experimental import pallas as pl
from jax.experimental.pallas import tpu as pltpu

F32 = jnp.float32
MXU_DTYPE = jnp.bfloat16

D_MODEL = 1024
HEAD_DIM = 64
N_HEADS = 16
N_KV_HEADS = 4
GQA_GROUP = 4
WINDOW = 128
ATTN_BLOCK = 128
RET_HEADS = 4
RET_QK_DIM = 256
RET_V_DIM = 512
RET_VWIDTH = 2048
RET_CHUNK = 512
D_FF = 2816
GRID_W = 64
ROPE_BASE = 10000.0
EPS = 1e-6
NEG_INF = -1e30

ADAM_LR = 0.001
ADAM_B1 = 0.9
ADAM_B2 = 0.999
ADAM_EPS = 1e-08
ADAM_WD = 0.01
ADAM_STEP = 10

N_DEV = 8
LANES = 128
ROW_TILE = 512
VMEM_LIMIT = 48 * 1024 * 1024

MESH = pl.DeviceIdType.MESH
_ANY = pl.BlockSpec(memory_space=pl.ANY)
_SMEM = pl.BlockSpec(memory_space=pltpu.SMEM)


def _params(**kw):
    return pltpu.CompilerParams(vmem_limit_bytes=VMEM_LIMIT, **kw)


def _mx(v):
    return v.astype(MXU_DTYPE)


def _dot(a, b, dims):
    return lax.dot_general(_mx(a), _mx(b), (dims, ((), ())), preferred_element_type=F32)


_NN = ((1,), (0,))
_NT = ((1,), (1,))
_TN = ((0,), (0,))


def _tile(n, cands):
    for c in cands:
        if n % c == 0:
            return c
    return n


def _big_tile(n, cap):
    if n <= cap:
        return n
    for t in range(cap - cap % LANES, 0, -LANES):
        if n % t == 0:
            return t
    return n


MM_ROWS = 1024
MM_COLS = 1408
MM_DEPTH = 2048


def _k_tile(k):
    return _big_tile(k, MM_DEPTH)


def _mm(a, b, mode, out_dtype, name, *, bias=None, res=None, gate=None, gidx_for=None, gate_rows=None):
    if mode == "nn":
        (M, K), (_, N) = a.shape, b.shape
    elif mode == "nt":
        (M, K), (N, _) = a.shape, b.shape
    else:
        (K, M), (_, N) = a.shape, b.shape
    if res is not None:
        tm, tn = gate_rows, _big_tile(N, 512)
        gidx = gidx_for(tm)
    else:
        tm = _big_tile(M, MM_COLS if mode == "tn" else MM_ROWS)
        tn = _big_tile(N, MM_COLS)
    tk = _k_tile(K)
    nk = K // tk
    dims = {"nn": _NN, "nt": _NT, "tn": _TN}[mode]
    a_spec = (pl.BlockSpec((tk, tm), lambda i, j, k: (k, i)) if mode == "tn"
              else pl.BlockSpec((tm, tk), lambda i, j, k: (i, k)))
    b_spec = (pl.BlockSpec((tn, tk), lambda i, j, k: (j, k)) if mode == "nt"
              else pl.BlockSpec((tk, tn), lambda i, j, k: (k, j)))
    o_spec = pl.BlockSpec((tm, tn), lambda i, j, k: (i, j))
    in_specs, operands = [a_spec, b_spec], [a, b]
    if bias is not None:
        in_specs.append(pl.BlockSpec((1, tn), lambda i, j, k: (0, j)))
        operands.append(bias)
    if res is not None:
        in_specs += [o_spec, pl.BlockSpec((1, 1, tn), lambda i, j, k: (gidx(i), 0, j))]
        operands += [res, gate]
        out_shape = (jax.ShapeDtypeStruct((M, N), F32), jax.ShapeDtypeStruct((M, N), F32))
        out_specs = (o_spec, o_spec)
    else:
        out_shape = jax.ShapeDtypeStruct((M, N), out_dtype)
        out_specs = o_spec

    def body(*refs):
        a_ref, b_ref = refs[0], refs[1]
        extra = refs[2:len(operands)]
        outs = refs[len(operands):]
        prod = _dot(a_ref[...], b_ref[...], dims)

        def finish(acc):
            if bias is not None:
                outs[0][...] = (acc + extra[0][...]).astype(out_dtype)
            elif res is not None:
                outs[0][...] = acc
                outs[1][...] = extra[0][...] + extra[1][0] * acc
            else:
                outs[0][...] = acc.astype(out_dtype)

        if nk == 1:
            finish(prod)
        else:
            acc_ref = outs[-1]
            outs = outs[:-1]
            k = pl.program_id(2)

            @pl.when(k == 0)
            def _():
                acc_ref[...] = prod

            @pl.when(k > 0)
            def _():
                acc_ref[...] += prod

            @pl.when(k == nk - 1)
            def _():
                finish(acc_ref[...])

    return pl.pallas_call(
        body, name=name, grid=(M // tm, N // tn, nk), in_specs=in_specs, out_specs=out_specs, out_shape=out_shape,
        scratch_shapes=[pltpu.VMEM((tm, tn), F32)] if nk > 1 else [],
        compiler_params=_params())(*operands)


def _group_index(n_x_tiles, tiles_per_example, n_examples):
    def gidx(i):
        return jnp.where(i < n_x_tiles, i // tiles_per_example, n_examples)
    return gidx


def _norm_mod_fwd(x, g, shift, scale, gidx, name):
    T, Dm = x.shape

    def body(x_ref, g_ref, sh_ref, sc_ref, h_ref):
        xv = x_ref[...]
        r = lax.rsqrt(jnp.mean(xv * xv, axis=-1, keepdims=True) + EPS)
        y = xv * r * g_ref[...]
        h_ref[...] = (y * (1.0 + sc_ref[0]) + sh_ref[0]).astype(h_ref.dtype)

    row = pl.BlockSpec((ROW_TILE, Dm), lambda i: (i, 0))
    mod = pl.BlockSpec((1, 1, Dm), lambda i: (gidx(i), 0, 0))
    return pl.pallas_call(
        body, name=name, grid=(T // ROW_TILE,),
        in_specs=[row, pl.BlockSpec((1, Dm), lambda i: (0, 0)), mod, mod],
        out_specs=row, out_shape=jax.ShapeDtypeStruct((T, Dm), MXU_DTYPE),
        compiler_params=_params())(x, g, shift, scale)


def _first_of_group(i, gidx):
    return jnp.logical_or(i == 0, gidx(i) != gidx(jnp.maximum(i - 1, 0)))


def _norm_mod_bwd(dh, x, g, scale, dres, gidx, n_groups, name, gated=None):
    T, Dm = x.shape
    res_tiles = dres.shape[0] // ROW_TILE

    def body(*refs):
        dh_ref, x_ref, g_ref, sc_ref, dres_ref = refs[:5]
        n_in = 7 if gated else 5
        dx_ref, dsh_ref, dsc_ref, dg_ref = refs[n_in:n_in + 4]
        i = pl.program_id(0)
        xv, dhv = x_ref[...], dh_ref[...]
        r = lax.rsqrt(jnp.mean(xv * xv, axis=-1, keepdims=True) + EPS)
        xn = xv * r
        y = xn * g_ref[...]

        @pl.when(_first_of_group(i, gidx))
        def _():
            dsh_ref[...] = jnp.zeros_like(dsh_ref)
            dsc_ref[...] = jnp.zeros_like(dsc_ref)

        @pl.when(i == 0)
        def _():
            dg_ref[...] = jnp.zeros_like(dg_ref)

        dsh_ref[0] += jnp.sum(dhv, axis=0, keepdims=True)
        dsc_ref[0] += jnp.sum(dhv * y, axis=0, keepdims=True)
        dy = dhv * (1.0 + sc_ref[0])
        dg_ref[...] += jnp.sum(dy * xn, axis=0, keepdims=True)
        dxn = dy * g_ref[...]
        dx = r * (dxn - xn * jnp.mean(dxn * xn, axis=-1, keepdims=True))
        dx = dx + (dres_ref[...] if res_tiles == T // ROW_TILE else jnp.where(i < res_tiles, dres_ref[...], 0.0))
        dx_ref[...] = dx
        if gated:
            f_ref, gate_ref = refs[5:7]
            dz_ref, dgate_ref = refs[n_in + 4:]

            @pl.when(_first_of_group(i, gidx))
            def _():
                dgate_ref[...] = jnp.zeros_like(dgate_ref)

            dgate_ref[0] += jnp.sum(dx * f_ref[...], axis=0, keepdims=True)
            dz_ref[...] = (dx * gate_ref[0]).astype(dz_ref.dtype)

    row = pl.BlockSpec((ROW_TILE, Dm), lambda i: (i, 0))
    mod = pl.BlockSpec((1, 1, Dm), lambda i: (gidx(i), 0, 0))
    vec = pl.BlockSpec((1, Dm), lambda i: (0, 0))
    mod_shape = jax.ShapeDtypeStruct((n_groups, 1, Dm), F32)
    res_row = pl.BlockSpec((ROW_TILE, Dm), lambda i: (jnp.minimum(i, res_tiles - 1), 0))
    in_specs, operands = [row, row, vec, mod, res_row], [dh, x, g, scale, dres]
    out_specs = [row, mod, mod, vec]
    out_shape = [jax.ShapeDtypeStruct((T, Dm), F32), mod_shape, mod_shape, jax.ShapeDtypeStruct((1, Dm), F32)]
    if gated:
        in_specs, operands = in_specs + [row, mod], operands + list(gated)
        out_specs, out_shape = out_specs + [row, mod], out_shape + [jax.ShapeDtypeStruct((T, Dm), MXU_DTYPE), mod_shape]
    return pl.pallas_call(
        body, name=name, grid=(T // ROW_TILE,), in_specs=in_specs, out_specs=tuple(out_specs),
        out_shape=tuple(out_shape), compiler_params=_params())(*operands)


def _gate_bwd(dy, f, gate, gidx, n_groups, name):
    T, Dm = dy.shape

    def body(dy_ref, f_ref, gate_ref, dz_ref, dgate_ref):
        i = pl.program_id(0)
        dyv = dy_ref[...]

        @pl.when(_first_of_group(i, gidx))
        def _():
            dgate_ref[...] = jnp.zeros_like(dgate_ref)

        dgate_ref[0] += jnp.sum(dyv * f_ref[...], axis=0, keepdims=True)
        dz_ref[...] = (dyv * gate_ref[0]).astype(dz_ref.dtype)

    row = pl.BlockSpec((ROW_TILE, Dm), lambda i: (i, 0))
    mod = pl.BlockSpec((1, 1, Dm), lambda i: (gidx(i), 0, 0))
    return pl.pallas_call(
        body, name=name, grid=(T // ROW_TILE,), in_specs=[row, row, mod], out_specs=(row, mod),
        out_shape=(jax.ShapeDtypeStruct((T, Dm), MXU_DTYPE), jax.ShapeDtypeStruct((n_groups, 1, Dm), F32)),
        compiler_params=_params())(dy, f, gate)


FFN_IN_ROWS = 512
FFN_IN_COLS = 1408
FFN_BWD_ROWS = 256


def _ffn_in_swiglu(h, w_in, name):
    T, Dm = h.shape
    nj = D_FF // FFN_IN_COLS

    def body(h_ref, wg_ref, wu_ref, g_ref, u_ref, a_ref):
        hv = h_ref[...]
        gate = _dot(hv, wg_ref[...], _NN)
        up = _dot(hv, wu_ref[...], _NN)
        g_ref[...] = gate
        u_ref[...] = up
        a_ref[...] = (gate * jax.nn.sigmoid(gate) * up).astype(a_ref.dtype)

    out = pl.BlockSpec((FFN_IN_ROWS, FFN_IN_COLS), lambda i, j: (i, j))
    pre = jax.ShapeDtypeStruct((T, D_FF), F32)
    return pl.pallas_call(
        body, name=name, grid=(T // FFN_IN_ROWS, nj),
        in_specs=[pl.BlockSpec((FFN_IN_ROWS, Dm), lambda i, j: (i, 0)),
                  pl.BlockSpec((Dm, FFN_IN_COLS), lambda i, j: (0, j)),
                  pl.BlockSpec((Dm, FFN_IN_COLS), lambda i, j: (0, nj + j))],
        out_specs=(out, out, out), out_shape=(pre, pre, jax.ShapeDtypeStruct((T, D_FF), MXU_DTYPE)),
        compiler_params=_params())(h, w_in, w_in)


def _ffn_out_bwd_swiglu(dz, w_out, gate, up, name):
    T, Dm = dz.shape

    def body(dz_ref, w_ref, g_ref, u_ref, du_ref):
        da = _dot(dz_ref[...], w_ref[...], _NT)
        gv, uv = g_ref[...], u_ref[...]
        sg = jax.nn.sigmoid(gv)
        du_ref[:, :D_FF] = (da * uv * (sg * (1.0 + gv * (1.0 - sg)))).astype(du_ref.dtype)
        du_ref[:, D_FF:] = (da * gv * sg).astype(du_ref.dtype)

    half = pl.BlockSpec((FFN_BWD_ROWS, D_FF), lambda i: (i, 0))
    return pl.pallas_call(
        body, name=name, grid=(T // FFN_BWD_ROWS,),
        in_specs=[pl.BlockSpec((FFN_BWD_ROWS, Dm), lambda i: (i, 0)), pl.BlockSpec((D_FF, Dm), lambda i: (0, 0)), half, half],
        out_specs=pl.BlockSpec((FFN_BWD_ROWS, 2 * D_FF), lambda i: (i, 0)),
        out_shape=jax.ShapeDtypeStruct((T, 2 * D_FF), MXU_DTYPE), compiler_params=_params())(dz, w_out, gate, up)


def _loss_fwd_bwd(y, target, name):
    T, Dm = y.shape

    def body(y_ref, t_ref, loss_ref, dy_ref):
        err = y_ref[...] - t_ref[...]

        @pl.when(pl.program_id(0) == 0)
        def _():
            loss_ref[...] = jnp.zeros_like(loss_ref)

        loss_ref[...] += 0.5 * jnp.sum(jnp.mean(err * err, axis=-1, keepdims=True))
        dy_ref[...] = err * (1.0 / Dm)

    row = pl.BlockSpec((ROW_TILE, Dm), lambda i: (i, 0))
    return pl.pallas_call(
        body, name=name, grid=(T // ROW_TILE,), in_specs=[row, row],
        out_specs=(pl.BlockSpec((8, LANES), lambda i: (0, 0)), row),
        out_shape=(jax.ShapeDtypeStruct((8, LANES), F32), jax.ShapeDtypeStruct((T, Dm), F32)),
        compiler_params=_params())(y, target)


def _rope_tables(seq, head_dim):
    axis_dim = head_dim // 2
    half = axis_dim // 2
    pos = jnp.arange(seq, dtype=jnp.int32)
    row = (pos // GRID_W).astype(F32)[:, None]
    col = (pos % GRID_W).astype(F32)[:, None]
    inv = ROPE_BASE ** (-jnp.arange(0, axis_dim, 2, dtype=F32) / axis_dim)
    lane = jnp.arange(head_dim, dtype=jnp.int32)
    within = lane % axis_dim
    ang = jnp.where((lane // axis_dim == 0)[None, :], row, col) * inv[within % half][None, :]
    cos = jnp.cos(ang)
    sin = jnp.where((within < half)[None, :], -jnp.sin(ang), jnp.sin(ang))
    cos = jnp.concatenate([cos, jnp.ones((ROW_TILE, head_dim), F32)], axis=0)
    sin = jnp.concatenate([sin, jnp.zeros((ROW_TILE, head_dim), F32)], axis=0)
    return cos, sin


def _pair_swap(v, half):
    if 2 * half == LANES:
        return pltpu.roll(v, half, axis=1)
    lane = lax.broadcasted_iota(jnp.int32, v.shape, 1)
    return jnp.where((lane % (2 * half)) < half, pltpu.roll(v, LANES - half, axis=1), pltpu.roll(v, half, axis=1))


def _head_sum(v, ones_ref):
    hi = v.astype(MXU_DTYPE)
    lo = (v - hi.astype(F32)).astype(MXU_DTYPE)
    return (jnp.dot(hi, ones_ref[...], preferred_element_type=F32)
            + jnp.dot(lo, ones_ref[...], preferred_element_type=F32))


def _head_ones():
    lane = jnp.arange(LANES)
    return (lane[:, None] // HEAD_DIM == lane[None, :] // HEAD_DIM).astype(MXU_DTYPE)


ATTN_QK_BLOCKS = (N_HEADS + N_KV_HEADS) * HEAD_DIM // LANES
ATTN_ALL_BLOCKS = (N_HEADS + 2 * N_KV_HEADS) * HEAD_DIM // LANES
ATTN_Q_BLOCKS = N_HEADS * HEAD_DIM // LANES
ATTN_SCALE = HEAD_DIM ** -0.5


def _attn_prep_fwd(qkv, gains, cos, sin, tidx, name):
    T, W = qkv.shape

    def body(x_ref, g_ref, cos_ref, sin_ref, ones_ref, o_ref):
        for cb in range(ATTN_ALL_BLOCKS):
            cols = slice(cb * LANES, (cb + 1) * LANES)
            xv = x_ref[:, cols]
            if cb < ATTN_QK_BLOCKS:
                r = lax.rsqrt(_head_sum(xv * xv, ones_ref) * (1.0 / HEAD_DIM) + EPS)
                y = xv * r * g_ref[0 if cb < ATTN_Q_BLOCKS else 1]
                xv = y * cos_ref[...] + _pair_swap(y, HEAD_DIM // 4) * sin_ref[...]
                if cb < ATTN_Q_BLOCKS:
                    xv = xv * ATTN_SCALE
            o_ref[:, cols] = xv.astype(o_ref.dtype)

    row = pl.BlockSpec((ROW_TILE, W), lambda i: (i, 0))
    tab = pl.BlockSpec((ROW_TILE, LANES), lambda i: (tidx(i), 0))
    return pl.pallas_call(
        body, name=name, grid=(T // ROW_TILE,),
        in_specs=[row, pl.BlockSpec((2, 1, LANES), lambda i: (0, 0, 0)), tab, tab,
                  pl.BlockSpec((LANES, LANES), lambda i: (0, 0))],
        out_specs=row, out_shape=jax.ShapeDtypeStruct(qkv.shape, MXU_DTYPE),
        compiler_params=_params())(qkv, gains, cos, sin, _head_ones())


def _attn_prep_bwd(latent, context, qkv, gains, cos, sin, tidx, name):
    T, W = qkv.shape
    qk_w = ATTN_QK_BLOCKS * LANES
    q_w = ATTN_Q_BLOCKS * LANES
    n_x = latent[0].shape[0] // ROW_TILE

    def body(dqx_ref, dkx_ref, dvx_ref, dqc_ref, dkc1_ref, dkc2_ref, dvc1_ref, dvc2_ref,
             x_ref, g_ref, cos_ref, sin_ref, ones_ref, o_ref, dg_ref):
        is_latent = pl.program_id(0) < n_x

        @pl.when(pl.program_id(0) == 0)
        def _():
            dg_ref[...] = jnp.zeros_like(dg_ref)

        for cb in range(ATTN_QK_BLOCKS):
            cols = slice(cb * LANES, (cb + 1) * LANES)
            xv = x_ref[:, cols]
            if cb < ATTN_Q_BLOCKS:
                d = jnp.where(is_latent, dqx_ref[:, cols], dqc_ref[:, cols]) * ATTN_SCALE
            else:
                kc = slice(cb * LANES - q_w, (cb + 1) * LANES - q_w)
                d = jnp.where(is_latent, dkx_ref[:, kc], dkc1_ref[:, kc] + dkc2_ref[:, kc])
            r = lax.rsqrt(_head_sum(xv * xv, ones_ref) * (1.0 / HEAD_DIM) + EPS)
            xn = xv * r
            dy = d * cos_ref[...] + _pair_swap(d * sin_ref[...], HEAD_DIM // 4)
            dg_ref[:, cols] += jnp.sum(dy * xn, axis=0, keepdims=True)
            dxn = dy * g_ref[0 if cb < ATTN_Q_BLOCKS else 1]
            dx = r * (dxn - xn * (_head_sum(dxn * xn, ones_ref) * (1.0 / HEAD_DIM)))
            o_ref[:, cols] = dx.astype(o_ref.dtype)
        o_ref[:, qk_w:] = jnp.where(is_latent, dvx_ref[...], dvc1_ref[...] + dvc2_ref[...]).astype(o_ref.dtype)

    row = lambda w: pl.BlockSpec((ROW_TILE, w), lambda i: (i, 0))
    lat = lambda t: pl.BlockSpec((ROW_TILE, t.shape[1]), lambda i: (jnp.minimum(i, n_x - 1), 0))
    ctx = lambda t: pl.BlockSpec((ROW_TILE, t.shape[1]), lambda i: (jnp.maximum(i - n_x, 0), 0))
    tab = pl.BlockSpec((ROW_TILE, LANES), lambda i: (tidx(i), 0))
    return pl.pallas_call(
        body, name=name, grid=(T // ROW_TILE,),
        in_specs=[lat(t) for t in latent] + [ctx(t) for t in context]
        + [row(W), pl.BlockSpec((2, 1, LANES), lambda i: (0, 0, 0)), tab, tab, pl.BlockSpec((LANES, LANES), lambda i: (0, 0))],
        out_specs=(row(W), pl.BlockSpec((1, qk_w), lambda i: (0, 0))),
        out_shape=(jax.ShapeDtypeStruct(qkv.shape, MXU_DTYPE), jax.ShapeDtypeStruct((1, qk_w), F32)),
        compiler_params=_params())(*latent, *context, qkv, gains, cos, sin, _head_ones())


RET_QK_BLOCKS = 2 * RET_HEADS * RET_QK_DIM // LANES


def _ret_rope(x, cos, sin, tidx, name):
    T = x.shape[0]
    W = RET_QK_BLOCKS * LANES
    k_scale = RET_QK_DIM ** -0.5

    def body(x_ref, cos_ref, sin_ref, o_ref):
        for cb in range(RET_QK_BLOCKS):
            cols = slice(cb * LANES, (cb + 1) * LANES)
            tcols = slice((cb % 2) * LANES, (cb % 2 + 1) * LANES)
            xv = x_ref[:, cols]
            out = xv * cos_ref[:, tcols] + pltpu.roll(xv, LANES // 2, axis=1) * sin_ref[:, tcols]
            if cb >= RET_QK_BLOCKS // 2:
                out = out * k_scale
            o_ref[:, cols] = out

    row = pl.BlockSpec((ROW_TILE, W), lambda i: (i, 0))
    tab = pl.BlockSpec((ROW_TILE, RET_QK_DIM), lambda i: (tidx(i), 0))
    return pl.pallas_call(
        body, name=name, grid=(T // ROW_TILE,), in_specs=[row, tab, tab], out_specs=row,
        out_shape=jax.ShapeDtypeStruct((T, W), F32), compiler_params=_params())(x, cos, sin)


ASSEMBLE_ROWS = 256


def _ret_grad_assemble(x_parts, c_parts, dg, cos, sin, seq, name):
    NX, NC = x_parts[0].shape[0], c_parts[0].shape[0]
    T = NX + NC
    rt = ASSEMBLE_ROWS
    nxt = NX // rt
    qk_w = RET_HEADS * RET_QK_DIM
    k_scale = RET_QK_DIM ** -0.5

    def unrotate(d, cos_ref, sin_ref, scale):
        outs = []
        for cb in range(qk_w // LANES):
            cols = slice(cb * LANES, (cb + 1) * LANES)
            tcols = slice((cb % 2) * LANES, (cb % 2 + 1) * LANES)
            dv_ = d[:, cols]
            o = dv_ * cos_ref[:, tcols] + pltpu.roll(dv_ * sin_ref[:, tcols], LANES // 2, axis=1)
            outs.append(o * scale if scale != 1.0 else o)
        return outs

    def body(dqf, dqb, dkf, dkb, dvf, dvb, dg_ref, dkcf, dkcb, dvcf, dvcb, cos_ref, sin_ref, o_ref):
        i = pl.program_id(0)

        def write_k(parts):
            for cb, o in enumerate(parts):
                o_ref[:, qk_w + cb * LANES:qk_w + (cb + 1) * LANES] = o.astype(o_ref.dtype)

        @pl.when(i < nxt)
        def _():
            for cb, o in enumerate(unrotate(dqf[...] + dqb[...], cos_ref, sin_ref, 1.0)):
                o_ref[:, cb * LANES:(cb + 1) * LANES] = o.astype(o_ref.dtype)
            write_k(unrotate(dkf[...] + dkb[...], cos_ref, sin_ref, k_scale))
            o_ref[:, 2 * qk_w:2 * qk_w + RET_VWIDTH] = (dvf[...] + dvb[...]).astype(o_ref.dtype)
            o_ref[:, 2 * qk_w + RET_VWIDTH:] = dg_ref[...].astype(o_ref.dtype)

        @pl.when(i >= nxt)
        def _():
            o_ref[:, :qk_w] = jnp.zeros((rt, qk_w), o_ref.dtype)
            write_k(unrotate(dkcf[...] + dkcb[...], cos_ref, sin_ref, k_scale))
            o_ref[:, 2 * qk_w:2 * qk_w + RET_VWIDTH] = (dvcf[...] + dvcb[...]).astype(o_ref.dtype)
            o_ref[:, 2 * qk_w + RET_VWIDTH:] = jnp.zeros((rt, RET_VWIDTH), o_ref.dtype)

    xs = lambda w: pl.BlockSpec((rt, w), lambda i: (jnp.minimum(i, nxt - 1), 0))
    cs = lambda w: pl.BlockSpec((rt, w), lambda i: (jnp.maximum(i - nxt, 0), 0))
    tab = pl.BlockSpec((rt, RET_QK_DIM), lambda i: (jnp.where(i < nxt, i % (seq // rt), seq // rt), 0))
    return pl.pallas_call(
        body, name=name, grid=(T // rt,),
        in_specs=[xs(qk_w)] * 4 + [xs(RET_VWIDTH)] * 3 + [cs(qk_w)] * 2 + [cs(RET_VWIDTH)] * 2 + [tab, tab],
        out_specs=pl.BlockSpec((rt, 2 * qk_w + 2 * RET_VWIDTH), lambda i: (i, 0)),
        out_shape=jax.ShapeDtypeStruct((T, 2 * qk_w + 2 * RET_VWIDTH), MXU_DTYPE),
        compiler_params=_params())(*x_parts, dg, *c_parts, cos, sin)


def _band_bias(qb, seq):
    nb = seq // qb
    assert nb >= 2
    i = jnp.arange(GQA_GROUP * qb, dtype=jnp.int32)[:, None] % qb
    n = jnp.arange(3 * qb, dtype=jnp.int32)[None, :]
    in_window = (n >= i) & (n - i <= 2 * WINDOW)
    variants = [in_window & (n >= qb), in_window, in_window & (n < 2 * qb)]
    return jnp.stack([jnp.where(v, 0.0, NEG_INF).astype(F32) for v in variants])


GROUP_ORDER = (0, 2, 1, 3)


def _stack_halves(blk):
    return jnp.concatenate([blk[:, :LANES], blk[:, LANES:]], axis=0)


def _unstack_halves(v, rows):
    return jnp.concatenate([v[:rows], v[rows:]], axis=1)


def _align_head(pair, odd):
    lane = lax.broadcasted_iota(jnp.int32, pair.shape, 1)
    mine = jnp.where((lane >= HEAD_DIM) == odd, pair, jnp.zeros_like(pair))
    rolled = pltpu.roll(mine, HEAD_DIM, axis=1)
    return jnp.where(odd, rolled, mine), jnp.where(odd, mine, rolled)


def _scores(out_ref, q2, x_eo):
    half = q2.shape[0]
    out_ref[:half, :] = _dot(q2, x_eo[0], _NT)
    out_ref[half:, :] = _dot(q2, x_eo[1], _NT)


def _apply(p_ref, x_eo):
    half = p_ref.shape[0] // 2
    return _dot(p_ref[:half, :], x_eo[0], _NN) + _dot(p_ref[half:, :], x_eo[1], _NN)


def _kv_grad(a_ref, q2, odd):
    half = a_ref.shape[0] // 2
    even_t = _dot(q2, a_ref[:half, :], _TN)
    odd_t = _dot(q2, a_ref[half:, :], _TN)
    mine = even_t[:HEAD_DIM] + odd_t[HEAD_DIM:]
    zero = jnp.zeros_like(mine)
    placed = jnp.where(odd, jnp.concatenate([zero, mine], axis=0), jnp.concatenate([mine, zero], axis=0))
    return placed.T


ATTN_ROW_CHUNK = 32


def _softmax_chunks(s_c_ref, s_l_ref, bias_ref, sink_ref, kv_head, qb, emit):
    for r0 in range(0, GQA_GROUP * qb, ATTN_ROW_CHUNK):
        rows = slice(r0, r0 + ATTN_ROW_CHUNK)
        t = r0 // qb
        sink = jnp.full((ATTN_ROW_CHUNK, 1), sink_ref[kv_head, GROUP_ORDER[t]], F32)
        s_c = s_c_ref[rows, :]
        m = jnp.maximum(jnp.max(s_c, axis=-1, keepdims=True), sink)
        s_l = None
        if s_l_ref is not None:
            s_l = s_l_ref[rows, :] + bias_ref[0, rows, :]
            m = jnp.maximum(m, jnp.max(s_l, axis=-1, keepdims=True))
        e_c = jnp.exp(s_c - m)
        e_s = jnp.exp(sink - m)
        den = jnp.sum(e_c, axis=-1, keepdims=True) + e_s
        e_l = None
        if s_l_ref is not None:
            e_l = jnp.exp(s_l - m)
            den = den + jnp.sum(e_l, axis=-1, keepdims=True)
        inv = 1.0 / den
        emit(t, rows, e_c * inv, (None if e_l is None else e_l * inv), e_s * inv)


GROUP_W = GQA_GROUP * HEAD_DIM
K_LANE_BLOCK = N_HEADS * HEAD_DIM // LANES
V_LANE_BLOCK = K_LANE_BLOCK + N_KV_HEADS * HEAD_DIM // LANES


def _attn_specs(B, seq, ctx_len, ctx_queries):
    ctx0 = B * seq // ctx_len
    if ctx_queries:
        qb, nb = ctx_len, 1
        qrow = lambda b, j: ctx0 + b
    else:
        qb, nb = ATTN_BLOCK, seq // ATTN_BLOCK
        qrow = lambda b, j: b * nb + j
    q_spec = pl.BlockSpec((qb, GROUP_W), lambda b, k, j: (qrow(b, j), k))
    c_specs = [pl.BlockSpec((ctx_len, LANES), lambda b, k, j: (ctx0 + b, K_LANE_BLOCK + k // 2)),
               pl.BlockSpec((ctx_len, LANES), lambda b, k, j: (ctx0 + b, V_LANE_BLOCK + k // 2))]
    local = []
    if not ctx_queries:
        near = [lambda j: jnp.maximum(j - 1, 0), lambda j: j, lambda j: jnp.minimum(j + 1, nb - 1)]
        for lane0 in (K_LANE_BLOCK, V_LANE_BLOCK):
            for f in near:
                local.append(pl.BlockSpec((qb, LANES), lambda b, k, j, f=f, lane0=lane0: (b * nb + f(j), lane0 + k // 2)))
        local.append(pl.BlockSpec(
            (1, GQA_GROUP * qb, 3 * qb), lambda b, k, j: (jnp.where(j == 0, 0, jnp.where(j == nb - 1, 2, 1)), 0, 0)))
    return qb, nb, qrow, q_spec, c_specs, local


def _attn_operands(refs, has_local, kv_head):
    odd = (kv_head % 2) == 1
    n_local = 7 if has_local else 0
    q2 = _stack_halves(refs[0][...])
    kc = _align_head(refs[1 + n_local][...], odd)
    vc = _align_head(refs[2 + n_local][...], odd)
    kl = vl = bias_ref = None
    if has_local:
        kl = _align_head(jnp.concatenate([r[...] for r in refs[1:4]], axis=0), odd)
        vl = _align_head(jnp.concatenate([r[...] for r in refs[4:7]], axis=0), odd)
        bias_ref = refs[7]
    return odd, q2, kc, vc, kl, vl, bias_ref


def _score_scratch(qb, ctx_len, has_local, dtypes):
    rows = GQA_GROUP * qb
    out = []
    for dt in dtypes:
        out.append(pltpu.VMEM((rows, ctx_len), dt))
        if has_local:
            out.append(pltpu.VMEM((rows, 3 * qb), dt))
    return out


def _score_bufs(scratch, has_local):
    if has_local:
        return [(scratch[i], scratch[i + 1]) for i in range(0, len(scratch), 2)]
    return [(s, None) for s in scratch]


def _attn_fwd(qkv, sink, B, seq, ctx_len, ctx_queries, name):
    has_local = not ctx_queries
    qb, nb, _, q_spec, c_specs, local = _attn_specs(B, seq, ctx_len, ctx_queries)
    n_rows = B * (ctx_len if ctx_queries else seq)
    n_in = 1 + (7 if has_local else 0) + 3

    def body(*refs):
        sink_ref, o_ref = refs[n_in - 1], refs[n_in]
        (s_c_ref, s_l_ref), (p_c_ref, p_l_ref) = _score_bufs(refs[n_in + 1:], has_local)
        kv_head = pl.program_id(1)
        _, q2, kc, vc, kl, vl, bias_ref = _attn_operands(refs, has_local, kv_head)
        _scores(s_c_ref, q2, kc)
        if has_local:
            _scores(s_l_ref, q2, kl)

        def emit(t, rows, p_c, p_l, p_s):
            p_c_ref[rows, :] = p_c.astype(p_c_ref.dtype)
            if has_local:
                p_l_ref[rows, :] = p_l.astype(p_l_ref.dtype)

        _softmax_chunks(s_c_ref, s_l_ref, bias_ref, sink_ref, kv_head, qb, emit)
        o2 = _apply(p_c_ref, vc)
        if has_local:
            o2 = o2 + _apply(p_l_ref, vl)
        o_ref[...] = _unstack_halves(o2, qb).astype(o_ref.dtype)

    operands = [qkv] + ([qkv] * 6 + [_band_bias(qb, seq)] if has_local else []) + [qkv, qkv, sink]
    return pl.pallas_call(
        body, name=name, grid=(B, N_KV_HEADS, nb),
        in_specs=[q_spec] + local + c_specs + [_SMEM],
        out_specs=pl.BlockSpec((qb, GROUP_W), lambda b, k, j: (b * nb + j, k)),
        out_shape=jax.ShapeDtypeStruct((n_rows, N_HEADS * HEAD_DIM), MXU_DTYPE),
        scratch_shapes=_score_scratch(qb, ctx_len, has_local, (F32, MXU_DTYPE)),
        compiler_params=_params())(*operands)


def _attn_bwd(qkv, sink, do, B, seq, ctx_len, ctx_queries, name):
    has_local = not ctx_queries
    qb, nb, qrow, q_spec, c_specs, local = _attn_specs(B, seq, ctx_len, ctx_queries)
    n_rows = B * (ctx_len if ctx_queries else seq)

    n_out = 6 if has_local else 4

    def body(*refs):
        n_in = 1 + (7 if has_local else 0) + 4
        sink_ref, do_ref = refs[n_in - 2:n_in]
        outs = refs[n_in:n_in + n_out]
        (s_c_ref, s_l_ref), (dp_c_ref, dp_l_ref), (p_c_ref, p_l_ref), (ds_c_ref, ds_l_ref) = _score_bufs(
            refs[n_in + n_out:], has_local)
        dq_ref = outs[0]
        dkc_ref, dvc_ref, dsink_ref = outs[-3:]
        b, kv_head, j = pl.program_id(0), pl.program_id(1), pl.program_id(2)
        odd, q2, kc, vc, kl, vl, bias_ref = _attn_operands(refs, has_local, kv_head)
        do2 = _stack_halves(do_ref[...])
        _scores(s_c_ref, q2, kc)
        _scores(dp_c_ref, do2, vc)
        if has_local:
            _scores(s_l_ref, q2, kl)
            _scores(dp_l_ref, do2, vl)
        dsink_parts = [jnp.zeros((), F32)] * GQA_GROUP

        def emit(t, rows, p_c, p_l, p_s):
            dp_c = dp_c_ref[rows, :]
            delta = jnp.sum(p_c * dp_c, axis=-1, keepdims=True)
            if has_local:
                dp_l = dp_l_ref[rows, :]
                delta = delta + jnp.sum(p_l * dp_l, axis=-1, keepdims=True)
                p_l_ref[rows, :] = p_l.astype(p_l_ref.dtype)
                ds_l_ref[rows, :] = (p_l * (dp_l - delta)).astype(ds_l_ref.dtype)
            p_c_ref[rows, :] = p_c.astype(p_c_ref.dtype)
            ds_c_ref[rows, :] = (p_c * (dp_c - delta)).astype(ds_c_ref.dtype)
            dsink_parts[t] = dsink_parts[t] - jnp.sum(p_s * delta)

        _softmax_chunks(s_c_ref, s_l_ref, bias_ref, sink_ref, kv_head, qb, emit)
        dq2 = _apply(ds_c_ref, kc)

        @pl.when((kv_head % 2 == 0) & (j == 0))
        def _():
            dkc_ref[...] = jnp.zeros_like(dkc_ref)
            dvc_ref[...] = jnp.zeros_like(dvc_ref)
            if has_local:
                outs[1][...] = jnp.zeros_like(outs[1])
                outs[2][...] = jnp.zeros_like(outs[2])

        @pl.when((b == 0) & (kv_head == 0) & (j == 0))
        def _():
            dsink_ref[...] = jnp.zeros_like(dsink_ref)

        dkc_ref[...] += _kv_grad(ds_c_ref, q2, odd)
        dvc_ref[...] += _kv_grad(p_c_ref, do2, odd)
        if has_local:
            dq2 = dq2 + _apply(ds_l_ref, kl)
            dkl = _kv_grad(ds_l_ref, q2, odd)
            dvl = _kv_grad(p_l_ref, do2, odd)
            dk_ref, dv_ref = outs[1], outs[2]
            for t in range(3):
                def add(t=t):
                    start = pl.multiple_of((j - 1 + t) * qb, qb)
                    dk_ref[pl.ds(start, qb), :] += dkl[t * qb:(t + 1) * qb]
                    dv_ref[pl.ds(start, qb), :] += dvl[t * qb:(t + 1) * qb]
                if t == 0:
                    pl.when(j > 0)(add)
                elif t == 2:
                    pl.when(j < nb - 1)(add)
                else:
                    add()
        dq_ref[...] = _unstack_halves(dq2, qb)
        sub = lax.broadcasted_iota(jnp.int32, (8, LANES), 0)
        tile = jnp.zeros((8, LANES), F32)
        for t, gi in enumerate(GROUP_ORDER):
            tile = jnp.where(sub == gi, dsink_parts[t], tile)
        dsink_ref[pl.ds(pl.multiple_of(kv_head * 8, 8), 8), :] += tile

    kv_w = N_KV_HEADS * HEAD_DIM
    seq_spec = pl.BlockSpec((seq, LANES), lambda b, k, j: (b, k // 2))
    ctx_spec = pl.BlockSpec((ctx_len, LANES), lambda b, k, j: (b, k // 2))
    do_spec = pl.BlockSpec((qb, GROUP_W), lambda b, k, j: (qrow(b, j), k))
    operands = [qkv] + ([qkv] * 6 + [_band_bias(qb, seq)] if has_local else []) + [qkv, qkv, sink, do]
    out_specs = ([pl.BlockSpec((qb, GROUP_W), lambda b, k, j: (b * nb + j, k))] + ([seq_spec, seq_spec] if has_local else [])
                 + [ctx_spec, ctx_spec, pl.BlockSpec((32, LANES), lambda b, k, j: (0, 0))])
    out_shape = ([jax.ShapeDtypeStruct((n_rows, N_HEADS * HEAD_DIM), F32)]
                 + ([jax.ShapeDtypeStruct((B * seq, kv_w), F32)] * 2 if has_local else [])
                 + [jax.ShapeDtypeStruct((B * ctx_len, kv_w), F32)] * 2 + [jax.ShapeDtypeStruct((32, LANES), F32)])
    return pl.pallas_call(
        body, name=name, grid=(B, N_KV_HEADS, nb),
        in_specs=[q_spec] + local + c_specs + [_SMEM, do_spec],
        out_specs=tuple(out_specs), out_shape=tuple(out_shape),
        scratch_shapes=_score_scratch(qb, ctx_len, has_local, (F32, F32, MXU_DTYPE, MXU_DTYPE)),
        compiler_params=_params())(*operands)


def _ret_decays(lg, rev):
    n = lax.broadcasted_iota(jnp.int32, (RET_CHUNK, RET_CHUNK), 0).astype(F32)
    m = lax.broadcasted_iota(jnp.int32, (RET_CHUNK, RET_CHUNK), 1).astype(F32)
    pos = lax.broadcasted_iota(jnp.int32, (RET_CHUNK, 1), 0).astype(F32)
    diff = (m - n) if rev else (n - m)
    a_exp = jnp.maximum(diff, 0.0)
    intra = jnp.where(diff >= 0, jnp.exp(lg * a_exp), 0.0)
    q_exp = (RET_CHUNK - pos) if rev else (pos + 1.0)
    k_exp = pos if rev else (RET_CHUNK - 1.0 - pos)
    chunk = jnp.exp(jnp.full((1, 1), RET_CHUNK, F32) * lg)
    return intra, a_exp, jnp.exp(lg * q_exp), q_exp, jnp.exp(lg * k_exp), k_exp, chunk


def _ctx_decay(lg, ctx_len, rev):
    t = lax.broadcasted_iota(jnp.int32, (ctx_len, 1), 0).astype(F32)
    expo = t if rev else (ctx_len - 1.0 - t)
    return jnp.exp(lg * expo), expo


def _ret_specs(B, seq, ctx_len, order):
    nc = seq // RET_CHUNK
    x_blocks = B * seq // ctx_len

    def rows(b, c):
        return b * nc + order(c, nc)

    q_spec = pl.BlockSpec((RET_CHUNK, RET_QK_DIM), lambda b, h, c: (rows(b, c), h))
    k_spec = pl.BlockSpec((RET_CHUNK, RET_QK_DIM), lambda b, h, c: (rows(b, c), RET_HEADS + h))
    v_spec = pl.BlockSpec((RET_CHUNK, RET_V_DIM), lambda b, h, c: (rows(b, c), RET_HEADS + h))
    kc_spec = pl.BlockSpec((ctx_len, RET_QK_DIM), lambda b, h, c: (x_blocks + b, RET_HEADS + h))
    vc_spec = pl.BlockSpec((ctx_len, RET_V_DIM), lambda b, h, c: (x_blocks + b, RET_HEADS + h))
    st_spec = pl.BlockSpec((1, 1, 1, RET_QK_DIM, RET_V_DIM), lambda b, h, c: (b, h, order(c, nc), 0, 0))
    o_spec = pl.BlockSpec((RET_CHUNK, RET_V_DIM), lambda b, h, c: (rows(b, c), h))
    return nc, q_spec, k_spec, v_spec, kc_spec, vc_spec, st_spec, o_spec


_SCAN_UP = lambda c, nc: c
_SCAN_DOWN = lambda c, nc: nc - 1 - c


def _ret_fwd(qk, qkvg, log_g, B, seq, ctx_len, name):
    nc, qf, kf, vf, kc_spec, vc_spec, stf, of = _ret_specs(B, seq, ctx_len, _SCAN_UP)
    _, qr, kr, vr, _, _, str_, or_ = _ret_specs(B, seq, ctx_len, _SCAN_DOWN)

    def body(lg_ref, qf_ref, kf_ref, vf_ref, qr_ref, kr_ref, vr_ref, kc_ref, vc_ref,
             of_ref, stf_ref, or_ref, str_ref, state_f, state_r):
        h, c = pl.program_id(1), pl.program_id(2)
        dirs = ((False, lg_ref[0, h], qf_ref, kf_ref, vf_ref, of_ref, stf_ref, state_f),
                (True, lg_ref[1, h], qr_ref, kr_ref, vr_ref, or_ref, str_ref, state_r))

        @pl.when(c == 0)
        def _():
            for rev, lg, _, _, _, _, _, state in dirs:
                dec, _ = _ctx_decay(lg, ctx_len, rev)
                state[...] = _dot(kc_ref[...] * dec, vc_ref[...], _TN)

        for rev, lg, q_ref, k_ref, v_ref, o_ref, st_ref, state in dirs:
            intra, _, q_dec, _, k_dec, _, chunk_dec = _ret_decays(lg, rev)
            qv, kv, vv = q_ref[...], k_ref[...], v_ref[...]
            s_in = state[...]
            st_ref[0, 0, 0] = s_in
            w = _dot(qv, kv, _NT) * intra
            o_ref[...] = _dot(w, vv, _NN) + _dot(qv, s_in, _NN) * q_dec
            state[...] = s_in * chunk_dec + _dot(kv * k_dec, vv, _TN)

    o_shape = jax.ShapeDtypeStruct((B * seq, RET_VWIDTH), F32)
    st_shape = jax.ShapeDtypeStruct((B, RET_HEADS, nc, RET_QK_DIM, RET_V_DIM), F32)
    return pl.pallas_call(
        body, name=name, grid=(B, RET_HEADS, nc),
        in_specs=[_SMEM, qf, kf, vf, qr, kr, vr, kc_spec, vc_spec],
        out_specs=(of, stf, or_, str_), out_shape=(o_shape, st_shape, o_shape, st_shape),
        scratch_shapes=[pltpu.VMEM((RET_QK_DIM, RET_V_DIM), F32)] * 2,
        compiler_params=_params())(log_g, qk, qk, qkvg, qk, qk, qkvg, qk, qkvg)


def _ret_bwd_chunk(rev, lg, q_ref, k_ref, v_ref, st_ref, do_ref, dq_ref, dk_ref, dv_ref, dlg_ref, dstate):
    intra, a_exp, q_dec, q_exp, k_dec, k_exp, chunk_dec = _ret_decays(lg, rev)
    qv, kv, vv, dov = q_ref[...], k_ref[...], v_ref[...], do_ref[...]
    s_in, ds_out = st_ref[0, 0, 0], dstate[...]
    p = _dot(qv, kv, _NT)
    w = p * intra
    dw = _dot(dov, vv, _NT)
    dp = dw * intra
    do_dec = dov * q_dec
    kd = kv * k_dec
    v_ds = _dot(vv, ds_out, _NT)
    dq_ref[...] = _dot(dp, kv, _NN) + _dot(do_dec, s_in, _NT)
    dk_ref[...] = _dot(dp, qv, _TN) + v_ds * k_dec
    dv_ref[...] = _dot(w, dov, _TN) + _dot(kd, ds_out, _NN)
    q_s = _dot(qv, s_in, _NN)
    dlg = (jnp.sum(dw * w * a_exp)
           + jnp.sum(q_exp * q_dec * jnp.sum(dov * q_s, axis=-1, keepdims=True))
           + jnp.sum(k_exp * k_dec * jnp.sum(kv * v_ds, axis=-1, keepdims=True))
           + RET_CHUNK * jnp.sum(chunk_dec * (ds_out * s_in)))
    ds_in = ds_out * chunk_dec + _dot(qv, do_dec, _TN)
    dstate[...] = ds_in
    dlg_ref[...] += dlg
    return ds_in


def _ret_bwd(qk, qkvg, log_g, st_f, st_r, do, B, seq, ctx_len, name):
    nc, qf, kf, vf, kc_spec, vc_spec, stf, of = _ret_specs(B, seq, ctx_len, _SCAN_DOWN)
    _, qr, kr, vr, _, _, str_, or_ = _ret_specs(B, seq, ctx_len, _SCAN_UP)

    def body(lg_ref, qf_ref, kf_ref, vf_ref, stf_ref, dof_ref, qr_ref, kr_ref, vr_ref, str_ref, dor_ref, kc_ref, vc_ref,
             dqf, dkf, dvf, dkcf, dvcf, dlgf, dqr, dkr, dvr, dkcr, dvcr, dlgr, dstate_f, dstate_r):
        h, c = pl.program_id(1), pl.program_id(2)
        dirs = ((False, lg_ref[0, h], (qf_ref, kf_ref, vf_ref, stf_ref, dof_ref, dqf, dkf, dvf, dlgf, dstate_f), dkcf, dvcf),
                (True, lg_ref[1, h], (qr_ref, kr_ref, vr_ref, str_ref, dor_ref, dqr, dkr, dvr, dlgr, dstate_r), dkcr, dvcr))

        @pl.when(c == 0)
        def _():
            for _, _, refs, _, _ in dirs:
                refs[-1][...] = jnp.zeros_like(refs[-1])
                refs[-2][...] = jnp.zeros_like(refs[-2])

        ds_first = [_ret_bwd_chunk(rev, lg, *refs) for rev, lg, refs, _, _ in dirs]

        @pl.when(c == nc - 1)
        def _():
            for (rev, lg, refs, dkc_ref, dvc_ref), ds_in in zip(dirs, ds_first):
                dec, expo = _ctx_decay(lg, ctx_len, rev)
                kcv, vcv = kc_ref[...], vc_ref[...]
                vc_ds = _dot(vcv, ds_in, _NT)
                dkc_ref[...] = vc_ds * dec
                dvc_ref[...] = _dot(kcv * dec, ds_in, _NN)
                refs[-2][...] += jnp.sum(expo * dec * jnp.sum(kcv * vc_ds, axis=-1, keepdims=True))

    def outs(q_spec, o_spec):
        return (pl.BlockSpec((RET_CHUNK, RET_QK_DIM), q_spec.index_map),
                pl.BlockSpec((RET_CHUNK, RET_QK_DIM), q_spec.index_map), o_spec,
                pl.BlockSpec((ctx_len, RET_QK_DIM), lambda b, h, c: (b, h)),
                pl.BlockSpec((ctx_len, RET_V_DIM), lambda b, h, c: (b, h)),
                pl.BlockSpec((1, 1, 8, LANES), lambda b, h, c: (b, h, 0, 0)))

    shapes = (jax.ShapeDtypeStruct((B * seq, RET_HEADS * RET_QK_DIM), F32),
              jax.ShapeDtypeStruct((B * seq, RET_HEADS * RET_QK_DIM), F32),
              jax.ShapeDtypeStruct((B * seq, RET_VWIDTH), F32),
              jax.ShapeDtypeStruct((B * ctx_len, RET_HEADS * RET_QK_DIM), F32),
              jax.ShapeDtypeStruct((B * ctx_len, RET_VWIDTH), F32),
              jax.ShapeDtypeStruct((B, RET_HEADS, 8, LANES), F32))
    res = pl.pallas_call(
        body, name=name, grid=(B, RET_HEADS, nc),
        in_specs=[_SMEM, qf, kf, vf, stf, of, qr, kr, vr, str_, or_, kc_spec, vc_spec],
        out_specs=outs(qf, of) + outs(qr, or_), out_shape=shapes + shapes,
        scratch_shapes=[pltpu.VMEM((RET_QK_DIM, RET_V_DIM), F32)] * 2,
        compiler_params=_params())(log_g, qk, qk, qkvg, st_f, do, qk, qk, qkvg, st_r, do, qk, qkvg)
    return res[:6], res[6:]


def _gated_out_fwd(o_f, o_b, qkvg, gn_gain, name):
    T = o_f.shape[0]
    g_off = (2 * RET_HEADS * RET_QK_DIM + RET_VWIDTH) // RET_V_DIM

    def body(of_ref, ob_ref, g_ref, gain_ref, z_ref):
        o = of_ref[...] + ob_ref[...]
        mu = jnp.mean(o, axis=-1, keepdims=True)
        var = jnp.mean(jnp.square(o - mu), axis=-1, keepdims=True)
        y = (o - mu) * lax.rsqrt(var + EPS) * gain_ref[...]
        gv = g_ref[...]
        z_ref[...] = (gv * jax.nn.sigmoid(gv) * y).astype(z_ref.dtype)

    blk = pl.BlockSpec((ROW_TILE, RET_V_DIM), lambda i, h: (i, h))
    return pl.pallas_call(
        body, name=name, grid=(T // ROW_TILE, RET_HEADS),
        in_specs=[blk, blk, pl.BlockSpec((ROW_TILE, RET_V_DIM), lambda i, h: (i, g_off + h)),
                  pl.BlockSpec((1, RET_V_DIM), lambda i, h: (0, h))],
        out_specs=blk, out_shape=jax.ShapeDtypeStruct((T, RET_VWIDTH), MXU_DTYPE),
        compiler_params=_params())(o_f, o_b, qkvg, gn_gain)


def _gated_out_bwd(dz, o_f, o_b, qkvg, gn_gain, name):
    T = o_f.shape[0]
    g_off = (2 * RET_HEADS * RET_QK_DIM + RET_VWIDTH) // RET_V_DIM

    def body(dz_ref, of_ref, ob_ref, g_ref, gain_ref, do_ref, dg_ref, dgain_ref):
        o = of_ref[...] + ob_ref[...]
        mu = jnp.mean(o, axis=-1, keepdims=True)
        var = jnp.mean(jnp.square(o - mu), axis=-1, keepdims=True)
        rstd = lax.rsqrt(var + EPS)
        yhat = (o - mu) * rstd
        gv, dzv = g_ref[...], dz_ref[...]
        sg = jax.nn.sigmoid(gv)
        dg_ref[...] = (dzv * (yhat * gain_ref[...]) * (sg * (1.0 + gv * (1.0 - sg)))).astype(dg_ref.dtype)
        dy = dzv * (gv * sg)

        @pl.when(pl.program_id(1) == 0)
        def _():
            dgain_ref[...] = jnp.zeros_like(dgain_ref)

        dgain_ref[...] += jnp.sum(dy * yhat, axis=0, keepdims=True)
        dyh = dy * gain_ref[...]
        do_ref[...] = rstd * (dyh - jnp.mean(dyh, axis=-1, keepdims=True)
                              - yhat * jnp.mean(dyh * yhat, axis=-1, keepdims=True))

    blk = pl.BlockSpec((ROW_TILE, RET_V_DIM), lambda h, i: (i, h))
    vec = pl.BlockSpec((1, RET_V_DIM), lambda h, i: (0, h))
    return pl.pallas_call(
        body, name=name, grid=(RET_HEADS, T // ROW_TILE),
        in_specs=[blk, blk, blk, pl.BlockSpec((ROW_TILE, RET_V_DIM), lambda h, i: (i, g_off + h)), vec],
        out_specs=(blk, blk, vec),
        out_shape=(jax.ShapeDtypeStruct((T, RET_VWIDTH), F32), jax.ShapeDtypeStruct((T, RET_VWIDTH), MXU_DTYPE),
                   jax.ShapeDtypeStruct((1, RET_VWIDTH), F32)),
        compiler_params=_params())(dz, o_f, o_b, qkvg, gn_gain)


def _adamw(w, m, v, parts, name):
    R, C = w.shape
    tr = _tile(R, (256, 128, 64, 32, 16, 8))
    n_parts = [p.shape[0] for p in parts]

    def body(*refs):
        w_ref, m_ref, v_ref = refs[:3]
        part_refs = refs[3:3 + len(parts)]
        g_ref, d_ref, nm_ref, nv_ref = refs[3 + len(parts):]
        g = None
        for ref, n in zip(part_refs, n_parts):
            for r in range(n):
                term = ref[r].astype(F32)
                g = term if g is None else g + term
        mn = ADAM_B1 * m_ref[...] + (1.0 - ADAM_B1) * g
        vn = ADAM_B2 * v_ref[...] + (1.0 - ADAM_B2) * jnp.square(g)
        m_hat = mn / (1.0 - ADAM_B1 ** ADAM_STEP)
        v_hat = vn / (1.0 - ADAM_B2 ** ADAM_STEP)
        g_ref[...] = g
        d_ref[...] = -ADAM_LR * (m_hat / (jnp.sqrt(v_hat) + ADAM_EPS) + ADAM_WD * w_ref[...])
        nm_ref[...] = mn
        nv_ref[...] = vn

    blk = pl.BlockSpec((tr, C), lambda i: (i, 0))
    part_specs = [pl.BlockSpec((n, tr, C), lambda i: (0, i, 0)) for n in n_parts]
    shp = jax.ShapeDtypeStruct((R, C), F32)
    return pl.pallas_call(
        body, name=name, grid=(R // tr,), in_specs=[blk, blk, blk] + part_specs,
        out_specs=(blk, blk, blk, blk), out_shape=(shp, shp, shp, shp),
        compiler_params=_params())(w, m, v, *parts)


def _sum_rows(parts, name):
    n, R, C = parts.shape
    tr = _tile(R, (256, 128, 64, 32, 16, 8))

    def body(p_ref, o_ref):
        acc = p_ref[0]
        for r in range(1, n):
            acc = acc + p_ref[r]
        o_ref[...] = acc

    return pl.pallas_call(
        body, name=name, grid=(R // tr,), in_specs=[pl.BlockSpec((n, tr, C), lambda i: (0, i, 0))],
        out_specs=pl.BlockSpec((tr, C), lambda i: (i, 0)), out_shape=jax.ShapeDtypeStruct((R, C), F32),
        compiler_params=_params())(parts)


def _my_coords():
    return lax.axis_index("x"), lax.axis_index("y"), lax.axis_index("c")


def _flip(coord, bit):
    return 1 - coord if bit else coord


def _all_gather(x2d, name):
    R, C = x2d.shape

    def body(x_ref, out_ref, send_sems, recv_sems, local_sem):
        x, y, c = _my_coords()
        me, sibling = (x, y, c), (x, y, 1 - c)
        chips = [(1 - x, y), (x, 1 - y), (1 - x, 1 - y)]

        def rows(px, py, pc):
            return out_ref.at[4 * px + 2 * py + pc]

        def copy(k, block, to, src=None):
            return pltpu.make_async_remote_copy(
                src_ref=rows(*block) if src is None else src, dst_ref=rows(*block),
                send_sem=send_sems.at[k], recv_sem=recv_sems.at[k], device_id=to, device_id_type=MESH)

        mine = pltpu.make_async_copy(x_ref, rows(*me), local_sem)
        mine.start()
        first = [copy(0, me, sibling, src=x_ref)]
        first += [copy(1 + j, me, (*chip, c), src=x_ref) for j, chip in enumerate(chips)]
        for cp in first:
            cp.start()
        passed = [copy(4 + j, (*chip, c), sibling) for j, chip in enumerate(chips)]
        for j, chip in enumerate(chips):
            copy(1 + j, (*chip, c), me).wait_recv()
            passed[j].start()
        copy(0, sibling, me).wait_recv()
        for j, chip in enumerate(chips):
            copy(4 + j, (*chip, 1 - c), me).wait_recv()
        for cp in first + passed:
            cp.wait_send()
        mine.wait()

    return pl.pallas_call(
        body, name=name, out_shape=jax.ShapeDtypeStruct((N_DEV, R, C), x2d.dtype),
        in_specs=[_ANY], out_specs=_ANY,
        scratch_shapes=[pltpu.SemaphoreType.DMA((7,)), pltpu.SemaphoreType.DMA((7,)), pltpu.SemaphoreType.DMA],
    )(x2d)


BIG_WEIGHTS = {
    "ffn_w_in": (2, (2, D_MODEL, 2 * D_FF)),
    "ffn_w_out": (1, (2, D_FF, D_MODEL)),
    "attn_w_qkv": (2, (1, D_MODEL, (N_HEADS + 2 * N_KV_HEADS) * HEAD_DIM)),
    "attn_w_o": (1, (1, N_HEADS * HEAD_DIM, D_MODEL)),
    "ret_w_qkvg": (2, (1, D_MODEL, 2 * D_MODEL + 2 * RET_VWIDTH)),
    "ret_gn_g": (2, (1, 1, RET_VWIDTH)),
    "ret_w_o": (1, (1, RET_VWIDTH, D_MODEL)),
}


def _join_shards(name, stacked):
    axis, full = BIG_WEIGHTS[name]
    if axis == 2:
        stacked = stacked.transpose(0, 2, 1, 3)
    return stacked.reshape(full)


def _split_shards(name, full_arr):
    axis, (_, rows, cols) = BIG_WEIGHTS[name]
    L = full_arr.shape[0]
    if axis == 2:
        return full_arr.reshape(L, rows, N_DEV, cols // N_DEV).transpose(0, 2, 1, 3)
    return full_arr.reshape(L, N_DEV, rows // N_DEV, cols)


def _gather_shards(shards, name):
    n = len(shards)

    def body(*refs):
        x_refs, out_refs = refs[:n], refs[n:2 * n]
        send_sems, recv_sems, local_sems = refs[2 * n:]
        x, y, c = _my_coords()
        me, sibling = (x, y, c), (x, y, 1 - c)
        chips = [(1 - x, y), (x, 1 - y), (1 - x, 1 - y)]

        def rows(a, px, py, pc):
            return out_refs[a].at[:, 4 * px + 2 * py + pc]

        def copy(a, k, block, to, src=None):
            return pltpu.make_async_remote_copy(
                src_ref=rows(a, *block) if src is None else src, dst_ref=rows(a, *block),
                send_sem=send_sems.at[7 * a + k], recv_sem=recv_sems.at[7 * a + k], device_id=to, device_id_type=MESH)

        mine = [pltpu.make_async_copy(x_refs[a], rows(a, *me), local_sems.at[a]) for a in range(n)]
        for cp in mine:
            cp.start()
        first = []
        for a in range(n):
            first.append(copy(a, 0, me, sibling, src=x_refs[a]))
            first += [copy(a, 1 + j, me, (*chip, c), src=x_refs[a]) for j, chip in enumerate(chips)]
        for cp in first:
            cp.start()
        passed = []
        for j, chip in enumerate(chips):
            for a in range(n):
                copy(a, 1 + j, (*chip, c), me).wait_recv()
                fwd = copy(a, 4 + j, (*chip, c), sibling)
                fwd.start()
                passed.append(fwd)
        for a in range(n):
            copy(a, 0, sibling, me).wait_recv()
            for j, chip in enumerate(chips):
                copy(a, 4 + j, (*chip, 1 - c), me).wait_recv()
        for cp in first + passed:
            cp.wait_send()
        for cp in mine:
            cp.wait()

    return pl.pallas_call(
        body, name=name,
        out_shape=[jax.ShapeDtypeStruct((s.shape[0], N_DEV) + s.shape[1:], s.dtype) for s in shards],
        in_specs=[_ANY] * n, out_specs=[_ANY] * n,
        scratch_shapes=[pltpu.SemaphoreType.DMA((7 * n,)), pltpu.SemaphoreType.DMA((7 * n,)),
                        pltpu.SemaphoreType.DMA((n,))],
    )(*shards)


def _exchange_shards(arrs, masks, src_of, out_tail, name):
    n, nm = len(arrs), len(masks)

    def body(*refs):
        in_refs, out_refs = refs[:n], refs[n:2 * n]
        send_sems, recv_sems = refs[2 * n:]
        x, y, c = _my_coords()
        copies = []
        for a in range(n):
            for k, (bx, by, bc) in enumerate(masks):
                peer = (_flip(x, bx), _flip(y, by), _flip(c, bc))
                copies.append(pltpu.make_async_remote_copy(
                    src_ref=src_of(in_refs[a], peer, (x, y, c)), dst_ref=out_refs[a].at[k],
                    send_sem=send_sems.at[nm * a + k], recv_sem=recv_sems.at[nm * a + k],
                    device_id=peer, device_id_type=MESH))
        for cp in copies:
            cp.start()
        for cp in copies:
            cp.wait()

    return pl.pallas_call(
        body, name=name,
        out_shape=[jax.ShapeDtypeStruct((nm,) + out_tail(s), s.dtype) for s in arrs],
        in_specs=[_ANY] * n, out_specs=[_ANY] * n,
        scratch_shapes=[pltpu.SemaphoreType.DMA((nm * n,)), pltpu.SemaphoreType.DMA((nm * n,))],
    )(*arrs)


def _pair_sum(g, from_sibling, core, out_dtype, name):
    L, _, _, a, b = g.shape
    ta = a

    def body(core_ref, g_ref, s_ref, o_ref):
        o_ref[...] = (g_ref[...] + s_ref[...]).astype(out_dtype)

    blk = pl.BlockSpec((1, 1, ta, b), lambda l, q, i, core_ref: (l, q, i, 0))
    return pl.pallas_call(
        body, name=name,
        grid_spec=pltpu.PrefetchScalarGridSpec(
            num_scalar_prefetch=1, grid=(L, 4, a // ta),
            in_specs=[pl.BlockSpec((1, 1, pl.Squeezed(), ta, b), lambda l, q, i, core_ref: (l, q, core_ref[0], i, 0)), blk],
            out_specs=blk),
        out_shape=jax.ShapeDtypeStruct((L, 4, a, b), out_dtype), compiler_params=_params())(core, g, from_sibling)


def _mods(mod_x, mod_c, layer):
    both = jnp.concatenate([mod_x[:, layer], mod_c[layer][None]], axis=0)
    return [both[:, None, k * D_MODEL:(k + 1) * D_MODEL] for k in range(6)]


def _local_step(x, ctx, target, mod_x, mod_c, w, small, late_weights=None, hooks=None):
    B, S, _ = x.shape
    L = ctx.shape[1]
    NX, NC = B * S, B * L
    T = NX + NC
    tiles_per_ex = S // ROW_TILE
    nxt = NX // ROW_TILE
    gidx = _group_index(nxt, tiles_per_ex, B)
    gidx_for = lambda rows: _group_index(NX // rows, S // rows, B)
    mm_rows = _tile(S, (MM_ROWS, ROW_TILE))
    tidx = lambda i: jnp.where(i < nxt, i % tiles_per_ex, tiles_per_ex)
    G = B + 1
    x0 = jnp.concatenate([x.reshape(NX, D_MODEL), ctx.reshape(NC, D_MODEL)], axis=0)
    acos, asin = [jnp.tile(t, (1, LANES // HEAD_DIM)) for t in _rope_tables(S, HEAD_DIM)]
    rcos, rsin = _rope_tables(S, RET_QK_DIM)
    sink = small["attn_sink"].reshape(N_KV_HEADS, GQA_GROUP)
    gains = jnp.stack([jnp.tile(small["attn_q_norm"].reshape(1, HEAD_DIM), (1, LANES // HEAD_DIM)),
                       jnp.tile(small["attn_k_norm"].reshape(1, HEAD_DIM), (1, LANES // HEAD_DIM))])
    log_g = jax.nn.log_sigmoid(small["ret_decay_logit"].reshape(2, RET_HEADS))
    n1, n2 = small["norm1_g"], small["norm2_g"]

    m0 = _mods(mod_x, mod_c, 0)
    h1 = _norm_mod_fwd(x0, n1[0:1], m0[0], m0[1], gidx, "l0_norm1")
    qkv = _mm(h1, w["attn_w_qkv"][0], "nn", F32, "l0_qkv")
    qkv_r = _attn_prep_fwd(qkv, gains, acos, asin, tidx, "l0_qk_prep")
    o_x = _attn_fwd(qkv_r, sink, B, S, L, False, "l0_attn_x")
    o_c = _attn_fwd(qkv_r, sink, B, S, L, True, "l0_attn_c")
    o0 = jnp.concatenate([o_x, o_c], axis=0)
    mo0, x1 = _mm(o0, w["attn_w_o"][0], "nn", F32, "l0_attn_out", res=x0, gate=m0[2], gidx_for=gidx_for, gate_rows=mm_rows)
    h2 = _norm_mod_fwd(x1, n2[0:1], m0[3], m0[4], gidx, "l0_norm2")
    if late_weights is not None:
        w = {**w, **late_weights(x1)}
    ug0, uu0, a0 = _ffn_in_swiglu(h2, w["ffn_w_in"][0], "l0_ffn_in")
    f0, x2 = _mm(a0, w["ffn_w_out"][0], "nn", F32, "l0_ffn_out", res=x1, gate=m0[5], gidx_for=gidx_for, gate_rows=mm_rows)

    m1 = _mods(mod_x, mod_c, 1)
    g1 = _norm_mod_fwd(x2, n1[1:2], m1[0], m1[1], gidx, "l1_norm1")
    qkvg = _mm(g1, w["ret_w_qkvg"][0], "nn", F32, "l1_qkvg")
    qk = _ret_rope(qkvg, rcos, rsin, tidx, "l1_rope")
    of, st_f, ob, st_b = _ret_fwd(qk, qkvg, log_g, B, S, L, "l1_ret")
    gn = w["ret_gn_g"].reshape(1, RET_VWIDTH)
    z1 = _gated_out_fwd(of, ob, qkvg, gn, "l1_gated_out")
    gx = lambda i: i // tiles_per_ex
    m1x = [t[:B] for t in m1]
    mo1, y1 = _mm(z1, w["ret_w_o"][0], "nn", F32, "l1_ret_out", res=x2, gate=m1x[2], gidx_for=gidx_for, gate_rows=mm_rows)
    k2 = _norm_mod_fwd(y1, n2[1:2], m1x[3], m1x[4], gx, "l1_norm2")
    ug1, uu1, a1 = _ffn_in_swiglu(k2, w["ffn_w_in"][1], "l1_ffn_in")
    f1, y2 = _mm(a1, w["ffn_w_out"][1], "nn", F32, "l1_ffn_out", res=y1, gate=m1x[5], gidx_for=gidx_for, gate_rows=mm_rows)

    loss_tile, dy2 = _loss_fwd_bwd(y2, target.reshape(NX, D_MODEL), "loss")

    zg = jnp.zeros((1, 1, D_MODEL), F32)
    dz, dgate5_1 = _gate_bwd(dy2, f1, m1x[5], gx, B, "l1_ffn_gate_bwd")
    gw_ffn_out1 = _mm(a1, dz, "tn", F32, "l1_ffn_out_dw")
    du = _ffn_out_bwd_swiglu(dz, w["ffn_w_out"][1], ug1, uu1, "l1_ffn_out_dx")
    gw_ffn_in1 = _mm(k2, du, "tn", F32, "l1_ffn_in_dw")
    dk2 = _mm(du, w["ffn_w_in"][1], "nt", F32, "l1_ffn_in_dx")
    dy1, dsh3_1, dsc4_1, dn2_1, dzo, dgate2_1 = _norm_mod_bwd(dk2, y1, n2[1:2], m1x[4], dy2, gx, B, "l1_norm2_bwd",
                                                              gated=(mo1, m1x[2]))
    gw_ret_o = _mm(z1, dzo, "tn", F32, "l1_ret_out_dw")
    dz1 = _mm(dzo, w["ret_w_o"][0], "nt", F32, "l1_ret_out_dx")
    do_r, dg_r, dgn = _gated_out_bwd(dz1, of, ob, qkvg, gn, "l1_gated_out_bwd")
    ((dq_f, dk_f, dv_f, dkc_f, dvc_f, dlg_f),
     (dq_b, dk_b, dv_b, dkc_b, dvc_b, dlg_b)) = _ret_bwd(qk, qkvg, log_g, st_f, st_b, do_r, B, S, L, "l1_ret_bwd")
    dqkvg = _ret_grad_assemble((dq_f, dq_b, dk_f, dk_b, dv_f, dv_b), (dkc_f, dkc_b, dvc_f, dvc_b), dg_r, rcos, rsin, S,
                               "l1_qkvg_grad")
    gw_ret_qkvg = _mm(g1, dqkvg, "tn", F32, "l1_qkvg_dw")
    grads_layer1 = {
        "ffn_w_in": gw_ffn_in1[None],
        "ffn_w_out": gw_ffn_out1[None],
        "ret_w_qkvg": gw_ret_qkvg[None],
        "ret_gn_g": dgn.reshape(1, 1, RET_VWIDTH),
        "ret_w_o": gw_ret_o[None],
    }
    if hooks is not None:
        m0[5] = hooks.layer1_grads(grads_layer1, m0[5])
    dg1 = _mm(dqkvg, w["ret_w_qkvg"][0], "nt", F32, "l1_qkvg_dx")
    dx2, dsh0_1, dsc1_1, dn1_1, dz, dgate5_0 = _norm_mod_bwd(dg1, x2, n1[1:2], m1[1], dy1, gidx, G, "l1_norm1_bwd",
                                                             gated=(f0, m0[5]))
    dlg = jnp.stack([jnp.sum(dlg_f[:, :, 0, 0], axis=0), jnp.sum(dlg_b[:, :, 0, 0], axis=0)])
    d_decay = (dlg * jax.nn.sigmoid(-small["ret_decay_logit"].reshape(2, RET_HEADS))).reshape(1, 2, RET_HEADS)

    gw_ffn_out0 = _mm(a0, dz, "tn", F32, "l0_ffn_out_dw")
    du = _ffn_out_bwd_swiglu(dz, w["ffn_w_out"][0], ug0, uu0, "l0_ffn_out_dx")
    if hooks is not None:
        m0[4] = hooks.mid_ffn0_backward(du, m0[4])
    gw_ffn_in0 = _mm(h2, du, "tn", F32, "l0_ffn_in_dw")
    if hooks is not None:
        m0[2] = hooks.ffn0_grads({"ffn_w_in": gw_ffn_in0[None], "ffn_w_out": gw_ffn_out0[None]}, m0[2])
    dh2 = _mm(du, w["ffn_w_in"][0], "nt", F32, "l0_ffn_in_dx")
    dx1, dsh3_0, dsc4_0, dn2_0, dzo, dgate2_0 = _norm_mod_bwd(dh2, x1, n2[0:1], m0[4], dx2, gidx, G, "l0_norm2_bwd",
                                                              gated=(mo0, m0[2]))
    gw_attn_o = _mm(o0, dzo, "tn", F32, "l0_attn_out_dw")
    do0 = _mm(dzo, w["attn_w_o"][0], "nt", MXU_DTYPE, "l0_attn_out_dx")
    dq_x, dk_x, dv_x, dkc1, dvc1, dsink_x = _attn_bwd(qkv_r, sink, do0, B, S, L, False, "l0_attn_x_bwd")
    dq_c, dkc2, dvc2, dsink_c = _attn_bwd(qkv_r, sink, do0, B, S, L, True, "l0_attn_c_bwd")
    if hooks is not None:
        gains = hooks.after_attn_backward(dq_x, gains)
    dqkv, dgains = _attn_prep_bwd((dq_x, dk_x, dv_x), (dq_c, dkc1, dkc2, dvc1, dvc2), qkv, gains, acos, asin, tidx,
                                  "l0_qk_prep_bwd")
    gw_attn_qkv = _mm(h1, dqkv, "tn", F32, "l0_qkv_dw")
    dh1 = _mm(dqkv, w["attn_w_qkv"][0], "nt", F32, "l0_qkv_dx")
    dx0, dsh0_0, dsc1_0, dn1_0 = _norm_mod_bwd(dh1, x0, n1[0:1], m0[1], dx1, gidx, G, "l0_norm1_bwd")

    dgains = jnp.sum(dgains.reshape(ATTN_QK_BLOCKS, LANES // HEAD_DIM, HEAD_DIM), axis=1)
    dsink = (dsink_x + dsink_c).reshape(N_KV_HEADS, 8, LANES)[:, :GQA_GROUP, 0].reshape(1, N_HEADS)
    grads_layer0 = {
        "ffn_w_in": gw_ffn_in0[None],
        "ffn_w_out": gw_ffn_out0[None],
        "attn_w_qkv": gw_attn_qkv[None],
        "attn_w_o": gw_attn_o[None],
    }
    grads_small = {
        "norm1_g": jnp.concatenate([dn1_0, dn1_1], axis=0),
        "norm2_g": jnp.concatenate([dn2_0, dn2_1], axis=0),
        "attn_q_norm": jnp.sum(dgains[:ATTN_Q_BLOCKS], axis=0)[None],
        "attn_k_norm": jnp.sum(dgains[ATTN_Q_BLOCKS:ATTN_QK_BLOCKS], axis=0)[None],
        "attn_sink": dsink,
        "ret_decay_logit": d_decay,
    }

    def pad_g(t):
        return jnp.concatenate([t, zg], axis=0)

    d0 = jnp.concatenate([dsh0_0, dsc1_0, dgate2_0, dsh3_0, dsc4_0, dgate5_0], axis=2)[:, 0]
    d1 = jnp.concatenate([dsh0_1, dsc1_1, pad_g(dgate2_1), pad_g(dsh3_1), pad_g(dsc4_1), pad_g(dgate5_1)],
                         axis=2)[:, 0]
    dmod_x = jnp.stack([d0[:B], d1[:B]], axis=1)
    dmod_c = jnp.stack([d0[B], d1[B]], axis=0)
    return loss_tile, dx0[:NX].reshape(B, S, D_MODEL), (grads_layer0, grads_layer1), grads_small, dmod_x, dmod_c


SMALL_NAMES = ("c_ctx", "ada_b", "norm1_g", "norm2_g", "attn_q_norm", "attn_k_norm", "attn_sink", "ret_decay_logit")
ADA_ROWS = 64


def _pack_small(d, rows):
    flat = jnp.concatenate([d[k].reshape(-1) for k in SMALL_NAMES])
    n = rows * LANES
    return jnp.pad(flat, (0, n - flat.shape[0])).reshape(rows, LANES)


def _unpack_small(packed, shapes):
    flat = packed.reshape(-1)
    out, off = {}, 0
    for k in SMALL_NAMES:
        n = math.prod(shapes[k])
        out[k] = flat[off:off + n].reshape(shapes[k])
        off += n
    return out


EARLY_WEIGHTS = ("attn_w_qkv", "attn_w_o")
LATE_WEIGHTS = tuple(k for k in BIG_WEIGHTS if k not in EARLY_WEIGHTS)

_HBM = pl.BlockSpec(memory_space=pltpu.HBM)
_SEM = pl.BlockSpec(memory_space=pltpu.SEMAPHORE)
_DATAFLOW = pltpu.SideEffectType.DATAFLOW_SIDE_EFFECTING
_PEER_FLIPS = ((0, 0, 1), (0, 1, 0), (0, 1, 1), (1, 0, 0), (1, 0, 1), (1, 1, 0), (1, 1, 1))


def _wire_shard(name, t):
    return t.reshape(1, 1, -1) if name == "ret_gn_g" else t.astype(MXU_DTYPE)


def _direct_copies(x_refs, land_refs, send_sems, recv_sems, landing):
    x, y, c = _my_coords()
    out = []
    for a in range(len(x_refs)):
        for k, (bx, by, bc) in enumerate(_PEER_FLIPS):
            peer = (_flip(x, bx), _flip(y, by), _flip(c, bc))
            slot = (4 * peer[0] + 2 * peer[1] + peer[2]) if landing else (4 * x + 2 * y + c)
            out.append(pltpu.make_async_remote_copy(
                src_ref=x_refs[a], dst_ref=land_refs[a].at[:, slot], send_sem=send_sems.at[7 * a + k],
                recv_sem=recv_sems.at[7 * a + k], device_id=peer, device_id_type=MESH))
    return out


def _gather_start(shards, name):
    n = len(shards)
    lands = [lax.empty((s.shape[0], N_DEV) + s.shape[1:], s.dtype) for s in shards]

    def body(*refs):
        send_sems, recv_sems = refs[2 * n], refs[2 * n + 1]
        x_refs, land_refs = refs[2 * n + 2:3 * n + 2], refs[3 * n + 2:4 * n + 2]
        for cp in _direct_copies(x_refs, land_refs, send_sems, recv_sems, landing=False):
            cp.start()
        refs[-1][...] = jnp.zeros_like(refs[-1])

    hbm = lambda t: pltpu.with_memory_space_constraint(t, pltpu.HBM)
    res = pl.pallas_call(
        body, name=name,
        out_shape=(pltpu.SemaphoreType.DMA((7 * n,)), pltpu.SemaphoreType.DMA((7 * n,)))
        + tuple(pltpu.HBM(t.shape, t.dtype) for t in shards + lands) + (jax.ShapeDtypeStruct((8, LANES), F32),),
        in_specs=[_HBM] * (2 * n), out_specs=(_SEM, _SEM) + (_HBM,) * (2 * n) + (pl.BlockSpec(memory_space=pltpu.VMEM),),
        input_output_aliases={i: 2 + i for i in range(2 * n)},
        compiler_params=pltpu.CompilerParams(has_side_effects=_DATAFLOW))(*[hbm(t) for t in shards + lands])
    return res[0], res[1], list(res[2:2 + n]), list(res[2 + n:2 + 2 * n]), res[-1]


def _gather_wait(send_sems, recv_sems, shards, lands, after, name):
    n = len(shards)

    def body(*refs):
        x_refs, land_refs = refs[:n], refs[n:2 * n]
        for cp in _direct_copies(x_refs, land_refs, refs[2 * n], refs[2 * n + 1], landing=True):
            cp.wait_send()
            cp.wait_recv()

    res = pl.pallas_call(
        body, name=name, out_shape=tuple(pltpu.HBM(t.shape, t.dtype) for t in shards + lands),
        in_specs=[_HBM] * (2 * n) + [_SEM, _SEM, _ANY], out_specs=(_HBM,) * (2 * n),
        input_output_aliases={i: i for i in range(2 * n)},
        compiler_params=pltpu.CompilerParams(has_side_effects=_DATAFLOW))(*shards, *lands, send_sems, recv_sems, after)
    return list(res[n:])


def _gather_big_weights(weights, names, name):
    gathered = _gather_shards([_wire_shard(k, weights[k]) for k in names], name)
    return {k: _join_shards(k, g) for k, g in zip(names, gathered)}


_SIBLING = ((0, 0, 1),)
_CHIPS = ((1, 0, 0), (0, 1, 0), (1, 1, 0))
_to_sibling = lambda ref, peer: ref.at[:, :, peer[2]]
_to_chip = lambda ref, peer: ref.at[:, 2 * peer[0] + peer[1]]
_sibling_tail = lambda s: (s.shape[0], 4) + s.shape[3:]
_chip_tail = lambda s: (s.shape[0],) + s.shape[2:]


def _rs_split(grads):
    names = list(grads)
    split = []
    for k in names:
        s = _split_shards(k, grads[k])
        split.append(s.reshape(s.shape[0], 4, 2, s.shape[2], s.shape[3]))
    return names, split


def _rs_pair_sums(names, split, from_sibling, tag):
    core = lax.axis_index("c").astype(jnp.int32).reshape(1)
    return [_pair_sum(g, s, core, MXU_DTYPE, tag + k) for k, g, s in zip(names, split, from_sibling)]


def _rs_parts(names, split, from_sibling, from_chips):
    mx_, my_, mc_ = _my_coords()
    my_chip = 2 * mx_ + my_
    parts = {}
    for k, g, s, r in zip(names, split, from_sibling, from_chips):
        own_keep = lax.dynamic_index_in_dim(lax.dynamic_index_in_dim(g, my_chip, axis=1, keepdims=False), mc_, axis=1,
                                            keepdims=False)
        parts[k] = (own_keep, lax.dynamic_index_in_dim(s, my_chip, axis=1, keepdims=False), r)
    return parts


def _reduce_scatter_in_call(grads, tag):
    names, split = _rs_split(grads)
    from_sibling = [t[0] for t in _exchange_shards(split, _SIBLING, lambda ref, peer, me_: _to_sibling(ref, peer),
                                                   _sibling_tail, tag + "sibling")]
    pair = _rs_pair_sums(names, split, from_sibling, tag + "pair_")
    from_chips = _exchange_shards(pair, _CHIPS, lambda ref, peer, me_: _to_chip(ref, peer), _chip_tail, tag + "chips")
    return _rs_parts(names, split, from_sibling, from_chips)


def _exchange_copies(in_refs, land_refs, send_sems, recv_sems, masks, src_of):
    x, y, c = _my_coords()
    nm = len(masks)
    out = []
    for a in range(len(in_refs)):
        for k, (bx, by, bc) in enumerate(masks):
            peer = (_flip(x, bx), _flip(y, by), _flip(c, bc))
            out.append(pltpu.make_async_remote_copy(
                src_ref=src_of(in_refs[a], peer), dst_ref=land_refs[a].at[k], send_sem=send_sems.at[nm * a + k],
                recv_sem=recv_sems.at[nm * a + k], device_id=peer, device_id_type=MESH))
    return out


def _exchange_start(arrs, masks, src_of, out_tail, name):
    n, nm = len(arrs), len(masks)
    lands = [lax.empty((nm,) + out_tail(s), s.dtype) for s in arrs]

    def body(*refs):
        send_sems, recv_sems = refs[2 * n], refs[2 * n + 1]
        in_refs, land_refs = refs[2 * n + 2:3 * n + 2], refs[3 * n + 2:4 * n + 2]
        for cp in _exchange_copies(in_refs, land_refs, send_sems, recv_sems, masks, src_of):
            cp.start()
        refs[-1][...] = jnp.zeros_like(refs[-1])

    hbm = lambda t: pltpu.with_memory_space_constraint(t, pltpu.HBM)
    res = pl.pallas_call(
        body, name=name,
        out_shape=(pltpu.SemaphoreType.DMA((nm * n,)), pltpu.SemaphoreType.DMA((nm * n,)))
        + tuple(pltpu.HBM(t.shape, t.dtype) for t in list(arrs) + lands) + (jax.ShapeDtypeStruct((8, LANES), F32),),
        in_specs=[_HBM] * (2 * n), out_specs=(_SEM, _SEM) + (_HBM,) * (2 * n) + (pl.BlockSpec(memory_space=pltpu.VMEM),),
        input_output_aliases={i: 2 + i for i in range(2 * n)},
        compiler_params=pltpu.CompilerParams(has_side_effects=_DATAFLOW))(*[hbm(t) for t in list(arrs) + lands])
    return (res[0], res[1], list(res[2:2 + n]), list(res[2 + n:2 + 2 * n]), masks, src_of), res[-1]


def _exchange_wait(state, after, name):
    send_sems, recv_sems, arrs, lands, masks, src_of = state
    n = len(arrs)

    def body(*refs):
        for cp in _exchange_copies(refs[:n], refs[n:2 * n], refs[2 * n], refs[2 * n + 1], masks, src_of):
            cp.wait_send()
            cp.wait_recv()

    res = pl.pallas_call(
        body, name=name, out_shape=tuple(pltpu.HBM(t.shape, t.dtype) for t in arrs + lands),
        in_specs=[_HBM] * (2 * n) + [_SEM, _SEM, _ANY], out_specs=(_HBM,) * (2 * n),
        input_output_aliases={i: i for i in range(2 * n)},
        compiler_params=pltpu.CompilerParams(has_side_effects=_DATAFLOW))(*arrs, *lands, send_sems, recv_sems, after)
    return list(res[:n]), list(res[n:])


class _SplitReduce:
    def __init__(self, tag):
        self.tag = tag

    def start(self, grads, order_through):
        self.names, split = _rs_split(grads)
        self.sibling, tok = _exchange_start(split, _SIBLING, _to_sibling, _sibling_tail, self.tag + "sibling_start")
        return order_through + tok[0, 0]

    def middle(self, after, order_through):
        self.split, lands = _exchange_wait(self.sibling, after, self.tag + "sibling_wait")
        self.from_sibling = [t[0] for t in lands]
        pair = _rs_pair_sums(self.names, self.split, self.from_sibling, self.tag + "pair_")
        self.chips, tok = _exchange_start(pair, _CHIPS, _to_chip, _chip_tail, self.tag + "chips_start")
        return order_through + tok[0, 0]

    def finish(self, after):
        _, from_chips = _exchange_wait(self.chips, after, self.tag + "chips_wait")
        return _rs_parts(self.names, self.split, self.from_sibling, from_chips)


def _adamw_big(weights, mom1, mom2, part_groups):
    big = {}
    for k in BIG_WEIGHTS:
        parts = [g[k] for g in part_groups if k in g]
        own_keep = jnp.concatenate([p[0] for p in parts], axis=0)
        own_sib = jnp.concatenate([p[1] for p in parts], axis=0)
        recv = jnp.concatenate([p[2] for p in parts], axis=1)
        L_, a_, b_ = own_keep.shape
        rows = L_ * a_
        res = _adamw(weights[k].reshape(rows, b_), mom1[k].reshape(rows, b_), mom2[k].reshape(rows, b_),
                     [own_keep.reshape(1, rows, b_), own_sib.reshape(1, rows, b_), recv.reshape(3, rows, b_)],
                     "adamw_" + k)
        big[k] = [t.reshape(weights[k].shape) for t in res]
    return big


def kernel(x, c, ctx, c_ctx, ada_w, ada_b, norm1_g, norm2_g, ffn_w_in, ffn_w_out, attn_w_qkv, attn_q_norm, attn_k_norm, attn_sink, attn_w_o, ret_w_qkvg, ret_decay_logit, ret_gn_g, ret_w_o, loss_target, m_c_ctx, m_ada_w, m_ada_b, m_norm1_g, m_norm2_g, m_ffn_w_in, m_ffn_w_out, m_attn_w_qkv, m_attn_q_norm, m_attn_k_norm, m_attn_sink, m_attn_w_o, m_ret_w_qkvg, m_ret_decay_logit, m_ret_gn_g, m_ret_w_o, v_c_ctx, v_ada_w, v_ada_b, v_norm1_g, v_norm2_g, v_ffn_w_in, v_ffn_w_out, v_attn_w_qkv, v_attn_q_norm, v_attn_k_norm, v_attn_sink, v_attn_w_o, v_ret_w_qkvg, v_ret_decay_logit, v_ret_gn_g, v_ret_w_o):
    weights = dict(c_ctx=c_ctx, ada_w=ada_w, ada_b=ada_b, norm1_g=norm1_g, norm2_g=norm2_g, ffn_w_in=ffn_w_in,
                   ffn_w_out=ffn_w_out, attn_w_qkv=attn_w_qkv, attn_q_norm=attn_q_norm, attn_k_norm=attn_k_norm,
                   attn_sink=attn_sink, attn_w_o=attn_w_o, ret_w_qkvg=ret_w_qkvg, ret_decay_logit=ret_decay_logit,
                   ret_gn_g=ret_gn_g, ret_w_o=ret_w_o)
    mom1 = dict(c_ctx=m_c_ctx, ada_w=m_ada_w, ada_b=m_ada_b, norm1_g=m_norm1_g, norm2_g=m_norm2_g, ffn_w_in=m_ffn_w_in,
                ffn_w_out=m_ffn_w_out, attn_w_qkv=m_attn_w_qkv, attn_q_norm=m_attn_q_norm, attn_k_norm=m_attn_k_norm,
                attn_sink=m_attn_sink, attn_w_o=m_attn_w_o, ret_w_qkvg=m_ret_w_qkvg, ret_decay_logit=m_ret_decay_logit,
                ret_gn_g=m_ret_gn_g, ret_w_o=m_ret_w_o)
    mom2 = dict(c_ctx=v_c_ctx, ada_w=v_ada_w, ada_b=v_ada_b, norm1_g=v_norm1_g, norm2_g=v_norm2_g, ffn_w_in=v_ffn_w_in,
                ffn_w_out=v_ffn_w_out, attn_w_qkv=v_attn_w_qkv, attn_q_norm=v_attn_q_norm, attn_k_norm=v_attn_k_norm,
                attn_sink=v_attn_sink, attn_w_o=v_attn_w_o, ret_w_qkvg=v_ret_w_qkvg, ret_decay_logit=v_ret_decay_logit,
                ret_gn_g=v_ret_gn_g, ret_w_o=v_ret_w_o)
    B = x.shape[0]
    mx_, my_, mc_ = _my_coords()
    me = 4 * mx_ + 2 * my_ + mc_
    ada_cols = ada_w.shape[2]

    w_full = _gather_big_weights(weights, EARLY_WEIGHTS, "gather_early")

    c_all = _all_gather(jax.nn.silu(c), "gather_c").reshape(N_DEV * B, D_MODEL)
    cc_act = jax.nn.silu(c_ctx)[None]
    ada_in = jnp.concatenate([c_all, cc_act, jnp.zeros((ADA_ROWS - N_DEV * B - 1, D_MODEL), F32)], axis=0)
    ada_in = ada_in.astype(MXU_DTYPE)
    ada_w2 = jnp.concatenate([ada_w[0], ada_w[1]], axis=1)
    bias = lax.dynamic_slice_in_dim(ada_b.reshape(2, N_DEV, ada_cols), me, 1, axis=1).reshape(1, 2 * ada_cols)
    mod_cols = _mm(ada_in, ada_w2, "nn", F32, "ada_fwd", bias=bias)
    mod_all = _all_gather(mod_cols, "gather_mod")
    mod_all = mod_all.reshape(N_DEV, ADA_ROWS, 2, ada_cols).transpose(1, 2, 0, 3).reshape(ADA_ROWS, 2, N_DEV * ada_cols)
    mod_x = lax.dynamic_slice_in_dim(mod_all, me * B, B, axis=0)
    mod_c = mod_all[N_DEV * B]

    order = 0.0 * (mod_c[0, 0] + w_full["attn_w_o"][0, 0, 0].astype(F32))
    late_shards = [_wire_shard(k, weights[k] + order if k == "ret_gn_g" else weights[k]) for k in LATE_WEIGHTS]
    send_sems, recv_sems, late_thru, late_lands, token = _gather_start(late_shards, "gather_late_start")
    mod_x = mod_x + token[0, 0]

    def late_weights(after):
        lands = _gather_wait(send_sems, recv_sems, late_thru, late_lands, after, "gather_late_wait")
        own = [lax.dynamic_update_index_in_dim(land, shard, me, axis=1) for land, shard in zip(lands, late_shards)]
        return {k: _join_shards(k, g) for k, g in zip(LATE_WEIGHTS, own)}

    rs_layer1, rs_ffn0 = _SplitReduce("rs1_"), _SplitReduce("rs0_")

    class Hooks:
        layer1_grads = rs_layer1.start
        mid_ffn0_backward = rs_layer1.middle
        ffn0_grads = rs_ffn0.start
        after_attn_backward = rs_ffn0.middle

    small = {k: weights[k] for k in SMALL_NAMES}
    loss_tile, grad_x, (g_layer0, _), g_small, dmod_x, dmod_c = _local_step(
        x, ctx, loss_target, mod_x, mod_c, w_full, small, late_weights, Hooks)
    parts1 = rs_layer1.finish(grad_x)
    parts0_ffn = rs_ffn0.finish(grad_x)
    loss = lax.psum(loss_tile[0, 0], ("x", "y", "c"))

    n_mod = 2 * 6 * D_MODEL
    dm_rows = jnp.concatenate([dmod_x.reshape(B, n_mod), dmod_c.reshape(1, n_mod),
                               jnp.zeros((8 - B - 1, n_mod), F32)], axis=0)
    dm_all = _all_gather(dm_rows, "gather_dmod")
    dmc_tot = _sum_rows(dm_all[:, B:B + 1].reshape(N_DEV, 1, n_mod)[:, :, :].reshape(N_DEV, n_mod // LANES, LANES),
                        "sum_dmod_c").reshape(1, n_mod)
    dmod_rows = jnp.concatenate([dm_all[:, :B].reshape(N_DEV * B, n_mod), dmc_tot,
                                 jnp.zeros((ADA_ROWS - N_DEV * B - 1, n_mod), F32)], axis=0)
    dmod_mine = lax.dynamic_slice_in_dim(dmod_rows.reshape(ADA_ROWS, 2, N_DEV, ada_cols), me, 1, axis=2)
    dmod_mine = dmod_mine.reshape(ADA_ROWS, 2 * ada_cols).astype(MXU_DTYPE)
    g_ada2 = _mm(ada_in, dmod_mine, "tn", F32, "ada_dw")
    g_ada_w = jnp.stack([g_ada2[:, :ada_cols], g_ada2[:, ada_cols:]])
    dmc_mine = jnp.concatenate([dmod_mine[N_DEV * B:N_DEV * B + 1], jnp.zeros((7, 2 * ada_cols), MXU_DTYPE)], axis=0)
    dcc_part = _mm(dmc_mine, ada_w2, "nt", F32, "ada_dc")[0:1]
    g_ada_b = _sum_rows(dmod_rows[:, None, :].reshape(ADA_ROWS, n_mod // LANES, LANES), "sum_dmod_b").reshape(2, 6 * D_MODEL)
    sg = jax.nn.sigmoid(c_ctx)
    g_small["c_ctx"] = dcc_part.reshape(D_MODEL) * (sg * (1.0 + c_ctx * (1.0 - sg)))
    g_small["ada_b"] = g_ada_b * (1.0 / N_DEV)

    shapes = {k: weights[k].shape for k in SMALL_NAMES}
    n_small = sum(math.prod(s) for s in shapes.values())
    srows = -(-(-(-n_small // LANES)) // 8) * 8
    gs_all = _all_gather(_pack_small(g_small, srows), "gather_small_grads")
    sm = _adamw(_pack_small({k: weights[k] for k in SMALL_NAMES}, srows), _pack_small({k: mom1[k] for k in SMALL_NAMES}, srows),
                _pack_small({k: mom2[k] for k in SMALL_NAMES}, srows), [gs_all], "adamw_small")
    sm = [_unpack_small(t, shapes) for t in sm]

    ada_shape = ada_w.shape
    r2 = lambda t: t.reshape(ada_shape[0] * ada_shape[1], ada_shape[2])
    ada = [t.reshape(ada_shape) for t in _adamw(r2(ada_w), r2(m_ada_w), r2(v_ada_w), [r2(g_ada_w)[None]], "adamw_ada")]

    attn_grads = {k: g_layer0[k] for k in EARLY_WEIGHTS}
    big = _adamw_big(weights, mom1, mom2, [_reduce_scatter_in_call(attn_grads, "rs_"), parts0_ffn, parts1])

    def pick(i, name):
        if name in BIG_WEIGHTS:
            return big[name][i]
        if name == "ada_w":
            return ada[i]
        return sm[i][name]

    order = ("c_ctx", "ada_w", "ada_b", "norm1_g", "norm2_g", "ffn_w_in", "ffn_w_out", "attn_w_qkv", "attn_q_norm",
             "attn_k_norm", "attn_sink", "attn_w_o", "ret_w_qkvg", "ret_decay_logit", "ret_gn_g", "ret_w_o")
    outs = [loss, grad_x]
    for i in range(4):
        outs += [pick(i, n) for n in order]
    return tuple(outs)
```

```python
import functools
import math

import jax
import jax.numpy as jnp
from jax import lax
from jax.experimental import pallas as pl
from jax.experimental.pallas import tpu as pltpu

F32 = jnp.float32
MXU_DTYPE = jnp.bfloat16

D_MODEL = 1024
HEAD_DIM = 64
N_HEADS = 16
N_KV_HEADS = 4
GQA_GROUP = 4
WINDOW = 128
ATTN_BLOCK = 128
RET_HEADS = 4
RET_QK_DIM = 256
RET_V_DIM = 512
RET_VWIDTH = 2048
RET_CHUNK = 512
D_FF = 2816
GRID_W = 64
ROPE_BASE = 10000.0
EPS = 1e-6
NEG_INF = -1e30

ADAM_LR = 0.001
ADAM_B1 = 0.9
ADAM_B2 = 0.999
ADAM_EPS = 1e-08
ADAM_WD = 0.01
ADAM_STEP = 10

N_DEV = 8
LANES = 128
ROW_TILE = 512
VMEM_LIMIT = 48 * 1024 * 1024

MESH = pl.DeviceIdType.MESH
_ANY = pl.BlockSpec(memory_space=pl.ANY)
_SMEM = pl.BlockSpec(memory_space=pltpu.SMEM)


def _params(**kw):
    return pltpu.CompilerParams(vmem_limit_bytes=VMEM_LIMIT, **kw)


def _mx(v):
    return v.astype(MXU_DTYPE)


def _dot(a, b, dims):
    return lax.dot_general(_mx(a), _mx(b), (dims, ((), ())), preferred_element_type=F32)


_NN = ((1,), (0,))
_NT = ((1,), (1,))
_TN = ((0,), (0,))


def _tile(n, cands):
    for c in cands:
        if n % c == 0:
            return c
    return n


def _big_tile(n, cap):
    if n <= cap:
        return n
    for t in range(cap - cap % LANES, 0, -LANES):
        if n % t == 0:
            return t
    return n


MM_ROWS = 1024
MM_COLS = 1408
MM_DEPTH = 2048


def _k_tile(k):
    return _big_tile(k, MM_DEPTH)


def _mm(a, b, mode, out_dtype, name, *, bias=None, res=None, gate=None, gidx_for=None, gate_rows=None):
    if mode == "nn":
        (M, K), (_, N) = a.shape, b.shape
    elif mode == "nt":
        (M, K), (N, _) = a.shape, b.shape
    else:
        (K, M), (_, N) = a.shape, b.shape
    if res is not None:
        tm, tn = gate_rows, _big_tile(N, 512)
        gidx = gidx_for(tm)
    else:
        tm = _big_tile(M, MM_COLS if mode == "tn" else MM_ROWS)
        tn = _big_tile(N, MM_COLS)
    tk = _k_tile(K)
    nk = K // tk
    dims = {"nn": _NN, "nt": _NT, "tn": _TN}[mode]
    a_spec = (pl.BlockSpec((tk, tm), lambda i, j, k: (k, i)) if mode == "tn"
              else pl.BlockSpec((tm, tk), lambda i, j, k: (i, k)))
    b_spec = (pl.BlockSpec((tn, tk), lambda i, j, k: (j, k)) if mode == "nt"
              else pl.BlockSpec((tk, tn), lambda i, j, k: (k, j)))
    o_spec = pl.BlockSpec((tm, tn), lambda i, j, k: (i, j))
    in_specs, operands = [a_spec, b_spec], [a, b]
    if bias is not None:
        in_specs.append(pl.BlockSpec((1, tn), lambda i, j, k: (0, j)))
        operands.append(bias)
    if res is not None:
        in_specs += [o_spec, pl.BlockSpec((1, 1, tn), lambda i, j, k: (gidx(i), 0, j))]
        operands += [res, gate]
        out_shape = (jax.ShapeDtypeStruct((M, N), F32), jax.ShapeDtypeStruct((M, N), F32))
        out_specs = (o_spec, o_spec)
    else:
        out_shape = jax.ShapeDtypeStruct((M, N), out_dtype)
        out_specs = o_spec

    def body(*refs):
        a_ref, b_ref = refs[0], refs[1]
        extra = refs[2:len(operands)]
        outs = refs[len(operands):]
        prod = _dot(a_ref[...], b_ref[...], dims)

        def finish(acc):
            if bias is not None:
                outs[0][...] = (acc + extra[0][...]).astype(out_dtype)
            elif res is not None:
                outs[0][...] = acc
                outs[1][...] = extra[0][...] + extra[1][0] * acc
            else:
                outs[0][...] = acc.astype(out_dtype)

        if nk == 1:
            finish(prod)
        else:
            acc_ref = outs[-1]
            outs = outs[:-1]
            k = pl.program_id(2)

            @pl.when(k == 0)
            def _():
                acc_ref[...] = prod

            @pl.when(k > 0)
            def _():
                acc_ref[...] += prod

            @pl.when(k == nk - 1)
            def _():
                finish(acc_ref[...])

    return pl.pallas_call(
        body, name=name, grid=(M // tm, N // tn, nk), in_specs=in_specs, out_specs=out_specs, out_shape=out_shape,
        scratch_shapes=[pltpu.VMEM((tm, tn), F32)] if nk > 1 else [],
        compiler_params=_params())(*operands)


def _group_index(n_x_tiles, tiles_per_example, n_examples):
    def gidx(i):
        return jnp.where(i < n_x_tiles, i // tiles_per_example, n_examples)
    return gidx


def _norm_mod_fwd(x, g, shift, scale, gidx, name):
    T, Dm = x.shape

    def body(x_ref, g_ref, sh_ref, sc_ref, h_ref):
        xv = x_ref[...]
        r = lax.rsqrt(jnp.mean(xv * xv, axis=-1, keepdims=True) + EPS)
        y = xv * r * g_ref[...]
        h_ref[...] = (y * (1.0 + sc_ref[0]) + sh_ref[0]).astype(h_ref.dtype)

    row = pl.BlockSpec((ROW_TILE, Dm), lambda i: (i, 0))
    mod = pl.BlockSpec((1, 1, Dm), lambda i: (gidx(i), 0, 0))
    return pl.pallas_call(
        body, name=name, grid=(T // ROW_TILE,),
        in_specs=[row, pl.BlockSpec((1, Dm), lambda i: (0, 0)), mod, mod],
        out_specs=row, out_shape=jax.ShapeDtypeStruct((T, Dm), MXU_DTYPE),
        compiler_params=_params())(x, g, shift, scale)


def _first_of_group(i, gidx):
    return jnp.logical_or(i == 0, gidx(i) != gidx(jnp.maximum(i - 1, 0)))


def _norm_mod_bwd(dh, x, g, scale, dres, gidx, n_groups, name, gated=None):
    T, Dm = x.shape
    res_tiles = dres.shape[0] // ROW_TILE

    def body(*refs):
        dh_ref, x_ref, g_ref, sc_ref, dres_ref = refs[:5]
        n_in = 7 if gated else 5
        dx_ref, dsh_ref, dsc_ref, dg_ref = refs[n_in:n_in + 4]
        i = pl.program_id(0)
        xv, dhv = x_ref[...], dh_ref[...]
        r = lax.rsqrt(jnp.mean(xv * xv, axis=-1, keepdims=True) + EPS)
        xn = xv * r
        y = xn * g_ref[...]

        @pl.when(_first_of_group(i, gidx))
        def _():
            dsh_ref[...] = jnp.zeros_like(dsh_ref)
            dsc_ref[...] = jnp.zeros_like(dsc_ref)

        @pl.when(i == 0)
        def _():
            dg_ref[...] = jnp.zeros_like(dg_ref)

        dsh_ref[0] += jnp.sum(dhv, axis=0, keepdims=True)
        dsc_ref[0] += jnp.sum(dhv * y, axis=0, keepdims=True)
        dy = dhv * (1.0 + sc_ref[0])
        dg_ref[...] += jnp.sum(dy * xn, axis=0, keepdims=True)
        dxn = dy * g_ref[...]
        dx = r * (dxn - xn * jnp.mean(dxn * xn, axis=-1, keepdims=True))
        dx = dx + (dres_ref[...] if res_tiles == T // ROW_TILE else jnp.where(i < res_tiles, dres_ref[...], 0.0))
        dx_ref[...] = dx
        if gated:
            f_ref, gate_ref = refs[5:7]
            dz_ref, dgate_ref = refs[n_in + 4:]

            @pl.when(_first_of_group(i, gidx))
            def _():
                dgate_ref[...] = jnp.zeros_like(dgate_ref)

            dgate_ref[0] += jnp.sum(dx * f_ref[...], axis=0, keepdims=True)
            dz_ref[...] = (dx * gate_ref[0]).astype(dz_ref.dtype)

    row = pl.BlockSpec((ROW_TILE, Dm), lambda i: (i, 0))
    mod = pl.BlockSpec((1, 1, Dm), lambda i: (gidx(i), 0, 0))
    vec = pl.BlockSpec((1, Dm), lambda i: (0, 0))
    mod_shape = jax.ShapeDtypeStruct((n_groups, 1, Dm), F32)
    res_row = pl.BlockSpec((ROW_TILE, Dm), lambda i: (jnp.minimum(i, res_tiles - 1), 0))
    in_specs, operands = [row, row, vec, mod, res_row], [dh, x, g, scale, dres]
    out_specs = [row, mod, mod, vec]
    out_shape = [jax.ShapeDtypeStruct((T, Dm), F32), mod_shape, mod_shape, jax.ShapeDtypeStruct((1, Dm), F32)]
    if gated:
        in_specs, operands = in_specs + [row, mod], operands + list(gated)
        out_specs, out_shape = out_specs + [row, mod], out_shape + [jax.ShapeDtypeStruct((T, Dm), MXU_DTYPE), mod_shape]
    return pl.pallas_call(
        body, name=name, grid=(T // ROW_TILE,), in_specs=in_specs, out_specs=tuple(out_specs),
        out_shape=tuple(out_shape), compiler_params=_params())(*operands)


def _gate_bwd(dy, f, gate, gidx, n_groups, name):
    T, Dm = dy.shape

    def body(dy_ref, f_ref, gate_ref, dz_ref, dgate_ref):
        i = pl.program_id(0)
        dyv = dy_ref[...]

        @pl.when(_first_of_group(i, gidx))
        def _():
            dgate_ref[...] = jnp.zeros_like(dgate_ref)

        dgate_ref[0] += jnp.sum(dyv * f_ref[...], axis=0, keepdims=True)
        dz_ref[...] = (dyv * gate_ref[0]).astype(dz_ref.dtype)

    row = pl.BlockSpec((ROW_TILE, Dm), lambda i: (i, 0))
    mod = pl.BlockSpec((1, 1, Dm), lambda i: (gidx(i), 0, 0))
    return pl.pallas_call(
        body, name=name, grid=(T // ROW_TILE,), in_specs=[row, row, mod], out_specs=(row, mod),
        out_shape=(jax.ShapeDtypeStruct((T, Dm), MXU_DTYPE), jax.ShapeDtypeStruct((n_groups, 1, Dm), F32)),
        compiler_params=_params())(dy, f, gate)


FFN_IN_ROWS = 512
FFN_IN_COLS = 1408
FFN_BWD_ROWS = 256


def _ffn_in_swiglu(h, w_in, name):
    T, Dm = h.shape
    nj = D_FF // FFN_IN_COLS

    def body(h_ref, wg_ref, wu_ref, g_ref, u_ref, a_ref):
        hv = h_ref[...]
        gate = _dot(hv, wg_ref[...], _NN)
        up = _dot(hv, wu_ref[...], _NN)
        g_ref[...] = gate
        u_ref[...] = up
        a_ref[...] = (gate * jax.nn.sigmoid(gate) * up).astype(a_ref.dtype)

    out = pl.BlockSpec((FFN_IN_ROWS, FFN_IN_COLS), lambda i, j: (i, j))
    pre = jax.ShapeDtypeStruct((T, D_FF), F32)
    return pl.pallas_call(
        body, name=name, grid=(T // FFN_IN_ROWS, nj),
        in_specs=[pl.BlockSpec((FFN_IN_ROWS, Dm), lambda i, j: (i, 0)),
                  pl.BlockSpec((Dm, FFN_IN_COLS), lambda i, j: (0, j)),
                  pl.BlockSpec((Dm, FFN_IN_COLS), lambda i, j: (0, nj + j))],
        out_specs=(out, out, out), out_shape=(pre, pre, jax.ShapeDtypeStruct((T, D_FF), MXU_DTYPE)),
        compiler_params=_params())(h, w_in, w_in)


def _ffn_out_bwd_swiglu(dz, w_out, gate, up, name):
    T, Dm = dz.shape

    def body(dz_ref, w_ref, g_ref, u_ref, du_ref):
        da = _dot(dz_ref[...], w_ref[...], _NT)
        gv, uv = g_ref[...], u_ref[...]
        sg = jax.nn.sigmoid(gv)
        du_ref[:, :D_FF] = (da * uv * (sg * (1.0 + gv * (1.0 - sg)))).astype(du_ref.dtype)
        du_ref[:, D_FF:] = (da * gv * sg).astype(du_ref.dtype)

    half = pl.BlockSpec((FFN_BWD_ROWS, D_FF), lambda i: (i, 0))
    return pl.pallas_call(
        body, name=name, grid=(T // FFN_BWD_ROWS,),
        in_specs=[pl.BlockSpec((FFN_BWD_ROWS, Dm), lambda i: (i, 0)), pl.BlockSpec((D_FF, Dm), lambda i: (0, 0)), half, half],
        out_specs=pl.BlockSpec((FFN_BWD_ROWS, 2 * D_FF), lambda i: (i, 0)),
        out_shape=jax.ShapeDtypeStruct((T, 2 * D_FF), MXU_DTYPE), compiler_params=_params())(dz, w_out, gate, up)


def _loss_fwd_bwd(y, target, name):
    T, Dm = y.shape

    def body(y_ref, t_ref, loss_ref, dy_ref):
        err = y_ref[...] - t_ref[...]

        @pl.when(pl.program_id(0) == 0)
        def _():
            loss_ref[...] = jnp.zeros_like(loss_ref)

        loss_ref[...] += 0.5 * jnp.sum(jnp.mean(err * err, axis=-1, keepdims=True))
        dy_ref[...] = err * (1.0 / Dm)

    row = pl.BlockSpec((ROW_TILE, Dm), lambda i: (i, 0))
    return pl.pallas_call(
        body, name=name, grid=(T // ROW_TILE,), in_specs=[row, row],
        out_specs=(pl.BlockSpec((8, LANES), lambda i: (0, 0)), row),
        out_shape=(jax.ShapeDtypeStruct((8, LANES), F32), jax.ShapeDtypeStruct((T, Dm), F32)),
        compiler_params=_params())(y, target)


def _rope_tables(seq, head_dim):
    axis_dim = head_dim // 2
    half = axis_dim // 2
    pos = jnp.arange(seq, dtype=jnp.int32)
    row = (pos // GRID_W).astype(F32)[:, None]
    col = (pos % GRID_W).astype(F32)[:, None]
    inv = ROPE_BASE ** (-jnp.arange(0, axis_dim, 2, dtype=F32) / axis_dim)
    lane = jnp.arange(head_dim, dtype=jnp.int32)
    within = lane % axis_dim
    ang = jnp.where((lane // axis_dim == 0)[None, :], row, col) * inv[within % half][None, :]
    cos = jnp.cos(ang)
    sin = jnp.where((within < half)[None, :], -jnp.sin(ang), jnp.sin(ang))
    cos = jnp.concatenate([cos, jnp.ones((ROW_TILE, head_dim), F32)], axis=0)
    sin = jnp.concatenate([sin, jnp.zeros((ROW_TILE, head_dim), F32)], axis=0)
    return cos, sin


def _pair_swap(v, half):
    if 2 * half == LANES:
        return pltpu.roll(v, half, axis=1)
    lane = lax.broadcasted_iota(jnp.int32, v.shape, 1)
    return jnp.where((lane % (2 * half)) < half, pltpu.roll(v, LANES - half, axis=1), pltpu.roll(v, half, axis=1))


def _head_sum(v, ones_ref):
    hi = v.astype(MXU_DTYPE)
    lo = (v - hi.astype(F32)).astype(MXU_DTYPE)
    return (jnp.dot(hi, ones_ref[...], preferred_element_type=F32)
            + jnp.dot(lo, ones_ref[...], preferred_element_type=F32))


def _head_ones():
    lane = jnp.arange(LANES)
    return (lane[:, None] // HEAD_DIM == lane[None, :] // HEAD_DIM).astype(MXU_DTYPE)


ATTN_QK_BLOCKS = (N_HEADS + N_KV_HEADS) * HEAD_DIM // LANES
ATTN_ALL_BLOCKS = (N_HEADS + 2 * N_KV_HEADS) * HEAD_DIM // LANES
ATTN_Q_BLOCKS = N_HEADS * HEAD_DIM // LANES
ATTN_SCALE = HEAD_DIM ** -0.5


def _attn_prep_fwd(qkv, gains, cos, sin, tidx, name):
    T, W = qkv.shape

    def body(x_ref, g_ref, cos_ref, sin_ref, ones_ref, o_ref):
        for cb in range(ATTN_ALL_BLOCKS):
            cols = slice(cb * LANES, (cb + 1) * LANES)
            xv = x_ref[:, cols]
            if cb < ATTN_QK_BLOCKS:
                r = lax.rsqrt(_head_sum(xv * xv, ones_ref) * (1.0 / HEAD_DIM) + EPS)
                y = xv * r * g_ref[0 if cb < ATTN_Q_BLOCKS else 1]
                xv = y * cos_ref[...] + _pair_swap(y, HEAD_DIM // 4) * sin_ref[...]
                if cb < ATTN_Q_BLOCKS:
                    xv = xv * ATTN_SCALE
            o_ref[:, cols] = xv.astype(o_ref.dtype)

    row = pl.BlockSpec((ROW_TILE, W), lambda i: (i, 0))
    tab = pl.BlockSpec((ROW_TILE, LANES), lambda i: (tidx(i), 0))
    return pl.pallas_call(
        body, name=name, grid=(T // ROW_TILE,),
        in_specs=[row, pl.BlockSpec((2, 1, LANES), lambda i: (0, 0, 0)), tab, tab,
                  pl.BlockSpec((LANES, LANES), lambda i: (0, 0))],
        out_specs=row, out_shape=jax.ShapeDtypeStruct(qkv.shape, MXU_DTYPE),
        compiler_params=_params())(qkv, gains, cos, sin, _head_ones())


def _attn_prep_bwd(latent, context, qkv, gains, cos, sin, tidx, name):
    T, W = qkv.shape
    qk_w = ATTN_QK_BLOCKS * LANES
    q_w = ATTN_Q_BLOCKS * LANES
    n_x = latent[0].shape[0] // ROW_TILE

    def body(dqx_ref, dkx_ref, dvx_ref, dqc_ref, dkc1_ref, dkc2_ref, dvc1_ref, dvc2_ref,
             x_ref, g_ref, cos_ref, sin_ref, ones_ref, o_ref, dg_ref):
        is_latent = pl.program_id(0) < n_x

        @pl.when(pl.program_id(0) == 0)
        def _():
            dg_ref[...] = jnp.zeros_like(dg_ref)

        for cb in range(ATTN_QK_BLOCKS):
            cols = slice(cb * LANES, (cb + 1) * LANES)
            xv = x_ref[:, cols]
            if cb < ATTN_Q_BLOCKS:
                d = jnp.where(is_latent, dqx_ref[:, cols], dqc_ref[:, cols]) * ATTN_SCALE
            else:
                kc = slice(cb * LANES - q_w, (cb + 1) * LANES - q_w)
                d = jnp.where(is_latent, dkx_ref[:, kc], dkc1_ref[:, kc] + dkc2_ref[:, kc])
            r = lax.rsqrt(_head_sum(xv * xv, ones_ref) * (1.0 / HEAD_DIM) + EPS)
            xn = xv * r
            dy = d * cos_ref[...] + _pair_swap(d * sin_ref[...], HEAD_DIM // 4)
            dg_ref[:, cols] += jnp.sum(dy * xn, axis=0, keepdims=True)
            dxn = dy * g_ref[0 if cb < ATTN_Q_BLOCKS else 1]
            dx = r * (dxn - xn * (_head_sum(dxn * xn, ones_ref) * (1.0 / HEAD_DIM)))
            o_ref[:, cols] = dx.astype(o_ref.dtype)
        o_ref[:, qk_w:] = jnp.where(is_latent, dvx_ref[...], dvc1_ref[...] + dvc2_ref[...]).astype(o_ref.dtype)

    row = lambda w: pl.BlockSpec((ROW_TILE, w), lambda i: (i, 0))
    lat = lambda t: pl.BlockSpec((ROW_TILE, t.shape[1]), lambda i: (jnp.minimum(i, n_x - 1), 0))
    ctx = lambda t: pl.BlockSpec((ROW_TILE, t.shape[1]), lambda i: (jnp.maximum(i - n_x, 0), 0))
    tab = pl.BlockSpec((ROW_TILE, LANES), lambda i: (tidx(i), 0))
    return pl.pallas_call(
        body, name=name, grid=(T // ROW_TILE,),
        in_specs=[lat(t) for t in latent] + [ctx(t) for t in context]
        + [row(W), pl.BlockSpec((2, 1, LANES), lambda i: (0, 0, 0)), tab, tab, pl.BlockSpec((LANES, LANES), lambda i: (0, 0))],
        out_specs=(row(W), pl.BlockSpec((1, qk_w), lambda i: (0, 0))),
        out_shape=(jax.ShapeDtypeStruct(qkv.shape, MXU_DTYPE), jax.ShapeDtypeStruct((1, qk_w), F32)),
        compiler_params=_params())(*latent, *context, qkv, gains, cos, sin, _head_ones())


RET_QK_BLOCKS = 2 * RET_HEADS * RET_QK_DIM // LANES


def _ret_rope(x, cos, sin, tidx, name):
    T = x.shape[0]
    W = RET_QK_BLOCKS * LANES
    k_scale = RET_QK_DIM ** -0.5

    def body(x_ref, cos_ref, sin_ref, o_ref):
        for cb in range(RET_QK_BLOCKS):
            cols = slice(cb * LANES, (cb + 1) * LANES)
            tcols = slice((cb % 2) * LANES, (cb % 2 + 1) * LANES)
            xv = x_ref[:, cols]
            out = xv * cos_ref[:, tcols] + pltpu.roll(xv, LANES // 2, axis=1) * sin_ref[:, tcols]
            if cb >= RET_QK_BLOCKS // 2:
                out = out * k_scale
            o_ref[:, cols] = out

    row = pl.BlockSpec((ROW_TILE, W), lambda i: (i, 0))
    tab = pl.BlockSpec((ROW_TILE, RET_QK_DIM), lambda i: (tidx(i), 0))
    return pl.pallas_call(
        body, name=name, grid=(T // ROW_TILE,), in_specs=[row, tab, tab], out_specs=row,
        out_shape=jax.ShapeDtypeStruct((T, W), F32), compiler_params=_params())(x, cos, sin)


ASSEMBLE_ROWS = 256


def _ret_grad_assemble(x_parts, c_parts, dg, cos, sin, seq, name):
    NX, NC = x_parts[0].shape[0], c_parts[0].shape[0]
    T = NX + NC
    rt = ASSEMBLE_ROWS
    nxt = NX // rt
    qk_w = RET_HEADS * RET_QK_DIM
    k_scale = RET_QK_DIM ** -0.5

    def unrotate(d, cos_ref, sin_ref, scale):
        outs = []
        for cb in range(qk_w // LANES):
            cols = slice(cb * LANES, (cb + 1) * LANES)
            tcols = slice((cb % 2) * LANES, (cb % 2 + 1) * LANES)
            dv_ = d[:, cols]
            o = dv_ * cos_ref[:, tcols] + pltpu.roll(dv_ * sin_ref[:, tcols], LANES // 2, axis=1)
            outs.append(o * scale if scale != 1.0 else o)
        return outs

    def body(dqf, dqb, dkf, dkb, dvf, dvb, dg_ref, dkcf, dkcb, dvcf, dvcb, cos_ref, sin_ref, o_ref):
        i = pl.program_id(0)

        def write_k(parts):
            for cb, o in enumerate(parts):
                o_ref[:, qk_w + cb * LANES:qk_w + (cb + 1) * LANES] = o.astype(o_ref.dtype)

        @pl.when(i < nxt)
        def _():
            for cb, o in enumerate(unrotate(dqf[...] + dqb[...], cos_ref, sin_ref, 1.0)):
                o_ref[:, cb * LANES:(cb + 1) * LANES] = o.astype(o_ref.dtype)
            write_k(unrotate(dkf[...] + dkb[...], cos_ref, sin_ref, k_scale))
            o_ref[:, 2 * qk_w:2 * qk_w + RET_VWIDTH] = (dvf[...] + dvb[...]).astype(o_ref.dtype)
            o_ref[:, 2 * qk_w + RET_VWIDTH:] = dg_ref[...].astype(o_ref.dtype)

        @pl.when(i >= nxt)
        def _():
            o_ref[:, :qk_w] = jnp.zeros((rt, qk_w), o_ref.dtype)
            write_k(unrotate(dkcf[...] + dkcb[...], cos_ref, sin_ref, k_scale))
            o_ref[:, 2 * qk_w:2 * qk_w + RET_VWIDTH] = (dvcf[...] + dvcb[...]).astype(o_ref.dtype)
            o_ref[:, 2 * qk_w + RET_VWIDTH:] = jnp.zeros((rt, RET_VWIDTH), o_ref.dtype)

    xs = lambda w: pl.BlockSpec((rt, w), lambda i: (jnp.minimum(i, nxt - 1), 0))
    cs = lambda w: pl.BlockSpec((rt, w), lambda i: (jnp.maximum(i - nxt, 0), 0))
    tab = pl.BlockSpec((rt, RET_QK_DIM), lambda i: (jnp.where(i < nxt, i % (seq // rt), seq // rt), 0))
    return pl.pallas_call(
        body, name=name, grid=(T // rt,),
        in_specs=[xs(qk_w)] * 4 + [xs(RET_VWIDTH)] * 3 + [cs(qk_w)] * 2 + [cs(RET_VWIDTH)] * 2 + [tab, tab],
        out_specs=pl.BlockSpec((rt, 2 * qk_w + 2 * RET_VWIDTH), lambda i: (i, 0)),
        out_shape=jax.ShapeDtypeStruct((T, 2 * qk_w + 2 * RET_VWIDTH), MXU_DTYPE),
        compiler_params=_params())(*x_parts, dg, *c_parts, cos, sin)


def _band_bias(qb, seq):
    nb = seq // qb
    assert nb >= 2
    i = jnp.arange(GQA_GROUP * qb, dtype=jnp.int32)[:, None] % qb
    n = jnp.arange(3 * qb, dtype=jnp.int32)[None, :]
    in_window = (n >= i) & (n - i <= 2 * WINDOW)
    variants = [in_window & (n >= qb), in_window, in_window & (n < 2 * qb)]
    return jnp.stack([jnp.where(v, 0.0, NEG_INF).astype(F32) for v in variants])


GROUP_ORDER = (0, 2, 1, 3)


def _stack_halves(blk):
    return jnp.concatenate([blk[:, :LANES], blk[:, LANES:]], axis=0)


def _unstack_halves(v, rows):
    return jnp.concatenate([v[:rows], v[rows:]], axis=1)


def _align_head(pair, odd):
    lane = lax.broadcasted_iota(jnp.int32, pair.shape, 1)
    mine = jnp.where((lane >= HEAD_DIM) == odd, pair, jnp.zeros_like(pair))
    rolled = pltpu.roll(mine, HEAD_DIM, axis=1)
    return jnp.where(odd, rolled, mine), jnp.where(odd, mine, rolled)


def _scores(out_ref, q2, x_eo):
    half = q2.shape[0]
    out_ref[:half, :] = _dot(q2, x_eo[0], _NT)
    out_ref[half:, :] = _dot(q2, x_eo[1], _NT)


def _apply(p_ref, x_eo):
    half = p_ref.shape[0] // 2
    return _dot(p_ref[:half, :], x_eo[0], _NN) + _dot(p_ref[half:, :], x_eo[1], _NN)


def _kv_grad(a_ref, q2, odd):
    half = a_ref.shape[0] // 2
    even_t = _dot(q2, a_ref[:half, :], _TN)
    odd_t = _dot(q2, a_ref[half:, :], _TN)
    mine = even_t[:HEAD_DIM] + odd_t[HEAD_DIM:]
    zero = jnp.zeros_like(mine)
    placed = jnp.where(odd, jnp.concatenate([zero, mine], axis=0), jnp.concatenate([mine, zero], axis=0))
    return placed.T


ATTN_ROW_CHUNK = 32


def _softmax_chunks(s_c_ref, s_l_ref, bias_ref, sink_ref, kv_head, qb, emit):
    for r0 in range(0, GQA_GROUP * qb, ATTN_ROW_CHUNK):
        rows = slice(r0, r0 + ATTN_ROW_CHUNK)
        t = r0 // qb
        sink = jnp.full((ATTN_ROW_CHUNK, 1), sink_ref[kv_head, GROUP_ORDER[t]], F32)
        s_c = s_c_ref[rows, :]
        m = jnp.maximum(jnp.max(s_c, axis=-1, keepdims=True), sink)
        s_l = None
        if s_l_ref is not None:
            s_l = s_l_ref[rows, :] + bias_ref[0, rows, :]
            m = jnp.maximum(m, jnp.max(s_l, axis=-1, keepdims=True))
        e_c = jnp.exp(s_c - m)
        e_s = jnp.exp(sink - m)
        den = jnp.sum(e_c, axis=-1, keepdims=True) + e_s
        e_l = None
        if s_l_ref is not None:
            e_l = jnp.exp(s_l - m)
            den = den + jnp.sum(e_l, axis=-1, keepdims=True)
        inv = 1.0 / den
        emit(t, rows, e_c * inv, (None if e_l is None else e_l * inv), e_s * inv)


GROUP_W = GQA_GROUP * HEAD_DIM
K_LANE_BLOCK = N_HEADS * HEAD_DIM // LANES
V_LANE_BLOCK = K_LANE_BLOCK + N_KV_HEADS * HEAD_DIM // LANES


def _attn_specs(B, seq, ctx_len, ctx_queries):
    ctx0 = B * seq // ctx_len
    if ctx_queries:
        qb, nb = ctx_len, 1
        qrow = lambda b, j: ctx0 + b
    else:
        qb, nb = ATTN_BLOCK, seq // ATTN_BLOCK
        qrow = lambda b, j: b * nb + j
    q_spec = pl.BlockSpec((qb, 2 * GROUP_W), lambda b, p, j: (qrow(b, j), p))
    c_specs = [pl.BlockSpec((ctx_len, LANES), lambda b, p, j: (ctx0 + b, K_LANE_BLOCK + p)),
               pl.BlockSpec((ctx_len, LANES), lambda b, p, j: (ctx0 + b, V_LANE_BLOCK + p))]
    local = []
    if not ctx_queries:
        near = [lambda j: jnp.maximum(j - 1, 0), lambda j: j, lambda j: jnp.minimum(j + 1, nb - 1)]
        for lane0 in (K_LANE_BLOCK, V_LANE_BLOCK):
            for f in near:
                local.append(pl.BlockSpec((qb, LANES), lambda b, p, j, f=f, lane0=lane0: (b * nb + f(j), lane0 + p)))
        local.append(pl.BlockSpec(
            (1, GQA_GROUP * qb, 3 * qb), lambda b, p, j: (jnp.where(j == 0, 0, jnp.where(j == nb - 1, 2, 1)), 0, 0)))
    return qb, nb, qrow, q_spec, c_specs, local


def _attn_operands(refs, has_local, sub):
    odd = bool(sub)
    n_local = 7 if has_local else 0
    q2 = _stack_halves(refs[0][:, sub * GROUP_W:(sub + 1) * GROUP_W])
    kc = _align_head(refs[1 + n_local][...], odd)
    vc = _align_head(refs[2 + n_local][...], odd)
    kl = vl = bias_ref = None
    if has_local:
        kl = _align_head(jnp.concatenate([r[...] for r in refs[1:4]], axis=0), odd)
        vl = _align_head(jnp.concatenate([r[...] for r in refs[4:7]], axis=0), odd)
        bias_ref = refs[7]
    return odd, q2, kc, vc, kl, vl, bias_ref


def _score_scratch(qb, ctx_len, has_local, dtypes):
    rows = GQA_GROUP * qb
    out = []
    for dt in dtypes:
        out.append(pltpu.VMEM((rows, ctx_len), dt))
        if has_local:
            out.append(pltpu.VMEM((rows, 3 * qb), dt))
    return out


def _score_bufs(scratch, has_local):
    if has_local:
        return [(scratch[i], scratch[i + 1]) for i in range(0, len(scratch), 2)]
    return [(s, None) for s in scratch]


def _attn_fwd(qkv, sink, B, seq, ctx_len, ctx_queries, name):
    has_local = not ctx_queries
    qb, nb, _, q_spec, c_specs, local = _attn_specs(B, seq, ctx_len, ctx_queries)
    n_rows = B * (ctx_len if ctx_queries else seq)
    n_in = 1 + (7 if has_local else 0) + 3

    per_head = _score_scratch(qb, ctx_len, has_local, (F32, MXU_DTYPE))

    def body(*refs):
        sink_ref, o_ref = refs[n_in - 1], refs[n_in]
        scratch = refs[n_in + 1:]
        for sub in range(2):
            (s_c_ref, s_l_ref), (p_c_ref, p_l_ref) = _score_bufs(
                scratch[sub * len(per_head):(sub + 1) * len(per_head)], has_local)
            kv_head = 2 * pl.program_id(1) + sub
            _, q2, kc, vc, kl, vl, bias_ref = _attn_operands(refs, has_local, sub)
            _scores(s_c_ref, q2, kc)
            if has_local:
                _scores(s_l_ref, q2, kl)

            def emit(t, rows, p_c, p_l, p_s, p_c_ref=p_c_ref, p_l_ref=p_l_ref):
                p_c_ref[rows, :] = p_c.astype(p_c_ref.dtype)
                if has_local:
                    p_l_ref[rows, :] = p_l.astype(p_l_ref.dtype)

            _softmax_chunks(s_c_ref, s_l_ref, bias_ref, sink_ref, kv_head, qb, emit)
            o2 = _apply(p_c_ref, vc)
            if has_local:
                o2 = o2 + _apply(p_l_ref, vl)
            o_ref[:, sub * GROUP_W:(sub + 1) * GROUP_W] = _unstack_halves(o2, qb).astype(o_ref.dtype)

    operands = [qkv] + ([qkv] * 6 + [_band_bias(qb, seq)] if has_local else []) + [qkv, qkv, sink]
    return pl.pallas_call(
        body, name=name, grid=(B, N_KV_HEADS // 2, nb),
        in_specs=[q_spec] + local + c_specs + [_SMEM],
        out_specs=pl.BlockSpec((qb, 2 * GROUP_W), lambda b, p, j: (b * nb + j, p)),
        out_shape=jax.ShapeDtypeStruct((n_rows, N_HEADS * HEAD_DIM), MXU_DTYPE),
        scratch_shapes=per_head * 2, compiler_params=_params())(*operands)


def _attn_bwd(qkv, sink, do, B, seq, ctx_len, ctx_queries, name):
    has_local = not ctx_queries
    qb, nb, qrow, q_spec, c_specs, local = _attn_specs(B, seq, ctx_len, ctx_queries)
    n_rows = B * (ctx_len if ctx_queries else seq)

    n_out = 6 if has_local else 4
    per_head = _score_scratch(qb, ctx_len, has_local, (F32, F32, MXU_DTYPE, MXU_DTYPE))

    def body(*refs):
        n_in = 1 + (7 if has_local else 0) + 4
        sink_ref, do_ref = refs[n_in - 2:n_in]
        outs = refs[n_in:n_in + n_out]
        scratch = refs[n_in + n_out:]
        dq_ref = outs[0]
        dkc_ref, dvc_ref, dsink_ref = outs[-3:]
        b, pair, j = pl.program_id(0), pl.program_id(1), pl.program_id(2)

        @pl.when(j == 0)
        def _():
            dkc_ref[...] = jnp.zeros_like(dkc_ref)
            dvc_ref[...] = jnp.zeros_like(dvc_ref)
            if has_local:
                outs[1][...] = jnp.zeros_like(outs[1])
                outs[2][...] = jnp.zeros_like(outs[2])

        @pl.when((b == 0) & (pair == 0) & (j == 0))
        def _():
            dsink_ref[...] = jnp.zeros_like(dsink_ref)

        for sub in range(2):
            (s_c_ref, s_l_ref), (dp_c_ref, dp_l_ref), (p_c_ref, p_l_ref), (ds_c_ref, ds_l_ref) = _score_bufs(
                scratch[sub * len(per_head):(sub + 1) * len(per_head)], has_local)
            kv_head = 2 * pair + sub
            odd, q2, kc, vc, kl, vl, bias_ref = _attn_operands(refs, has_local, sub)
            do2 = _stack_halves(do_ref[:, sub * GROUP_W:(sub + 1) * GROUP_W])
            _scores(s_c_ref, q2, kc)
            _scores(dp_c_ref, do2, vc)
            if has_local:
                _scores(s_l_ref, q2, kl)
                _scores(dp_l_ref, do2, vl)
            dsink_parts = [jnp.zeros((), F32)] * GQA_GROUP

            def emit(t, rows, p_c, p_l, p_s, dp_c_ref=dp_c_ref, dp_l_ref=dp_l_ref, p_c_ref=p_c_ref, p_l_ref=p_l_ref,
                     ds_c_ref=ds_c_ref, ds_l_ref=ds_l_ref, dsink_parts=dsink_parts):
                dp_c = dp_c_ref[rows, :]
                delta = jnp.sum(p_c * dp_c, axis=-1, keepdims=True)
                if has_local:
                    dp_l = dp_l_ref[rows, :]
                    delta = delta + jnp.sum(p_l * dp_l, axis=-1, keepdims=True)
                    p_l_ref[rows, :] = p_l.astype(p_l_ref.dtype)
                    ds_l_ref[rows, :] = (p_l * (dp_l - delta)).astype(ds_l_ref.dtype)
                p_c_ref[rows, :] = p_c.astype(p_c_ref.dtype)
                ds_c_ref[rows, :] = (p_c * (dp_c - delta)).astype(ds_c_ref.dtype)
                dsink_parts[t] = dsink_parts[t] - jnp.sum(p_s * delta)

            _softmax_chunks(s_c_ref, s_l_ref, bias_ref, sink_ref, kv_head, qb, emit)
            dq2 = _apply(ds_c_ref, kc)
            dkc_ref[...] += _kv_grad(ds_c_ref, q2, odd)
            dvc_ref[...] += _kv_grad(p_c_ref, do2, odd)
            if has_local:
                dq2 = dq2 + _apply(ds_l_ref, kl)
                dkl = _kv_grad(ds_l_ref, q2, odd)
                dvl = _kv_grad(p_l_ref, do2, odd)
                dk_ref, dv_ref = outs[1], outs[2]
                for t in range(3):
                    def add(t=t, dkl=dkl, dvl=dvl):
                        start = pl.multiple_of((j - 1 + t) * qb, qb)
                        dk_ref[pl.ds(start, qb), :] += dkl[t * qb:(t + 1) * qb]
                        dv_ref[pl.ds(start, qb), :] += dvl[t * qb:(t + 1) * qb]
                    if t == 0:
                        pl.when(j > 0)(add)
                    elif t == 2:
                        pl.when(j < nb - 1)(add)
                    else:
                        add()
            dq_ref[:, sub * GROUP_W:(sub + 1) * GROUP_W] = _unstack_halves(dq2, qb)
            row8 = lax.broadcasted_iota(jnp.int32, (8, LANES), 0)
            tile = jnp.zeros((8, LANES), F32)
            for t, gi in enumerate(GROUP_ORDER):
                tile = jnp.where(row8 == gi, dsink_parts[t], tile)
            dsink_ref[pl.ds(pl.multiple_of(kv_head * 8, 8), 8), :] += tile

    kv_w = N_KV_HEADS * HEAD_DIM
    seq_spec = pl.BlockSpec((seq, LANES), lambda b, p, j: (b, p))
    ctx_spec = pl.BlockSpec((ctx_len, LANES), lambda b, p, j: (b, p))
    do_spec = pl.BlockSpec((qb, 2 * GROUP_W), lambda b, p, j: (qrow(b, j), p))
    operands = [qkv] + ([qkv] * 6 + [_band_bias(qb, seq)] if has_local else []) + [qkv, qkv, sink, do]
    out_specs = ([pl.BlockSpec((qb, 2 * GROUP_W), lambda b, p, j: (b * nb + j, p))] + ([seq_spec, seq_spec] if has_local else [])
                 + [ctx_spec, ctx_spec, pl.BlockSpec((32, LANES), lambda b, p, j: (0, 0))])
    out_shape = ([jax.ShapeDtypeStruct((n_rows, N_HEADS * HEAD_DIM), F32)]
                 + ([jax.ShapeDtypeStruct((B * seq, kv_w), F32)] * 2 if has_local else [])
                 + [jax.ShapeDtypeStruct((B * ctx_len, kv_w), F32)] * 2 + [jax.ShapeDtypeStruct((32, LANES), F32)])
    return pl.pallas_call(
        body, name=name, grid=(B, N_KV_HEADS // 2, nb),
        in_specs=[q_spec] + local + c_specs + [_SMEM, do_spec],
        out_specs=tuple(out_specs), out_shape=tuple(out_shape), scratch_shapes=per_head * 2,
        compiler_params=_params())(*operands)


def _ret_decays(lg, rev):
    n = lax.broadcasted_iota(jnp.int32, (RET_CHUNK, RET_CHUNK), 0).astype(F32)
    m = lax.broadcasted_iota(jnp.int32, (RET_CHUNK, RET_CHUNK), 1).astype(F32)
    pos = lax.broadcasted_iota(jnp.int32, (RET_CHUNK, 1), 0).astype(F32)
    diff = (m - n) if rev else (n - m)
    a_exp = jnp.maximum(diff, 0.0)
    intra = jnp.where(diff >= 0, jnp.exp(lg * a_exp), 0.0)
    q_exp = (RET_CHUNK - pos) if rev else (pos + 1.0)
    k_exp = pos if rev else (RET_CHUNK - 1.0 - pos)
    chunk = jnp.exp(jnp.full((1, 1), RET_CHUNK, F32) * lg)
    return intra, a_exp, jnp.exp(lg * q_exp), q_exp, jnp.exp(lg * k_exp), k_exp, chunk


def _ctx_decay(lg, ctx_len, rev):
    t = lax.broadcasted_iota(jnp.int32, (ctx_len, 1), 0).astype(F32)
    expo = t if rev else (ctx_len - 1.0 - t)
    return jnp.exp(lg * expo), expo


def _ret_specs(B, seq, ctx_len, order):
    nc = seq // RET_CHUNK
    x_blocks = B * seq // ctx_len

    def rows(b, c):
        return b * nc + order(c, nc)

    q_spec = pl.BlockSpec((RET_CHUNK, RET_QK_DIM), lambda b, h, c: (rows(b, c), h))
    k_spec = pl.BlockSpec((RET_CHUNK, RET_QK_DIM), lambda b, h, c: (rows(b, c), RET_HEADS + h))
    v_spec = pl.BlockSpec((RET_CHUNK, RET_V_DIM), lambda b, h, c: (rows(b, c), RET_HEADS + h))
    kc_spec = pl.BlockSpec((ctx_len, RET_QK_DIM), lambda b, h, c: (x_blocks + b, RET_HEADS + h))
    vc_spec = pl.BlockSpec((ctx_len, RET_V_DIM), lambda b, h, c: (x_blocks + b, RET_HEADS + h))
    st_spec = pl.BlockSpec((1, 1, 1, RET_QK_DIM, RET_V_DIM), lambda b, h, c: (b, h, order(c, nc), 0, 0))
    o_spec = pl.BlockSpec((RET_CHUNK, RET_V_DIM), lambda b, h, c: (rows(b, c), h))
    return nc, q_spec, k_spec, v_spec, kc_spec, vc_spec, st_spec, o_spec


_SCAN_UP = lambda c, nc: c
_SCAN_DOWN = lambda c, nc: nc - 1 - c


def _ret_fwd(qk, qkvg, log_g, B, seq, ctx_len, name):
    nc, qf, kf, vf, kc_spec, vc_spec, stf, of = _ret_specs(B, seq, ctx_len, _SCAN_UP)
    _, qr, kr, vr, _, _, str_, or_ = _ret_specs(B, seq, ctx_len, _SCAN_DOWN)

    def body(lg_ref, qf_ref, kf_ref, vf_ref, qr_ref, kr_ref, vr_ref, kc_ref, vc_ref,
             of_ref, stf_ref, or_ref, str_ref, state_f, state_r):
        h, c = pl.program_id(1), pl.program_id(2)
        dirs = ((False, lg_ref[0, h], qf_ref, kf_ref, vf_ref, of_ref, stf_ref, state_f),
                (True, lg_ref[1, h], qr_ref, kr_ref, vr_ref, or_ref, str_ref, state_r))

        @pl.when(c == 0)
        def _():
            for rev, lg, _, _, _, _, _, state in dirs:
                dec, _ = _ctx_decay(lg, ctx_len, rev)
                state[...] = _dot(kc_ref[...] * dec, vc_ref[...], _TN)

        for rev, lg, q_ref, k_ref, v_ref, o_ref, st_ref, state in dirs:
            intra, _, q_dec, _, k_dec, _, chunk_dec = _ret_decays(lg, rev)
            qv, kv, vv = q_ref[...], k_ref[...], v_ref[...]
            s_in = state[...]
            st_ref[0, 0, 0] = s_in
            w = _dot(qv, kv, _NT) * intra
            o_ref[...] = _dot(w, vv, _NN) + _dot(qv, s_in, _NN) * q_dec
            state[...] = s_in * chunk_dec + _dot(kv * k_dec, vv, _TN)

    o_shape = jax.ShapeDtypeStruct((B * seq, RET_VWIDTH), F32)
    st_shape = jax.ShapeDtypeStruct((B, RET_HEADS, nc, RET_QK_DIM, RET_V_DIM), F32)
    return pl.pallas_call(
        body, name=name, grid=(B, RET_HEADS, nc),
        in_specs=[_SMEM, qf, kf, vf, qr, kr, vr, kc_spec, vc_spec],
        out_specs=(of, stf, or_, str_), out_shape=(o_shape, st_shape, o_shape, st_shape),
        scratch_shapes=[pltpu.VMEM((RET_QK_DIM, RET_V_DIM), F32)] * 2,
        compiler_params=_params())(log_g, qk, qk, qkvg, qk, qk, qkvg, qk, qkvg)


def _ret_bwd_chunk(rev, lg, q_ref, k_ref, v_ref, st_ref, do_ref, dq_ref, dk_ref, dv_ref, dlg_ref, dstate):
    intra, a_exp, q_dec, q_exp, k_dec, k_exp, chunk_dec = _ret_decays(lg, rev)
    qv, kv, vv, dov = q_ref[...], k_ref[...], v_ref[...], do_ref[...]
    s_in, ds_out = st_ref[0, 0, 0], dstate[...]
    p = _dot(qv, kv, _NT)
    w = p * intra
    dw = _dot(dov, vv, _NT)
    dp = dw * intra
    do_dec = dov * q_dec
    kd = kv * k_dec
    v_ds = _dot(vv, ds_out, _NT)
    dq_ref[...] = _dot(dp, kv, _NN) + _dot(do_dec, s_in, _NT)
    dk_ref[...] = _dot(dp, qv, _TN) + v_ds * k_dec
    dv_ref[...] = _dot(w, dov, _TN) + _dot(kd, ds_out, _NN)
    q_s = _dot(qv, s_in, _NN)
    dlg = (jnp.sum(dw * w * a_exp)
           + jnp.sum(q_exp * q_dec * jnp.sum(dov * q_s, axis=-1, keepdims=True))
           + jnp.sum(k_exp * k_dec * jnp.sum(kv * v_ds, axis=-1, keepdims=True))
           + RET_CHUNK * jnp.sum(chunk_dec * (ds_out * s_in)))
    ds_in = ds_out * chunk_dec + _dot(qv, do_dec, _TN)
    dstate[...] = ds_in
    dlg_ref[...] += dlg
    return ds_in


def _ret_bwd(qk, qkvg, log_g, st_f, st_r, do, B, seq, ctx_len, name):
    nc, qf, kf, vf, kc_spec, vc_spec, stf, of = _ret_specs(B, seq, ctx_len, _SCAN_DOWN)
    _, qr, kr, vr, _, _, str_, or_ = _ret_specs(B, seq, ctx_len, _SCAN_UP)

    def body(lg_ref, qf_ref, kf_ref, vf_ref, stf_ref, dof_ref, qr_ref, kr_ref, vr_ref, str_ref, dor_ref, kc_ref, vc_ref,
             dqf, dkf, dvf, dkcf, dvcf, dlgf, dqr, dkr, dvr, dkcr, dvcr, dlgr, dstate_f, dstate_r):
        h, c = pl.program_id(1), pl.program_id(2)
        dirs = ((False, lg_ref[0, h], (qf_ref, kf_ref, vf_ref, stf_ref, dof_ref, dqf, dkf, dvf, dlgf, dstate_f), dkcf, dvcf),
                (True, lg_ref[1, h], (qr_ref, kr_ref, vr_ref, str_ref, dor_ref, dqr, dkr, dvr, dlgr, dstate_r), dkcr, dvcr))

        @pl.when(c == 0)
        def _():
            for _, _, refs, _, _ in dirs:
                refs[-1][...] = jnp.zeros_like(refs[-1])
                refs[-2][...] = jnp.zeros_like(refs[-2])

        ds_first = [_ret_bwd_chunk(rev, lg, *refs) for rev, lg, refs, _, _ in dirs]

        @pl.when(c == nc - 1)
        def _():
            for (rev, lg, refs, dkc_ref, dvc_ref), ds_in in zip(dirs, ds_first):
                dec, expo = _ctx_decay(lg, ctx_len, rev)
                kcv, vcv = kc_ref[...], vc_ref[...]
                vc_ds = _dot(vcv, ds_in, _NT)
                dkc_ref[...] = vc_ds * dec
                dvc_ref[...] = _dot(kcv * dec, ds_in, _NN)
                refs[-2][...] += jnp.sum(expo * dec * jnp.sum(kcv * vc_ds, axis=-1, keepdims=True))

    def outs(q_spec, o_spec):
        return (pl.BlockSpec((RET_CHUNK, RET_QK_DIM), q_spec.index_map),
                pl.BlockSpec((RET_CHUNK, RET_QK_DIM), q_spec.index_map), o_spec,
                pl.BlockSpec((ctx_len, RET_QK_DIM), lambda b, h, c: (b, h)),
                pl.BlockSpec((ctx_len, RET_V_DIM), lambda b, h, c: (b, h)),
                pl.BlockSpec((1, 1, 8, LANES), lambda b, h, c: (b, h, 0, 0)))

    shapes = (jax.ShapeDtypeStruct((B * seq, RET_HEADS * RET_QK_DIM), F32),
              jax.ShapeDtypeStruct((B * seq, RET_HEADS * RET_QK_DIM), F32),
              jax.ShapeDtypeStruct((B * seq, RET_VWIDTH), F32),
              jax.ShapeDtypeStruct((B * ctx_len, RET_HEADS * RET_QK_DIM), F32),
              jax.ShapeDtypeStruct((B * ctx_len, RET_VWIDTH), F32),
              jax.ShapeDtypeStruct((B, RET_HEADS, 8, LANES), F32))
    res = pl.pallas_call(
        body, name=name, grid=(B, RET_HEADS, nc),
        in_specs=[_SMEM, qf, kf, vf, stf, of, qr, kr, vr, str_, or_, kc_spec, vc_spec],
        out_specs=outs(qf, of) + outs(qr, or_), out_shape=shapes + shapes,
        scratch_shapes=[pltpu.VMEM((RET_QK_DIM, RET_V_DIM), F32)] * 2,
        compiler_params=_params())(log_g, qk, qk, qkvg, st_f, do, qk, qk, qkvg, st_r, do, qk, qkvg)
    return res[:6], res[6:]


def _gated_out_fwd(o_f, o_b, qkvg, gn_gain, name):
    T = o_f.shape[0]
    g_off = (2 * RET_HEADS * RET_QK_DIM + RET_VWIDTH) // RET_V_DIM

    def body(of_ref, ob_ref, g_ref, gain_ref, z_ref):
        o = of_ref[...] + ob_ref[...]
        mu = jnp.mean(o, axis=-1, keepdims=True)
        var = jnp.mean(jnp.square(o - mu), axis=-1, keepdims=True)
        y = (o - mu) * lax.rsqrt(var + EPS) * gain_ref[...]
        gv = g_ref[...]
        z_ref[...] = (gv * jax.nn.sigmoid(gv) * y).astype(z_ref.dtype)

    blk = pl.BlockSpec((ROW_TILE, RET_V_DIM), lambda i, h: (i, h))
    return pl.pallas_call(
        body, name=name, grid=(T // ROW_TILE, RET_HEADS),
        in_specs=[blk, blk, pl.BlockSpec((ROW_TILE, RET_V_DIM), lambda i, h: (i, g_off + h)),
                  pl.BlockSpec((1, RET_V_DIM), lambda i, h: (0, h))],
        out_specs=blk, out_shape=jax.ShapeDtypeStruct((T, RET_VWIDTH), MXU_DTYPE),
        compiler_params=_params())(o_f, o_b, qkvg, gn_gain)


def _gated_out_bwd(dz, o_f, o_b, qkvg, gn_gain, name):
    T = o_f.shape[0]
    g_off = (2 * RET_HEADS * RET_QK_DIM + RET_VWIDTH) // RET_V_DIM

    def body(dz_ref, of_ref, ob_ref, g_ref, gain_ref, do_ref, dg_ref, dgain_ref):
        o = of_ref[...] + ob_ref[...]
        mu = jnp.mean(o, axis=-1, keepdims=True)
        var = jnp.mean(jnp.square(o - mu), axis=-1, keepdims=True)
        rstd = lax.rsqrt(var + EPS)
        yhat = (o - mu) * rstd
        gv, dzv = g_ref[...], dz_ref[...]
        sg = jax.nn.sigmoid(gv)
        dg_ref[...] = (dzv * (yhat * gain_ref[...]) * (sg * (1.0 + gv * (1.0 - sg)))).astype(dg_ref.dtype)
        dy = dzv * (gv * sg)

        @pl.when(pl.program_id(1) == 0)
        def _():
            dgain_ref[...] = jnp.zeros_like(dgain_ref)

        dgain_ref[...] += jnp.sum(dy * yhat, axis=0, keepdims=True)
        dyh = dy * gain_ref[...]
        do_ref[...] = rstd * (dyh - jnp.mean(dyh, axis=-1, keepdims=True)
                              - yhat * jnp.mean(dyh * yhat, axis=-1, keepdims=True))

    blk = pl.BlockSpec((ROW_TILE, RET_V_DIM), lambda h, i: (i, h))
    vec = pl.BlockSpec((1, RET_V_DIM), lambda h, i: (0, h))
    return pl.pallas_call(
        body, name=name, grid=(RET_HEADS, T // ROW_TILE),
        in_specs=[blk, blk, blk, pl.BlockSpec((ROW_TILE, RET_V_DIM), lambda h, i: (i, g_off + h)), vec],
        out_specs=(blk, blk, vec),
        out_shape=(jax.ShapeDtypeStruct((T, RET_VWIDTH), F32), jax.ShapeDtypeStruct((T, RET_VWIDTH), MXU_DTYPE),
                   jax.ShapeDtypeStruct((1, RET_VWIDTH), F32)),
        compiler_params=_params())(dz, o_f, o_b, qkvg, gn_gain)


def _adamw(w, m, v, parts, name):
    R, C = w.shape
    tr = _tile(R, (256, 128, 64, 32, 16, 8))
    n_parts = [p.shape[0] for p in parts]

    def body(*refs):
        w_ref, m_ref, v_ref = refs[:3]
        part_refs = refs[3:3 + len(parts)]
        g_ref, d_ref, nm_ref, nv_ref = refs[3 + len(parts):]
        g = None
        for ref, n in zip(part_refs, n_parts):
            for r in range(n):
                term = ref[r].astype(F32)
                g = term if g is None else g + term
        mn = ADAM_B1 * m_ref[...] + (1.0 - ADAM_B1) * g
        vn = ADAM_B2 * v_ref[...] + (1.0 - ADAM_B2) * jnp.square(g)
        m_hat = mn / (1.0 - ADAM_B1 ** ADAM_STEP)
        v_hat = vn / (1.0 - ADAM_B2 ** ADAM_STEP)
        g_ref[...] = g
        d_ref[...] = -ADAM_LR * (m_hat / (jnp.sqrt(v_hat) + ADAM_EPS) + ADAM_WD * w_ref[...])
        nm_ref[...] = mn
        nv_ref[...] = vn

    blk = pl.BlockSpec((tr, C), lambda i: (i, 0))
    part_specs = [pl.BlockSpec((n, tr, C), lambda i: (0, i, 0)) for n in n_parts]
    shp = jax.ShapeDtypeStruct((R, C), F32)
    return pl.pallas_call(
        body, name=name, grid=(R // tr,), in_specs=[blk, blk, blk] + part_specs,
        out_specs=(blk, blk, blk, blk), out_shape=(shp, shp, shp, shp),
        compiler_params=_params())(w, m, v, *parts)


def _sum_rows(parts, name):
    n, R, C = parts.shape
    tr = _tile(R, (256, 128, 64, 32, 16, 8))

    def body(p_ref, o_ref):
        acc = p_ref[0]
        for r in range(1, n):
            acc = acc + p_ref[r]
        o_ref[...] = acc

    return pl.pallas_call(
        body, name=name, grid=(R // tr,), in_specs=[pl.BlockSpec((n, tr, C), lambda i: (0, i, 0))],
        out_specs=pl.BlockSpec((tr, C), lambda i: (i, 0)), out_shape=jax.ShapeDtypeStruct((R, C), F32),
        compiler_params=_params())(parts)


def _my_coords():
    return lax.axis_index("x"), lax.axis_index("y"), lax.axis_index("c")


def _flip(coord, bit):
    return 1 - coord if bit else coord


def _all_gather(x2d, name):
    R, C = x2d.shape

    def body(x_ref, out_ref, send_sems, recv_sems, local_sem):
        x, y, c = _my_coords()
        me, sibling = (x, y, c), (x, y, 1 - c)
        chips = [(1 - x, y), (x, 1 - y), (1 - x, 1 - y)]

        def rows(px, py, pc):
            return out_ref.at[4 * px + 2 * py + pc]

        def copy(k, block, to, src=None):
            return pltpu.make_async_remote_copy(
                src_ref=rows(*block) if src is None else src, dst_ref=rows(*block),
                send_sem=send_sems.at[k], recv_sem=recv_sems.at[k], device_id=to, device_id_type=MESH)

        mine = pltpu.make_async_copy(x_ref, rows(*me), local_sem)
        mine.start()
        first = [copy(0, me, sibling, src=x_ref)]
        first += [copy(1 + j, me, (*chip, c), src=x_ref) for j, chip in enumerate(chips)]
        for cp in first:
            cp.start()
        passed = [copy(4 + j, (*chip, c), sibling) for j, chip in enumerate(chips)]
        for j, chip in enumerate(chips):
            copy(1 + j, (*chip, c), me).wait_recv()
            passed[j].start()
        copy(0, sibling, me).wait_recv()
        for j, chip in enumerate(chips):
            copy(4 + j, (*chip, 1 - c), me).wait_recv()
        for cp in first + passed:
            cp.wait_send()
        mine.wait()

    return pl.pallas_call(
        body, name=name, out_shape=jax.ShapeDtypeStruct((N_DEV, R, C), x2d.dtype),
        in_specs=[_ANY], out_specs=_ANY,
        scratch_shapes=[pltpu.SemaphoreType.DMA((7,)), pltpu.SemaphoreType.DMA((7,)), pltpu.SemaphoreType.DMA],
    )(x2d)


BIG_WEIGHTS = {
    "ffn_w_in": (2, (2, D_MODEL, 2 * D_FF)),
    "ffn_w_out": (1, (2, D_FF, D_MODEL)),
    "attn_w_qkv": (2, (1, D_MODEL, (N_HEADS + 2 * N_KV_HEADS) * HEAD_DIM)),
    "attn_w_o": (1, (1, N_HEADS * HEAD_DIM, D_MODEL)),
    "ret_w_qkvg": (2, (1, D_MODEL, 2 * D_MODEL + 2 * RET_VWIDTH)),
    "ret_gn_g": (2, (1, 1, RET_VWIDTH)),
    "ret_w_o": (1, (1, RET_VWIDTH, D_MODEL)),
}


def _join_shards(name, stacked):
    axis, full = BIG_WEIGHTS[name]
    if axis == 2:
        stacked = stacked.transpose(0, 2, 1, 3)
    return stacked.reshape(full)


def _split_shards(name, full_arr):
    axis, (_, rows, cols) = BIG_WEIGHTS[name]
    L = full_arr.shape[0]
    if axis == 2:
        return full_arr.reshape(L, rows, N_DEV, cols // N_DEV).transpose(0, 2, 1, 3)
    return full_arr.reshape(L, N_DEV, rows // N_DEV, cols)


def _gather_shards(shards, name):
    n = len(shards)

    def body(*refs):
        x_refs, out_refs = refs[:n], refs[n:2 * n]
        send_sems, recv_sems, local_sems = refs[2 * n:]
        x, y, c = _my_coords()
        me, sibling = (x, y, c), (x, y, 1 - c)
        chips = [(1 - x, y), (x, 1 - y), (1 - x, 1 - y)]

        def rows(a, px, py, pc):
            return out_refs[a].at[:, 4 * px + 2 * py + pc]

        def copy(a, k, block, to, src=None):
            return pltpu.make_async_remote_copy(
                src_ref=rows(a, *block) if src is None else src, dst_ref=rows(a, *block),
                send_sem=send_sems.at[7 * a + k], recv_sem=recv_sems.at[7 * a + k], device_id=to, device_id_type=MESH)

        mine = [pltpu.make_async_copy(x_refs[a], rows(a, *me), local_sems.at[a]) for a in range(n)]
        for cp in mine:
            cp.start()
        first = []
        for a in range(n):
            first.append(copy(a, 0, me, sibling, src=x_refs[a]))
            first += [copy(a, 1 + j, me, (*chip, c), src=x_refs[a]) for j, chip in enumerate(chips)]
        for cp in first:
            cp.start()
        passed = []
        for j, chip in enumerate(chips):
            for a in range(n):
                copy(a, 1 + j, (*chip, c), me).wait_recv()
                fwd = copy(a, 4 + j, (*chip, c), sibling)
                fwd.start()
                passed.append(fwd)
        for a in range(n):
            copy(a, 0, sibling, me).wait_recv()
            for j, chip in enumerate(chips):
                copy(a, 4 + j, (*chip, 1 - c), me).wait_recv()
        for cp in first + passed:
            cp.wait_send()
        for cp in mine:
            cp.wait()

    return pl.pallas_call(
        body, name=name,
        out_shape=[jax.ShapeDtypeStruct((s.shape[0], N_DEV) + s.shape[1:], s.dtype) for s in shards],
        in_specs=[_ANY] * n, out_specs=[_ANY] * n,
        scratch_shapes=[pltpu.SemaphoreType.DMA((7 * n,)), pltpu.SemaphoreType.DMA((7 * n,)),
                        pltpu.SemaphoreType.DMA((n,))],
    )(*shards)


def _exchange_shards(arrs, masks, src_of, out_tail, name):
    n, nm = len(arrs), len(masks)

    def body(*refs):
        in_refs, out_refs = refs[:n], refs[n:2 * n]
        send_sems, recv_sems = refs[2 * n:]
        x, y, c = _my_coords()
        copies = []
        for a in range(n):
            for k, (bx, by, bc) in enumerate(masks):
                peer = (_flip(x, bx), _flip(y, by), _flip(c, bc))
                copies.append(pltpu.make_async_remote_copy(
                    src_ref=src_of(in_refs[a], peer, (x, y, c)), dst_ref=out_refs[a].at[k],
                    send_sem=send_sems.at[nm * a + k], recv_sem=recv_sems.at[nm * a + k],
                    device_id=peer, device_id_type=MESH))
        for cp in copies:
            cp.start()
        for cp in copies:
            cp.wait()

    return pl.pallas_call(
        body, name=name,
        out_shape=[jax.ShapeDtypeStruct((nm,) + out_tail(s), s.dtype) for s in arrs],
        in_specs=[_ANY] * n, out_specs=[_ANY] * n,
        scratch_shapes=[pltpu.SemaphoreType.DMA((nm * n,)), pltpu.SemaphoreType.DMA((nm * n,))],
    )(*arrs)


def _pair_sum(g, from_sibling, core, out_dtype, name):
    L, _, _, a, b = g.shape
    ta = a

    def body(core_ref, g_ref, s_ref, o_ref):
        o_ref[...] = (g_ref[...] + s_ref[...]).astype(out_dtype)

    blk = pl.BlockSpec((1, 1, ta, b), lambda l, q, i, core_ref: (l, q, i, 0))
    return pl.pallas_call(
        body, name=name,
        grid_spec=pltpu.PrefetchScalarGridSpec(
            num_scalar_prefetch=1, grid=(L, 4, a // ta),
            in_specs=[pl.BlockSpec((1, 1, pl.Squeezed(), ta, b), lambda l, q, i, core_ref: (l, q, core_ref[0], i, 0)), blk],
            out_specs=blk),
        out_shape=jax.ShapeDtypeStruct((L, 4, a, b), out_dtype), compiler_params=_params())(core, g, from_sibling)


def _mods(mod_x, mod_c, layer):
    both = jnp.concatenate([mod_x[:, layer], mod_c[layer][None]], axis=0)
    return [both[:, None, k * D_MODEL:(k + 1) * D_MODEL] for k in range(6)]


def _local_step(x, ctx, target, mod_x, mod_c, w, small, late_weights=None, hooks=None):
    B, S, _ = x.shape
    L = ctx.shape[1]
    NX, NC = B * S, B * L
    T = NX + NC
    tiles_per_ex = S // ROW_TILE
    nxt = NX // ROW_TILE
    gidx = _group_index(nxt, tiles_per_ex, B)
    gidx_for = lambda rows: _group_index(NX // rows, S // rows, B)
    mm_rows = _tile(S, (MM_ROWS, ROW_TILE))
    tidx = lambda i: jnp.where(i < nxt, i % tiles_per_ex, tiles_per_ex)
    G = B + 1
    x0 = jnp.concatenate([x.reshape(NX, D_MODEL), ctx.reshape(NC, D_MODEL)], axis=0)
    acos, asin = [jnp.tile(t, (1, LANES // HEAD_DIM)) for t in _rope_tables(S, HEAD_DIM)]
    rcos, rsin = _rope_tables(S, RET_QK_DIM)
    sink = small["attn_sink"].reshape(N_KV_HEADS, GQA_GROUP)
    gains = jnp.stack([jnp.tile(small["attn_q_norm"].reshape(1, HEAD_DIM), (1, LANES // HEAD_DIM)),
                       jnp.tile(small["attn_k_norm"].reshape(1, HEAD_DIM), (1, LANES // HEAD_DIM))])
    log_g = jax.nn.log_sigmoid(small["ret_decay_logit"].reshape(2, RET_HEADS))
    n1, n2 = small["norm1_g"], small["norm2_g"]

    m0 = _mods(mod_x, mod_c, 0)
    h1 = _norm_mod_fwd(x0, n1[0:1], m0[0], m0[1], gidx, "l0_norm1")
    qkv = _mm(h1, w["attn_w_qkv"][0], "nn", F32, "l0_qkv")
    qkv_r = _attn_prep_fwd(qkv, gains, acos, asin, tidx, "l0_qk_prep")
    o_x = _attn_fwd(qkv_r, sink, B, S, L, False, "l0_attn_x")
    o_c = _attn_fwd(qkv_r, sink, B, S, L, True, "l0_attn_c")
    o0 = jnp.concatenate([o_x, o_c], axis=0)
    mo0, x1 = _mm(o0, w["attn_w_o"][0], "nn", F32, "l0_attn_out", res=x0, gate=m0[2], gidx_for=gidx_for, gate_rows=mm_rows)
    h2 = _norm_mod_fwd(x1, n2[0:1], m0[3], m0[4], gidx, "l0_norm2")
    if late_weights is not None:
        w = {**w, **late_weights(x1)}
    ug0, uu0, a0 = _ffn_in_swiglu(h2, w["ffn_w_in"][0], "l0_ffn_in")
    f0, x2 = _mm(a0, w["ffn_w_out"][0], "nn", F32, "l0_ffn_out", res=x1, gate=m0[5], gidx_for=gidx_for, gate_rows=mm_rows)

    m1 = _mods(mod_x, mod_c, 1)
    g1 = _norm_mod_fwd(x2, n1[1:2], m1[0], m1[1], gidx, "l1_norm1")
    qkvg = _mm(g1, w["ret_w_qkvg"][0], "nn", F32, "l1_qkvg")
    qk = _ret_rope(qkvg, rcos, rsin, tidx, "l1_rope")
    of, st_f, ob, st_b = _ret_fwd(qk, qkvg, log_g, B, S, L, "l1_ret")
    gn = w["ret_gn_g"].reshape(1, RET_VWIDTH)
    z1 = _gated_out_fwd(of, ob, qkvg, gn, "l1_gated_out")
    gx = lambda i: i // tiles_per_ex
    m1x = [t[:B] for t in m1]
    mo1, y1 = _mm(z1, w["ret_w_o"][0], "nn", F32, "l1_ret_out", res=x2, gate=m1x[2], gidx_for=gidx_for, gate_rows=mm_rows)
    k2 = _norm_mod_fwd(y1, n2[1:2], m1x[3], m1x[4], gx, "l1_norm2")
    ug1, uu1, a1 = _ffn_in_swiglu(k2, w["ffn_w_in"][1], "l1_ffn_in")
    f1, y2 = _mm(a1, w["ffn_w_out"][1], "nn", F32, "l1_ffn_out", res=y1, gate=m1x[5], gidx_for=gidx_for, gate_rows=mm_rows)

    loss_tile, dy2 = _loss_fwd_bwd(y2, target.reshape(NX, D_MODEL), "loss")

    zg = jnp.zeros((1, 1, D_MODEL), F32)
    dz, dgate5_1 = _gate_bwd(dy2, f1, m1x[5], gx, B, "l1_ffn_gate_bwd")
    gw_ffn_out1 = _mm(a1, dz, "tn", F32, "l1_ffn_out_dw")
    du = _ffn_out_bwd_swiglu(dz, w["ffn_w_out"][1], ug1, uu1, "l1_ffn_out_dx")
    gw_ffn_in1 = _mm(k2, du, "tn", F32, "l1_ffn_in_dw")
    dk2 = _mm(du, w["ffn_w_in"][1], "nt", F32, "l1_ffn_in_dx")
    dy1, dsh3_1, dsc4_1, dn2_1, dzo, dgate2_1 = _norm_mod_bwd(dk2, y1, n2[1:2], m1x[4], dy2, gx, B, "l1_norm2_bwd",
                                                              gated=(mo1, m1x[2]))
    gw_ret_o = _mm(z1, dzo, "tn", F32, "l1_ret_out_dw")
    dz1 = _mm(dzo, w["ret_w_o"][0], "nt", F32, "l1_ret_out_dx")
    do_r, dg_r, dgn = _gated_out_bwd(dz1, of, ob, qkvg, gn, "l1_gated_out_bwd")
    ((dq_f, dk_f, dv_f, dkc_f, dvc_f, dlg_f),
     (dq_b, dk_b, dv_b, dkc_b, dvc_b, dlg_b)) = _ret_bwd(qk, qkvg, log_g, st_f, st_b, do_r, B, S, L, "l1_ret_bwd")
    dqkvg = _ret_grad_assemble((dq_f, dq_b, dk_f, dk_b, dv_f, dv_b), (dkc_f, dkc_b, dvc_f, dvc_b), dg_r, rcos, rsin, S,
                               "l1_qkvg_grad")
    gw_ret_qkvg = _mm(g1, dqkvg, "tn", F32, "l1_qkvg_dw")
    grads_layer1 = {
        "ffn_w_in": gw_ffn_in1[None],
        "ffn_w_out": gw_ffn_out1[None],
        "ret_w_qkvg": gw_ret_qkvg[None],
        "ret_gn_g": dgn.reshape(1, 1, RET_VWIDTH),
        "ret_w_o": gw_ret_o[None],
    }
    if hooks is not None:
        m0[5] = hooks.layer1_grads(grads_layer1, m0[5])
    dg1 = _mm(dqkvg, w["ret_w_qkvg"][0], "nt", F32, "l1_qkvg_dx")
    dx2, dsh0_1, dsc1_1, dn1_1, dz, dgate5_0 = _norm_mod_bwd(dg1, x2, n1[1:2], m1[1], dy1, gidx, G, "l1_norm1_bwd",
                                                             gated=(f0, m0[5]))
    dlg = jnp.stack([jnp.sum(dlg_f[:, :, 0, 0], axis=0), jnp.sum(dlg_b[:, :, 0, 0], axis=0)])
    d_decay = (dlg * jax.nn.sigmoid(-small["ret_decay_logit"].reshape(2, RET_HEADS))).reshape(1, 2, RET_HEADS)

    gw_ffn_out0 = _mm(a0, dz, "tn", F32, "l0_ffn_out_dw")
    du = _ffn_out_bwd_swiglu(dz, w["ffn_w_out"][0], ug0, uu0, "l0_ffn_out_dx")
    if hooks is not None:
        m0[4] = hooks.mid_ffn0_backward(du, m0[4])
    gw_ffn_in0 = _mm(h2, du, "tn", F32, "l0_ffn_in_dw")
    if hooks is not None:
        m0[2] = hooks.ffn0_grads({"ffn_w_in": gw_ffn_in0[None], "ffn_w_out": gw_ffn_out0[None]}, m0[2])
    dh2 = _mm(du, w["ffn_w_in"][0], "nt", F32, "l0_ffn_in_dx")
    dx1, dsh3_0, dsc4_0, dn2_0, dzo, dgate2_0 = _norm_mod_bwd(dh2, x1, n2[0:1], m0[4], dx2, gidx, G, "l0_norm2_bwd",
                                                              gated=(mo0, m0[2]))
    gw_attn_o = _mm(o0, dzo, "tn", F32, "l0_attn_out_dw")
    do0 = _mm(dzo, w["attn_w_o"][0], "nt", MXU_DTYPE, "l0_attn_out_dx")
    dq_x, dk_x, dv_x, dkc1, dvc1, dsink_x = _attn_bwd(qkv_r, sink, do0, B, S, L, False, "l0_attn_x_bwd")
    dq_c, dkc2, dvc2, dsink_c = _attn_bwd(qkv_r, sink, do0, B, S, L, True, "l0_attn_c_bwd")
    if hooks is not None:
        gains = hooks.after_attn_backward(dq_x, gains)
    dqkv, dgains = _attn_prep_bwd((dq_x, dk_x, dv_x), (dq_c, dkc1, dkc2, dvc1, dvc2), qkv, gains, acos, asin, tidx,
                                  "l0_qk_prep_bwd")
    gw_attn_qkv = _mm(h1, dqkv, "tn", F32, "l0_qkv_dw")
    dh1 = _mm(dqkv, w["attn_w_qkv"][0], "nt", F32, "l0_qkv_dx")
    dx0, dsh0_0, dsc1_0, dn1_0 = _norm_mod_bwd(dh1, x0, n1[0:1], m0[1], dx1, gidx, G, "l0_norm1_bwd")

    dgains = jnp.sum(dgains.reshape(ATTN_QK_BLOCKS, LANES // HEAD_DIM, HEAD_DIM), axis=1)
    dsink = (dsink_x + dsink_c).reshape(N_KV_HEADS, 8, LANES)[:, :GQA_GROUP, 0].reshape(1, N_HEADS)
    grads_layer0 = {
        "ffn_w_in": gw_ffn_in0[None],
        "ffn_w_out": gw_ffn_out0[None],
        "attn_w_qkv": gw_attn_qkv[None],
        "attn_w_o": gw_attn_o[None],
    }
    grads_small = {
        "norm1_g": jnp.concatenate([dn1_0, dn1_1], axis=0),
        "norm2_g": jnp.concatenate([dn2_0, dn2_1], axis=0),
        "attn_q_norm": jnp.sum(dgains[:ATTN_Q_BLOCKS], axis=0)[None],
        "attn_k_norm": jnp.sum(dgains[ATTN_Q_BLOCKS:ATTN_QK_BLOCKS], axis=0)[None],
        "attn_sink": dsink,
        "ret_decay_logit": d_decay,
    }

    def pad_g(t):
        return jnp.concatenate([t, zg], axis=0)

    d0 = jnp.concatenate([dsh0_0, dsc1_0, dgate2_0, dsh3_0, dsc4_0, dgate5_0], axis=2)[:, 0]
    d1 = jnp.concatenate([dsh0_1, dsc1_1, pad_g(dgate2_1), pad_g(dsh3_1), pad_g(dsc4_1), pad_g(dgate5_1)],
                         axis=2)[:, 0]
    dmod_x = jnp.stack([d0[:B], d1[:B]], axis=1)
    dmod_c = jnp.stack([d0[B], d1[B]], axis=0)
    return loss_tile, dx0[:NX].reshape(B, S, D_MODEL), (grads_layer0, grads_layer1), grads_small, dmod_x, dmod_c


SMALL_NAMES = ("c_ctx", "ada_b", "norm1_g", "norm2_g", "attn_q_norm", "attn_k_norm", "attn_sink", "ret_decay_logit")
ADA_ROWS = 64


def _pack_small(d, rows):
    flat = jnp.concatenate([d[k].reshape(-1) for k in SMALL_NAMES])
    n = rows * LANES
    return jnp.pad(flat, (0, n - flat.shape[0])).reshape(rows, LANES)


def _unpack_small(packed, shapes):
    flat = packed.reshape(-1)
    out, off = {}, 0
    for k in SMALL_NAMES:
        n = math.prod(shapes[k])
        out[k] = flat[off:off + n].reshape(shapes[k])
        off += n
    return out


EARLY_WEIGHTS = ("attn_w_qkv", "attn_w_o")
LATE_WEIGHTS = tuple(k for k in BIG_WEIGHTS if k not in EARLY_WEIGHTS)

_HBM = pl.BlockSpec(memory_space=pltpu.HBM)
_SEM = pl.BlockSpec(memory_space=pltpu.SEMAPHORE)
_DATAFLOW = pltpu.SideEffectType.DATAFLOW_SIDE_EFFECTING
_PEER_FLIPS = ((0, 0, 1), (0, 1, 0), (0, 1, 1), (1, 0, 0), (1, 0, 1), (1, 1, 0), (1, 1, 1))


def _wire_shard(name, t):
    return t.reshape(1, 1, -1) if name == "ret_gn_g" else t.astype(MXU_DTYPE)


def _direct_copies(x_refs, land_refs, send_sems, recv_sems, landing):
    x, y, c = _my_coords()
    out = []
    for a in range(len(x_refs)):
        for k, (bx, by, bc) in enumerate(_PEER_FLIPS):
            peer = (_flip(x, bx), _flip(y, by), _flip(c, bc))
            slot = (4 * peer[0] + 2 * peer[1] + peer[2]) if landing else (4 * x + 2 * y + c)
            out.append(pltpu.make_async_remote_copy(
                src_ref=x_refs[a], dst_ref=land_refs[a].at[:, slot], send_sem=send_sems.at[7 * a + k],
                recv_sem=recv_sems.at[7 * a + k], device_id=peer, device_id_type=MESH))
    return out


def _gather_start(shards, name):
    n = len(shards)
    lands = [lax.empty((s.shape[0], N_DEV) + s.shape[1:], s.dtype) for s in shards]

    def body(*refs):
        send_sems, recv_sems = refs[2 * n], refs[2 * n + 1]
        x_refs, land_refs = refs[2 * n + 2:3 * n + 2], refs[3 * n + 2:4 * n + 2]
        for cp in _direct_copies(x_refs, land_refs, send_sems, recv_sems, landing=False):
            cp.start()
        refs[-1][...] = jnp.zeros_like(refs[-1])

    hbm = lambda t: pltpu.with_memory_space_constraint(t, pltpu.HBM)
    res = pl.pallas_call(
        body, name=name,
        out_shape=(pltpu.SemaphoreType.DMA((7 * n,)), pltpu.SemaphoreType.DMA((7 * n,)))
        + tuple(pltpu.HBM(t.shape, t.dtype) for t in shards + lands) + (jax.ShapeDtypeStruct((8, LANES), F32),),
        in_specs=[_HBM] * (2 * n), out_specs=(_SEM, _SEM) + (_HBM,) * (2 * n) + (pl.BlockSpec(memory_space=pltpu.VMEM),),
        input_output_aliases={i: 2 + i for i in range(2 * n)},
        compiler_params=pltpu.CompilerParams(has_side_effects=_DATAFLOW))(*[hbm(t) for t in shards + lands])
    return res[0], res[1], list(res[2:2 + n]), list(res[2 + n:2 + 2 * n]), res[-1]


def _gather_wait(send_sems, recv_sems, shards, lands, after, name):
    n = len(shards)

    def body(*refs):
        x_refs, land_refs = refs[:n], refs[n:2 * n]
        for cp in _direct_copies(x_refs, land_refs, refs[2 * n], refs[2 * n + 1], landing=True):
            cp.wait_send()
            cp.wait_recv()

    res = pl.pallas_call(
        body, name=name, out_shape=tuple(pltpu.HBM(t.shape, t.dtype) for t in shards + lands),
        in_specs=[_HBM] * (2 * n) + [_SEM, _SEM, _ANY], out_specs=(_HBM,) * (2 * n),
        input_output_aliases={i: i for i in range(2 * n)},
        compiler_params=pltpu.CompilerParams(has_side_effects=_DATAFLOW))(*shards, *lands, send_sems, recv_sems, after)
    return list(res[n:])


def _gather_big_weights(weights, names, name):
    gathered = _gather_shards([_wire_shard(k, weights[k]) for k in names], name)
    return {k: _join_shards(k, g) for k, g in zip(names, gathered)}


_SIBLING = ((0, 0, 1),)
_CHIPS = ((1, 0, 0), (0, 1, 0), (1, 1, 0))
_to_sibling = lambda ref, peer: ref.at[:, :, peer[2]]
_to_chip = lambda ref, peer: ref.at[:, 2 * peer[0] + peer[1]]
_sibling_tail = lambda s: (s.shape[0], 4) + s.shape[3:]
_chip_tail = lambda s: (s.shape[0],) + s.shape[2:]


def _rs_split(grads):
    names = list(grads)
    split = []
    for k in names:
        s = _split_shards(k, grads[k])
        split.append(s.reshape(s.shape[0], 4, 2, s.shape[2], s.shape[3]))
    return names, split


def _rs_pair_sums(names, split, from_sibling, tag):
    core = lax.axis_index("c").astype(jnp.int32).reshape(1)
    return [_pair_sum(g, s, core, MXU_DTYPE, tag + k) for k, g, s in zip(names, split, from_sibling)]


def _rs_parts(names, split, from_sibling, from_chips):
    mx_, my_, mc_ = _my_coords()
    my_chip = 2 * mx_ + my_
    parts = {}
    for k, g, s, r in zip(names, split, from_sibling, from_chips):
        own_keep = lax.dynamic_index_in_dim(lax.dynamic_index_in_dim(g, my_chip, axis=1, keepdims=False), mc_, axis=1,
                                            keepdims=False)
        parts[k] = (own_keep, lax.dynamic_index_in_dim(s, my_chip, axis=1, keepdims=False), r)
    return parts


def _reduce_scatter_in_call(grads, tag):
    names, split = _rs_split(grads)
    from_sibling = [t[0] for t in _exchange_shards(split, _SIBLING, lambda ref, peer, me_: _to_sibling(ref, peer),
                                                   _sibling_tail, tag + "sibling")]
    pair = _rs_pair_sums(names, split, from_sibling, tag + "pair_")
    from_chips = _exchange_shards(pair, _CHIPS, lambda ref, peer, me_: _to_chip(ref, peer), _chip_tail, tag + "chips")
    return _rs_parts(names, split, from_sibling, from_chips)


def _exchange_copies(in_refs, land_refs, send_sems, recv_sems, masks, src_of):
    x, y, c = _my_coords()
    nm = len(masks)
    out = []
    for a in range(len(in_refs)):
        for k, (bx, by, bc) in enumerate(masks):
            peer = (_flip(x, bx), _flip(y, by), _flip(c, bc))
            out.append(pltpu.make_async_remote_copy(
                src_ref=src_of(in_refs[a], peer), dst_ref=land_refs[a].at[k], send_sem=send_sems.at[nm * a + k],
                recv_sem=recv_sems.at[nm * a + k], device_id=peer, device_id_type=MESH))
    return out


def _exchange_start(arrs, masks, src_of, out_tail, name):
    n, nm = len(arrs), len(masks)
    lands = [lax.empty((nm,) + out_tail(s), s.dtype) for s in arrs]

    def body(*refs):
        send_sems, recv_sems = refs[2 * n], refs[2 * n + 1]
        in_refs, land_refs = refs[2 * n + 2:3 * n + 2], refs[3 * n + 2:4 * n + 2]
        for cp in _exchange_copies(in_refs, land_refs, send_sems, recv_sems, masks, src_of):
            cp.start()
        refs[-1][...] = jnp.zeros_like(refs[-1])

    hbm = lambda t: pltpu.with_memory_space_constraint(t, pltpu.HBM)
    res = pl.pallas_call(
        body, name=name,
        out_shape=(pltpu.SemaphoreType.DMA((nm * n,)), pltpu.SemaphoreType.DMA((nm * n,)))
        + tuple(pltpu.HBM(t.shape, t.dtype) for t in list(arrs) + lands) + (jax.ShapeDtypeStruct((8, LANES), F32),),
        in_specs=[_HBM] * (2 * n), out_specs=(_SEM, _SEM) + (_HBM,) * (2 * n) + (pl.BlockSpec(memory_space=pltpu.VMEM),),
        input_output_aliases={i: 2 + i for i in range(2 * n)},
        compiler_params=pltpu.CompilerParams(has_side_effects=_DATAFLOW))(*[hbm(t) for t in list(arrs) + lands])
    return (res[0], res[1], list(res[2:2 + n]), list(res[2 + n:2 + 2 * n]), masks, src_of), res[-1]


def _exchange_wait(state, after, name):
    send_sems, recv_sems, arrs, lands, masks, src_of = state
    n = len(arrs)

    def body(*refs):
        for cp in _exchange_copies(refs[:n], refs[n:2 * n], refs[2 * n], refs[2 * n + 1], masks, src_of):
            cp.wait_send()
            cp.wait_recv()

    res = pl.pallas_call(
        body, name=name, out_shape=tuple(pltpu.HBM(t.shape, t.dtype) for t in arrs + lands),
        in_specs=[_HBM] * (2 * n) + [_SEM, _SEM, _ANY], out_specs=(_HBM,) * (2 * n),
        input_output_aliases={i: i for i in range(2 * n)},
        compiler_params=pltpu.CompilerParams(has_side_effects=_DATAFLOW))(*arrs, *lands, send_sems, recv_sems, after)
    return list(res[:n]), list(res[n:])


class _SplitReduce:
    def __init__(self, tag):
        self.tag = tag

    def start(self, grads, order_through):
        self.names, split = _rs_split(grads)
        self.sibling, tok = _exchange_start(split, _SIBLING, _to_sibling, _sibling_tail, self.tag + "sibling_start")
        return order_through + tok[0, 0]

    def middle(self, after, order_through):
        self.split, lands = _exchange_wait(self.sibling, after, self.tag + "sibling_wait")
        self.from_sibling = [t[0] for t in lands]
        pair = _rs_pair_sums(self.names, self.split, self.from_sibling, self.tag + "pair_")
        self.chips, tok = _exchange_start(pair, _CHIPS, _to_chip, _chip_tail, self.tag + "chips_start")
        return order_through + tok[0, 0]

    def finish(self, after):
        _, from_chips = _exchange_wait(self.chips, after, self.tag + "chips_wait")
        return _rs_parts(self.names, self.split, self.from_sibling, from_chips)


def _adamw_big(weights, mom1, mom2, part_groups):
    big = {}
    for k in BIG_WEIGHTS:
        parts = [g[k] for g in part_groups if k in g]
        own_keep = jnp.concatenate([p[0] for p in parts], axis=0)
        own_sib = jnp.concatenate([p[1] for p in parts], axis=0)
        recv = jnp.concatenate([p[2] for p in parts], axis=1)
        L_, a_, b_ = own_keep.shape
        rows = L_ * a_
        res = _adamw(weights[k].reshape(rows, b_), mom1[k].reshape(rows, b_), mom2[k].reshape(rows, b_),
                     [own_keep.reshape(1, rows, b_), own_sib.reshape(1, rows, b_), recv.reshape(3, rows, b_)],
                     "adamw_" + k)
        big[k] = [t.reshape(weights[k].shape) for t in res]
    return big


def kernel(x, c, ctx, c_ctx, ada_w, ada_b, norm1_g, norm2_g, ffn_w_in, ffn_w_out, attn_w_qkv, attn_q_norm, attn_k_norm, attn_sink, attn_w_o, ret_w_qkvg, ret_decay_logit, ret_gn_g, ret_w_o, loss_target, m_c_ctx, m_ada_w, m_ada_b, m_norm1_g, m_norm2_g, m_ffn_w_in, m_ffn_w_out, m_attn_w_qkv, m_attn_q_norm, m_attn_k_norm, m_attn_sink, m_attn_w_o, m_ret_w_qkvg, m_ret_decay_logit, m_ret_gn_g, m_ret_w_o, v_c_ctx, v_ada_w, v_ada_b, v_norm1_g, v_norm2_g, v_ffn_w_in, v_ffn_w_out, v_attn_w_qkv, v_attn_q_norm, v_attn_k_norm, v_attn_sink, v_attn_w_o, v_ret_w_qkvg, v_ret_decay_logit, v_ret_gn_g, v_ret_w_o):
    weights = dict(c_ctx=c_ctx, ada_w=ada_w, ada_b=ada_b, norm1_g=norm1_g, norm2_g=norm2_g, ffn_w_in=ffn_w_in,
                   ffn_w_out=ffn_w_out, attn_w_qkv=attn_w_qkv, attn_q_norm=attn_q_norm, attn_k_norm=attn_k_norm,
                   attn_sink=attn_sink, attn_w_o=attn_w_o, ret_w_qkvg=ret_w_qkvg, ret_decay_logit=ret_decay_logit,
                   ret_gn_g=ret_gn_g, ret_w_o=ret_w_o)
    mom1 = dict(c_ctx=m_c_ctx, ada_w=m_ada_w, ada_b=m_ada_b, norm1_g=m_norm1_g, norm2_g=m_norm2_g, ffn_w_in=m_ffn_w_in,
                ffn_w_out=m_ffn_w_out, attn_w_qkv=m_attn_w_qkv, attn_q_norm=m_attn_q_norm, attn_k_norm=m_attn_k_norm,
                attn_sink=m_attn_sink, attn_w_o=m_attn_w_o, ret_w_qkvg=m_ret_w_qkvg, ret_decay_logit=m_ret_decay_logit,
                ret_gn_g=m_ret_gn_g, ret_w_o=m_ret_w_o)
    mom2 = dict(c_ctx=v_c_ctx, ada_w=v_ada_w, ada_b=v_ada_b, norm1_g=v_norm1_g, norm2_g=v_norm2_g, ffn_w_in=v_ffn_w_in,
                ffn_w_out=v_ffn_w_out, attn_w_qkv=v_attn_w_qkv, attn_q_norm=v_attn_q_norm, attn_k_norm=v_attn_k_norm,
                attn_sink=v_attn_sink, attn_w_o=v_attn_w_o, ret_w_qkvg=v_ret_w_qkvg, ret_decay_logit=v_ret_decay_logit,
                ret_gn_g=v_ret_gn_g, ret_w_o=v_ret_w_o)
    B = x.shape[0]
    mx_, my_, mc_ = _my_coords()
    me = 4 * mx_ + 2 * my_ + mc_
    ada_cols = ada_w.shape[2]

    w_full = _gather_big_weights(weights, EARLY_WEIGHTS, "gather_early")

    c_all = _all_gather(jax.nn.silu(c), "gather_c").reshape(N_DEV * B, D_MODEL)
    cc_act = jax.nn.silu(c_ctx)[None]
    ada_in = jnp.concatenate([c_all, cc_act, jnp.zeros((ADA_ROWS - N_DEV * B - 1, D_MODEL), F32)], axis=0)
    ada_in = ada_in.astype(MXU_DTYPE)
    ada_w2 = jnp.concatenate([ada_w[0], ada_w[1]], axis=1)
    bias = lax.dynamic_slice_in_dim(ada_b.reshape(2, N_DEV, ada_cols), me, 1, axis=1).reshape(1, 2 * ada_cols)
    mod_cols = _mm(ada_in, ada_w2, "nn", F32, "ada_fwd", bias=bias)
    mod_all = _all_gather(mod_cols, "gather_mod")
    mod_all = mod_all.reshape(N_DEV, ADA_ROWS, 2, ada_cols).transpose(1, 2, 0, 3).reshape(ADA_ROWS, 2, N_DEV * ada_cols)
    mod_x = lax.dynamic_slice_in_dim(mod_all, me * B, B, axis=0)
    mod_c = mod_all[N_DEV * B]

    order = 0.0 * (mod_c[0, 0] + w_full["attn_w_o"][0, 0, 0].astype(F32))
    late_shards = [_wire_shard(k, weights[k] + order if k == "ret_gn_g" else weights[k]) for k in LATE_WEIGHTS]
    send_sems, recv_sems, late_thru, late_lands, token = _gather_start(late_shards, "gather_late_start")
    mod_x = mod_x + token[0, 0]

    def late_weights(after):
        lands = _gather_wait(send_sems, recv_sems, late_thru, late_lands, after, "gather_late_wait")
        own = [lax.dynamic_update_index_in_dim(land, shard, me, axis=1) for land, shard in zip(lands, late_shards)]
        return {k: _join_shards(k, g) for k, g in zip(LATE_WEIGHTS, own)}

    rs_layer1, rs_ffn0 = _SplitReduce("rs1_"), _SplitReduce("rs0_")

    class Hooks:
        layer1_grads = rs_layer1.start
        mid_ffn0_backward = rs_layer1.middle
        ffn0_grads = rs_ffn0.start
        after_attn_backward = rs_ffn0.middle

    small = {k: weights[k] for k in SMALL_NAMES}
    loss_tile, grad_x, (g_layer0, _), g_small, dmod_x, dmod_c = _local_step(
        x, ctx, loss_target, mod_x, mod_c, w_full, small, late_weights, Hooks)
    parts1 = rs_layer1.finish(grad_x)
    parts0_ffn = rs_ffn0.finish(grad_x)
    loss = lax.psum(loss_tile[0, 0], ("x", "y", "c"))

    n_mod = 2 * 6 * D_MODEL
    dm_rows = jnp.concatenate([dmod_x.reshape(B, n_mod), dmod_c.reshape(1, n_mod),
                               jnp.zeros((8 - B - 1, n_mod), F32)], axis=0)
    dm_all = _all_gather(dm_rows, "gather_dmod")
    dmc_tot = _sum_rows(dm_all[:, B:B + 1].reshape(N_DEV, 1, n_mod)[:, :, :].reshape(N_DEV, n_mod // LANES, LANES),
                        "sum_dmod_c").reshape(1, n_mod)
    dmod_rows = jnp.concatenate([dm_all[:, :B].reshape(N_DEV * B, n_mod), dmc_tot,
                                 jnp.zeros((ADA_ROWS - N_DEV * B - 1, n_mod), F32)], axis=0)
    dmod_mine = lax.dynamic_slice_in_dim(dmod_rows.reshape(ADA_ROWS, 2, N_DEV, ada_cols), me, 1, axis=2)
    dmod_mine = dmod_mine.reshape(ADA_ROWS, 2 * ada_cols).astype(MXU_DTYPE)
    g_ada2 = _mm(ada_in, dmod_mine, "tn", F32, "ada_dw")
    g_ada_w = jnp.stack([g_ada2[:, :ada_cols], g_ada2[:, ada_cols:]])
    dmc_mine = jnp.concatenate([dmod_mine[N_DEV * B:N_DEV * B + 1], jnp.zeros((7, 2 * ada_cols), MXU_DTYPE)], axis=0)
    dcc_part = _mm(dmc_mine, ada_w2, "nt", F32, "ada_dc")[0:1]
    g_ada_b = _sum_rows(dmod_rows[:, None, :].reshape(ADA_ROWS, n_mod // LANES, LANES), "sum_dmod_b").reshape(2, 6 * D_MODEL)
    sg = jax.nn.sigmoid(c_ctx)
    g_small["c_ctx"] = dcc_part.reshape(D_MODEL) * (sg * (1.0 + c_ctx * (1.0 - sg)))
    g_small["ada_b"] = g_ada_b * (1.0 / N_DEV)

    shapes = {k: weights[k].shape for k in SMALL_NAMES}
    n_small = sum(math.prod(s) for s in shapes.values())
    srows = -(-(-(-n_small // LANES)) // 8) * 8
    gs_all = _all_gather(_pack_small(g_small, srows), "gather_small_grads")
    sm = _adamw(_pack_small({k: weights[k] for k in SMALL_NAMES}, srows), _pack_small({k: mom1[k] for k in SMALL_NAMES}, srows),
                _pack_small({k: mom2[k] for k in SMALL_NAMES}, srows), [gs_all], "adamw_small")
    sm = [_unpack_small(t, shapes) for t in sm]

    ada_shape = ada_w.shape
    r2 = lambda t: t.reshape(ada_shape[0] * ada_shape[1], ada_shape[2])
    ada = [t.reshape(ada_shape) for t in _adamw(r2(ada_w), r2(m_ada_w), r2(v_ada_w), [r2(g_ada_w)[None]], "adamw_ada")]

    attn_grads = {k: g_layer0[k] for k in EARLY_WEIGHTS}
    big = _adamw_big(weights, mom1, mom2, [_reduce_scatter_in_call(attn_grads, "rs_"), parts0_ffn, parts1])

    def pick(i, name):
        if name in BIG_WEIGHTS:
            return big[name][i]
        if name == "ada_w":
            return ada[i]
        return sm[i][name]

    order = ("c_ctx", "ada_w", "ada_b", "norm1_g", "norm2_g", "ffn_w_in", "ffn_w_out", "attn_w_qkv", "attn_q_norm",
             "attn_k_norm", "attn_sink", "attn_w_o", "ret_w_qkvg", "ret_decay_logit", "ret_gn_g", "ret_w_o")
    outs = [loss, grad_x]
    for i in range(4):
        outs += [pick(i, n) for n in order]
    return tuple(outs)
```

```python
import functools
import math

import jax
import jax.numpy as jnp
from jax import lax
from jax.experimental import pallas as pl
from jax.experimental.pallas import tpu as pltpu

F32 = jnp.float32
MXU_DTYPE = jnp.bfloat16

D_MODEL = 1024
HEAD_DIM = 64
N_HEADS = 16
N_KV_HEADS = 4
GQA_GROUP = 4
WINDOW = 128
ATTN_BLOCK = 128
RET_HEADS = 4
RET_QK_DIM = 256
RET_V_DIM = 512
RET_VWIDTH = 2048
RET_CHUNK = 512
D_FF = 2816
GRID_W = 64
ROPE_BASE = 10000.0
EPS = 1e-6
NEG_INF = -1e30

ADAM_LR = 0.001
ADAM_B1 = 0.9
ADAM_B2 = 0.999
ADAM_EPS = 1e-08
ADAM_WD = 0.01
ADAM_STEP = 10

N_DEV = 8
LANES = 128
ROW_TILE = 512
VMEM_LIMIT = 48 * 1024 * 1024

MESH = pl.DeviceIdType.MESH
_ANY = pl.BlockSpec(memory_space=pl.ANY)
_SMEM = pl.BlockSpec(memory_space=pltpu.SMEM)


def _params(**kw):
    return pltpu.CompilerParams(vmem_limit_bytes=VMEM_LIMIT, **kw)


def _mx(v):
    return v.astype(MXU_DTYPE)


def _dot(a, b, dims):
    return lax.dot_general(_mx(a), _mx(b), (dims, ((), ())), preferred_element_type=F32)


_NN = ((1,), (0,))
_NT = ((1,), (1,))
_TN = ((0,), (0,))


def _tile(n, cands):
    for c in cands:
        if n % c == 0:
            return c
    return n


def _big_tile(n, cap):
    if n <= cap:
        return n
    for t in range(cap - cap % LANES, 0, -LANES):
        if n % t == 0:
            return t
    return n


MM_ROWS = 1024
MM_COLS = 1408
MM_DEPTH = 2048


def _k_tile(k):
    return _big_tile(k, MM_DEPTH)


def _mm(a, b, mode, out_dtype, name, *, bias=None, res=None, gate=None, gidx_for=None, gate_rows=None):
    if mode == "nn":
        (M, K), (_, N) = a.shape, b.shape
    elif mode == "nt":
        (M, K), (N, _) = a.shape, b.shape
    else:
        (K, M), (_, N) = a.shape, b.shape
    if res is not None:
        tm, tn = gate_rows, _big_tile(N, 512)
        gidx = gidx_for(tm)
    else:
        tm = _big_tile(M, MM_COLS if mode == "tn" else MM_ROWS)
        tn = _big_tile(N, MM_COLS)
    tk = _k_tile(K)
    nk = K // tk
    dims = {"nn": _NN, "nt": _NT, "tn": _TN}[mode]
    a_spec = (pl.BlockSpec((tk, tm), lambda i, j, k: (k, i)) if mode == "tn"
              else pl.BlockSpec((tm, tk), lambda i, j, k: (i, k)))
    b_spec = (pl.BlockSpec((tn, tk), lambda i, j, k: (j, k)) if mode == "nt"
              else pl.BlockSpec((tk, tn), lambda i, j, k: (k, j)))
    o_spec = pl.BlockSpec((tm, tn), lambda i, j, k: (i, j))
    in_specs, operands = [a_spec, b_spec], [a, b]
    if bias is not None:
        in_specs.append(pl.BlockSpec((1, tn), lambda i, j, k: (0, j)))
        operands.append(bias)
    if res is not None:
        in_specs += [o_spec, pl.BlockSpec((1, 1, tn), lambda i, j, k: (gidx(i), 0, j))]
        operands += [res, gate]
        out_shape = (jax.ShapeDtypeStruct((M, N), F32), jax.ShapeDtypeStruct((M, N), F32))
        out_specs = (o_spec, o_spec)
    else:
        out_shape = jax.ShapeDtypeStruct((M, N), out_dtype)
        out_specs = o_spec

    def body(*refs):
        a_ref, b_ref = refs[0], refs[1]
        extra = refs[2:len(operands)]
        outs = refs[len(operands):]
        prod = _dot(a_ref[...], b_ref[...], dims)

        def finish(acc):
            if bias is not None:
                outs[0][...] = (acc + extra[0][...]).astype(out_dtype)
            elif res is not None:
                outs[0][...] = acc
                outs[1][...] = extra[0][...] + extra[1][0] * acc
            else:
                outs[0][...] = acc.astype(out_dtype)

        if nk == 1:
            finish(prod)
        else:
            acc_ref = outs[-1]
            outs = outs[:-1]
            k = pl.program_id(2)

            @pl.when(k == 0)
            def _():
                acc_ref[...] = prod

            @pl.when(k > 0)
            def _():
                acc_ref[...] += prod

            @pl.when(k == nk - 1)
            def _():
                finish(acc_ref[...])

    return pl.pallas_call(
        body, name=name, grid=(M // tm, N // tn, nk), in_specs=in_specs, out_specs=out_specs, out_shape=out_shape,
        scratch_shapes=[pltpu.VMEM((tm, tn), F32)] if nk > 1 else [],
        compiler_params=_params())(*operands)


def _group_index(n_x_tiles, tiles_per_example, n_examples):
    def gidx(i):
        return jnp.where(i < n_x_tiles, i // tiles_per_example, n_examples)
    return gidx


def _norm_mod_fwd(x, g, shift, scale, gidx, name):
    T, Dm = x.shape

    def body(x_ref, g_ref, sh_ref, sc_ref, h_ref):
        xv = x_ref[...]
        r = lax.rsqrt(jnp.mean(xv * xv, axis=-1, keepdims=True) + EPS)
        y = xv * r * g_ref[...]
        h_ref[...] = (y * (1.0 + sc_ref[0]) + sh_ref[0]).astype(h_ref.dtype)

    row = pl.BlockSpec((ROW_TILE, Dm), lambda i: (i, 0))
    mod = pl.BlockSpec((1, 1, Dm), lambda i: (gidx(i), 0, 0))
    return pl.pallas_call(
        body, name=name, grid=(T // ROW_TILE,),
        in_specs=[row, pl.BlockSpec((1, Dm), lambda i: (0, 0)), mod, mod],
        out_specs=row, out_shape=jax.ShapeDtypeStruct((T, Dm), MXU_DTYPE),
        compiler_params=_params())(x, g, shift, scale)


def _first_of_group(i, gidx):
    return jnp.logical_or(i == 0, gidx(i) != gidx(jnp.maximum(i - 1, 0)))


def _norm_mod_bwd(dh, x, g, scale, dres, gidx, n_groups, name, gated=None, dx_rows=None):
    T, Dm = x.shape
    res_tiles = dres.shape[0] // ROW_TILE
    dx_tiles = (dx_rows or T) // ROW_TILE

    def body(*refs):
        dh_ref, x_ref, g_ref, sc_ref, dres_ref = refs[:5]
        n_in = 7 if gated else 5
        dx_ref, dsh_ref, dsc_ref, dg_ref = refs[n_in:n_in + 4]
        i = pl.program_id(0)
        xv, dhv = x_ref[...], dh_ref[...]
        r = lax.rsqrt(jnp.mean(xv * xv, axis=-1, keepdims=True) + EPS)
        xn = xv * r
        y = xn * g_ref[...]

        @pl.when(_first_of_group(i, gidx))
        def _():
            dsh_ref[...] = jnp.zeros_like(dsh_ref)
            dsc_ref[...] = jnp.zeros_like(dsc_ref)

        @pl.when(i == 0)
        def _():
            dg_ref[...] = jnp.zeros_like(dg_ref)

        dsh_ref[0] += jnp.sum(dhv, axis=0, keepdims=True)
        dsc_ref[0] += jnp.sum(dhv * y, axis=0, keepdims=True)
        dy = dhv * (1.0 + sc_ref[0])
        dg_ref[...] += jnp.sum(dy * xn, axis=0, keepdims=True)
        dxn = dy * g_ref[...]
        dx = r * (dxn - xn * jnp.mean(dxn * xn, axis=-1, keepdims=True))
        dx = dx + (dres_ref[...] if res_tiles == T // ROW_TILE else jnp.where(i < res_tiles, dres_ref[...], 0.0))
        if dx_tiles == T // ROW_TILE:
            dx_ref[...] = dx
        else:
            @pl.when(i < dx_tiles)
            def _():
                dx_ref[...] = dx
        if gated:
            f_ref, gate_ref = refs[5:7]
            dz_ref, dgate_ref = refs[n_in + 4:]

            @pl.when(_first_of_group(i, gidx))
            def _():
                dgate_ref[...] = jnp.zeros_like(dgate_ref)

            dgate_ref[0] += jnp.sum(dx * f_ref[...], axis=0, keepdims=True)
            dz_ref[...] = (dx * gate_ref[0]).astype(dz_ref.dtype)

    row = pl.BlockSpec((ROW_TILE, Dm), lambda i: (i, 0))
    mod = pl.BlockSpec((1, 1, Dm), lambda i: (gidx(i), 0, 0))
    vec = pl.BlockSpec((1, Dm), lambda i: (0, 0))
    mod_shape = jax.ShapeDtypeStruct((n_groups, 1, Dm), F32)
    res_row = pl.BlockSpec((ROW_TILE, Dm), lambda i: (jnp.minimum(i, res_tiles - 1), 0))
    in_specs, operands = [row, row, vec, mod, res_row], [dh, x, g, scale, dres]
    dx_row = pl.BlockSpec((ROW_TILE, Dm), lambda i: (jnp.minimum(i, dx_tiles - 1), 0))
    out_specs = [dx_row, mod, mod, vec]
    out_shape = [jax.ShapeDtypeStruct((dx_tiles * ROW_TILE, Dm), F32), mod_shape, mod_shape,
                 jax.ShapeDtypeStruct((1, Dm), F32)]
    if gated:
        in_specs, operands = in_specs + [row, mod], operands + list(gated)
        out_specs, out_shape = out_specs + [row, mod], out_shape + [jax.ShapeDtypeStruct((T, Dm), MXU_DTYPE), mod_shape]
    return pl.pallas_call(
        body, name=name, grid=(T // ROW_TILE,), in_specs=in_specs, out_specs=tuple(out_specs),
        out_shape=tuple(out_shape), compiler_params=_params())(*operands)


FFN_IN_ROWS = 512
FFN_IN_COLS = 1408
FFN_BWD_ROWS = 256


def _ffn_in_swiglu(h, w_in, name):
    T, Dm = h.shape
    nj = D_FF // FFN_IN_COLS

    def body(h_ref, wg_ref, wu_ref, g_ref, u_ref, a_ref):
        hv = h_ref[...]
        gate = _dot(hv, wg_ref[...], _NN)
        up = _dot(hv, wu_ref[...], _NN)
        g_ref[...] = gate
        u_ref[...] = up
        a_ref[...] = (gate * jax.nn.sigmoid(gate) * up).astype(a_ref.dtype)

    out = pl.BlockSpec((FFN_IN_ROWS, FFN_IN_COLS), lambda i, j: (i, j))
    pre = jax.ShapeDtypeStruct((T, D_FF), F32)
    return pl.pallas_call(
        body, name=name, grid=(T // FFN_IN_ROWS, nj),
        in_specs=[pl.BlockSpec((FFN_IN_ROWS, Dm), lambda i, j: (i, 0)),
                  pl.BlockSpec((Dm, FFN_IN_COLS), lambda i, j: (0, j)),
                  pl.BlockSpec((Dm, FFN_IN_COLS), lambda i, j: (0, nj + j))],
        out_specs=(out, out, out), out_shape=(pre, pre, jax.ShapeDtypeStruct((T, D_FF), MXU_DTYPE)),
        compiler_params=_params())(h, w_in, w_in)


def _ffn_out_bwd_swiglu(dz, w_out, gate, up, name):
    T, Dm = dz.shape

    def body(dz_ref, w_ref, g_ref, u_ref, du_ref):
        da = _dot(dz_ref[...], w_ref[...], _NT)
        gv, uv = g_ref[...], u_ref[...]
        sg = jax.nn.sigmoid(gv)
        du_ref[:, :D_FF] = (da * uv * (sg * (1.0 + gv * (1.0 - sg)))).astype(du_ref.dtype)
        du_ref[:, D_FF:] = (da * gv * sg).astype(du_ref.dtype)

    half = pl.BlockSpec((FFN_BWD_ROWS, D_FF), lambda i: (i, 0))
    return pl.pallas_call(
        body, name=name, grid=(T // FFN_BWD_ROWS,),
        in_specs=[pl.BlockSpec((FFN_BWD_ROWS, Dm), lambda i: (i, 0)), pl.BlockSpec((D_FF, Dm), lambda i: (0, 0)), half, half],
        out_specs=pl.BlockSpec((FFN_BWD_ROWS, 2 * D_FF), lambda i: (i, 0)),
        out_shape=jax.ShapeDtypeStruct((T, 2 * D_FF), MXU_DTYPE), compiler_params=_params())(dz, w_out, gate, up)


def _loss_fwd_bwd(y, target, f, gate, gidx, n_groups, name):
    T, Dm = y.shape

    def body(y_ref, t_ref, f_ref, gate_ref, loss_ref, dy_ref, dz_ref, dgate_ref):
        i = pl.program_id(0)
        err = y_ref[...] - t_ref[...]

        @pl.when(i == 0)
        def _():
            loss_ref[...] = jnp.zeros_like(loss_ref)

        @pl.when(_first_of_group(i, gidx))
        def _():
            dgate_ref[...] = jnp.zeros_like(dgate_ref)

        loss_ref[...] += 0.5 * jnp.sum(jnp.mean(err * err, axis=-1, keepdims=True))
        dy = err * (1.0 / Dm)
        dy_ref[...] = dy
        dgate_ref[0] += jnp.sum(dy * f_ref[...], axis=0, keepdims=True)
        dz_ref[...] = (dy * gate_ref[0]).astype(dz_ref.dtype)

    row = pl.BlockSpec((ROW_TILE, Dm), lambda i: (i, 0))
    mod = pl.BlockSpec((1, 1, Dm), lambda i: (gidx(i), 0, 0))
    return pl.pallas_call(
        body, name=name, grid=(T // ROW_TILE,), in_specs=[row, row, row, mod],
        out_specs=(pl.BlockSpec((8, LANES), lambda i: (0, 0)), row, row, mod),
        out_shape=(jax.ShapeDtypeStruct((8, LANES), F32), jax.ShapeDtypeStruct((T, Dm), F32),
                   jax.ShapeDtypeStruct((T, Dm), MXU_DTYPE), jax.ShapeDtypeStruct((n_groups, 1, Dm), F32)),
        compiler_params=_params())(y, target, f, gate)


def _rope_tables(seq, head_dim):
    axis_dim = head_dim // 2
    half = axis_dim // 2
    pos = jnp.arange(seq, dtype=jnp.int32)
    row = (pos // GRID_W).astype(F32)[:, None]
    col = (pos % GRID_W).astype(F32)[:, None]
    inv = ROPE_BASE ** (-jnp.arange(0, axis_dim, 2, dtype=F32) / axis_dim)
    lane = jnp.arange(head_dim, dtype=jnp.int32)
    within = lane % axis_dim
    ang = jnp.where((lane // axis_dim == 0)[None, :], row, col) * inv[within % half][None, :]
    cos = jnp.cos(ang)
    sin = jnp.where((within < half)[None, :], -jnp.sin(ang), jnp.sin(ang))
    cos = jnp.concatenate([cos, jnp.ones((ROW_TILE, head_dim), F32)], axis=0)
    sin = jnp.concatenate([sin, jnp.zeros((ROW_TILE, head_dim), F32)], axis=0)
    return cos, sin


def _pair_swap(v, half):
    if 2 * half == LANES:
        return pltpu.roll(v, half, axis=1)
    lane = lax.broadcasted_iota(jnp.int32, v.shape, 1)
    return jnp.where((lane % (2 * half)) < half, pltpu.roll(v, LANES - half, axis=1), pltpu.roll(v, half, axis=1))


def _head_sum(v, ones_ref):
    hi = v.astype(MXU_DTYPE)
    lo = (v - hi.astype(F32)).astype(MXU_DTYPE)
    return (jnp.dot(hi, ones_ref[...], preferred_element_type=F32)
            + jnp.dot(lo, ones_ref[...], preferred_element_type=F32))


def _head_ones():
    lane = jnp.arange(LANES)
    return (lane[:, None] // HEAD_DIM == lane[None, :] // HEAD_DIM).astype(MXU_DTYPE)


ATTN_QK_BLOCKS = (N_HEADS + N_KV_HEADS) * HEAD_DIM // LANES
ATTN_ALL_BLOCKS = (N_HEADS + 2 * N_KV_HEADS) * HEAD_DIM // LANES
ATTN_Q_BLOCKS = N_HEADS * HEAD_DIM // LANES
ATTN_SCALE = HEAD_DIM ** -0.5


def _attn_prep_fwd(qkv, gains, cos, sin, tidx, name):
    T, W = qkv.shape

    def body(x_ref, g_ref, cos_ref, sin_ref, ones_ref, o_ref):
        for cb in range(ATTN_ALL_BLOCKS):
            cols = slice(cb * LANES, (cb + 1) * LANES)
            xv = x_ref[:, cols]
            if cb < ATTN_QK_BLOCKS:
                r = lax.rsqrt(_head_sum(xv * xv, ones_ref) * (1.0 / HEAD_DIM) + EPS)
                y = xv * r * g_ref[0 if cb < ATTN_Q_BLOCKS else 1]
                xv = y * cos_ref[...] + _pair_swap(y, HEAD_DIM // 4) * sin_ref[...]
                if cb < ATTN_Q_BLOCKS:
                    xv = xv * ATTN_SCALE
            o_ref[:, cols] = xv.astype(o_ref.dtype)

    row = pl.BlockSpec((ROW_TILE, W), lambda i: (i, 0))
    tab = pl.BlockSpec((ROW_TILE, LANES), lambda i: (tidx(i), 0))
    return pl.pallas_call(
        body, name=name, grid=(T // ROW_TILE,),
        in_specs=[row, pl.BlockSpec((2, 1, LANES), lambda i: (0, 0, 0)), tab, tab,
                  pl.BlockSpec((LANES, LANES), lambda i: (0, 0))],
        out_specs=row, out_shape=jax.ShapeDtypeStruct(qkv.shape, MXU_DTYPE),
        compiler_params=_params())(qkv, gains, cos, sin, _head_ones())


def _attn_prep_bwd(latent, context, qkv, gains, cos, sin, tidx, name):
    T, W = qkv.shape
    qk_w = ATTN_QK_BLOCKS * LANES
    q_w = ATTN_Q_BLOCKS * LANES
    n_x = latent[0].shape[0] // ROW_TILE

    def body(dqx_ref, dkx_ref, dvx_ref, dqc_ref, dkc1_ref, dkc2_ref, dvc1_ref, dvc2_ref,
             x_ref, g_ref, cos_ref, sin_ref, ones_ref, o_ref, dg_ref):
        is_latent = pl.program_id(0) < n_x

        @pl.when(pl.program_id(0) == 0)
        def _():
            dg_ref[...] = jnp.zeros_like(dg_ref)

        for cb in range(ATTN_QK_BLOCKS):
            cols = slice(cb * LANES, (cb + 1) * LANES)
            xv = x_ref[:, cols]
            if cb < ATTN_Q_BLOCKS:
                d = jnp.where(is_latent, dqx_ref[:, cols], dqc_ref[:, cols]) * ATTN_SCALE
            else:
                kc = slice(cb * LANES - q_w, (cb + 1) * LANES - q_w)
                d = jnp.where(is_latent, dkx_ref[:, kc], dkc1_ref[:, kc] + dkc2_ref[:, kc])
            r = lax.rsqrt(_head_sum(xv * xv, ones_ref) * (1.0 / HEAD_DIM) + EPS)
            xn = xv * r
            dy = d * cos_ref[...] + _pair_swap(d * sin_ref[...], HEAD_DIM // 4)
            dg_ref[:, cols] += jnp.sum(dy * xn, axis=0, keepdims=True)
            dxn = dy * g_ref[0 if cb < ATTN_Q_BLOCKS else 1]
            dx = r * (dxn - xn * (_head_sum(dxn * xn, ones_ref) * (1.0 / HEAD_DIM)))
            o_ref[:, cols] = dx.astype(o_ref.dtype)
        o_ref[:, qk_w:] = jnp.where(is_latent, dvx_ref[...], dvc1_ref[...] + dvc2_ref[...]).astype(o_ref.dtype)

    row = lambda w: pl.BlockSpec((ROW_TILE, w), lambda i: (i, 0))
    lat = lambda t: pl.BlockSpec((ROW_TILE, t.shape[1]), lambda i: (jnp.minimum(i, n_x - 1), 0))
    ctx = lambda t: pl.BlockSpec((ROW_TILE, t.shape[1]), lambda i: (jnp.maximum(i - n_x, 0), 0))
    tab = pl.BlockSpec((ROW_TILE, LANES), lambda i: (tidx(i), 0))
    return pl.pallas_call(
        body, name=name, grid=(T // ROW_TILE,),
        in_specs=[lat(t) for t in latent] + [ctx(t) for t in context]
        + [row(W), pl.BlockSpec((2, 1, LANES), lambda i: (0, 0, 0)), tab, tab, pl.BlockSpec((LANES, LANES), lambda i: (0, 0))],
        out_specs=(row(W), pl.BlockSpec((1, qk_w), lambda i: (0, 0))),
        out_shape=(jax.ShapeDtypeStruct(qkv.shape, MXU_DTYPE), jax.ShapeDtypeStruct((1, qk_w), F32)),
        compiler_params=_params())(*latent, *context, qkv, gains, cos, sin, _head_ones())


RET_QK_BLOCKS = 2 * RET_HEADS * RET_QK_DIM // LANES


def _ret_rope(x, cos, sin, tidx, name):
    T = x.shape[0]
    W = RET_QK_BLOCKS * LANES
    k_scale = RET_QK_DIM ** -0.5

    def body(x_ref, cos_ref, sin_ref, o_ref):
        for cb in range(RET_QK_BLOCKS):
            cols = slice(cb * LANES, (cb + 1) * LANES)
            tcols = slice((cb % 2) * LANES, (cb % 2 + 1) * LANES)
            xv = x_ref[:, cols]
            out = xv * cos_ref[:, tcols] + pltpu.roll(xv, LANES // 2, axis=1) * sin_ref[:, tcols]
            if cb >= RET_QK_BLOCKS // 2:
                out = out * k_scale
            o_ref[:, cols] = out

    row = pl.BlockSpec((ROW_TILE, W), lambda i: (i, 0))
    tab = pl.BlockSpec((ROW_TILE, RET_QK_DIM), lambda i: (tidx(i), 0))
    return pl.pallas_call(
        body, name=name, grid=(T // ROW_TILE,), in_specs=[row, tab, tab], out_specs=row,
        out_shape=jax.ShapeDtypeStruct((T, W), F32), compiler_params=_params())(x, cos, sin)


ASSEMBLE_ROWS = 256


def _ret_grad_assemble(x_parts, c_parts, dg, cos, sin, seq, name):
    NX, NC = x_parts[0].shape[0], c_parts[0].shape[0]
    T = NX + NC
    rt = ASSEMBLE_ROWS
    nxt = NX // rt
    qk_w = RET_HEADS * RET_QK_DIM
    k_scale = RET_QK_DIM ** -0.5

    def unrotate(d, cos_ref, sin_ref, scale):
        outs = []
        for cb in range(qk_w // LANES):
            cols = slice(cb * LANES, (cb + 1) * LANES)
            tcols = slice((cb % 2) * LANES, (cb % 2 + 1) * LANES)
            dv_ = d[:, cols]
            o = dv_ * cos_ref[:, tcols] + pltpu.roll(dv_ * sin_ref[:, tcols], LANES // 2, axis=1)
            outs.append(o * scale if scale != 1.0 else o)
        return outs

    def body(dqf, dqb, dkf, dkb, dvf, dvb, dg_ref, dkcf, dkcb, dvcf, dvcb, cos_ref, sin_ref, o_ref):
        i = pl.program_id(0)

        def write_k(parts):
            for cb, o in enumerate(parts):
                o_ref[:, qk_w + cb * LANES:qk_w + (cb + 1) * LANES] = o.astype(o_ref.dtype)

        @pl.when(i < nxt)
        def _():
            for cb, o in enumerate(unrotate(dqf[...] + dqb[...], cos_ref, sin_ref, 1.0)):
                o_ref[:, cb * LANES:(cb + 1) * LANES] = o.astype(o_ref.dtype)
            write_k(unrotate(dkf[...] + dkb[...], cos_ref, sin_ref, k_scale))
            o_ref[:, 2 * qk_w:2 * qk_w + RET_VWIDTH] = (dvf[...] + dvb[...]).astype(o_ref.dtype)
            o_ref[:, 2 * qk_w + RET_VWIDTH:] = dg_ref[...].astype(o_ref.dtype)

        @pl.when(i >= nxt)
        def _():
            o_ref[:, :qk_w] = jnp.zeros((rt, qk_w), o_ref.dtype)
            write_k(unrotate(dkcf[...] + dkcb[...], cos_ref, sin_ref, k_scale))
            o_ref[:, 2 * qk_w:2 * qk_w + RET_VWIDTH] = (dvcf[...] + dvcb[...]).astype(o_ref.dtype)
            o_ref[:, 2 * qk_w + RET_VWIDTH:] = jnp.zeros((rt, RET_VWIDTH), o_ref.dtype)

    xs = lambda w: pl.BlockSpec((rt, w), lambda i: (jnp.minimum(i, nxt - 1), 0))
    cs = lambda w: pl.BlockSpec((rt, w), lambda i: (jnp.maximum(i - nxt, 0), 0))
    tab = pl.BlockSpec((rt, RET_QK_DIM), lambda i: (jnp.where(i < nxt, i % (seq // rt), seq // rt), 0))
    return pl.pallas_call(
        body, name=name, grid=(T // rt,),
        in_specs=[xs(qk_w)] * 4 + [xs(RET_VWIDTH)] * 3 + [cs(qk_w)] * 2 + [cs(RET_VWIDTH)] * 2 + [tab, tab],
        out_specs=pl.BlockSpec((rt, 2 * qk_w + 2 * RET_VWIDTH), lambda i: (i, 0)),
        out_shape=jax.ShapeDtypeStruct((T, 2 * qk_w + 2 * RET_VWIDTH), MXU_DTYPE),
        compiler_params=_params())(*x_parts, dg, *c_parts, cos, sin)


def _band_bias(qb, seq):
    nb = seq // qb
    assert nb >= 2
    i = jnp.arange(GQA_GROUP * qb, dtype=jnp.int32)[:, None] % qb
    n = jnp.arange(3 * qb, dtype=jnp.int32)[None, :]
    in_window = (n >= i) & (n - i <= 2 * WINDOW)
    variants = [in_window & (n >= qb), in_window, in_window & (n < 2 * qb)]
    return jnp.stack([jnp.where(v, 0.0, NEG_INF).astype(F32) for v in variants])


GROUP_ORDER = (0, 2, 1, 3)


def _stack_halves(blk):
    return jnp.concatenate([blk[:, :LANES], blk[:, LANES:]], axis=0)


def _unstack_halves(v, rows):
    return jnp.concatenate([v[:rows], v[rows:]], axis=1)


def _align_head(pair, odd):
    lane = lax.broadcasted_iota(jnp.int32, pair.shape, 1)
    mine = jnp.where((lane >= HEAD_DIM) == odd, pair, jnp.zeros_like(pair))
    rolled = pltpu.roll(mine, HEAD_DIM, axis=1)
    return jnp.where(odd, rolled, mine), jnp.where(odd, mine, rolled)


def _scores(out_ref, q2, x_eo):
    half = q2.shape[0]
    out_ref[:half, :] = _dot(q2, x_eo[0], _NT)
    out_ref[half:, :] = _dot(q2, x_eo[1], _NT)


def _apply(p_ref, x_eo):
    half = p_ref.shape[0] // 2
    return _dot(p_ref[:half, :], x_eo[0], _NN) + _dot(p_ref[half:, :], x_eo[1], _NN)


def _kv_grad(a_ref, q2, odd):
    half = a_ref.shape[0] // 2
    even_t = _dot(q2, a_ref[:half, :], _TN)
    odd_t = _dot(q2, a_ref[half:, :], _TN)
    mine = even_t[:HEAD_DIM] + odd_t[HEAD_DIM:]
    zero = jnp.zeros_like(mine)
    placed = jnp.where(odd, jnp.concatenate([zero, mine], axis=0), jnp.concatenate([mine, zero], axis=0))
    return placed.T


ATTN_ROW_CHUNK = 32


def _softmax_chunks(s_c_ref, s_l_ref, bias_ref, sink_ref, kv_head, qb, emit):
    for r0 in range(0, GQA_GROUP * qb, ATTN_ROW_CHUNK):
        rows = slice(r0, r0 + ATTN_ROW_CHUNK)
        t = r0 // qb
        sink = jnp.full((ATTN_ROW_CHUNK, 1), sink_ref[kv_head, GROUP_ORDER[t]], F32)
        s_c = s_c_ref[rows, :]
        m = jnp.maximum(jnp.max(s_c, axis=-1, keepdims=True), sink)
        s_l = None
        if s_l_ref is not None:
            s_l = s_l_ref[rows, :] + bias_ref[0, rows, :]
            m = jnp.maximum(m, jnp.max(s_l, axis=-1, keepdims=True))
        e_c = jnp.exp(s_c - m)
        e_s = jnp.exp(sink - m)
        den = jnp.sum(e_c, axis=-1, keepdims=True) + e_s
        e_l = None
        if s_l_ref is not None:
            e_l = jnp.exp(s_l - m)
            den = den + jnp.sum(e_l, axis=-1, keepdims=True)
        inv = 1.0 / den
        emit(t, rows, e_c * inv, (None if e_l is None else e_l * inv), e_s * inv)


GROUP_W = GQA_GROUP * HEAD_DIM
K_LANE_BLOCK = N_HEADS * HEAD_DIM // LANES
V_LANE_BLOCK = K_LANE_BLOCK + N_KV_HEADS * HEAD_DIM // LANES


def _attn_specs(B, seq, ctx_len, ctx_queries):
    ctx0 = B * seq // ctx_len
    if ctx_queries:
        qb, nb = ctx_len, 1
        qrow = lambda b, j: ctx0 + b
    else:
        qb, nb = ATTN_BLOCK, seq // ATTN_BLOCK
        qrow = lambda b, j: b * nb + j
    q_spec = pl.BlockSpec((qb, 2 * GROUP_W), lambda b, p, j: (qrow(b, j), p))
    c_specs = [pl.BlockSpec((ctx_len, LANES), lambda b, p, j: (ctx0 + b, K_LANE_BLOCK + p)),
               pl.BlockSpec((ctx_len, LANES), lambda b, p, j: (ctx0 + b, V_LANE_BLOCK + p))]
    local = []
    if not ctx_queries:
        near = [lambda j: jnp.maximum(j - 1, 0), lambda j: j, lambda j: jnp.minimum(j + 1, nb - 1)]
        for lane0 in (K_LANE_BLOCK, V_LANE_BLOCK):
            for f in near:
                local.append(pl.BlockSpec((qb, LANES), lambda b, p, j, f=f, lane0=lane0: (b * nb + f(j), lane0 + p)))
        local.append(pl.BlockSpec(
            (1, GQA_GROUP * qb, 3 * qb), lambda b, p, j: (jnp.where(j == 0, 0, jnp.where(j == nb - 1, 2, 1)), 0, 0)))
    return qb, nb, qrow, q_spec, c_specs, local


def _attn_operands(refs, has_local, sub):
    odd = bool(sub)
    n_local = 7 if has_local else 0
    q2 = _stack_halves(refs[0][:, sub * GROUP_W:(sub + 1) * GROUP_W])
    kc = _align_head(refs[1 + n_local][...], odd)
    vc = _align_head(refs[2 + n_local][...], odd)
    kl = vl = bias_ref = None
    if has_local:
        kl = _align_head(jnp.concatenate([r[...] for r in refs[1:4]], axis=0), odd)
        vl = _align_head(jnp.concatenate([r[...] for r in refs[4:7]], axis=0), odd)
        bias_ref = refs[7]
    return odd, q2, kc, vc, kl, vl, bias_ref


def _score_scratch(qb, ctx_len, has_local, dtypes):
    rows = GQA_GROUP * qb
    out = []
    for dt in dtypes:
        out.append(pltpu.VMEM((rows, ctx_len), dt))
        if has_local:
            out.append(pltpu.VMEM((rows, 3 * qb), dt))
    return out


def _score_bufs(scratch, has_local):
    if has_local:
        return [(scratch[i], scratch[i + 1]) for i in range(0, len(scratch), 2)]
    return [(s, None) for s in scratch]


def _attn_fwd(qkv, sink, B, seq, ctx_len, ctx_queries, name):
    has_local = not ctx_queries
    qb, nb, _, q_spec, c_specs, local = _attn_specs(B, seq, ctx_len, ctx_queries)
    n_rows = B * (ctx_len if ctx_queries else seq)
    n_in = 1 + (7 if has_local else 0) + 3

    per_head = _score_scratch(qb, ctx_len, has_local, (F32, MXU_DTYPE))

    def body(*refs):
        sink_ref, o_ref = refs[n_in - 1], refs[n_in]
        scratch = refs[n_in + 1:]
        for sub in range(2):
            (s_c_ref, s_l_ref), (p_c_ref, p_l_ref) = _score_bufs(
                scratch[sub * len(per_head):(sub + 1) * len(per_head)], has_local)
            kv_head = 2 * pl.program_id(1) + sub
            _, q2, kc, vc, kl, vl, bias_ref = _attn_operands(refs, has_local, sub)
            _scores(s_c_ref, q2, kc)
            if has_local:
                _scores(s_l_ref, q2, kl)

            def emit(t, rows, p_c, p_l, p_s, p_c_ref=p_c_ref, p_l_ref=p_l_ref):
                p_c_ref[rows, :] = p_c.astype(p_c_ref.dtype)
                if has_local:
                    p_l_ref[rows, :] = p_l.astype(p_l_ref.dtype)

            _softmax_chunks(s_c_ref, s_l_ref, bias_ref, sink_ref, kv_head, qb, emit)
            o2 = _apply(p_c_ref, vc)
            if has_local:
                o2 = o2 + _apply(p_l_ref, vl)
            o_ref[:, sub * GROUP_W:(sub + 1) * GROUP_W] = _unstack_halves(o2, qb).astype(o_ref.dtype)

    operands = [qkv] + ([qkv] * 6 + [_band_bias(qb, seq)] if has_local else []) + [qkv, qkv, sink]
    return pl.pallas_call(
        body, name=name, grid=(B, N_KV_HEADS // 2, nb),
        in_specs=[q_spec] + local + c_specs + [_SMEM],
        out_specs=pl.BlockSpec((qb, 2 * GROUP_W), lambda b, p, j: (b * nb + j, p)),
        out_shape=jax.ShapeDtypeStruct((n_rows, N_HEADS * HEAD_DIM), MXU_DTYPE),
        scratch_shapes=per_head * 2, compiler_params=_params())(*operands)


def _attn_bwd(qkv, sink, do, B, seq, ctx_len, ctx_queries, name):
    has_local = not ctx_queries
    qb, nb, qrow, q_spec, c_specs, local = _attn_specs(B, seq, ctx_len, ctx_queries)
    n_rows = B * (ctx_len if ctx_queries else seq)

    n_out = 6 if has_local else 4
    per_head = _score_scratch(qb, ctx_len, has_local, (F32, F32, MXU_DTYPE, MXU_DTYPE))

    def body(*refs):
        n_in = 1 + (7 if has_local else 0) + 4
        sink_ref, do_ref = refs[n_in - 2:n_in]
        outs = refs[n_in:n_in + n_out]
        scratch = refs[n_in + n_out:]
        dq_ref = outs[0]
        dkc_ref, dvc_ref, dsink_ref = outs[-3:]
        b, pair, j = pl.program_id(0), pl.program_id(1), pl.program_id(2)

        @pl.when(j == 0)
        def _():
            dkc_ref[...] = jnp.zeros_like(dkc_ref)
            dvc_ref[...] = jnp.zeros_like(dvc_ref)
            if has_local:
                outs[1][...] = jnp.zeros_like(outs[1])
                outs[2][...] = jnp.zeros_like(outs[2])

        @pl.when((b == 0) & (pair == 0) & (j == 0))
        def _():
            dsink_ref[...] = jnp.zeros_like(dsink_ref)

        for sub in range(2):
            (s_c_ref, s_l_ref), (dp_c_ref, dp_l_ref), (p_c_ref, p_l_ref), (ds_c_ref, ds_l_ref) = _score_bufs(
                scratch[sub * len(per_head):(sub + 1) * len(per_head)], has_local)
            kv_head = 2 * pair + sub
            odd, q2, kc, vc, kl, vl, bias_ref = _attn_operands(refs, has_local, sub)
            do2 = _stack_halves(do_ref[:, sub * GROUP_W:(sub + 1) * GROUP_W])
            _scores(s_c_ref, q2, kc)
            _scores(dp_c_ref, do2, vc)
            if has_local:
                _scores(s_l_ref, q2, kl)
                _scores(dp_l_ref, do2, vl)
            dsink_parts = [jnp.zeros((), F32)] * GQA_GROUP

            def emit(t, rows, p_c, p_l, p_s, dp_c_ref=dp_c_ref, dp_l_ref=dp_l_ref, p_c_ref=p_c_ref, p_l_ref=p_l_ref,
                     ds_c_ref=ds_c_ref, ds_l_ref=ds_l_ref, dsink_parts=dsink_parts):
                dp_c = dp_c_ref[rows, :]
                delta = jnp.sum(p_c * dp_c, axis=-1, keepdims=True)
                if has_local:
                    dp_l = dp_l_ref[rows, :]
                    delta = delta + jnp.sum(p_l * dp_l, axis=-1, keepdims=True)
                    p_l_ref[rows, :] = p_l.astype(p_l_ref.dtype)
                    ds_l_ref[rows, :] = (p_l * (dp_l - delta)).astype(ds_l_ref.dtype)
                p_c_ref[rows, :] = p_c.astype(p_c_ref.dtype)
                ds_c_ref[rows, :] = (p_c * (dp_c - delta)).astype(ds_c_ref.dtype)
                dsink_parts[t] = dsink_parts[t] - jnp.sum(p_s * delta)

            _softmax_chunks(s_c_ref, s_l_ref, bias_ref, sink_ref, kv_head, qb, emit)
            dq2 = _apply(ds_c_ref, kc)
            dkc_ref[...] += _kv_grad(ds_c_ref, q2, odd)
            dvc_ref[...] += _kv_grad(p_c_ref, do2, odd)
            if has_local:
                dq2 = dq2 + _apply(ds_l_ref, kl)
                dkl = _kv_grad(ds_l_ref, q2, odd)
                dvl = _kv_grad(p_l_ref, do2, odd)
                dk_ref, dv_ref = outs[1], outs[2]
                for t in range(3):
                    def add(t=t, dkl=dkl, dvl=dvl):
                        start = pl.multiple_of((j - 1 + t) * qb, qb)
                        dk_ref[pl.ds(start, qb), :] += dkl[t * qb:(t + 1) * qb]
                        dv_ref[pl.ds(start, qb), :] += dvl[t * qb:(t + 1) * qb]
                    if t == 0:
                        pl.when(j > 0)(add)
                    elif t == 2:
                        pl.when(j < nb - 1)(add)
                    else:
                        add()
            dq_ref[:, sub * GROUP_W:(sub + 1) * GROUP_W] = _unstack_halves(dq2, qb)
            row8 = lax.broadcasted_iota(jnp.int32, (8, LANES), 0)
            tile = jnp.zeros((8, LANES), F32)
            for t, gi in enumerate(GROUP_ORDER):
                tile = jnp.where(row8 == gi, dsink_parts[t], tile)
            dsink_ref[pl.ds(pl.multiple_of(kv_head * 8, 8), 8), :] += tile

    kv_w = N_KV_HEADS * HEAD_DIM
    seq_spec = pl.BlockSpec((seq, LANES), lambda b, p, j: (b, p))
    ctx_spec = pl.BlockSpec((ctx_len, LANES), lambda b, p, j: (b, p))
    do_spec = pl.BlockSpec((qb, 2 * GROUP_W), lambda b, p, j: (qrow(b, j), p))
    operands = [qkv] + ([qkv] * 6 + [_band_bias(qb, seq)] if has_local else []) + [qkv, qkv, sink, do]
    out_specs = ([pl.BlockSpec((qb, 2 * GROUP_W), lambda b, p, j: (b * nb + j, p))] + ([seq_spec, seq_spec] if has_local else [])
                 + [ctx_spec, ctx_spec, pl.BlockSpec((32, LANES), lambda b, p, j: (0, 0))])
    out_shape = ([jax.ShapeDtypeStruct((n_rows, N_HEADS * HEAD_DIM), F32)]
                 + ([jax.ShapeDtypeStruct((B * seq, kv_w), F32)] * 2 if has_local else [])
                 + [jax.ShapeDtypeStruct((B * ctx_len, kv_w), F32)] * 2 + [jax.ShapeDtypeStruct((32, LANES), F32)])
    return pl.pallas_call(
        body, name=name, grid=(B, N_KV_HEADS // 2, nb),
        in_specs=[q_spec] + local + c_specs + [_SMEM, do_spec],
        out_specs=tuple(out_specs), out_shape=tuple(out_shape), scratch_shapes=per_head * 2,
        compiler_params=_params())(*operands)


def _ret_decays(lg, rev):
    n = lax.broadcasted_iota(jnp.int32, (RET_CHUNK, RET_CHUNK), 0).astype(F32)
    m = lax.broadcasted_iota(jnp.int32, (RET_CHUNK, RET_CHUNK), 1).astype(F32)
    pos = lax.broadcasted_iota(jnp.int32, (RET_CHUNK, 1), 0).astype(F32)
    diff = (m - n) if rev else (n - m)
    a_exp = jnp.maximum(diff, 0.0)
    intra = jnp.where(diff >= 0, jnp.exp(lg * a_exp), 0.0)
    q_exp = (RET_CHUNK - pos) if rev else (pos + 1.0)
    k_exp = pos if rev else (RET_CHUNK - 1.0 - pos)
    chunk = jnp.exp(jnp.full((1, 1), RET_CHUNK, F32) * lg)
    return intra, a_exp, jnp.exp(lg * q_exp), q_exp, jnp.exp(lg * k_exp), k_exp, chunk


def _ctx_decay(lg, ctx_len, rev):
    t = lax.broadcasted_iota(jnp.int32, (ctx_len, 1), 0).astype(F32)
    expo = t if rev else (ctx_len - 1.0 - t)
    return jnp.exp(lg * expo), expo


def _ret_specs(B, seq, ctx_len, order):
    nc = seq // RET_CHUNK
    x_blocks = B * seq // ctx_len

    def rows(b, c):
        return b * nc + order(c, nc)

    q_spec = pl.BlockSpec((RET_CHUNK, RET_QK_DIM), lambda b, h, c: (rows(b, c), h))
    k_spec = pl.BlockSpec((RET_CHUNK, RET_QK_DIM), lambda b, h, c: (rows(b, c), RET_HEADS + h))
    v_spec = pl.BlockSpec((RET_CHUNK, RET_V_DIM), lambda b, h, c: (rows(b, c), RET_HEADS + h))
    kc_spec = pl.BlockSpec((ctx_len, RET_QK_DIM), lambda b, h, c: (x_blocks + b, RET_HEADS + h))
    vc_spec = pl.BlockSpec((ctx_len, RET_V_DIM), lambda b, h, c: (x_blocks + b, RET_HEADS + h))
    st_spec = pl.BlockSpec((1, 1, 1, RET_QK_DIM, RET_V_DIM), lambda b, h, c: (b, h, order(c, nc), 0, 0))
    o_spec = pl.BlockSpec((RET_CHUNK, RET_V_DIM), lambda b, h, c: (rows(b, c), h))
    return nc, q_spec, k_spec, v_spec, kc_spec, vc_spec, st_spec, o_spec


_SCAN_UP = lambda c, nc: c
_SCAN_DOWN = lambda c, nc: nc - 1 - c


def _ret_fwd(qk, qkvg, log_g, B, seq, ctx_len, name):
    nc, qf, kf, vf, kc_spec, vc_spec, stf, of = _ret_specs(B, seq, ctx_len, _SCAN_UP)
    _, qr, kr, vr, _, _, str_, or_ = _ret_specs(B, seq, ctx_len, _SCAN_DOWN)

    def body(lg_ref, qf_ref, kf_ref, vf_ref, qr_ref, kr_ref, vr_ref, kc_ref, vc_ref,
             of_ref, stf_ref, or_ref, str_ref, state_f, state_r):
        h, c = pl.program_id(1), pl.program_id(2)
        dirs = ((False, lg_ref[0, h], qf_ref, kf_ref, vf_ref, of_ref, stf_ref, state_f),
                (True, lg_ref[1, h], qr_ref, kr_ref, vr_ref, or_ref, str_ref, state_r))

        @pl.when(c == 0)
        def _():
            for rev, lg, _, _, _, _, _, state in dirs:
                dec, _ = _ctx_decay(lg, ctx_len, rev)
                state[...] = _dot(kc_ref[...] * dec, vc_ref[...], _TN)

        for rev, lg, q_ref, k_ref, v_ref, o_ref, st_ref, state in dirs:
            intra, _, q_dec, _, k_dec, _, chunk_dec = _ret_decays(lg, rev)
            qv, kv, vv = q_ref[...], k_ref[...], v_ref[...]
            s_in = state[...]
            st_ref[0, 0, 0] = s_in
            w = _dot(qv, kv, _NT) * intra
            o_ref[...] = _dot(w, vv, _NN) + _dot(qv, s_in, _NN) * q_dec
            state[...] = s_in * chunk_dec + _dot(kv * k_dec, vv, _TN)

    o_shape = jax.ShapeDtypeStruct((B * seq, RET_VWIDTH), F32)
    st_shape = jax.ShapeDtypeStruct((B, RET_HEADS, nc, RET_QK_DIM, RET_V_DIM), F32)
    return pl.pallas_call(
        body, name=name, grid=(B, RET_HEADS, nc),
        in_specs=[_SMEM, qf, kf, vf, qr, kr, vr, kc_spec, vc_spec],
        out_specs=(of, stf, or_, str_), out_shape=(o_shape, st_shape, o_shape, st_shape),
        scratch_shapes=[pltpu.VMEM((RET_QK_DIM, RET_V_DIM), F32)] * 2,
        compiler_params=_params())(log_g, qk, qk, qkvg, qk, qk, qkvg, qk, qkvg)


def _ret_bwd_chunk(rev, lg, q_ref, k_ref, v_ref, st_ref, do_ref, dq_ref, dk_ref, dv_ref, dlg_ref, dstate):
    intra, a_exp, q_dec, q_exp, k_dec, k_exp, chunk_dec = _ret_decays(lg, rev)
    qv, kv, vv, dov = q_ref[...], k_ref[...], v_ref[...], do_ref[...]
    s_in, ds_out = st_ref[0, 0, 0], dstate[...]
    p = _dot(qv, kv, _NT)
    w = p * intra
    dw = _dot(dov, vv, _NT)
    dp = dw * intra
    do_dec = dov * q_dec
    kd = kv * k_dec
    v_ds = _dot(vv, ds_out, _NT)
    dq_ref[...] = _dot(dp, kv, _NN) + _dot(do_dec, s_in, _NT)
    dk_ref[...] = _dot(dp, qv, _TN) + v_ds * k_dec
    dv_ref[...] = _dot(w, dov, _TN) + _dot(kd, ds_out, _NN)
    q_s = _dot(qv, s_in, _NN)
    dlg = (jnp.sum(dw * w * a_exp)
           + jnp.sum(q_exp * q_dec * jnp.sum(dov * q_s, axis=-1, keepdims=True))
           + jnp.sum(k_exp * k_dec * jnp.sum(kv * v_ds, axis=-1, keepdims=True))
           + RET_CHUNK * jnp.sum(chunk_dec * (ds_out * s_in)))
    ds_in = ds_out * chunk_dec + _dot(qv, do_dec, _TN)
    dstate[...] = ds_in
    dlg_ref[...] += dlg
    return ds_in


def _ret_bwd(qk, qkvg, log_g, st_f, st_r, do, B, seq, ctx_len, name):
    nc, qf, kf, vf, kc_spec, vc_spec, stf, of = _ret_specs(B, seq, ctx_len, _SCAN_DOWN)
    _, qr, kr, vr, _, _, str_, or_ = _ret_specs(B, seq, ctx_len, _SCAN_UP)

    def body(lg_ref, qf_ref, kf_ref, vf_ref, stf_ref, dof_ref, qr_ref, kr_ref, vr_ref, str_ref, dor_ref, kc_ref, vc_ref,
             dqf, dkf, dvf, dkcf, dvcf, dlgf, dqr, dkr, dvr, dkcr, dvcr, dlgr, dstate_f, dstate_r):
        h, c = pl.program_id(1), pl.program_id(2)
        dirs = ((False, lg_ref[0, h], (qf_ref, kf_ref, vf_ref, stf_ref, dof_ref, dqf, dkf, dvf, dlgf, dstate_f), dkcf, dvcf),
                (True, lg_ref[1, h], (qr_ref, kr_ref, vr_ref, str_ref, dor_ref, dqr, dkr, dvr, dlgr, dstate_r), dkcr, dvcr))

        @pl.when(c == 0)
        def _():
            for _, _, refs, _, _ in dirs:
                refs[-1][...] = jnp.zeros_like(refs[-1])
                refs[-2][...] = jnp.zeros_like(refs[-2])

        ds_first = [_ret_bwd_chunk(rev, lg, *refs) for rev, lg, refs, _, _ in dirs]

        @pl.when(c == nc - 1)
        def _():
            for (rev, lg, refs, dkc_ref, dvc_ref), ds_in in zip(dirs, ds_first):
                dec, expo = _ctx_decay(lg, ctx_len, rev)
                kcv, vcv = kc_ref[...], vc_ref[...]
                vc_ds = _dot(vcv, ds_in, _NT)
                dkc_ref[...] = vc_ds * dec
                dvc_ref[...] = _dot(kcv * dec, ds_in, _NN)
                refs[-2][...] += jnp.sum(expo * dec * jnp.sum(kcv * vc_ds, axis=-1, keepdims=True))

    def outs(q_spec, o_spec):
        return (pl.BlockSpec((RET_CHUNK, RET_QK_DIM), q_spec.index_map),
                pl.BlockSpec((RET_CHUNK, RET_QK_DIM), q_spec.index_map), o_spec,
                pl.BlockSpec((ctx_len, RET_QK_DIM), lambda b, h, c: (b, h)),
                pl.BlockSpec((ctx_len, RET_V_DIM), lambda b, h, c: (b, h)),
                pl.BlockSpec((1, 1, 8, LANES), lambda b, h, c: (b, h, 0, 0)))

    shapes = (jax.ShapeDtypeStruct((B * seq, RET_HEADS * RET_QK_DIM), F32),
              jax.ShapeDtypeStruct((B * seq, RET_HEADS * RET_QK_DIM), F32),
              jax.ShapeDtypeStruct((B * seq, RET_VWIDTH), F32),
              jax.ShapeDtypeStruct((B * ctx_len, RET_HEADS * RET_QK_DIM), F32),
              jax.ShapeDtypeStruct((B * ctx_len, RET_VWIDTH), F32),
              jax.ShapeDtypeStruct((B, RET_HEADS, 8, LANES), F32))
    res = pl.pallas_call(
        body, name=name, grid=(B, RET_HEADS, nc),
        in_specs=[_SMEM, qf, kf, vf, stf, of, qr, kr, vr, str_, or_, kc_spec, vc_spec],
        out_specs=outs(qf, of) + outs(qr, or_), out_shape=shapes + shapes,
        scratch_shapes=[pltpu.VMEM((RET_QK_DIM, RET_V_DIM), F32)] * 2,
        compiler_params=_params())(log_g, qk, qk, qkvg, st_f, do, qk, qk, qkvg, st_r, do, qk, qkvg)
    return res[:6], res[6:]


def _gated_out_fwd(o_f, o_b, qkvg, gn_gain, name):
    T = o_f.shape[0]
    g_off = (2 * RET_HEADS * RET_QK_DIM + RET_VWIDTH) // RET_V_DIM

    def body(of_ref, ob_ref, g_ref, gain_ref, z_ref):
        o = of_ref[...] + ob_ref[...]
        mu = jnp.mean(o, axis=-1, keepdims=True)
        var = jnp.mean(jnp.square(o - mu), axis=-1, keepdims=True)
        y = (o - mu) * lax.rsqrt(var + EPS) * gain_ref[...]
        gv = g_ref[...]
        z_ref[...] = (gv * jax.nn.sigmoid(gv) * y).astype(z_ref.dtype)

    blk = pl.BlockSpec((ROW_TILE, RET_V_DIM), lambda i, h: (i, h))
    return pl.pallas_call(
        body, name=name, grid=(T // ROW_TILE, RET_HEADS),
        in_specs=[blk, blk, pl.BlockSpec((ROW_TILE, RET_V_DIM), lambda i, h: (i, g_off + h)),
                  pl.BlockSpec((1, RET_V_DIM), lambda i, h: (0, h))],
        out_specs=blk, out_shape=jax.ShapeDtypeStruct((T, RET_VWIDTH), MXU_DTYPE),
        compiler_params=_params())(o_f, o_b, qkvg, gn_gain)


def _gated_out_bwd(dz, o_f, o_b, qkvg, gn_gain, name):
    T = o_f.shape[0]
    g_off = (2 * RET_HEADS * RET_QK_DIM + RET_VWIDTH) // RET_V_DIM

    def body(dz_ref, of_ref, ob_ref, g_ref, gain_ref, do_ref, dg_ref, dgain_ref):
        o = of_ref[...] + ob_ref[...]
        mu = jnp.mean(o, axis=-1, keepdims=True)
        var = jnp.mean(jnp.square(o - mu), axis=-1, keepdims=True)
        rstd = lax.rsqrt(var + EPS)
        yhat = (o - mu) * rstd
        gv, dzv = g_ref[...], dz_ref[...]
        sg = jax.nn.sigmoid(gv)
        dg_ref[...] = (dzv * (yhat * gain_ref[...]) * (sg * (1.0 + gv * (1.0 - sg)))).astype(dg_ref.dtype)
        dy = dzv * (gv * sg)

        @pl.when(pl.program_id(1) == 0)
        def _():
            dgain_ref[...] = jnp.zeros_like(dgain_ref)

        dgain_ref[...] += jnp.sum(dy * yhat, axis=0, keepdims=True)
        dyh = dy * gain_ref[...]
        do_ref[...] = rstd * (dyh - jnp.mean(dyh, axis=-1, keepdims=True)
                              - yhat * jnp.mean(dyh * yhat, axis=-1, keepdims=True))

    blk = pl.BlockSpec((ROW_TILE, RET_V_DIM), lambda h, i: (i, h))
    vec = pl.BlockSpec((1, RET_V_DIM), lambda h, i: (0, h))
    return pl.pallas_call(
        body, name=name, grid=(RET_HEADS, T // ROW_TILE),
        in_specs=[blk, blk, blk, pl.BlockSpec((ROW_TILE, RET_V_DIM), lambda h, i: (i, g_off + h)), vec],
        out_specs=(blk, blk, vec),
        out_shape=(jax.ShapeDtypeStruct((T, RET_VWIDTH), F32), jax.ShapeDtypeStruct((T, RET_VWIDTH), MXU_DTYPE),
                   jax.ShapeDtypeStruct((1, RET_VWIDTH), F32)),
        compiler_params=_params())(dz, o_f, o_b, qkvg, gn_gain)


def _adamw(w, m, v, parts, name):
    R, C = w.shape
    tr = _tile(R, (256, 128, 64, 32, 16, 8))
    n_parts = [p.shape[0] for p in parts]

    def body(*refs):
        w_ref, m_ref, v_ref = refs[:3]
        part_refs = refs[3:3 + len(parts)]
        g_ref, d_ref, nm_ref, nv_ref = refs[3 + len(parts):]
        g = None
        for ref, n in zip(part_refs, n_parts):
            for r in range(n):
                term = ref[r].astype(F32)
                g = term if g is None else g + term
        mn = ADAM_B1 * m_ref[...] + (1.0 - ADAM_B1) * g
        vn = ADAM_B2 * v_ref[...] + (1.0 - ADAM_B2) * jnp.square(g)
        m_hat = mn / (1.0 - ADAM_B1 ** ADAM_STEP)
        v_hat = vn / (1.0 - ADAM_B2 ** ADAM_STEP)
        g_ref[...] = g
        d_ref[...] = -ADAM_LR * (m_hat / (jnp.sqrt(v_hat) + ADAM_EPS) + ADAM_WD * w_ref[...])
        nm_ref[...] = mn
        nv_ref[...] = vn

    blk = pl.BlockSpec((tr, C), lambda i: (i, 0))
    part_specs = [pl.BlockSpec((n, tr, C), lambda i: (0, i, 0)) for n in n_parts]
    shp = jax.ShapeDtypeStruct((R, C), F32)
    return pl.pallas_call(
        body, name=name, grid=(R // tr,), in_specs=[blk, blk, blk] + part_specs,
        out_specs=(blk, blk, blk, blk), out_shape=(shp, shp, shp, shp),
        compiler_params=_params())(w, m, v, *parts)


def _sum_rows(parts, name):
    n, R, C = parts.shape
    tr = _tile(R, (256, 128, 64, 32, 16, 8))

    def body(p_ref, o_ref):
        acc = p_ref[0]
        for r in range(1, n):
            acc = acc + p_ref[r]
        o_ref[...] = acc

    return pl.pallas_call(
        body, name=name, grid=(R // tr,), in_specs=[pl.BlockSpec((n, tr, C), lambda i: (0, i, 0))],
        out_specs=pl.BlockSpec((tr, C), lambda i: (i, 0)), out_shape=jax.ShapeDtypeStruct((R, C), F32),
        compiler_params=_params())(parts)


def _my_coords():
    return lax.axis_index("x"), lax.axis_index("y"), lax.axis_index("c")


def _flip(coord, bit):
    return 1 - coord if bit else coord


def _all_gather(x2d, name):
    R, C = x2d.shape

    def body(x_ref, out_ref, send_sems, recv_sems, local_sem):
        x, y, c = _my_coords()
        me, sibling = (x, y, c), (x, y, 1 - c)
        chips = [(1 - x, y), (x, 1 - y), (1 - x, 1 - y)]

        def rows(px, py, pc):
            return out_ref.at[4 * px + 2 * py + pc]

        def copy(k, block, to, src=None):
            return pltpu.make_async_remote_copy(
                src_ref=rows(*block) if src is None else src, dst_ref=rows(*block),
                send_sem=send_sems.at[k], recv_sem=recv_sems.at[k], device_id=to, device_id_type=MESH)

        mine = pltpu.make_async_copy(x_ref, rows(*me), local_sem)
        mine.start()
        first = [copy(0, me, sibling, src=x_ref)]
        first += [copy(1 + j, me, (*chip, c), src=x_ref) for j, chip in enumerate(chips)]
        for cp in first:
            cp.start()
        passed = [copy(4 + j, (*chip, c), sibling) for j, chip in enumerate(chips)]
        for j, chip in enumerate(chips):
            copy(1 + j, (*chip, c), me).wait_recv()
            passed[j].start()
        copy(0, sibling, me).wait_recv()
        for j, chip in enumerate(chips):
            copy(4 + j, (*chip, 1 - c), me).wait_recv()
        for cp in first + passed:
            cp.wait_send()
        mine.wait()

    return pl.pallas_call(
        body, name=name, out_shape=jax.ShapeDtypeStruct((N_DEV, R, C), x2d.dtype),
        in_specs=[_ANY], out_specs=_ANY,
        scratch_shapes=[pltpu.SemaphoreType.DMA((7,)), pltpu.SemaphoreType.DMA((7,)), pltpu.SemaphoreType.DMA],
    )(x2d)


BIG_WEIGHTS = {
    "ffn_w_in": (2, (2, D_MODEL, 2 * D_FF)),
    "ffn_w_out": (1, (2, D_FF, D_MODEL)),
    "attn_w_qkv": (2, (1, D_MODEL, (N_HEADS + 2 * N_KV_HEADS) * HEAD_DIM)),
    "attn_w_o": (1, (1, N_HEADS * HEAD_DIM, D_MODEL)),
    "ret_w_qkvg": (2, (1, D_MODEL, 2 * D_MODEL + 2 * RET_VWIDTH)),
    "ret_gn_g": (2, (1, 1, RET_VWIDTH)),
    "ret_w_o": (1, (1, RET_VWIDTH, D_MODEL)),
}


def _join_shards(name, stacked):
    axis, full = BIG_WEIGHTS[name]
    if axis == 2:
        stacked = stacked.transpose(0, 2, 1, 3)
    return stacked.reshape(full)


def _split_shards(name, full_arr):
    axis, (_, rows, cols) = BIG_WEIGHTS[name]
    L = full_arr.shape[0]
    if axis == 2:
        return full_arr.reshape(L, rows, N_DEV, cols // N_DEV).transpose(0, 2, 1, 3)
    return full_arr.reshape(L, N_DEV, rows // N_DEV, cols)


def _gather_shards(shards, name):
    n = len(shards)

    def body(*refs):
        x_refs, out_refs = refs[:n], refs[n:2 * n]
        send_sems, recv_sems, local_sems = refs[2 * n:]
        x, y, c = _my_coords()
        me, sibling = (x, y, c), (x, y, 1 - c)
        chips = [(1 - x, y), (x, 1 - y), (1 - x, 1 - y)]

        def rows(a, px, py, pc):
            return out_refs[a].at[:, 4 * px + 2 * py + pc]

        def copy(a, k, block, to, src=None):
            return pltpu.make_async_remote_copy(
                src_ref=rows(a, *block) if src is None else src, dst_ref=rows(a, *block),
                send_sem=send_sems.at[7 * a + k], recv_sem=recv_sems.at[7 * a + k], device_id=to, device_id_type=MESH)

        mine = [pltpu.make_async_copy(x_refs[a], rows(a, *me), local_sems.at[a]) for a in range(n)]
        for cp in mine:
            cp.start()
        first = []
        for a in range(n):
            first.append(copy(a, 0, me, sibling, src=x_refs[a]))
            first += [copy(a, 1 + j, me, (*chip, c), src=x_refs[a]) for j, chip in enumerate(chips)]
        for cp in first:
            cp.start()
        passed = []
        for j, chip in enumerate(chips):
            for a in range(n):
                copy(a, 1 + j, (*chip, c), me).wait_recv()
                fwd = copy(a, 4 + j, (*chip, c), sibling)
                fwd.start()
                passed.append(fwd)
        for a in range(n):
            copy(a, 0, sibling, me).wait_recv()
            for j, chip in enumerate(chips):
                copy(a, 4 + j, (*chip, 1 - c), me).wait_recv()
        for cp in first + passed:
            cp.wait_send()
        for cp in mine:
            cp.wait()

    return pl.pallas_call(
        body, name=name,
        out_shape=[jax.ShapeDtypeStruct((s.shape[0], N_DEV) + s.shape[1:], s.dtype) for s in shards],
        in_specs=[_ANY] * n, out_specs=[_ANY] * n,
        scratch_shapes=[pltpu.SemaphoreType.DMA((7 * n,)), pltpu.SemaphoreType.DMA((7 * n,)),
                        pltpu.SemaphoreType.DMA((n,))],
    )(*shards)


def _exchange_shards(arrs, masks, src_of, out_tail, name):
    n, nm = len(arrs), len(masks)

    def body(*refs):
        in_refs, out_refs = refs[:n], refs[n:2 * n]
        send_sems, recv_sems = refs[2 * n:]
        x, y, c = _my_coords()
        copies = []
        for a in range(n):
            for k, (bx, by, bc) in enumerate(masks):
                peer = (_flip(x, bx), _flip(y, by), _flip(c, bc))
                copies.append(pltpu.make_async_remote_copy(
                    src_ref=src_of(in_refs[a], peer, (x, y, c)), dst_ref=out_refs[a].at[k],
                    send_sem=send_sems.at[nm * a + k], recv_sem=recv_sems.at[nm * a + k],
                    device_id=peer, device_id_type=MESH))
        for cp in copies:
            cp.start()
        for cp in copies:
            cp.wait()

    return pl.pallas_call(
        body, name=name,
        out_shape=[jax.ShapeDtypeStruct((nm,) + out_tail(s), s.dtype) for s in arrs],
        in_specs=[_ANY] * n, out_specs=[_ANY] * n,
        scratch_shapes=[pltpu.SemaphoreType.DMA((nm * n,)), pltpu.SemaphoreType.DMA((nm * n,))],
    )(*arrs)


def _pair_sum(g, from_sibling, core, out_dtype, name):
    L, _, _, a, b = g.shape
    ta = a

    def body(core_ref, g_ref, s_ref, o_ref):
        o_ref[...] = (g_ref[...] + s_ref[...]).astype(out_dtype)

    blk = pl.BlockSpec((1, 1, ta, b), lambda l, q, i, core_ref: (l, q, i, 0))
    return pl.pallas_call(
        body, name=name,
        grid_spec=pltpu.PrefetchScalarGridSpec(
            num_scalar_prefetch=1, grid=(L, 4, a // ta),
            in_specs=[pl.BlockSpec((1, 1, pl.Squeezed(), ta, b), lambda l, q, i, core_ref: (l, q, core_ref[0], i, 0)), blk],
            out_specs=blk),
        out_shape=jax.ShapeDtypeStruct((L, 4, a, b), out_dtype), compiler_params=_params())(core, g, from_sibling)


def _mods(mod_x, mod_c, layer):
    both = jnp.concatenate([mod_x[:, layer], mod_c[layer][None]], axis=0)
    return [both[:, None, k * D_MODEL:(k + 1) * D_MODEL] for k in range(6)]


def _local_step(x, ctx, target, mod_x, mod_c, w, small, late_weights=None, hooks=None):
    B, S, _ = x.shape
    L = ctx.shape[1]
    NX, NC = B * S, B * L
    T = NX + NC
    tiles_per_ex = S // ROW_TILE
    nxt = NX // ROW_TILE
    gidx = _group_index(nxt, tiles_per_ex, B)
    gidx_for = lambda rows: _group_index(NX // rows, S // rows, B)
    mm_rows = _tile(S, (MM_ROWS, ROW_TILE))
    tidx = lambda i: jnp.where(i < nxt, i % tiles_per_ex, tiles_per_ex)
    G = B + 1
    x0 = jnp.concatenate([x.reshape(NX, D_MODEL), ctx.reshape(NC, D_MODEL)], axis=0)
    acos, asin = [jnp.tile(t, (1, LANES // HEAD_DIM)) for t in _rope_tables(S, HEAD_DIM)]
    rcos, rsin = _rope_tables(S, RET_QK_DIM)
    sink = small["attn_sink"].reshape(N_KV_HEADS, GQA_GROUP)
    gains = jnp.stack([jnp.tile(small["attn_q_norm"].reshape(1, HEAD_DIM), (1, LANES // HEAD_DIM)),
                       jnp.tile(small["attn_k_norm"].reshape(1, HEAD_DIM), (1, LANES // HEAD_DIM))])
    log_g = jax.nn.log_sigmoid(small["ret_decay_logit"].reshape(2, RET_HEADS))
    n1, n2 = small["norm1_g"], small["norm2_g"]

    m0 = _mods(mod_x, mod_c, 0)
    h1 = _norm_mod_fwd(x0, n1[0:1], m0[0], m0[1], gidx, "l0_norm1")
    qkv = _mm(h1, w["attn_w_qkv"][0], "nn", F32, "l0_qkv")
    qkv_r = _attn_prep_fwd(qkv, gains, acos, asin, tidx, "l0_qk_prep")
    o_x = _attn_fwd(qkv_r, sink, B, S, L, False, "l0_attn_x")
    o_c = _attn_fwd(qkv_r, sink, B, S, L, True, "l0_attn_c")
    o0 = jnp.concatenate([o_x, o_c], axis=0)
    mo0, x1 = _mm(o0, w["attn_w_o"][0], "nn", F32, "l0_attn_out", res=x0, gate=m0[2], gidx_for=gidx_for, gate_rows=mm_rows)
    h2 = _norm_mod_fwd(x1, n2[0:1], m0[3], m0[4], gidx, "l0_norm2")
    if late_weights is not None:
        w = {**w, **late_weights(x1)}
    ug0, uu0, a0 = _ffn_in_swiglu(h2, w["ffn_w_in"][0], "l0_ffn_in")
    f0, x2 = _mm(a0, w["ffn_w_out"][0], "nn", F32, "l0_ffn_out", res=x1, gate=m0[5], gidx_for=gidx_for, gate_rows=mm_rows)

    m1 = _mods(mod_x, mod_c, 1)
    g1 = _norm_mod_fwd(x2, n1[1:2], m1[0], m1[1], gidx, "l1_norm1")
    qkvg = _mm(g1, w["ret_w_qkvg"][0], "nn", F32, "l1_qkvg")
    qk = _ret_rope(qkvg, rcos, rsin, tidx, "l1_rope")
    of, st_f, ob, st_b = _ret_fwd(qk, qkvg, log_g, B, S, L, "l1_ret")
    gn = w["ret_gn_g"].reshape(1, RET_VWIDTH)
    z1 = _gated_out_fwd(of, ob, qkvg, gn, "l1_gated_out")
    gx = lambda i: i // tiles_per_ex
    m1x = [t[:B] for t in m1]
    mo1, y1 = _mm(z1, w["ret_w_o"][0], "nn", F32, "l1_ret_out", res=x2, gate=m1x[2], gidx_for=gidx_for, gate_rows=mm_rows)
    k2 = _norm_mod_fwd(y1, n2[1:2], m1x[3], m1x[4], gx, "l1_norm2")
    ug1, uu1, a1 = _ffn_in_swiglu(k2, w["ffn_w_in"][1], "l1_ffn_in")
    f1, y2 = _mm(a1, w["ffn_w_out"][1], "nn", F32, "l1_ffn_out", res=y1, gate=m1x[5], gidx_for=gidx_for, gate_rows=mm_rows)

    loss_tile, dy2, dz, dgate5_1 = _loss_fwd_bwd(y2, target.reshape(NX, D_MODEL), f1, m1x[5], gx, B, "loss")

    zg = jnp.zeros((1, 1, D_MODEL), F32)
    gw_ffn_out1 = _mm(a1, dz, "tn", F32, "l1_ffn_out_dw")
    du = _ffn_out_bwd_swiglu(dz, w["ffn_w_out"][1], ug1, uu1, "l1_ffn_out_dx")
    gw_ffn_in1 = _mm(k2, du, "tn", F32, "l1_ffn_in_dw")
    dk2 = _mm(du, w["ffn_w_in"][1], "nt", F32, "l1_ffn_in_dx")
    dy1, dsh3_1, dsc4_1, dn2_1, dzo, dgate2_1 = _norm_mod_bwd(dk2, y1, n2[1:2], m1x[4], dy2, gx, B, "l1_norm2_bwd",
                                                              gated=(mo1, m1x[2]))
    gw_ret_o = _mm(z1, dzo, "tn", F32, "l1_ret_out_dw")
    dz1 = _mm(dzo, w["ret_w_o"][0], "nt", F32, "l1_ret_out_dx")
    do_r, dg_r, dgn = _gated_out_bwd(dz1, of, ob, qkvg, gn, "l1_gated_out_bwd")
    ((dq_f, dk_f, dv_f, dkc_f, dvc_f, dlg_f),
     (dq_b, dk_b, dv_b, dkc_b, dvc_b, dlg_b)) = _ret_bwd(qk, qkvg, log_g, st_f, st_b, do_r, B, S, L, "l1_ret_bwd")
    dqkvg = _ret_grad_assemble((dq_f, dq_b, dk_f, dk_b, dv_f, dv_b), (dkc_f, dkc_b, dvc_f, dvc_b), dg_r, rcos, rsin, S,
                               "l1_qkvg_grad")
    gw_ret_qkvg = _mm(g1, dqkvg, "tn", F32, "l1_qkvg_dw")
    grads_layer1 = {
        "ffn_w_in": gw_ffn_in1[None],
        "ffn_w_out": gw_ffn_out1[None],
        "ret_w_qkvg": gw_ret_qkvg[None],
        "ret_gn_g": dgn.reshape(1, 1, RET_VWIDTH),
        "ret_w_o": gw_ret_o[None],
    }
    if hooks is not None:
        m0[5] = hooks.layer1_grads(grads_layer1, m0[5])
    dg1 = _mm(dqkvg, w["ret_w_qkvg"][0], "nt", F32, "l1_qkvg_dx")
    dx2, dsh0_1, dsc1_1, dn1_1, dz, dgate5_0 = _norm_mod_bwd(dg1, x2, n1[1:2], m1[1], dy1, gidx, G, "l1_norm1_bwd",
                                                             gated=(f0, m0[5]))
    dlg = jnp.stack([jnp.sum(dlg_f[:, :, 0, 0], axis=0), jnp.sum(dlg_b[:, :, 0, 0], axis=0)])
    d_decay = (dlg * jax.nn.sigmoid(-small["ret_decay_logit"].reshape(2, RET_HEADS))).reshape(1, 2, RET_HEADS)

    gw_ffn_out0 = _mm(a0, dz, "tn", F32, "l0_ffn_out_dw")
    du = _ffn_out_bwd_swiglu(dz, w["ffn_w_out"][0], ug0, uu0, "l0_ffn_out_dx")
    if hooks is not None:
        m0[4] = hooks.mid_ffn0_backward(du, m0[4])
    gw_ffn_in0 = _mm(h2, du, "tn", F32, "l0_ffn_in_dw")
    if hooks is not None:
        m0[2] = hooks.ffn0_grads({"ffn_w_in": gw_ffn_in0[None], "ffn_w_out": gw_ffn_out0[None]}, m0[2])
    dh2 = _mm(du, w["ffn_w_in"][0], "nt", F32, "l0_ffn_in_dx")
    dx1, dsh3_0, dsc4_0, dn2_0, dzo, dgate2_0 = _norm_mod_bwd(dh2, x1, n2[0:1], m0[4], dx2, gidx, G, "l0_norm2_bwd",
                                                              gated=(mo0, m0[2]))
    gw_attn_o = _mm(o0, dzo, "tn", F32, "l0_attn_out_dw")
    do0 = _mm(dzo, w["attn_w_o"][0], "nt", MXU_DTYPE, "l0_attn_out_dx")
    dq_x, dk_x, dv_x, dkc1, dvc1, dsink_x = _attn_bwd(qkv_r, sink, do0, B, S, L, False, "l0_attn_x_bwd")
    dq_c, dkc2, dvc2, dsink_c = _attn_bwd(qkv_r, sink, do0, B, S, L, True, "l0_attn_c_bwd")
    if hooks is not None:
        gains = hooks.after_attn_backward(dq_x, gains)
    dqkv, dgains = _attn_prep_bwd((dq_x, dk_x, dv_x), (dq_c, dkc1, dkc2, dvc1, dvc2), qkv, gains, acos, asin, tidx,
                                  "l0_qk_prep_bwd")
    gw_attn_qkv = _mm(h1, dqkv, "tn", F32, "l0_qkv_dw")
    dh1 = _mm(dqkv, w["attn_w_qkv"][0], "nt", F32, "l0_qkv_dx")
    dx0, dsh0_0, dsc1_0, dn1_0 = _norm_mod_bwd(dh1, x0, n1[0:1], m0[1], dx1, gidx, G, "l0_norm1_bwd", dx_rows=NX)

    dgains = jnp.sum(dgains.reshape(ATTN_QK_BLOCKS, LANES // HEAD_DIM, HEAD_DIM), axis=1)
    dsink = (dsink_x + dsink_c).reshape(N_KV_HEADS, 8, LANES)[:, :GQA_GROUP, 0].reshape(1, N_HEADS)
    grads_layer0 = {
        "ffn_w_in": gw_ffn_in0[None],
        "ffn_w_out": gw_ffn_out0[None],
        "attn_w_qkv": gw_attn_qkv[None],
        "attn_w_o": gw_attn_o[None],
    }
    grads_small = {
        "norm1_g": jnp.concatenate([dn1_0, dn1_1], axis=0),
        "norm2_g": jnp.concatenate([dn2_0, dn2_1], axis=0),
        "attn_q_norm": jnp.sum(dgains[:ATTN_Q_BLOCKS], axis=0)[None],
        "attn_k_norm": jnp.sum(dgains[ATTN_Q_BLOCKS:ATTN_QK_BLOCKS], axis=0)[None],
        "attn_sink": dsink,
        "ret_decay_logit": d_decay,
    }

    def pad_g(t):
        return jnp.concatenate([t, zg], axis=0)

    d0 = jnp.concatenate([dsh0_0, dsc1_0, dgate2_0, dsh3_0, dsc4_0, dgate5_0], axis=2)[:, 0]
    d1 = jnp.concatenate([dsh0_1, dsc1_1, pad_g(dgate2_1), pad_g(dsh3_1), pad_g(dsc4_1), pad_g(dgate5_1)],
                         axis=2)[:, 0]
    dmod_x = jnp.stack([d0[:B], d1[:B]], axis=1)
    dmod_c = jnp.stack([d0[B], d1[B]], axis=0)
    return loss_tile, dx0.reshape(B, S, D_MODEL), (grads_layer0, grads_layer1), grads_small, dmod_x, dmod_c


SMALL_NAMES = ("c_ctx", "ada_b", "norm1_g", "norm2_g", "attn_q_norm", "attn_k_norm", "attn_sink", "ret_decay_logit")
ADA_ROWS = 64


def _pack_small(d, rows):
    flat = jnp.concatenate([d[k].reshape(-1) for k in SMALL_NAMES])
    n = rows * LANES
    return jnp.pad(flat, (0, n - flat.shape[0])).reshape(rows, LANES)


def _unpack_small(packed, shapes):
    flat = packed.reshape(-1)
    out, off = {}, 0
    for k in SMALL_NAMES:
        n = math.prod(shapes[k])
        out[k] = flat[off:off + n].reshape(shapes[k])
        off += n
    return out


EARLY_WEIGHTS = ("attn_w_qkv", "attn_w_o")
LATE_WEIGHTS = tuple(k for k in BIG_WEIGHTS if k not in EARLY_WEIGHTS)

_HBM = pl.BlockSpec(memory_space=pltpu.HBM)
_SEM = pl.BlockSpec(memory_space=pltpu.SEMAPHORE)
_DATAFLOW = pltpu.SideEffectType.DATAFLOW_SIDE_EFFECTING
_PEER_FLIPS = ((0, 0, 1), (0, 1, 0), (0, 1, 1), (1, 0, 0), (1, 0, 1), (1, 1, 0), (1, 1, 1))


def _wire_shard(name, t):
    return t.reshape(1, 1, -1) if name == "ret_gn_g" else t.astype(MXU_DTYPE)


def _direct_copies(x_refs, land_refs, send_sems, recv_sems, landing):
    x, y, c = _my_coords()
    out = []
    for a in range(len(x_refs)):
        for k, (bx, by, bc) in enumerate(_PEER_FLIPS):
            peer = (_flip(x, bx), _flip(y, by), _flip(c, bc))
            slot = (4 * peer[0] + 2 * peer[1] + peer[2]) if landing else (4 * x + 2 * y + c)
            out.append(pltpu.make_async_remote_copy(
                src_ref=x_refs[a], dst_ref=land_refs[a].at[:, slot], send_sem=send_sems.at[7 * a + k],
                recv_sem=recv_sems.at[7 * a + k], device_id=peer, device_id_type=MESH))
    return out


def _gather_start(shards, name):
    n = len(shards)
    lands = [lax.empty((s.shape[0], N_DEV) + s.shape[1:], s.dtype) for s in shards]

    def body(*refs):
        send_sems, recv_sems = refs[2 * n], refs[2 * n + 1]
        x_refs, land_refs = refs[2 * n + 2:3 * n + 2], refs[3 * n + 2:4 * n + 2]
        for cp in _direct_copies(x_refs, land_refs, send_sems, recv_sems, landing=False):
            cp.start()
        refs[-1][...] = jnp.zeros_like(refs[-1])

    hbm = lambda t: pltpu.with_memory_space_constraint(t, pltpu.HBM)
    res = pl.pallas_call(
        body, name=name,
        out_shape=(pltpu.SemaphoreType.DMA((7 * n,)), pltpu.SemaphoreType.DMA((7 * n,)))
        + tuple(pltpu.HBM(t.shape, t.dtype) for t in shards + lands) + (jax.ShapeDtypeStruct((8, LANES), F32),),
        in_specs=[_HBM] * (2 * n), out_specs=(_SEM, _SEM) + (_HBM,) * (2 * n) + (pl.BlockSpec(memory_space=pltpu.VMEM),),
        input_output_aliases={i: 2 + i for i in range(2 * n)},
        compiler_params=pltpu.CompilerParams(has_side_effects=_DATAFLOW))(*[hbm(t) for t in shards + lands])
    return res[0], res[1], list(res[2:2 + n]), list(res[2 + n:2 + 2 * n]), res[-1]


def _gather_wait(send_sems, recv_sems, shards, lands, after, name):
    n = len(shards)

    def body(*refs):
        x_refs, land_refs = refs[:n], refs[n:2 * n]
        for cp in _direct_copies(x_refs, land_refs, refs[2 * n], refs[2 * n + 1], landing=True):
            cp.wait_send()
            cp.wait_recv()

    res = pl.pallas_call(
        body, name=name, out_shape=tuple(pltpu.HBM(t.shape, t.dtype) for t in shards + lands),
        in_specs=[_HBM] * (2 * n) + [_SEM, _SEM, _ANY], out_specs=(_HBM,) * (2 * n),
        input_output_aliases={i: i for i in range(2 * n)},
        compiler_params=pltpu.CompilerParams(has_side_effects=_DATAFLOW))(*shards, *lands, send_sems, recv_sems, after)
    return list(res[n:])


def _gather_big_weights(weights, names, name):
    gathered = _gather_shards([_wire_shard(k, weights[k]) for k in names], name)
    return {k: _join_shards(k, g) for k, g in zip(names, gathered)}


_SIBLING = ((0, 0, 1),)
_CHIPS = ((1, 0, 0), (0, 1, 0), (1, 1, 0))
_to_sibling = lambda ref, peer: ref.at[:, :, peer[2]]
_to_chip = lambda ref, peer: ref.at[:, 2 * peer[0] + peer[1]]
_sibling_tail = lambda s: (s.shape[0], 4) + s.shape[3:]
_chip_tail = lambda s: (s.shape[0],) + s.shape[2:]


def _rs_split(grads):
    names = list(grads)
    split = []
    for k in names:
        s = _split_shards(k, grads[k])
        split.append(s.reshape(s.shape[0], 4, 2, s.shape[2], s.shape[3]))
    return names, split


def _rs_pair_sums(names, split, from_sibling, tag):
    core = lax.axis_index("c").astype(jnp.int32).reshape(1)
    return [_pair_sum(g, s, core, MXU_DTYPE, tag + k) for k, g, s in zip(names, split, from_sibling)]


def _rs_parts(names, split, from_sibling, from_chips):
    mx_, my_, mc_ = _my_coords()
    my_chip = 2 * mx_ + my_
    parts = {}
    for k, g, s, r in zip(names, split, from_sibling, from_chips):
        own_keep = lax.dynamic_index_in_dim(lax.dynamic_index_in_dim(g, my_chip, axis=1, keepdims=False), mc_, axis=1,
                                            keepdims=False)
        parts[k] = (own_keep, lax.dynamic_index_in_dim(s, my_chip, axis=1, keepdims=False), r)
    return parts


def _reduce_scatter_in_call(grads, tag):
    names, split = _rs_split(grads)
    from_sibling = [t[0] for t in _exchange_shards(split, _SIBLING, lambda ref, peer, me_: _to_sibling(ref, peer),
                                                   _sibling_tail, tag + "sibling")]
    pair = _rs_pair_sums(names, split, from_sibling, tag + "pair_")
    from_chips = _exchange_shards(pair, _CHIPS, lambda ref, peer, me_: _to_chip(ref, peer), _chip_tail, tag + "chips")
    return _rs_parts(names, split, from_sibling, from_chips)


def _exchange_copies(in_refs, land_refs, send_sems, recv_sems, masks, src_of):
    x, y, c = _my_coords()
    nm = len(masks)
    out = []
    for a in range(len(in_refs)):
        for k, (bx, by, bc) in enumerate(masks):
            peer = (_flip(x, bx), _flip(y, by), _flip(c, bc))
            out.append(pltpu.make_async_remote_copy(
                src_ref=src_of(in_refs[a], peer), dst_ref=land_refs[a].at[k], send_sem=send_sems.at[nm * a + k],
                recv_sem=recv_sems.at[nm * a + k], device_id=peer, device_id_type=MESH))
    return out


def _exchange_start(arrs, masks, src_of, out_tail, name):
    n, nm = len(arrs), len(masks)
    lands = [lax.empty((nm,) + out_tail(s), s.dtype) for s in arrs]

    def body(*refs):
        send_sems, recv_sems = refs[2 * n], refs[2 * n + 1]
        in_refs, land_refs = refs[2 * n + 2:3 * n + 2], refs[3 * n + 2:4 * n + 2]
        for cp in _exchange_copies(in_refs, land_refs, send_sems, recv_sems, masks, src_of):
            cp.start()
        refs[-1][...] = jnp.zeros_like(refs[-1])

    hbm = lambda t: pltpu.with_memory_space_constraint(t, pltpu.HBM)
    res = pl.pallas_call(
        body, name=name,
        out_shape=(pltpu.SemaphoreType.DMA((nm * n,)), pltpu.SemaphoreType.DMA((nm * n,)))
        + tuple(pltpu.HBM(t.shape, t.dtype) for t in list(arrs) + lands) + (jax.ShapeDtypeStruct((8, LANES), F32),),
        in_specs=[_HBM] * (2 * n), out_specs=(_SEM, _SEM) + (_HBM,) * (2 * n) + (pl.BlockSpec(memory_space=pltpu.VMEM),),
        input_output_aliases={i: 2 + i for i in range(2 * n)},
        compiler_params=pltpu.CompilerParams(has_side_effects=_DATAFLOW))(*[hbm(t) for t in list(arrs) + lands])
    return (res[0], res[1], list(res[2:2 + n]), list(res[2 + n:2 + 2 * n]), masks, src_of), res[-1]


def _exchange_wait(state, after, name):
    send_sems, recv_sems, arrs, lands, masks, src_of = state
    n = len(arrs)

    def body(*refs):
        for cp in _exchange_copies(refs[:n], refs[n:2 * n], refs[2 * n], refs[2 * n + 1], masks, src_of):
            cp.wait_send()
            cp.wait_recv()

    res = pl.pallas_call(
        body, name=name, out_shape=tuple(pltpu.HBM(t.shape, t.dtype) for t in arrs + lands),
        in_specs=[_HBM] * (2 * n) + [_SEM, _SEM, _ANY], out_specs=(_HBM,) * (2 * n),
        input_output_aliases={i: i for i in range(2 * n)},
        compiler_params=pltpu.CompilerParams(has_side_effects=_DATAFLOW))(*arrs, *lands, send_sems, recv_sems, after)
    return list(res[:n]), list(res[n:])


class _SplitReduce:
    def __init__(self, tag):
        self.tag = tag

    def start(self, grads, order_through):
        self.names, split = _rs_split(grads)
        self.sibling, tok = _exchange_start(split, _SIBLING, _to_sibling, _sibling_tail, self.tag + "sibling_start")
        return order_through + tok[0, 0]

    def middle(self, after, order_through):
        self.split, lands = _exchange_wait(self.sibling, after, self.tag + "sibling_wait")
        self.from_sibling = [t[0] for t in lands]
        pair = _rs_pair_sums(self.names, self.split, self.from_sibling, self.tag + "pair_")
        self.chips, tok = _exchange_start(pair, _CHIPS, _to_chip, _chip_tail, self.tag + "chips_start")
        return order_through + tok[0, 0]

    def finish(self, after):
        _, from_chips = _exchange_wait(self.chips, after, self.tag + "chips_wait")
        return _rs_parts(self.names, self.split, self.from_sibling, from_chips)


def _adamw_big(weights, mom1, mom2, part_groups):
    big = {}
    for k in BIG_WEIGHTS:
        parts = [g[k] for g in part_groups if k in g]
        own_keep = jnp.concatenate([p[0] for p in parts], axis=0)
        own_sib = jnp.concatenate([p[1] for p in parts], axis=0)
        recv = jnp.concatenate([p[2] for p in parts], axis=1)
        L_, a_, b_ = own_keep.shape
        rows = L_ * a_
        res = _adamw(weights[k].reshape(rows, b_), mom1[k].reshape(rows, b_), mom2[k].reshape(rows, b_),
                     [own_keep.reshape(1, rows, b_), own_sib.reshape(1, rows, b_), recv.reshape(3, rows, b_)],
                     "adamw_" + k)
        big[k] = [t.reshape(weights[k].shape) for t in res]
    return big


def kernel(x, c, ctx, c_ctx, ada_w, ada_b, norm1_g, norm2_g, ffn_w_in, ffn_w_out, attn_w_qkv, attn_q_norm, attn_k_norm, attn_sink, attn_w_o, ret_w_qkvg, ret_decay_logit, ret_gn_g, ret_w_o, loss_target, m_c_ctx, m_ada_w, m_ada_b, m_norm1_g, m_norm2_g, m_ffn_w_in, m_ffn_w_out, m_attn_w_qkv, m_attn_q_norm, m_attn_k_norm, m_attn_sink, m_attn_w_o, m_ret_w_qkvg, m_ret_decay_logit, m_ret_gn_g, m_ret_w_o, v_c_ctx, v_ada_w, v_ada_b, v_norm1_g, v_norm2_g, v_ffn_w_in, v_ffn_w_out, v_attn_w_qkv, v_attn_q_norm, v_attn_k_norm, v_attn_sink, v_attn_w_o, v_ret_w_qkvg, v_ret_decay_logit, v_ret_gn_g, v_ret_w_o):
    weights = dict(c_ctx=c_ctx, ada_w=ada_w, ada_b=ada_b, norm1_g=norm1_g, norm2_g=norm2_g, ffn_w_in=ffn_w_in,
                   ffn_w_out=ffn_w_out, attn_w_qkv=attn_w_qkv, attn_q_norm=attn_q_norm, attn_k_norm=attn_k_norm,
                   attn_sink=attn_sink, attn_w_o=attn_w_o, ret_w_qkvg=ret_w_qkvg, ret_decay_logit=ret_decay_logit,
                   ret_gn_g=ret_gn_g, ret_w_o=ret_w_o)
    mom1 = dict(c_ctx=m_c_ctx, ada_w=m_ada_w, ada_b=m_ada_b, norm1_g=m_norm1_g, norm2_g=m_norm2_g, ffn_w_in=m_ffn_w_in,
                ffn_w_out=m_ffn_w_out, attn_w_qkv=m_attn_w_qkv, attn_q_norm=m_attn_q_norm, attn_k_norm=m_attn_k_norm,
                attn_sink=m_attn_sink, attn_w_o=m_attn_w_o, ret_w_qkvg=m_ret_w_qkvg, ret_decay_logit=m_ret_decay_logit,
                ret_gn_g=m_ret_gn_g, ret_w_o=m_ret_w_o)
    mom2 = dict(c_ctx=v_c_ctx, ada_w=v_ada_w, ada_b=v_ada_b, norm1_g=v_norm1_g, norm2_g=v_norm2_g, ffn_w_in=v_ffn_w_in,
                ffn_w_out=v_ffn_w_out, attn_w_qkv=v_attn_w_qkv, attn_q_norm=v_attn_q_norm, attn_k_norm=v_attn_k_norm,
                attn_sink=v_attn_sink, attn_w_o=v_attn_w_o, ret_w_qkvg=v_ret_w_qkvg, ret_decay_logit=v_ret_decay_logit,
                ret_gn_g=v_ret_gn_g, ret_w_o=v_ret_w_o)
    B = x.shape[0]
    mx_, my_, mc_ = _my_coords()
    me = 4 * mx_ + 2 * my_ + mc_
    ada_cols = ada_w.shape[2]

    w_full = _gather_big_weights(weights, EARLY_WEIGHTS, "gather_early")

    c_all = _all_gather(jax.nn.silu(c), "gather_c").reshape(N_DEV * B, D_MODEL)
    cc_act = jax.nn.silu(c_ctx)[None]
    ada_in = jnp.concatenate([c_all, cc_act, jnp.zeros((ADA_ROWS - N_DEV * B - 1, D_MODEL), F32)], axis=0)
    ada_in = ada_in.astype(MXU_DTYPE)
    ada_w2 = jnp.concatenate([ada_w[0], ada_w[1]], axis=1)
    bias = lax.dynamic_slice_in_dim(ada_b.reshape(2, N_DEV, ada_cols), me, 1, axis=1).reshape(1, 2 * ada_cols)
    mod_cols = _mm(ada_in, ada_w2, "nn", F32, "ada_fwd", bias=bias)
    mod_all = _all_gather(mod_cols, "gather_mod")
    mod_all = mod_all.reshape(N_DEV, ADA_ROWS, 2, ada_cols).transpose(1, 2, 0, 3).reshape(ADA_ROWS, 2, N_DEV * ada_cols)
    mod_x = lax.dynamic_slice_in_dim(mod_all, me * B, B, axis=0)
    mod_c = mod_all[N_DEV * B]

    order = 0.0 * (mod_c[0, 0] + w_full["attn_w_o"][0, 0, 0].astype(F32))
    late_shards = [_wire_shard(k, weights[k] + order if k == "ret_gn_g" else weights[k]) for k in LATE_WEIGHTS]
    send_sems, recv_sems, late_thru, late_lands, token = _gather_start(late_shards, "gather_late_start")
    mod_x = mod_x + token[0, 0]

    def late_weights(after):
        lands = _gather_wait(send_sems, recv_sems, late_thru, late_lands, after, "gather_late_wait")
        own = [lax.dynamic_update_index_in_dim(land, shard, me, axis=1) for land, shard in zip(lands, late_shards)]
        return {k: _join_shards(k, g) for k, g in zip(LATE_WEIGHTS, own)}

    rs_layer1, rs_ffn0 = _SplitReduce("rs1_"), _SplitReduce("rs0_")

    class Hooks:
        layer1_grads = rs_layer1.start
        mid_ffn0_backward = rs_layer1.middle
        ffn0_grads = rs_ffn0.start
        after_attn_backward = rs_ffn0.middle

    small = {k: weights[k] for k in SMALL_NAMES}
    loss_tile, grad_x, (g_layer0, _), g_small, dmod_x, dmod_c = _local_step(
        x, ctx, loss_target, mod_x, mod_c, w_full, small, late_weights, Hooks)
    parts1 = rs_layer1.finish(grad_x)
    parts0_ffn = rs_ffn0.finish(grad_x)
    loss = lax.psum(loss_tile[0, 0], ("x", "y", "c"))

    n_mod = 2 * 6 * D_MODEL
    dm_rows = jnp.concatenate([dmod_x.reshape(B, n_mod), dmod_c.reshape(1, n_mod),
                               jnp.zeros((8 - B - 1, n_mod), F32)], axis=0)
    dm_all = _all_gather(dm_rows, "gather_dmod")
    dmc_tot = _sum_rows(dm_all[:, B:B + 1].reshape(N_DEV, 1, n_mod)[:, :, :].reshape(N_DEV, n_mod // LANES, LANES),
                        "sum_dmod_c").reshape(1, n_mod)
    dmod_rows = jnp.concatenate([dm_all[:, :B].reshape(N_DEV * B, n_mod), dmc_tot,
                                 jnp.zeros((ADA_ROWS - N_DEV * B - 1, n_mod), F32)], axis=0)
    dmod_mine = lax.dynamic_slice_in_dim(dmod_rows.reshape(ADA_ROWS, 2, N_DEV, ada_cols), me, 1, axis=2)
    dmod_mine = dmod_mine.reshape(ADA_ROWS, 2 * ada_cols).astype(MXU_DTYPE)
    g_ada2 = _mm(ada_in, dmod_mine, "tn", F32, "ada_dw")
    g_ada_w = jnp.stack([g_ada2[:, :ada_cols], g_ada2[:, ada_cols:]])
    dmc_mine = jnp.concatenate([dmod_mine[N_DEV * B:N_DEV * B + 1], jnp.zeros((7, 2 * ada_cols), MXU_DTYPE)], axis=0)
    dcc_part = _mm(dmc_mine, ada_w2, "nt", F32, "ada_dc")[0:1]
    g_ada_b = _sum_rows(dmod_rows[:, None, :].reshape(ADA_ROWS, n_mod // LANES, LANES), "sum_dmod_b").reshape(2, 6 * D_MODEL)
    sg = jax.nn.sigmoid(c_ctx)
    g_small["c_ctx"] = dcc_part.reshape(D_MODEL) * (sg * (1.0 + c_ctx * (1.0 - sg)))
    g_small["ada_b"] = g_ada_b * (1.0 / N_DEV)

    shapes = {k: weights[k].shape for k in SMALL_NAMES}
    n_small = sum(math.prod(s) for s in shapes.values())
    srows = -(-(-(-n_small // LANES)) // 8) * 8
    gs_all = _all_gather(_pack_small(g_small, srows), "gather_small_grads")
    sm = _adamw(_pack_small({k: weights[k] for k in SMALL_NAMES}, srows), _pack_small({k: mom1[k] for k in SMALL_NAMES}, srows),
                _pack_small({k: mom2[k] for k in SMALL_NAMES}, srows), [gs_all], "adamw_small")
    sm = [_unpack_small(t, shapes) for t in sm]

    ada_shape = ada_w.shape
    r2 = lambda t: t.reshape(ada_shape[0] * ada_shape[1], ada_shape[2])
    ada = [t.reshape(ada_shape) for t in _adamw(r2(ada_w), r2(m_ada_w), r2(v_ada_w), [r2(g_ada_w)[None]], "adamw_ada")]

    attn_grads = {k: g_layer0[k] for k in EARLY_WEIGHTS}
    big = _adamw_big(weights, mom1, mom2, [_reduce_scatter_in_call(attn_grads, "rs_"), parts0_ffn, parts1])

    def pick(i, name):
        if name in BIG_WEIGHTS:
            return big[name][i]
        if name == "ada_w":
            return ada[i]
        return sm[i][name]

    order = ("c_ctx", "ada_w", "ada_b", "norm1_g", "norm2_g", "ffn_w_in", "ffn_w_out", "attn_w_qkv", "attn_q_norm",
             "attn_k_norm", "attn_sink", "attn_w_o", "ret_w_qkvg", "ret_decay_logit", "ret_gn_g", "ret_w_o")
    outs = [loss, grad_x]
    for i in range(4):
        outs += [pick(i, n) for n in order]
    return tuple(outs)
```

```python
import functools
import math

import jax
import jax.numpy as jnp
from jax import lax
from jax.experimental import pallas as pl
from jax.experimental.pallas import tpu as pltpu

F32 = jnp.float32
MXU_DTYPE = jnp.bfloat16

D_MODEL = 1024
HEAD_DIM = 64
N_HEADS = 16
N_KV_HEADS = 4
GQA_GROUP = 4
WINDOW = 128
ATTN_BLOCK = 128
RET_HEADS = 4
RET_QK_DIM = 256
RET_V_DIM = 512
RET_VWIDTH = 2048
RET_CHUNK = 512
D_FF = 2816
GRID_W = 64
ROPE_BASE = 10000.0
EPS = 1e-6
NEG_INF = -1e30

ADAM_LR = 0.001
ADAM_B1 = 0.9
ADAM_B2 = 0.999
ADAM_EPS = 1e-08
ADAM_WD = 0.01
ADAM_STEP = 10

N_DEV = 8
LANES = 128
ROW_TILE = 512
VMEM_LIMIT = 48 * 1024 * 1024

MESH = pl.DeviceIdType.MESH
_ANY = pl.BlockSpec(memory_space=pl.ANY)
_SMEM = pl.BlockSpec(memory_space=pltpu.SMEM)


def _params(**kw):
    return pltpu.CompilerParams(vmem_limit_bytes=VMEM_LIMIT, **kw)


def _mx(v):
    return v.astype(MXU_DTYPE)


def _dot(a, b, dims):
    return lax.dot_general(_mx(a), _mx(b), (dims, ((), ())), preferred_element_type=F32)


_NN = ((1,), (0,))
_NT = ((1,), (1,))
_TN = ((0,), (0,))


def _tile(n, cands):
    for c in cands:
        if n % c == 0:
            return c
    return n


def _big_tile(n, cap):
    if n <= cap:
        return n
    for t in range(cap - cap % LANES, 0, -LANES):
        if n % t == 0:
            return t
    return n


MM_ROWS = 1024
MM_COLS = 1408
MM_DEPTH = 2048


def _k_tile(k):
    return _big_tile(k, MM_DEPTH)


def _mm(a, b, mode, out_dtype, name, *, bias=None, res=None, gate=None, gidx_for=None, gate_rows=None):
    if mode == "nn":
        (M, K), (_, N) = a.shape, b.shape
    elif mode == "nt":
        (M, K), (N, _) = a.shape, b.shape
    else:
        (K, M), (_, N) = a.shape, b.shape
    if res is not None:
        tm, tn = gate_rows, _big_tile(N, 512)
        gidx = gidx_for(tm)
    else:
        tm = _big_tile(M, MM_COLS if mode == "tn" else MM_ROWS)
        tn = _big_tile(N, MM_COLS)
    tk = _k_tile(K)
    nk = K // tk
    dims = {"nn": _NN, "nt": _NT, "tn": _TN}[mode]
    a_spec = (pl.BlockSpec((tk, tm), lambda i, j, k: (k, i)) if mode == "tn"
              else pl.BlockSpec((tm, tk), lambda i, j, k: (i, k)))
    b_spec = (pl.BlockSpec((tn, tk), lambda i, j, k: (j, k)) if mode == "nt"
              else pl.BlockSpec((tk, tn), lambda i, j, k: (k, j)))
    o_spec = pl.BlockSpec((tm, tn), lambda i, j, k: (i, j))
    in_specs, operands = [a_spec, b_spec], [a, b]
    if bias is not None:
        in_specs.append(pl.BlockSpec((1, tn), lambda i, j, k: (0, j)))
        operands.append(bias)
    if res is not None:
        in_specs += [o_spec, pl.BlockSpec((1, 1, tn), lambda i, j, k: (gidx(i), 0, j))]
        operands += [res, gate]
        out_shape = (jax.ShapeDtypeStruct((M, N), F32), jax.ShapeDtypeStruct((M, N), F32))
        out_specs = (o_spec, o_spec)
    else:
        out_shape = jax.ShapeDtypeStruct((M, N), out_dtype)
        out_specs = o_spec

    def body(*refs):
        a_ref, b_ref = refs[0], refs[1]
        extra = refs[2:len(operands)]
        outs = refs[len(operands):]
        prod = _dot(a_ref[...], b_ref[...], dims)

        def finish(acc):
            if bias is not None:
                outs[0][...] = (acc + extra[0][...]).astype(out_dtype)
            elif res is not None:
                outs[0][...] = acc
                outs[1][...] = extra[0][...] + extra[1][0] * acc
            else:
                outs[0][...] = acc.astype(out_dtype)

        if nk == 1:
            finish(prod)
        else:
            acc_ref = outs[-1]
            outs = outs[:-1]
            k = pl.program_id(2)

            @pl.when(k == 0)
            def _():
                acc_ref[...] = prod

            @pl.when(k > 0)
            def _():
                acc_ref[...] += prod

            @pl.when(k == nk - 1)
            def _():
                finish(acc_ref[...])

    return pl.pallas_call(
        body, name=name, grid=(M // tm, N // tn, nk), in_specs=in_specs, out_specs=out_specs, out_shape=out_shape,
        scratch_shapes=[pltpu.VMEM((tm, tn), F32)] if nk > 1 else [],
        compiler_params=_params())(*operands)


def _group_index(n_x_tiles, tiles_per_example, n_examples):
    def gidx(i):
        return jnp.where(i < n_x_tiles, i // tiles_per_example, n_examples)
    return gidx


def _norm_mod_fwd(x, g, shift, scale, gidx, name):
    T, Dm = x.shape

    def body(x_ref, g_ref, sh_ref, sc_ref, h_ref):
        xv = x_ref[...]
        r = lax.rsqrt(jnp.mean(xv * xv, axis=-1, keepdims=True) + EPS)
        y = xv * r * g_ref[...]
        h_ref[...] = (y * (1.0 + sc_ref[0]) + sh_ref[0]).astype(h_ref.dtype)

    row = pl.BlockSpec((ROW_TILE, Dm), lambda i: (i, 0))
    mod = pl.BlockSpec((1, 1, Dm), lambda i: (gidx(i), 0, 0))
    return pl.pallas_call(
        body, name=name, grid=(T // ROW_TILE,),
        in_specs=[row, pl.BlockSpec((1, Dm), lambda i: (0, 0)), mod, mod],
        out_specs=row, out_shape=jax.ShapeDtypeStruct((T, Dm), MXU_DTYPE),
        compiler_params=_params())(x, g, shift, scale)


def _first_of_group(i, gidx):
    return jnp.logical_or(i == 0, gidx(i) != gidx(jnp.maximum(i - 1, 0)))


def _norm_mod_bwd(dh, x, g, scale, dres, gidx, n_groups, name, gated=None, dx_rows=None):
    T, Dm = x.shape
    res_tiles = dres.shape[0] // ROW_TILE
    dx_tiles = (dx_rows or T) // ROW_TILE

    def body(*refs):
        dh_ref, x_ref, g_ref, sc_ref, dres_ref = refs[:5]
        n_in = 7 if gated else 5
        dx_ref, dsh_ref, dsc_ref, dg_ref = refs[n_in:n_in + 4]
        i = pl.program_id(0)
        xv, dhv = x_ref[...], dh_ref[...]
        r = lax.rsqrt(jnp.mean(xv * xv, axis=-1, keepdims=True) + EPS)
        xn = xv * r
        y = xn * g_ref[...]

        @pl.when(_first_of_group(i, gidx))
        def _():
            dsh_ref[...] = jnp.zeros_like(dsh_ref)
            dsc_ref[...] = jnp.zeros_like(dsc_ref)

        @pl.when(i == 0)
        def _():
            dg_ref[...] = jnp.zeros_like(dg_ref)

        dsh_ref[0] += jnp.sum(dhv, axis=0, keepdims=True)
        dsc_ref[0] += jnp.sum(dhv * y, axis=0, keepdims=True)
        dy = dhv * (1.0 + sc_ref[0])
        dg_ref[...] += jnp.sum(dy * xn, axis=0, keepdims=True)
        dxn = dy * g_ref[...]
        dx = r * (dxn - xn * jnp.mean(dxn * xn, axis=-1, keepdims=True))
        dx = dx + (dres_ref[...] if res_tiles == T // ROW_TILE else jnp.where(i < res_tiles, dres_ref[...], 0.0))
        if dx_tiles == T // ROW_TILE:
            dx_ref[...] = dx
        else:
            @pl.when(i < dx_tiles)
            def _():
                dx_ref[...] = dx
        if gated:
            f_ref, gate_ref = refs[5:7]
            dz_ref, dgate_ref = refs[n_in + 4:]

            @pl.when(_first_of_group(i, gidx))
            def _():
                dgate_ref[...] = jnp.zeros_like(dgate_ref)

            dgate_ref[0] += jnp.sum(dx * f_ref[...], axis=0, keepdims=True)
            dz_ref[...] = (dx * gate_ref[0]).astype(dz_ref.dtype)

    row = pl.BlockSpec((ROW_TILE, Dm), lambda i: (i, 0))
    mod = pl.BlockSpec((1, 1, Dm), lambda i: (gidx(i), 0, 0))
    vec = pl.BlockSpec((1, Dm), lambda i: (0, 0))
    mod_shape = jax.ShapeDtypeStruct((n_groups, 1, Dm), F32)
    res_row = pl.BlockSpec((ROW_TILE, Dm), lambda i: (jnp.minimum(i, res_tiles - 1), 0))
    in_specs, operands = [row, row, vec, mod, res_row], [dh, x, g, scale, dres]
    dx_row = pl.BlockSpec((ROW_TILE, Dm), lambda i: (jnp.minimum(i, dx_tiles - 1), 0))
    out_specs = [dx_row, mod, mod, vec]
    out_shape = [jax.ShapeDtypeStruct((dx_tiles * ROW_TILE, Dm), F32), mod_shape, mod_shape,
                 jax.ShapeDtypeStruct((1, Dm), F32)]
    if gated:
        in_specs, operands = in_specs + [row, mod], operands + list(gated)
        out_specs, out_shape = out_specs + [row, mod], out_shape + [jax.ShapeDtypeStruct((T, Dm), MXU_DTYPE), mod_shape]
    return pl.pallas_call(
        body, name=name, grid=(T // ROW_TILE,), in_specs=in_specs, out_specs=tuple(out_specs),
        out_shape=tuple(out_shape), compiler_params=_params())(*operands)


FFN_IN_ROWS = 512
FFN_IN_COLS = 1408
FFN_BWD_ROWS = 256


def _ffn_in_swiglu(h, w_in, name):
    T, Dm = h.shape
    nj = D_FF // FFN_IN_COLS

    def body(h_ref, wg_ref, wu_ref, g_ref, u_ref, a_ref):
        hv = h_ref[...]
        gate = _dot(hv, wg_ref[...], _NN)
        up = _dot(hv, wu_ref[...], _NN)
        g_ref[...] = gate
        u_ref[...] = up
        a_ref[...] = (gate * jax.nn.sigmoid(gate) * up).astype(a_ref.dtype)

    out = pl.BlockSpec((FFN_IN_ROWS, FFN_IN_COLS), lambda i, j: (i, j))
    pre = jax.ShapeDtypeStruct((T, D_FF), F32)
    return pl.pallas_call(
        body, name=name, grid=(T // FFN_IN_ROWS, nj),
        in_specs=[pl.BlockSpec((FFN_IN_ROWS, Dm), lambda i, j: (i, 0)),
                  pl.BlockSpec((Dm, FFN_IN_COLS), lambda i, j: (0, j)),
                  pl.BlockSpec((Dm, FFN_IN_COLS), lambda i, j: (0, nj + j))],
        out_specs=(out, out, out), out_shape=(pre, pre, jax.ShapeDtypeStruct((T, D_FF), MXU_DTYPE)),
        compiler_params=_params())(h, w_in, w_in)


def _ffn_out_bwd_swiglu(dz, w_out, gate, up, name):
    T, Dm = dz.shape

    def body(dz_ref, w_ref, g_ref, u_ref, du_ref):
        da = _dot(dz_ref[...], w_ref[...], _NT)
        gv, uv = g_ref[...], u_ref[...]
        sg = jax.nn.sigmoid(gv)
        du_ref[:, :D_FF] = (da * uv * (sg * (1.0 + gv * (1.0 - sg)))).astype(du_ref.dtype)
        du_ref[:, D_FF:] = (da * gv * sg).astype(du_ref.dtype)

    half = pl.BlockSpec((FFN_BWD_ROWS, D_FF), lambda i: (i, 0))
    return pl.pallas_call(
        body, name=name, grid=(T // FFN_BWD_ROWS,),
        in_specs=[pl.BlockSpec((FFN_BWD_ROWS, Dm), lambda i: (i, 0)), pl.BlockSpec((D_FF, Dm), lambda i: (0, 0)), half, half],
        out_specs=pl.BlockSpec((FFN_BWD_ROWS, 2 * D_FF), lambda i: (i, 0)),
        out_shape=jax.ShapeDtypeStruct((T, 2 * D_FF), MXU_DTYPE), compiler_params=_params())(dz, w_out, gate, up)


def _loss_fwd_bwd(y, target, f, gate, gidx, n_groups, name):
    T, Dm = y.shape

    def body(y_ref, t_ref, f_ref, gate_ref, loss_ref, dy_ref, dz_ref, dgate_ref):
        i = pl.program_id(0)
        err = y_ref[...] - t_ref[...]

        @pl.when(i == 0)
        def _():
            loss_ref[...] = jnp.zeros_like(loss_ref)

        @pl.when(_first_of_group(i, gidx))
        def _():
            dgate_ref[...] = jnp.zeros_like(dgate_ref)

        loss_ref[...] += 0.5 * jnp.sum(jnp.mean(err * err, axis=-1, keepdims=True))
        dy = err * (1.0 / Dm)
        dy_ref[...] = dy
        dgate_ref[0] += jnp.sum(dy * f_ref[...], axis=0, keepdims=True)
        dz_ref[...] = (dy * gate_ref[0]).astype(dz_ref.dtype)

    row = pl.BlockSpec((ROW_TILE, Dm), lambda i: (i, 0))
    mod = pl.BlockSpec((1, 1, Dm), lambda i: (gidx(i), 0, 0))
    return pl.pallas_call(
        body, name=name, grid=(T // ROW_TILE,), in_specs=[row, row, row, mod],
        out_specs=(pl.BlockSpec((8, LANES), lambda i: (0, 0)), row, row, mod),
        out_shape=(jax.ShapeDtypeStruct((8, LANES), F32), jax.ShapeDtypeStruct((T, Dm), F32),
                   jax.ShapeDtypeStruct((T, Dm), MXU_DTYPE), jax.ShapeDtypeStruct((n_groups, 1, Dm), F32)),
        compiler_params=_params())(y, target, f, gate)


def _rope_tables(seq, head_dim):
    axis_dim = head_dim // 2
    half = axis_dim // 2
    pos = jnp.arange(seq, dtype=jnp.int32)
    row = (pos // GRID_W).astype(F32)[:, None]
    col = (pos % GRID_W).astype(F32)[:, None]
    inv = ROPE_BASE ** (-jnp.arange(0, axis_dim, 2, dtype=F32) / axis_dim)
    lane = jnp.arange(head_dim, dtype=jnp.int32)
    within = lane % axis_dim
    ang = jnp.where((lane // axis_dim == 0)[None, :], row, col) * inv[within % half][None, :]
    cos = jnp.cos(ang)
    sin = jnp.where((within < half)[None, :], -jnp.sin(ang), jnp.sin(ang))
    cos = jnp.concatenate([cos, jnp.ones((ROW_TILE, head_dim), F32)], axis=0)
    sin = jnp.concatenate([sin, jnp.zeros((ROW_TILE, head_dim), F32)], axis=0)
    return cos, sin


def _pair_swap(v, half):
    if 2 * half == LANES:
        return pltpu.roll(v, half, axis=1)
    lane = lax.broadcasted_iota(jnp.int32, v.shape, 1)
    return jnp.where((lane % (2 * half)) < half, pltpu.roll(v, LANES - half, axis=1), pltpu.roll(v, half, axis=1))


def _head_sum(v, ones_ref):
    hi = v.astype(MXU_DTYPE)
    lo = (v - hi.astype(F32)).astype(MXU_DTYPE)
    return (jnp.dot(hi, ones_ref[...], preferred_element_type=F32)
            + jnp.dot(lo, ones_ref[...], preferred_element_type=F32))


def _head_ones():
    lane = jnp.arange(LANES)
    return (lane[:, None] // HEAD_DIM == lane[None, :] // HEAD_DIM).astype(MXU_DTYPE)


ATTN_QK_BLOCKS = (N_HEADS + N_KV_HEADS) * HEAD_DIM // LANES
ATTN_ALL_BLOCKS = (N_HEADS + 2 * N_KV_HEADS) * HEAD_DIM // LANES
ATTN_Q_BLOCKS = N_HEADS * HEAD_DIM // LANES
ATTN_SCALE = HEAD_DIM ** -0.5


def _attn_prep_fwd(qkv, gains, cos, sin, tidx, name):
    T, W = qkv.shape

    def body(x_ref, g_ref, cos_ref, sin_ref, ones_ref, o_ref):
        for cb in range(ATTN_ALL_BLOCKS):
            cols = slice(cb * LANES, (cb + 1) * LANES)
            xv = x_ref[:, cols]
            if cb < ATTN_QK_BLOCKS:
                r = lax.rsqrt(_head_sum(xv * xv, ones_ref) * (1.0 / HEAD_DIM) + EPS)
                y = xv * r * g_ref[0 if cb < ATTN_Q_BLOCKS else 1]
                xv = y * cos_ref[...] + _pair_swap(y, HEAD_DIM // 4) * sin_ref[...]
                if cb < ATTN_Q_BLOCKS:
                    xv = xv * ATTN_SCALE
            o_ref[:, cols] = xv.astype(o_ref.dtype)

    row = pl.BlockSpec((ROW_TILE, W), lambda i: (i, 0))
    tab = pl.BlockSpec((ROW_TILE, LANES), lambda i: (tidx(i), 0))
    return pl.pallas_call(
        body, name=name, grid=(T // ROW_TILE,),
        in_specs=[row, pl.BlockSpec((2, 1, LANES), lambda i: (0, 0, 0)), tab, tab,
                  pl.BlockSpec((LANES, LANES), lambda i: (0, 0))],
        out_specs=row, out_shape=jax.ShapeDtypeStruct(qkv.shape, MXU_DTYPE),
        compiler_params=_params())(qkv, gains, cos, sin, _head_ones())


def _attn_prep_bwd(latent, context, qkv, gains, cos, sin, tidx, name):
    T, W = qkv.shape
    qk_w = ATTN_QK_BLOCKS * LANES
    q_w = ATTN_Q_BLOCKS * LANES
    n_x = latent[0].shape[0] // ROW_TILE

    def body(dqx_ref, dkx_ref, dvx_ref, dqc_ref, dkc1_ref, dkc2_ref, dvc1_ref, dvc2_ref,
             x_ref, g_ref, cos_ref, sin_ref, ones_ref, o_ref, dg_ref):
        is_latent = pl.program_id(0) < n_x

        @pl.when(pl.program_id(0) == 0)
        def _():
            dg_ref[...] = jnp.zeros_like(dg_ref)

        for cb in range(ATTN_QK_BLOCKS):
            cols = slice(cb * LANES, (cb + 1) * LANES)
            xv = x_ref[:, cols]
            if cb < ATTN_Q_BLOCKS:
                d = jnp.where(is_latent, dqx_ref[:, cols], dqc_ref[:, cols]) * ATTN_SCALE
            else:
                kc = slice(cb * LANES - q_w, (cb + 1) * LANES - q_w)
                d = jnp.where(is_latent, dkx_ref[:, kc], dkc1_ref[:, kc] + dkc2_ref[:, kc])
            r = lax.rsqrt(_head_sum(xv * xv, ones_ref) * (1.0 / HEAD_DIM) + EPS)
            xn = xv * r
            dy = d * cos_ref[...] + _pair_swap(d * sin_ref[...], HEAD_DIM // 4)
            dg_ref[:, cols] += jnp.sum(dy * xn, axis=0, keepdims=True)
            dxn = dy * g_ref[0 if cb < ATTN_Q_BLOCKS else 1]
            dx = r * (dxn - xn * (_head_sum(dxn * xn, ones_ref) * (1.0 / HEAD_DIM)))
            o_ref[:, cols] = dx.astype(o_ref.dtype)
        o_ref[:, qk_w:] = jnp.where(is_latent, dvx_ref[...], dvc1_ref[...] + dvc2_ref[...]).astype(o_ref.dtype)

    row = lambda w: pl.BlockSpec((ROW_TILE, w), lambda i: (i, 0))
    lat = lambda t: pl.BlockSpec((ROW_TILE, t.shape[1]), lambda i: (jnp.minimum(i, n_x - 1), 0))
    ctx = lambda t: pl.BlockSpec((ROW_TILE, t.shape[1]), lambda i: (jnp.maximum(i - n_x, 0), 0))
    tab = pl.BlockSpec((ROW_TILE, LANES), lambda i: (tidx(i), 0))
    return pl.pallas_call(
        body, name=name, grid=(T // ROW_TILE,),
        in_specs=[lat(t) for t in latent] + [ctx(t) for t in context]
        + [row(W), pl.BlockSpec((2, 1, LANES), lambda i: (0, 0, 0)), tab, tab, pl.BlockSpec((LANES, LANES), lambda i: (0, 0))],
        out_specs=(row(W), pl.BlockSpec((1, qk_w), lambda i: (0, 0))),
        out_shape=(jax.ShapeDtypeStruct(qkv.shape, MXU_DTYPE), jax.ShapeDtypeStruct((1, qk_w), F32)),
        compiler_params=_params())(*latent, *context, qkv, gains, cos, sin, _head_ones())


RET_QK_BLOCKS = 2 * RET_HEADS * RET_QK_DIM // LANES


def _ret_rope(x, cos, sin, tidx, name):
    T = x.shape[0]
    W = RET_QK_BLOCKS * LANES
    k_scale = RET_QK_DIM ** -0.5

    def body(x_ref, cos_ref, sin_ref, o_ref):
        for cb in range(RET_QK_BLOCKS):
            cols = slice(cb * LANES, (cb + 1) * LANES)
            tcols = slice((cb % 2) * LANES, (cb % 2 + 1) * LANES)
            xv = x_ref[:, cols]
            out = xv * cos_ref[:, tcols] + pltpu.roll(xv, LANES // 2, axis=1) * sin_ref[:, tcols]
            if cb >= RET_QK_BLOCKS // 2:
                out = out * k_scale
            o_ref[:, cols] = out

    row = pl.BlockSpec((ROW_TILE, W), lambda i: (i, 0))
    tab = pl.BlockSpec((ROW_TILE, RET_QK_DIM), lambda i: (tidx(i), 0))
    return pl.pallas_call(
        body, name=name, grid=(T // ROW_TILE,), in_specs=[row, tab, tab], out_specs=row,
        out_shape=jax.ShapeDtypeStruct((T, W), F32), compiler_params=_params())(x, cos, sin)


ASSEMBLE_ROWS = 256


def _ret_grad_assemble(x_parts, c_parts, dg, cos, sin, seq, name):
    NX, NC = x_parts[0].shape[0], c_parts[0].shape[0]
    T = NX + NC
    rt = ASSEMBLE_ROWS
    nxt = NX // rt
    qk_w = RET_HEADS * RET_QK_DIM
    k_scale = RET_QK_DIM ** -0.5

    def unrotate(d, cos_ref, sin_ref, scale):
        outs = []
        for cb in range(qk_w // LANES):
            cols = slice(cb * LANES, (cb + 1) * LANES)
            tcols = slice((cb % 2) * LANES, (cb % 2 + 1) * LANES)
            dv_ = d[:, cols]
            o = dv_ * cos_ref[:, tcols] + pltpu.roll(dv_ * sin_ref[:, tcols], LANES // 2, axis=1)
            outs.append(o * scale if scale != 1.0 else o)
        return outs

    def body(dqf, dqb, dkf, dkb, dvf, dvb, dg_ref, dkcf, dkcb, dvcf, dvcb, cos_ref, sin_ref, o_ref):
        i = pl.program_id(0)

        def write_k(parts):
            for cb, o in enumerate(parts):
                o_ref[:, qk_w + cb * LANES:qk_w + (cb + 1) * LANES] = o.astype(o_ref.dtype)

        @pl.when(i < nxt)
        def _():
            for cb, o in enumerate(unrotate(dqf[...] + dqb[...], cos_ref, sin_ref, 1.0)):
                o_ref[:, cb * LANES:(cb + 1) * LANES] = o.astype(o_ref.dtype)
            write_k(unrotate(dkf[...] + dkb[...], cos_ref, sin_ref, k_scale))
            o_ref[:, 2 * qk_w:2 * qk_w + RET_VWIDTH] = (dvf[...] + dvb[...]).astype(o_ref.dtype)
            o_ref[:, 2 * qk_w + RET_VWIDTH:] = dg_ref[...].astype(o_ref.dtype)

        @pl.when(i >= nxt)
        def _():
            o_ref[:, :qk_w] = jnp.zeros((rt, qk_w), o_ref.dtype)
            write_k(unrotate(dkcf[...] + dkcb[...], cos_ref, sin_ref, k_scale))
            o_ref[:, 2 * qk_w:2 * qk_w + RET_VWIDTH] = (dvcf[...] + dvcb[...]).astype(o_ref.dtype)
            o_ref[:, 2 * qk_w + RET_VWIDTH:] = jnp.zeros((rt, RET_VWIDTH), o_ref.dtype)

    xs = lambda w: pl.BlockSpec((rt, w), lambda i: (jnp.minimum(i, nxt - 1), 0))
    cs = lambda w: pl.BlockSpec((rt, w), lambda i: (jnp.maximum(i - nxt, 0), 0))
    tab = pl.BlockSpec((rt, RET_QK_DIM), lambda i: (jnp.where(i < nxt, i % (seq // rt), seq // rt), 0))
    return pl.pallas_call(
        body, name=name, grid=(T // rt,),
        in_specs=[xs(qk_w)] * 4 + [xs(RET_VWIDTH)] * 3 + [cs(qk_w)] * 2 + [cs(RET_VWIDTH)] * 2 + [tab, tab],
        out_specs=pl.BlockSpec((rt, 2 * qk_w + 2 * RET_VWIDTH), lambda i: (i, 0)),
        out_shape=jax.ShapeDtypeStruct((T, 2 * qk_w + 2 * RET_VWIDTH), MXU_DTYPE),
        compiler_params=_params())(*x_parts, dg, *c_parts, cos, sin)


def _band_bias(qb, seq):
    nb = seq // qb
    assert nb >= 2
    i = jnp.arange(GQA_GROUP * qb, dtype=jnp.int32)[:, None] % qb
    n = jnp.arange(3 * qb, dtype=jnp.int32)[None, :]
    in_window = (n >= i) & (n - i <= 2 * WINDOW)
    variants = [in_window & (n >= qb), in_window, in_window & (n < 2 * qb)]
    return jnp.stack([jnp.where(v, 0.0, NEG_INF).astype(F32) for v in variants])


GROUP_ORDER = (0, 2, 1, 3)


def _stack_halves(blk):
    return jnp.concatenate([blk[:, :LANES], blk[:, LANES:]], axis=0)


def _unstack_halves(v, rows):
    return jnp.concatenate([v[:rows], v[rows:]], axis=1)


def _align_head(pair, odd):
    lane = lax.broadcasted_iota(jnp.int32, pair.shape, 1)
    mine = jnp.where((lane >= HEAD_DIM) == odd, pair, jnp.zeros_like(pair))
    rolled = pltpu.roll(mine, HEAD_DIM, axis=1)
    return jnp.where(odd, rolled, mine), jnp.where(odd, mine, rolled)


def _scores(out_ref, q2, x_eo):
    half = q2.shape[0]
    out_ref[:half, :] = _dot(q2, x_eo[0], _NT)
    out_ref[half:, :] = _dot(q2, x_eo[1], _NT)


def _apply(p_ref, x_eo):
    half = p_ref.shape[0] // 2
    return _dot(p_ref[:half, :], x_eo[0], _NN) + _dot(p_ref[half:, :], x_eo[1], _NN)


def _kv_grad(a_ref, q2, odd):
    half = a_ref.shape[0] // 2
    even_t = _dot(q2, a_ref[:half, :], _TN)
    odd_t = _dot(q2, a_ref[half:, :], _TN)
    mine = even_t[:HEAD_DIM] + odd_t[HEAD_DIM:]
    zero = jnp.zeros_like(mine)
    placed = jnp.where(odd, jnp.concatenate([zero, mine], axis=0), jnp.concatenate([mine, zero], axis=0))
    return placed.T


ATTN_ROW_CHUNK = 32


def _softmax_chunks(s_c_ref, s_l_ref, bias_ref, sink_ref, kv_head, qb, emit):
    for r0 in range(0, GQA_GROUP * qb, ATTN_ROW_CHUNK):
        rows = slice(r0, r0 + ATTN_ROW_CHUNK)
        t = r0 // qb
        sink = jnp.full((ATTN_ROW_CHUNK, 1), sink_ref[kv_head, GROUP_ORDER[t]], F32)
        s_c = s_c_ref[rows, :]
        m = jnp.maximum(jnp.max(s_c, axis=-1, keepdims=True), sink)
        s_l = None
        if s_l_ref is not None:
            s_l = s_l_ref[rows, :] + bias_ref[0, rows, :]
            m = jnp.maximum(m, jnp.max(s_l, axis=-1, keepdims=True))
        e_c = jnp.exp(s_c - m)
        e_s = jnp.exp(sink - m)
        den = jnp.sum(e_c, axis=-1, keepdims=True) + e_s
        e_l = None
        if s_l_ref is not None:
            e_l = jnp.exp(s_l - m)
            den = den + jnp.sum(e_l, axis=-1, keepdims=True)
        inv = 1.0 / den
        emit(t, rows, e_c * inv, (None if e_l is None else e_l * inv), e_s * inv)


GROUP_W = GQA_GROUP * HEAD_DIM
K_LANE_BLOCK = N_HEADS * HEAD_DIM // LANES
V_LANE_BLOCK = K_LANE_BLOCK + N_KV_HEADS * HEAD_DIM // LANES


def _attn_specs(B, seq, ctx_len, ctx_queries):
    ctx0 = B * seq // ctx_len
    if ctx_queries:
        qb, nb = ctx_len, 1
        qrow = lambda b, j: ctx0 + b
    else:
        qb, nb = ATTN_BLOCK, seq // ATTN_BLOCK
        qrow = lambda b, j: b * nb + j
    kv_w = N_KV_HEADS * HEAD_DIM
    k_blk, v_blk = N_HEADS * HEAD_DIM // kv_w, N_HEADS * HEAD_DIM // kv_w + 1
    q_spec = pl.BlockSpec((qb, N_HEADS * HEAD_DIM), lambda b, p, j: (qrow(b, j), 0))
    c_specs = [pl.BlockSpec((ctx_len, kv_w), lambda b, p, j: (ctx0 + b, k_blk)),
               pl.BlockSpec((ctx_len, kv_w), lambda b, p, j: (ctx0 + b, v_blk))]
    local = []
    if not ctx_queries:
        near = [lambda j: jnp.maximum(j - 1, 0), lambda j: j, lambda j: jnp.minimum(j + 1, nb - 1)]
        for blk in (k_blk, v_blk):
            for f in near:
                local.append(pl.BlockSpec((qb, kv_w), lambda b, p, j, f=f, blk=blk: (b * nb + f(j), blk)))
        local.append(pl.BlockSpec(
            (1, GQA_GROUP * qb, 3 * qb), lambda b, p, j: (jnp.where(j == 0, 0, jnp.where(j == nb - 1, 2, 1)), 0, 0)))
    return qb, nb, qrow, q_spec, c_specs, local


def _attn_operands(refs, has_local, head):
    odd = bool(head % 2)
    pair = slice((head // 2) * LANES, (head // 2 + 1) * LANES)
    n_local = 7 if has_local else 0
    q2 = _stack_halves(refs[0][:, head * GROUP_W:(head + 1) * GROUP_W])
    kc = _align_head(refs[1 + n_local][:, pair], odd)
    vc = _align_head(refs[2 + n_local][:, pair], odd)
    kl = vl = bias_ref = None
    if has_local:
        kl = _align_head(jnp.concatenate([r[:, pair] for r in refs[1:4]], axis=0), odd)
        vl = _align_head(jnp.concatenate([r[:, pair] for r in refs[4:7]], axis=0), odd)
        bias_ref = refs[7]
    return odd, q2, kc, vc, kl, vl, bias_ref


def _score_scratch(qb, ctx_len, has_local, dtypes):
    rows = GQA_GROUP * qb
    out = []
    for dt in dtypes:
        out.append(pltpu.VMEM((rows, ctx_len), dt))
        if has_local:
            out.append(pltpu.VMEM((rows, 3 * qb), dt))
    return out


def _score_bufs(scratch, has_local):
    if has_local:
        return [(scratch[i], scratch[i + 1]) for i in range(0, len(scratch), 2)]
    return [(s, None) for s in scratch]


def _attn_fwd(qkv, sink, B, seq, ctx_len, ctx_queries, name):
    has_local = not ctx_queries
    qb, nb, _, q_spec, c_specs, local = _attn_specs(B, seq, ctx_len, ctx_queries)
    n_rows = B * (ctx_len if ctx_queries else seq)
    n_in = 1 + (7 if has_local else 0) + 3

    per_head = _score_scratch(qb, ctx_len, has_local, (F32, MXU_DTYPE))

    def body(*refs):
        sink_ref, o_ref = refs[n_in - 1], refs[n_in]
        scratch = refs[n_in + 1:]
        for kv_head in range(N_KV_HEADS):
            sub = kv_head
            (s_c_ref, s_l_ref), (p_c_ref, p_l_ref) = _score_bufs(
                scratch[sub * len(per_head):(sub + 1) * len(per_head)], has_local)
            _, q2, kc, vc, kl, vl, bias_ref = _attn_operands(refs, has_local, kv_head)
            _scores(s_c_ref, q2, kc)
            if has_local:
                _scores(s_l_ref, q2, kl)

            def emit(t, rows, p_c, p_l, p_s, p_c_ref=p_c_ref, p_l_ref=p_l_ref):
                p_c_ref[rows, :] = p_c.astype(p_c_ref.dtype)
                if has_local:
                    p_l_ref[rows, :] = p_l.astype(p_l_ref.dtype)

            _softmax_chunks(s_c_ref, s_l_ref, bias_ref, sink_ref, kv_head, qb, emit)
            o2 = _apply(p_c_ref, vc)
            if has_local:
                o2 = o2 + _apply(p_l_ref, vl)
            o_ref[:, sub * GROUP_W:(sub + 1) * GROUP_W] = _unstack_halves(o2, qb).astype(o_ref.dtype)

    operands = [qkv] + ([qkv] * 6 + [_band_bias(qb, seq)] if has_local else []) + [qkv, qkv, sink]
    return pl.pallas_call(
        body, name=name, grid=(B, 1, nb),
        in_specs=[q_spec] + local + c_specs + [_SMEM],
        out_specs=pl.BlockSpec((qb, N_HEADS * HEAD_DIM), lambda b, p, j: (b * nb + j, 0)),
        out_shape=jax.ShapeDtypeStruct((n_rows, N_HEADS * HEAD_DIM), MXU_DTYPE),
        scratch_shapes=per_head * N_KV_HEADS, compiler_params=_params())(*operands)


def _attn_bwd(qkv, sink, do, B, seq, ctx_len, ctx_queries, name):
    has_local = not ctx_queries
    qb, nb, qrow, q_spec, c_specs, local = _attn_specs(B, seq, ctx_len, ctx_queries)
    n_rows = B * (ctx_len if ctx_queries else seq)

    n_out = 6 if has_local else 4
    per_head = _score_scratch(qb, ctx_len, has_local, (F32, F32, MXU_DTYPE, MXU_DTYPE))

    def body(*refs):
        n_in = 1 + (7 if has_local else 0) + 4
        sink_ref, do_ref = refs[n_in - 2:n_in]
        outs = refs[n_in:n_in + n_out]
        scratch = refs[n_in + n_out:]
        dq_ref = outs[0]
        dkc_ref, dvc_ref, dsink_ref = outs[-3:]
        b, j = pl.program_id(0), pl.program_id(2)

        @pl.when(j == 0)
        def _():
            dkc_ref[...] = jnp.zeros_like(dkc_ref)
            dvc_ref[...] = jnp.zeros_like(dvc_ref)
            if has_local:
                outs[1][...] = jnp.zeros_like(outs[1])
                outs[2][...] = jnp.zeros_like(outs[2])

        @pl.when((b == 0) & (j == 0))
        def _():
            dsink_ref[...] = jnp.zeros_like(dsink_ref)

        for kv_head in range(N_KV_HEADS):
            sub = kv_head
            lanes = slice((kv_head // 2) * LANES, (kv_head // 2 + 1) * LANES)
            (s_c_ref, s_l_ref), (dp_c_ref, dp_l_ref), (p_c_ref, p_l_ref), (ds_c_ref, ds_l_ref) = _score_bufs(
                scratch[sub * len(per_head):(sub + 1) * len(per_head)], has_local)
            odd, q2, kc, vc, kl, vl, bias_ref = _attn_operands(refs, has_local, kv_head)
            do2 = _stack_halves(do_ref[:, sub * GROUP_W:(sub + 1) * GROUP_W])
            _scores(s_c_ref, q2, kc)
            _scores(dp_c_ref, do2, vc)
            if has_local:
                _scores(s_l_ref, q2, kl)
                _scores(dp_l_ref, do2, vl)
            dsink_parts = [jnp.zeros((), F32)] * GQA_GROUP

            def emit(t, rows, p_c, p_l, p_s, dp_c_ref=dp_c_ref, dp_l_ref=dp_l_ref, p_c_ref=p_c_ref, p_l_ref=p_l_ref,
                     ds_c_ref=ds_c_ref, ds_l_ref=ds_l_ref, dsink_parts=dsink_parts):
                dp_c = dp_c_ref[rows, :]
                delta = jnp.sum(p_c * dp_c, axis=-1, keepdims=True)
                if has_local:
                    dp_l = dp_l_ref[rows, :]
                    delta = delta + jnp.sum(p_l * dp_l, axis=-1, keepdims=True)
                    p_l_ref[rows, :] = p_l.astype(p_l_ref.dtype)
                    ds_l_ref[rows, :] = (p_l * (dp_l - delta)).astype(ds_l_ref.dtype)
                p_c_ref[rows, :] = p_c.astype(p_c_ref.dtype)
                ds_c_ref[rows, :] = (p_c * (dp_c - delta)).astype(ds_c_ref.dtype)
                dsink_parts[t] = dsink_parts[t] - jnp.sum(p_s * delta)

            _softmax_chunks(s_c_ref, s_l_ref, bias_ref, sink_ref, kv_head, qb, emit)
            dq2 = _apply(ds_c_ref, kc)
            dkc_ref[:, lanes] += _kv_grad(ds_c_ref, q2, odd)
            dvc_ref[:, lanes] += _kv_grad(p_c_ref, do2, odd)
            if has_local:
                dq2 = dq2 + _apply(ds_l_ref, kl)
                dkl = _kv_grad(ds_l_ref, q2, odd)
                dvl = _kv_grad(p_l_ref, do2, odd)
                dk_ref, dv_ref = outs[1], outs[2]
                for t in range(3):
                    def add(t=t, dkl=dkl, dvl=dvl, lanes=lanes):
                        start = pl.multiple_of((j - 1 + t) * qb, qb)
                        dk_ref[pl.ds(start, qb), lanes] += dkl[t * qb:(t + 1) * qb]
                        dv_ref[pl.ds(start, qb), lanes] += dvl[t * qb:(t + 1) * qb]
                    if t == 0:
                        pl.when(j > 0)(add)
                    elif t == 2:
                        pl.when(j < nb - 1)(add)
                    else:
                        add()
            dq_ref[:, sub * GROUP_W:(sub + 1) * GROUP_W] = _unstack_halves(dq2, qb)
            row8 = lax.broadcasted_iota(jnp.int32, (8, LANES), 0)
            tile = jnp.zeros((8, LANES), F32)
            for t, gi in enumerate(GROUP_ORDER):
                tile = jnp.where(row8 == gi, dsink_parts[t], tile)
            dsink_ref[kv_head * 8:(kv_head + 1) * 8, :] += tile

    kv_w = N_KV_HEADS * HEAD_DIM
    seq_spec = pl.BlockSpec((seq, kv_w), lambda b, p, j: (b, 0))
    ctx_spec = pl.BlockSpec((ctx_len, kv_w), lambda b, p, j: (b, 0))
    do_spec = pl.BlockSpec((qb, N_HEADS * HEAD_DIM), lambda b, p, j: (qrow(b, j), 0))
    operands = [qkv] + ([qkv] * 6 + [_band_bias(qb, seq)] if has_local else []) + [qkv, qkv, sink, do]
    out_specs = ([pl.BlockSpec((qb, N_HEADS * HEAD_DIM), lambda b, p, j: (b * nb + j, 0))]
                 + ([seq_spec, seq_spec] if has_local else [])
                 + [ctx_spec, ctx_spec, pl.BlockSpec((32, LANES), lambda b, p, j: (0, 0))])
    out_shape = ([jax.ShapeDtypeStruct((n_rows, N_HEADS * HEAD_DIM), F32)]
                 + ([jax.ShapeDtypeStruct((B * seq, kv_w), F32)] * 2 if has_local else [])
                 + [jax.ShapeDtypeStruct((B * ctx_len, kv_w), F32)] * 2 + [jax.ShapeDtypeStruct((32, LANES), F32)])
    return pl.pallas_call(
        body, name=name, grid=(B, 1, nb),
        in_specs=[q_spec] + local + c_specs + [_SMEM, do_spec],
        out_specs=tuple(out_specs), out_shape=tuple(out_shape), scratch_shapes=per_head * N_KV_HEADS,
        compiler_params=_params())(*operands)


def _ret_decays(lg, rev):
    n = lax.broadcasted_iota(jnp.int32, (RET_CHUNK, RET_CHUNK), 0).astype(F32)
    m = lax.broadcasted_iota(jnp.int32, (RET_CHUNK, RET_CHUNK), 1).astype(F32)
    pos = lax.broadcasted_iota(jnp.int32, (RET_CHUNK, 1), 0).astype(F32)
    diff = (m - n) if rev else (n - m)
    a_exp = jnp.maximum(diff, 0.0)
    intra = jnp.where(diff >= 0, jnp.exp(lg * a_exp), 0.0)
    q_exp = (RET_CHUNK - pos) if rev else (pos + 1.0)
    k_exp = pos if rev else (RET_CHUNK - 1.0 - pos)
    chunk = jnp.exp(jnp.full((1, 1), RET_CHUNK, F32) * lg)
    return intra, a_exp, jnp.exp(lg * q_exp), q_exp, jnp.exp(lg * k_exp), k_exp, chunk


def _ctx_decay(lg, ctx_len, rev):
    t = lax.broadcasted_iota(jnp.int32, (ctx_len, 1), 0).astype(F32)
    expo = t if rev else (ctx_len - 1.0 - t)
    return jnp.exp(lg * expo), expo


def _ret_specs(B, seq, ctx_len, order):
    nc = seq // RET_CHUNK
    x_blocks = B * seq // ctx_len

    def rows(b, c):
        return b * nc + order(c, nc)

    q_spec = pl.BlockSpec((RET_CHUNK, RET_QK_DIM), lambda b, h, c: (rows(b, c), h))
    k_spec = pl.BlockSpec((RET_CHUNK, RET_QK_DIM), lambda b, h, c: (rows(b, c), RET_HEADS + h))
    v_spec = pl.BlockSpec((RET_CHUNK, RET_V_DIM), lambda b, h, c: (rows(b, c), RET_HEADS + h))
    kc_spec = pl.BlockSpec((ctx_len, RET_QK_DIM), lambda b, h, c: (x_blocks + b, RET_HEADS + h))
    vc_spec = pl.BlockSpec((ctx_len, RET_V_DIM), lambda b, h, c: (x_blocks + b, RET_HEADS + h))
    st_spec = pl.BlockSpec((1, 1, 1, RET_QK_DIM, RET_V_DIM), lambda b, h, c: (b, h, order(c, nc), 0, 0))
    o_spec = pl.BlockSpec((RET_CHUNK, RET_V_DIM), lambda b, h, c: (rows(b, c), h))
    return nc, q_spec, k_spec, v_spec, kc_spec, vc_spec, st_spec, o_spec


_SCAN_UP = lambda c, nc: c
_SCAN_DOWN = lambda c, nc: nc - 1 - c


def _ret_fwd(qk, qkvg, log_g, B, seq, ctx_len, name):
    nc, qf, kf, vf, kc_spec, vc_spec, stf, of = _ret_specs(B, seq, ctx_len, _SCAN_UP)
    _, qr, kr, vr, _, _, str_, or_ = _ret_specs(B, seq, ctx_len, _SCAN_DOWN)

    def body(lg_ref, qf_ref, kf_ref, vf_ref, qr_ref, kr_ref, vr_ref, kc_ref, vc_ref,
             of_ref, stf_ref, or_ref, str_ref, state_f, state_r):
        h, c = pl.program_id(1), pl.program_id(2)
        dirs = ((False, lg_ref[0, h], qf_ref, kf_ref, vf_ref, of_ref, stf_ref, state_f),
                (True, lg_ref[1, h], qr_ref, kr_ref, vr_ref, or_ref, str_ref, state_r))

        @pl.when(c == 0)
        def _():
            for rev, lg, _, _, _, _, _, state in dirs:
                dec, _ = _ctx_decay(lg, ctx_len, rev)
                state[...] = _dot(kc_ref[...] * dec, vc_ref[...], _TN)

        for rev, lg, q_ref, k_ref, v_ref, o_ref, st_ref, state in dirs:
            intra, _, q_dec, _, k_dec, _, chunk_dec = _ret_decays(lg, rev)
            qv, kv, vv = q_ref[...], k_ref[...], v_ref[...]
            s_in = state[...]
            st_ref[0, 0, 0] = s_in
            w = _dot(qv, kv, _NT) * intra
            o_ref[...] = _dot(w, vv, _NN) + _dot(qv, s_in, _NN) * q_dec
            state[...] = s_in * chunk_dec + _dot(kv * k_dec, vv, _TN)

    o_shape = jax.ShapeDtypeStruct((B * seq, RET_VWIDTH), F32)
    st_shape = jax.ShapeDtypeStruct((B, RET_HEADS, nc, RET_QK_DIM, RET_V_DIM), F32)
    return pl.pallas_call(
        body, name=name, grid=(B, RET_HEADS, nc),
        in_specs=[_SMEM, qf, kf, vf, qr, kr, vr, kc_spec, vc_spec],
        out_specs=(of, stf, or_, str_), out_shape=(o_shape, st_shape, o_shape, st_shape),
        scratch_shapes=[pltpu.VMEM((RET_QK_DIM, RET_V_DIM), F32)] * 2,
        compiler_params=_params())(log_g, qk, qk, qkvg, qk, qk, qkvg, qk, qkvg)


def _ret_bwd_chunk(rev, lg, q_ref, k_ref, v_ref, st_ref, do_ref, dq_ref, dk_ref, dv_ref, dlg_ref, dstate):
    intra, a_exp, q_dec, q_exp, k_dec, k_exp, chunk_dec = _ret_decays(lg, rev)
    qv, kv, vv, dov = q_ref[...], k_ref[...], v_ref[...], do_ref[...]
    s_in, ds_out = st_ref[0, 0, 0], dstate[...]
    p = _dot(qv, kv, _NT)
    w = p * intra
    dw = _dot(dov, vv, _NT)
    dp = dw * intra
    do_dec = dov * q_dec
    kd = kv * k_dec
    v_ds = _dot(vv, ds_out, _NT)
    dq_ref[...] = _dot(dp, kv, _NN) + _dot(do_dec, s_in, _NT)
    dk_ref[...] = _dot(dp, qv, _TN) + v_ds * k_dec
    dv_ref[...] = _dot(w, dov, _TN) + _dot(kd, ds_out, _NN)
    q_s = _dot(qv, s_in, _NN)
    dlg = (jnp.sum(dw * w * a_exp)
           + jnp.sum(q_exp * q_dec * jnp.sum(dov * q_s, axis=-1, keepdims=True))
           + jnp.sum(k_exp * k_dec * jnp.sum(kv * v_ds, axis=-1, keepdims=True))
           + RET_CHUNK * jnp.sum(chunk_dec * (ds_out * s_in)))
    ds_in = ds_out * chunk_dec + _dot(qv, do_dec, _TN)
    dstate[...] = ds_in
    dlg_ref[...] += dlg
    return ds_in


def _ret_bwd(qk, qkvg, log_g, st_f, st_r, do, B, seq, ctx_len, name):
    nc, qf, kf, vf, kc_spec, vc_spec, stf, of = _ret_specs(B, seq, ctx_len, _SCAN_DOWN)
    _, qr, kr, vr, _, _, str_, or_ = _ret_specs(B, seq, ctx_len, _SCAN_UP)

    def body(lg_ref, qf_ref, kf_ref, vf_ref, stf_ref, dof_ref, qr_ref, kr_ref, vr_ref, str_ref, dor_ref, kc_ref, vc_ref,
             dqf, dkf, dvf, dkcf, dvcf, dlgf, dqr, dkr, dvr, dkcr, dvcr, dlgr, dstate_f, dstate_r):
        h, c = pl.program_id(1), pl.program_id(2)
        dirs = ((False, lg_ref[0, h], (qf_ref, kf_ref, vf_ref, stf_ref, dof_ref, dqf, dkf, dvf, dlgf, dstate_f), dkcf, dvcf),
                (True, lg_ref[1, h], (qr_ref, kr_ref, vr_ref, str_ref, dor_ref, dqr, dkr, dvr, dlgr, dstate_r), dkcr, dvcr))

        @pl.when(c == 0)
        def _():
            for _, _, refs, _, _ in dirs:
                refs[-1][...] = jnp.zeros_like(refs[-1])
                refs[-2][...] = jnp.zeros_like(refs[-2])

        ds_first = [_ret_bwd_chunk(rev, lg, *refs) for rev, lg, refs, _, _ in dirs]

        @pl.when(c == nc - 1)
        def _():
            for (rev, lg, refs, dkc_ref, dvc_ref), ds_in in zip(dirs, ds_first):
                dec, expo = _ctx_decay(lg, ctx_len, rev)
                kcv, vcv = kc_ref[...], vc_ref[...]
                vc_ds = _dot(vcv, ds_in, _NT)
                dkc_ref[...] = vc_ds * dec
                dvc_ref[...] = _dot(kcv * dec, ds_in, _NN)
                refs[-2][...] += jnp.sum(expo * dec * jnp.sum(kcv * vc_ds, axis=-1, keepdims=True))

    def outs(q_spec, o_spec):
        return (pl.BlockSpec((RET_CHUNK, RET_QK_DIM), q_spec.index_map),
                pl.BlockSpec((RET_CHUNK, RET_QK_DIM), q_spec.index_map), o_spec,
                pl.BlockSpec((ctx_len, RET_QK_DIM), lambda b, h, c: (b, h)),
                pl.BlockSpec((ctx_len, RET_V_DIM), lambda b, h, c: (b, h)),
                pl.BlockSpec((1, 1, 8, LANES), lambda b, h, c: (b, h, 0, 0)))

    shapes = (jax.ShapeDtypeStruct((B * seq, RET_HEADS * RET_QK_DIM), F32),
              jax.ShapeDtypeStruct((B * seq, RET_HEADS * RET_QK_DIM), F32),
              jax.ShapeDtypeStruct((B * seq, RET_VWIDTH), F32),
              jax.ShapeDtypeStruct((B * ctx_len, RET_HEADS * RET_QK_DIM), F32),
              jax.ShapeDtypeStruct((B * ctx_len, RET_VWIDTH), F32),
              jax.ShapeDtypeStruct((B, RET_HEADS, 8, LANES), F32))
    res = pl.pallas_call(
        body, name=name, grid=(B, RET_HEADS, nc),
        in_specs=[_SMEM, qf, kf, vf, stf, of, qr, kr, vr, str_, or_, kc_spec, vc_spec],
        out_specs=outs(qf, of) + outs(qr, or_), out_shape=shapes + shapes,
        scratch_shapes=[pltpu.VMEM((RET_QK_DIM, RET_V_DIM), F32)] * 2,
        compiler_params=_params())(log_g, qk, qk, qkvg, st_f, do, qk, qk, qkvg, st_r, do, qk, qkvg)
    return res[:6], res[6:]


def _gated_out_fwd(o_f, o_b, qkvg, gn_gain, name):
    T = o_f.shape[0]
    g_off = (2 * RET_HEADS * RET_QK_DIM + RET_VWIDTH) // RET_V_DIM

    def body(of_ref, ob_ref, g_ref, gain_ref, z_ref):
        o = of_ref[...] + ob_ref[...]
        mu = jnp.mean(o, axis=-1, keepdims=True)
        var = jnp.mean(jnp.square(o - mu), axis=-1, keepdims=True)
        y = (o - mu) * lax.rsqrt(var + EPS) * gain_ref[...]
        gv = g_ref[...]
        z_ref[...] = (gv * jax.nn.sigmoid(gv) * y).astype(z_ref.dtype)

    blk = pl.BlockSpec((ROW_TILE, RET_V_DIM), lambda i, h: (i, h))
    return pl.pallas_call(
        body, name=name, grid=(T // ROW_TILE, RET_HEADS),
        in_specs=[blk, blk, pl.BlockSpec((ROW_TILE, RET_V_DIM), lambda i, h: (i, g_off + h)),
                  pl.BlockSpec((1, RET_V_DIM), lambda i, h: (0, h))],
        out_specs=blk, out_shape=jax.ShapeDtypeStruct((T, RET_VWIDTH), MXU_DTYPE),
        compiler_params=_params())(o_f, o_b, qkvg, gn_gain)


def _gated_out_bwd(dz, o_f, o_b, qkvg, gn_gain, name):
    T = o_f.shape[0]
    g_off = (2 * RET_HEADS * RET_QK_DIM + RET_VWIDTH) // RET_V_DIM

    def body(dz_ref, of_ref, ob_ref, g_ref, gain_ref, do_ref, dg_ref, dgain_ref):
        o = of_ref[...] + ob_ref[...]
        mu = jnp.mean(o, axis=-1, keepdims=True)
        var = jnp.mean(jnp.square(o - mu), axis=-1, keepdims=True)
        rstd = lax.rsqrt(var + EPS)
        yhat = (o - mu) * rstd
        gv, dzv = g_ref[...], dz_ref[...]
        sg = jax.nn.sigmoid(gv)
        dg_ref[...] = (dzv * (yhat * gain_ref[...]) * (sg * (1.0 + gv * (1.0 - sg)))).astype(dg_ref.dtype)
        dy = dzv * (gv * sg)

        @pl.when(pl.program_id(1) == 0)
        def _():
            dgain_ref[...] = jnp.zeros_like(dgain_ref)

        dgain_ref[...] += jnp.sum(dy * yhat, axis=0, keepdims=True)
        dyh = dy * gain_ref[...]
        do_ref[...] = rstd * (dyh - jnp.mean(dyh, axis=-1, keepdims=True)
                              - yhat * jnp.mean(dyh * yhat, axis=-1, keepdims=True))

    blk = pl.BlockSpec((ROW_TILE, RET_V_DIM), lambda h, i: (i, h))
    vec = pl.BlockSpec((1, RET_V_DIM), lambda h, i: (0, h))
    return pl.pallas_call(
        body, name=name, grid=(RET_HEADS, T // ROW_TILE),
        in_specs=[blk, blk, blk, pl.BlockSpec((ROW_TILE, RET_V_DIM), lambda h, i: (i, g_off + h)), vec],
        out_specs=(blk, blk, vec),
        out_shape=(jax.ShapeDtypeStruct((T, RET_VWIDTH), F32), jax.ShapeDtypeStruct((T, RET_VWIDTH), MXU_DTYPE),
                   jax.ShapeDtypeStruct((1, RET_VWIDTH), F32)),
        compiler_params=_params())(dz, o_f, o_b, qkvg, gn_gain)


def _adamw(w, m, v, parts, name):
    R, C = w.shape
    tr = _tile(R, (256, 128, 64, 32, 16, 8))
    n_parts = [p.shape[0] for p in parts]

    def body(*refs):
        w_ref, m_ref, v_ref = refs[:3]
        part_refs = refs[3:3 + len(parts)]
        g_ref, d_ref, nm_ref, nv_ref = refs[3 + len(parts):]
        g = None
        for ref, n in zip(part_refs, n_parts):
            for r in range(n):
                term = ref[r].astype(F32)
                g = term if g is None else g + term
        mn = ADAM_B1 * m_ref[...] + (1.0 - ADAM_B1) * g
        vn = ADAM_B2 * v_ref[...] + (1.0 - ADAM_B2) * jnp.square(g)
        m_hat = mn / (1.0 - ADAM_B1 ** ADAM_STEP)
        v_hat = vn / (1.0 - ADAM_B2 ** ADAM_STEP)
        g_ref[...] = g
        d_ref[...] = -ADAM_LR * (m_hat / (jnp.sqrt(v_hat) + ADAM_EPS) + ADAM_WD * w_ref[...])
        nm_ref[...] = mn
        nv_ref[...] = vn

    blk = pl.BlockSpec((tr, C), lambda i: (i, 0))
    part_specs = [pl.BlockSpec((n, tr, C), lambda i: (0, i, 0)) for n in n_parts]
    shp = jax.ShapeDtypeStruct((R, C), F32)
    return pl.pallas_call(
        body, name=name, grid=(R // tr,), in_specs=[blk, blk, blk] + part_specs,
        out_specs=(blk, blk, blk, blk), out_shape=(shp, shp, shp, shp),
        compiler_params=_params())(w, m, v, *parts)


def _sum_rows(parts, name):
    n, R, C = parts.shape
    tr = _tile(R, (256, 128, 64, 32, 16, 8))

    def body(p_ref, o_ref):
        acc = p_ref[0]
        for r in range(1, n):
            acc = acc + p_ref[r]
        o_ref[...] = acc

    return pl.pallas_call(
        body, name=name, grid=(R // tr,), in_specs=[pl.BlockSpec((n, tr, C), lambda i: (0, i, 0))],
        out_specs=pl.BlockSpec((tr, C), lambda i: (i, 0)), out_shape=jax.ShapeDtypeStruct((R, C), F32),
        compiler_params=_params())(parts)


def _my_coords():
    return lax.axis_index("x"), lax.axis_index("y"), lax.axis_index("c")


def _flip(coord, bit):
    return 1 - coord if bit else coord


def _all_gather(x2d, name):
    R, C = x2d.shape

    def body(x_ref, out_ref, send_sems, recv_sems, local_sem):
        x, y, c = _my_coords()
        me, sibling = (x, y, c), (x, y, 1 - c)
        chips = [(1 - x, y), (x, 1 - y), (1 - x, 1 - y)]

        def rows(px, py, pc):
            return out_ref.at[4 * px + 2 * py + pc]

        def copy(k, block, to, src=None):
            return pltpu.make_async_remote_copy(
                src_ref=rows(*block) if src is None else src, dst_ref=rows(*block),
                send_sem=send_sems.at[k], recv_sem=recv_sems.at[k], device_id=to, device_id_type=MESH)

        mine = pltpu.make_async_copy(x_ref, rows(*me), local_sem)
        mine.start()
        first = [copy(0, me, sibling, src=x_ref)]
        first += [copy(1 + j, me, (*chip, c), src=x_ref) for j, chip in enumerate(chips)]
        for cp in first:
            cp.start()
        passed = [copy(4 + j, (*chip, c), sibling) for j, chip in enumerate(chips)]
        for j, chip in enumerate(chips):
            copy(1 + j, (*chip, c), me).wait_recv()
            passed[j].start()
        copy(0, sibling, me).wait_recv()
        for j, chip in enumerate(chips):
            copy(4 + j, (*chip, 1 - c), me).wait_recv()
        for cp in first + passed:
            cp.wait_send()
        mine.wait()

    return pl.pallas_call(
        body, name=name, out_shape=jax.ShapeDtypeStruct((N_DEV, R, C), x2d.dtype),
        in_specs=[_ANY], out_specs=_ANY,
        scratch_shapes=[pltpu.SemaphoreType.DMA((7,)), pltpu.SemaphoreType.DMA((7,)), pltpu.SemaphoreType.DMA],
    )(x2d)


BIG_WEIGHTS = {
    "ffn_w_in": (2, (2, D_MODEL, 2 * D_FF)),
    "ffn_w_out": (1, (2, D_FF, D_MODEL)),
    "attn_w_qkv": (2, (1, D_MODEL, (N_HEADS + 2 * N_KV_HEADS) * HEAD_DIM)),
    "attn_w_o": (1, (1, N_HEADS * HEAD_DIM, D_MODEL)),
    "ret_w_qkvg": (2, (1, D_MODEL, 2 * D_MODEL + 2 * RET_VWIDTH)),
    "ret_gn_g": (2, (1, 1, RET_VWIDTH)),
    "ret_w_o": (1, (1, RET_VWIDTH, D_MODEL)),
}


def _join_shards(name, stacked):
    axis, full = BIG_WEIGHTS[name]
    if axis == 2:
        stacked = stacked.transpose(0, 2, 1, 3)
    return stacked.reshape(full)


def _split_shards(name, full_arr):
    axis, (_, rows, cols) = BIG_WEIGHTS[name]
    L = full_arr.shape[0]
    if axis == 2:
        return full_arr.reshape(L, rows, N_DEV, cols // N_DEV).transpose(0, 2, 1, 3)
    return full_arr.reshape(L, N_DEV, rows // N_DEV, cols)


def _gather_shards(shards, name):
    n = len(shards)

    def body(*refs):
        x_refs, out_refs = refs[:n], refs[n:2 * n]
        send_sems, recv_sems, local_sems = refs[2 * n:]
        x, y, c = _my_coords()
        me, sibling = (x, y, c), (x, y, 1 - c)
        chips = [(1 - x, y), (x, 1 - y), (1 - x, 1 - y)]

        def rows(a, px, py, pc):
            return out_refs[a].at[:, 4 * px + 2 * py + pc]

        def copy(a, k, block, to, src=None):
            return pltpu.make_async_remote_copy(
                src_ref=rows(a, *block) if src is None else src, dst_ref=rows(a, *block),
                send_sem=send_sems.at[7 * a + k], recv_sem=recv_sems.at[7 * a + k], device_id=to, device_id_type=MESH)

        mine = [pltpu.make_async_copy(x_refs[a], rows(a, *me), local_sems.at[a]) for a in range(n)]
        for cp in mine:
            cp.start()
        first = []
        for a in range(n):
            first.append(copy(a, 0, me, sibling, src=x_refs[a]))
            first += [copy(a, 1 + j, me, (*chip, c), src=x_refs[a]) for j, chip in enumerate(chips)]
        for cp in first:
            cp.start()
        passed = []
        for j, chip in enumerate(chips):
            for a in range(n):
                copy(a, 1 + j, (*chip, c), me).wait_recv()
                fwd = copy(a, 4 + j, (*chip, c), sibling)
                fwd.start()
                passed.append(fwd)
        for a in range(n):
            copy(a, 0, sibling, me).wait_recv()
            for j, chip in enumerate(chips):
                copy(a, 4 + j, (*chip, 1 - c), me).wait_recv()
        for cp in first + passed:
            cp.wait_send()
        for cp in mine:
            cp.wait()

    return pl.pallas_call(
        body, name=name,
        out_shape=[jax.ShapeDtypeStruct((s.shape[0], N_DEV) + s.shape[1:], s.dtype) for s in shards],
        in_specs=[_ANY] * n, out_specs=[_ANY] * n,
        scratch_shapes=[pltpu.SemaphoreType.DMA((7 * n,)), pltpu.SemaphoreType.DMA((7 * n,)),
                        pltpu.SemaphoreType.DMA((n,))],
    )(*shards)


def _exchange_shards(arrs, masks, src_of, out_tail, name):
    n, nm = len(arrs), len(masks)

    def body(*refs):
        in_refs, out_refs = refs[:n], refs[n:2 * n]
        send_sems, recv_sems = refs[2 * n:]
        x, y, c = _my_coords()
        copies = []
        for a in range(n):
            for k, (bx, by, bc) in enumerate(masks):
                peer = (_flip(x, bx), _flip(y, by), _flip(c, bc))
                copies.append(pltpu.make_async_remote_copy(
                    src_ref=src_of(in_refs[a], peer, (x, y, c)), dst_ref=out_refs[a].at[k],
                    send_sem=send_sems.at[nm * a + k], recv_sem=recv_sems.at[nm * a + k],
                    device_id=peer, device_id_type=MESH))
        for cp in copies:
            cp.start()
        for cp in copies:
            cp.wait()

    return pl.pallas_call(
        body, name=name,
        out_shape=[jax.ShapeDtypeStruct((nm,) + out_tail(s), s.dtype) for s in arrs],
        in_specs=[_ANY] * n, out_specs=[_ANY] * n,
        scratch_shapes=[pltpu.SemaphoreType.DMA((nm * n,)), pltpu.SemaphoreType.DMA((nm * n,))],
    )(*arrs)


def _pair_sum(g, from_sibling, core, out_dtype, name):
    L, _, _, a, b = g.shape
    ta = a

    def body(core_ref, g_ref, s_ref, o_ref):
        o_ref[...] = (g_ref[...] + s_ref[...]).astype(out_dtype)

    blk = pl.BlockSpec((1, 1, ta, b), lambda l, q, i, core_ref: (l, q, i, 0))
    return pl.pallas_call(
        body, name=name,
        grid_spec=pltpu.PrefetchScalarGridSpec(
            num_scalar_prefetch=1, grid=(L, 4, a // ta),
            in_specs=[pl.BlockSpec((1, 1, pl.Squeezed(), ta, b), lambda l, q, i, core_ref: (l, q, core_ref[0], i, 0)), blk],
            out_specs=blk),
        out_shape=jax.ShapeDtypeStruct((L, 4, a, b), out_dtype), compiler_params=_params())(core, g, from_sibling)


def _mods(mod_x, mod_c, layer):
    both = jnp.concatenate([mod_x[:, layer], mod_c[layer][None]], axis=0)
    return [both[:, None, k * D_MODEL:(k + 1) * D_MODEL] for k in range(6)]


def _local_step(x, ctx, target, mod_x, mod_c, w, small, late_weights=None, hooks=None):
    B, S, _ = x.shape
    L = ctx.shape[1]
    NX, NC = B * S, B * L
    T = NX + NC
    tiles_per_ex = S // ROW_TILE
    nxt = NX // ROW_TILE
    gidx = _group_index(nxt, tiles_per_ex, B)
    gidx_for = lambda rows: _group_index(NX // rows, S // rows, B)
    mm_rows = _tile(S, (MM_ROWS, ROW_TILE))
    tidx = lambda i: jnp.where(i < nxt, i % tiles_per_ex, tiles_per_ex)
    G = B + 1
    x0 = jnp.concatenate([x.reshape(NX, D_MODEL), ctx.reshape(NC, D_MODEL)], axis=0)
    acos, asin = [jnp.tile(t, (1, LANES // HEAD_DIM)) for t in _rope_tables(S, HEAD_DIM)]
    rcos, rsin = _rope_tables(S, RET_QK_DIM)
    sink = small["attn_sink"].reshape(N_KV_HEADS, GQA_GROUP)
    gains = jnp.stack([jnp.tile(small["attn_q_norm"].reshape(1, HEAD_DIM), (1, LANES // HEAD_DIM)),
                       jnp.tile(small["attn_k_norm"].reshape(1, HEAD_DIM), (1, LANES // HEAD_DIM))])
    log_g = jax.nn.log_sigmoid(small["ret_decay_logit"].reshape(2, RET_HEADS))
    n1, n2 = small["norm1_g"], small["norm2_g"]

    m0 = _mods(mod_x, mod_c, 0)
    h1 = _norm_mod_fwd(x0, n1[0:1], m0[0], m0[1], gidx, "l0_norm1")
    qkv = _mm(h1, w["attn_w_qkv"][0], "nn", F32, "l0_qkv")
    qkv_r = _attn_prep_fwd(qkv, gains, acos, asin, tidx, "l0_qk_prep")
    o_x = _attn_fwd(qkv_r, sink, B, S, L, False, "l0_attn_x")
    o_c = _attn_fwd(qkv_r, sink, B, S, L, True, "l0_attn_c")
    o0 = jnp.concatenate([o_x, o_c], axis=0)
    mo0, x1 = _mm(o0, w["attn_w_o"][0], "nn", F32, "l0_attn_out", res=x0, gate=m0[2], gidx_for=gidx_for, gate_rows=mm_rows)
    h2 = _norm_mod_fwd(x1, n2[0:1], m0[3], m0[4], gidx, "l0_norm2")
    if late_weights is not None:
        w = {**w, **late_weights(x1)}
    ug0, uu0, a0 = _ffn_in_swiglu(h2, w["ffn_w_in"][0], "l0_ffn_in")
    f0, x2 = _mm(a0, w["ffn_w_out"][0], "nn", F32, "l0_ffn_out", res=x1, gate=m0[5], gidx_for=gidx_for, gate_rows=mm_rows)

    m1 = _mods(mod_x, mod_c, 1)
    g1 = _norm_mod_fwd(x2, n1[1:2], m1[0], m1[1], gidx, "l1_norm1")
    qkvg = _mm(g1, w["ret_w_qkvg"][0], "nn", F32, "l1_qkvg")
    qk = _ret_rope(qkvg, rcos, rsin, tidx, "l1_rope")
    of, st_f, ob, st_b = _ret_fwd(qk, qkvg, log_g, B, S, L, "l1_ret")
    gn = w["ret_gn_g"].reshape(1, RET_VWIDTH)
    z1 = _gated_out_fwd(of, ob, qkvg, gn, "l1_gated_out")
    gx = lambda i: i // tiles_per_ex
    m1x = [t[:B] for t in m1]
    mo1, y1 = _mm(z1, w["ret_w_o"][0], "nn", F32, "l1_ret_out", res=x2, gate=m1x[2], gidx_for=gidx_for, gate_rows=mm_rows)
    k2 = _norm_mod_fwd(y1, n2[1:2], m1x[3], m1x[4], gx, "l1_norm2")
    ug1, uu1, a1 = _ffn_in_swiglu(k2, w["ffn_w_in"][1], "l1_ffn_in")
    f1, y2 = _mm(a1, w["ffn_w_out"][1], "nn", F32, "l1_ffn_out", res=y1, gate=m1x[5], gidx_for=gidx_for, gate_rows=mm_rows)

    loss_tile, dy2, dz, dgate5_1 = _loss_fwd_bwd(y2, target.reshape(NX, D_MODEL), f1, m1x[5], gx, B, "loss")

    zg = jnp.zeros((1, 1, D_MODEL), F32)
    gw_ffn_out1 = _mm(a1, dz, "tn", F32, "l1_ffn_out_dw")
    du = _ffn_out_bwd_swiglu(dz, w["ffn_w_out"][1], ug1, uu1, "l1_ffn_out_dx")
    gw_ffn_in1 = _mm(k2, du, "tn", F32, "l1_ffn_in_dw")
    dk2 = _mm(du, w["ffn_w_in"][1], "nt", F32, "l1_ffn_in_dx")
    dy1, dsh3_1, dsc4_1, dn2_1, dzo, dgate2_1 = _norm_mod_bwd(dk2, y1, n2[1:2], m1x[4], dy2, gx, B, "l1_norm2_bwd",
                                                              gated=(mo1, m1x[2]))
    gw_ret_o = _mm(z1, dzo, "tn", F32, "l1_ret_out_dw")
    dz1 = _mm(dzo, w["ret_w_o"][0], "nt", F32, "l1_ret_out_dx")
    do_r, dg_r, dgn = _gated_out_bwd(dz1, of, ob, qkvg, gn, "l1_gated_out_bwd")
    ((dq_f, dk_f, dv_f, dkc_f, dvc_f, dlg_f),
     (dq_b, dk_b, dv_b, dkc_b, dvc_b, dlg_b)) = _ret_bwd(qk, qkvg, log_g, st_f, st_b, do_r, B, S, L, "l1_ret_bwd")
    dqkvg = _ret_grad_assemble((dq_f, dq_b, dk_f, dk_b, dv_f, dv_b), (dkc_f, dkc_b, dvc_f, dvc_b), dg_r, rcos, rsin, S,
                               "l1_qkvg_grad")
    gw_ret_qkvg = _mm(g1, dqkvg, "tn", F32, "l1_qkvg_dw")
    grads_layer1 = {
        "ffn_w_in": gw_ffn_in1[None],
        "ffn_w_out": gw_ffn_out1[None],
        "ret_w_qkvg": gw_ret_qkvg[None],
        "ret_gn_g": dgn.reshape(1, 1, RET_VWIDTH),
        "ret_w_o": gw_ret_o[None],
    }
    if hooks is not None:
        m0[5] = hooks.layer1_grads(grads_layer1, m0[5])
    dg1 = _mm(dqkvg, w["ret_w_qkvg"][0], "nt", F32, "l1_qkvg_dx")
    dx2, dsh0_1, dsc1_1, dn1_1, dz, dgate5_0 = _norm_mod_bwd(dg1, x2, n1[1:2], m1[1], dy1, gidx, G, "l1_norm1_bwd",
                                                             gated=(f0, m0[5]))
    dlg = jnp.stack([jnp.sum(dlg_f[:, :, 0, 0], axis=0), jnp.sum(dlg_b[:, :, 0, 0], axis=0)])
    d_decay = (dlg * jax.nn.sigmoid(-small["ret_decay_logit"].reshape(2, RET_HEADS))).reshape(1, 2, RET_HEADS)

    gw_ffn_out0 = _mm(a0, dz, "tn", F32, "l0_ffn_out_dw")
    du = _ffn_out_bwd_swiglu(dz, w["ffn_w_out"][0], ug0, uu0, "l0_ffn_out_dx")
    if hooks is not None:
        m0[4] = hooks.mid_ffn0_backward(du, m0[4])
    gw_ffn_in0 = _mm(h2, du, "tn", F32, "l0_ffn_in_dw")
    if hooks is not None:
        m0[2] = hooks.ffn0_grads({"ffn_w_in": gw_ffn_in0[None], "ffn_w_out": gw_ffn_out0[None]}, m0[2])
    dh2 = _mm(du, w["ffn_w_in"][0], "nt", F32, "l0_ffn_in_dx")
    dx1, dsh3_0, dsc4_0, dn2_0, dzo, dgate2_0 = _norm_mod_bwd(dh2, x1, n2[0:1], m0[4], dx2, gidx, G, "l0_norm2_bwd",
                                                              gated=(mo0, m0[2]))
    gw_attn_o = _mm(o0, dzo, "tn", F32, "l0_attn_out_dw")
    do0 = _mm(dzo, w["attn_w_o"][0], "nt", MXU_DTYPE, "l0_attn_out_dx")
    dq_x, dk_x, dv_x, dkc1, dvc1, dsink_x = _attn_bwd(qkv_r, sink, do0, B, S, L, False, "l0_attn_x_bwd")
    dq_c, dkc2, dvc2, dsink_c = _attn_bwd(qkv_r, sink, do0, B, S, L, True, "l0_attn_c_bwd")
    if hooks is not None:
        gains = hooks.after_attn_backward(dq_x, gains)
    dqkv, dgains = _attn_prep_bwd((dq_x, dk_x, dv_x), (dq_c, dkc1, dkc2, dvc1, dvc2), qkv, gains, acos, asin, tidx,
                                  "l0_qk_prep_bwd")
    gw_attn_qkv = _mm(h1, dqkv, "tn", F32, "l0_qkv_dw")
    dh1 = _mm(dqkv, w["attn_w_qkv"][0], "nt", F32, "l0_qkv_dx")
    dx0, dsh0_0, dsc1_0, dn1_0 = _norm_mod_bwd(dh1, x0, n1[0:1], m0[1], dx1, gidx, G, "l0_norm1_bwd", dx_rows=NX)

    dgains = jnp.sum(dgains.reshape(ATTN_QK_BLOCKS, LANES // HEAD_DIM, HEAD_DIM), axis=1)
    dsink = (dsink_x + dsink_c).reshape(N_KV_HEADS, 8, LANES)[:, :GQA_GROUP, 0].reshape(1, N_HEADS)
    grads_layer0 = {
        "ffn_w_in": gw_ffn_in0[None],
        "ffn_w_out": gw_ffn_out0[None],
        "attn_w_qkv": gw_attn_qkv[None],
        "attn_w_o": gw_attn_o[None],
    }
    grads_small = {
        "norm1_g": jnp.concatenate([dn1_0, dn1_1], axis=0),
        "norm2_g": jnp.concatenate([dn2_0, dn2_1], axis=0),
        "attn_q_norm": jnp.sum(dgains[:ATTN_Q_BLOCKS], axis=0)[None],
        "attn_k_norm": jnp.sum(dgains[ATTN_Q_BLOCKS:ATTN_QK_BLOCKS], axis=0)[None],
        "attn_sink": dsink,
        "ret_decay_logit": d_decay,
    }

    def pad_g(t):
        return jnp.concatenate([t, zg], axis=0)

    d0 = jnp.concatenate([dsh0_0, dsc1_0, dgate2_0, dsh3_0, dsc4_0, dgate5_0], axis=2)[:, 0]
    d1 = jnp.concatenate([dsh0_1, dsc1_1, pad_g(dgate2_1), pad_g(dsh3_1), pad_g(dsc4_1), pad_g(dgate5_1)],
                         axis=2)[:, 0]
    dmod_x = jnp.stack([d0[:B], d1[:B]], axis=1)
    dmod_c = jnp.stack([d0[B], d1[B]], axis=0)
    return loss_tile, dx0.reshape(B, S, D_MODEL), (grads_layer0, grads_layer1), grads_small, dmod_x, dmod_c


SMALL_NAMES = ("c_ctx", "ada_b", "norm1_g", "norm2_g", "attn_q_norm", "attn_k_norm", "attn_sink", "ret_decay_logit")
ADA_ROWS = 64


def _pack_small(d, rows):
    flat = jnp.concatenate([d[k].reshape(-1) for k in SMALL_NAMES])
    n = rows * LANES
    return jnp.pad(flat, (0, n - flat.shape[0])).reshape(rows, LANES)


def _unpack_small(packed, shapes):
    flat = packed.reshape(-1)
    out, off = {}, 0
    for k in SMALL_NAMES:
        n = math.prod(shapes[k])
        out[k] = flat[off:off + n].reshape(shapes[k])
        off += n
    return out


EARLY_WEIGHTS = ("attn_w_qkv", "attn_w_o")
LATE_WEIGHTS = tuple(k for k in BIG_WEIGHTS if k not in EARLY_WEIGHTS)

_HBM = pl.BlockSpec(memory_space=pltpu.HBM)
_SEM = pl.BlockSpec(memory_space=pltpu.SEMAPHORE)
_DATAFLOW = pltpu.SideEffectType.DATAFLOW_SIDE_EFFECTING
_PEER_FLIPS = ((0, 0, 1), (0, 1, 0), (0, 1, 1), (1, 0, 0), (1, 0, 1), (1, 1, 0), (1, 1, 1))


def _wire_shard(name, t):
    return t.reshape(1, 1, -1) if name == "ret_gn_g" else t.astype(MXU_DTYPE)


def _direct_copies(x_refs, land_refs, send_sems, recv_sems, landing):
    x, y, c = _my_coords()
    out = []
    for a in range(len(x_refs)):
        for k, (bx, by, bc) in enumerate(_PEER_FLIPS):
            peer = (_flip(x, bx), _flip(y, by), _flip(c, bc))
            slot = (4 * peer[0] + 2 * peer[1] + peer[2]) if landing else (4 * x + 2 * y + c)
            out.append(pltpu.make_async_remote_copy(
                src_ref=x_refs[a], dst_ref=land_refs[a].at[:, slot], send_sem=send_sems.at[7 * a + k],
                recv_sem=recv_sems.at[7 * a + k], device_id=peer, device_id_type=MESH))
    return out


def _gather_start(shards, name):
    n = len(shards)
    lands = [lax.empty((s.shape[0], N_DEV) + s.shape[1:], s.dtype) for s in shards]

    def body(*refs):
        send_sems, recv_sems = refs[2 * n], refs[2 * n + 1]
        x_refs, land_refs = refs[2 * n + 2:3 * n + 2], refs[3 * n + 2:4 * n + 2]
        for cp in _direct_copies(x_refs, land_refs, send_sems, recv_sems, landing=False):
            cp.start()
        refs[-1][...] = jnp.zeros_like(refs[-1])

    hbm = lambda t: pltpu.with_memory_space_constraint(t, pltpu.HBM)
    res = pl.pallas_call(
        body, name=name,
        out_shape=(pltpu.SemaphoreType.DMA((7 * n,)), pltpu.SemaphoreType.DMA((7 * n,)))
        + tuple(pltpu.HBM(t.shape, t.dtype) for t in shards + lands) + (jax.ShapeDtypeStruct((8, LANES), F32),),
        in_specs=[_HBM] * (2 * n), out_specs=(_SEM, _SEM) + (_HBM,) * (2 * n) + (pl.BlockSpec(memory_space=pltpu.VMEM),),
        input_output_aliases={i: 2 + i for i in range(2 * n)},
        compiler_params=pltpu.CompilerParams(has_side_effects=_DATAFLOW))(*[hbm(t) for t in shards + lands])
    return res[0], res[1], list(res[2:2 + n]), list(res[2 + n:2 + 2 * n]), res[-1]


def _gather_wait(send_sems, recv_sems, shards, lands, after, name):
    n = len(shards)

    def body(*refs):
        x_refs, land_refs = refs[:n], refs[n:2 * n]
        for cp in _direct_copies(x_refs, land_refs, refs[2 * n], refs[2 * n + 1], landing=True):
            cp.wait_send()
            cp.wait_recv()

    res = pl.pallas_call(
        body, name=name, out_shape=tuple(pltpu.HBM(t.shape, t.dtype) for t in shards + lands),
        in_specs=[_HBM] * (2 * n) + [_SEM, _SEM, _ANY], out_specs=(_HBM,) * (2 * n),
        input_output_aliases={i: i for i in range(2 * n)},
        compiler_params=pltpu.CompilerParams(has_side_effects=_DATAFLOW))(*shards, *lands, send_sems, recv_sems, after)
    return list(res[n:])


def _gather_big_weights(weights, names, name):
    gathered = _gather_shards([_wire_shard(k, weights[k]) for k in names], name)
    return {k: _join_shards(k, g) for k, g in zip(names, gathered)}


_SIBLING = ((0, 0, 1),)
_CHIPS = ((1, 0, 0), (0, 1, 0), (1, 1, 0))
_to_sibling = lambda ref, peer: ref.at[:, :, peer[2]]
_to_chip = lambda ref, peer: ref.at[:, 2 * peer[0] + peer[1]]
_sibling_tail = lambda s: (s.shape[0], 4) + s.shape[3:]
_chip_tail = lambda s: (s.shape[0],) + s.shape[2:]


def _rs_split(grads):
    names = list(grads)
    split = []
    for k in names:
        s = _split_shards(k, grads[k])
        split.append(s.reshape(s.shape[0], 4, 2, s.shape[2], s.shape[3]))
    return names, split


def _rs_pair_sums(names, split, from_sibling, tag):
    core = lax.axis_index("c").astype(jnp.int32).reshape(1)
    return [_pair_sum(g, s, core, MXU_DTYPE, tag + k) for k, g, s in zip(names, split, from_sibling)]


def _rs_parts(names, split, from_sibling, from_chips):
    mx_, my_, mc_ = _my_coords()
    my_chip = 2 * mx_ + my_
    parts = {}
    for k, g, s, r in zip(names, split, from_sibling, from_chips):
        own_keep = lax.dynamic_index_in_dim(lax.dynamic_index_in_dim(g, my_chip, axis=1, keepdims=False), mc_, axis=1,
                                            keepdims=False)
        parts[k] = (own_keep, lax.dynamic_index_in_dim(s, my_chip, axis=1, keepdims=False), r)
    return parts


def _reduce_scatter_in_call(grads, tag):
    names, split = _rs_split(grads)
    from_sibling = [t[0] for t in _exchange_shards(split, _SIBLING, lambda ref, peer, me_: _to_sibling(ref, peer),
                                                   _sibling_tail, tag + "sibling")]
    pair = _rs_pair_sums(names, split, from_sibling, tag + "pair_")
    from_chips = _exchange_shards(pair, _CHIPS, lambda ref, peer, me_: _to_chip(ref, peer), _chip_tail, tag + "chips")
    return _rs_parts(names, split, from_sibling, from_chips)


def _exchange_copies(in_refs, land_refs, send_sems, recv_sems, masks, src_of):
    x, y, c = _my_coords()
    nm = len(masks)
    out = []
    for a in range(len(in_refs)):
        for k, (bx, by, bc) in enumerate(masks):
            peer = (_flip(x, bx), _flip(y, by), _flip(c, bc))
            out.append(pltpu.make_async_remote_copy(
                src_ref=src_of(in_refs[a], peer), dst_ref=land_refs[a].at[k], send_sem=send_sems.at[nm * a + k],
                recv_sem=recv_sems.at[nm * a + k], device_id=peer, device_id_type=MESH))
    return out


def _exchange_start(arrs, masks, src_of, out_tail, name):
    n, nm = len(arrs), len(masks)
    lands = [lax.empty((nm,) + out_tail(s), s.dtype) for s in arrs]

    def body(*refs):
        send_sems, recv_sems = refs[2 * n], refs[2 * n + 1]
        in_refs, land_refs = refs[2 * n + 2:3 * n + 2], refs[3 * n + 2:4 * n + 2]
        for cp in _exchange_copies(in_refs, land_refs, send_sems, recv_sems, masks, src_of):
            cp.start()
        refs[-1][...] = jnp.zeros_like(refs[-1])

    hbm = lambda t: pltpu.with_memory_space_constraint(t, pltpu.HBM)
    res = pl.pallas_call(
        body, name=name,
        out_shape=(pltpu.SemaphoreType.DMA((nm * n,)), pltpu.SemaphoreType.DMA((nm * n,)))
        + tuple(pltpu.HBM(t.shape, t.dtype) for t in list(arrs) + lands) + (jax.ShapeDtypeStruct((8, LANES), F32),),
        in_specs=[_HBM] * (2 * n), out_specs=(_SEM, _SEM) + (_HBM,) * (2 * n) + (pl.BlockSpec(memory_space=pltpu.VMEM),),
        input_output_aliases={i: 2 + i for i in range(2 * n)},
        compiler_params=pltpu.CompilerParams(has_side_effects=_DATAFLOW))(*[hbm(t) for t in list(arrs) + lands])
    return (res[0], res[1], list(res[2:2 + n]), list(res[2 + n:2 + 2 * n]), masks, src_of), res[-1]


def _exchange_wait(state, after, name):
    send_sems, recv_sems, arrs, lands, masks, src_of = state
    n = len(arrs)

    def body(*refs):
        for cp in _exchange_copies(refs[:n], refs[n:2 * n], refs[2 * n], refs[2 * n + 1], masks, src_of):
            cp.wait_send()
            cp.wait_recv()

    res = pl.pallas_call(
        body, name=name, out_shape=tuple(pltpu.HBM(t.shape, t.dtype) for t in arrs + lands),
        in_specs=[_HBM] * (2 * n) + [_SEM, _SEM, _ANY], out_specs=(_HBM,) * (2 * n),
        input_output_aliases={i: i for i in range(2 * n)},
        compiler_params=pltpu.CompilerParams(has_side_effects=_DATAFLOW))(*arrs, *lands, send_sems, recv_sems, after)
    return list(res[:n]), list(res[n:])


class _SplitReduce:
    def __init__(self, tag):
        self.tag = tag

    def start(self, grads, order_through):
        self.names, split = _rs_split(grads)
        self.sibling, tok = _exchange_start(split, _SIBLING, _to_sibling, _sibling_tail, self.tag + "sibling_start")
        return order_through + tok[0, 0]

    def middle(self, after, order_through):
        self.split, lands = _exchange_wait(self.sibling, after, self.tag + "sibling_wait")
        self.from_sibling = [t[0] for t in lands]
        pair = _rs_pair_sums(self.names, self.split, self.from_sibling, self.tag + "pair_")
        self.chips, tok = _exchange_start(pair, _CHIPS, _to_chip, _chip_tail, self.tag + "chips_start")
        return order_through + tok[0, 0]

    def finish(self, after):
        _, from_chips = _exchange_wait(self.chips, after, self.tag + "chips_wait")
        return _rs_parts(self.names, self.split, self.from_sibling, from_chips)


def _adamw_big(weights, mom1, mom2, part_groups):
    big = {}
    for k in BIG_WEIGHTS:
        parts = [g[k] for g in part_groups if k in g]
        own_keep = jnp.concatenate([p[0] for p in parts], axis=0)
        own_sib = jnp.concatenate([p[1] for p in parts], axis=0)
        recv = jnp.concatenate([p[2] for p in parts], axis=1)
        L_, a_, b_ = own_keep.shape
        rows = L_ * a_
        res = _adamw(weights[k].reshape(rows, b_), mom1[k].reshape(rows, b_), mom2[k].reshape(rows, b_),
                     [own_keep.reshape(1, rows, b_), own_sib.reshape(1, rows, b_), recv.reshape(3, rows, b_)],
                     "adamw_" + k)
        big[k] = [t.reshape(weights[k].shape) for t in res]
    return big


def kernel(x, c, ctx, c_ctx, ada_w, ada_b, norm1_g, norm2_g, ffn_w_in, ffn_w_out, attn_w_qkv, attn_q_norm, attn_k_norm, attn_sink, attn_w_o, ret_w_qkvg, ret_decay_logit, ret_gn_g, ret_w_o, loss_target, m_c_ctx, m_ada_w, m_ada_b, m_norm1_g, m_norm2_g, m_ffn_w_in, m_ffn_w_out, m_attn_w_qkv, m_attn_q_norm, m_attn_k_norm, m_attn_sink, m_attn_w_o, m_ret_w_qkvg, m_ret_decay_logit, m_ret_gn_g, m_ret_w_o, v_c_ctx, v_ada_w, v_ada_b, v_norm1_g, v_norm2_g, v_ffn_w_in, v_ffn_w_out, v_attn_w_qkv, v_attn_q_norm, v_attn_k_norm, v_attn_sink, v_attn_w_o, v_ret_w_qkvg, v_ret_decay_logit, v_ret_gn_g, v_ret_w_o):
    weights = dict(c_ctx=c_ctx, ada_w=ada_w, ada_b=ada_b, norm1_g=norm1_g, norm2_g=norm2_g, ffn_w_in=ffn_w_in,
                   ffn_w_out=ffn_w_out, attn_w_qkv=attn_w_qkv, attn_q_norm=attn_q_norm, attn_k_norm=attn_k_norm,
                   attn_sink=attn_sink, attn_w_o=attn_w_o, ret_w_qkvg=ret_w_qkvg, ret_decay_logit=ret_decay_logit,
                   ret_gn_g=ret_gn_g, ret_w_o=ret_w_o)
    mom1 = dict(c_ctx=m_c_ctx, ada_w=m_ada_w, ada_b=m_ada_b, norm1_g=m_norm1_g, norm2_g=m_norm2_g, ffn_w_in=m_ffn_w_in,
                ffn_w_out=m_ffn_w_out, attn_w_qkv=m_attn_w_qkv, attn_q_norm=m_attn_q_norm, attn_k_norm=m_attn_k_norm,
                attn_sink=m_attn_sink, attn_w_o=m_attn_w_o, ret_w_qkvg=m_ret_w_qkvg, ret_decay_logit=m_ret_decay_logit,
                ret_gn_g=m_ret_gn_g, ret_w_o=m_ret_w_o)
    mom2 = dict(c_ctx=v_c_ctx, ada_w=v_ada_w, ada_b=v_ada_b, norm1_g=v_norm1_g, norm2_g=v_norm2_g, ffn_w_in=v_ffn_w_in,
                ffn_w_out=v_ffn_w_out, attn_w_qkv=v_attn_w_qkv, attn_q_norm=v_attn_q_norm, attn_k_norm=v_attn_k_norm,
                attn_sink=v_attn_sink, attn_w_o=v_attn_w_o, ret_w_qkvg=v_ret_w_qkvg, ret_decay_logit=v_ret_decay_logit,
                ret_gn_g=v_ret_gn_g, ret_w_o=v_ret_w_o)
    B = x.shape[0]
    mx_, my_, mc_ = _my_coords()
    me = 4 * mx_ + 2 * my_ + mc_
    ada_cols = ada_w.shape[2]

    w_full = _gather_big_weights(weights, EARLY_WEIGHTS, "gather_early")

    c_all = _all_gather(jax.nn.silu(c), "gather_c").reshape(N_DEV * B, D_MODEL)
    cc_act = jax.nn.silu(c_ctx)[None]
    ada_in = jnp.concatenate([c_all, cc_act, jnp.zeros((ADA_ROWS - N_DEV * B - 1, D_MODEL), F32)], axis=0)
    ada_in = ada_in.astype(MXU_DTYPE)
    ada_w2 = jnp.concatenate([ada_w[0], ada_w[1]], axis=1)
    bias = lax.dynamic_slice_in_dim(ada_b.reshape(2, N_DEV, ada_cols), me, 1, axis=1).reshape(1, 2 * ada_cols)
    mod_cols = _mm(ada_in, ada_w2, "nn", F32, "ada_fwd", bias=bias)
    mod_all = _all_gather(mod_cols, "gather_mod")
    mod_all = mod_all.reshape(N_DEV, ADA_ROWS, 2, ada_cols).transpose(1, 2, 0, 3).reshape(ADA_ROWS, 2, N_DEV * ada_cols)
    mod_x = lax.dynamic_slice_in_dim(mod_all, me * B, B, axis=0)
    mod_c = mod_all[N_DEV * B]

    order = 0.0 * (mod_c[0, 0] + w_full["attn_w_o"][0, 0, 0].astype(F32))
    late_shards = [_wire_shard(k, weights[k] + order if k == "ret_gn_g" else weights[k]) for k in LATE_WEIGHTS]
    send_sems, recv_sems, late_thru, late_lands, token = _gather_start(late_shards, "gather_late_start")
    mod_x = mod_x + token[0, 0]

    def late_weights(after):
        lands = _gather_wait(send_sems, recv_sems, late_thru, late_lands, after, "gather_late_wait")
        own = [lax.dynamic_update_index_in_dim(land, shard, me, axis=1) for land, shard in zip(lands, late_shards)]
        return {k: _join_shards(k, g) for k, g in zip(LATE_WEIGHTS, own)}

    rs_layer1, rs_ffn0 = _SplitReduce("rs1_"), _SplitReduce("rs0_")

    class Hooks:
        layer1_grads = rs_layer1.start
        mid_ffn0_backward = rs_layer1.middle
        ffn0_grads = rs_ffn0.start
        after_attn_backward = rs_ffn0.middle

    small = {k: weights[k] for k in SMALL_NAMES}
    loss_tile, grad_x, (g_layer0, _), g_small, dmod_x, dmod_c = _local_step(
        x, ctx, loss_target, mod_x, mod_c, w_full, small, late_weights, Hooks)
    parts1 = rs_layer1.finish(grad_x)
    parts0_ffn = rs_ffn0.finish(grad_x)
    loss = lax.psum(loss_tile[0, 0], ("x", "y", "c"))

    n_mod = 2 * 6 * D_MODEL
    dm_rows = jnp.concatenate([dmod_x.reshape(B, n_mod), dmod_c.reshape(1, n_mod),
                               jnp.zeros((8 - B - 1, n_mod), F32)], axis=0)
    dm_all = _all_gather(dm_rows, "gather_dmod")
    dmc_tot = _sum_rows(dm_all[:, B:B + 1].reshape(N_DEV, 1, n_mod)[:, :, :].reshape(N_DEV, n_mod // LANES, LANES),
                        "sum_dmod_c").reshape(1, n_mod)
    dmod_rows = jnp.concatenate([dm_all[:, :B].reshape(N_DEV * B, n_mod), dmc_tot,
                                 jnp.zeros((ADA_ROWS - N_DEV * B - 1, n_mod), F32)], axis=0)
    dmod_mine = lax.dynamic_slice_in_dim(dmod_rows.reshape(ADA_ROWS, 2, N_DEV, ada_cols), me, 1, axis=2)
    dmod_mine = dmod_mine.reshape(ADA_ROWS, 2 * ada_cols).astype(MXU_DTYPE)
    g_ada2 = _mm(ada_in, dmod_mine, "tn", F32, "ada_dw")
    g_ada_w = jnp.stack([g_ada2[:, :ada_cols], g_ada2[:, ada_cols:]])
    dmc_mine = jnp.concatenate([dmod_mine[N_DEV * B:N_DEV * B + 1], jnp.zeros((7, 2 * ada_cols), MXU_DTYPE)], axis=0)
    dcc_part = _mm(dmc_mine, ada_w2, "nt", F32, "ada_dc")[0:1]
    g_ada_b = _sum_rows(dmod_rows[:, None, :].reshape(ADA_ROWS, n_mod // LANES, LANES), "sum_dmod_b").reshape(2, 6 * D_MODEL)
    sg = jax.nn.sigmoid(c_ctx)
    g_small["c_ctx"] = dcc_part.reshape(D_MODEL) * (sg * (1.0 + c_ctx * (1.0 - sg)))
    g_small["ada_b"] = g_ada_b * (1.0 / N_DEV)

    shapes = {k: weights[k].shape for k in SMALL_NAMES}
    n_small = sum(math.prod(s) for s in shapes.values())
    srows = -(-(-(-n_small // LANES)) // 8) * 8
    gs_all = _all_gather(_pack_small(g_small, srows), "gather_small_grads")
    sm = _adamw(_pack_small({k: weights[k] for k in SMALL_NAMES}, srows), _pack_small({k: mom1[k] for k in SMALL_NAMES}, srows),
                _pack_small({k: mom2[k] for k in SMALL_NAMES}, srows), [gs_all], "adamw_small")
    sm = [_unpack_small(t, shapes) for t in sm]

    ada_shape = ada_w.shape
    r2 = lambda t: t.reshape(ada_shape[0] * ada_shape[1], ada_shape[2])
    ada = [t.reshape(ada_shape) for t in _adamw(r2(ada_w), r2(m_ada_w), r2(v_ada_w), [r2(g_ada_w)[None]], "adamw_ada")]

    attn_grads = {k: g_layer0[k] for k in EARLY_WEIGHTS}
    big = _adamw_big(weights, mom1, mom2, [_reduce_scatter_in_call(attn_grads, "rs_"), parts0_ffn, parts1])

    def pick(i, name):
        if name in BIG_WEIGHTS:
            return big[name][i]
        if name == "ada_w":
            return ada[i]
        return sm[i][name]

    order = ("c_ctx", "ada_w", "ada_b", "norm1_g", "norm2_g", "ffn_w_in", "ffn_w_out", "attn_w_qkv", "attn_q_norm",
             "attn_k_norm", "attn_sink", "attn_w_o", "ret_w_qkvg", "ret_decay_logit", "ret_gn_g", "ret_w_o")
    outs = [loss, grad_x]
    for i in range(4):
        outs += [pick(i, n) for n in order]
    return tuple(outs)
```

```python
import functools
import math

import jax
import jax.numpy as jnp
from jax import lax
from jax.experimental import pallas as pl
from jax.experimental.pallas import tpu as pltpu

F32 = jnp.float32
MXU_DTYPE = jnp.bfloat16

D_MODEL = 1024
HEAD_DIM = 64
N_HEADS = 16
N_KV_HEADS = 4
GQA_GROUP = 4
WINDOW = 128
ATTN_BLOCK = 128
RET_HEADS = 4
RET_QK_DIM = 256
RET_V_DIM = 512
RET_VWIDTH = 2048
RET_CHUNK = 512
D_FF = 2816
GRID_W = 64
ROPE_BASE = 10000.0
EPS = 1e-6
NEG_INF = -1e30

ADAM_LR = 0.001
ADAM_B1 = 0.9
ADAM_B2 = 0.999
ADAM_EPS = 1e-08
ADAM_WD = 0.01
ADAM_STEP = 10

N_DEV = 8
LANES = 128
ROW_TILE = 512
VMEM_LIMIT = 48 * 1024 * 1024

MESH = pl.DeviceIdType.MESH
_ANY = pl.BlockSpec(memory_space=pl.ANY)
_SMEM = pl.BlockSpec(memory_space=pltpu.SMEM)


def _params(**kw):
    return pltpu.CompilerParams(vmem_limit_bytes=VMEM_LIMIT, **kw)


def _mx(v):
    return v.astype(MXU_DTYPE)


def _dot(a, b, dims):
    return lax.dot_general(_mx(a), _mx(b), (dims, ((), ())), preferred_element_type=F32)


_NN = ((1,), (0,))
_NT = ((1,), (1,))
_TN = ((0,), (0,))


def _tile(n, cands):
    for c in cands:
        if n % c == 0:
            return c
    return n


def _big_tile(n, cap):
    if n <= cap:
        return n
    for t in range(cap - cap % LANES, 0, -LANES):
        if n % t == 0:
            return t
    return n


MM_ROWS = 1024
MM_COLS = 1408
MM_DEPTH = 2048


def _k_tile(k):
    return _big_tile(k, MM_DEPTH)


def _mm(a, b, mode, out_dtype, name, *, bias=None, res=None, gate=None, gidx_for=None, gate_rows=None):
    if mode == "nn":
        (M, K), (_, N) = a.shape, b.shape
    elif mode == "nt":
        (M, K), (N, _) = a.shape, b.shape
    else:
        (K, M), (_, N) = a.shape, b.shape
    if res is not None:
        tm, tn = gate_rows, _big_tile(N, 512)
        gidx = gidx_for(tm)
    else:
        tm = _big_tile(M, MM_COLS if mode == "tn" else MM_ROWS)
        tn = _big_tile(N, MM_COLS)
    tk = _k_tile(K)
    nk = K // tk
    dims = {"nn": _NN, "nt": _NT, "tn": _TN}[mode]
    a_spec = (pl.BlockSpec((tk, tm), lambda i, j, k: (k, i)) if mode == "tn"
              else pl.BlockSpec((tm, tk), lambda i, j, k: (i, k)))
    b_spec = (pl.BlockSpec((tn, tk), lambda i, j, k: (j, k)) if mode == "nt"
              else pl.BlockSpec((tk, tn), lambda i, j, k: (k, j)))
    o_spec = pl.BlockSpec((tm, tn), lambda i, j, k: (i, j))
    in_specs, operands = [a_spec, b_spec], [a, b]
    if bias is not None:
        in_specs.append(pl.BlockSpec((1, tn), lambda i, j, k: (0, j)))
        operands.append(bias)
    if res is not None:
        in_specs += [o_spec, pl.BlockSpec((1, 1, tn), lambda i, j, k: (gidx(i), 0, j))]
        operands += [res, gate]
        out_shape = (jax.ShapeDtypeStruct((M, N), F32), jax.ShapeDtypeStruct((M, N), F32))
        out_specs = (o_spec, o_spec)
    else:
        out_shape = jax.ShapeDtypeStruct((M, N), out_dtype)
        out_specs = o_spec

    def body(*refs):
        a_ref, b_ref = refs[0], refs[1]
        extra = refs[2:len(operands)]
        outs = refs[len(operands):]
        prod = _dot(a_ref[...], b_ref[...], dims)

        def finish(acc):
            if bias is not None:
                outs[0][...] = (acc + extra[0][...]).astype(out_dtype)
            elif res is not None:
                outs[0][...] = acc
                outs[1][...] = extra[0][...] + extra[1][0] * acc
            else:
                outs[0][...] = acc.astype(out_dtype)

        if nk == 1:
            finish(prod)
        else:
            acc_ref = outs[-1]
            outs = outs[:-1]
            k = pl.program_id(2)

            @pl.when(k == 0)
            def _():
                acc_ref[...] = prod

            @pl.when(k > 0)
            def _():
                acc_ref[...] += prod

            @pl.when(k == nk - 1)
            def _():
                finish(acc_ref[...])

    return pl.pallas_call(
        body, name=name, grid=(M // tm, N // tn, nk), in_specs=in_specs, out_specs=out_specs, out_shape=out_shape,
        scratch_shapes=[pltpu.VMEM((tm, tn), F32)] if nk > 1 else [],
        compiler_params=_params())(*operands)


def _group_index(n_x_tiles, tiles_per_example, n_examples):
    def gidx(i):
        return jnp.where(i < n_x_tiles, i // tiles_per_example, n_examples)
    return gidx


def _norm_mod_fwd(x, g, shift, scale, gidx, name):
    T, Dm = x.shape

    def body(x_ref, g_ref, sh_ref, sc_ref, h_ref):
        xv = x_ref[...]
        r = lax.rsqrt(jnp.mean(xv * xv, axis=-1, keepdims=True) + EPS)
        y = xv * r * g_ref[...]
        h_ref[...] = (y * (1.0 + sc_ref[0]) + sh_ref[0]).astype(h_ref.dtype)

    row = pl.BlockSpec((ROW_TILE, Dm), lambda i: (i, 0))
    mod = pl.BlockSpec((1, 1, Dm), lambda i: (gidx(i), 0, 0))
    return pl.pallas_call(
        body, name=name, grid=(T // ROW_TILE,),
        in_specs=[row, pl.BlockSpec((1, Dm), lambda i: (0, 0)), mod, mod],
        out_specs=row, out_shape=jax.ShapeDtypeStruct((T, Dm), MXU_DTYPE),
        compiler_params=_params())(x, g, shift, scale)


def _first_of_group(i, gidx):
    return jnp.logical_or(i == 0, gidx(i) != gidx(jnp.maximum(i - 1, 0)))


def _norm_mod_bwd(dh, x, g, scale, dres, gidx, n_groups, name, gated=None, dx_rows=None):
    T, Dm = x.shape
    res_tiles = dres.shape[0] // ROW_TILE
    dx_tiles = (dx_rows or T) // ROW_TILE

    def body(*refs):
        dh_ref, x_ref, g_ref, sc_ref, dres_ref = refs[:5]
        n_in = 7 if gated else 5
        dx_ref, dsh_ref, dsc_ref, dg_ref = refs[n_in:n_in + 4]
        i = pl.program_id(0)
        xv, dhv = x_ref[...], dh_ref[...]
        r = lax.rsqrt(jnp.mean(xv * xv, axis=-1, keepdims=True) + EPS)
        xn = xv * r
        y = xn * g_ref[...]

        @pl.when(_first_of_group(i, gidx))
        def _():
            dsh_ref[...] = jnp.zeros_like(dsh_ref)
            dsc_ref[...] = jnp.zeros_like(dsc_ref)

        @pl.when(i == 0)
        def _():
            dg_ref[...] = jnp.zeros_like(dg_ref)

        dsh_ref[0] += jnp.sum(dhv, axis=0, keepdims=True)
        dsc_ref[0] += jnp.sum(dhv * y, axis=0, keepdims=True)
        dy = dhv * (1.0 + sc_ref[0])
        dg_ref[...] += jnp.sum(dy * xn, axis=0, keepdims=True)
        dxn = dy * g_ref[...]
        dx = r * (dxn - xn * jnp.mean(dxn * xn, axis=-1, keepdims=True))
        dx = dx + (dres_ref[...] if res_tiles == T // ROW_TILE else jnp.where(i < res_tiles, dres_ref[...], 0.0))
        if dx_tiles == T // ROW_TILE:
            dx_ref[...] = dx
        else:
            @pl.when(i < dx_tiles)
            def _():
                dx_ref[...] = dx
        if gated:
            f_ref, gate_ref = refs[5:7]
            dz_ref, dgate_ref = refs[n_in + 4:]

            @pl.when(_first_of_group(i, gidx))
            def _():
                dgate_ref[...] = jnp.zeros_like(dgate_ref)

            dgate_ref[0] += jnp.sum(dx * f_ref[...], axis=0, keepdims=True)
            dz_ref[...] = (dx * gate_ref[0]).astype(dz_ref.dtype)

    row = pl.BlockSpec((ROW_TILE, Dm), lambda i: (i, 0))
    mod = pl.BlockSpec((1, 1, Dm), lambda i: (gidx(i), 0, 0))
    vec = pl.BlockSpec((1, Dm), lambda i: (0, 0))
    mod_shape = jax.ShapeDtypeStruct((n_groups, 1, Dm), F32)
    res_row = pl.BlockSpec((ROW_TILE, Dm), lambda i: (jnp.minimum(i, res_tiles - 1), 0))
    in_specs, operands = [row, row, vec, mod, res_row], [dh, x, g, scale, dres]
    dx_row = pl.BlockSpec((ROW_TILE, Dm), lambda i: (jnp.minimum(i, dx_tiles - 1), 0))
    out_specs = [dx_row, mod, mod, vec]
    out_shape = [jax.ShapeDtypeStruct((dx_tiles * ROW_TILE, Dm), F32), mod_shape, mod_shape,
                 jax.ShapeDtypeStruct((1, Dm), F32)]
    if gated:
        in_specs, operands = in_specs + [row, mod], operands + list(gated)
        out_specs, out_shape = out_specs + [row, mod], out_shape + [jax.ShapeDtypeStruct((T, Dm), MXU_DTYPE), mod_shape]
    return pl.pallas_call(
        body, name=name, grid=(T // ROW_TILE,), in_specs=in_specs, out_specs=tuple(out_specs),
        out_shape=tuple(out_shape), compiler_params=_params())(*operands)


FFN_IN_ROWS = 1024
FFN_IN_COLS = 1408
FFN_BWD_ROWS = 256


def _ffn_in_swiglu(h, w_in, name):
    T, Dm = h.shape
    nj = D_FF // FFN_IN_COLS

    def body(h_ref, wg_ref, wu_ref, g_ref, u_ref, a_ref):
        hv = h_ref[...]
        gate = _dot(hv, wg_ref[...], _NN)
        up = _dot(hv, wu_ref[...], _NN)
        g_ref[...] = gate
        u_ref[...] = up
        a_ref[...] = (gate * jax.nn.sigmoid(gate) * up).astype(a_ref.dtype)

    out = pl.BlockSpec((FFN_IN_ROWS, FFN_IN_COLS), lambda i, j: (i, j))
    pre = jax.ShapeDtypeStruct((T, D_FF), F32)
    return pl.pallas_call(
        body, name=name, grid=(T // FFN_IN_ROWS, nj),
        in_specs=[pl.BlockSpec((FFN_IN_ROWS, Dm), lambda i, j: (i, 0)),
                  pl.BlockSpec((Dm, FFN_IN_COLS), lambda i, j: (0, j)),
                  pl.BlockSpec((Dm, FFN_IN_COLS), lambda i, j: (0, nj + j))],
        out_specs=(out, out, out), out_shape=(pre, pre, jax.ShapeDtypeStruct((T, D_FF), MXU_DTYPE)),
        compiler_params=_params())(h, w_in, w_in)


def _ffn_out_bwd_swiglu(dz, w_out, gate, up, name):
    T, Dm = dz.shape

    def body(dz_ref, w_ref, g_ref, u_ref, du_ref):
        da = _dot(dz_ref[...], w_ref[...], _NT)
        gv, uv = g_ref[...], u_ref[...]
        sg = jax.nn.sigmoid(gv)
        du_ref[:, :D_FF] = (da * uv * (sg * (1.0 + gv * (1.0 - sg)))).astype(du_ref.dtype)
        du_ref[:, D_FF:] = (da * gv * sg).astype(du_ref.dtype)

    half = pl.BlockSpec((FFN_BWD_ROWS, D_FF), lambda i: (i, 0))
    return pl.pallas_call(
        body, name=name, grid=(T // FFN_BWD_ROWS,),
        in_specs=[pl.BlockSpec((FFN_BWD_ROWS, Dm), lambda i: (i, 0)), pl.BlockSpec((D_FF, Dm), lambda i: (0, 0)), half, half],
        out_specs=pl.BlockSpec((FFN_BWD_ROWS, 2 * D_FF), lambda i: (i, 0)),
        out_shape=jax.ShapeDtypeStruct((T, 2 * D_FF), MXU_DTYPE), compiler_params=_params())(dz, w_out, gate, up)


def _loss_fwd_bwd(y, target, f, gate, gidx, n_groups, name):
    T, Dm = y.shape

    def body(y_ref, t_ref, f_ref, gate_ref, loss_ref, dy_ref, dz_ref, dgate_ref):
        i = pl.program_id(0)
        err = y_ref[...] - t_ref[...]

        @pl.when(i == 0)
        def _():
            loss_ref[...] = jnp.zeros_like(loss_ref)

        @pl.when(_first_of_group(i, gidx))
        def _():
            dgate_ref[...] = jnp.zeros_like(dgate_ref)

        loss_ref[...] += 0.5 * jnp.sum(jnp.mean(err * err, axis=-1, keepdims=True))
        dy = err * (1.0 / Dm)
        dy_ref[...] = dy
        dgate_ref[0] += jnp.sum(dy * f_ref[...], axis=0, keepdims=True)
        dz_ref[...] = (dy * gate_ref[0]).astype(dz_ref.dtype)

    row = pl.BlockSpec((ROW_TILE, Dm), lambda i: (i, 0))
    mod = pl.BlockSpec((1, 1, Dm), lambda i: (gidx(i), 0, 0))
    return pl.pallas_call(
        body, name=name, grid=(T // ROW_TILE,), in_specs=[row, row, row, mod],
        out_specs=(pl.BlockSpec((8, LANES), lambda i: (0, 0)), row, row, mod),
        out_shape=(jax.ShapeDtypeStruct((8, LANES), F32), jax.ShapeDtypeStruct((T, Dm), F32),
                   jax.ShapeDtypeStruct((T, Dm), MXU_DTYPE), jax.ShapeDtypeStruct((n_groups, 1, Dm), F32)),
        compiler_params=_params())(y, target, f, gate)


def _rope_tables(seq, head_dim):
    axis_dim = head_dim // 2
    half = axis_dim // 2
    pos = jnp.arange(seq, dtype=jnp.int32)
    row = (pos // GRID_W).astype(F32)[:, None]
    col = (pos % GRID_W).astype(F32)[:, None]
    inv = ROPE_BASE ** (-jnp.arange(0, axis_dim, 2, dtype=F32) / axis_dim)
    lane = jnp.arange(head_dim, dtype=jnp.int32)
    within = lane % axis_dim
    ang = jnp.where((lane // axis_dim == 0)[None, :], row, col) * inv[within % half][None, :]
    cos = jnp.cos(ang)
    sin = jnp.where((within < half)[None, :], -jnp.sin(ang), jnp.sin(ang))
    cos = jnp.concatenate([cos, jnp.ones((ROW_TILE, head_dim), F32)], axis=0)
    sin = jnp.concatenate([sin, jnp.zeros((ROW_TILE, head_dim), F32)], axis=0)
    return cos, sin


def _pair_swap(v, half):
    if 2 * half == LANES:
        return pltpu.roll(v, half, axis=1)
    lane = lax.broadcasted_iota(jnp.int32, v.shape, 1)
    return jnp.where((lane % (2 * half)) < half, pltpu.roll(v, LANES - half, axis=1), pltpu.roll(v, half, axis=1))


def _head_sum(v, ones_ref):
    hi = v.astype(MXU_DTYPE)
    lo = (v - hi.astype(F32)).astype(MXU_DTYPE)
    return (jnp.dot(hi, ones_ref[...], preferred_element_type=F32)
            + jnp.dot(lo, ones_ref[...], preferred_element_type=F32))


def _head_ones():
    lane = jnp.arange(LANES)
    return (lane[:, None] // HEAD_DIM == lane[None, :] // HEAD_DIM).astype(MXU_DTYPE)


ATTN_QK_BLOCKS = (N_HEADS + N_KV_HEADS) * HEAD_DIM // LANES
ATTN_ALL_BLOCKS = (N_HEADS + 2 * N_KV_HEADS) * HEAD_DIM // LANES
ATTN_Q_BLOCKS = N_HEADS * HEAD_DIM // LANES
ATTN_SCALE = HEAD_DIM ** -0.5


def _attn_prep_fwd(qkv, gains, cos, sin, tidx, name):
    T, W = qkv.shape

    def body(x_ref, g_ref, cos_ref, sin_ref, ones_ref, o_ref):
        for cb in range(ATTN_ALL_BLOCKS):
            cols = slice(cb * LANES, (cb + 1) * LANES)
            xv = x_ref[:, cols]
            if cb < ATTN_QK_BLOCKS:
                r = lax.rsqrt(_head_sum(xv * xv, ones_ref) * (1.0 / HEAD_DIM) + EPS)
                y = xv * r * g_ref[0 if cb < ATTN_Q_BLOCKS else 1]
                xv = y * cos_ref[...] + _pair_swap(y, HEAD_DIM // 4) * sin_ref[...]
                if cb < ATTN_Q_BLOCKS:
                    xv = xv * ATTN_SCALE
            o_ref[:, cols] = xv.astype(o_ref.dtype)

    row = pl.BlockSpec((ROW_TILE, W), lambda i: (i, 0))
    tab = pl.BlockSpec((ROW_TILE, LANES), lambda i: (tidx(i), 0))
    return pl.pallas_call(
        body, name=name, grid=(T // ROW_TILE,),
        in_specs=[row, pl.BlockSpec((2, 1, LANES), lambda i: (0, 0, 0)), tab, tab,
                  pl.BlockSpec((LANES, LANES), lambda i: (0, 0))],
        out_specs=row, out_shape=jax.ShapeDtypeStruct(qkv.shape, MXU_DTYPE),
        compiler_params=_params())(qkv, gains, cos, sin, _head_ones())


def _attn_prep_bwd(latent, context, qkv, gains, cos, sin, tidx, name):
    T, W = qkv.shape
    qk_w = ATTN_QK_BLOCKS * LANES
    q_w = ATTN_Q_BLOCKS * LANES
    n_x = latent[0].shape[0] // ROW_TILE

    def body(dqx_ref, dkx_ref, dvx_ref, dqc_ref, dkc1_ref, dkc2_ref, dvc1_ref, dvc2_ref,
             x_ref, g_ref, cos_ref, sin_ref, ones_ref, o_ref, dg_ref):
        is_latent = pl.program_id(0) < n_x

        @pl.when(pl.program_id(0) == 0)
        def _():
            dg_ref[...] = jnp.zeros_like(dg_ref)

        for cb in range(ATTN_QK_BLOCKS):
            cols = slice(cb * LANES, (cb + 1) * LANES)
            xv = x_ref[:, cols]
            if cb < ATTN_Q_BLOCKS:
                d = jnp.where(is_latent, dqx_ref[:, cols], dqc_ref[:, cols]) * ATTN_SCALE
            else:
                kc = slice(cb * LANES - q_w, (cb + 1) * LANES - q_w)
                d = jnp.where(is_latent, dkx_ref[:, kc], dkc1_ref[:, kc] + dkc2_ref[:, kc])
            r = lax.rsqrt(_head_sum(xv * xv, ones_ref) * (1.0 / HEAD_DIM) + EPS)
            xn = xv * r
            dy = d * cos_ref[...] + _pair_swap(d * sin_ref[...], HEAD_DIM // 4)
            dg_ref[:, cols] += jnp.sum(dy * xn, axis=0, keepdims=True)
            dxn = dy * g_ref[0 if cb < ATTN_Q_BLOCKS else 1]
            dx = r * (dxn - xn * (_head_sum(dxn * xn, ones_ref) * (1.0 / HEAD_DIM)))
            o_ref[:, cols] = dx.astype(o_ref.dtype)
        o_ref[:, qk_w:] = jnp.where(is_latent, dvx_ref[...], dvc1_ref[...] + dvc2_ref[...]).astype(o_ref.dtype)

    row = lambda w: pl.BlockSpec((ROW_TILE, w), lambda i: (i, 0))
    lat = lambda t: pl.BlockSpec((ROW_TILE, t.shape[1]), lambda i: (jnp.minimum(i, n_x - 1), 0))
    ctx = lambda t: pl.BlockSpec((ROW_TILE, t.shape[1]), lambda i: (jnp.maximum(i - n_x, 0), 0))
    tab = pl.BlockSpec((ROW_TILE, LANES), lambda i: (tidx(i), 0))
    return pl.pallas_call(
        body, name=name, grid=(T // ROW_TILE,),
        in_specs=[lat(t) for t in latent] + [ctx(t) for t in context]
        + [row(W), pl.BlockSpec((2, 1, LANES), lambda i: (0, 0, 0)), tab, tab, pl.BlockSpec((LANES, LANES), lambda i: (0, 0))],
        out_specs=(row(W), pl.BlockSpec((1, qk_w), lambda i: (0, 0))),
        out_shape=(jax.ShapeDtypeStruct(qkv.shape, MXU_DTYPE), jax.ShapeDtypeStruct((1, qk_w), F32)),
        compiler_params=_params())(*latent, *context, qkv, gains, cos, sin, _head_ones())


RET_QK_BLOCKS = 2 * RET_HEADS * RET_QK_DIM // LANES


def _ret_rope(x, cos, sin, tidx, name):
    T = x.shape[0]
    W = RET_QK_BLOCKS * LANES
    k_scale = RET_QK_DIM ** -0.5

    def body(x_ref, cos_ref, sin_ref, o_ref):
        for cb in range(RET_QK_BLOCKS):
            cols = slice(cb * LANES, (cb + 1) * LANES)
            tcols = slice((cb % 2) * LANES, (cb % 2 + 1) * LANES)
            xv = x_ref[:, cols]
            out = xv * cos_ref[:, tcols] + pltpu.roll(xv, LANES // 2, axis=1) * sin_ref[:, tcols]
            if cb >= RET_QK_BLOCKS // 2:
                out = out * k_scale
            o_ref[:, cols] = out

    row = pl.BlockSpec((ROW_TILE, W), lambda i: (i, 0))
    tab = pl.BlockSpec((ROW_TILE, RET_QK_DIM), lambda i: (tidx(i), 0))
    return pl.pallas_call(
        body, name=name, grid=(T // ROW_TILE,), in_specs=[row, tab, tab], out_specs=row,
        out_shape=jax.ShapeDtypeStruct((T, W), F32), compiler_params=_params())(x, cos, sin)


ASSEMBLE_ROWS = 256


def _ret_grad_assemble(x_parts, c_parts, dg, cos, sin, seq, name):
    NX, NC = x_parts[0].shape[0], c_parts[0].shape[0]
    T = NX + NC
    rt = ASSEMBLE_ROWS
    nxt = NX // rt
    qk_w = RET_HEADS * RET_QK_DIM
    k_scale = RET_QK_DIM ** -0.5

    def unrotate(d, cos_ref, sin_ref, scale):
        outs = []
        for cb in range(qk_w // LANES):
            cols = slice(cb * LANES, (cb + 1) * LANES)
            tcols = slice((cb % 2) * LANES, (cb % 2 + 1) * LANES)
            dv_ = d[:, cols]
            o = dv_ * cos_ref[:, tcols] + pltpu.roll(dv_ * sin_ref[:, tcols], LANES // 2, axis=1)
            outs.append(o * scale if scale != 1.0 else o)
        return outs

    def body(dqf, dqb, dkf, dkb, dvf, dvb, dg_ref, dkcf, dkcb, dvcf, dvcb, cos_ref, sin_ref, o_ref):
        i = pl.program_id(0)

        def write_k(parts):
            for cb, o in enumerate(parts):
                o_ref[:, qk_w + cb * LANES:qk_w + (cb + 1) * LANES] = o.astype(o_ref.dtype)

        @pl.when(i < nxt)
        def _():
            for cb, o in enumerate(unrotate(dqf[...] + dqb[...], cos_ref, sin_ref, 1.0)):
                o_ref[:, cb * LANES:(cb + 1) * LANES] = o.astype(o_ref.dtype)
            write_k(unrotate(dkf[...] + dkb[...], cos_ref, sin_ref, k_scale))
            o_ref[:, 2 * qk_w:2 * qk_w + RET_VWIDTH] = (dvf[...] + dvb[...]).astype(o_ref.dtype)
            o_ref[:, 2 * qk_w + RET_VWIDTH:] = dg_ref[...].astype(o_ref.dtype)

        @pl.when(i >= nxt)
        def _():
            o_ref[:, :qk_w] = jnp.zeros((rt, qk_w), o_ref.dtype)
            write_k(unrotate(dkcf[...] + dkcb[...], cos_ref, sin_ref, k_scale))
            o_ref[:, 2 * qk_w:2 * qk_w + RET_VWIDTH] = (dvcf[...] + dvcb[...]).astype(o_ref.dtype)
            o_ref[:, 2 * qk_w + RET_VWIDTH:] = jnp.zeros((rt, RET_VWIDTH), o_ref.dtype)

    xs = lambda w: pl.BlockSpec((rt, w), lambda i: (jnp.minimum(i, nxt - 1), 0))
    cs = lambda w: pl.BlockSpec((rt, w), lambda i: (jnp.maximum(i - nxt, 0), 0))
    tab = pl.BlockSpec((rt, RET_QK_DIM), lambda i: (jnp.where(i < nxt, i % (seq // rt), seq // rt), 0))
    return pl.pallas_call(
        body, name=name, grid=(T // rt,),
        in_specs=[xs(qk_w)] * 4 + [xs(RET_VWIDTH)] * 3 + [cs(qk_w)] * 2 + [cs(RET_VWIDTH)] * 2 + [tab, tab],
        out_specs=pl.BlockSpec((rt, 2 * qk_w + 2 * RET_VWIDTH), lambda i: (i, 0)),
        out_shape=jax.ShapeDtypeStruct((T, 2 * qk_w + 2 * RET_VWIDTH), MXU_DTYPE),
        compiler_params=_params())(*x_parts, dg, *c_parts, cos, sin)


def _band_bias(qb, seq):
    nb = seq // qb
    assert nb >= 2
    i = jnp.arange(GQA_GROUP * qb, dtype=jnp.int32)[:, None] % qb
    n = jnp.arange(3 * qb, dtype=jnp.int32)[None, :]
    in_window = (n >= i) & (n - i <= 2 * WINDOW)
    variants = [in_window & (n >= qb), in_window, in_window & (n < 2 * qb)]
    return jnp.stack([jnp.where(v, 0.0, NEG_INF).astype(F32) for v in variants])


GROUP_ORDER = (0, 2, 1, 3)


def _stack_halves(blk):
    return jnp.concatenate([blk[:, :LANES], blk[:, LANES:]], axis=0)


def _unstack_halves(v, rows):
    return jnp.concatenate([v[:rows], v[rows:]], axis=1)


def _align_head(pair, odd):
    lane = lax.broadcasted_iota(jnp.int32, pair.shape, 1)
    mine = jnp.where((lane >= HEAD_DIM) == odd, pair, jnp.zeros_like(pair))
    rolled = pltpu.roll(mine, HEAD_DIM, axis=1)
    return jnp.where(odd, rolled, mine), jnp.where(odd, mine, rolled)


def _scores(out_ref, q2, x_eo):
    half = q2.shape[0]
    out_ref[:half, :] = _dot(q2, x_eo[0], _NT)
    out_ref[half:, :] = _dot(q2, x_eo[1], _NT)


def _apply(p_ref, x_eo):
    half = p_ref.shape[0] // 2
    return _dot(p_ref[:half, :], x_eo[0], _NN) + _dot(p_ref[half:, :], x_eo[1], _NN)


def _kv_grad(a_ref, q2, odd):
    half = a_ref.shape[0] // 2
    even_t = _dot(q2, a_ref[:half, :], _TN)
    odd_t = _dot(q2, a_ref[half:, :], _TN)
    mine = even_t[:HEAD_DIM] + odd_t[HEAD_DIM:]
    zero = jnp.zeros_like(mine)
    placed = jnp.where(odd, jnp.concatenate([zero, mine], axis=0), jnp.concatenate([mine, zero], axis=0))
    return placed.T


ATTN_ROW_CHUNK = 32


def _softmax_chunks(s_c_ref, s_l_ref, bias_ref, sink_ref, kv_head, qb, emit):
    for r0 in range(0, GQA_GROUP * qb, ATTN_ROW_CHUNK):
        rows = slice(r0, r0 + ATTN_ROW_CHUNK)
        t = r0 // qb
        sink = jnp.full((ATTN_ROW_CHUNK, 1), sink_ref[kv_head, GROUP_ORDER[t]], F32)
        s_c = s_c_ref[rows, :]
        m = jnp.maximum(jnp.max(s_c, axis=-1, keepdims=True), sink)
        s_l = None
        if s_l_ref is not None:
            s_l = s_l_ref[rows, :] + bias_ref[0, rows, :]
            m = jnp.maximum(m, jnp.max(s_l, axis=-1, keepdims=True))
        e_c = jnp.exp(s_c - m)
        e_s = jnp.exp(sink - m)
        den = jnp.sum(e_c, axis=-1, keepdims=True) + e_s
        e_l = None
        if s_l_ref is not None:
            e_l = jnp.exp(s_l - m)
            den = den + jnp.sum(e_l, axis=-1, keepdims=True)
        inv = 1.0 / den
        emit(t, rows, e_c * inv, (None if e_l is None else e_l * inv), e_s * inv)


GROUP_W = GQA_GROUP * HEAD_DIM
K_LANE_BLOCK = N_HEADS * HEAD_DIM // LANES
V_LANE_BLOCK = K_LANE_BLOCK + N_KV_HEADS * HEAD_DIM // LANES


def _attn_specs(B, seq, ctx_len, ctx_queries):
    ctx0 = B * seq // ctx_len
    if ctx_queries:
        qb, nb = ctx_len, 1
        qrow = lambda b, j: ctx0 + b
    else:
        qb, nb = ATTN_BLOCK, seq // ATTN_BLOCK
        qrow = lambda b, j: b * nb + j
    kv_w = N_KV_HEADS * HEAD_DIM
    k_blk, v_blk = N_HEADS * HEAD_DIM // kv_w, N_HEADS * HEAD_DIM // kv_w + 1
    q_spec = pl.BlockSpec((qb, N_HEADS * HEAD_DIM), lambda b, p, j: (qrow(b, j), 0))
    c_specs = [pl.BlockSpec((ctx_len, kv_w), lambda b, p, j: (ctx0 + b, k_blk)),
               pl.BlockSpec((ctx_len, kv_w), lambda b, p, j: (ctx0 + b, v_blk))]
    local = []
    if not ctx_queries:
        near = [lambda j: jnp.maximum(j - 1, 0), lambda j: j, lambda j: jnp.minimum(j + 1, nb - 1)]
        for blk in (k_blk, v_blk):
            for f in near:
                local.append(pl.BlockSpec((qb, kv_w), lambda b, p, j, f=f, blk=blk: (b * nb + f(j), blk)))
        local.append(pl.BlockSpec(
            (1, GQA_GROUP * qb, 3 * qb), lambda b, p, j: (jnp.where(j == 0, 0, jnp.where(j == nb - 1, 2, 1)), 0, 0)))
    return qb, nb, qrow, q_spec, c_specs, local


def _attn_operands(refs, has_local, head):
    odd = bool(head % 2)
    pair = slice((head // 2) * LANES, (head // 2 + 1) * LANES)
    n_local = 7 if has_local else 0
    q2 = _stack_halves(refs[0][:, head * GROUP_W:(head + 1) * GROUP_W])
    kc = _align_head(refs[1 + n_local][:, pair], odd)
    vc = _align_head(refs[2 + n_local][:, pair], odd)
    kl = vl = bias_ref = None
    if has_local:
        kl = _align_head(jnp.concatenate([r[:, pair] for r in refs[1:4]], axis=0), odd)
        vl = _align_head(jnp.concatenate([r[:, pair] for r in refs[4:7]], axis=0), odd)
        bias_ref = refs[7]
    return odd, q2, kc, vc, kl, vl, bias_ref


def _score_scratch(qb, ctx_len, has_local, dtypes):
    rows = GQA_GROUP * qb
    out = []
    for dt in dtypes:
        out.append(pltpu.VMEM((rows, ctx_len), dt))
        if has_local:
            out.append(pltpu.VMEM((rows, 3 * qb), dt))
    return out


def _score_bufs(scratch, has_local):
    if has_local:
        return [(scratch[i], scratch[i + 1]) for i in range(0, len(scratch), 2)]
    return [(s, None) for s in scratch]


def _attn_fwd(qkv, sink, B, seq, ctx_len, ctx_queries, name):
    has_local = not ctx_queries
    qb, nb, _, q_spec, c_specs, local = _attn_specs(B, seq, ctx_len, ctx_queries)
    n_rows = B * (ctx_len if ctx_queries else seq)
    n_in = 1 + (7 if has_local else 0) + 3

    per_head = _score_scratch(qb, ctx_len, has_local, (F32, MXU_DTYPE))

    def body(*refs):
        sink_ref, o_ref = refs[n_in - 1], refs[n_in]
        scratch = refs[n_in + 1:]
        for kv_head in range(N_KV_HEADS):
            sub = kv_head
            (s_c_ref, s_l_ref), (p_c_ref, p_l_ref) = _score_bufs(
                scratch[sub * len(per_head):(sub + 1) * len(per_head)], has_local)
            _, q2, kc, vc, kl, vl, bias_ref = _attn_operands(refs, has_local, kv_head)
            _scores(s_c_ref, q2, kc)
            if has_local:
                _scores(s_l_ref, q2, kl)

            def emit(t, rows, p_c, p_l, p_s, p_c_ref=p_c_ref, p_l_ref=p_l_ref):
                p_c_ref[rows, :] = p_c.astype(p_c_ref.dtype)
                if has_local:
                    p_l_ref[rows, :] = p_l.astype(p_l_ref.dtype)

            _softmax_chunks(s_c_ref, s_l_ref, bias_ref, sink_ref, kv_head, qb, emit)
            o2 = _apply(p_c_ref, vc)
            if has_local:
                o2 = o2 + _apply(p_l_ref, vl)
            o_ref[:, sub * GROUP_W:(sub + 1) * GROUP_W] = _unstack_halves(o2, qb).astype(o_ref.dtype)

    operands = [qkv] + ([qkv] * 6 + [_band_bias(qb, seq)] if has_local else []) + [qkv, qkv, sink]
    return pl.pallas_call(
        body, name=name, grid=(B, 1, nb),
        in_specs=[q_spec] + local + c_specs + [_SMEM],
        out_specs=pl.BlockSpec((qb, N_HEADS * HEAD_DIM), lambda b, p, j: (b * nb + j, 0)),
        out_shape=jax.ShapeDtypeStruct((n_rows, N_HEADS * HEAD_DIM), MXU_DTYPE),
        scratch_shapes=per_head * N_KV_HEADS, compiler_params=_params())(*operands)


def _attn_bwd(qkv, sink, do, B, seq, ctx_len, ctx_queries, name):
    has_local = not ctx_queries
    qb, nb, qrow, q_spec, c_specs, local = _attn_specs(B, seq, ctx_len, ctx_queries)
    n_rows = B * (ctx_len if ctx_queries else seq)

    n_out = 6 if has_local else 4
    per_head = _score_scratch(qb, ctx_len, has_local, (F32, F32, MXU_DTYPE, MXU_DTYPE))

    def body(*refs):
        n_in = 1 + (7 if has_local else 0) + 4
        sink_ref, do_ref = refs[n_in - 2:n_in]
        outs = refs[n_in:n_in + n_out]
        scratch = refs[n_in + n_out:]
        dq_ref = outs[0]
        dkc_ref, dvc_ref, dsink_ref = outs[-3:]
        b, j = pl.program_id(0), pl.program_id(2)

        @pl.when(j == 0)
        def _():
            dkc_ref[...] = jnp.zeros_like(dkc_ref)
            dvc_ref[...] = jnp.zeros_like(dvc_ref)
            if has_local:
                outs[1][...] = jnp.zeros_like(outs[1])
                outs[2][...] = jnp.zeros_like(outs[2])

        @pl.when((b == 0) & (j == 0))
        def _():
            dsink_ref[...] = jnp.zeros_like(dsink_ref)

        for kv_head in range(N_KV_HEADS):
            sub = kv_head
            lanes = slice((kv_head // 2) * LANES, (kv_head // 2 + 1) * LANES)
            (s_c_ref, s_l_ref), (dp_c_ref, dp_l_ref), (p_c_ref, p_l_ref), (ds_c_ref, ds_l_ref) = _score_bufs(
                scratch[sub * len(per_head):(sub + 1) * len(per_head)], has_local)
            odd, q2, kc, vc, kl, vl, bias_ref = _attn_operands(refs, has_local, kv_head)
            do2 = _stack_halves(do_ref[:, sub * GROUP_W:(sub + 1) * GROUP_W])
            _scores(s_c_ref, q2, kc)
            _scores(dp_c_ref, do2, vc)
            if has_local:
                _scores(s_l_ref, q2, kl)
                _scores(dp_l_ref, do2, vl)
            dsink_parts = [jnp.zeros((), F32)] * GQA_GROUP

            def emit(t, rows, p_c, p_l, p_s, dp_c_ref=dp_c_ref, dp_l_ref=dp_l_ref, p_c_ref=p_c_ref, p_l_ref=p_l_ref,
                     ds_c_ref=ds_c_ref, ds_l_ref=ds_l_ref, dsink_parts=dsink_parts):
                dp_c = dp_c_ref[rows, :]
                delta = jnp.sum(p_c * dp_c, axis=-1, keepdims=True)
                if has_local:
                    dp_l = dp_l_ref[rows, :]
                    delta = delta + jnp.sum(p_l * dp_l, axis=-1, keepdims=True)
                    p_l_ref[rows, :] = p_l.astype(p_l_ref.dtype)
                    ds_l_ref[rows, :] = (p_l * (dp_l - delta)).astype(ds_l_ref.dtype)
                p_c_ref[rows, :] = p_c.astype(p_c_ref.dtype)
                ds_c_ref[rows, :] = (p_c * (dp_c - delta)).astype(ds_c_ref.dtype)
                dsink_parts[t] = dsink_parts[t] - jnp.sum(p_s * delta)

            _softmax_chunks(s_c_ref, s_l_ref, bias_ref, sink_ref, kv_head, qb, emit)
            dq2 = _apply(ds_c_ref, kc)
            dkc_ref[:, lanes] += _kv_grad(ds_c_ref, q2, odd)
            dvc_ref[:, lanes] += _kv_grad(p_c_ref, do2, odd)
            if has_local:
                dq2 = dq2 + _apply(ds_l_ref, kl)
                dkl = _kv_grad(ds_l_ref, q2, odd)
                dvl = _kv_grad(p_l_ref, do2, odd)
                dk_ref, dv_ref = outs[1], outs[2]
                for t in range(3):
                    def add(t=t, dkl=dkl, dvl=dvl, lanes=lanes):
                        start = pl.multiple_of((j - 1 + t) * qb, qb)
                        dk_ref[pl.ds(start, qb), lanes] += dkl[t * qb:(t + 1) * qb]
                        dv_ref[pl.ds(start, qb), lanes] += dvl[t * qb:(t + 1) * qb]
                    if t == 0:
                        pl.when(j > 0)(add)
                    elif t == 2:
                        pl.when(j < nb - 1)(add)
                    else:
                        add()
            dq_ref[:, sub * GROUP_W:(sub + 1) * GROUP_W] = _unstack_halves(dq2, qb)
            row8 = lax.broadcasted_iota(jnp.int32, (8, LANES), 0)
            tile = jnp.zeros((8, LANES), F32)
            for t, gi in enumerate(GROUP_ORDER):
                tile = jnp.where(row8 == gi, dsink_parts[t], tile)
            dsink_ref[kv_head * 8:(kv_head + 1) * 8, :] += tile

    kv_w = N_KV_HEADS * HEAD_DIM
    seq_spec = pl.BlockSpec((seq, kv_w), lambda b, p, j: (b, 0))
    ctx_spec = pl.BlockSpec((ctx_len, kv_w), lambda b, p, j: (b, 0))
    do_spec = pl.BlockSpec((qb, N_HEADS * HEAD_DIM), lambda b, p, j: (qrow(b, j), 0))
    operands = [qkv] + ([qkv] * 6 + [_band_bias(qb, seq)] if has_local else []) + [qkv, qkv, sink, do]
    out_specs = ([pl.BlockSpec((qb, N_HEADS * HEAD_DIM), lambda b, p, j: (b * nb + j, 0))]
                 + ([seq_spec, seq_spec] if has_local else [])
                 + [ctx_spec, ctx_spec, pl.BlockSpec((32, LANES), lambda b, p, j: (0, 0))])
    out_shape = ([jax.ShapeDtypeStruct((n_rows, N_HEADS * HEAD_DIM), F32)]
                 + ([jax.ShapeDtypeStruct((B * seq, kv_w), F32)] * 2 if has_local else [])
                 + [jax.ShapeDtypeStruct((B * ctx_len, kv_w), F32)] * 2 + [jax.ShapeDtypeStruct((32, LANES), F32)])
    return pl.pallas_call(
        body, name=name, grid=(B, 1, nb),
        in_specs=[q_spec] + local + c_specs + [_SMEM, do_spec],
        out_specs=tuple(out_specs), out_shape=tuple(out_shape), scratch_shapes=per_head * N_KV_HEADS,
        compiler_params=_params())(*operands)


def _ret_decays(lg, rev):
    n = lax.broadcasted_iota(jnp.int32, (RET_CHUNK, RET_CHUNK), 0).astype(F32)
    m = lax.broadcasted_iota(jnp.int32, (RET_CHUNK, RET_CHUNK), 1).astype(F32)
    pos = lax.broadcasted_iota(jnp.int32, (RET_CHUNK, 1), 0).astype(F32)
    diff = (m - n) if rev else (n - m)
    a_exp = jnp.maximum(diff, 0.0)
    intra = jnp.where(diff >= 0, jnp.exp(lg * a_exp), 0.0)
    q_exp = (RET_CHUNK - pos) if rev else (pos + 1.0)
    k_exp = pos if rev else (RET_CHUNK - 1.0 - pos)
    chunk = jnp.exp(jnp.full((1, 1), RET_CHUNK, F32) * lg)
    return intra, a_exp, jnp.exp(lg * q_exp), q_exp, jnp.exp(lg * k_exp), k_exp, chunk


def _ctx_decay(lg, ctx_len, rev):
    t = lax.broadcasted_iota(jnp.int32, (ctx_len, 1), 0).astype(F32)
    expo = t if rev else (ctx_len - 1.0 - t)
    return jnp.exp(lg * expo), expo


def _ret_specs(B, seq, ctx_len, order):
    nc = seq // RET_CHUNK
    x_blocks = B * seq // ctx_len

    def rows(b, c):
        return b * nc + order(c, nc)

    q_spec = pl.BlockSpec((RET_CHUNK, RET_QK_DIM), lambda b, h, c: (rows(b, c), h))
    k_spec = pl.BlockSpec((RET_CHUNK, RET_QK_DIM), lambda b, h, c: (rows(b, c), RET_HEADS + h))
    v_spec = pl.BlockSpec((RET_CHUNK, RET_V_DIM), lambda b, h, c: (rows(b, c), RET_HEADS + h))
    kc_spec = pl.BlockSpec((ctx_len, RET_QK_DIM), lambda b, h, c: (x_blocks + b, RET_HEADS + h))
    vc_spec = pl.BlockSpec((ctx_len, RET_V_DIM), lambda b, h, c: (x_blocks + b, RET_HEADS + h))
    st_spec = pl.BlockSpec((1, 1, 1, RET_QK_DIM, RET_V_DIM), lambda b, h, c: (b, h, order(c, nc), 0, 0))
    o_spec = pl.BlockSpec((RET_CHUNK, RET_V_DIM), lambda b, h, c: (rows(b, c), h))
    return nc, q_spec, k_spec, v_spec, kc_spec, vc_spec, st_spec, o_spec


_SCAN_UP = lambda c, nc: c
_SCAN_DOWN = lambda c, nc: nc - 1 - c


def _ret_fwd(qk, qkvg, log_g, B, seq, ctx_len, name):
    nc, qf, kf, vf, kc_spec, vc_spec, stf, of = _ret_specs(B, seq, ctx_len, _SCAN_UP)
    _, qr, kr, vr, _, _, str_, or_ = _ret_specs(B, seq, ctx_len, _SCAN_DOWN)

    def body(lg_ref, qf_ref, kf_ref, vf_ref, qr_ref, kr_ref, vr_ref, kc_ref, vc_ref,
             of_ref, stf_ref, or_ref, str_ref, state_f, state_r):
        h, c = pl.program_id(1), pl.program_id(2)
        dirs = ((False, lg_ref[0, h], qf_ref, kf_ref, vf_ref, of_ref, stf_ref, state_f),
                (True, lg_ref[1, h], qr_ref, kr_ref, vr_ref, or_ref, str_ref, state_r))

        @pl.when(c == 0)
        def _():
            for rev, lg, _, _, _, _, _, state in dirs:
                dec, _ = _ctx_decay(lg, ctx_len, rev)
                state[...] = _dot(kc_ref[...] * dec, vc_ref[...], _TN)

        for rev, lg, q_ref, k_ref, v_ref, o_ref, st_ref, state in dirs:
            intra, _, q_dec, _, k_dec, _, chunk_dec = _ret_decays(lg, rev)
            qv, kv, vv = q_ref[...], k_ref[...], v_ref[...]
            s_in = state[...]
            st_ref[0, 0, 0] = s_in
            w = _dot(qv, kv, _NT) * intra
            o_ref[...] = _dot(w, vv, _NN) + _dot(qv, s_in, _NN) * q_dec
            state[...] = s_in * chunk_dec + _dot(kv * k_dec, vv, _TN)

    o_shape = jax.ShapeDtypeStruct((B * seq, RET_VWIDTH), F32)
    st_shape = jax.ShapeDtypeStruct((B, RET_HEADS, nc, RET_QK_DIM, RET_V_DIM), F32)
    return pl.pallas_call(
        body, name=name, grid=(B, RET_HEADS, nc),
        in_specs=[_SMEM, qf, kf, vf, qr, kr, vr, kc_spec, vc_spec],
        out_specs=(of, stf, or_, str_), out_shape=(o_shape, st_shape, o_shape, st_shape),
        scratch_shapes=[pltpu.VMEM((RET_QK_DIM, RET_V_DIM), F32)] * 2,
        compiler_params=_params())(log_g, qk, qk, qkvg, qk, qk, qkvg, qk, qkvg)


def _ret_bwd_chunk(rev, lg, q_ref, k_ref, v_ref, st_ref, do_ref, dq_ref, dk_ref, dv_ref, dlg_ref, dstate):
    intra, a_exp, q_dec, q_exp, k_dec, k_exp, chunk_dec = _ret_decays(lg, rev)
    qv, kv, vv, dov = q_ref[...], k_ref[...], v_ref[...], do_ref[...]
    s_in, ds_out = st_ref[0, 0, 0], dstate[...]
    p = _dot(qv, kv, _NT)
    w = p * intra
    dw = _dot(dov, vv, _NT)
    dp = dw * intra
    do_dec = dov * q_dec
    kd = kv * k_dec
    v_ds = _dot(vv, ds_out, _NT)
    dq_ref[...] = _dot(dp, kv, _NN) + _dot(do_dec, s_in, _NT)
    dk_ref[...] = _dot(dp, qv, _TN) + v_ds * k_dec
    dv_ref[...] = _dot(w, dov, _TN) + _dot(kd, ds_out, _NN)
    q_s = _dot(qv, s_in, _NN)
    dlg = (jnp.sum(dw * w * a_exp)
           + jnp.sum(q_exp * q_dec * jnp.sum(dov * q_s, axis=-1, keepdims=True))
           + jnp.sum(k_exp * k_dec * jnp.sum(kv * v_ds, axis=-1, keepdims=True))
           + RET_CHUNK * jnp.sum(chunk_dec * (ds_out * s_in)))
    ds_in = ds_out * chunk_dec + _dot(qv, do_dec, _TN)
    dstate[...] = ds_in
    dlg_ref[...] += dlg
    return ds_in


def _ret_bwd(qk, qkvg, log_g, st_f, st_r, do, B, seq, ctx_len, name):
    nc, qf, kf, vf, kc_spec, vc_spec, stf, of = _ret_specs(B, seq, ctx_len, _SCAN_DOWN)
    _, qr, kr, vr, _, _, str_, or_ = _ret_specs(B, seq, ctx_len, _SCAN_UP)

    def body(lg_ref, qf_ref, kf_ref, vf_ref, stf_ref, dof_ref, qr_ref, kr_ref, vr_ref, str_ref, dor_ref, kc_ref, vc_ref,
             dqf, dkf, dvf, dkcf, dvcf, dlgf, dqr, dkr, dvr, dkcr, dvcr, dlgr, dstate_f, dstate_r):
        h, c = pl.program_id(1), pl.program_id(2)
        dirs = ((False, lg_ref[0, h], (qf_ref, kf_ref, vf_ref, stf_ref, dof_ref, dqf, dkf, dvf, dlgf, dstate_f), dkcf, dvcf),
                (True, lg_ref[1, h], (qr_ref, kr_ref, vr_ref, str_ref, dor_ref, dqr, dkr, dvr, dlgr, dstate_r), dkcr, dvcr))

        @pl.when(c == 0)
        def _():
            for _, _, refs, _, _ in dirs:
                refs[-1][...] = jnp.zeros_like(refs[-1])
                refs[-2][...] = jnp.zeros_like(refs[-2])

        ds_first = [_ret_bwd_chunk(rev, lg, *refs) for rev, lg, refs, _, _ in dirs]

        @pl.when(c == nc - 1)
        def _():
            for (rev, lg, refs, dkc_ref, dvc_ref), ds_in in zip(dirs, ds_first):
                dec, expo = _ctx_decay(lg, ctx_len, rev)
                kcv, vcv = kc_ref[...], vc_ref[...]
                vc_ds = _dot(vcv, ds_in, _NT)
                dkc_ref[...] = vc_ds * dec
                dvc_ref[...] = _dot(kcv * dec, ds_in, _NN)
                refs[-2][...] += jnp.sum(expo * dec * jnp.sum(kcv * vc_ds, axis=-1, keepdims=True))

    def outs(q_spec, o_spec):
        return (pl.BlockSpec((RET_CHUNK, RET_QK_DIM), q_spec.index_map),
                pl.BlockSpec((RET_CHUNK, RET_QK_DIM), q_spec.index_map), o_spec,
                pl.BlockSpec((ctx_len, RET_QK_DIM), lambda b, h, c: (b, h)),
                pl.BlockSpec((ctx_len, RET_V_DIM), lambda b, h, c: (b, h)),
                pl.BlockSpec((1, 1, 8, LANES), lambda b, h, c: (b, h, 0, 0)))

    shapes = (jax.ShapeDtypeStruct((B * seq, RET_HEADS * RET_QK_DIM), F32),
              jax.ShapeDtypeStruct((B * seq, RET_HEADS * RET_QK_DIM), F32),
              jax.ShapeDtypeStruct((B * seq, RET_VWIDTH), F32),
              jax.ShapeDtypeStruct((B * ctx_len, RET_HEADS * RET_QK_DIM), F32),
              jax.ShapeDtypeStruct((B * ctx_len, RET_VWIDTH), F32),
              jax.ShapeDtypeStruct((B, RET_HEADS, 8, LANES), F32))
    res = pl.pallas_call(
        body, name=name, grid=(B, RET_HEADS, nc),
        in_specs=[_SMEM, qf, kf, vf, stf, of, qr, kr, vr, str_, or_, kc_spec, vc_spec],
        out_specs=outs(qf, of) + outs(qr, or_), out_shape=shapes + shapes,
        scratch_shapes=[pltpu.VMEM((RET_QK_DIM, RET_V_DIM), F32)] * 2,
        compiler_params=_params())(log_g, qk, qk, qkvg, st_f, do, qk, qk, qkvg, st_r, do, qk, qkvg)
    return res[:6], res[6:]


def _gated_out_fwd(o_f, o_b, qkvg, gn_gain, name):
    T = o_f.shape[0]
    g_off = (2 * RET_HEADS * RET_QK_DIM + RET_VWIDTH) // RET_V_DIM

    def body(of_ref, ob_ref, g_ref, gain_ref, z_ref):
        o = of_ref[...] + ob_ref[...]
        mu = jnp.mean(o, axis=-1, keepdims=True)
        var = jnp.mean(jnp.square(o - mu), axis=-1, keepdims=True)
        y = (o - mu) * lax.rsqrt(var + EPS) * gain_ref[...]
        gv = g_ref[...]
        z_ref[...] = (gv * jax.nn.sigmoid(gv) * y).astype(z_ref.dtype)

    blk = pl.BlockSpec((ROW_TILE, RET_V_DIM), lambda i, h: (i, h))
    return pl.pallas_call(
        body, name=name, grid=(T // ROW_TILE, RET_HEADS),
        in_specs=[blk, blk, pl.BlockSpec((ROW_TILE, RET_V_DIM), lambda i, h: (i, g_off + h)),
                  pl.BlockSpec((1, RET_V_DIM), lambda i, h: (0, h))],
        out_specs=blk, out_shape=jax.ShapeDtypeStruct((T, RET_VWIDTH), MXU_DTYPE),
        compiler_params=_params())(o_f, o_b, qkvg, gn_gain)


def _gated_out_bwd(dz, o_f, o_b, qkvg, gn_gain, name):
    T = o_f.shape[0]
    g_off = (2 * RET_HEADS * RET_QK_DIM + RET_VWIDTH) // RET_V_DIM

    def body(dz_ref, of_ref, ob_ref, g_ref, gain_ref, do_ref, dg_ref, dgain_ref):
        o = of_ref[...] + ob_ref[...]
        mu = jnp.mean(o, axis=-1, keepdims=True)
        var = jnp.mean(jnp.square(o - mu), axis=-1, keepdims=True)
        rstd = lax.rsqrt(var + EPS)
        yhat = (o - mu) * rstd
        gv, dzv = g_ref[...], dz_ref[...]
        sg = jax.nn.sigmoid(gv)
        dg_ref[...] = (dzv * (yhat * gain_ref[...]) * (sg * (1.0 + gv * (1.0 - sg)))).astype(dg_ref.dtype)
        dy = dzv * (gv * sg)

        @pl.when(pl.program_id(1) == 0)
        def _():
            dgain_ref[...] = jnp.zeros_like(dgain_ref)

        dgain_ref[...] += jnp.sum(dy * yhat, axis=0, keepdims=True)
        dyh = dy * gain_ref[...]
        do_ref[...] = rstd * (dyh - jnp.mean(dyh, axis=-1, keepdims=True)
                              - yhat * jnp.mean(dyh * yhat, axis=-1, keepdims=True))

    blk = pl.BlockSpec((ROW_TILE, RET_V_DIM), lambda h, i: (i, h))
    vec = pl.BlockSpec((1, RET_V_DIM), lambda h, i: (0, h))
    return pl.pallas_call(
        body, name=name, grid=(RET_HEADS, T // ROW_TILE),
        in_specs=[blk, blk, blk, pl.BlockSpec((ROW_TILE, RET_V_DIM), lambda h, i: (i, g_off + h)), vec],
        out_specs=(blk, blk, vec),
        out_shape=(jax.ShapeDtypeStruct((T, RET_VWIDTH), F32), jax.ShapeDtypeStruct((T, RET_VWIDTH), MXU_DTYPE),
                   jax.ShapeDtypeStruct((1, RET_VWIDTH), F32)),
        compiler_params=_params())(dz, o_f, o_b, qkvg, gn_gain)


def _adamw(w, m, v, parts, name):
    R, C = w.shape
    tr = _tile(R, (256, 128, 64, 32, 16, 8))
    n_parts = [p.shape[0] for p in parts]

    def body(*refs):
        w_ref, m_ref, v_ref = refs[:3]
        part_refs = refs[3:3 + len(parts)]
        g_ref, d_ref, nm_ref, nv_ref = refs[3 + len(parts):]
        g = None
        for ref, n in zip(part_refs, n_parts):
            for r in range(n):
                term = ref[r].astype(F32)
                g = term if g is None else g + term
        mn = ADAM_B1 * m_ref[...] + (1.0 - ADAM_B1) * g
        vn = ADAM_B2 * v_ref[...] + (1.0 - ADAM_B2) * jnp.square(g)
        m_hat = mn / (1.0 - ADAM_B1 ** ADAM_STEP)
        v_hat = vn / (1.0 - ADAM_B2 ** ADAM_STEP)
        g_ref[...] = g
        d_ref[...] = -ADAM_LR * (m_hat / (jnp.sqrt(v_hat) + ADAM_EPS) + ADAM_WD * w_ref[...])
        nm_ref[...] = mn
        nv_ref[...] = vn

    blk = pl.BlockSpec((tr, C), lambda i: (i, 0))
    part_specs = [pl.BlockSpec((n, tr, C), lambda i: (0, i, 0)) for n in n_parts]
    shp = jax.ShapeDtypeStruct((R, C), F32)
    return pl.pallas_call(
        body, name=name, grid=(R // tr,), in_specs=[blk, blk, blk] + part_specs,
        out_specs=(blk, blk, blk, blk), out_shape=(shp, shp, shp, shp),
        compiler_params=_params())(w, m, v, *parts)


def _sum_rows(parts, name):
    n, R, C = parts.shape
    tr = _tile(R, (256, 128, 64, 32, 16, 8))

    def body(p_ref, o_ref):
        acc = p_ref[0]
        for r in range(1, n):
            acc = acc + p_ref[r]
        o_ref[...] = acc

    return pl.pallas_call(
        body, name=name, grid=(R // tr,), in_specs=[pl.BlockSpec((n, tr, C), lambda i: (0, i, 0))],
        out_specs=pl.BlockSpec((tr, C), lambda i: (i, 0)), out_shape=jax.ShapeDtypeStruct((R, C), F32),
        compiler_params=_params())(parts)


def _my_coords():
    return lax.axis_index("x"), lax.axis_index("y"), lax.axis_index("c")


def _flip(coord, bit):
    return 1 - coord if bit else coord


def _all_gather(x2d, name):
    R, C = x2d.shape

    def body(x_ref, out_ref, send_sems, recv_sems, local_sem):
        x, y, c = _my_coords()
        me, sibling = (x, y, c), (x, y, 1 - c)
        chips = [(1 - x, y), (x, 1 - y), (1 - x, 1 - y)]

        def rows(px, py, pc):
            return out_ref.at[4 * px + 2 * py + pc]

        def copy(k, block, to, src=None):
            return pltpu.make_async_remote_copy(
                src_ref=rows(*block) if src is None else src, dst_ref=rows(*block),
                send_sem=send_sems.at[k], recv_sem=recv_sems.at[k], device_id=to, device_id_type=MESH)

        mine = pltpu.make_async_copy(x_ref, rows(*me), local_sem)
        mine.start()
        first = [copy(0, me, sibling, src=x_ref)]
        first += [copy(1 + j, me, (*chip, c), src=x_ref) for j, chip in enumerate(chips)]
        for cp in first:
            cp.start()
        passed = [copy(4 + j, (*chip, c), sibling) for j, chip in enumerate(chips)]
        for j, chip in enumerate(chips):
            copy(1 + j, (*chip, c), me).wait_recv()
            passed[j].start()
        copy(0, sibling, me).wait_recv()
        for j, chip in enumerate(chips):
            copy(4 + j, (*chip, 1 - c), me).wait_recv()
        for cp in first + passed:
            cp.wait_send()
        mine.wait()

    return pl.pallas_call(
        body, name=name, out_shape=jax.ShapeDtypeStruct((N_DEV, R, C), x2d.dtype),
        in_specs=[_ANY], out_specs=_ANY,
        scratch_shapes=[pltpu.SemaphoreType.DMA((7,)), pltpu.SemaphoreType.DMA((7,)), pltpu.SemaphoreType.DMA],
    )(x2d)


BIG_WEIGHTS = {
    "ffn_w_in": (2, (2, D_MODEL, 2 * D_FF)),
    "ffn_w_out": (1, (2, D_FF, D_MODEL)),
    "attn_w_qkv": (2, (1, D_MODEL, (N_HEADS + 2 * N_KV_HEADS) * HEAD_DIM)),
    "attn_w_o": (1, (1, N_HEADS * HEAD_DIM, D_MODEL)),
    "ret_w_qkvg": (2, (1, D_MODEL, 2 * D_MODEL + 2 * RET_VWIDTH)),
    "ret_gn_g": (2, (1, 1, RET_VWIDTH)),
    "ret_w_o": (1, (1, RET_VWIDTH, D_MODEL)),
}


def _join_shards(name, stacked):
    axis, full = BIG_WEIGHTS[name]
    if axis == 2:
        stacked = stacked.transpose(0, 2, 1, 3)
    return stacked.reshape(full)


def _split_shards(name, full_arr):
    axis, (_, rows, cols) = BIG_WEIGHTS[name]
    L = full_arr.shape[0]
    if axis == 2:
        return full_arr.reshape(L, rows, N_DEV, cols // N_DEV).transpose(0, 2, 1, 3)
    return full_arr.reshape(L, N_DEV, rows // N_DEV, cols)


def _gather_shards(shards, name):
    n = len(shards)

    def body(*refs):
        x_refs, out_refs = refs[:n], refs[n:2 * n]
        send_sems, recv_sems, local_sems = refs[2 * n:]
        x, y, c = _my_coords()
        me, sibling = (x, y, c), (x, y, 1 - c)
        chips = [(1 - x, y), (x, 1 - y), (1 - x, 1 - y)]

        def rows(a, px, py, pc):
            return out_refs[a].at[:, 4 * px + 2 * py + pc]

        def copy(a, k, block, to, src=None):
            return pltpu.make_async_remote_copy(
                src_ref=rows(a, *block) if src is None else src, dst_ref=rows(a, *block),
                send_sem=send_sems.at[7 * a + k], recv_sem=recv_sems.at[7 * a + k], device_id=to, device_id_type=MESH)

        mine = [pltpu.make_async_copy(x_refs[a], rows(a, *me), local_sems.at[a]) for a in range(n)]
        for cp in mine:
            cp.start()
        first = []
        for a in range(n):
            first.append(copy(a, 0, me, sibling, src=x_refs[a]))
            first += [copy(a, 1 + j, me, (*chip, c), src=x_refs[a]) for j, chip in enumerate(chips)]
        for cp in first:
            cp.start()
        passed = []
        for j, chip in enumerate(chips):
            for a in range(n):
                copy(a, 1 + j, (*chip, c), me).wait_recv()
                fwd = copy(a, 4 + j, (*chip, c), sibling)
                fwd.start()
                passed.append(fwd)
        for a in range(n):
            copy(a, 0, sibling, me).wait_recv()
            for j, chip in enumerate(chips):
                copy(a, 4 + j, (*chip, 1 - c), me).wait_recv()
        for cp in first + passed:
            cp.wait_send()
        for cp in mine:
            cp.wait()

    return pl.pallas_call(
        body, name=name,
        out_shape=[jax.ShapeDtypeStruct((s.shape[0], N_DEV) + s.shape[1:], s.dtype) for s in shards],
        in_specs=[_ANY] * n, out_specs=[_ANY] * n,
        scratch_shapes=[pltpu.SemaphoreType.DMA((7 * n,)), pltpu.SemaphoreType.DMA((7 * n,)),
                        pltpu.SemaphoreType.DMA((n,))],
    )(*shards)


def _exchange_shards(arrs, masks, src_of, out_tail, name):
    n, nm = len(arrs), len(masks)

    def body(*refs):
        in_refs, out_refs = refs[:n], refs[n:2 * n]
        send_sems, recv_sems = refs[2 * n:]
        x, y, c = _my_coords()
        copies = []
        for a in range(n):
            for k, (bx, by, bc) in enumerate(masks):
                peer = (_flip(x, bx), _flip(y, by), _flip(c, bc))
                copies.append(pltpu.make_async_remote_copy(
                    src_ref=src_of(in_refs[a], peer, (x, y, c)), dst_ref=out_refs[a].at[k],
                    send_sem=send_sems.at[nm * a + k], recv_sem=recv_sems.at[nm * a + k],
                    device_id=peer, device_id_type=MESH))
        for cp in copies:
            cp.start()
        for cp in copies:
            cp.wait()

    return pl.pallas_call(
        body, name=name,
        out_shape=[jax.ShapeDtypeStruct((nm,) + out_tail(s), s.dtype) for s in arrs],
        in_specs=[_ANY] * n, out_specs=[_ANY] * n,
        scratch_shapes=[pltpu.SemaphoreType.DMA((nm * n,)), pltpu.SemaphoreType.DMA((nm * n,))],
    )(*arrs)


def _pair_sum(g, from_sibling, core, out_dtype, name):
    L, _, _, a, b = g.shape
    ta = a

    def body(core_ref, g_ref, s_ref, o_ref):
        o_ref[...] = (g_ref[...] + s_ref[...]).astype(out_dtype)

    blk = pl.BlockSpec((1, 1, ta, b), lambda l, q, i, core_ref: (l, q, i, 0))
    return pl.pallas_call(
        body, name=name,
        grid_spec=pltpu.PrefetchScalarGridSpec(
            num_scalar_prefetch=1, grid=(L, 4, a // ta),
            in_specs=[pl.BlockSpec((1, 1, pl.Squeezed(), ta, b), lambda l, q, i, core_ref: (l, q, core_ref[0], i, 0)), blk],
            out_specs=blk),
        out_shape=jax.ShapeDtypeStruct((L, 4, a, b), out_dtype), compiler_params=_params())(core, g, from_sibling)


def _mods(mod_x, mod_c, layer):
    both = jnp.concatenate([mod_x[:, layer], mod_c[layer][None]], axis=0)
    return [both[:, None, k * D_MODEL:(k + 1) * D_MODEL] for k in range(6)]


def _local_step(x, ctx, target, mod_x, mod_c, w, small, late_weights=None, hooks=None):
    B, S, _ = x.shape
    L = ctx.shape[1]
    NX, NC = B * S, B * L
    T = NX + NC
    tiles_per_ex = S // ROW_TILE
    nxt = NX // ROW_TILE
    gidx = _group_index(nxt, tiles_per_ex, B)
    gidx_for = lambda rows: _group_index(NX // rows, S // rows, B)
    mm_rows = _tile(S, (MM_ROWS, ROW_TILE))
    tidx = lambda i: jnp.where(i < nxt, i % tiles_per_ex, tiles_per_ex)
    G = B + 1
    x0 = jnp.concatenate([x.reshape(NX, D_MODEL), ctx.reshape(NC, D_MODEL)], axis=0)
    acos, asin = [jnp.tile(t, (1, LANES // HEAD_DIM)) for t in _rope_tables(S, HEAD_DIM)]
    rcos, rsin = _rope_tables(S, RET_QK_DIM)
    sink = small["attn_sink"].reshape(N_KV_HEADS, GQA_GROUP)
    gains = jnp.stack([jnp.tile(small["attn_q_norm"].reshape(1, HEAD_DIM), (1, LANES // HEAD_DIM)),
                       jnp.tile(small["attn_k_norm"].reshape(1, HEAD_DIM), (1, LANES // HEAD_DIM))])
    log_g = jax.nn.log_sigmoid(small["ret_decay_logit"].reshape(2, RET_HEADS))
    n1, n2 = small["norm1_g"], small["norm2_g"]

    m0 = _mods(mod_x, mod_c, 0)
    h1 = _norm_mod_fwd(x0, n1[0:1], m0[0], m0[1], gidx, "l0_norm1")
    qkv = _mm(h1, w["attn_w_qkv"][0], "nn", F32, "l0_qkv")
    qkv_r = _attn_prep_fwd(qkv, gains, acos, asin, tidx, "l0_qk_prep")
    o_x = _attn_fwd(qkv_r, sink, B, S, L, False, "l0_attn_x")
    o_c = _attn_fwd(qkv_r, sink, B, S, L, True, "l0_attn_c")
    o0 = jnp.concatenate([o_x, o_c], axis=0)
    mo0, x1 = _mm(o0, w["attn_w_o"][0], "nn", F32, "l0_attn_out", res=x0, gate=m0[2], gidx_for=gidx_for, gate_rows=mm_rows)
    h2 = _norm_mod_fwd(x1, n2[0:1], m0[3], m0[4], gidx, "l0_norm2")
    if late_weights is not None:
        w = {**w, **late_weights(x1)}
    ug0, uu0, a0 = _ffn_in_swiglu(h2, w["ffn_w_in"][0], "l0_ffn_in")
    f0, x2 = _mm(a0, w["ffn_w_out"][0], "nn", F32, "l0_ffn_out", res=x1, gate=m0[5], gidx_for=gidx_for, gate_rows=mm_rows)

    m1 = _mods(mod_x, mod_c, 1)
    g1 = _norm_mod_fwd(x2, n1[1:2], m1[0], m1[1], gidx, "l1_norm1")
    qkvg = _mm(g1, w["ret_w_qkvg"][0], "nn", F32, "l1_qkvg")
    qk = _ret_rope(qkvg, rcos, rsin, tidx, "l1_rope")
    of, st_f, ob, st_b = _ret_fwd(qk, qkvg, log_g, B, S, L, "l1_ret")
    gn = w["ret_gn_g"].reshape(1, RET_VWIDTH)
    z1 = _gated_out_fwd(of, ob, qkvg, gn, "l1_gated_out")
    gx = lambda i: i // tiles_per_ex
    m1x = [t[:B] for t in m1]
    mo1, y1 = _mm(z1, w["ret_w_o"][0], "nn", F32, "l1_ret_out", res=x2, gate=m1x[2], gidx_for=gidx_for, gate_rows=mm_rows)
    k2 = _norm_mod_fwd(y1, n2[1:2], m1x[3], m1x[4], gx, "l1_norm2")
    ug1, uu1, a1 = _ffn_in_swiglu(k2, w["ffn_w_in"][1], "l1_ffn_in")
    f1, y2 = _mm(a1, w["ffn_w_out"][1], "nn", F32, "l1_ffn_out", res=y1, gate=m1x[5], gidx_for=gidx_for, gate_rows=mm_rows)

    loss_tile, dy2, dz, dgate5_1 = _loss_fwd_bwd(y2, target.reshape(NX, D_MODEL), f1, m1x[5], gx, B, "loss")

    zg = jnp.zeros((1, 1, D_MODEL), F32)
    gw_ffn_out1 = _mm(a1, dz, "tn", F32, "l1_ffn_out_dw")
    du = _ffn_out_bwd_swiglu(dz, w["ffn_w_out"][1], ug1, uu1, "l1_ffn_out_dx")
    gw_ffn_in1 = _mm(k2, du, "tn", F32, "l1_ffn_in_dw")
    dk2 = _mm(du, w["ffn_w_in"][1], "nt", F32, "l1_ffn_in_dx")
    dy1, dsh3_1, dsc4_1, dn2_1, dzo, dgate2_1 = _norm_mod_bwd(dk2, y1, n2[1:2], m1x[4], dy2, gx, B, "l1_norm2_bwd",
                                                              gated=(mo1, m1x[2]))
    gw_ret_o = _mm(z1, dzo, "tn", F32, "l1_ret_out_dw")
    dz1 = _mm(dzo, w["ret_w_o"][0], "nt", F32, "l1_ret_out_dx")
    do_r, dg_r, dgn = _gated_out_bwd(dz1, of, ob, qkvg, gn, "l1_gated_out_bwd")
    ((dq_f, dk_f, dv_f, dkc_f, dvc_f, dlg_f),
     (dq_b, dk_b, dv_b, dkc_b, dvc_b, dlg_b)) = _ret_bwd(qk, qkvg, log_g, st_f, st_b, do_r, B, S, L, "l1_ret_bwd")
    dqkvg = _ret_grad_assemble((dq_f, dq_b, dk_f, dk_b, dv_f, dv_b), (dkc_f, dkc_b, dvc_f, dvc_b), dg_r, rcos, rsin, S,
                               "l1_qkvg_grad")
    gw_ret_qkvg = _mm(g1, dqkvg, "tn", F32, "l1_qkvg_dw")
    grads_layer1 = {
        "ffn_w_in": gw_ffn_in1[None],
        "ffn_w_out": gw_ffn_out1[None],
        "ret_w_qkvg": gw_ret_qkvg[None],
        "ret_gn_g": dgn.reshape(1, 1, RET_VWIDTH),
        "ret_w_o": gw_ret_o[None],
    }
    if hooks is not None:
        m0[5] = hooks.layer1_grads(grads_layer1, m0[5])
    dg1 = _mm(dqkvg, w["ret_w_qkvg"][0], "nt", F32, "l1_qkvg_dx")
    dx2, dsh0_1, dsc1_1, dn1_1, dz, dgate5_0 = _norm_mod_bwd(dg1, x2, n1[1:2], m1[1], dy1, gidx, G, "l1_norm1_bwd",
                                                             gated=(f0, m0[5]))
    dlg = jnp.stack([jnp.sum(dlg_f[:, :, 0, 0], axis=0), jnp.sum(dlg_b[:, :, 0, 0], axis=0)])
    d_decay = (dlg * jax.nn.sigmoid(-small["ret_decay_logit"].reshape(2, RET_HEADS))).reshape(1, 2, RET_HEADS)

    gw_ffn_out0 = _mm(a0, dz, "tn", F32, "l0_ffn_out_dw")
    du = _ffn_out_bwd_swiglu(dz, w["ffn_w_out"][0], ug0, uu0, "l0_ffn_out_dx")
    if hooks is not None:
        m0[4] = hooks.mid_ffn0_backward(du, m0[4])
    gw_ffn_in0 = _mm(h2, du, "tn", F32, "l0_ffn_in_dw")
    if hooks is not None:
        m0[2] = hooks.ffn0_grads({"ffn_w_in": gw_ffn_in0[None], "ffn_w_out": gw_ffn_out0[None]}, m0[2])
    dh2 = _mm(du, w["ffn_w_in"][0], "nt", F32, "l0_ffn_in_dx")
    dx1, dsh3_0, dsc4_0, dn2_0, dzo, dgate2_0 = _norm_mod_bwd(dh2, x1, n2[0:1], m0[4], dx2, gidx, G, "l0_norm2_bwd",
                                                              gated=(mo0, m0[2]))
    gw_attn_o = _mm(o0, dzo, "tn", F32, "l0_attn_out_dw")
    do0 = _mm(dzo, w["attn_w_o"][0], "nt", MXU_DTYPE, "l0_attn_out_dx")
    dq_x, dk_x, dv_x, dkc1, dvc1, dsink_x = _attn_bwd(qkv_r, sink, do0, B, S, L, False, "l0_attn_x_bwd")
    dq_c, dkc2, dvc2, dsink_c = _attn_bwd(qkv_r, sink, do0, B, S, L, True, "l0_attn_c_bwd")
    if hooks is not None:
        gains = hooks.after_attn_backward(dq_x, gains)
    dqkv, dgains = _attn_prep_bwd((dq_x, dk_x, dv_x), (dq_c, dkc1, dkc2, dvc1, dvc2), qkv, gains, acos, asin, tidx,
                                  "l0_qk_prep_bwd")
    gw_attn_qkv = _mm(h1, dqkv, "tn", F32, "l0_qkv_dw")
    dh1 = _mm(dqkv, w["attn_w_qkv"][0], "nt", F32, "l0_qkv_dx")
    dx0, dsh0_0, dsc1_0, dn1_0 = _norm_mod_bwd(dh1, x0, n1[0:1], m0[1], dx1, gidx, G, "l0_norm1_bwd", dx_rows=NX)

    dgains = jnp.sum(dgains.reshape(ATTN_QK_BLOCKS, LANES // HEAD_DIM, HEAD_DIM), axis=1)
    dsink = (dsink_x + dsink_c).reshape(N_KV_HEADS, 8, LANES)[:, :GQA_GROUP, 0].reshape(1, N_HEADS)
    grads_layer0 = {
        "ffn_w_in": gw_ffn_in0[None],
        "ffn_w_out": gw_ffn_out0[None],
        "attn_w_qkv": gw_attn_qkv[None],
        "attn_w_o": gw_attn_o[None],
    }
    grads_small = {
        "norm1_g": jnp.concatenate([dn1_0, dn1_1], axis=0),
        "norm2_g": jnp.concatenate([dn2_0, dn2_1], axis=0),
        "attn_q_norm": jnp.sum(dgains[:ATTN_Q_BLOCKS], axis=0)[None],
        "attn_k_norm": jnp.sum(dgains[ATTN_Q_BLOCKS:ATTN_QK_BLOCKS], axis=0)[None],
        "attn_sink": dsink,
        "ret_decay_logit": d_decay,
    }

    def pad_g(t):
        return jnp.concatenate([t, zg], axis=0)

    d0 = jnp.concatenate([dsh0_0, dsc1_0, dgate2_0, dsh3_0, dsc4_0, dgate5_0], axis=2)[:, 0]
    d1 = jnp.concatenate([dsh0_1, dsc1_1, pad_g(dgate2_1), pad_g(dsh3_1), pad_g(dsc4_1), pad_g(dgate5_1)],
                         axis=2)[:, 0]
    dmod_x = jnp.stack([d0[:B], d1[:B]], axis=1)
    dmod_c = jnp.stack([d0[B], d1[B]], axis=0)
    return loss_tile, dx0.reshape(B, S, D_MODEL), (grads_layer0, grads_layer1), grads_small, dmod_x, dmod_c


SMALL_NAMES = ("c_ctx", "ada_b", "norm1_g", "norm2_g", "attn_q_norm", "attn_k_norm", "attn_sink", "ret_decay_logit")
ADA_ROWS = 64


def _pack_small(d, rows):
    flat = jnp.concatenate([d[k].reshape(-1) for k in SMALL_NAMES])
    n = rows * LANES
    return jnp.pad(flat, (0, n - flat.shape[0])).reshape(rows, LANES)


def _unpack_small(packed, shapes):
    flat = packed.reshape(-1)
    out, off = {}, 0
    for k in SMALL_NAMES:
        n = math.prod(shapes[k])
        out[k] = flat[off:off + n].reshape(shapes[k])
        off += n
    return out


EARLY_WEIGHTS = ("attn_w_qkv", "attn_w_o")
LATE_WEIGHTS = tuple(k for k in BIG_WEIGHTS if k not in EARLY_WEIGHTS)

_HBM = pl.BlockSpec(memory_space=pltpu.HBM)
_SEM = pl.BlockSpec(memory_space=pltpu.SEMAPHORE)
_DATAFLOW = pltpu.SideEffectType.DATAFLOW_SIDE_EFFECTING
_PEER_FLIPS = ((0, 0, 1), (0, 1, 0), (0, 1, 1), (1, 0, 0), (1, 0, 1), (1, 1, 0), (1, 1, 1))


def _wire_shard(name, t):
    return t.reshape(1, 1, -1) if name == "ret_gn_g" else t.astype(MXU_DTYPE)


def _direct_copies(x_refs, land_refs, send_sems, recv_sems, landing):
    x, y, c = _my_coords()
    out = []
    for a in range(len(x_refs)):
        for k, (bx, by, bc) in enumerate(_PEER_FLIPS):
            peer = (_flip(x, bx), _flip(y, by), _flip(c, bc))
            slot = (4 * peer[0] + 2 * peer[1] + peer[2]) if landing else (4 * x + 2 * y + c)
            out.append(pltpu.make_async_remote_copy(
                src_ref=x_refs[a], dst_ref=land_refs[a].at[:, slot], send_sem=send_sems.at[7 * a + k],
                recv_sem=recv_sems.at[7 * a + k], device_id=peer, device_id_type=MESH))
    return out


def _gather_start(shards, name):
    n = len(shards)
    lands = [lax.empty((s.shape[0], N_DEV) + s.shape[1:], s.dtype) for s in shards]

    def body(*refs):
        send_sems, recv_sems = refs[2 * n], refs[2 * n + 1]
        x_refs, land_refs = refs[2 * n + 2:3 * n + 2], refs[3 * n + 2:4 * n + 2]
        for cp in _direct_copies(x_refs, land_refs, send_sems, recv_sems, landing=False):
            cp.start()
        refs[-1][...] = jnp.zeros_like(refs[-1])

    hbm = lambda t: pltpu.with_memory_space_constraint(t, pltpu.HBM)
    res = pl.pallas_call(
        body, name=name,
        out_shape=(pltpu.SemaphoreType.DMA((7 * n,)), pltpu.SemaphoreType.DMA((7 * n,)))
        + tuple(pltpu.HBM(t.shape, t.dtype) for t in shards + lands) + (jax.ShapeDtypeStruct((8, LANES), F32),),
        in_specs=[_HBM] * (2 * n), out_specs=(_SEM, _SEM) + (_HBM,) * (2 * n) + (pl.BlockSpec(memory_space=pltpu.VMEM),),
        input_output_aliases={i: 2 + i for i in range(2 * n)},
        compiler_params=pltpu.CompilerParams(has_side_effects=_DATAFLOW))(*[hbm(t) for t in shards + lands])
    return res[0], res[1], list(res[2:2 + n]), list(res[2 + n:2 + 2 * n]), res[-1]


def _gather_wait(send_sems, recv_sems, shards, lands, after, name):
    n = len(shards)

    def body(*refs):
        x_refs, land_refs = refs[:n], refs[n:2 * n]
        for cp in _direct_copies(x_refs, land_refs, refs[2 * n], refs[2 * n + 1], landing=True):
            cp.wait_send()
            cp.wait_recv()

    res = pl.pallas_call(
        body, name=name, out_shape=tuple(pltpu.HBM(t.shape, t.dtype) for t in shards + lands),
        in_specs=[_HBM] * (2 * n) + [_SEM, _SEM, _ANY], out_specs=(_HBM,) * (2 * n),
        input_output_aliases={i: i for i in range(2 * n)},
        compiler_params=pltpu.CompilerParams(has_side_effects=_DATAFLOW))(*shards, *lands, send_sems, recv_sems, after)
    return list(res[n:])


def _gather_big_weights(weights, names, name):
    gathered = _gather_shards([_wire_shard(k, weights[k]) for k in names], name)
    return {k: _join_shards(k, g) for k, g in zip(names, gathered)}


_SIBLING = ((0, 0, 1),)
_CHIPS = ((1, 0, 0), (0, 1, 0), (1, 1, 0))
_to_sibling = lambda ref, peer: ref.at[:, :, peer[2]]
_to_chip = lambda ref, peer: ref.at[:, 2 * peer[0] + peer[1]]
_sibling_tail = lambda s: (s.shape[0], 4) + s.shape[3:]
_chip_tail = lambda s: (s.shape[0],) + s.shape[2:]


def _rs_split(grads):
    names = list(grads)
    split = []
    for k in names:
        s = _split_shards(k, grads[k])
        split.append(s.reshape(s.shape[0], 4, 2, s.shape[2], s.shape[3]))
    return names, split


def _rs_pair_sums(names, split, from_sibling, tag):
    core = lax.axis_index("c").astype(jnp.int32).reshape(1)
    return [_pair_sum(g, s, core, MXU_DTYPE, tag + k) for k, g, s in zip(names, split, from_sibling)]


def _rs_parts(names, split, from_sibling, from_chips):
    mx_, my_, mc_ = _my_coords()
    my_chip = 2 * mx_ + my_
    parts = {}
    for k, g, s, r in zip(names, split, from_sibling, from_chips):
        own_keep = lax.dynamic_index_in_dim(lax.dynamic_index_in_dim(g, my_chip, axis=1, keepdims=False), mc_, axis=1,
                                            keepdims=False)
        parts[k] = (own_keep, lax.dynamic_index_in_dim(s, my_chip, axis=1, keepdims=False), r)
    return parts


def _reduce_scatter_in_call(grads, tag):
    names, split = _rs_split(grads)
    from_sibling = [t[0] for t in _exchange_shards(split, _SIBLING, lambda ref, peer, me_: _to_sibling(ref, peer),
                                                   _sibling_tail, tag + "sibling")]
    pair = _rs_pair_sums(names, split, from_sibling, tag + "pair_")
    from_chips = _exchange_shards(pair, _CHIPS, lambda ref, peer, me_: _to_chip(ref, peer), _chip_tail, tag + "chips")
    return _rs_parts(names, split, from_sibling, from_chips)


def _exchange_copies(in_refs, land_refs, send_sems, recv_sems, masks, src_of):
    x, y, c = _my_coords()
    nm = len(masks)
    out = []
    for a in range(len(in_refs)):
        for k, (bx, by, bc) in enumerate(masks):
            peer = (_flip(x, bx), _flip(y, by), _flip(c, bc))
            out.append(pltpu.make_async_remote_copy(
                src_ref=src_of(in_refs[a], peer), dst_ref=land_refs[a].at[k], send_sem=send_sems.at[nm * a + k],
                recv_sem=recv_sems.at[nm * a + k], device_id=peer, device_id_type=MESH))
    return out


def _exchange_start(arrs, masks, src_of, out_tail, name):
    n, nm = len(arrs), len(masks)
    lands = [lax.empty((nm,) + out_tail(s), s.dtype) for s in arrs]

    def body(*refs):
        send_sems, recv_sems = refs[2 * n], refs[2 * n + 1]
        in_refs, land_refs = refs[2 * n + 2:3 * n + 2], refs[3 * n + 2:4 * n + 2]
        for cp in _exchange_copies(in_refs, land_refs, send_sems, recv_sems, masks, src_of):
            cp.start()
        refs[-1][...] = jnp.zeros_like(refs[-1])

    hbm = lambda t: pltpu.with_memory_space_constraint(t, pltpu.HBM)
    res = pl.pallas_call(
        body, name=name,
        out_shape=(pltpu.SemaphoreType.DMA((nm * n,)), pltpu.SemaphoreType.DMA((nm * n,)))
        + tuple(pltpu.HBM(t.shape, t.dtype) for t in list(arrs) + lands) + (jax.ShapeDtypeStruct((8, LANES), F32),),
        in_specs=[_HBM] * (2 * n), out_specs=(_SEM, _SEM) + (_HBM,) * (2 * n) + (pl.BlockSpec(memory_space=pltpu.VMEM),),
        input_output_aliases={i: 2 + i for i in range(2 * n)},
        compiler_params=pltpu.CompilerParams(has_side_effects=_DATAFLOW))(*[hbm(t) for t in list(arrs) + lands])
    return (res[0], res[1], list(res[2:2 + n]), list(res[2 + n:2 + 2 * n]), masks, src_of), res[-1]


def _exchange_wait(state, after, name):
    send_sems, recv_sems, arrs, lands, masks, src_of = state
    n = len(arrs)

    def body(*refs):
        for cp in _exchange_copies(refs[:n], refs[n:2 * n], refs[2 * n], refs[2 * n + 1], masks, src_of):
            cp.wait_send()
            cp.wait_recv()

    res = pl.pallas_call(
        body, name=name, out_shape=tuple(pltpu.HBM(t.shape, t.dtype) for t in arrs + lands),
        in_specs=[_HBM] * (2 * n) + [_SEM, _SEM, _ANY], out_specs=(_HBM,) * (2 * n),
        input_output_aliases={i: i for i in range(2 * n)},
        compiler_params=pltpu.CompilerParams(has_side_effects=_DATAFLOW))(*arrs, *lands, send_sems, recv_sems, after)
    return list(res[:n]), list(res[n:])


class _SplitReduce:
    def __init__(self, tag):
        self.tag = tag

    def start(self, grads, order_through):
        self.names, split = _rs_split(grads)
        self.sibling, tok = _exchange_start(split, _SIBLING, _to_sibling, _sibling_tail, self.tag + "sibling_start")
        return order_through + tok[0, 0]

    def middle(self, after, order_through):
        self.split, lands = _exchange_wait(self.sibling, after, self.tag + "sibling_wait")
        self.from_sibling = [t[0] for t in lands]
        pair = _rs_pair_sums(self.names, self.split, self.from_sibling, self.tag + "pair_")
        self.chips, tok = _exchange_start(pair, _CHIPS, _to_chip, _chip_tail, self.tag + "chips_start")
        return order_through + tok[0, 0]

    def finish(self, after):
        _, from_chips = _exchange_wait(self.chips, after, self.tag + "chips_wait")
        return _rs_parts(self.names, self.split, self.from_sibling, from_chips)


def _adamw_big(weights, mom1, mom2, part_groups):
    big = {}
    for k in BIG_WEIGHTS:
        parts = [g[k] for g in part_groups if k in g]
        own_keep = jnp.concatenate([p[0] for p in parts], axis=0)
        own_sib = jnp.concatenate([p[1] for p in parts], axis=0)
        recv = jnp.concatenate([p[2] for p in parts], axis=1)
        L_, a_, b_ = own_keep.shape
        rows = L_ * a_
        res = _adamw(weights[k].reshape(rows, b_), mom1[k].reshape(rows, b_), mom2[k].reshape(rows, b_),
                     [own_keep.reshape(1, rows, b_), own_sib.reshape(1, rows, b_), recv.reshape(3, rows, b_)],
                     "adamw_" + k)
        big[k] = [t.reshape(weights[k].shape) for t in res]
    return big


def kernel(x, c, ctx, c_ctx, ada_w, ada_b, norm1_g, norm2_g, ffn_w_in, ffn_w_out, attn_w_qkv, attn_q_norm, attn_k_norm, attn_sink, attn_w_o, ret_w_qkvg, ret_decay_logit, ret_gn_g, ret_w_o, loss_target, m_c_ctx, m_ada_w, m_ada_b, m_norm1_g, m_norm2_g, m_ffn_w_in, m_ffn_w_out, m_attn_w_qkv, m_attn_q_norm, m_attn_k_norm, m_attn_sink, m_attn_w_o, m_ret_w_qkvg, m_ret_decay_logit, m_ret_gn_g, m_ret_w_o, v_c_ctx, v_ada_w, v_ada_b, v_norm1_g, v_norm2_g, v_ffn_w_in, v_ffn_w_out, v_attn_w_qkv, v_attn_q_norm, v_attn_k_norm, v_attn_sink, v_attn_w_o, v_ret_w_qkvg, v_ret_decay_logit, v_ret_gn_g, v_ret_w_o):
    weights = dict(c_ctx=c_ctx, ada_w=ada_w, ada_b=ada_b, norm1_g=norm1_g, norm2_g=norm2_g, ffn_w_in=ffn_w_in,
                   ffn_w_out=ffn_w_out, attn_w_qkv=attn_w_qkv, attn_q_norm=attn_q_norm, attn_k_norm=attn_k_norm,
                   attn_sink=attn_sink, attn_w_o=attn_w_o, ret_w_qkvg=ret_w_qkvg, ret_decay_logit=ret_decay_logit,
                   ret_gn_g=ret_gn_g, ret_w_o=ret_w_o)
    mom1 = dict(c_ctx=m_c_ctx, ada_w=m_ada_w, ada_b=m_ada_b, norm1_g=m_norm1_g, norm2_g=m_norm2_g, ffn_w_in=m_ffn_w_in,
                ffn_w_out=m_ffn_w_out, attn_w_qkv=m_attn_w_qkv, attn_q_norm=m_attn_q_norm, attn_k_norm=m_attn_k_norm,
                attn_sink=m_attn_sink, attn_w_o=m_attn_w_o, ret_w_qkvg=m_ret_w_qkvg, ret_decay_logit=m_ret_decay_logit,
                ret_gn_g=m_ret_gn_g, ret_w_o=m_ret_w_o)
    mom2 = dict(c_ctx=v_c_ctx, ada_w=v_ada_w, ada_b=v_ada_b, norm1_g=v_norm1_g, norm2_g=v_norm2_g, ffn_w_in=v_ffn_w_in,
                ffn_w_out=v_ffn_w_out, attn_w_qkv=v_attn_w_qkv, attn_q_norm=v_attn_q_norm, attn_k_norm=v_attn_k_norm,
                attn_sink=v_attn_sink, attn_w_o=v_attn_w_o, ret_w_qkvg=v_ret_w_qkvg, ret_decay_logit=v_ret_decay_logit,
                ret_gn_g=v_ret_gn_g, ret_w_o=v_ret_w_o)
    B = x.shape[0]
    mx_, my_, mc_ = _my_coords()
    me = 4 * mx_ + 2 * my_ + mc_
    ada_cols = ada_w.shape[2]

    w_full = _gather_big_weights(weights, EARLY_WEIGHTS, "gather_early")

    c_all = _all_gather(jax.nn.silu(c), "gather_c").reshape(N_DEV * B, D_MODEL)
    cc_act = jax.nn.silu(c_ctx)[None]
    ada_in = jnp.concatenate([c_all, cc_act, jnp.zeros((ADA_ROWS - N_DEV * B - 1, D_MODEL), F32)], axis=0)
    ada_in = ada_in.astype(MXU_DTYPE)
    ada_w2 = jnp.concatenate([ada_w[0], ada_w[1]], axis=1)
    bias = lax.dynamic_slice_in_dim(ada_b.reshape(2, N_DEV, ada_cols), me, 1, axis=1).reshape(1, 2 * ada_cols)
    mod_cols = _mm(ada_in, ada_w2, "nn", F32, "ada_fwd", bias=bias)
    mod_all = _all_gather(mod_cols, "gather_mod")
    mod_all = mod_all.reshape(N_DEV, ADA_ROWS, 2, ada_cols).transpose(1, 2, 0, 3).reshape(ADA_ROWS, 2, N_DEV * ada_cols)
    mod_x = lax.dynamic_slice_in_dim(mod_all, me * B, B, axis=0)
    mod_c = mod_all[N_DEV * B]

    order = 0.0 * (mod_c[0, 0] + w_full["attn_w_o"][0, 0, 0].astype(F32))
    late_shards = [_wire_shard(k, weights[k] + order if k == "ret_gn_g" else weights[k]) for k in LATE_WEIGHTS]
    send_sems, recv_sems, late_thru, late_lands, token = _gather_start(late_shards, "gather_late_start")
    mod_x = mod_x + token[0, 0]

    def late_weights(after):
        lands = _gather_wait(send_sems, recv_sems, late_thru, late_lands, after, "gather_late_wait")
        own = [lax.dynamic_update_index_in_dim(land, shard, me, axis=1) for land, shard in zip(lands, late_shards)]
        return {k: _join_shards(k, g) for k, g in zip(LATE_WEIGHTS, own)}

    rs_layer1, rs_ffn0 = _SplitReduce("rs1_"), _SplitReduce("rs0_")

    class Hooks:
        layer1_grads = rs_layer1.start
        mid_ffn0_backward = rs_layer1.middle
        ffn0_grads = rs_ffn0.start
        after_attn_backward = rs_ffn0.middle

    small = {k: weights[k] for k in SMALL_NAMES}
    loss_tile, grad_x, (g_layer0, _), g_small, dmod_x, dmod_c = _local_step(
        x, ctx, loss_target, mod_x, mod_c, w_full, small, late_weights, Hooks)
    parts1 = rs_layer1.finish(grad_x)
    parts0_ffn = rs_ffn0.finish(grad_x)
    loss = lax.psum(loss_tile[0, 0], ("x", "y", "c"))

    n_mod = 2 * 6 * D_MODEL
    dm_rows = jnp.concatenate([dmod_x.reshape(B, n_mod), dmod_c.reshape(1, n_mod),
                               jnp.zeros((8 - B - 1, n_mod), F32)], axis=0)
    dm_all = _all_gather(dm_rows, "gather_dmod")
    dmc_tot = _sum_rows(dm_all[:, B:B + 1].reshape(N_DEV, 1, n_mod)[:, :, :].reshape(N_DEV, n_mod // LANES, LANES),
                        "sum_dmod_c").reshape(1, n_mod)
    dmod_rows = jnp.concatenate([dm_all[:, :B].reshape(N_DEV * B, n_mod), dmc_tot,
                                 jnp.zeros((ADA_ROWS - N_DEV * B - 1, n_mod), F32)], axis=0)
    dmod_mine = lax.dynamic_slice_in_dim(dmod_rows.reshape(ADA_ROWS, 2, N_DEV, ada_cols), me, 1, axis=2)
    dmod_mine = dmod_mine.reshape(ADA_ROWS, 2 * ada_cols).astype(MXU_DTYPE)
    g_ada2 = _mm(ada_in, dmod_mine, "tn", F32, "ada_dw")
    g_ada_w = jnp.stack([g_ada2[:, :ada_cols], g_ada2[:, ada_cols:]])
    dmc_mine = jnp.concatenate([dmod_mine[N_DEV * B:N_DEV * B + 1], jnp.zeros((7, 2 * ada_cols), MXU_DTYPE)], axis=0)
    dcc_part = _mm(dmc_mine, ada_w2, "nt", F32, "ada_dc")[0:1]
    g_ada_b = _sum_rows(dmod_rows[:, None, :].reshape(ADA_ROWS, n_mod // LANES, LANES), "sum_dmod_b").reshape(2, 6 * D_MODEL)
    sg = jax.nn.sigmoid(c_ctx)
    g_small["c_ctx"] = dcc_part.reshape(D_MODEL) * (sg * (1.0 + c_ctx * (1.0 - sg)))
    g_small["ada_b"] = g_ada_b * (1.0 / N_DEV)

    shapes = {k: weights[k].shape for k in SMALL_NAMES}
    n_small = sum(math.prod(s) for s in shapes.values())
    srows = -(-(-(-n_small // LANES)) // 8) * 8
    gs_all = _all_gather(_pack_small(g_small, srows), "gather_small_grads")
    sm = _adamw(_pack_small({k: weights[k] for k in SMALL_NAMES}, srows), _pack_small({k: mom1[k] for k in SMALL_NAMES}, srows),
                _pack_small({k: mom2[k] for k in SMALL_NAMES}, srows), [gs_all], "adamw_small")
    sm = [_unpack_small(t, shapes) for t in sm]

    ada_shape = ada_w.shape
    r2 = lambda t: t.reshape(ada_shape[0] * ada_shape[1], ada_shape[2])
    ada = [t.reshape(ada_shape) for t in _adamw(r2(ada_w), r2(m_ada_w), r2(v_ada_w), [r2(g_ada_w)[None]], "adamw_ada")]

    attn_grads = {k: g_layer0[k] for k in EARLY_WEIGHTS}
    big = _adamw_big(weights, mom1, mom2, [_reduce_scatter_in_call(attn_grads, "rs_"), parts0_ffn, parts1])

    def pick(i, name):
        if name in BIG_WEIGHTS:
            return big[name][i]
        if name == "ada_w":
            return ada[i]
        return sm[i][name]

    order = ("c_ctx", "ada_w", "ada_b", "norm1_g", "norm2_g", "ffn_w_in", "ffn_w_out", "attn_w_qkv", "attn_q_norm",
             "attn_k_norm", "attn_sink", "attn_w_o", "ret_w_qkvg", "ret_decay_logit", "ret_gn_g", "ret_w_o")
    outs = [loss, grad_x]
    for i in range(4):
        outs += [pick(i, n) for n in order]
    return tuple(outs)
```
